```python
import math
import jax, jax.numpy as jnp
from jax import lax
import numpy as np

D_MODEL = 1024
BATCH = 16
SEQ = 2048
DEPTH = 2

N_EVEN = (DEPTH + 1) // 2
N_ODD = DEPTH // 2
EPS = 1e-6
D_FF = 2816
FFN_RES = 0.5
CONV_W = 4

SSD_D_INNER = D_MODEL
SSD_HEAD_DIM = 64
SSD_HEADS = SSD_D_INNER // SSD_HEAD_DIM
SSD_GROUPS = 2
SSD_D_STATE = 128
SSD_BC = SSD_GROUPS * SSD_D_STATE
SSD_CONV_CH = SSD_D_INNER + 2 * SSD_BC
SSD_CHUNK = 128

ML_D_INNER = D_MODEL
ML_HEADS = 4
ML_HEAD_DIM = ML_D_INNER // ML_HEADS
ML_QKV_BLOCK = 4
ML_QKV_BLOCKS = ML_D_INNER // ML_QKV_BLOCK
ML_CHUNK = 128

IN_COLS = SSD_D_INNER + SSD_CONV_CH + SSD_HEADS + 2 * ML_D_INNER
MIX_WIDTH = SSD_D_INNER + ML_D_INNER

S5_GROUP = 16
S5_GROUPS = D_MODEL // S5_GROUP
S5_STATE = 64

kernel_name = 'hybrid_ssd_mlstm_s5_macaron'


def _rmsnorm(x, w):
    xf = x.astype(jnp.float32)
    y = xf * lax.rsqrt(jnp.mean(xf * xf, axis=-1, keepdims=True) + EPS)
    return (y * w.astype(jnp.float32)).astype(x.dtype)


def _swiglu(x, w_gate, w_up, w_down):
    return (jax.nn.silu(x @ w_gate) * (x @ w_up)) @ w_down


def _causal_dwconv(x, w, b):
    k_w, s = w.shape[0], x.shape[1]
    xp = jnp.pad(x, ((0, 0), (k_w - 1, 0), (0, 0)))
    y = b
    for j in range(k_w):
        y = y + xp[:, j:j + s] * w[j]
    return y


def _segsum(x):
    t = x.shape[-1]
    cs = jnp.cumsum(x, axis=-1)
    d = cs[..., :, None] - cs[..., None, :]
    return jnp.where(jnp.tril(jnp.ones((t, t), dtype=bool)), d, -jnp.inf)


def _headwise(x, w):
    nb, o, i = w.shape
    y = jnp.einsum('bsni,noi->bsno', x.reshape(x.shape[:-1] + (nb, i)), w)
    return y.reshape(x.shape[:-1] + (nb * o,))


def _ssd_chunked(x, dt, a, bm, cm):
    bsz, s, h, p = x.shape
    g, n = bm.shape[2], bm.shape[3]
    r, l = h // g, SSD_CHUNK
    c = s // l
    xd = (x * dt[..., None]).reshape(bsz, c, l, g, r, p)
    adt = (dt * a).reshape(bsz, c, l, g, r).transpose(0, 1, 3, 4, 2)
    bc = bm.reshape(bsz, c, l, g, n)
    cc = cm.reshape(bsz, c, l, g, n)
    a_cs = jnp.cumsum(adt, axis=-1)
    lmat = jnp.exp(_segsum(adt))
    cb = jnp.einsum('bclgn,bcsgn->bcgls', cc, bc)
    y_diag = jnp.einsum('bcgls,bcgrls,bcsgrp->bclgrp', cb, lmat, xd)
    decay_states = jnp.exp(a_cs[..., -1:] - a_cs)
    states = jnp.einsum('bclgn,bcgrl,bclgrp->bcgrpn', bc, decay_states, xd)
    states = jnp.concatenate([jnp.zeros_like(states[:, :1]), states], axis=1)
    chunk_tot = jnp.pad(a_cs[..., -1].transpose(0, 2, 3, 1), ((0, 0), (0, 0), (0, 0), (1, 0)))
    decay_chunk = jnp.exp(_segsum(chunk_tot))
    states = jnp.einsum('bgrzc,bcgrpn->bzgrpn', decay_chunk, states)[:, :-1]
    y_off = jnp.einsum('bclgn,bcgrpn,bcgrl->bclgrp', cc, states, jnp.exp(a_cs))
    return (y_diag + y_off).reshape(bsz, s, h, p)


def _mlstm_chunkwise(q, k, v, i_pre, f_pre):
    bsz, s, h, d = q.shape
    l = ML_CHUNK
    c = s // l
    q = q.astype(jnp.float32)
    k = k.astype(jnp.float32) / math.sqrt(d)
    v = v.astype(jnp.float32)
    to_chunks = lambda t: t.reshape(bsz, c, l, h, -1).transpose(0, 1, 3, 2, 4)
    qc, kc, vc = to_chunks(q), to_chunks(k), to_chunks(v)
    logf = jax.nn.log_sigmoid(f_pre).reshape(bsz, c, l, h).transpose(0, 1, 3, 2)
    ig = i_pre.reshape(bsz, c, l, h).transpose(0, 1, 3, 2)
    bcum = jnp.cumsum(logf, axis=-1)
    dlog = bcum[..., :, None] - bcum[..., None, :] + ig[..., None, :]
    dlog = jnp.where(jnp.tril(jnp.ones((l, l), dtype=bool)), dlog, -jnp.inf)
    b_last = bcum[..., -1]
    w_state = b_last[..., None] - bcum + ig

    def step(carry, inp):
        c_prev, n_prev, m_prev = carry
        kk, vv, bl, ws = inp
        m_new = jnp.maximum(bl + m_prev, jnp.max(ws, axis=-1))
        decay = jnp.exp(bl + m_prev - m_new)
        wts = jnp.exp(ws - m_new[..., None])
        c_new = decay[..., None, None] * c_prev + jnp.einsum('bhl,bhlv,bhlk->bhvk', wts, vv, kk)
        n_new = decay[..., None] * n_prev + jnp.einsum('bhl,bhlk->bhk', wts, kk)
        return (c_new, n_new, m_new), (c_prev, n_prev, m_prev)

    init = (jnp.zeros((bsz, h, d, d), jnp.float32), jnp.zeros((bsz, h, d), jnp.float32),
            jnp.full((bsz, h), -1e30, jnp.float32))
    xs = (jnp.moveaxis(kc, 1, 0), jnp.moveaxis(vc, 1, 0), jnp.moveaxis(b_last, 1, 0), jnp.moveaxis(w_state, 1, 0))
    _, (c_all, n_all, m_all) = lax.scan(step, init, xs)
    c_all, n_all, m_all = jnp.moveaxis(c_all, 0, 1), jnp.moveaxis(n_all, 0, 1), jnp.moveaxis(m_all, 0, 1)

    m_inter = bcum + m_all[..., None]
    m_t = jnp.maximum(jnp.max(dlog, axis=-1), m_inter)
    scores = jnp.einsum('bchld,bchsd->bchls', qc, kc) * jnp.exp(dlog - m_t[..., None])
    inter_w = jnp.exp(m_inter - m_t)
    num = jnp.einsum('bchls,bchsv->bchlv', scores, vc) + inter_w[..., None] * jnp.einsum('bchlk,bchvk->bchlv', qc, c_all)
    den = jnp.sum(scores, axis=-1) + inter_w * jnp.einsum('bchlk,bchk->bchl', qc, n_all)
    hout = num / jnp.maximum(jnp.abs(den), jnp.exp(-m_t))[..., None]
    return hout.transpose(0, 1, 3, 2, 4).reshape(bsz, s, h, d)


def _hybrid_mixer(u, w_in, ssd_conv_w, ssd_conv_b, ssd_dt_bias, ssd_a_log, ssd_d, ssd_norm_w,
                  ml_conv_w, ml_conv_b, ml_w_q, ml_w_k, ml_w_v, ml_w_if, ml_b_if, ml_norm_w, ml_skip, w_out):
    bsz, s, _ = u.shape
    f32 = jnp.float32
    proj = u @ w_in
    o1 = SSD_D_INNER
    o2 = o1 + SSD_CONV_CH
    o3 = o2 + SSD_HEADS
    o4 = o3 + ML_D_INNER
    z_s, xbc, dt_raw, m_x, m_z = proj[..., :o1], proj[..., o1:o2], proj[..., o2:o3], proj[..., o3:o4], proj[..., o4:]
    xbc = jax.nn.silu(_causal_dwconv(xbc, ssd_conv_w, ssd_conv_b))
    xs = xbc[..., :SSD_D_INNER].reshape(bsz, s, SSD_HEADS, SSD_HEAD_DIM).astype(f32)
    bm = xbc[..., SSD_D_INNER:SSD_D_INNER + SSD_BC].reshape(bsz, s, SSD_GROUPS, SSD_D_STATE).astype(f32)
    cm = xbc[..., SSD_D_INNER + SSD_BC:].reshape(bsz, s, SSD_GROUPS, SSD_D_STATE).astype(f32)
    dt = jax.nn.softplus(dt_raw.astype(f32) + ssd_dt_bias.astype(f32))
    a = -jnp.exp(ssd_a_log.astype(f32))
    y = _ssd_chunked(xs, dt, a, bm, cm) + ssd_d.astype(f32)[:, None] * xs
    yg = (y.reshape(bsz, s, SSD_D_INNER) * jax.nn.silu(z_s.astype(f32))).reshape(bsz, s, SSD_GROUPS, -1)
    yg = yg * lax.rsqrt(jnp.mean(yg * yg, axis=-1, keepdims=True) + EPS)
    y_ssd = yg.reshape(bsz, s, SSD_D_INNER) * ssd_norm_w.astype(f32)
    xc = jax.nn.silu(_causal_dwconv(m_x, ml_conv_w, ml_conv_b))
    q = _headwise(xc, ml_w_q)
    k = _headwise(xc, ml_w_k)
    v = _headwise(m_x, ml_w_v)
    gates = (jnp.concatenate([q, k, v], axis=-1) @ ml_w_if + ml_b_if).astype(f32)
    hs = (bsz, s, ML_HEADS, ML_HEAD_DIM)
    hm = _mlstm_chunkwise(q.reshape(hs), k.reshape(hs), v.reshape(hs), gates[..., :ML_HEADS], gates[..., ML_HEADS:])
    mu = jnp.mean(hm, axis=-1, keepdims=True)
    var = jnp.mean(jnp.square(hm - mu), axis=-1, keepdims=True)
    hm = ((hm - mu) * lax.rsqrt(var + EPS)).reshape(bsz, s, ML_D_INNER) * ml_norm_w.astype(f32)
    y_ml = (hm + ml_skip.astype(f32) * xc.astype(f32)) * jax.nn.silu(m_z.astype(f32))
    y = jnp.concatenate([y_ssd, y_ml], axis=-1).astype(u.dtype)
    return y @ w_out


def _s5_mixer(u, a_re, a_im, log_step, b_re, b_im, c_re, c_im, d_skip, w_a, b_a, w_b, b_b):
    bsz, s, _ = u.shape
    f32 = jnp.float32
    a_re, a_im = a_re.astype(f32), a_im.astype(f32)
    b_re, b_im, c_re, c_im = b_re.astype(f32), b_im.astype(f32), c_re.astype(f32), c_im.astype(f32)
    step = jnp.exp(log_step.astype(f32))[:, None]
    mag = jnp.exp(a_re * step)
    lam_re, lam_im = mag * jnp.cos(a_im * step), mag * jnp.sin(a_im * step)
    den = a_re * a_re + a_im * a_im
    coef_re = ((lam_re - 1.0) * a_re + lam_im * a_im) / den
    coef_im = (lam_im * a_re - (lam_re - 1.0) * a_im) / den
    bb_re = coef_re[..., None] * b_re - coef_im[..., None] * b_im
    bb_im = coef_re[..., None] * b_im + coef_im[..., None] * b_re
    ug = u.astype(f32).reshape(bsz, s, S5_GROUPS, S5_GROUP)
    bu_re = jnp.einsum('bsgc,gnc->bsgn', ug, bb_re)
    bu_im = jnp.einsum('bsgc,gnc->bsgn', ug, bb_im)
    a_seq_re = jnp.broadcast_to(lam_re, (s,) + lam_re.shape)
    a_seq_im = jnp.broadcast_to(lam_im, (s,) + lam_im.shape)

    def combine(e1, e2):
        a1r, a1i, b1r, b1i = e1
        a2r, a2i, b2r, b2i = e2
        return (a2r * a1r - a2i * a1i, a2r * a1i + a2i * a1r,
                a2r * b1r - a2i * b1i + b2r, a2r * b1i + a2i * b1r + b2i)

    def scan_one(br, bi):
        _, _, xr, xi = lax.associative_scan(combine, (a_seq_re, a_seq_im, br, bi), axis=0)
        return xr, xi

    x_re, x_im = jax.vmap(scan_one)(bu_re, bu_im)
    y = jnp.einsum('bsgn,gcn->bsgc', x_re, c_re) - jnp.einsum('bsgn,gcn->bsgc', x_im, c_im)
    y = y.reshape(bsz, s, D_MODEL) + d_skip.astype(f32) * u.astype(f32)
    g = jax.nn.gelu(y).astype(u.dtype)
    return (g @ w_a + b_a) * jax.nn.sigmoid(g @ w_b + b_b)


def _fwd_setup_inputs(seed: int = 0) -> dict:
    key = jax.random.key(seed)
    keys = iter(jax.random.split(key, 64))
    nk = lambda: next(keys)
    nrm = lambda shape, scale: jax.random.normal(nk(), shape, jnp.float32) * scale
    gain = lambda shape: 1.0 + nrm(shape, 0.02)
    uni = lambda shape, lo, hi: jax.random.uniform(nk(), shape, jnp.float32, lo, hi)
    x = nrm((BATCH, SEQ, D_MODEL), 1.0)
    ssd_dt = jnp.exp(uni((N_EVEN, SSD_HEADS), math.log(1e-3), math.log(1e-1)))
    f_bias = jnp.broadcast_to(jnp.linspace(3.0, 6.0, ML_HEADS, dtype=jnp.float32), (N_EVEN, ML_HEADS)) + nrm((N_EVEN, ML_HEADS), 0.01)
    i_bias = nrm((N_EVEN, ML_HEADS), 0.1)
    s5_a_im = jnp.broadcast_to(math.pi * jnp.arange(S5_STATE, dtype=jnp.float32), (N_ODD, S5_GROUPS, S5_STATE))
    return {
        'x': x,
        'ffn1_norm': gain((DEPTH, D_MODEL)),
        'ffn1_w_gate': nrm((DEPTH, D_MODEL, D_FF), D_MODEL ** -0.5),
        'ffn1_w_up': nrm((DEPTH, D_MODEL, D_FF), D_MODEL ** -0.5),
        'ffn1_w_down': nrm((DEPTH, D_FF, D_MODEL), D_FF ** -0.5),
        'mix_norm': gain((DEPTH, D_MODEL)),
        'ffn2_norm': gain((DEPTH, D_MODEL)),
        'ffn2_w_gate': nrm((DEPTH, D_MODEL, D_FF), D_MODEL ** -0.5),
        'ffn2_w_up': nrm((DEPTH, D_MODEL, D_FF), D_MODEL ** -0.5),
        'ffn2_w_down': nrm((DEPTH, D_FF, D_MODEL), D_FF ** -0.5),
        'hy_w_in': nrm((N_EVEN, D_MODEL, IN_COLS), D_MODEL ** -0.5),
        'ssd_conv_w': nrm((N_EVEN, CONV_W, SSD_CONV_CH), CONV_W ** -0.5),
        'ssd_conv_b': nrm((N_EVEN, SSD_CONV_CH), 0.02),
        'ssd_dt_bias': ssd_dt + jnp.log(-jnp.expm1(-ssd_dt)),
        'ssd_a_log': jnp.log(uni((N_EVEN, SSD_HEADS), 1.0, 16.0)),
        'ssd_d': 1.0 + nrm((N_EVEN, SSD_HEADS), 0.1),
        'ssd_norm_w': gain((N_EVEN, SSD_D_INNER)),
        'ml_conv_w': nrm((N_EVEN, CONV_W, ML_D_INNER), CONV_W ** -0.5),
        'ml_conv_b': nrm((N_EVEN, ML_D_INNER), 0.02),
        'ml_w_q': nrm((N_EVEN, ML_QKV_BLOCKS, ML_QKV_BLOCK, ML_QKV_BLOCK), ML_QKV_BLOCK ** -0.5),
        'ml_w_k': nrm((N_EVEN, ML_QKV_BLOCKS, ML_QKV_BLOCK, ML_QKV_BLOCK), ML_QKV_BLOCK ** -0.5),
        'ml_w_v': nrm((N_EVEN, ML_QKV_BLOCKS, ML_QKV_BLOCK, ML_QKV_BLOCK), ML_QKV_BLOCK ** -0.5),
        'ml_w_if': nrm((N_EVEN, 3 * ML_D_INNER, 2 * ML_HEADS), 0.5 * (3 * ML_D_INNER) ** -0.5),
        'ml_b_if': jnp.concatenate([i_bias, f_bias], axis=-1),
        'ml_norm_w': gain((N_EVEN, ML_D_INNER)),
        'ml_skip': gain((N_EVEN, ML_D_INNER)),
        'hy_w_out': nrm((N_EVEN, MIX_WIDTH, D_MODEL), MIX_WIDTH ** -0.5),
        's5_a_re': -0.5 + nrm((N_ODD, S5_GROUPS, S5_STATE), 0.01),
        's5_a_im': s5_a_im + nrm((N_ODD, S5_GROUPS, S5_STATE), 0.01),
        's5_log_step': uni((N_ODD, S5_GROUPS), math.log(1e-3), math.log(1e-1)),
        's5_b_re': nrm((N_ODD, S5_GROUPS, S5_STATE, S5_GROUP), (2.0 * S5_GROUP) ** -0.5),
        's5_b_im': nrm((N_ODD, S5_GROUPS, S5_STATE, S5_GROUP), (2.0 * S5_GROUP) ** -0.5),
        's5_c_re': nrm((N_ODD, S5_GROUPS, S5_GROUP, S5_STATE), (2.0 * S5_STATE) ** -0.5),
        's5_c_im': nrm((N_ODD, S5_GROUPS, S5_GROUP, S5_STATE), (2.0 * S5_STATE) ** -0.5),
        's5_d': nrm((N_ODD, D_MODEL), 1.0),
        's5_w_a': nrm((N_ODD, D_MODEL, D_MODEL), D_MODEL ** -0.5),
        's5_b_a': nrm((N_ODD, D_MODEL), 0.02),
        's5_w_b': nrm((N_ODD, D_MODEL, D_MODEL), D_MODEL ** -0.5),
        's5_b_b': nrm((N_ODD, D_MODEL), 0.02),
        'final_norm': gain((D_MODEL,)),
    }


def _fwd_reference(x, ffn1_norm, ffn1_w_gate, ffn1_w_up, ffn1_w_down, mix_norm, ffn2_norm, ffn2_w_gate, ffn2_w_up, ffn2_w_down,
              hy_w_in, ssd_conv_w, ssd_conv_b, ssd_dt_bias, ssd_a_log, ssd_d, ssd_norm_w,
              ml_conv_w, ml_conv_b, ml_w_q, ml_w_k, ml_w_v, ml_w_if, ml_b_if, ml_norm_w, ml_skip, hy_w_out,
              s5_a_re, s5_a_im, s5_log_step, s5_b_re, s5_b_im, s5_c_re, s5_c_im, s5_d, s5_w_a, s5_b_a, s5_w_b, s5_b_b,
              final_norm):
    for layer in range(DEPTH):
        h = _rmsnorm(x, ffn1_norm[layer])
        x = x + FFN_RES * _swiglu(h, ffn1_w_gate[layer], ffn1_w_up[layer], ffn1_w_down[layer])
        u = _rmsnorm(x, mix_norm[layer])
        if layer % 2 == 0:
            e = layer // 2
            y = _hybrid_mixer(u, hy_w_in[e], ssd_conv_w[e], ssd_conv_b[e], ssd_dt_bias[e], ssd_a_log[e], ssd_d[e], ssd_norm_w[e],
                              ml_conv_w[e], ml_conv_b[e], ml_w_q[e], ml_w_k[e], ml_w_v[e], ml_w_if[e], ml_b_if[e],
                              ml_norm_w[e], ml_skip[e], hy_w_out[e])
        else:
            o = layer // 2
            y = _s5_mixer(u, s5_a_re[o], s5_a_im[o], s5_log_step[o], s5_b_re[o], s5_b_im[o], s5_c_re[o], s5_c_im[o],
                          s5_d[o], s5_w_a[o], s5_b_a[o], s5_w_b[o], s5_b_b[o])
        x = x + y.astype(x.dtype)
        h = _rmsnorm(x, ffn2_norm[layer])
        x = x + FFN_RES * _swiglu(h, ffn2_w_gate[layer], ffn2_w_up[layer], ffn2_w_down[layer])
    return _rmsnorm(x, final_norm)


import jax as _jax
import jax.numpy as _jnp

TWIN_FORMAT = 'train_step'
FWD_PARAMS = ['x', 'ffn1_norm', 'ffn1_w_gate', 'ffn1_w_up', 'ffn1_w_down', 'mix_norm', 'ffn2_norm', 'ffn2_w_gate', 'ffn2_w_up', 'ffn2_w_down', 'hy_w_in', 'ssd_conv_w', 'ssd_conv_b', 'ssd_dt_bias', 'ssd_a_log', 'ssd_d', 'ssd_norm_w', 'ml_conv_w', 'ml_conv_b', 'ml_w_q', 'ml_w_k', 'ml_w_v', 'ml_w_if', 'ml_b_if', 'ml_norm_w', 'ml_skip', 'hy_w_out', 's5_a_re', 's5_a_im', 's5_log_step', 's5_b_re', 's5_b_im', 's5_c_re', 's5_c_im', 's5_d', 's5_w_a', 's5_b_a', 's5_w_b', 's5_b_b', 'final_norm']
TWIN_WEIGHTS = ['ffn1_norm', 'ffn1_w_gate', 'ffn1_w_up', 'ffn1_w_down', 'mix_norm', 'ffn2_norm', 'ffn2_w_gate', 'ffn2_w_up', 'ffn2_w_down', 'hy_w_in', 'ssd_conv_w', 'ssd_conv_b', 'ssd_dt_bias', 'ssd_a_log', 'ssd_d', 'ssd_norm_w', 'ml_conv_w', 'ml_conv_b', 'ml_w_q', 'ml_w_k', 'ml_w_v', 'ml_w_if', 'ml_b_if', 'ml_norm_w', 'ml_skip', 'hy_w_out', 's5_a_re', 's5_a_im', 's5_log_step', 's5_b_re', 's5_b_im', 's5_c_re', 's5_c_im', 's5_d', 's5_w_a', 's5_b_a', 's5_w_b', 's5_b_b', 'final_norm']
TWIN_DIFF_INPUT = 'x'
TWIN_INPUTS = ['x', 'ffn1_norm', 'ffn1_w_gate', 'ffn1_w_up', 'ffn1_w_down', 'mix_norm', 'ffn2_norm', 'ffn2_w_gate', 'ffn2_w_up', 'ffn2_w_down', 'hy_w_in', 'ssd_conv_w', 'ssd_conv_b', 'ssd_dt_bias', 'ssd_a_log', 'ssd_d', 'ssd_norm_w', 'ml_conv_w', 'ml_conv_b', 'ml_w_q', 'ml_w_k', 'ml_w_v', 'ml_w_if', 'ml_b_if', 'ml_norm_w', 'ml_skip', 'hy_w_out', 's5_a_re', 's5_a_im', 's5_log_step', 's5_b_re', 's5_b_im', 's5_c_re', 's5_c_im', 's5_d', 's5_w_a', 's5_b_a', 's5_w_b', 's5_b_b', 'final_norm', 'loss_target', 'm_ffn1_norm', 'm_ffn1_w_gate', 'm_ffn1_w_up', 'm_ffn1_w_down', 'm_mix_norm', 'm_ffn2_norm', 'm_ffn2_w_gate', 'm_ffn2_w_up', 'm_ffn2_w_down', 'm_hy_w_in', 'm_ssd_conv_w', 'm_ssd_conv_b', 'm_ssd_dt_bias', 'm_ssd_a_log', 'm_ssd_d', 'm_ssd_norm_w', 'm_ml_conv_w', 'm_ml_conv_b', 'm_ml_w_q', 'm_ml_w_k', 'm_ml_w_v', 'm_ml_w_if', 'm_ml_b_if', 'm_ml_norm_w', 'm_ml_skip', 'm_hy_w_out', 'm_s5_a_re', 'm_s5_a_im', 'm_s5_log_step', 'm_s5_b_re', 'm_s5_b_im', 'm_s5_c_re', 'm_s5_c_im', 'm_s5_d', 'm_s5_w_a', 'm_s5_b_a', 'm_s5_w_b', 'm_s5_b_b', 'm_final_norm', 'v_ffn1_norm', 'v_ffn1_w_gate', 'v_ffn1_w_up', 'v_ffn1_w_down', 'v_mix_norm', 'v_ffn2_norm', 'v_ffn2_w_gate', 'v_ffn2_w_up', 'v_ffn2_w_down', 'v_hy_w_in', 'v_ssd_conv_w', 'v_ssd_conv_b', 'v_ssd_dt_bias', 'v_ssd_a_log', 'v_ssd_d', 'v_ssd_norm_w', 'v_ml_conv_w', 'v_ml_conv_b', 'v_ml_w_q', 'v_ml_w_k', 'v_ml_w_v', 'v_ml_w_if', 'v_ml_b_if', 'v_ml_norm_w', 'v_ml_skip', 'v_hy_w_out', 'v_s5_a_re', 'v_s5_a_im', 'v_s5_log_step', 'v_s5_b_re', 'v_s5_b_im', 'v_s5_c_re', 'v_s5_c_im', 'v_s5_d', 'v_s5_w_a', 'v_s5_b_a', 'v_s5_w_b', 'v_s5_b_b', 'v_final_norm']
TWIN_OUTPUTS = ['loss', 'grad_x', 'grad_ffn1_norm', 'grad_ffn1_w_gate', 'grad_ffn1_w_up', 'grad_ffn1_w_down', 'grad_mix_norm', 'grad_ffn2_norm', 'grad_ffn2_w_gate', 'grad_ffn2_w_up', 'grad_ffn2_w_down', 'grad_hy_w_in', 'grad_ssd_conv_w', 'grad_ssd_conv_b', 'grad_ssd_dt_bias', 'grad_ssd_a_log', 'grad_ssd_d', 'grad_ssd_norm_w', 'grad_ml_conv_w', 'grad_ml_conv_b', 'grad_ml_w_q', 'grad_ml_w_k', 'grad_ml_w_v', 'grad_ml_w_if', 'grad_ml_b_if', 'grad_ml_norm_w', 'grad_ml_skip', 'grad_hy_w_out', 'grad_s5_a_re', 'grad_s5_a_im', 'grad_s5_log_step', 'grad_s5_b_re', 'grad_s5_b_im', 'grad_s5_c_re', 'grad_s5_c_im', 'grad_s5_d', 'grad_s5_w_a', 'grad_s5_b_a', 'grad_s5_w_b', 'grad_s5_b_b', 'grad_final_norm', 'delta_ffn1_norm', 'delta_ffn1_w_gate', 'delta_ffn1_w_up', 'delta_ffn1_w_down', 'delta_mix_norm', 'delta_ffn2_norm', 'delta_ffn2_w_gate', 'delta_ffn2_w_up', 'delta_ffn2_w_down', 'delta_hy_w_in', 'delta_ssd_conv_w', 'delta_ssd_conv_b', 'delta_ssd_dt_bias', 'delta_ssd_a_log', 'delta_ssd_d', 'delta_ssd_norm_w', 'delta_ml_conv_w', 'delta_ml_conv_b', 'delta_ml_w_q', 'delta_ml_w_k', 'delta_ml_w_v', 'delta_ml_w_if', 'delta_ml_b_if', 'delta_ml_norm_w', 'delta_ml_skip', 'delta_hy_w_out', 'delta_s5_a_re', 'delta_s5_a_im', 'delta_s5_log_step', 'delta_s5_b_re', 'delta_s5_b_im', 'delta_s5_c_re', 'delta_s5_c_im', 'delta_s5_d', 'delta_s5_w_a', 'delta_s5_b_a', 'delta_s5_w_b', 'delta_s5_b_b', 'delta_final_norm', 'new_m_ffn1_norm', 'new_m_ffn1_w_gate', 'new_m_ffn1_w_up', 'new_m_ffn1_w_down', 'new_m_mix_norm', 'new_m_ffn2_norm', 'new_m_ffn2_w_gate', 'new_m_ffn2_w_up', 'new_m_ffn2_w_down', 'new_m_hy_w_in', 'new_m_ssd_conv_w', 'new_m_ssd_conv_b', 'new_m_ssd_dt_bias', 'new_m_ssd_a_log', 'new_m_ssd_d', 'new_m_ssd_norm_w', 'new_m_ml_conv_w', 'new_m_ml_conv_b', 'new_m_ml_w_q', 'new_m_ml_w_k', 'new_m_ml_w_v', 'new_m_ml_w_if', 'new_m_ml_b_if', 'new_m_ml_norm_w', 'new_m_ml_skip', 'new_m_hy_w_out', 'new_m_s5_a_re', 'new_m_s5_a_im', 'new_m_s5_log_step', 'new_m_s5_b_re', 'new_m_s5_b_im', 'new_m_s5_c_re', 'new_m_s5_c_im', 'new_m_s5_d', 'new_m_s5_w_a', 'new_m_s5_b_a', 'new_m_s5_w_b', 'new_m_s5_b_b', 'new_m_final_norm', 'new_v_ffn1_norm', 'new_v_ffn1_w_gate', 'new_v_ffn1_w_up', 'new_v_ffn1_w_down', 'new_v_mix_norm', 'new_v_ffn2_norm', 'new_v_ffn2_w_gate', 'new_v_ffn2_w_up', 'new_v_ffn2_w_down', 'new_v_hy_w_in', 'new_v_ssd_conv_w', 'new_v_ssd_conv_b', 'new_v_ssd_dt_bias', 'new_v_ssd_a_log', 'new_v_ssd_d', 'new_v_ssd_norm_w', 'new_v_ml_conv_w', 'new_v_ml_conv_b', 'new_v_ml_w_q', 'new_v_ml_w_k', 'new_v_ml_w_v', 'new_v_ml_w_if', 'new_v_ml_b_if', 'new_v_ml_norm_w', 'new_v_ml_skip', 'new_v_hy_w_out', 'new_v_s5_a_re', 'new_v_s5_a_im', 'new_v_s5_log_step', 'new_v_s5_b_re', 'new_v_s5_b_im', 'new_v_s5_c_re', 'new_v_s5_c_im', 'new_v_s5_d', 'new_v_s5_w_a', 'new_v_s5_b_a', 'new_v_s5_w_b', 'new_v_s5_b_b', 'new_v_final_norm']
TWIN_LEAF_KINDS = {'loss': 'loss', 'grad_x': 'grad_x', 'grad_ffn1_norm': 'grad_w', 'grad_ffn1_w_gate': 'grad_w', 'grad_ffn1_w_up': 'grad_w', 'grad_ffn1_w_down': 'grad_w', 'grad_mix_norm': 'grad_w', 'grad_ffn2_norm': 'grad_w', 'grad_ffn2_w_gate': 'grad_w', 'grad_ffn2_w_up': 'grad_w', 'grad_ffn2_w_down': 'grad_w', 'grad_hy_w_in': 'grad_w', 'grad_ssd_conv_w': 'grad_w', 'grad_ssd_conv_b': 'grad_w', 'grad_ssd_dt_bias': 'grad_w', 'grad_ssd_a_log': 'grad_w', 'grad_ssd_d': 'grad_w', 'grad_ssd_norm_w': 'grad_w', 'grad_ml_conv_w': 'grad_w', 'grad_ml_conv_b': 'grad_w', 'grad_ml_w_q': 'grad_w', 'grad_ml_w_k': 'grad_w', 'grad_ml_w_v': 'grad_w', 'grad_ml_w_if': 'grad_w', 'grad_ml_b_if': 'grad_w', 'grad_ml_norm_w': 'grad_w', 'grad_ml_skip': 'grad_w', 'grad_hy_w_out': 'grad_w', 'grad_s5_a_re': 'grad_w', 'grad_s5_a_im': 'grad_w', 'grad_s5_log_step': 'grad_w', 'grad_s5_b_re': 'grad_w', 'grad_s5_b_im': 'grad_w', 'grad_s5_c_re': 'grad_w', 'grad_s5_c_im': 'grad_w', 'grad_s5_d': 'grad_w', 'grad_s5_w_a': 'grad_w', 'grad_s5_b_a': 'grad_w', 'grad_s5_w_b': 'grad_w', 'grad_s5_b_b': 'grad_w', 'grad_final_norm': 'grad_w', 'delta_ffn1_norm': 'delta_w', 'delta_ffn1_w_gate': 'delta_w', 'delta_ffn1_w_up': 'delta_w', 'delta_ffn1_w_down': 'delta_w', 'delta_mix_norm': 'delta_w', 'delta_ffn2_norm': 'delta_w', 'delta_ffn2_w_gate': 'delta_w', 'delta_ffn2_w_up': 'delta_w', 'delta_ffn2_w_down': 'delta_w', 'delta_hy_w_in': 'delta_w', 'delta_ssd_conv_w': 'delta_w', 'delta_ssd_conv_b': 'delta_w', 'delta_ssd_dt_bias': 'delta_w', 'delta_ssd_a_log': 'delta_w', 'delta_ssd_d': 'delta_w', 'delta_ssd_norm_w': 'delta_w', 'delta_ml_conv_w': 'delta_w', 'delta_ml_conv_b': 'delta_w', 'delta_ml_w_q': 'delta_w', 'delta_ml_w_k': 'delta_w', 'delta_ml_w_v': 'delta_w', 'delta_ml_w_if': 'delta_w', 'delta_ml_b_if': 'delta_w', 'delta_ml_norm_w': 'delta_w', 'delta_ml_skip': 'delta_w', 'delta_hy_w_out': 'delta_w', 'delta_s5_a_re': 'delta_w', 'delta_s5_a_im': 'delta_w', 'delta_s5_log_step': 'delta_w', 'delta_s5_b_re': 'delta_w', 'delta_s5_b_im': 'delta_w', 'delta_s5_c_re': 'delta_w', 'delta_s5_c_im': 'delta_w', 'delta_s5_d': 'delta_w', 'delta_s5_w_a': 'delta_w', 'delta_s5_b_a': 'delta_w', 'delta_s5_w_b': 'delta_w', 'delta_s5_b_b': 'delta_w', 'delta_final_norm': 'delta_w', 'new_m_ffn1_norm': 'new_m', 'new_m_ffn1_w_gate': 'new_m', 'new_m_ffn1_w_up': 'new_m', 'new_m_ffn1_w_down': 'new_m', 'new_m_mix_norm': 'new_m', 'new_m_ffn2_norm': 'new_m', 'new_m_ffn2_w_gate': 'new_m', 'new_m_ffn2_w_up': 'new_m', 'new_m_ffn2_w_down': 'new_m', 'new_m_hy_w_in': 'new_m', 'new_m_ssd_conv_w': 'new_m', 'new_m_ssd_conv_b': 'new_m', 'new_m_ssd_dt_bias': 'new_m', 'new_m_ssd_a_log': 'new_m', 'new_m_ssd_d': 'new_m', 'new_m_ssd_norm_w': 'new_m', 'new_m_ml_conv_w': 'new_m', 'new_m_ml_conv_b': 'new_m', 'new_m_ml_w_q': 'new_m', 'new_m_ml_w_k': 'new_m', 'new_m_ml_w_v': 'new_m', 'new_m_ml_w_if': 'new_m', 'new_m_ml_b_if': 'new_m', 'new_m_ml_norm_w': 'new_m', 'new_m_ml_skip': 'new_m', 'new_m_hy_w_out': 'new_m', 'new_m_s5_a_re': 'new_m', 'new_m_s5_a_im': 'new_m', 'new_m_s5_log_step': 'new_m', 'new_m_s5_b_re': 'new_m', 'new_m_s5_b_im': 'new_m', 'new_m_s5_c_re': 'new_m', 'new_m_s5_c_im': 'new_m', 'new_m_s5_d': 'new_m', 'new_m_s5_w_a': 'new_m', 'new_m_s5_b_a': 'new_m', 'new_m_s5_w_b': 'new_m', 'new_m_s5_b_b': 'new_m', 'new_m_final_norm': 'new_m', 'new_v_ffn1_norm': 'new_v', 'new_v_ffn1_w_gate': 'new_v', 'new_v_ffn1_w_up': 'new_v', 'new_v_ffn1_w_down': 'new_v', 'new_v_mix_norm': 'new_v', 'new_v_ffn2_norm': 'new_v', 'new_v_ffn2_w_gate': 'new_v', 'new_v_ffn2_w_up': 'new_v', 'new_v_ffn2_w_down': 'new_v', 'new_v_hy_w_in': 'new_v', 'new_v_ssd_conv_w': 'new_v', 'new_v_ssd_conv_b': 'new_v', 'new_v_ssd_dt_bias': 'new_v', 'new_v_ssd_a_log': 'new_v', 'new_v_ssd_d': 'new_v', 'new_v_ssd_norm_w': 'new_v', 'new_v_ml_conv_w': 'new_v', 'new_v_ml_conv_b': 'new_v', 'new_v_ml_w_q': 'new_v', 'new_v_ml_w_k': 'new_v', 'new_v_ml_w_v': 'new_v', 'new_v_ml_w_if': 'new_v', 'new_v_ml_b_if': 'new_v', 'new_v_ml_norm_w': 'new_v', 'new_v_ml_skip': 'new_v', 'new_v_hy_w_out': 'new_v', 'new_v_s5_a_re': 'new_v', 'new_v_s5_a_im': 'new_v', 'new_v_s5_log_step': 'new_v', 'new_v_s5_b_re': 'new_v', 'new_v_s5_b_im': 'new_v', 'new_v_s5_c_re': 'new_v', 'new_v_s5_c_im': 'new_v', 'new_v_s5_d': 'new_v', 'new_v_s5_w_a': 'new_v', 'new_v_s5_b_a': 'new_v', 'new_v_s5_w_b': 'new_v', 'new_v_s5_b_b': 'new_v', 'new_v_final_norm': 'new_v'}


def _forward(args):
    return _fwd_reference(*[args[k] for k in FWD_PARAMS])


def _output_shape():
    out = _jax.eval_shape(lambda: _forward(_fwd_setup_inputs(0)))
    return out.shape, out.dtype

N_MICROBATCH = 1
ADAM_LR = 0.001
ADAM_B1 = 0.9
ADAM_B2 = 0.999
ADAM_EPS = 1e-08
ADAM_WD = 0.01
ADAM_STEP = 10
PER_EXAMPLE_BATCH_AXIS = {'x': 0, 'loss_target': 0}
SHARED_INPUTS = []
_WEIGHT_DTYPES = {'ffn1_norm': _jnp.float32, 'ffn1_w_gate': _jnp.float32, 'ffn1_w_up': _jnp.float32, 'ffn1_w_down': _jnp.float32, 'mix_norm': _jnp.float32, 'ffn2_norm': _jnp.float32, 'ffn2_w_gate': _jnp.float32, 'ffn2_w_up': _jnp.float32, 'ffn2_w_down': _jnp.float32, 'hy_w_in': _jnp.float32, 'ssd_conv_w': _jnp.float32, 'ssd_conv_b': _jnp.float32, 'ssd_dt_bias': _jnp.float32, 'ssd_a_log': _jnp.float32, 'ssd_d': _jnp.float32, 'ssd_norm_w': _jnp.float32, 'ml_conv_w': _jnp.float32, 'ml_conv_b': _jnp.float32, 'ml_w_q': _jnp.float32, 'ml_w_k': _jnp.float32, 'ml_w_v': _jnp.float32, 'ml_w_if': _jnp.float32, 'ml_b_if': _jnp.float32, 'ml_norm_w': _jnp.float32, 'ml_skip': _jnp.float32, 'hy_w_out': _jnp.float32, 's5_a_re': _jnp.float32, 's5_a_im': _jnp.float32, 's5_log_step': _jnp.float32, 's5_b_re': _jnp.float32, 's5_b_im': _jnp.float32, 's5_c_re': _jnp.float32, 's5_c_im': _jnp.float32, 's5_d': _jnp.float32, 's5_w_a': _jnp.float32, 's5_b_a': _jnp.float32, 's5_w_b': _jnp.float32, 's5_b_b': _jnp.float32, 'final_norm': _jnp.float32}
MOMENT_SCALE = {'ffn1_norm': 8.203984e-02, 'ffn1_w_gate': 3.538401e-02, 'ffn1_w_up': 3.418990e-02, 'ffn1_w_down': 5.667956e-02, 'mix_norm': 1.410587e-01, 'ffn2_norm': 5.396573e-02, 'ffn2_w_gate': 2.348603e-02, 'ffn2_w_up': 2.275966e-02, 'ffn2_w_down': 3.780049e-02, 'hy_w_in': 9.360145e-02, 'ssd_conv_w': 8.522951e-02, 'ssd_conv_b': 1.178921e-01, 'ssd_dt_bias': 2.291225e-01, 'ssd_a_log': 1.793788e-01, 'ssd_d': 6.721967e-01, 'ssd_norm_w': 1.121558e-01, 'ml_conv_w': 1.002061e-01, 'ml_conv_b': 1.002964e-01, 'ml_w_q': 6.023338e-02, 'ml_w_k': 6.291686e-02, 'ml_w_v': 6.059421e-02, 'ml_w_if': 4.740191e-01, 'ml_b_if': 3.702201e-01, 'ml_norm_w': 6.069604e-02, 'ml_skip': 3.706590e-02, 'hy_w_out': 1.213701e-01, 's5_a_re': 2.960739e-03, 's5_a_im': 2.205200e-03, 's5_log_step': 1.237250e+00, 's5_b_re': 1.523606e-03, 's5_b_im': 1.487530e-03, 's5_c_re': 3.189399e-03, 's5_c_im': 3.083112e-03, 's5_d': 4.831565e-02, 's5_w_a': 4.310294e-02, 's5_b_a': 7.510373e-02, 's5_w_b': 1.285570e-02, 's5_b_b': 2.427609e-02, 'final_norm': 3.195111e+01}


def _to_microbatches(a, axis):
    t = _jnp.moveaxis(a, axis, 0)
    t = t.reshape((N_MICROBATCH, t.shape[0] // N_MICROBATCH) + t.shape[1:])
    return _jnp.moveaxis(t, 1, axis + 1)


def setup_inputs(seed: int = 0) -> dict:
    inp = _fwd_setup_inputs(seed)
    key = _jax.random.fold_in(_jax.random.key(seed), 7919)
    shape, _ = _output_shape()
    out = dict(inp)
    out["loss_target"] = _jax.random.normal(_jax.random.fold_in(key, 0), shape, _jnp.float32)
    for i, name in enumerate(TWIN_WEIGHTS):
        w = inp[name].astype(_jnp.float32)
        if MOMENT_SCALE is None:
            s = _jnp.sqrt(_jnp.mean(_jnp.square(w)) + 1e-30)
        else:
            s = MOMENT_SCALE[name]
        km, kv = _jax.random.split(_jax.random.fold_in(key, i + 1))
        out[name] = w
        out["m_" + name] = s * _jax.random.normal(km, w.shape, _jnp.float32)
        out["v_" + name] = (s * s) * _jax.random.uniform(kv, w.shape, _jnp.float32, 0.5, 1.5)
    if N_MICROBATCH > 1:
        for name, axis in PER_EXAMPLE_BATCH_AXIS.items():
            out[name] = _to_microbatches(out[name], axis)
    return {'x': out['x'], 'ffn1_norm': out['ffn1_norm'], 'ffn1_w_gate': out['ffn1_w_gate'], 'ffn1_w_up': out['ffn1_w_up'], 'ffn1_w_down': out['ffn1_w_down'], 'mix_norm': out['mix_norm'], 'ffn2_norm': out['ffn2_norm'], 'ffn2_w_gate': out['ffn2_w_gate'], 'ffn2_w_up': out['ffn2_w_up'], 'ffn2_w_down': out['ffn2_w_down'], 'hy_w_in': out['hy_w_in'], 'ssd_conv_w': out['ssd_conv_w'], 'ssd_conv_b': out['ssd_conv_b'], 'ssd_dt_bias': out['ssd_dt_bias'], 'ssd_a_log': out['ssd_a_log'], 'ssd_d': out['ssd_d'], 'ssd_norm_w': out['ssd_norm_w'], 'ml_conv_w': out['ml_conv_w'], 'ml_conv_b': out['ml_conv_b'], 'ml_w_q': out['ml_w_q'], 'ml_w_k': out['ml_w_k'], 'ml_w_v': out['ml_w_v'], 'ml_w_if': out['ml_w_if'], 'ml_b_if': out['ml_b_if'], 'ml_norm_w': out['ml_norm_w'], 'ml_skip': out['ml_skip'], 'hy_w_out': out['hy_w_out'], 's5_a_re': out['s5_a_re'], 's5_a_im': out['s5_a_im'], 's5_log_step': out['s5_log_step'], 's5_b_re': out['s5_b_re'], 's5_b_im': out['s5_b_im'], 's5_c_re': out['s5_c_re'], 's5_c_im': out['s5_c_im'], 's5_d': out['s5_d'], 's5_w_a': out['s5_w_a'], 's5_b_a': out['s5_b_a'], 's5_w_b': out['s5_w_b'], 's5_b_b': out['s5_b_b'], 'final_norm': out['final_norm'], 'loss_target': out['loss_target'], 'm_ffn1_norm': out['m_ffn1_norm'], 'm_ffn1_w_gate': out['m_ffn1_w_gate'], 'm_ffn1_w_up': out['m_ffn1_w_up'], 'm_ffn1_w_down': out['m_ffn1_w_down'], 'm_mix_norm': out['m_mix_norm'], 'm_ffn2_norm': out['m_ffn2_norm'], 'm_ffn2_w_gate': out['m_ffn2_w_gate'], 'm_ffn2_w_up': out['m_ffn2_w_up'], 'm_ffn2_w_down': out['m_ffn2_w_down'], 'm_hy_w_in': out['m_hy_w_in'], 'm_ssd_conv_w': out['m_ssd_conv_w'], 'm_ssd_conv_b': out['m_ssd_conv_b'], 'm_ssd_dt_bias': out['m_ssd_dt_bias'], 'm_ssd_a_log': out['m_ssd_a_log'], 'm_ssd_d': out['m_ssd_d'], 'm_ssd_norm_w': out['m_ssd_norm_w'], 'm_ml_conv_w': out['m_ml_conv_w'], 'm_ml_conv_b': out['m_ml_conv_b'], 'm_ml_w_q': out['m_ml_w_q'], 'm_ml_w_k': out['m_ml_w_k'], 'm_ml_w_v': out['m_ml_w_v'], 'm_ml_w_if': out['m_ml_w_if'], 'm_ml_b_if': out['m_ml_b_if'], 'm_ml_norm_w': out['m_ml_norm_w'], 'm_ml_skip': out['m_ml_skip'], 'm_hy_w_out': out['m_hy_w_out'], 'm_s5_a_re': out['m_s5_a_re'], 'm_s5_a_im': out['m_s5_a_im'], 'm_s5_log_step': out['m_s5_log_step'], 'm_s5_b_re': out['m_s5_b_re'], 'm_s5_b_im': out['m_s5_b_im'], 'm_s5_c_re': out['m_s5_c_re'], 'm_s5_c_im': out['m_s5_c_im'], 'm_s5_d': out['m_s5_d'], 'm_s5_w_a': out['m_s5_w_a'], 'm_s5_b_a': out['m_s5_b_a'], 'm_s5_w_b': out['m_s5_w_b'], 'm_s5_b_b': out['m_s5_b_b'], 'm_final_norm': out['m_final_norm'], 'v_ffn1_norm': out['v_ffn1_norm'], 'v_ffn1_w_gate': out['v_ffn1_w_gate'], 'v_ffn1_w_up': out['v_ffn1_w_up'], 'v_ffn1_w_down': out['v_ffn1_w_down'], 'v_mix_norm': out['v_mix_norm'], 'v_ffn2_norm': out['v_ffn2_norm'], 'v_ffn2_w_gate': out['v_ffn2_w_gate'], 'v_ffn2_w_up': out['v_ffn2_w_up'], 'v_ffn2_w_down': out['v_ffn2_w_down'], 'v_hy_w_in': out['v_hy_w_in'], 'v_ssd_conv_w': out['v_ssd_conv_w'], 'v_ssd_conv_b': out['v_ssd_conv_b'], 'v_ssd_dt_bias': out['v_ssd_dt_bias'], 'v_ssd_a_log': out['v_ssd_a_log'], 'v_ssd_d': out['v_ssd_d'], 'v_ssd_norm_w': out['v_ssd_norm_w'], 'v_ml_conv_w': out['v_ml_conv_w'], 'v_ml_conv_b': out['v_ml_conv_b'], 'v_ml_w_q': out['v_ml_w_q'], 'v_ml_w_k': out['v_ml_w_k'], 'v_ml_w_v': out['v_ml_w_v'], 'v_ml_w_if': out['v_ml_w_if'], 'v_ml_b_if': out['v_ml_b_if'], 'v_ml_norm_w': out['v_ml_norm_w'], 'v_ml_skip': out['v_ml_skip'], 'v_hy_w_out': out['v_hy_w_out'], 'v_s5_a_re': out['v_s5_a_re'], 'v_s5_a_im': out['v_s5_a_im'], 'v_s5_log_step': out['v_s5_log_step'], 'v_s5_b_re': out['v_s5_b_re'], 'v_s5_b_im': out['v_s5_b_im'], 'v_s5_c_re': out['v_s5_c_re'], 'v_s5_c_im': out['v_s5_c_im'], 'v_s5_d': out['v_s5_d'], 'v_s5_w_a': out['v_s5_w_a'], 'v_s5_b_a': out['v_s5_b_a'], 'v_s5_w_b': out['v_s5_w_b'], 'v_s5_b_b': out['v_s5_b_b'], 'v_final_norm': out['v_final_norm']}


def _loss(weights, diff, rest, loss_target):
    with _jax.named_scope("forward"):
        args = {**rest, TWIN_DIFF_INPUT: diff, **{k: w.astype(_WEIGHT_DTYPES[k]) for k, w in weights.items()}}
        y = _forward(args)
    with _jax.named_scope("loss_head"):
        err = _jnp.square(y.astype(_jnp.float32) - loss_target)
        return 0.5 * _jnp.sum(_jnp.mean(err, axis=-1)) if err.ndim else 0.5 * err


def _adamw(w, g, m, v):
    m = ADAM_B1 * m + (1.0 - ADAM_B1) * g
    v = ADAM_B2 * v + (1.0 - ADAM_B2) * _jnp.square(g)
    m_hat = m / (1.0 - ADAM_B1 ** ADAM_STEP)
    v_hat = v / (1.0 - ADAM_B2 ** ADAM_STEP)
    delta = -ADAM_LR * (m_hat / (_jnp.sqrt(v_hat) + ADAM_EPS) + ADAM_WD * w)
    return delta, m, v


def reference(x, ffn1_norm, ffn1_w_gate, ffn1_w_up, ffn1_w_down, mix_norm, ffn2_norm, ffn2_w_gate, ffn2_w_up, ffn2_w_down, hy_w_in, ssd_conv_w, ssd_conv_b, ssd_dt_bias, ssd_a_log, ssd_d, ssd_norm_w, ml_conv_w, ml_conv_b, ml_w_q, ml_w_k, ml_w_v, ml_w_if, ml_b_if, ml_norm_w, ml_skip, hy_w_out, s5_a_re, s5_a_im, s5_log_step, s5_b_re, s5_b_im, s5_c_re, s5_c_im, s5_d, s5_w_a, s5_b_a, s5_w_b, s5_b_b, final_norm, loss_target, m_ffn1_norm, m_ffn1_w_gate, m_ffn1_w_up, m_ffn1_w_down, m_mix_norm, m_ffn2_norm, m_ffn2_w_gate, m_ffn2_w_up, m_ffn2_w_down, m_hy_w_in, m_ssd_conv_w, m_ssd_conv_b, m_ssd_dt_bias, m_ssd_a_log, m_ssd_d, m_ssd_norm_w, m_ml_conv_w, m_ml_conv_b, m_ml_w_q, m_ml_w_k, m_ml_w_v, m_ml_w_if, m_ml_b_if, m_ml_norm_w, m_ml_skip, m_hy_w_out, m_s5_a_re, m_s5_a_im, m_s5_log_step, m_s5_b_re, m_s5_b_im, m_s5_c_re, m_s5_c_im, m_s5_d, m_s5_w_a, m_s5_b_a, m_s5_w_b, m_s5_b_b, m_final_norm, v_ffn1_norm, v_ffn1_w_gate, v_ffn1_w_up, v_ffn1_w_down, v_mix_norm, v_ffn2_norm, v_ffn2_w_gate, v_ffn2_w_up, v_ffn2_w_down, v_hy_w_in, v_ssd_conv_w, v_ssd_conv_b, v_ssd_dt_bias, v_ssd_a_log, v_ssd_d, v_ssd_norm_w, v_ml_conv_w, v_ml_conv_b, v_ml_w_q, v_ml_w_k, v_ml_w_v, v_ml_w_if, v_ml_b_if, v_ml_norm_w, v_ml_skip, v_hy_w_out, v_s5_a_re, v_s5_a_im, v_s5_log_step, v_s5_b_re, v_s5_b_im, v_s5_c_re, v_s5_c_im, v_s5_d, v_s5_w_a, v_s5_b_a, v_s5_w_b, v_s5_b_b, v_final_norm):
    given = dict(x=x, ffn1_norm=ffn1_norm, ffn1_w_gate=ffn1_w_gate, ffn1_w_up=ffn1_w_up, ffn1_w_down=ffn1_w_down, mix_norm=mix_norm, ffn2_norm=ffn2_norm, ffn2_w_gate=ffn2_w_gate, ffn2_w_up=ffn2_w_up, ffn2_w_down=ffn2_w_down, hy_w_in=hy_w_in, ssd_conv_w=ssd_conv_w, ssd_conv_b=ssd_conv_b, ssd_dt_bias=ssd_dt_bias, ssd_a_log=ssd_a_log, ssd_d=ssd_d, ssd_norm_w=ssd_norm_w, ml_conv_w=ml_conv_w, ml_conv_b=ml_conv_b, ml_w_q=ml_w_q, ml_w_k=ml_w_k, ml_w_v=ml_w_v, ml_w_if=ml_w_if, ml_b_if=ml_b_if, ml_norm_w=ml_norm_w, ml_skip=ml_skip, hy_w_out=hy_w_out, s5_a_re=s5_a_re, s5_a_im=s5_a_im, s5_log_step=s5_log_step, s5_b_re=s5_b_re, s5_b_im=s5_b_im, s5_c_re=s5_c_re, s5_c_im=s5_c_im, s5_d=s5_d, s5_w_a=s5_w_a, s5_b_a=s5_b_a, s5_w_b=s5_w_b, s5_b_b=s5_b_b, final_norm=final_norm, loss_target=loss_target, m_ffn1_norm=m_ffn1_norm, m_ffn1_w_gate=m_ffn1_w_gate, m_ffn1_w_up=m_ffn1_w_up, m_ffn1_w_down=m_ffn1_w_down, m_mix_norm=m_mix_norm, m_ffn2_norm=m_ffn2_norm, m_ffn2_w_gate=m_ffn2_w_gate, m_ffn2_w_up=m_ffn2_w_up, m_ffn2_w_down=m_ffn2_w_down, m_hy_w_in=m_hy_w_in, m_ssd_conv_w=m_ssd_conv_w, m_ssd_conv_b=m_ssd_conv_b, m_ssd_dt_bias=m_ssd_dt_bias, m_ssd_a_log=m_ssd_a_log, m_ssd_d=m_ssd_d, m_ssd_norm_w=m_ssd_norm_w, m_ml_conv_w=m_ml_conv_w, m_ml_conv_b=m_ml_conv_b, m_ml_w_q=m_ml_w_q, m_ml_w_k=m_ml_w_k, m_ml_w_v=m_ml_w_v, m_ml_w_if=m_ml_w_if, m_ml_b_if=m_ml_b_if, m_ml_norm_w=m_ml_norm_w, m_ml_skip=m_ml_skip, m_hy_w_out=m_hy_w_out, m_s5_a_re=m_s5_a_re, m_s5_a_im=m_s5_a_im, m_s5_log_step=m_s5_log_step, m_s5_b_re=m_s5_b_re, m_s5_b_im=m_s5_b_im, m_s5_c_re=m_s5_c_re, m_s5_c_im=m_s5_c_im, m_s5_d=m_s5_d, m_s5_w_a=m_s5_w_a, m_s5_b_a=m_s5_b_a, m_s5_w_b=m_s5_w_b, m_s5_b_b=m_s5_b_b, m_final_norm=m_final_norm, v_ffn1_norm=v_ffn1_norm, v_ffn1_w_gate=v_ffn1_w_gate, v_ffn1_w_up=v_ffn1_w_up, v_ffn1_w_down=v_ffn1_w_down, v_mix_norm=v_mix_norm, v_ffn2_norm=v_ffn2_norm, v_ffn2_w_gate=v_ffn2_w_gate, v_ffn2_w_up=v_ffn2_w_up, v_ffn2_w_down=v_ffn2_w_down, v_hy_w_in=v_hy_w_in, v_ssd_conv_w=v_ssd_conv_w, v_ssd_conv_b=v_ssd_conv_b, v_ssd_dt_bias=v_ssd_dt_bias, v_ssd_a_log=v_ssd_a_log, v_ssd_d=v_ssd_d, v_ssd_norm_w=v_ssd_norm_w, v_ml_conv_w=v_ml_conv_w, v_ml_conv_b=v_ml_conv_b, v_ml_w_q=v_ml_w_q, v_ml_w_k=v_ml_w_k, v_ml_w_v=v_ml_w_v, v_ml_w_if=v_ml_w_if, v_ml_b_if=v_ml_b_if, v_ml_norm_w=v_ml_norm_w, v_ml_skip=v_ml_skip, v_hy_w_out=v_hy_w_out, v_s5_a_re=v_s5_a_re, v_s5_a_im=v_s5_a_im, v_s5_log_step=v_s5_log_step, v_s5_b_re=v_s5_b_re, v_s5_b_im=v_s5_b_im, v_s5_c_re=v_s5_c_re, v_s5_c_im=v_s5_c_im, v_s5_d=v_s5_d, v_s5_w_a=v_s5_w_a, v_s5_b_a=v_s5_b_a, v_s5_w_b=v_s5_w_b, v_s5_b_b=v_s5_b_b, v_final_norm=v_final_norm)
    weights = {n: given[n] for n in TWIN_WEIGHTS}
    shared = {n: given[n] for n in SHARED_INPUTS}
    per_example = {n: given[n] for n in ['x']}
    grad_fn = _jax.value_and_grad(_loss, argnums=(0, 1))

    def one_microbatch(ex, loss_target):
        ex = dict(ex)
        diff = ex.pop(TWIN_DIFF_INPUT)
        return grad_fn(weights, diff, {**shared, **ex}, loss_target)

    if N_MICROBATCH == 1:
        loss, (grad_w, grad_x) = one_microbatch(per_example, given["loss_target"])
    else:
        def body(carry, xs):
            loss_sum, grad_sum = carry
            l_k, (gw_k, gx_k) = one_microbatch(xs[0], xs[1])
            with _jax.named_scope("update"):
                return (loss_sum + l_k, _jax.tree.map(_jnp.add, grad_sum, gw_k)), gx_k

        init = (_jnp.zeros((), _jnp.float32), _jax.tree.map(_jnp.zeros_like, weights))
        (loss, grad_w), grad_x = _jax.lax.scan(body, init, (per_example, given["loss_target"]))
    with _jax.named_scope("update"):
        delta_w, new_m, new_v = {}, {}, {}
        for n in TWIN_WEIGHTS:
            delta_w[n], new_m[n], new_v[n] = _adamw(weights[n], grad_w[n], given["m_" + n], given["v_" + n])
    return (loss, grad_x, *[grad_w[n] for n in TWIN_WEIGHTS], *[delta_w[n] for n in TWIN_WEIGHTS],
            *[new_m[n] for n in TWIN_WEIGHTS], *[new_v[n] for n in TWIN_WEIGHTS])
```

```python
import functools
import math

import jax
import jax.numpy as jnp
from jax import lax
from jax.experimental import pallas as pl
from jax.experimental.pallas import tpu as pltpu

f32 = jnp.float32
bf16 = jnp.bfloat16

N_DEV = 8
D_MODEL = 1024
D_FF = 2816
EPS = 1e-6
FFN_RES = 0.5
CONV_W = 4
SSD_HEADS = 16
SSD_HEAD_DIM = 64
SSD_GROUPS = 2
SSD_STATE = 128
SSD_HG = SSD_HEADS // SSD_GROUPS
SSD_GW = SSD_HG * SSD_HEAD_DIM
CHUNK = 128
ML_HEADS = 4
ML_HD = 256
S5_GROUP = 16
S5_GROUPS = 64
S5_STATE = 64
S5_CB = 8
S5_CH = (S5_GROUPS // S5_CB) * S5_STATE
S5_TL = 128
LANES = 128
IN_COLS = 4624
PROJ_W = 4864
OFF_Z, OFF_MX, OFF_MZ, OFF_XBC, OFF_DT = 0, 1024, 2048, 3072, 4608
ADAM_LR, ADAM_B1, ADAM_B2, ADAM_EPS, ADAM_WD, ADAM_STEP = 0.001, 0.9, 0.999, 1e-08, 0.01, 10
NEG = -1e30
VMEM_LIMIT = 56 * 1024 * 1024
HI = lax.Precision.HIGHEST


def _cp(n):
    return pltpu.CompilerParams(dimension_semantics=("arbitrary",) * n, vmem_limit_bytes=VMEM_LIMIT)


def _dg(a, b, ca, cb):
    return lax.dot_general(a.astype(bf16), b.astype(bf16), (((ca,), (cb,)), ((), ())), preferred_element_type=f32)


@functools.partial(jax.custom_vjp, nondiff_argnums=(2, 3))
def bdot(a, b, ca, cb):
    return _dg(a, b, ca, cb)


def _bdot_fwd(a, b, ca, cb):
    return _dg(a, b, ca, cb), (a, b)


def _bdot_bwd(ca, cb, res, ct):
    a, b = res
    da = _dg(ct, b, 1, 1 - cb) if ca == 1 else _dg(b, ct, 1 - cb, 1)
    db = _dg(a, ct, 1 - ca, 0) if cb == 0 else _dg(ct, a, 0, 1 - ca)
    return da, db


bdot.defvjp(_bdot_fwd, _bdot_bwd)


def hdot(a, b):
    return jnp.dot(a, b, precision=HI, preferred_element_type=f32)


def _iota(shape, dim):
    return lax.broadcasted_iota(jnp.int32, shape, dim)


def _tri(n):
    return (_iota((n, n), 0) >= _iota((n, n), 1))


@functools.partial(jax.custom_vjp, nondiff_argnums=(1,))
def tshift(x, k):
    return jnp.where(_iota(x.shape, 0) >= k, pltpu.roll(x, k, 0), 0.0)


def _tshift_fwd(x, k):
    return tshift(x, k), None


def _tshift_bwd(k, _, ct):
    n = ct.shape[0]
    return (jnp.where(_iota(ct.shape, 0) < n - k, pltpu.roll(ct, n - k, 0), 0.0),)


tshift.defvjp(_tshift_fwd, _tshift_bwd)


def _lane_pick(a, idx):
    return jnp.sum(jnp.where(_iota(a.shape, 1) == idx, a, 0.0), axis=1, keepdims=True)


def _row_pick(a, idx):
    return jnp.sum(jnp.where(_iota(a.shape, 0) == idx, a, 0.0), axis=0, keepdims=True)


def _silu(x):
    return x * jax.nn.sigmoid(x)


def map_fwd(name, f, grid, ins, in_specs, out_shapes, out_specs):
    n_in = len(ins)

    def body(*refs):
        pids = tuple(pl.program_id(i) for i in range(len(grid)))
        outs = f(pids, *[r[...] for r in refs[:n_in]])
        for r, o in zip(refs[n_in:], outs):
            r[...] = o.astype(r.dtype)

    return pl.pallas_call(body, name=name, grid=grid, in_specs=in_specs, out_specs=out_specs,
                          out_shape=out_shapes, compiler_params=_cp(len(grid)))(*ins)


def scan_fwd(name, f, grid, slot_axis, ins, in_specs, out_shapes, out_specs, state_shapes, state_init, save_shapes, save_specs):
    n_in, n_out, n_st = len(ins), len(out_shapes), len(state_shapes)
    n_slots = grid[slot_axis]
    cax = len(grid) - 1 if slot_axis != len(grid) - 1 else len(grid) - 2

    def body(*refs):
        pids = tuple(pl.program_id(i) for i in range(len(grid)))
        in_refs, out_refs = refs[:n_in], refs[n_in:n_in + n_out]
        save_refs = refs[n_in + n_out:n_in + n_out + n_st]
        st_refs = refs[n_in + n_out + n_st:]
        slot = pids[slot_axis]

        @pl.when(pids[cax] == 0)
        def _():
            for s, init in zip(st_refs, state_init):
                s[slot] = jnp.full(s.shape[1:], init, f32)

        states = tuple(s[slot] for s in st_refs)
        for sv, st in zip(save_refs, states):
            sv[...] = st.reshape(sv.shape)
        outs, new = f(pids, states, *[r[...] for r in in_refs])
        for r, o in zip(out_refs, outs):
            r[...] = o.astype(r.dtype)
        for s, v in zip(st_refs, new):
            s[slot] = v

    scratch = [pltpu.VMEM((n_slots,) + tuple(s), f32) for s in state_shapes]
    return pl.pallas_call(body, name=name, grid=grid, in_specs=in_specs, out_specs=list(out_specs) + list(save_specs),
                          out_shape=list(out_shapes) + list(save_shapes), scratch_shapes=scratch,
                          compiler_params=_cp(len(grid)))(*ins)


def scan_bwd(name, f, grid, slot_axis, ins, in_specs, saves, save_specs, cts, ct_specs, state_shapes, wrt, acc_first):
    n_in, n_st, n_ct = len(ins), len(saves), len(cts)
    n_slots = grid[slot_axis]
    cax = len(grid) - 1 if slot_axis != len(grid) - 1 else len(grid) - 2

    def body(*refs):
        pids = tuple(pl.program_id(i) for i in range(len(grid)))
        in_refs = refs[:n_in]
        save_refs = refs[n_in:n_in + n_st]
        ct_refs = refs[n_in + n_st:n_in + n_st + n_ct]
        out_refs = refs[n_in + n_st + n_ct:n_in + n_st + n_ct + len(wrt)]
        dst_refs = refs[n_in + n_st + n_ct + len(wrt):]
        slot = pids[slot_axis]

        @pl.when(pids[cax] == 0)
        def _():
            for s in dst_refs:
                s[slot] = jnp.zeros(s.shape[1:], f32)

        vals = [r[...] for r in in_refs]
        states = tuple(sv[...].reshape(shp) for sv, shp in zip(save_refs, state_shapes))
        ctv = tuple(r[...].astype(f32) for r in ct_refs)
        dnew = tuple(s[slot] for s in dst_refs)

        def g(st, *dv):
            full = list(vals)
            for i, v in zip(wrt, dv):
                full[i] = v
            outs, new = f(pids, st, *full)
            return tuple(outs), tuple(new)

        _, vjp = jax.vjp(g, states, *[vals[i] for i in wrt])
        grads = vjp((ctv, dnew))
        for s, v in zip(dst_refs, grads[0]):
            s[slot] = v
        for i, o_ref, gr in zip(wrt, out_refs, grads[1:]):
            first = acc_first.get(i)
            if first is None:
                o_ref[...] = gr.astype(o_ref.dtype)
            else:
                @pl.when(first(pids))
                def _():
                    o_ref[...] = jnp.zeros_like(o_ref)
                o_ref[...] += gr

    out_shapes = [jax.ShapeDtypeStruct(ins[i].shape, f32) for i in wrt]
    out_specs = [in_specs[i] for i in wrt]
    scratch = [pltpu.VMEM((n_slots,) + tuple(s), f32) for s in state_shapes]
    return pl.pallas_call(body, name=name, grid=grid, in_specs=list(in_specs) + list(save_specs) + list(ct_specs),
                          out_specs=out_specs, out_shape=out_shapes, scratch_shapes=scratch,
                          compiler_params=_cp(len(grid)))(*ins, *saves, *cts)


def matmul(name, a, b, ca=1, cb=0, add=None, out_dtype=f32, a_off=0, a_width=None, tm=512, tn=512, tk=512):
    a_width = a.shape[1] if a_width is None else a_width
    kdim = b.shape[cb]
    n = b.shape[1 - cb]
    m = a.shape[0] if ca == 1 else a_width
    tm, tn, tk = min(tm, m), min(tn, n), min(tk, kdim)
    assert m % tm == 0 and n % tn == 0 and kdim % tk == 0
    nk = kdim // tk
    if ca == 1:
        assert a_off % tk == 0 and a_width == kdim
        koff = a_off // tk
        a_spec = pl.BlockSpec((tm, tk), lambda i, j, k: (i, k + koff))
    else:
        assert a_off % tm == 0 and a.shape[0] == kdim
        ioff = a_off // tm
        a_spec = pl.BlockSpec((tk, tm), lambda i, j, k: (k, i + ioff))
    b_spec = pl.BlockSpec((tk, tn), lambda i, j, k: (k, j)) if cb == 0 else pl.BlockSpec((tn, tk), lambda i, j, k: (j, k))
    o_spec = pl.BlockSpec((tm, tn), lambda i, j, k: (i, j))
    has_add = add is not None

    def body(*refs):
        a_ref, b_ref = refs[0], refs[1]
        add_ref = refs[2] if has_add else None
        o_ref, acc = refs[-2], refs[-1]
        k = pl.program_id(2)

        @pl.when(k == 0)
        def _():
            acc[...] = add_ref[...].astype(f32) if has_add else jnp.zeros_like(acc)

        acc[...] += _dg(a_ref[...], b_ref[...], ca, cb)

        @pl.when(k == nk - 1)
        def _():
            o_ref[...] = acc[...].astype(o_ref.dtype)

    ins = [a, b] + ([add] if has_add else [])
    specs = [a_spec, b_spec] + ([o_spec] if has_add else [])
    return pl.pallas_call(body, name=name, grid=(m // tm, n // tn, nk), in_specs=specs, out_specs=o_spec,
                          out_shape=jax.ShapeDtypeStruct((m, n), out_dtype),
                          scratch_shapes=[pltpu.VMEM((tm, tn), f32)], compiler_params=_cp(3))(*ins)


def f_rms(pids, x, w):
    r = lax.rsqrt(jnp.mean(x * x, axis=-1, keepdims=True) + EPS)
    return (x * r * w,)


def _row_spec(tm, width, col=0):
    return pl.BlockSpec((tm, width), lambda i: (i, col))


def _par_spec(shape):
    return pl.BlockSpec(shape, lambda *p: (0,) * len(shape))


def rms_fwd(x, w, tm=512):
    t, d = x.shape
    return map_fwd("rms_fwd", f_rms, (t // tm,), [x, w], [_row_spec(tm, d), _par_spec((1, d))],
                   [jax.ShapeDtypeStruct((t, d), f32)], [_row_spec(tm, d)])[0]


def rms_bwd(dys, x, w, dres, tm=512):
    t, d = x.shape
    n = len(dys)

    def body(*refs):
        x_ref, w_ref, dres_ref, dx_ref, dw_ref = refs[n:]
        dy = refs[0][...]
        for r in refs[1:n]:
            dy = dy + r[...]
        _, vjp = jax.vjp(lambda xx, ww: f_rms(None, xx, ww)[0], x_ref[...], w_ref[...])
        dx, dw = vjp(dy)
        dx_ref[...] = dx + dres_ref[...]

        @pl.when(pl.program_id(0) == 0)
        def _():
            dw_ref[...] = jnp.zeros_like(dw_ref)
        dw_ref[...] += dw

    return pl.pallas_call(body, name="rms_bwd", grid=(t // tm,),
                          in_specs=[_row_spec(tm, d)] * (n + 1) + [_par_spec((1, d)), _row_spec(tm, d)],
                          out_specs=[_row_spec(tm, d), _par_spec((1, d))],
                          out_shape=[jax.ShapeDtypeStruct((t, d), f32), jax.ShapeDtypeStruct((1, d), f32)],
                          compiler_params=_cp(1))(*dys, x, w, dres)


def loss_head(x, w, tgt, tm=512):
    t, d = x.shape

    def fl(xx, ww, tt):
        y = f_rms(None, xx, ww)[0]
        return 0.5 * jnp.sum(jnp.mean(jnp.square(y - tt), axis=-1, keepdims=True), axis=0, keepdims=True)

    def body(x_ref, w_ref, t_ref, loss_ref, dx_ref, dw_ref):
        val, vjp = jax.vjp(lambda xx, ww: fl(xx, ww, t_ref[...]), x_ref[...], w_ref[...])
        dx, dw = vjp(jnp.ones((1, 1), f32))
        dx_ref[...] = dx

        @pl.when(pl.program_id(0) == 0)
        def _():
            dw_ref[...] = jnp.zeros_like(dw_ref)
            loss_ref[...] = jnp.zeros_like(loss_ref)
        dw_ref[...] += dw
        loss_ref[...] += val

    return pl.pallas_call(body, name="loss_head", grid=(t // tm,),
                          in_specs=[_row_spec(tm, d), _par_spec((1, d)), _row_spec(tm, d)],
                          out_specs=[_par_spec((1, 1)), _row_spec(tm, d), _par_spec((1, d))],
                          out_shape=[jax.ShapeDtypeStruct((1, 1), f32), jax.ShapeDtypeStruct((t, d), f32),
                                     jax.ShapeDtypeStruct((1, d), f32)],
                          compiler_params=_cp(1))(x, w, tgt)


def ffn_fwd(x, nw, wg, wu, wd, tm=512):
    t, d = x.shape
    ns, _, fs = wg.shape

    def body(x_ref, nw_ref, wg_ref, wu_ref, wd_ref, xo_ref, h_ref, g_ref, u_ref, acc):
        j = pl.program_id(1)

        @pl.when(j == 0)
        def _():
            h_ref[...] = f_rms(None, x_ref[...], nw_ref[...])[0].astype(bf16)
            acc[...] = jnp.zeros_like(acc)

        h = h_ref[...]
        g = jnp.dot(h, wg_ref[0], preferred_element_type=f32)
        u = jnp.dot(h, wu_ref[0], preferred_element_type=f32)
        g_ref[0] = g
        u_ref[0] = u
        acc[...] += jnp.dot((_silu(g) * u).astype(bf16), wd_ref[0], preferred_element_type=f32)

        @pl.when(j == ns - 1)
        def _():
            xo_ref[...] = x_ref[...] + FFN_RES * acc[...]

    row = pl.BlockSpec((tm, d), lambda i, j: (i, 0))
    wcol = pl.BlockSpec((1, d, fs), lambda i, j: (j, 0, 0))
    wrow = pl.BlockSpec((1, fs, d), lambda i, j: (j, 0, 0))
    act = pl.BlockSpec((1, tm, fs), lambda i, j: (j, i, 0))
    return pl.pallas_call(body, name="ffn_fwd", grid=(t // tm, ns),
                          in_specs=[row, pl.BlockSpec((1, d), lambda i, j: (0, 0)), wcol, wcol, wrow],
                          out_specs=[row, row, act, act],
                          out_shape=[jax.ShapeDtypeStruct((t, d), f32), jax.ShapeDtypeStruct((t, d), bf16),
                                     jax.ShapeDtypeStruct((ns, t, fs), f32), jax.ShapeDtypeStruct((ns, t, fs), f32)],
                          scratch_shapes=[pltpu.VMEM((tm, d), f32)], compiler_params=_cp(2))(x, nw, wg, wu, wd)


def ffn_bwd_act(dy, x, nw, g, u, wg, wu, wd, tm=512):
    t, d = x.shape
    ns, _, fs = wg.shape

    def body(dy_ref, x_ref, nw_ref, g_ref, u_ref, wg_ref, wu_ref, wd_ref, dx_ref, dnw_ref, dg_ref, du_ref, a_ref, acc):
        i, j = pl.program_id(0), pl.program_id(1)

        @pl.when(j == 0)
        def _():
            acc[...] = jnp.zeros_like(acc)

        dyh = (FFN_RES * dy_ref[...]).astype(bf16)
        da = _dg(dyh, wd_ref[0], 1, 1)
        gg, uu = g_ref[0], u_ref[0]
        sg = jax.nn.sigmoid(gg)
        si = gg * sg
        dgv = (da * uu * (sg * (1.0 + gg * (1.0 - sg)))).astype(bf16)
        duv = (da * si).astype(bf16)
        dg_ref[0] = dgv
        du_ref[0] = duv
        a_ref[0] = (si * uu).astype(bf16)
        acc[...] += _dg(dgv, wg_ref[0], 1, 1) + _dg(duv, wu_ref[0], 1, 1)

        @pl.when(j == ns - 1)
        def _():
            _, vjp = jax.vjp(lambda xx, ww: f_rms(None, xx, ww)[0], x_ref[...], nw_ref[...])
            dx, dw = vjp(acc[...])
            dx_ref[...] = dx + dy_ref[...]

            @pl.when(i == 0)
            def _():
                dnw_ref[...] = jnp.zeros_like(dnw_ref)
            dnw_ref[...] += dw

    row = pl.BlockSpec((tm, d), lambda i, j: (i, 0))
    wcol = pl.BlockSpec((1, d, fs), lambda i, j: (j, 0, 0))
    wrow = pl.BlockSpec((1, fs, d), lambda i, j: (j, 0, 0))
    act = pl.BlockSpec((1, tm, fs), lambda i, j: (j, i, 0))
    par = pl.BlockSpec((1, d), lambda i, j: (0, 0))
    return pl.pallas_call(body, name="ffn_bwd_act", grid=(t // tm, ns),
                          in_specs=[row, row, par, act, act, wcol, wcol, wrow],
                          out_specs=[row, par, act, act, act],
                          out_shape=[jax.ShapeDtypeStruct((t, d), f32), jax.ShapeDtypeStruct((1, d), f32)]
                          + [jax.ShapeDtypeStruct((ns, t, fs), bf16)] * 3,
                          scratch_shapes=[pltpu.VMEM((tm, d), f32)], compiler_params=_cp(2))(dy, x, nw, g, u, wg, wu, wd)


def ffn_bwd_w(h, dy, dg, du, a, tk=512):
    t, d = h.shape
    ns, _, fs = dg.shape
    nk = t // tk

    def body(h_ref, dy_ref, dg_ref, du_ref, a_ref, og, ou, od, ag, au, ad):
        k = pl.program_id(1)

        @pl.when(k == 0)
        def _():
            ag[...] = jnp.zeros_like(ag)
            au[...] = jnp.zeros_like(au)
            ad[...] = jnp.zeros_like(ad)

        hh = h_ref[...]
        ag[...] += _dg(hh, dg_ref[0], 0, 0)
        au[...] += _dg(hh, du_ref[0], 0, 0)
        ad[...] += _dg(a_ref[0], FFN_RES * dy_ref[...], 0, 0)

        @pl.when(k == nk - 1)
        def _():
            og[0] = ag[...].astype(og.dtype)
            ou[0] = au[...].astype(ou.dtype)
            od[0] = ad[...].astype(od.dtype)

    row = pl.BlockSpec((tk, d), lambda j, k: (k, 0))
    act = pl.BlockSpec((1, tk, fs), lambda j, k: (j, k, 0))
    wcol = pl.BlockSpec((1, d, fs), lambda j, k: (j, 0, 0))
    wrow = pl.BlockSpec((1, fs, d), lambda j, k: (j, 0, 0))
    return pl.pallas_call(body, name="ffn_bwd_w", grid=(ns, nk), in_specs=[row, row, act, act, act],
                          out_specs=[wcol, wcol, wrow],
                          out_shape=[jax.ShapeDtypeStruct((ns, d, fs), bf16)] * 2 + [jax.ShapeDtypeStruct((ns, fs, d), bf16)],
                          scratch_shapes=[pltpu.VMEM((d, fs), f32), pltpu.VMEM((d, fs), f32), pltpu.VMEM((fs, d), f32)],
                          compiler_params=_cp(2))(h, dy, dg, du, a)


def f_conv(pids, x, w, b):
    y = b + x * w[CONV_W - 1:CONV_W, :]
    for j in range(CONV_W - 1):
        y = y + tshift(x, CONV_W - 1 - j) * w[j:j + 1, :]
    return (_silu(y),)


def _conv_specs(seq, col0, cb):
    xs = pl.BlockSpec((seq, cb), lambda c, b: (b, col0 + c))
    ws = pl.BlockSpec((CONV_W, cb), lambda c, b: (0, c))
    bs = pl.BlockSpec((1, cb), lambda c, b: (0, c))
    ys = pl.BlockSpec((seq, cb), lambda c, b: (b, c))
    return xs, ws, bs, ys


def conv_fwd(name, src, col_off, w, b, seq, cb=256):
    t = src.shape[0]
    c = w.shape[1]
    xs, ws, bs, ys = _conv_specs(seq, col_off // cb, cb)
    return map_fwd(name, f_conv, (c // cb, t // seq), [src, w, b], [xs, ws, bs],
                   [jax.ShapeDtypeStruct((t, c), f32)], [ys])[0]


def conv_bwd(name, dy, src, col_off, w, b, seq, cb=256):
    t = src.shape[0]
    c = w.shape[1]
    xs, ws, bs, ys = _conv_specs(seq, col_off // cb, cb)

    def body(x_ref, w_ref, b_ref, dy_ref, dx_ref, dw_ref, db_ref):
        _, vjp = jax.vjp(lambda xx, ww, bb: f_conv(None, xx, ww, bb)[0], x_ref[...], w_ref[...], b_ref[...])
        dx, dw, db = vjp(dy_ref[...])
        dx_ref[...] = dx

        @pl.when(pl.program_id(1) == 0)
        def _():
            dw_ref[...] = jnp.zeros_like(dw_ref)
            db_ref[...] = jnp.zeros_like(db_ref)
        dw_ref[...] += dw
        db_ref[...] += db

    return pl.pallas_call(body, name=name, grid=(c // cb, t // seq), in_specs=[xs, ws, bs, ys], out_specs=[ys, ws, bs],
                          out_shape=[jax.ShapeDtypeStruct((t, c), f32), jax.ShapeDtypeStruct(w.shape, f32),
                                     jax.ShapeDtypeStruct(b.shape, f32)], compiler_params=_cp(2))(src, w, b, dy)


def f_ssd(pids, states, xs, dtraw, bm, cm, a_log, dt_bias, d_skip):
    g = pids[2]
    (hn,) = states
    l = xs.shape[0]
    head_of_lane = _iota((LANES, SSD_GW), 1) // SSD_HEAD_DIM + SSD_HG * g
    expand = (_iota((LANES, SSD_GW), 0) == head_of_lane).astype(f32)
    tri = _tri(l)
    dt = jax.nn.softplus(dtraw + dt_bias)
    adt = dt * (-jnp.exp(a_log))
    cs = hdot(tri.astype(f32), adt)
    cst = cs.T
    cs_last = cs[l - 1:l, :]
    dt_e, cs_e, csl_e = hdot(dt, expand), hdot(cs, expand), hdot(cs_last, expand)
    xd = xs * dt_e
    gmat = bdot(cm, bm, 1, 1)
    half = _iota((l, LANES), 1) < SSD_HEAD_DIM
    blocks = []
    for pair in range(SSD_HG // 2):
        xb = xd[:, pair * LANES:(pair + 1) * LANES]
        res = []
        for sub in range(2):
            hid = SSD_HG * g + 2 * pair + sub
            col, row = _lane_pick(cs, hid), _row_pick(cst, hid)
            lm = jnp.exp(jnp.where(tri, col - row, NEG))
            res.append(bdot(gmat * lm, xb, 1, 0))
        blocks.append(jnp.where(half, res[0], res[1]))
    y = jnp.concatenate(blocks, axis=1)
    y = y + jnp.exp(cs_e) * bdot(cm, hn, 1, 0)
    y = y + hdot(d_skip, expand) * xs
    hn_new = jnp.exp(csl_e) * hn + bdot(bm, jnp.exp(csl_e - cs_e) * xd, 0, 0)
    return (y,), (hn_new,)


def _ssd_specs(seq, nch, rev):
    cc = (lambda c: nch - 1 - c) if rev else (lambda c: c)
    xs = pl.BlockSpec((CHUNK, SSD_GW), lambda b, c, g: (b * nch + cc(c), g))
    dt = pl.BlockSpec((CHUNK, LANES), lambda b, c, g: (b * nch + cc(c), OFF_DT // LANES))
    bm = pl.BlockSpec((CHUNK, SSD_STATE), lambda b, c, g: (b * nch + cc(c), 1024 // SSD_STATE + g))
    cm = pl.BlockSpec((CHUNK, SSD_STATE), lambda b, c, g: (b * nch + cc(c), 1024 // SSD_STATE + SSD_GROUPS + g))
    par = pl.BlockSpec((1, LANES), lambda b, c, g: (0, 0))
    sv = pl.BlockSpec((1, 1, SSD_STATE, SSD_GW), lambda b, c, g: (b * nch + cc(c), g, 0, 0))
    ddt = pl.BlockSpec((CHUNK, LANES), lambda b, c, g: (b * nch + cc(c), 0))
    dbc = pl.BlockSpec((CHUNK, SSD_STATE), lambda b, c, g: (b * nch + cc(c), g))
    return xs, dt, bm, cm, par, sv, ddt, dbc


def ssd_fwd(xbc, proj, a_log, dt_bias, d_skip, seq):
    t = xbc.shape[0]
    nch = seq // CHUNK
    xs, dt, bm, cm, par, sv, _, _ = _ssd_specs(seq, nch, False)
    grid = (t // seq, nch, SSD_GROUPS)
    y, hsave = scan_fwd("ssd_fwd", f_ssd, grid, 2, [xbc, proj, xbc, xbc, a_log, dt_bias, d_skip],
                        [xs, dt, bm, cm, par, par, par], [jax.ShapeDtypeStruct((t, SSD_GROUPS * SSD_GW), f32)], [xs],
                        [(SSD_STATE, SSD_GW)], [0.0],
                        [jax.ShapeDtypeStruct((t // CHUNK, SSD_GROUPS, SSD_STATE, SSD_GW), f32)], [sv])
    return y, hsave


def ssd_bwd(dy, xbc, proj, a_log, dt_bias, d_skip, hsave, seq):
    t = xbc.shape[0]
    nch = seq // CHUNK
    xs, dt, bm, cm, par, sv, ddt, dbc = _ssd_specs(seq, nch, True)
    grid = (t // seq, nch, SSD_GROUPS)

    def body(x_ref, dt_ref, b_ref, c_ref, al_ref, db_ref, ds_ref, h_ref, dy_ref,
             dxbc_x, dxbc_b, dxbc_c, ddt_ref, dal_ref, ddb_ref, dds_ref, dst):
        pids = tuple(pl.program_id(i) for i in range(3))
        slot = pids[2]

        @pl.when(pids[1] == 0)
        def _():
            dst[slot] = jnp.zeros(dst.shape[1:], f32)

        vals = [x_ref[...], dt_ref[...], b_ref[...], c_ref[...], al_ref[...], db_ref[...], ds_ref[...]]

        def gfun(st, *v):
            outs, new = f_ssd(pids, (st,), *v)
            return outs[0], new[0]

        _, vjp = jax.vjp(gfun, h_ref[0, 0], *vals)
        grads = vjp((dy_ref[...], dst[slot]))
        dst[slot] = grads[0]
        dxbc_x[...] = grads[1]
        dxbc_b[...] = grads[3]
        dxbc_c[...] = grads[4]

        @pl.when(slot == 0)
        def _():
            ddt_ref[...] = jnp.zeros_like(ddt_ref)
        ddt_ref[...] += grads[2]
        first = jnp.logical_and(jnp.logical_and(pids[0] == 0, pids[1] == 0), slot == 0)

        @pl.when(first)
        def _():
            dal_ref[...] = jnp.zeros_like(dal_ref)
            ddb_ref[...] = jnp.zeros_like(ddb_ref)
            dds_ref[...] = jnp.zeros_like(dds_ref)
        dal_ref[...] += grads[5]
        ddb_ref[...] += grads[6]
        dds_ref[...] += grads[7]

    bc_shape = jax.ShapeDtypeStruct((t, SSD_GROUPS * SSD_STATE), f32)
    par_shape = jax.ShapeDtypeStruct((1, LANES), f32)
    outs = pl.pallas_call(body, name="ssd_bwd", grid=grid, in_specs=[xs, dt, bm, cm, par, par, par, sv, xs],
                          out_specs=[xs, dbc, dbc, ddt, par, par, par],
                          out_shape=[jax.ShapeDtypeStruct((t, SSD_GROUPS * SSD_GW), f32),
                                     bc_shape, bc_shape, jax.ShapeDtypeStruct((t, LANES), f32),
                                     par_shape, par_shape, par_shape],
                          scratch_shapes=[pltpu.VMEM((SSD_GROUPS, SSD_STATE, SSD_GW), f32)],
                          compiler_params=_cp(3))(xbc, proj, xbc, xbc, a_log, dt_bias, d_skip, hsave, dy)
    return outs


def f_ssd_epi(pids, y, z, nw):
    yg = y * _silu(z)
    hw = yg.shape[1] // SSD_GROUPS
    parts = []
    for g in range(SSD_GROUPS):
        p = yg[:, g * hw:(g + 1) * hw]
        parts.append(p * lax.rsqrt(jnp.mean(p * p, axis=-1, keepdims=True) + EPS))
    return (jnp.concatenate(parts, axis=1) * nw,)


def f_ml_epi(pids, hm, xc, mz, nw, skip):
    parts = []
    for h in range(ML_HEADS):
        p = hm[:, h * ML_HD:(h + 1) * ML_HD]
        mu = jnp.mean(p, axis=-1, keepdims=True)
        var = jnp.mean(jnp.square(p - mu), axis=-1, keepdims=True)
        parts.append((p - mu) * lax.rsqrt(var + EPS))
    hn = jnp.concatenate(parts, axis=1) * nw
    return ((hn + skip * xc) * _silu(mz),)


def f_s5_post(pids, ys, u, d_skip):
    return (jax.nn.gelu(ys + d_skip * u),)


def f_glu(pids, pab, ba, bb):
    d = ba.shape[1]
    return ((pab[:, :d] + ba) * jax.nn.sigmoid(pab[:, d:] + bb),)


def f_glu_res(pids, pab, xres, ba, bb):
    return (xres + f_glu(pids, pab, ba, bb)[0],)


def rowwise_fwd(name, f, rows, row_cols, pars, out_width, tm=512):
    t = rows[0].shape[0]
    specs = [_row_spec(tm, w, c) for (w, c) in row_cols] + [_par_spec(p.shape) for p in pars]
    return map_fwd(name, f, (t // tm,), list(rows) + list(pars), specs, [jax.ShapeDtypeStruct((t, out_width), f32)],
                   [_row_spec(tm, out_width)])[0]


def rowwise_bwd(name, f, rows, row_cols, pars, dy, tm=256):
    t = rows[0].shape[0]
    n_r, n_p = len(rows), len(pars)
    specs = [_row_spec(tm, w, c) for (w, c) in row_cols] + [_par_spec(p.shape) for p in pars]
    out_w = dy.shape[1]

    def body(*refs):
        vals = [r[...] for r in refs[:n_r + n_p]]
        dy_ref = refs[n_r + n_p]
        outs = refs[n_r + n_p + 1:]
        _, vjp = jax.vjp(lambda *v: f(None, *v)[0], *vals)
        grads = vjp(dy_ref[...])
        for k in range(n_r):
            outs[k][...] = grads[k]

        @pl.when(pl.program_id(0) == 0)
        def _():
            for k in range(n_p):
                outs[n_r + k][...] = jnp.zeros_like(outs[n_r + k])
        for k in range(n_p):
            outs[n_r + k][...] += grads[n_r + k]

    out_shapes = [jax.ShapeDtypeStruct((t, w), f32) for (w, c) in row_cols] + [jax.ShapeDtypeStruct(p.shape, f32) for p in pars]
    out_specs = [_row_spec(tm, w) for (w, c) in row_cols] + [_par_spec(p.shape) for p in pars]
    return pl.pallas_call(body, name=name, grid=(t // tm,), in_specs=specs + [_row_spec(tm, out_w)], out_specs=out_specs,
                          out_shape=out_shapes, compiler_params=_cp(1))(*rows, *pars, dy)


def f_ml(pids, states, q, k, v, g1, g2, g3, b_if):
    h = pids[2]
    cst, nst, mst = states
    l = q.shape[0]
    gt = g1 + g2 + g3 + b_if
    k = k * (1.0 / math.sqrt(ML_HD))
    tri = _tri(l)
    bc_all = hdot(tri.astype(f32), jax.nn.log_sigmoid(gt))
    bcum, ig = _lane_pick(bc_all, ML_HEADS + h), _lane_pick(gt, h)
    bcum_t, ig_t = _row_pick(bc_all.T, ML_HEADS + h), _row_pick(gt.T, h)
    b_last = bcum[l - 1:l, :]
    dlog = jnp.where(tri, bcum - bcum_t + ig_t, NEG)
    ws = b_last - bcum + ig
    m_prev = mst[:, 0:1]
    m_new = lax.stop_gradient(jnp.maximum(b_last + m_prev, jnp.max(ws, axis=0, keepdims=True)))
    decay = jnp.exp(b_last + m_prev - m_new)
    wts = jnp.exp(ws - m_new)
    c_new = decay * cst + bdot(wts * v, k, 0, 0)
    n_new = decay * nst + jnp.sum(wts * k, axis=0, keepdims=True)
    m_inter = bcum + m_prev
    m_t = lax.stop_gradient(jnp.maximum(jnp.max(dlog, axis=1, keepdims=True), m_inter))
    scores = bdot(q, k, 1, 1) * jnp.exp(dlog - m_t)
    inter_w = jnp.exp(m_inter - m_t)
    num = bdot(scores, v, 1, 0) + inter_w * bdot(q, cst, 1, 1)
    den = jnp.sum(scores, axis=1, keepdims=True) + inter_w * jnp.sum(q * nst, axis=1, keepdims=True)
    hout = num / jnp.maximum(jnp.abs(den), jnp.exp(-m_t))
    return (hout,), (c_new, n_new, jnp.broadcast_to(m_new, mst.shape))


def _ml_specs(nch, rev):
    cc = (lambda c: nch - 1 - c) if rev else (lambda c: c)
    hd = pl.BlockSpec((CHUNK, ML_HD), lambda b, c, h: (b * nch + cc(c), h))
    gt = pl.BlockSpec((CHUNK, LANES), lambda b, c, h: (b * nch + cc(c), 0))
    par = pl.BlockSpec((1, LANES), lambda b, c, h: (0, 0))
    sc = pl.BlockSpec((1, 1, ML_HD, ML_HD), lambda b, c, h: (b * nch + cc(c), h, 0, 0))
    sn = pl.BlockSpec((1, 1, 1, ML_HD), lambda b, c, h: (b * nch + cc(c), h, 0, 0))
    sm = pl.BlockSpec((1, 1, 1, LANES), lambda b, c, h: (b * nch + cc(c), h, 0, 0))
    return hd, gt, par, sc, sn, sm


ML_STATE_SHAPES = [(ML_HD, ML_HD), (1, ML_HD), (1, LANES)]


def ml_fwd(q, k, v, g1, g2, g3, b_if, seq):
    t = q.shape[0]
    nch = seq // CHUNK
    hd, gt, par, sc, sn, sm = _ml_specs(nch, False)
    nc = t // CHUNK
    outs = scan_fwd("ml_fwd", f_ml, (t // seq, nch, ML_HEADS), 2, [q, k, v, g1, g2, g3, b_if],
                    [hd, hd, hd, gt, gt, gt, par], [jax.ShapeDtypeStruct((t, ML_HEADS * ML_HD), f32)], [hd],
                    ML_STATE_SHAPES, [0.0, 0.0, NEG],
                    [jax.ShapeDtypeStruct((nc, ML_HEADS, ML_HD, ML_HD), f32), jax.ShapeDtypeStruct((nc, ML_HEADS, 1, ML_HD), f32),
                     jax.ShapeDtypeStruct((nc, ML_HEADS, 1, LANES), f32)], [sc, sn, sm])
    return outs[0], outs[1:]


def ml_bwd(dh, q, k, v, g1, g2, g3, b_if, saves, seq):
    t = q.shape[0]
    nch = seq // CHUNK
    hd, gt, par, sc, sn, sm = _ml_specs(nch, True)

    def f(pids, states, q, k, v, gsum, b_if):
        return f_ml(pids, states, q, k, v, gsum, jnp.zeros_like(gsum), jnp.zeros_like(gsum), b_if)

    def body(q_ref, k_ref, v_ref, g1_ref, g2_ref, g3_ref, b_ref, c_ref, n_ref, m_ref, dh_ref,
             dq_ref, dk_ref, dv_ref, dg_ref, db_ref, dc_s, dn_s):
        pids = tuple(pl.program_id(i) for i in range(3))
        slot = pids[2]

        @pl.when(pids[1] == 0)
        def _():
            dc_s[slot] = jnp.zeros(dc_s.shape[1:], f32)
            dn_s[slot] = jnp.zeros(dn_s.shape[1:], f32)

        gsum = g1_ref[...] + g2_ref[...] + g3_ref[...]
        mst = m_ref[0, 0]

        def gfun(cst, nst, qq, kk, vv, gs, bb):
            outs, new = f(pids, (cst, nst, mst), qq, kk, vv, gs, bb)
            return outs[0], new[0], new[1]

        _, vjp = jax.vjp(gfun, c_ref[0, 0], n_ref[0, 0], q_ref[...], k_ref[...], v_ref[...], gsum, b_ref[...])
        grads = vjp((dh_ref[...], dc_s[slot], dn_s[slot]))
        dc_s[slot] = grads[0]
        dn_s[slot] = grads[1]
        dq_ref[...] = grads[2]
        dk_ref[...] = grads[3]
        dv_ref[...] = grads[4]

        @pl.when(slot == 0)
        def _():
            dg_ref[...] = jnp.zeros_like(dg_ref)
        dg_ref[...] += grads[5]
        first = jnp.logical_and(jnp.logical_and(pids[0] == 0, pids[1] == 0), slot == 0)

        @pl.when(first)
        def _():
            db_ref[...] = jnp.zeros_like(db_ref)
        db_ref[...] += grads[6]

    big = jax.ShapeDtypeStruct((t, ML_HEADS * ML_HD), f32)
    return pl.pallas_call(body, name="ml_bwd", grid=(t // seq, nch, ML_HEADS),
                          in_specs=[hd, hd, hd, gt, gt, gt, par, sc, sn, sm, hd], out_specs=[hd, hd, hd, gt, par],
                          out_shape=[big, big, big, jax.ShapeDtypeStruct((t, LANES), f32), jax.ShapeDtypeStruct((1, LANES), f32)],
                          scratch_shapes=[pltpu.VMEM((ML_HEADS, ML_HD, ML_HD), f32), pltpu.VMEM((ML_HEADS, 1, ML_HD), f32)],
                          compiler_params=_cp(3))(q, k, v, g1, g2, g3, b_if, *saves, dh)


def f_s5(pids, states, u, bb, cc, pw):
    (carry,) = states
    tl = u.shape[0]
    bu = bdot(u, bb, 1, 0)
    xr, xi = bu[:, :S5_CH], bu[:, S5_CH:]
    sh = 1
    while sh < tl:
        pr, pi = pw[sh - 1:sh, :S5_CH], pw[sh - 1:sh, S5_CH:]
        sr, si = tshift(xr, sh), tshift(xi, sh)
        xr, xi = xr + pr * sr - pi * si, xi + pr * si + pi * sr
        sh *= 2
    cr, ci = carry[:, :S5_CH], carry[:, S5_CH:]
    pwr, pwi = pw[:, :S5_CH], pw[:, S5_CH:]
    xr, xi = xr + pwr * cr - pwi * ci, xi + pwr * ci + pwi * cr
    x = jnp.concatenate([xr, xi], axis=1)
    y = bdot(x, cc, 1, 0)
    return (y,), (x[tl - 1:tl, :],)


def _s5_specs(ntl, rev):
    tt = (lambda t: ntl - 1 - t) if rev else (lambda t: t)
    us = pl.BlockSpec((S5_TL, LANES), lambda c, b, t: (b * ntl + tt(t), c))
    bbs = pl.BlockSpec((1, LANES, 2 * S5_CH), lambda c, b, t: (c, 0, 0))
    ccs = pl.BlockSpec((1, 2 * S5_CH, LANES), lambda c, b, t: (c, 0, 0))
    pws = pl.BlockSpec((1, S5_TL, 2 * S5_CH), lambda c, b, t: (c, 0, 0))
    sv = pl.BlockSpec((1, 1, 1, 2 * S5_CH), lambda c, b, t: (b * ntl + tt(t), c, 0, 0))
    return us, bbs, ccs, pws, sv


def s5_fwd(u, bb, cc, pw, seq):
    t = u.shape[0]
    ntl = seq // S5_TL
    us, bbs, ccs, pws, sv = _s5_specs(ntl, False)

    def f(pids, states, uu, b3, c3, p3):
        return f_s5(pids, states, uu, b3[0], c3[0], p3[0])

    y, carries = scan_fwd("s5_fwd", f, (S5_CB, t // seq, ntl), 0, [u, bb, cc, pw], [us, bbs, ccs, pws],
                          [jax.ShapeDtypeStruct((t, S5_CB * LANES), f32)], [us], [(1, 2 * S5_CH)], [0.0],
                          [jax.ShapeDtypeStruct((t // S5_TL, S5_CB, 1, 2 * S5_CH), f32)], [sv])
    return y, carries


def s5_bwd(dy, u, bb, cc, pw, carries, seq):
    t = u.shape[0]
    ntl = seq // S5_TL
    us, bbs, ccs, pws, sv = _s5_specs(ntl, True)

    def f(pids, states, uu, b3, c3, p3):
        return f_s5(pids, states, uu, b3[0], c3[0], p3[0])

    first = lambda pids: jnp.logical_and(pids[1] == 0, pids[2] == 0)
    return scan_bwd("s5_bwd", f, (S5_CB, t // seq, ntl), 0, [u, bb, cc, pw], [us, bbs, ccs, pws], [carries], [sv],
                    [dy], [us], [(1, 2 * S5_CH)], [0, 1, 2, 3], {1: first, 2: first, 3: first})


def _adam_math(g, w, m, v):
    m2 = ADAM_B1 * m + (1.0 - ADAM_B1) * g
    v2 = ADAM_B2 * v + (1.0 - ADAM_B2) * jnp.square(g)
    m_hat = m2 / (1.0 - ADAM_B1 ** ADAM_STEP)
    v_hat = v2 / (1.0 - ADAM_B2 ** ADAM_STEP)
    delta = -ADAM_LR * (m_hat / (jnp.sqrt(v_hat) + ADAM_EPS) + ADAM_WD * w)
    return delta, m2, v2


def adamw(name, parts, w, m, v, tr=256):
    n, r, c = parts.shape
    tr = min(tr, r)
    assert r % tr == 0

    def body(p_ref, w_ref, m_ref, v_ref, g_ref, d_ref, m2_ref, v2_ref):
        g = p_ref[0].astype(f32)
        for s in range(1, n):
            g = g + p_ref[s].astype(f32)
        d, m2, v2 = _adam_math(g, w_ref[...], m_ref[...], v_ref[...])
        g_ref[...] = g
        d_ref[...] = d
        m2_ref[...] = m2
        v2_ref[...] = v2

    ps = pl.BlockSpec((n, tr, c), lambda i: (0, i, 0))
    rs = pl.BlockSpec((tr, c), lambda i: (i, 0))
    return pl.pallas_call(body, name=name, grid=(r // tr,), in_specs=[ps, rs, rs, rs], out_specs=[rs] * 4,
                          out_shape=[jax.ShapeDtypeStruct((r, c), f32)] * 4, compiler_params=_cp(1))(parts, w, m, v)


def sum_parts(name, parts, tr=256):
    n, r, c = parts.shape
    tr = min(tr, r)
    assert r % tr == 0

    def body(p_ref, o_ref):
        g = p_ref[0].astype(f32)
        for s in range(1, n):
            g = g + p_ref[s].astype(f32)
        o_ref[...] = g

    return pl.pallas_call(body, name=name, grid=(r // tr,), in_specs=[pl.BlockSpec((n, tr, c), lambda i: (0, i, 0))],
                          out_specs=pl.BlockSpec((tr, c), lambda i: (i, 0)),
                          out_shape=jax.ShapeDtypeStruct((r, c), f32), compiler_params=_cp(1))(parts)


def exchange(name, ops):
    n = len(ops)

    def body(*refs):
        ins, outs = refs[:n], refs[n:2 * n]
        send_sems, recv_sems, loc_sems = refs[2 * n:]
        x, y, c = lax.axis_index("x"), lax.axis_index("y"), lax.axis_index("c")
        me = 4 * x + 2 * y + c
        copies = []
        for k, (_, mode) in enumerate(ops):
            src_me = ins[k] if mode == "gather" else ins[k].at[me]
            loc = pltpu.make_async_copy(src_me, outs[k].at[me], loc_sems.at[k])
            loc.start()
            copies.append(loc)
            for d in range(1, N_DEV):
                dx, dy, dc = (d >> 2) & 1, (d >> 1) & 1, d & 1
                px = 1 - x if dx else x
                py = 1 - y if dy else y
                pc = 1 - c if dc else c
                src = ins[k] if mode == "gather" else ins[k].at[4 * px + 2 * py + pc]
                cp = pltpu.make_async_remote_copy(src_ref=src, dst_ref=outs[k].at[me], send_sem=send_sems.at[k, d - 1],
                                                  recv_sem=recv_sems.at[k, d - 1], device_id=(px, py, pc),
                                                  device_id_type=pl.DeviceIdType.MESH)
                cp.start()
                copies.append(cp)
        for cp in copies:
            cp.wait()

    out_shapes = []
    for a, mode in ops:
        shp = (N_DEV,) + tuple(a.shape) if mode == "gather" else tuple(a.shape)
        out_shapes.append(jax.ShapeDtypeStruct(shp, a.dtype))
    anyspec = pl.BlockSpec(memory_space=pl.ANY)
    return pl.pallas_call(body, name=name, in_specs=[anyspec] * n, out_specs=[anyspec] * n, out_shape=out_shapes,
                          scratch_shapes=[pltpu.SemaphoreType.DMA((n, N_DEV - 1)), pltpu.SemaphoreType.DMA((n, N_DEV - 1)),
                                          pltpu.SemaphoreType.DMA((n,))])(*[a for a, _ in ops])


def _lanes(v, width=LANES):
    v = v.reshape(1, -1)
    return jnp.pad(v, ((0, 0), (0, width - v.shape[1])))


def win_to_padded(w):
    return jnp.concatenate([w[:, :1024], w[:, 2576:3600], w[:, 3600:4624], w[:, 1024:2560], w[:, 2560:2576],
                            jnp.zeros((w.shape[0], PROJ_W - IN_COLS), w.dtype)], axis=1)


def win_from_padded(wp):
    return jnp.concatenate([wp[:, 0:1024], wp[:, 3072:4608], wp[:, 4608:4624], wp[:, 1024:2048], wp[:, 2048:3072]], axis=1)


def headwise_dense(w):
    nb, o, i = w.shape
    return jnp.einsum("noi,nm->nimo", w, jnp.eye(nb, dtype=w.dtype)).reshape(nb * i, nb * o)


def headwise_from_dense(dd, o=4, i=4):
    nb = dd.shape[0] // i
    return jnp.diagonal(dd.reshape(nb, i, nb, o), axis1=0, axis2=2).transpose(2, 1, 0)


def s5_tables(a_re, a_im, log_step, b_re, b_im, c_re, c_im):
    step = jnp.exp(log_step)[:, None]
    r = jnp.arange(1, S5_TL + 1, dtype=f32)[:, None, None]
    mag = jnp.exp(r * (a_re * step))
    pw_re, pw_im = mag * jnp.cos(r * (a_im * step)), mag * jnp.sin(r * (a_im * step))
    lam_re, lam_im = pw_re[0], pw_im[0]
    den = a_re * a_re + a_im * a_im
    coef_re = ((lam_re - 1.0) * a_re + lam_im * a_im) / den
    coef_im = (lam_im * a_re - (lam_re - 1.0) * a_im) / den
    bb_re = coef_re[..., None] * b_re - coef_im[..., None] * b_im
    bb_im = coef_re[..., None] * b_im + coef_im[..., None] * b_re
    gl = S5_GROUPS // S5_CB
    eye = jnp.eye(gl, dtype=f32)

    def blk_b(t):
        t4 = t.transpose(0, 2, 1).reshape(S5_CB, gl, S5_GROUP, S5_STATE)
        return jnp.einsum("kgcn,gh->kgchn", t4, eye).reshape(S5_CB, gl * S5_GROUP, gl * S5_STATE)

    def blk_c(t):
        t4 = t.reshape(S5_CB, gl, S5_GROUP, S5_STATE)
        return jnp.einsum("kgcn,gh->kgnhc", t4, eye).reshape(S5_CB, gl * S5_STATE, gl * S5_GROUP)

    def blk_p(t):
        return t.reshape(S5_TL, S5_CB, gl * S5_STATE).transpose(1, 0, 2)

    bb = jnp.concatenate([blk_b(bb_re), blk_b(bb_im)], axis=2)
    cc = jnp.concatenate([blk_c(c_re), -blk_c(c_im)], axis=1)
    pw = jnp.concatenate([blk_p(pw_re), blk_p(pw_im)], axis=2)
    return bb, cc, pw


def ffn_step_bwd(dy, x, nw, saved, wg, wu, wd):
    h, g, u = saved
    dx, dnw, dg, du, a = ffn_bwd_act(dy, x, nw, g, u, wg, wu, wd)
    dwg, dwu, dwd = ffn_bwd_w(h, dy, dg, du, a)
    return dx, dnw, dwg, dwu, dwd


def hybrid_fwd(x1, p, seq):
    u = rms_fwd(x1, p["mix_norm"])
    proj = matmul("hy_in", u, p["win"], tn=256)
    xbc = conv_fwd("ssd_conv", proj, OFF_XBC, p["ssd_conv_w"], p["ssd_conv_b"], seq)
    yraw, hsave = ssd_fwd(xbc, proj, p["a_log"], p["dt_bias"], p["ssd_d"], seq)
    yssd = rowwise_fwd("ssd_epi", f_ssd_epi, [yraw, proj], [(D_MODEL, 0), (D_MODEL, OFF_Z // D_MODEL)], [p["ssd_norm_w"]], D_MODEL)
    xc = conv_fwd("ml_conv", proj, OFF_MX, p["ml_conv_w"], p["ml_conv_b"], seq)
    q = matmul("hw_q", xc, p["wq"])
    k = matmul("hw_k", xc, p["wk"])
    v = matmul("hw_v", proj, p["wv"], a_off=OFF_MX, a_width=D_MODEL)
    g1 = matmul("gate_q", q, p["wif_q"])
    g2 = matmul("gate_k", k, p["wif_k"])
    g3 = matmul("gate_v", v, p["wif_v"])
    hm, mlsave = ml_fwd(q, k, v, g1, g2, g3, p["b_if"], seq)
    yml = rowwise_fwd("ml_epi", f_ml_epi, [hm, xc, proj], [(D_MODEL, 0), (D_MODEL, 0), (D_MODEL, OFF_MZ // D_MODEL)],
                      [p["ml_norm_w"], p["ml_skip"]], D_MODEL)
    t = matmul("hy_out1", yssd, p["wo1"], add=x1)
    x2 = matmul("hy_out2", yml, p["wo2"], add=t)
    return x2, (u, proj, xbc, yraw, hsave, yssd, xc, q, k, v, g1, g2, g3, hm, mlsave, yml)


def hybrid_bwd(dx2, x1, p, saved, seq):
    u, proj, xbc, yraw, hsave, yssd, xc, q, k, v, g1, g2, g3, hm, mlsave, yml = saved
    gr = {}
    dyssd = matmul("d_yssd", dx2, p["wo1"], cb=1)
    dyml = matmul("d_yml", dx2, p["wo2"], cb=1)
    gr["wo"] = jnp.concatenate([matmul("dw_o1", yssd, dx2, ca=0), matmul("dw_o2", yml, dx2, ca=0)], axis=0)
    d_hm, d_xc, d_mz, gr["ml_norm_w"], gr["ml_skip"] = rowwise_bwd(
        "ml_epi_bwd", f_ml_epi, [hm, xc, proj], [(D_MODEL, 0), (D_MODEL, 0), (D_MODEL, OFF_MZ // D_MODEL)],
        [p["ml_norm_w"], p["ml_skip"]], dyml)
    dq, dk, dv, dgt, gr["b_if"] = ml_bwd(d_hm, q, k, v, g1, g2, g3, p["b_if"], mlsave, seq)
    dq = matmul("dq_gate", dgt, p["wif_q"], cb=1, add=dq)
    dk = matmul("dk_gate", dgt, p["wif_k"], cb=1, add=dk)
    dv = matmul("dv_gate", dgt, p["wif_v"], cb=1, add=dv)
    gr["wif"] = jnp.concatenate([matmul("dw_if_q", q, dgt, ca=0), matmul("dw_if_k", k, dgt, ca=0),
                                 matmul("dw_if_v", v, dgt, ca=0)], axis=0)
    d_xc = matmul("dxc_q", dq, p["wq"], cb=1, add=d_xc)
    d_xc = matmul("dxc_k", dk, p["wk"], cb=1, add=d_xc)
    gr["wq"] = matmul("dw_q", xc, dq, ca=0)
    gr["wk"] = matmul("dw_k", xc, dk, ca=0)
    gr["wv"] = matmul("dw_v", proj, dv, ca=0, a_off=OFF_MX, a_width=D_MODEL)
    d_mx, gr["ml_conv_w"], gr["ml_conv_b"] = conv_bwd("ml_conv_bwd", d_xc, proj, OFF_MX, p["ml_conv_w"], p["ml_conv_b"], seq)
    d_mx = matmul("dmx_v", dv, p["wv"], cb=1, add=d_mx)
    d_yraw, d_z, gr["ssd_norm_w"] = rowwise_bwd("ssd_epi_bwd", f_ssd_epi, [yraw, proj],
                                                [(D_MODEL, 0), (D_MODEL, OFF_Z // D_MODEL)], [p["ssd_norm_w"]], dyssd)
    d_xs, d_b, d_c, d_dt, gr["a_log"], gr["dt_bias"], gr["ssd_d"] = ssd_bwd(
        d_yraw, xbc, proj, p["a_log"], p["dt_bias"], p["ssd_d"], hsave, seq)
    d_xbc, gr["ssd_conv_w"], gr["ssd_conv_b"] = conv_bwd("ssd_conv_bwd", jnp.concatenate([d_xs, d_b, d_c], axis=1), proj, OFF_XBC,
                                                         p["ssd_conv_w"], p["ssd_conv_b"], seq)
    dproj = jnp.concatenate([d_z, d_mx, d_mz, d_xbc, d_dt, jnp.zeros((d_dt.shape[0], PROJ_W - OFF_DT - LANES), f32)], axis=1)
    du = matmul("d_u", dproj, p["win"], cb=1, tk=256)
    gr["win"] = matmul("dw_in", u, dproj, ca=0, tn=256)
    dx1, gr["mix_norm"] = rms_bwd([du], x1, p["mix_norm"], dx2)
    return dx1, gr


def s5_layer_fwd(x4, p, seq):
    u = rms_fwd(x4, p["mix_norm"])
    ys, carries = s5_fwd(u, p["bb"], p["cc"], p["pw"], seq)
    gg = rowwise_fwd("s5_post", f_s5_post, [ys, u], [(D_MODEL, 0), (D_MODEL, 0)], [p["s5_d"]], D_MODEL)
    pab = matmul("s5_ab", gg, p["wab"])
    x5 = rowwise_fwd("s5_glu", f_glu_res, [pab, x4], [(2 * D_MODEL, 0), (D_MODEL, 0)], [p["b_a"], p["b_b"]], D_MODEL)
    return x5, (u, ys, carries, gg, pab)


def s5_layer_bwd(dx5, x4, p, saved, seq):
    u, ys, carries, gg, pab = saved
    gr = {}
    dpab, gr["b_a"], gr["b_b"] = rowwise_bwd("s5_glu_bwd", f_glu, [pab], [(2 * D_MODEL, 0)], [p["b_a"], p["b_b"]], dx5)
    dgg = matmul("d_gg", dpab, p["wab"], cb=1)
    gr["wab"] = matmul("dw_ab", gg, dpab, ca=0)
    dys, du_a, gr["s5_d"] = rowwise_bwd("s5_post_bwd", f_s5_post, [ys, u], [(D_MODEL, 0), (D_MODEL, 0)], [p["s5_d"]], dgg)
    du_b, gr["bb"], gr["cc"], gr["pw"] = s5_bwd(dys, u, p["bb"], p["cc"], p["pw"], carries, seq)
    dx4, gr["mix_norm"] = rms_bwd([du_a, du_b], x4, p["mix_norm"], dx5)
    return dx4, gr


def local_step(x, tgt, w0, w1, final_norm, seq):
    f = [None] * 4
    x1, f[0] = _ffn(x, w0["f1"])
    x2, sv_h = hybrid_fwd(x1, w0, seq)
    x3, f[1] = _ffn(x2, w0["f2"])
    x4, f[2] = _ffn(x3, w1["f1"])
    x5, sv_s = s5_layer_fwd(x4, w1, seq)
    x6, f[3] = _ffn(x5, w1["f2"])
    loss, dx6, d_final = loss_head(x6, final_norm, tgt)
    dx5, *g_f3 = ffn_step_bwd(dx6, x5, w1["f2"]["nw"], f[3], w1["f2"]["wg"], w1["f2"]["wu"], w1["f2"]["wd"])
    dx4, g_s5 = s5_layer_bwd(dx5, x4, w1, sv_s, seq)
    dx3, *g_f2 = ffn_step_bwd(dx4, x3, w1["f1"]["nw"], f[2], w1["f1"]["wg"], w1["f1"]["wu"], w1["f1"]["wd"])
    dx2, *g_f1 = ffn_step_bwd(dx3, x2, w0["f2"]["nw"], f[1], w0["f2"]["wg"], w0["f2"]["wu"], w0["f2"]["wd"])
    dx1, g_hy = hybrid_bwd(dx2, x1, w0, sv_h, seq)
    dx0, *g_f0 = ffn_step_bwd(dx1, x, w0["f1"]["nw"], f[0], w0["f1"]["wg"], w0["f1"]["wu"], w0["f1"]["wd"])
    return loss, dx0, d_final, [g_f0, g_f1, g_f2, g_f3], g_hy, g_s5


def _ffn(x, fw):
    xo, h, g, u = ffn_fwd(x, fw["nw"], fw["wg"], fw["wu"], fw["wd"])
    return xo, (h, g, u)


BIG = ["ffn1_w_gate", "ffn1_w_up", "ffn1_w_down", "ffn2_w_gate", "ffn2_w_up", "ffn2_w_down", "hy_w_in", "hy_w_out", "s5_w_a", "s5_w_b"]
SMALL_SHARDED = {"ssd_conv_w": 2, "ml_conv_w": 2, "ml_w_q": 1, "ml_w_k": 1, "ml_w_v": 1, "ml_w_if": 1, "s5_d": 1, "s5_b_a": 1, "s5_b_b": 1}
WEIGHTS = ["ffn1_norm", "ffn1_w_gate", "ffn1_w_up", "ffn1_w_down", "mix_norm", "ffn2_norm", "ffn2_w_gate", "ffn2_w_up", "ffn2_w_down",
           "hy_w_in", "ssd_conv_w", "ssd_conv_b", "ssd_dt_bias", "ssd_a_log", "ssd_d", "ssd_norm_w", "ml_conv_w", "ml_conv_b",
           "ml_w_q", "ml_w_k", "ml_w_v", "ml_w_if", "ml_b_if", "ml_norm_w", "ml_skip", "hy_w_out", "s5_a_re", "s5_a_im",
           "s5_log_step", "s5_b_re", "s5_b_im", "s5_c_re", "s5_c_im", "s5_d", "s5_w_a", "s5_b_a", "s5_w_b", "s5_b_b", "final_norm"]
SMALL = [n for n in WEIGHTS if n not in BIG]
S5_PARAMS = ["s5_a_re", "s5_a_im", "s5_log_step", "s5_b_re", "s5_b_im", "s5_c_re", "s5_c_im"]


def _unshard(g, axis):
    return jnp.concatenate([g[i] for i in range(N_DEV)], axis=axis)


def assemble(gw, rep):
    def ffn(pre, l):
        return dict(nw=rep[pre + "_norm"][l:l + 1], wg=gw[pre + "_w_gate"][:, l].astype(bf16),
                    wu=gw[pre + "_w_up"][:, l].astype(bf16), wd=gw[pre + "_w_down"][:, l].astype(bf16))

    padn = lambda w: jnp.pad(w, ((0, 0), (0, LANES - w.shape[1]))).astype(bf16)
    wif = _unshard(gw["ml_w_if"], 1)[0]
    wo = _unshard(gw["hy_w_out"], 1)[0].astype(bf16)
    dense = lambda n: headwise_dense(_unshard(gw[n], 1)[0].astype(f32)).astype(bf16)
    w0 = dict(f1=ffn("ffn1", 0), f2=ffn("ffn2", 0), mix_norm=rep["mix_norm"][0:1],
              win=win_to_padded(_unshard(gw["hy_w_in"], 2)[0]).astype(bf16),
              ssd_conv_w=_unshard(gw["ssd_conv_w"], 2)[0], ssd_conv_b=rep["ssd_conv_b"],
              a_log=_lanes(rep["ssd_a_log"]), dt_bias=_lanes(rep["ssd_dt_bias"]), ssd_d=_lanes(rep["ssd_d"]),
              ssd_norm_w=rep["ssd_norm_w"], ml_conv_w=_unshard(gw["ml_conv_w"], 2)[0], ml_conv_b=rep["ml_conv_b"],
              wq=dense("ml_w_q"), wk=dense("ml_w_k"), wv=dense("ml_w_v"),
              wif_q=padn(wif[0:1024]), wif_k=padn(wif[1024:2048]), wif_v=padn(wif[2048:3072]),
              b_if=_lanes(rep["ml_b_if"]), ml_norm_w=rep["ml_norm_w"], ml_skip=rep["ml_skip"],
              wo1=wo[:D_MODEL], wo2=wo[D_MODEL:])
    bb, cc, pw = s5_tables(*[rep[n][0] for n in S5_PARAMS])
    wab = jnp.concatenate([_unshard(gw["s5_w_a"], 1)[0], _unshard(gw["s5_w_b"], 1)[0]], axis=1).astype(bf16)
    w1 = dict(f1=ffn("ffn1", 1), f2=ffn("ffn2", 1), mix_norm=rep["mix_norm"][1:2], bb=bb, cc=cc, pw=pw,
              s5_d=_unshard(gw["s5_d"], 1), wab=wab, b_a=_unshard(gw["s5_b_a"], 1), b_b=_unshard(gw["s5_b_b"], 1))
    return w0, w1


def _shards(full, axis):
    return jnp.stack(jnp.split(full, N_DEV, axis=axis), axis=0)


def disassemble(g_ffn, g_hy, g_s5, d_final, rep):
    big, small = {}, {}
    for pre, (a, b) in (("ffn1", (0, 2)), ("ffn2", (1, 3))):
        small[pre + "_norm"] = jnp.concatenate([g_ffn[a][0], g_ffn[b][0]], axis=0)
        for k, nm in ((1, "_w_gate"), (2, "_w_up"), (3, "_w_down")):
            big[pre + nm] = jnp.stack([g_ffn[a][k], g_ffn[b][k]], axis=1)
    big["hy_w_in"] = _shards(win_from_padded(g_hy["win"])[None], 2)
    big["hy_w_out"] = _shards(g_hy["wo"][None], 1)
    big["s5_w_a"] = _shards(g_s5["wab"][None, :, :D_MODEL], 1)
    big["s5_w_b"] = _shards(g_s5["wab"][None, :, D_MODEL:], 1)
    small["mix_norm"] = jnp.concatenate([g_hy["mix_norm"], g_s5["mix_norm"]], axis=0)
    small["ssd_conv_w"] = g_hy["ssd_conv_w"][None]
    small["ssd_conv_b"] = g_hy["ssd_conv_b"]
    small["ssd_dt_bias"] = g_hy["dt_bias"][:, :SSD_HEADS]
    small["ssd_a_log"] = g_hy["a_log"][:, :SSD_HEADS]
    small["ssd_d"] = g_hy["ssd_d"][:, :SSD_HEADS]
    small["ssd_norm_w"] = g_hy["ssd_norm_w"]
    small["ml_conv_w"] = g_hy["ml_conv_w"][None]
    small["ml_conv_b"] = g_hy["ml_conv_b"]
    for nm, key in (("ml_w_q", "wq"), ("ml_w_k", "wk"), ("ml_w_v", "wv")):
        small[nm] = headwise_from_dense(g_hy[key])[None]
    small["ml_w_if"] = g_hy["wif"][None, :, :2 * ML_HEADS]
    small["ml_b_if"] = g_hy["b_if"][:, :2 * ML_HEADS]
    small["ml_norm_w"] = g_hy["ml_norm_w"]
    small["ml_skip"] = g_hy["ml_skip"]
    _, tvjp = jax.vjp(s5_tables, *[rep[n][0] for n in S5_PARAMS])
    for n, g in zip(S5_PARAMS, tvjp((g_s5["bb"], g_s5["cc"], g_s5["pw"]))):
        small[n] = g[None]
    small["s5_d"] = g_s5["s5_d"]
    small["s5_b_a"] = g_s5["b_a"]
    small["s5_b_b"] = g_s5["b_b"]
    small["final_norm"] = d_final.reshape(-1)
    return big, small


ROW = 1024
BF16_ROWS = 16
F32_ROWS = 8


def _to_rows(flat):
    n = flat.shape[-1]
    pad = (-n) % ROW
    if pad:
        flat = jnp.pad(flat, [(0, 0)] * (flat.ndim - 1) + [(0, pad)])
    return flat.reshape(flat.shape[:-1] + ((n + pad) // ROW, ROW))


def _pack(pieces, row_mult):
    rows = sum(p.shape[-2] for p in pieces)
    pad = (-rows) % row_mult
    if pad:
        pieces = list(pieces) + [jnp.zeros(pieces[0].shape[:-2] + (pad, ROW), pieces[0].dtype)]
    return jnp.concatenate(pieces, axis=-2)


def _f32_as_bf16_rows(a):
    return _to_rows(lax.bitcast_convert_type(a.reshape(-1), bf16).reshape(-1))


def _bf16_rows_as_f32(rows, n):
    lead = rows.shape[:-2]
    pairs = rows.reshape(lead + (-1, 2))
    return lax.bitcast_convert_type(pairs, f32)[..., :n]


def _tile_rows(r):
    for t in (512, 256, 128, 64, 32, 16, 8):
        if r % t == 0:
            return t
    return r


def _flat2d(a):
    return a.reshape(-1, a.shape[-1])


def kernel(x, ffn1_norm, ffn1_w_gate, ffn1_w_up, ffn1_w_down, mix_norm, ffn2_norm, ffn2_w_gate, ffn2_w_up, ffn2_w_down, hy_w_in, ssd_conv_w, ssd_conv_b, ssd_dt_bias, ssd_a_log, ssd_d, ssd_norm_w, ml_conv_w, ml_conv_b, ml_w_q, ml_w_k, ml_w_v, ml_w_if, ml_b_if, ml_norm_w, ml_skip, hy_w_out, s5_a_re, s5_a_im, s5_log_step, s5_b_re, s5_b_im, s5_c_re, s5_c_im, s5_d, s5_w_a, s5_b_a, s5_w_b, s5_b_b, final_norm, loss_target, m_ffn1_norm, m_ffn1_w_gate, m_ffn1_w_up, m_ffn1_w_down, m_mix_norm, m_ffn2_norm, m_ffn2_w_gate, m_ffn2_w_up, m_ffn2_w_down, m_hy_w_in, m_ssd_conv_w, m_ssd_conv_b, m_ssd_dt_bias, m_ssd_a_log, m_ssd_d, m_ssd_norm_w, m_ml_conv_w, m_ml_conv_b, m_ml_w_q, m_ml_w_k, m_ml_w_v, m_ml_w_if, m_ml_b_if, m_ml_norm_w, m_ml_skip, m_hy_w_out, m_s5_a_re, m_s5_a_im, m_s5_log_step, m_s5_b_re, m_s5_b_im, m_s5_c_re, m_s5_c_im, m_s5_d, m_s5_w_a, m_s5_b_a, m_s5_w_b, m_s5_b_b, m_final_norm, v_ffn1_norm, v_ffn1_w_gate, v_ffn1_w_up, v_ffn1_w_down, v_mix_norm, v_ffn2_norm, v_ffn2_w_gate, v_ffn2_w_up, v_ffn2_w_down, v_hy_w_in, v_ssd_conv_w, v_ssd_conv_b, v_ssd_dt_bias, v_ssd_a_log, v_ssd_d, v_ssd_norm_w, v_ml_conv_w, v_ml_conv_b, v_ml_w_q, v_ml_w_k, v_ml_w_v, v_ml_w_if, v_ml_b_if, v_ml_norm_w, v_ml_skip, v_hy_w_out, v_s5_a_re, v_s5_a_im, v_s5_log_step, v_s5_b_re, v_s5_b_im, v_s5_c_re, v_s5_c_im, v_s5_d, v_s5_w_a, v_s5_b_a, v_s5_w_b, v_s5_b_b, v_final_norm):
    given = dict(locals())
    w = {n: given[n] for n in WEIGHTS}
    mom = {n: given["m_" + n] for n in WEIGHTS}
    var = {n: given["v_" + n] for n in WEIGHTS}
    bl, seq, d = x.shape
    me = 4 * lax.axis_index("x") + 2 * lax.axis_index("y") + lax.axis_index("c")

    pieces = [_to_rows(w[n].astype(bf16).reshape(-1)) for n in BIG] + [_f32_as_bf16_rows(w[n]) for n in SMALL_SHARDED]
    rows = [p.shape[0] for p in pieces]
    (gathered,) = exchange("gather_weights", [(_pack(pieces, BF16_ROWS), "gather")])
    gw, r0 = {}, 0
    for n, r in zip(list(BIG) + list(SMALL_SHARDED), rows):
        part = gathered[:, r0:r0 + r]
        if n in BIG:
            gw[n] = part.reshape((N_DEV,) + w[n].shape)
        else:
            gw[n] = _bf16_rows_as_f32(part, w[n].size).reshape((N_DEV,) + w[n].shape)
        r0 += r
    rep = {n: w[n] for n in WEIGHTS if n not in BIG and n not in SMALL_SHARDED}
    w0, w1 = assemble(gw, rep)

    loss, dx0, d_final, g_ffn, g_hy, g_s5 = local_step(x.reshape(bl * seq, d), loss_target.reshape(bl * seq, d), w0, w1,
                                                       final_norm.reshape(1, d), seq)
    big, small = disassemble(g_ffn, g_hy, g_s5, d_final, rep)

    bpieces = [big[n].astype(bf16).reshape(N_DEV, -1, ROW) for n in BIG]
    brows = [p.shape[1] for p in bpieces]
    spieces = [_to_rows(small[n].astype(f32).reshape(-1)) for n in SMALL]
    srows = [p.shape[0] for p in spieces]
    big_parts, small_parts = exchange("reduce_grads", [(_pack(bpieces, BF16_ROWS), "scatter"), (_pack(spieces, F32_ROWS), "gather")])
    small_sum = sum_parts("sum_small", small_parts, tr=_tile_rows(small_parts.shape[1]))

    out_g, out_d, out_m, out_v = {}, {}, {}, {}
    r0 = 0
    for n, r in zip(BIG, brows):
        shp = w[n].shape
        parts = big_parts[:, r0:r0 + r].reshape((N_DEV,) + _flat2d(w[n]).shape)
        res = adamw("adamw_" + n, parts, _flat2d(w[n]), _flat2d(mom[n]), _flat2d(var[n]), tr=_tile_rows(_flat2d(w[n]).shape[0]))
        out_g[n], out_d[n], out_m[n], out_v[n] = [a.reshape(shp) for a in res]
        r0 += r
    g_small, r0 = {}, 0
    for n, r in zip(SMALL, srows):
        full = small_sum[r0:r0 + r].reshape(-1)[:small[n].size].reshape(small[n].shape)
        if n in SMALL_SHARDED:
            ax = SMALL_SHARDED[n]
            full = lax.dynamic_slice_in_dim(full, me * w[n].shape[ax], w[n].shape[ax], axis=ax)
        g_small[n] = full
        r0 += r
    packs = [_pack([_to_rows(t[n].reshape(-1)) for n in SMALL], F32_ROWS) for t in (g_small, w, mom, var)]
    res = adamw("adamw_small", packs[0][None], packs[1], packs[2], packs[3], tr=_tile_rows(packs[0].shape[0]))
    r0 = 0
    for n in SMALL:
        r = -(-w[n].size // ROW)
        for dst, a in zip((out_g, out_d, out_m, out_v), res):
            dst[n] = a[r0:r0 + r].reshape(-1)[:w[n].size].reshape(w[n].shape)
        r0 += r

    total = lax.psum(loss[0, 0], ("x", "y", "c"))
    return (total, dx0.reshape(bl, seq, d), *[out_g[n] for n in WEIGHTS], *[out_d[n] for n in WEIGHTS],
            *[out_m[n] for n in WEIGHTS], *[out_v[n] for n in WEIGHTS])
```

```python
import functools
import math

import jax
import jax.numpy as jnp
from jax import lax
from jax.experimental import pallas as pl
from jax.experimental.pallas import tpu as pltpu

f32 = jnp.float32
bf16 = jnp.bfloat16

N_DEV = 8
D_MODEL = 1024
D_FF = 2816
EPS = 1e-6
FFN_RES = 0.5
CONV_W = 4
SSD_HEADS = 16
SSD_HEAD_DIM = 64
SSD_GROUPS = 2
SSD_STATE = 128
SSD_HG = SSD_HEADS // SSD_GROUPS
SSD_GW = SSD_HG * SSD_HEAD_DIM
CHUNK = 128
ML_HEADS = 4
ML_HD = 256
S5_GROUP = 16
S5_GROUPS = 64
S5_STATE = 64
S5_CB = 8
S5_CH = (S5_GROUPS // S5_CB) * S5_STATE
S5_TL = 128
LANES = 128
IN_COLS = 4624
PROJ_W = 4864
OFF_Z, OFF_MX, OFF_MZ, OFF_XBC, OFF_DT = 0, 1024, 2048, 3072, 4608
ADAM_LR, ADAM_B1, ADAM_B2, ADAM_EPS, ADAM_WD, ADAM_STEP = 0.001, 0.9, 0.999, 1e-08, 0.01, 10
NEG = -1e30
VMEM_LIMIT = 56 * 1024 * 1024
HI = lax.Precision.HIGHEST


def _cp(n):
    return pltpu.CompilerParams(dimension_semantics=("arbitrary",) * n, vmem_limit_bytes=VMEM_LIMIT)


def _dg(a, b, ca, cb):
    return lax.dot_general(a.astype(bf16), b.astype(bf16), (((ca,), (cb,)), ((), ())), preferred_element_type=f32)


@functools.partial(jax.custom_vjp, nondiff_argnums=(2, 3))
def bdot(a, b, ca, cb):
    return _dg(a, b, ca, cb)


def _bdot_fwd(a, b, ca, cb):
    return _dg(a, b, ca, cb), (a, b)


def _bdot_bwd(ca, cb, res, ct):
    a, b = res
    da = _dg(ct, b, 1, 1 - cb) if ca == 1 else _dg(b, ct, 1 - cb, 1)
    db = _dg(a, ct, 1 - ca, 0) if cb == 0 else _dg(ct, a, 0, 1 - ca)
    return da, db


bdot.defvjp(_bdot_fwd, _bdot_bwd)


def hdot(a, b):
    return jnp.dot(a, b, precision=HI, preferred_element_type=f32)


def _iota(shape, dim):
    return lax.broadcasted_iota(jnp.int32, shape, dim)


def _tri(n):
    return (_iota((n, n), 0) >= _iota((n, n), 1))


@functools.partial(jax.custom_vjp, nondiff_argnums=(1,))
def tshift(x, k):
    return jnp.where(_iota(x.shape, 0) >= k, pltpu.roll(x, k, 0), 0.0)


def _tshift_fwd(x, k):
    return tshift(x, k), None


def _tshift_bwd(k, _, ct):
    n = ct.shape[0]
    return (jnp.where(_iota(ct.shape, 0) < n - k, pltpu.roll(ct, n - k, 0), 0.0),)


tshift.defvjp(_tshift_fwd, _tshift_bwd)


def _lane_pick(a, idx):
    return jnp.sum(jnp.where(_iota(a.shape, 1) == idx, a, 0.0), axis=1, keepdims=True)


def _row_pick(a, idx):
    return jnp.sum(jnp.where(_iota(a.shape, 0) == idx, a, 0.0), axis=0, keepdims=True)


def _silu(x):
    return x * jax.nn.sigmoid(x)


def map_fwd(name, f, grid, ins, in_specs, out_shapes, out_specs):
    n_in = len(ins)

    def body(*refs):
        pids = tuple(pl.program_id(i) for i in range(len(grid)))
        outs = f(pids, *[r[...] for r in refs[:n_in]])
        for r, o in zip(refs[n_in:], outs):
            r[...] = o.astype(r.dtype)

    return pl.pallas_call(body, name=name, grid=grid, in_specs=in_specs, out_specs=out_specs,
                          out_shape=out_shapes, compiler_params=_cp(len(grid)))(*ins)


def scan_fwd(name, f, grid, slot_axis, ins, in_specs, out_shapes, out_specs, state_shapes, state_init, save_shapes, save_specs):
    n_in, n_out, n_st = len(ins), len(out_shapes), len(state_shapes)
    n_slots = grid[slot_axis]
    cax = len(grid) - 1 if slot_axis != len(grid) - 1 else len(grid) - 2

    def body(*refs):
        pids = tuple(pl.program_id(i) for i in range(len(grid)))
        in_refs, out_refs = refs[:n_in], refs[n_in:n_in + n_out]
        save_refs = refs[n_in + n_out:n_in + n_out + n_st]
        st_refs = refs[n_in + n_out + n_st:]
        slot = pids[slot_axis]

        @pl.when(pids[cax] == 0)
        def _():
            for s, init in zip(st_refs, state_init):
                s[slot] = jnp.full(s.shape[1:], init, f32)

        states = tuple(s[slot] for s in st_refs)
        for sv, st in zip(save_refs, states):
            sv[...] = st.reshape(sv.shape)
        outs, new = f(pids, states, *[r[...] for r in in_refs])
        for r, o in zip(out_refs, outs):
            r[...] = o.astype(r.dtype)
        for s, v in zip(st_refs, new):
            s[slot] = v

    scratch = [pltpu.VMEM((n_slots,) + tuple(s), f32) for s in state_shapes]
    return pl.pallas_call(body, name=name, grid=grid, in_specs=in_specs, out_specs=list(out_specs) + list(save_specs),
                          out_shape=list(out_shapes) + list(save_shapes), scratch_shapes=scratch,
                          compiler_params=_cp(len(grid)))(*ins)


def scan_bwd(name, f, grid, slot_axis, ins, in_specs, saves, save_specs, cts, ct_specs, state_shapes, wrt, acc_first):
    n_in, n_st, n_ct = len(ins), len(saves), len(cts)
    n_slots = grid[slot_axis]
    cax = len(grid) - 1 if slot_axis != len(grid) - 1 else len(grid) - 2

    def body(*refs):
        pids = tuple(pl.program_id(i) for i in range(len(grid)))
        in_refs = refs[:n_in]
        save_refs = refs[n_in:n_in + n_st]
        ct_refs = refs[n_in + n_st:n_in + n_st + n_ct]
        out_refs = refs[n_in + n_st + n_ct:n_in + n_st + n_ct + len(wrt)]
        dst_refs = refs[n_in + n_st + n_ct + len(wrt):]
        slot = pids[slot_axis]

        @pl.when(pids[cax] == 0)
        def _():
            for s in dst_refs:
                s[slot] = jnp.zeros(s.shape[1:], f32)

        vals = [r[...] for r in in_refs]
        states = tuple(sv[...].reshape(shp) for sv, shp in zip(save_refs, state_shapes))
        ctv = tuple(r[...].astype(f32) for r in ct_refs)
        dnew = tuple(s[slot] for s in dst_refs)

        def g(st, *dv):
            full = list(vals)
            for i, v in zip(wrt, dv):
                full[i] = v
            outs, new = f(pids, st, *full)
            return tuple(outs), tuple(new)

        _, vjp = jax.vjp(g, states, *[vals[i] for i in wrt])
        grads = vjp((ctv, dnew))
        for s, v in zip(dst_refs, grads[0]):
            s[slot] = v
        for i, o_ref, gr in zip(wrt, out_refs, grads[1:]):
            first = acc_first.get(i)
            if first is None:
                o_ref[...] = gr.astype(o_ref.dtype)
            else:
                @pl.when(first(pids))
                def _():
                    o_ref[...] = jnp.zeros_like(o_ref)
                o_ref[...] += gr

    out_shapes = [jax.ShapeDtypeStruct(ins[i].shape, f32) for i in wrt]
    out_specs = [in_specs[i] for i in wrt]
    scratch = [pltpu.VMEM((n_slots,) + tuple(s), f32) for s in state_shapes]
    return pl.pallas_call(body, name=name, grid=grid, in_specs=list(in_specs) + list(save_specs) + list(ct_specs),
                          out_specs=out_specs, out_shape=out_shapes, scratch_shapes=scratch,
                          compiler_params=_cp(len(grid)))(*ins, *saves, *cts)


def matmul(name, a, b, ca=1, cb=0, add=None, out_dtype=f32, a_off=0, a_width=None, tm=512, tn=512, tk=512):
    a_width = a.shape[1] if a_width is None else a_width
    kdim = b.shape[cb]
    n = b.shape[1 - cb]
    m = a.shape[0] if ca == 1 else a_width
    tm, tn, tk = min(tm, m), min(tn, n), min(tk, kdim)
    assert m % tm == 0 and n % tn == 0 and kdim % tk == 0
    nk = kdim // tk
    if ca == 1:
        assert a_off % tk == 0 and a_width == kdim
        koff = a_off // tk
        a_spec = pl.BlockSpec((tm, tk), lambda i, j, k: (i, k + koff))
    else:
        assert a_off % tm == 0 and a.shape[0] == kdim
        ioff = a_off // tm
        a_spec = pl.BlockSpec((tk, tm), lambda i, j, k: (k, i + ioff))
    b_spec = pl.BlockSpec((tk, tn), lambda i, j, k: (k, j)) if cb == 0 else pl.BlockSpec((tn, tk), lambda i, j, k: (j, k))
    o_spec = pl.BlockSpec((tm, tn), lambda i, j, k: (i, j))
    has_add = add is not None

    def body(*refs):
        a_ref, b_ref = refs[0], refs[1]
        add_ref = refs[2] if has_add else None
        o_ref, acc = refs[-2], refs[-1]
        k = pl.program_id(2)

        @pl.when(k == 0)
        def _():
            acc[...] = add_ref[...].astype(f32) if has_add else jnp.zeros_like(acc)

        acc[...] += _dg(a_ref[...], b_ref[...], ca, cb)

        @pl.when(k == nk - 1)
        def _():
            o_ref[...] = acc[...].astype(o_ref.dtype)

    ins = [a, b] + ([add] if has_add else [])
    specs = [a_spec, b_spec] + ([o_spec] if has_add else [])
    return pl.pallas_call(body, name=name, grid=(m // tm, n // tn, nk), in_specs=specs, out_specs=o_spec,
                          out_shape=jax.ShapeDtypeStruct((m, n), out_dtype),
                          scratch_shapes=[pltpu.VMEM((tm, tn), f32)], compiler_params=_cp(3))(*ins)


def f_rms(pids, x, w):
    r = lax.rsqrt(jnp.mean(x * x, axis=-1, keepdims=True) + EPS)
    return (x * r * w,)


def _row_spec(tm, width, col=0):
    return pl.BlockSpec((tm, width), lambda i: (i, col))


def _par_spec(shape):
    return pl.BlockSpec(shape, lambda *p: (0,) * len(shape))


def rms_fwd(x, w, tm=512):
    t, d = x.shape
    return map_fwd("rms_fwd", f_rms, (t // tm,), [x, w], [_row_spec(tm, d), _par_spec((1, d))],
                   [jax.ShapeDtypeStruct((t, d), f32)], [_row_spec(tm, d)])[0]


def rms_bwd(dys, x, w, dres, tm=512):
    t, d = x.shape
    n = len(dys)

    def body(*refs):
        x_ref, w_ref, dres_ref, dx_ref, dw_ref = refs[n:]
        dy = refs[0][...]
        for r in refs[1:n]:
            dy = dy + r[...]
        _, vjp = jax.vjp(lambda xx, ww: f_rms(None, xx, ww)[0], x_ref[...], w_ref[...])
        dx, dw = vjp(dy)
        dx_ref[...] = dx + dres_ref[...]

        @pl.when(pl.program_id(0) == 0)
        def _():
            dw_ref[...] = jnp.zeros_like(dw_ref)
        dw_ref[...] += dw

    return pl.pallas_call(body, name="rms_bwd", grid=(t // tm,),
                          in_specs=[_row_spec(tm, d)] * (n + 1) + [_par_spec((1, d)), _row_spec(tm, d)],
                          out_specs=[_row_spec(tm, d), _par_spec((1, d))],
                          out_shape=[jax.ShapeDtypeStruct((t, d), f32), jax.ShapeDtypeStruct((1, d), f32)],
                          compiler_params=_cp(1))(*dys, x, w, dres)


def loss_head(x, w, tgt, tm=512):
    t, d = x.shape

    def fl(xx, ww, tt):
        y = f_rms(None, xx, ww)[0]
        return 0.5 * jnp.sum(jnp.mean(jnp.square(y - tt), axis=-1, keepdims=True), axis=0, keepdims=True)

    def body(x_ref, w_ref, t_ref, loss_ref, dx_ref, dw_ref):
        val, vjp = jax.vjp(lambda xx, ww: fl(xx, ww, t_ref[...]), x_ref[...], w_ref[...])
        dx, dw = vjp(jnp.ones((1, 1), f32))
        dx_ref[...] = dx

        @pl.when(pl.program_id(0) == 0)
        def _():
            dw_ref[...] = jnp.zeros_like(dw_ref)
            loss_ref[...] = jnp.zeros_like(loss_ref)
        dw_ref[...] += dw
        loss_ref[...] += val

    return pl.pallas_call(body, name="loss_head", grid=(t // tm,),
                          in_specs=[_row_spec(tm, d), _par_spec((1, d)), _row_spec(tm, d)],
                          out_specs=[_par_spec((1, 1)), _row_spec(tm, d), _par_spec((1, d))],
                          out_shape=[jax.ShapeDtypeStruct((1, 1), f32), jax.ShapeDtypeStruct((t, d), f32),
                                     jax.ShapeDtypeStruct((1, d), f32)],
                          compiler_params=_cp(1))(x, w, tgt)


def ffn_fwd(x, nw, wg, wu, wd, layer, tm=512):
    t, d = x.shape
    ns, _, _, fs = wg.shape

    def body(x_ref, nw_ref, wg_ref, wu_ref, wd_ref, xo_ref, h_ref, g_ref, u_ref, acc):
        j = pl.program_id(1)

        @pl.when(j == 0)
        def _():
            h_ref[...] = f_rms(None, x_ref[...], nw_ref[...])[0].astype(bf16)
            acc[...] = jnp.zeros_like(acc)

        h = h_ref[...]
        g = jnp.dot(h, wg_ref[0, 0], preferred_element_type=f32)
        u = jnp.dot(h, wu_ref[0, 0], preferred_element_type=f32)
        g_ref[0] = g
        u_ref[0] = u
        acc[...] += jnp.dot((_silu(g) * u).astype(bf16), wd_ref[0, 0], preferred_element_type=f32)

        @pl.when(j == ns - 1)
        def _():
            xo_ref[...] = x_ref[...] + FFN_RES * acc[...]

    row = pl.BlockSpec((tm, d), lambda i, j: (i, 0))
    wcol = pl.BlockSpec((1, 1, d, fs), lambda i, j: (j, layer, 0, 0))
    wrow = pl.BlockSpec((1, 1, fs, d), lambda i, j: (j, layer, 0, 0))
    act = pl.BlockSpec((1, tm, fs), lambda i, j: (j, i, 0))
    return pl.pallas_call(body, name="ffn_fwd", grid=(t // tm, ns),
                          in_specs=[row, pl.BlockSpec((1, d), lambda i, j: (0, 0)), wcol, wcol, wrow],
                          out_specs=[row, row, act, act],
                          out_shape=[jax.ShapeDtypeStruct((t, d), f32), jax.ShapeDtypeStruct((t, d), bf16),
                                     jax.ShapeDtypeStruct((ns, t, fs), f32), jax.ShapeDtypeStruct((ns, t, fs), f32)],
                          scratch_shapes=[pltpu.VMEM((tm, d), f32)], compiler_params=_cp(2))(x, nw, wg, wu, wd)


def ffn_bwd_act(dy, x, nw, g, u, wg, wu, wd, layer, tm=512):
    t, d = x.shape
    ns, _, _, fs = wg.shape

    def body(dy_ref, x_ref, nw_ref, g_ref, u_ref, wg_ref, wu_ref, wd_ref, dx_ref, dnw_ref, dg_ref, du_ref, a_ref, acc):
        i, j = pl.program_id(0), pl.program_id(1)

        @pl.when(j == 0)
        def _():
            acc[...] = jnp.zeros_like(acc)

        dyh = (FFN_RES * dy_ref[...]).astype(bf16)
        da = _dg(dyh, wd_ref[0, 0], 1, 1)
        gg, uu = g_ref[0], u_ref[0]
        sg = jax.nn.sigmoid(gg)
        si = gg * sg
        dgv = (da * uu * (sg * (1.0 + gg * (1.0 - sg)))).astype(bf16)
        duv = (da * si).astype(bf16)
        dg_ref[0] = dgv
        du_ref[0] = duv
        a_ref[0] = (si * uu).astype(bf16)
        acc[...] += _dg(dgv, wg_ref[0, 0], 1, 1) + _dg(duv, wu_ref[0, 0], 1, 1)

        @pl.when(j == ns - 1)
        def _():
            _, vjp = jax.vjp(lambda xx, ww: f_rms(None, xx, ww)[0], x_ref[...], nw_ref[...])
            dx, dw = vjp(acc[...])
            dx_ref[...] = dx + dy_ref[...]

            @pl.when(i == 0)
            def _():
                dnw_ref[...] = jnp.zeros_like(dnw_ref)
            dnw_ref[...] += dw

    row = pl.BlockSpec((tm, d), lambda i, j: (i, 0))
    wcol = pl.BlockSpec((1, 1, d, fs), lambda i, j: (j, layer, 0, 0))
    wrow = pl.BlockSpec((1, 1, fs, d), lambda i, j: (j, layer, 0, 0))
    act = pl.BlockSpec((1, tm, fs), lambda i, j: (j, i, 0))
    par = pl.BlockSpec((1, d), lambda i, j: (0, 0))
    return pl.pallas_call(body, name="ffn_bwd_act", grid=(t // tm, ns),
                          in_specs=[row, row, par, act, act, wcol, wcol, wrow],
                          out_specs=[row, par, act, act, act],
                          out_shape=[jax.ShapeDtypeStruct((t, d), f32), jax.ShapeDtypeStruct((1, d), f32)]
                          + [jax.ShapeDtypeStruct((ns, t, fs), bf16)] * 3,
                          scratch_shapes=[pltpu.VMEM((tm, d), f32)], compiler_params=_cp(2))(dy, x, nw, g, u, wg, wu, wd)


def ffn_bwd_w(h, dy, dg, du, a, layer, n_layers, prev=None, tk=512):
    t, d = h.shape
    ns, _, fs = dg.shape
    nk = t // tk
    n_prev = 0 if prev is None else 3

    def body(*refs):
        h_ref, dy_ref, dg_ref, du_ref, a_ref = refs[:5]
        og, ou, od, ag, au, ad = refs[5 + n_prev:]
        k = pl.program_id(1)

        @pl.when(k == 0)
        def _():
            ag[...] = jnp.zeros_like(ag)
            au[...] = jnp.zeros_like(au)
            ad[...] = jnp.zeros_like(ad)

        hh = h_ref[...]
        ag[...] += _dg(hh, dg_ref[0], 0, 0)
        au[...] += _dg(hh, du_ref[0], 0, 0)
        ad[...] += _dg(a_ref[0], FFN_RES * dy_ref[...], 0, 0)

        @pl.when(k == nk - 1)
        def _():
            og[0, 0] = ag[...].astype(og.dtype)
            ou[0, 0] = au[...].astype(ou.dtype)
            od[0, 0] = ad[...].astype(od.dtype)

    row = pl.BlockSpec((tk, d), lambda j, k: (k, 0))
    act = pl.BlockSpec((1, tk, fs), lambda j, k: (j, k, 0))
    wcol = pl.BlockSpec((1, 1, d, fs), lambda j, k: (j, layer, 0, 0))
    wrow = pl.BlockSpec((1, 1, fs, d), lambda j, k: (j, layer, 0, 0))
    anyspec = pl.BlockSpec(memory_space=pl.ANY)
    return pl.pallas_call(body, name="ffn_bwd_w", grid=(ns, nk), in_specs=[row, row, act, act, act] + [anyspec] * n_prev,
                          out_specs=[wcol, wcol, wrow],
                          out_shape=[jax.ShapeDtypeStruct((ns, n_layers, d, fs), bf16)] * 2
                          + [jax.ShapeDtypeStruct((ns, n_layers, fs, d), bf16)],
                          input_output_aliases={5 + i: i for i in range(n_prev)},
                          scratch_shapes=[pltpu.VMEM((d, fs), f32), pltpu.VMEM((d, fs), f32), pltpu.VMEM((fs, d), f32)],
                          compiler_params=_cp(2))(h, dy, dg, du, a, *(prev or ()))


def f_conv(pids, x, w, b):
    y = b + x * w[CONV_W - 1:CONV_W, :]
    for j in range(CONV_W - 1):
        y = y + tshift(x, CONV_W - 1 - j) * w[j:j + 1, :]
    return (_silu(y),)


def _conv_specs(seq, col0, cb):
    xs = pl.BlockSpec((seq, cb), lambda c, b: (b, col0 + c))
    ws = pl.BlockSpec((CONV_W, cb), lambda c, b: (0, c))
    bs = pl.BlockSpec((1, cb), lambda c, b: (0, c))
    ys = pl.BlockSpec((seq, cb), lambda c, b: (b, c))
    return xs, ws, bs, ys


def conv_fwd(name, src, col_off, w, b, seq, cb=256):
    t = src.shape[0]
    c = w.shape[1]
    xs, ws, bs, ys = _conv_specs(seq, col_off // cb, cb)
    return map_fwd(name, f_conv, (c // cb, t // seq), [src, w, b], [xs, ws, bs],
                   [jax.ShapeDtypeStruct((t, c), f32)], [ys])[0]


def conv_bwd(name, dy, src, col_off, w, b, seq, cb=256):
    t = src.shape[0]
    c = w.shape[1]
    xs, ws, bs, ys = _conv_specs(seq, col_off // cb, cb)

    def body(x_ref, w_ref, b_ref, dy_ref, dx_ref, dw_ref, db_ref):
        _, vjp = jax.vjp(lambda xx, ww, bb: f_conv(None, xx, ww, bb)[0], x_ref[...], w_ref[...], b_ref[...])
        dx, dw, db = vjp(dy_ref[...])
        dx_ref[...] = dx

        @pl.when(pl.program_id(1) == 0)
        def _():
            dw_ref[...] = jnp.zeros_like(dw_ref)
            db_ref[...] = jnp.zeros_like(db_ref)
        dw_ref[...] += dw
        db_ref[...] += db

    return pl.pallas_call(body, name=name, grid=(c // cb, t // seq), in_specs=[xs, ws, bs, ys], out_specs=[ys, ws, bs],
                          out_shape=[jax.ShapeDtypeStruct((t, c), f32), jax.ShapeDtypeStruct(w.shape, f32),
                                     jax.ShapeDtypeStruct(b.shape, f32)], compiler_params=_cp(2))(src, w, b, dy)


def f_ssd(pids, states, xs, dtraw, bm, cm, a_log, dt_bias, d_skip):
    g = pids[2]
    (hn,) = states
    l = xs.shape[0]
    head_of_lane = _iota((LANES, SSD_GW), 1) // SSD_HEAD_DIM + SSD_HG * g
    expand = (_iota((LANES, SSD_GW), 0) == head_of_lane).astype(f32)
    tri = _tri(l)
    dt = jax.nn.softplus(dtraw + dt_bias)
    adt = dt * (-jnp.exp(a_log))
    cs = hdot(tri.astype(f32), adt)
    cst = cs.T
    cs_last = cs[l - 1:l, :]
    dt_e, cs_e, csl_e = hdot(dt, expand), hdot(cs, expand), hdot(cs_last, expand)
    xd = xs * dt_e
    gmat = bdot(cm, bm, 1, 1)
    half = _iota((l, LANES), 1) < SSD_HEAD_DIM
    blocks = []
    for pair in range(SSD_HG // 2):
        xb = xd[:, pair * LANES:(pair + 1) * LANES]
        res = []
        for sub in range(2):
            hid = SSD_HG * g + 2 * pair + sub
            col, row = _lane_pick(cs, hid), _row_pick(cst, hid)
            lm = jnp.exp(jnp.where(tri, col - row, NEG))
            res.append(bdot(gmat * lm, xb, 1, 0))
        blocks.append(jnp.where(half, res[0], res[1]))
    y = jnp.concatenate(blocks, axis=1)
    y = y + jnp.exp(cs_e) * bdot(cm, hn, 1, 0)
    y = y + hdot(d_skip, expand) * xs
    hn_new = jnp.exp(csl_e) * hn + bdot(bm, jnp.exp(csl_e - cs_e) * xd, 0, 0)
    return (y,), (hn_new,)


def _ssd_specs(seq, nch, rev):
    cc = (lambda c: nch - 1 - c) if rev else (lambda c: c)
    xs = pl.BlockSpec((CHUNK, SSD_GW), lambda b, c, g: (b * nch + cc(c), g))
    dt = pl.BlockSpec((CHUNK, LANES), lambda b, c, g: (b * nch + cc(c), OFF_DT // LANES))
    bm = pl.BlockSpec((CHUNK, SSD_STATE), lambda b, c, g: (b * nch + cc(c), 1024 // SSD_STATE + g))
    cm = pl.BlockSpec((CHUNK, SSD_STATE), lambda b, c, g: (b * nch + cc(c), 1024 // SSD_STATE + SSD_GROUPS + g))
    par = pl.BlockSpec((1, LANES), lambda b, c, g: (0, 0))
    sv = pl.BlockSpec((1, 1, SSD_STATE, SSD_GW), lambda b, c, g: (b * nch + cc(c), g, 0, 0))
    ddt = pl.BlockSpec((CHUNK, LANES), lambda b, c, g: (b * nch + cc(c), 0))
    dbc = pl.BlockSpec((CHUNK, SSD_STATE), lambda b, c, g: (b * nch + cc(c), g))
    return xs, dt, bm, cm, par, sv, ddt, dbc


def ssd_fwd(xbc, proj, a_log, dt_bias, d_skip, seq):
    t = xbc.shape[0]
    nch = seq // CHUNK
    xs, dt, bm, cm, par, sv, _, _ = _ssd_specs(seq, nch, False)
    grid = (t // seq, nch, SSD_GROUPS)
    y, hsave = scan_fwd("ssd_fwd", f_ssd, grid, 2, [xbc, proj, xbc, xbc, a_log, dt_bias, d_skip],
                        [xs, dt, bm, cm, par, par, par], [jax.ShapeDtypeStruct((t, SSD_GROUPS * SSD_GW), f32)], [xs],
                        [(SSD_STATE, SSD_GW)], [0.0],
                        [jax.ShapeDtypeStruct((t // CHUNK, SSD_GROUPS, SSD_STATE, SSD_GW), f32)], [sv])
    return y, hsave


def ssd_bwd(dy, xbc, proj, a_log, dt_bias, d_skip, hsave, seq):
    t = xbc.shape[0]
    nch = seq // CHUNK
    xs, dt, bm, cm, par, sv, ddt, dbc = _ssd_specs(seq, nch, True)
    grid = (t // seq, nch, SSD_GROUPS)

    def body(x_ref, dt_ref, b_ref, c_ref, al_ref, db_ref, ds_ref, h_ref, dy_ref,
             dxbc_x, dxbc_b, dxbc_c, ddt_ref, dal_ref, ddb_ref, dds_ref, dst):
        pids = tuple(pl.program_id(i) for i in range(3))
        slot = pids[2]

        @pl.when(pids[1] == 0)
        def _():
            dst[slot] = jnp.zeros(dst.shape[1:], f32)

        vals = [x_ref[...], dt_ref[...], b_ref[...], c_ref[...], al_ref[...], db_ref[...], ds_ref[...]]

        def gfun(st, *v):
            outs, new = f_ssd(pids, (st,), *v)
            return outs[0], new[0]

        _, vjp = jax.vjp(gfun, h_ref[0, 0], *vals)
        grads = vjp((dy_ref[...], dst[slot]))
        dst[slot] = grads[0]
        dxbc_x[...] = grads[1]
        dxbc_b[...] = grads[3]
        dxbc_c[...] = grads[4]

        @pl.when(slot == 0)
        def _():
            ddt_ref[...] = jnp.zeros_like(ddt_ref)
        ddt_ref[...] += grads[2]
        first = jnp.logical_and(jnp.logical_and(pids[0] == 0, pids[1] == 0), slot == 0)

        @pl.when(first)
        def _():
            dal_ref[...] = jnp.zeros_like(dal_ref)
            ddb_ref[...] = jnp.zeros_like(ddb_ref)
            dds_ref[...] = jnp.zeros_like(dds_ref)
        dal_ref[...] += grads[5]
        ddb_ref[...] += grads[6]
        dds_ref[...] += grads[7]

    bc_shape = jax.ShapeDtypeStruct((t, SSD_GROUPS * SSD_STATE), f32)
    par_shape = jax.ShapeDtypeStruct((1, LANES), f32)
    outs = pl.pallas_call(body, name="ssd_bwd", grid=grid, in_specs=[xs, dt, bm, cm, par, par, par, sv, xs],
                          out_specs=[xs, dbc, dbc, ddt, par, par, par],
                          out_shape=[jax.ShapeDtypeStruct((t, SSD_GROUPS * SSD_GW), f32),
                                     bc_shape, bc_shape, jax.ShapeDtypeStruct((t, LANES), f32),
                                     par_shape, par_shape, par_shape],
                          scratch_shapes=[pltpu.VMEM((SSD_GROUPS, SSD_STATE, SSD_GW), f32)],
                          compiler_params=_cp(3))(xbc, proj, xbc, xbc, a_log, dt_bias, d_skip, hsave, dy)
    return outs


def f_ssd_epi(pids, y, z, nw):
    yg = y * _silu(z)
    hw = yg.shape[1] // SSD_GROUPS
    parts = []
    for g in range(SSD_GROUPS):
        p = yg[:, g * hw:(g + 1) * hw]
        parts.append(p * lax.rsqrt(jnp.mean(p * p, axis=-1, keepdims=True) + EPS))
    return (jnp.concatenate(parts, axis=1) * nw,)


def f_ml_epi(pids, hm, xc, mz, nw, skip):
    parts = []
    for h in range(ML_HEADS):
        p = hm[:, h * ML_HD:(h + 1) * ML_HD]
        mu = jnp.mean(p, axis=-1, keepdims=True)
        var = jnp.mean(jnp.square(p - mu), axis=-1, keepdims=True)
        parts.append((p - mu) * lax.rsqrt(var + EPS))
    hn = jnp.concatenate(parts, axis=1) * nw
    return ((hn + skip * xc) * _silu(mz),)


def f_s5_post(pids, ys, u, d_skip):
    return (jax.nn.gelu(ys + d_skip * u),)


def f_glu(pids, pab, ba, bb):
    d = ba.shape[1]
    return ((pab[:, :d] + ba) * jax.nn.sigmoid(pab[:, d:] + bb),)


def f_glu_res(pids, pab, xres, ba, bb):
    return (xres + f_glu(pids, pab, ba, bb)[0],)


def rowwise_fwd(name, f, rows, row_cols, pars, out_width, tm=512):
    t = rows[0].shape[0]
    specs = [_row_spec(tm, w, c) for (w, c) in row_cols] + [_par_spec(p.shape) for p in pars]
    return map_fwd(name, f, (t // tm,), list(rows) + list(pars), specs, [jax.ShapeDtypeStruct((t, out_width), f32)],
                   [_row_spec(tm, out_width)])[0]


def rowwise_bwd(name, f, rows, row_cols, pars, dy, tm=256):
    t = rows[0].shape[0]
    n_r, n_p = len(rows), len(pars)
    specs = [_row_spec(tm, w, c) for (w, c) in row_cols] + [_par_spec(p.shape) for p in pars]
    out_w = dy.shape[1]

    def body(*refs):
        vals = [r[...] for r in refs[:n_r + n_p]]
        dy_ref = refs[n_r + n_p]
        outs = refs[n_r + n_p + 1:]
        _, vjp = jax.vjp(lambda *v: f(None, *v)[0], *vals)
        grads = vjp(dy_ref[...])
        for k in range(n_r):
            outs[k][...] = grads[k]

        @pl.when(pl.program_id(0) == 0)
        def _():
            for k in range(n_p):
                outs[n_r + k][...] = jnp.zeros_like(outs[n_r + k])
        for k in range(n_p):
            outs[n_r + k][...] += grads[n_r + k]

    out_shapes = [jax.ShapeDtypeStruct((t, w), f32) for (w, c) in row_cols] + [jax.ShapeDtypeStruct(p.shape, f32) for p in pars]
    out_specs = [_row_spec(tm, w) for (w, c) in row_cols] + [_par_spec(p.shape) for p in pars]
    return pl.pallas_call(body, name=name, grid=(t // tm,), in_specs=specs + [_row_spec(tm, out_w)], out_specs=out_specs,
                          out_shape=out_shapes, compiler_params=_cp(1))(*rows, *pars, dy)


def f_ml(pids, states, q, k, v, g1, g2, g3, b_if):
    h = pids[2]
    cst, nst, mst = states
    l = q.shape[0]
    gt = g1 + g2 + g3 + b_if
    k = k * (1.0 / math.sqrt(ML_HD))
    tri = _tri(l)
    bc_all = hdot(tri.astype(f32), jax.nn.log_sigmoid(gt))
    bcum, ig = _lane_pick(bc_all, ML_HEADS + h), _lane_pick(gt, h)
    bcum_t, ig_t = _row_pick(bc_all.T, ML_HEADS + h), _row_pick(gt.T, h)
    b_last = bcum[l - 1:l, :]
    dlog = jnp.where(tri, bcum - bcum_t + ig_t, NEG)
    ws = b_last - bcum + ig
    m_prev = mst[:, 0:1]
    m_new = lax.stop_gradient(jnp.maximum(b_last + m_prev, jnp.max(ws, axis=0, keepdims=True)))
    decay = jnp.exp(b_last + m_prev - m_new)
    wts = jnp.exp(ws - m_new)
    c_new = decay * cst + bdot(wts * v, k, 0, 0)
    n_new = decay * nst + jnp.sum(wts * k, axis=0, keepdims=True)
    m_inter = bcum + m_prev
    m_t = lax.stop_gradient(jnp.maximum(jnp.max(dlog, axis=1, keepdims=True), m_inter))
    scores = bdot(q, k, 1, 1) * jnp.exp(dlog - m_t)
    inter_w = jnp.exp(m_inter - m_t)
    num = bdot(scores, v, 1, 0) + inter_w * bdot(q, cst, 1, 1)
    den = jnp.sum(scores, axis=1, keepdims=True) + inter_w * jnp.sum(q * nst, axis=1, keepdims=True)
    hout = num / jnp.maximum(jnp.abs(den), jnp.exp(-m_t))
    return (hout,), (c_new, n_new, jnp.broadcast_to(m_new, mst.shape))


def _ml_specs(nch, rev):
    cc = (lambda c: nch - 1 - c) if rev else (lambda c: c)
    hd = pl.BlockSpec((CHUNK, ML_HD), lambda b, c, h: (b * nch + cc(c), h))
    gt = pl.BlockSpec((CHUNK, LANES), lambda b, c, h: (b * nch + cc(c), 0))
    par = pl.BlockSpec((1, LANES), lambda b, c, h: (0, 0))
    sc = pl.BlockSpec((1, 1, ML_HD, ML_HD), lambda b, c, h: (b * nch + cc(c), h, 0, 0))
    sn = pl.BlockSpec((1, 1, 1, ML_HD), lambda b, c, h: (b * nch + cc(c), h, 0, 0))
    sm = pl.BlockSpec((1, 1, 1, LANES), lambda b, c, h: (b * nch + cc(c), h, 0, 0))
    return hd, gt, par, sc, sn, sm


ML_STATE_SHAPES = [(ML_HD, ML_HD), (1, ML_HD), (1, LANES)]


def ml_fwd(q, k, v, g1, g2, g3, b_if, seq):
    t = q.shape[0]
    nch = seq // CHUNK
    hd, gt, par, sc, sn, sm = _ml_specs(nch, False)
    nc = t // CHUNK
    outs = scan_fwd("ml_fwd", f_ml, (t // seq, nch, ML_HEADS), 2, [q, k, v, g1, g2, g3, b_if],
                    [hd, hd, hd, gt, gt, gt, par], [jax.ShapeDtypeStruct((t, ML_HEADS * ML_HD), f32)], [hd],
                    ML_STATE_SHAPES, [0.0, 0.0, NEG],
                    [jax.ShapeDtypeStruct((nc, ML_HEADS, ML_HD, ML_HD), f32), jax.ShapeDtypeStruct((nc, ML_HEADS, 1, ML_HD), f32),
                     jax.ShapeDtypeStruct((nc, ML_HEADS, 1, LANES), f32)], [sc, sn, sm])
    return outs[0], outs[1:]


def ml_bwd(dh, q, k, v, g1, g2, g3, b_if, saves, seq):
    t = q.shape[0]
    nch = seq // CHUNK
    hd, gt, par, sc, sn, sm = _ml_specs(nch, True)

    def f(pids, states, q, k, v, gsum, b_if):
        return f_ml(pids, states, q, k, v, gsum, jnp.zeros_like(gsum), jnp.zeros_like(gsum), b_if)

    def body(q_ref, k_ref, v_ref, g1_ref, g2_ref, g3_ref, b_ref, c_ref, n_ref, m_ref, dh_ref,
             dq_ref, dk_ref, dv_ref, dg_ref, db_ref, dc_s, dn_s):
        pids = tuple(pl.program_id(i) for i in range(3))
        slot = pids[2]

        @pl.when(pids[1] == 0)
        def _():
            dc_s[slot] = jnp.zeros(dc_s.shape[1:], f32)
            dn_s[slot] = jnp.zeros(dn_s.shape[1:], f32)

        gsum = g1_ref[...] + g2_ref[...] + g3_ref[...]
        mst = m_ref[0, 0]

        def gfun(cst, nst, qq, kk, vv, gs, bb):
            outs, new = f(pids, (cst, nst, mst), qq, kk, vv, gs, bb)
            return outs[0], new[0], new[1]

        _, vjp = jax.vjp(gfun, c_ref[0, 0], n_ref[0, 0], q_ref[...], k_ref[...], v_ref[...], gsum, b_ref[...])
        grads = vjp((dh_ref[...], dc_s[slot], dn_s[slot]))
        dc_s[slot] = grads[0]
        dn_s[slot] = grads[1]
        dq_ref[...] = grads[2]
        dk_ref[...] = grads[3]
        dv_ref[...] = grads[4]

        @pl.when(slot == 0)
        def _():
            dg_ref[...] = jnp.zeros_like(dg_ref)
        dg_ref[...] += grads[5]
        first = jnp.logical_and(jnp.logical_and(pids[0] == 0, pids[1] == 0), slot == 0)

        @pl.when(first)
        def _():
            db_ref[...] = jnp.zeros_like(db_ref)
        db_ref[...] += grads[6]

    big = jax.ShapeDtypeStruct((t, ML_HEADS * ML_HD), f32)
    return pl.pallas_call(body, name="ml_bwd", grid=(t // seq, nch, ML_HEADS),
                          in_specs=[hd, hd, hd, gt, gt, gt, par, sc, sn, sm, hd], out_specs=[hd, hd, hd, gt, par],
                          out_shape=[big, big, big, jax.ShapeDtypeStruct((t, LANES), f32), jax.ShapeDtypeStruct((1, LANES), f32)],
                          scratch_shapes=[pltpu.VMEM((ML_HEADS, ML_HD, ML_HD), f32), pltpu.VMEM((ML_HEADS, 1, ML_HD), f32)],
                          compiler_params=_cp(3))(q, k, v, g1, g2, g3, b_if, *saves, dh)


def f_s5(pids, states, u, bb, cc, pw):
    (carry,) = states
    tl = u.shape[0]
    bu = bdot(u, bb, 1, 0)
    xr, xi = bu[:, :S5_CH], bu[:, S5_CH:]
    sh = 1
    while sh < tl:
        pr, pi = pw[sh - 1:sh, :S5_CH], pw[sh - 1:sh, S5_CH:]
        sr, si = tshift(xr, sh), tshift(xi, sh)
        xr, xi = xr + pr * sr - pi * si, xi + pr * si + pi * sr
        sh *= 2
    cr, ci = carry[:, :S5_CH], carry[:, S5_CH:]
    pwr, pwi = pw[:, :S5_CH], pw[:, S5_CH:]
    xr, xi = xr + pwr * cr - pwi * ci, xi + pwr * ci + pwi * cr
    x = jnp.concatenate([xr, xi], axis=1)
    y = bdot(x, cc, 1, 0)
    return (y,), (x[tl - 1:tl, :],)


def _s5_specs(ntl, rev):
    tt = (lambda t: ntl - 1 - t) if rev else (lambda t: t)
    us = pl.BlockSpec((S5_TL, LANES), lambda c, b, t: (b * ntl + tt(t), c))
    bbs = pl.BlockSpec((1, LANES, 2 * S5_CH), lambda c, b, t: (c, 0, 0))
    ccs = pl.BlockSpec((1, 2 * S5_CH, LANES), lambda c, b, t: (c, 0, 0))
    pws = pl.BlockSpec((1, S5_TL, 2 * S5_CH), lambda c, b, t: (c, 0, 0))
    sv = pl.BlockSpec((1, 1, 1, 2 * S5_CH), lambda c, b, t: (b * ntl + tt(t), c, 0, 0))
    return us, bbs, ccs, pws, sv


def s5_fwd(u, bb, cc, pw, seq):
    t = u.shape[0]
    ntl = seq // S5_TL
    us, bbs, ccs, pws, sv = _s5_specs(ntl, False)

    def f(pids, states, uu, b3, c3, p3):
        return f_s5(pids, states, uu, b3[0], c3[0], p3[0])

    y, carries = scan_fwd("s5_fwd", f, (S5_CB, t // seq, ntl), 0, [u, bb, cc, pw], [us, bbs, ccs, pws],
                          [jax.ShapeDtypeStruct((t, S5_CB * LANES), f32)], [us], [(1, 2 * S5_CH)], [0.0],
                          [jax.ShapeDtypeStruct((t // S5_TL, S5_CB, 1, 2 * S5_CH), f32)], [sv])
    return y, carries


def s5_bwd(dy, u, bb, cc, pw, carries, seq):
    t = u.shape[0]
    ntl = seq // S5_TL
    us, bbs, ccs, pws, sv = _s5_specs(ntl, True)

    def f(pids, states, uu, b3, c3, p3):
        return f_s5(pids, states, uu, b3[0], c3[0], p3[0])

    first = lambda pids: jnp.logical_and(pids[1] == 0, pids[2] == 0)
    return scan_bwd("s5_bwd", f, (S5_CB, t // seq, ntl), 0, [u, bb, cc, pw], [us, bbs, ccs, pws], [carries], [sv],
                    [dy], [us], [(1, 2 * S5_CH)], [0, 1, 2, 3], {1: first, 2: first, 3: first})


def _adam_math(g, w, m, v):
    m2 = ADAM_B1 * m + (1.0 - ADAM_B1) * g
    v2 = ADAM_B2 * v + (1.0 - ADAM_B2) * jnp.square(g)
    m_hat = m2 / (1.0 - ADAM_B1 ** ADAM_STEP)
    v_hat = v2 / (1.0 - ADAM_B2 ** ADAM_STEP)
    delta = -ADAM_LR * (m_hat / (jnp.sqrt(v_hat) + ADAM_EPS) + ADAM_WD * w)
    return delta, m2, v2


def adamw(name, parts, w, m, v, tr=256):
    n, r, c = parts.shape
    tr = min(tr, r)
    assert r % tr == 0

    def body(p_ref, w_ref, m_ref, v_ref, g_ref, d_ref, m2_ref, v2_ref):
        g = p_ref[0].astype(f32)
        for s in range(1, n):
            g = g + p_ref[s].astype(f32)
        d, m2, v2 = _adam_math(g, w_ref[...], m_ref[...], v_ref[...])
        g_ref[...] = g
        d_ref[...] = d
        m2_ref[...] = m2
        v2_ref[...] = v2

    ps = pl.BlockSpec((n, tr, c), lambda i: (0, i, 0))
    rs = pl.BlockSpec((tr, c), lambda i: (i, 0))
    return pl.pallas_call(body, name=name, grid=(r // tr,), in_specs=[ps, rs, rs, rs], out_specs=[rs] * 4,
                          out_shape=[jax.ShapeDtypeStruct((r, c), f32)] * 4, compiler_params=_cp(1))(parts, w, m, v)


def sum_parts(name, parts, tr=256):
    n, r, c = parts.shape
    tr = min(tr, r)
    assert r % tr == 0

    def body(p_ref, o_ref):
        g = p_ref[0].astype(f32)
        for s in range(1, n):
            g = g + p_ref[s].astype(f32)
        o_ref[...] = g

    return pl.pallas_call(body, name=name, grid=(r // tr,), in_specs=[pl.BlockSpec((n, tr, c), lambda i: (0, i, 0))],
                          out_specs=pl.BlockSpec((tr, c), lambda i: (i, 0)),
                          out_shape=jax.ShapeDtypeStruct((r, c), f32), compiler_params=_cp(1))(parts)


def exchange(name, ops):
    n = len(ops)

    def body(*refs):
        ins, outs = refs[:n], refs[n:2 * n]
        send_sems, recv_sems, loc_sems = refs[2 * n:]
        x, y, c = lax.axis_index("x"), lax.axis_index("y"), lax.axis_index("c")
        me = 4 * x + 2 * y + c
        copies = []
        for k, (_, mode) in enumerate(ops):
            src_me = ins[k] if mode == "gather" else ins[k].at[me]
            loc = pltpu.make_async_copy(src_me, outs[k].at[me], loc_sems.at[k])
            loc.start()
            copies.append(loc)
            for d in range(1, N_DEV):
                dx, dy, dc = (d >> 2) & 1, (d >> 1) & 1, d & 1
                px = 1 - x if dx else x
                py = 1 - y if dy else y
                pc = 1 - c if dc else c
                src = ins[k] if mode == "gather" else ins[k].at[4 * px + 2 * py + pc]
                cp = pltpu.make_async_remote_copy(src_ref=src, dst_ref=outs[k].at[me], send_sem=send_sems.at[k, d - 1],
                                                  recv_sem=recv_sems.at[k, d - 1], device_id=(px, py, pc),
                                                  device_id_type=pl.DeviceIdType.MESH)
                cp.start()
                copies.append(cp)
        for cp in copies:
            cp.wait()

    out_shapes = []
    for a, mode in ops:
        shp = (N_DEV,) + tuple(a.shape) if mode == "gather" else tuple(a.shape)
        out_shapes.append(jax.ShapeDtypeStruct(shp, a.dtype))
    anyspec = pl.BlockSpec(memory_space=pl.ANY)
    return pl.pallas_call(body, name=name, in_specs=[anyspec] * n, out_specs=[anyspec] * n, out_shape=out_shapes,
                          scratch_shapes=[pltpu.SemaphoreType.DMA((n, N_DEV - 1)), pltpu.SemaphoreType.DMA((n, N_DEV - 1)),
                                          pltpu.SemaphoreType.DMA((n,))])(*[a for a, _ in ops])


def _lanes(v, width=LANES):
    v = v.reshape(1, -1)
    return jnp.pad(v, ((0, 0), (0, width - v.shape[1])))


def win_to_padded(w):
    return jnp.concatenate([w[:, :1024], w[:, 2576:3600], w[:, 3600:4624], w[:, 1024:2560], w[:, 2560:2576],
                            jnp.zeros((w.shape[0], PROJ_W - IN_COLS), w.dtype)], axis=1)


def win_from_padded(wp):
    return jnp.concatenate([wp[:, 0:1024], wp[:, 3072:4608], wp[:, 4608:4624], wp[:, 1024:2048], wp[:, 2048:3072]], axis=1)


def headwise_dense(w):
    nb, o, i = w.shape
    return jnp.einsum("noi,nm->nimo", w, jnp.eye(nb, dtype=w.dtype)).reshape(nb * i, nb * o)


def headwise_from_dense(dd, o=4, i=4):
    nb = dd.shape[0] // i
    return jnp.diagonal(dd.reshape(nb, i, nb, o), axis1=0, axis2=2).transpose(2, 1, 0)


def s5_tables(a_re, a_im, log_step, b_re, b_im, c_re, c_im):
    step = jnp.exp(log_step)[:, None]
    r = jnp.arange(1, S5_TL + 1, dtype=f32)[:, None, None]
    mag = jnp.exp(r * (a_re * step))
    pw_re, pw_im = mag * jnp.cos(r * (a_im * step)), mag * jnp.sin(r * (a_im * step))
    lam_re, lam_im = pw_re[0], pw_im[0]
    den = a_re * a_re + a_im * a_im
    coef_re = ((lam_re - 1.0) * a_re + lam_im * a_im) / den
    coef_im = (lam_im * a_re - (lam_re - 1.0) * a_im) / den
    bb_re = coef_re[..., None] * b_re - coef_im[..., None] * b_im
    bb_im = coef_re[..., None] * b_im + coef_im[..., None] * b_re
    gl = S5_GROUPS // S5_CB
    eye = jnp.eye(gl, dtype=f32)

    def blk_b(t):
        t4 = t.transpose(0, 2, 1).reshape(S5_CB, gl, S5_GROUP, S5_STATE)
        return jnp.einsum("kgcn,gh->kgchn", t4, eye).reshape(S5_CB, gl * S5_GROUP, gl * S5_STATE)

    def blk_c(t):
        t4 = t.reshape(S5_CB, gl, S5_GROUP, S5_STATE)
        return jnp.einsum("kgcn,gh->kgnhc", t4, eye).reshape(S5_CB, gl * S5_STATE, gl * S5_GROUP)

    def blk_p(t):
        return t.reshape(S5_TL, S5_CB, gl * S5_STATE).transpose(1, 0, 2)

    bb = jnp.concatenate([blk_b(bb_re), blk_b(bb_im)], axis=2)
    cc = jnp.concatenate([blk_c(c_re), -blk_c(c_im)], axis=1)
    pw = jnp.concatenate([blk_p(pw_re), blk_p(pw_im)], axis=2)
    return bb, cc, pw


def ffn_step_bwd(dy, x, fw, layer, saved, prev):
    h, g, u = saved
    dx, dnw, dg, du, a = ffn_bwd_act(dy, x, fw["nw"][layer:layer + 1], g, u, fw["wg"], fw["wu"], fw["wd"], layer)
    dws = ffn_bwd_w(h, dy, dg, du, a, layer, fw["wg"].shape[1], prev)
    return dx, dnw, dws


def hybrid_fwd(x1, p, seq):
    u = rms_fwd(x1, p["mix_norm"])
    proj = matmul("hy_in", u, p["win"], tn=256)
    xbc = conv_fwd("ssd_conv", proj, OFF_XBC, p["ssd_conv_w"], p["ssd_conv_b"], seq)
    yraw, hsave = ssd_fwd(xbc, proj, p["a_log"], p["dt_bias"], p["ssd_d"], seq)
    yssd = rowwise_fwd("ssd_epi", f_ssd_epi, [yraw, proj], [(D_MODEL, 0), (D_MODEL, OFF_Z // D_MODEL)], [p["ssd_norm_w"]], D_MODEL)
    xc = conv_fwd("ml_conv", proj, OFF_MX, p["ml_conv_w"], p["ml_conv_b"], seq)
    q = matmul("hw_q", xc, p["wq"])
    k = matmul("hw_k", xc, p["wk"])
    v = matmul("hw_v", proj, p["wv"], a_off=OFF_MX, a_width=D_MODEL)
    g1 = matmul("gate_q", q, p["wif_q"])
    g2 = matmul("gate_k", k, p["wif_k"])
    g3 = matmul("gate_v", v, p["wif_v"])
    hm, mlsave = ml_fwd(q, k, v, g1, g2, g3, p["b_if"], seq)
    yml = rowwise_fwd("ml_epi", f_ml_epi, [hm, xc, proj], [(D_MODEL, 0), (D_MODEL, 0), (D_MODEL, OFF_MZ // D_MODEL)],
                      [p["ml_norm_w"], p["ml_skip"]], D_MODEL)
    t = matmul("hy_out1", yssd, p["wo1"], add=x1)
    x2 = matmul("hy_out2", yml, p["wo2"], add=t)
    return x2, (u, proj, xbc, yraw, hsave, yssd, xc, q, k, v, g1, g2, g3, hm, mlsave, yml)


def hybrid_bwd(dx2, x1, p, saved, seq):
    u, proj, xbc, yraw, hsave, yssd, xc, q, k, v, g1, g2, g3, hm, mlsave, yml = saved
    gr = {}
    dyssd = matmul("d_yssd", dx2, p["wo1"], cb=1)
    dyml = matmul("d_yml", dx2, p["wo2"], cb=1)
    gr["wo"] = jnp.concatenate([matmul("dw_o1", yssd, dx2, ca=0), matmul("dw_o2", yml, dx2, ca=0)], axis=0)
    d_hm, d_xc, d_mz, gr["ml_norm_w"], gr["ml_skip"] = rowwise_bwd(
        "ml_epi_bwd", f_ml_epi, [hm, xc, proj], [(D_MODEL, 0), (D_MODEL, 0), (D_MODEL, OFF_MZ // D_MODEL)],
        [p["ml_norm_w"], p["ml_skip"]], dyml)
    dq, dk, dv, dgt, gr["b_if"] = ml_bwd(d_hm, q, k, v, g1, g2, g3, p["b_if"], mlsave, seq)
    dq = matmul("dq_gate", dgt, p["wif_q"], cb=1, add=dq)
    dk = matmul("dk_gate", dgt, p["wif_k"], cb=1, add=dk)
    dv = matmul("dv_gate", dgt, p["wif_v"], cb=1, add=dv)
    gr["wif"] = jnp.concatenate([matmul("dw_if_q", q, dgt, ca=0), matmul("dw_if_k", k, dgt, ca=0),
                                 matmul("dw_if_v", v, dgt, ca=0)], axis=0)
    d_xc = matmul("dxc_q", dq, p["wq"], cb=1, add=d_xc)
    d_xc = matmul("dxc_k", dk, p["wk"], cb=1, add=d_xc)
    gr["wq"] = matmul("dw_q", xc, dq, ca=0)
    gr["wk"] = matmul("dw_k", xc, dk, ca=0)
    gr["wv"] = matmul("dw_v", proj, dv, ca=0, a_off=OFF_MX, a_width=D_MODEL)
    d_mx, gr["ml_conv_w"], gr["ml_conv_b"] = conv_bwd("ml_conv_bwd", d_xc, proj, OFF_MX, p["ml_conv_w"], p["ml_conv_b"], seq)
    d_mx = matmul("dmx_v", dv, p["wv"], cb=1, add=d_mx)
    d_yraw, d_z, gr["ssd_norm_w"] = rowwise_bwd("ssd_epi_bwd", f_ssd_epi, [yraw, proj],
                                                [(D_MODEL, 0), (D_MODEL, OFF_Z // D_MODEL)], [p["ssd_norm_w"]], dyssd)
    d_xs, d_b, d_c, d_dt, gr["a_log"], gr["dt_bias"], gr["ssd_d"] = ssd_bwd(
        d_yraw, xbc, proj, p["a_log"], p["dt_bias"], p["ssd_d"], hsave, seq)
    d_xbc, gr["ssd_conv_w"], gr["ssd_conv_b"] = conv_bwd("ssd_conv_bwd", jnp.concatenate([d_xs, d_b, d_c], axis=1), proj, OFF_XBC,
                                                         p["ssd_conv_w"], p["ssd_conv_b"], seq)
    dproj = jnp.concatenate([d_z, d_mx, d_mz, d_xbc, d_dt, jnp.zeros((d_dt.shape[0], PROJ_W - OFF_DT - LANES), f32)], axis=1)
    du = matmul("d_u", dproj, p["win"], cb=1, tk=256)
    gr["win"] = matmul("dw_in", u, dproj, ca=0, tn=256)
    dx1, gr["mix_norm"] = rms_bwd([du], x1, p["mix_norm"], dx2)
    return dx1, gr


def s5_layer_fwd(x4, p, seq):
    u = rms_fwd(x4, p["mix_norm"])
    ys, carries = s5_fwd(u, p["bb"], p["cc"], p["pw"], seq)
    gg = rowwise_fwd("s5_post", f_s5_post, [ys, u], [(D_MODEL, 0), (D_MODEL, 0)], [p["s5_d"]], D_MODEL)
    pab = matmul("s5_ab", gg, p["wab"])
    x5 = rowwise_fwd("s5_glu", f_glu_res, [pab, x4], [(2 * D_MODEL, 0), (D_MODEL, 0)], [p["b_a"], p["b_b"]], D_MODEL)
    return x5, (u, ys, carries, gg, pab)


def s5_layer_bwd(dx5, x4, p, saved, seq):
    u, ys, carries, gg, pab = saved
    gr = {}
    dpab, gr["b_a"], gr["b_b"] = rowwise_bwd("s5_glu_bwd", f_glu, [pab], [(2 * D_MODEL, 0)], [p["b_a"], p["b_b"]], dx5)
    dgg = matmul("d_gg", dpab, p["wab"], cb=1)
    gr["wab"] = matmul("dw_ab", gg, dpab, ca=0)
    dys, du_a, gr["s5_d"] = rowwise_bwd("s5_post_bwd", f_s5_post, [ys, u], [(D_MODEL, 0), (D_MODEL, 0)], [p["s5_d"]], dgg)
    du_b, gr["bb"], gr["cc"], gr["pw"] = s5_bwd(dys, u, p["bb"], p["cc"], p["pw"], carries, seq)
    dx4, gr["mix_norm"] = rms_bwd([du_a, du_b], x4, p["mix_norm"], dx5)
    return dx4, gr


def local_step(x, tgt, ffn, w0, w1, final_norm, seq):
    x1, f10 = _ffn(x, ffn["ffn1"], 0)
    x2, sv_h = hybrid_fwd(x1, w0, seq)
    x3, f20 = _ffn(x2, ffn["ffn2"], 0)
    x4, f11 = _ffn(x3, ffn["ffn1"], 1)
    x5, sv_s = s5_layer_fwd(x4, w1, seq)
    x6, f21 = _ffn(x5, ffn["ffn2"], 1)
    loss, dx6, d_final = loss_head(x6, final_norm, tgt)
    dx5, dn21, dw2 = ffn_step_bwd(dx6, x5, ffn["ffn2"], 1, f21, None)
    dx4, g_s5 = s5_layer_bwd(dx5, x4, w1, sv_s, seq)
    dx3, dn11, dw1 = ffn_step_bwd(dx4, x3, ffn["ffn1"], 1, f11, None)
    dx2, dn20, dw2 = ffn_step_bwd(dx3, x2, ffn["ffn2"], 0, f20, dw2)
    dx1, g_hy = hybrid_bwd(dx2, x1, w0, sv_h, seq)
    dx0, dn10, dw1 = ffn_step_bwd(dx1, x, ffn["ffn1"], 0, f10, dw1)
    g_ffn = {"ffn1": (jnp.concatenate([dn10, dn11], axis=0), dw1), "ffn2": (jnp.concatenate([dn20, dn21], axis=0), dw2)}
    return loss, dx0, d_final, g_ffn, g_hy, g_s5


def _ffn(x, fw, layer):
    xo, h, g, u = ffn_fwd(x, fw["nw"][layer:layer + 1], fw["wg"], fw["wu"], fw["wd"], layer)
    return xo, (h, g, u)


BIG = ["ffn1_w_gate", "ffn1_w_up", "ffn1_w_down", "ffn2_w_gate", "ffn2_w_up", "ffn2_w_down", "hy_w_in", "hy_w_out", "s5_w_a", "s5_w_b"]
SMALL_SHARDED = {"ssd_conv_w": 2, "ml_conv_w": 2, "ml_w_q": 1, "ml_w_k": 1, "ml_w_v": 1, "ml_w_if": 1, "s5_d": 1, "s5_b_a": 1, "s5_b_b": 1}
WEIGHTS = ["ffn1_norm", "ffn1_w_gate", "ffn1_w_up", "ffn1_w_down", "mix_norm", "ffn2_norm", "ffn2_w_gate", "ffn2_w_up", "ffn2_w_down",
           "hy_w_in", "ssd_conv_w", "ssd_conv_b", "ssd_dt_bias", "ssd_a_log", "ssd_d", "ssd_norm_w", "ml_conv_w", "ml_conv_b",
           "ml_w_q", "ml_w_k", "ml_w_v", "ml_w_if", "ml_b_if", "ml_norm_w", "ml_skip", "hy_w_out", "s5_a_re", "s5_a_im",
           "s5_log_step", "s5_b_re", "s5_b_im", "s5_c_re", "s5_c_im", "s5_d", "s5_w_a", "s5_b_a", "s5_w_b", "s5_b_b", "final_norm"]
SMALL = [n for n in WEIGHTS if n not in BIG]
S5_PARAMS = ["s5_a_re", "s5_a_im", "s5_log_step", "s5_b_re", "s5_b_im", "s5_c_re", "s5_c_im"]


def _unshard(g, axis):
    return jnp.concatenate([g[i] for i in range(N_DEV)], axis=axis)


def assemble(gw, rep):
    ffn = {pre: dict(nw=rep[pre + "_norm"], wg=gw[pre + "_w_gate"].astype(bf16), wu=gw[pre + "_w_up"].astype(bf16),
                     wd=gw[pre + "_w_down"].astype(bf16)) for pre in ("ffn1", "ffn2")}
    padn = lambda w: jnp.pad(w, ((0, 0), (0, LANES - w.shape[1]))).astype(bf16)
    wif = _unshard(gw["ml_w_if"], 1)[0]
    wo = _unshard(gw["hy_w_out"], 1)[0].astype(bf16)
    dense = lambda n: headwise_dense(_unshard(gw[n], 1)[0].astype(f32)).astype(bf16)
    w0 = dict(mix_norm=rep["mix_norm"][0:1],
              win=win_to_padded(_unshard(gw["hy_w_in"], 2)[0]).astype(bf16),
              ssd_conv_w=_unshard(gw["ssd_conv_w"], 2)[0], ssd_conv_b=rep["ssd_conv_b"],
              a_log=_lanes(rep["ssd_a_log"]), dt_bias=_lanes(rep["ssd_dt_bias"]), ssd_d=_lanes(rep["ssd_d"]),
              ssd_norm_w=rep["ssd_norm_w"], ml_conv_w=_unshard(gw["ml_conv_w"], 2)[0], ml_conv_b=rep["ml_conv_b"],
              wq=dense("ml_w_q"), wk=dense("ml_w_k"), wv=dense("ml_w_v"),
              wif_q=padn(wif[0:1024]), wif_k=padn(wif[1024:2048]), wif_v=padn(wif[2048:3072]),
              b_if=_lanes(rep["ml_b_if"]), ml_norm_w=rep["ml_norm_w"], ml_skip=rep["ml_skip"],
              wo1=wo[:D_MODEL], wo2=wo[D_MODEL:])
    bb, cc, pw = s5_tables(*[rep[n][0] for n in S5_PARAMS])
    wab = jnp.concatenate([_unshard(gw["s5_w_a"], 1)[0], _unshard(gw["s5_w_b"], 1)[0]], axis=1).astype(bf16)
    w1 = dict(mix_norm=rep["mix_norm"][1:2], bb=bb, cc=cc, pw=pw,
              s5_d=_unshard(gw["s5_d"], 1), wab=wab, b_a=_unshard(gw["s5_b_a"], 1), b_b=_unshard(gw["s5_b_b"], 1))
    return ffn, w0, w1


def _shards(full, axis):
    return jnp.stack(jnp.split(full, N_DEV, axis=axis), axis=0)


def disassemble(g_ffn, g_hy, g_s5, d_final, rep):
    big, small = {}, {}
    for pre in ("ffn1", "ffn2"):
        small[pre + "_norm"] = g_ffn[pre][0]
        for k, nm in enumerate(("_w_gate", "_w_up", "_w_down")):
            big[pre + nm] = g_ffn[pre][1][k]
    big["hy_w_in"] = _shards(win_from_padded(g_hy["win"])[None], 2)
    big["hy_w_out"] = _shards(g_hy["wo"][None], 1)
    big["s5_w_a"] = _shards(g_s5["wab"][None, :, :D_MODEL], 1)
    big["s5_w_b"] = _shards(g_s5["wab"][None, :, D_MODEL:], 1)
    small["mix_norm"] = jnp.concatenate([g_hy["mix_norm"], g_s5["mix_norm"]], axis=0)
    small["ssd_conv_w"] = g_hy["ssd_conv_w"][None]
    small["ssd_conv_b"] = g_hy["ssd_conv_b"]
    small["ssd_dt_bias"] = g_hy["dt_bias"][:, :SSD_HEADS]
    small["ssd_a_log"] = g_hy["a_log"][:, :SSD_HEADS]
    small["ssd_d"] = g_hy["ssd_d"][:, :SSD_HEADS]
    small["ssd_norm_w"] = g_hy["ssd_norm_w"]
    small["ml_conv_w"] = g_hy["ml_conv_w"][None]
    small["ml_conv_b"] = g_hy["ml_conv_b"]
    for nm, key in (("ml_w_q", "wq"), ("ml_w_k", "wk"), ("ml_w_v", "wv")):
        small[nm] = headwise_from_dense(g_hy[key])[None]
    small["ml_w_if"] = g_hy["wif"][None, :, :2 * ML_HEADS]
    small["ml_b_if"] = g_hy["b_if"][:, :2 * ML_HEADS]
    small["ml_norm_w"] = g_hy["ml_norm_w"]
    small["ml_skip"] = g_hy["ml_skip"]
    _, tvjp = jax.vjp(s5_tables, *[rep[n][0] for n in S5_PARAMS])
    for n, g in zip(S5_PARAMS, tvjp((g_s5["bb"], g_s5["cc"], g_s5["pw"]))):
        small[n] = g[None]
    small["s5_d"] = g_s5["s5_d"]
    small["s5_b_a"] = g_s5["b_a"]
    small["s5_b_b"] = g_s5["b_b"]
    small["final_norm"] = d_final.reshape(-1)
    return big, small


ROW = 1024
F32_ROWS = 8


def _piece_rows(size):
    return -(-size // (ROW * F32_ROWS)) * F32_ROWS


def _pack(arrays):
    pieces = []
    for a in arrays:
        flat = a.astype(f32).reshape(-1)
        pieces.append(jnp.pad(flat, (0, _piece_rows(a.size) * ROW - a.size)).reshape(-1, ROW))
    return jnp.concatenate(pieces, axis=0)


def _unpack(buf, shapes):
    out, r0 = [], 0
    lead = buf.shape[:-2]
    for shp in shapes:
        size = math.prod(shp)
        r = _piece_rows(size)
        out.append(buf[..., r0:r0 + r, :].reshape(lead + (-1,))[..., :size].reshape(lead + tuple(shp)))
        r0 += r
    return out


def _tile_rows(r):
    for t in (512, 256, 128, 64, 32, 16, 8):
        if r % t == 0:
            return t
    return r


def _flat2d(a):
    return a.reshape(-1, a.shape[-1])


def kernel(x, ffn1_norm, ffn1_w_gate, ffn1_w_up, ffn1_w_down, mix_norm, ffn2_norm, ffn2_w_gate, ffn2_w_up, ffn2_w_down, hy_w_in, ssd_conv_w, ssd_conv_b, ssd_dt_bias, ssd_a_log, ssd_d, ssd_norm_w, ml_conv_w, ml_conv_b, ml_w_q, ml_w_k, ml_w_v, ml_w_if, ml_b_if, ml_norm_w, ml_skip, hy_w_out, s5_a_re, s5_a_im, s5_log_step, s5_b_re, s5_b_im, s5_c_re, s5_c_im, s5_d, s5_w_a, s5_b_a, s5_w_b, s5_b_b, final_norm, loss_target, m_ffn1_norm, m_ffn1_w_gate, m_ffn1_w_up, m_ffn1_w_down, m_mix_norm, m_ffn2_norm, m_ffn2_w_gate, m_ffn2_w_up, m_ffn2_w_down, m_hy_w_in, m_ssd_conv_w, m_ssd_conv_b, m_ssd_dt_bias, m_ssd_a_log, m_ssd_d, m_ssd_norm_w, m_ml_conv_w, m_ml_conv_b, m_ml_w_q, m_ml_w_k, m_ml_w_v, m_ml_w_if, m_ml_b_if, m_ml_norm_w, m_ml_skip, m_hy_w_out, m_s5_a_re, m_s5_a_im, m_s5_log_step, m_s5_b_re, m_s5_b_im, m_s5_c_re, m_s5_c_im, m_s5_d, m_s5_w_a, m_s5_b_a, m_s5_w_b, m_s5_b_b, m_final_norm, v_ffn1_norm, v_ffn1_w_gate, v_ffn1_w_up, v_ffn1_w_down, v_mix_norm, v_ffn2_norm, v_ffn2_w_gate, v_ffn2_w_up, v_ffn2_w_down, v_hy_w_in, v_ssd_conv_w, v_ssd_conv_b, v_ssd_dt_bias, v_ssd_a_log, v_ssd_d, v_ssd_norm_w, v_ml_conv_w, v_ml_conv_b, v_ml_w_q, v_ml_w_k, v_ml_w_v, v_ml_w_if, v_ml_b_if, v_ml_norm_w, v_ml_skip, v_hy_w_out, v_s5_a_re, v_s5_a_im, v_s5_log_step, v_s5_b_re, v_s5_b_im, v_s5_c_re, v_s5_c_im, v_s5_d, v_s5_w_a, v_s5_b_a, v_s5_w_b, v_s5_b_b, v_final_norm):
    given = dict(locals())
    w = {n: given[n] for n in WEIGHTS}
    mom = {n: given["m_" + n] for n in WEIGHTS}
    var = {n: given["v_" + n] for n in WEIGHTS}
    bl, seq, d = x.shape
    me = 4 * lax.axis_index("x") + 2 * lax.axis_index("y") + lax.axis_index("c")

    res = exchange("gather_weights", [(w[n].astype(bf16), "gather") for n in BIG] + [(_pack([w[n] for n in SMALL_SHARDED]), "gather")])
    gw = dict(zip(BIG, res[:-1]))
    gw.update(zip(SMALL_SHARDED, _unpack(res[-1], [w[n].shape for n in SMALL_SHARDED])))
    rep = {n: w[n] for n in WEIGHTS if n not in BIG and n not in SMALL_SHARDED}
    ffn, w0, w1 = assemble(gw, rep)

    loss, dx0, d_final, g_ffn, g_hy, g_s5 = local_step(x.reshape(bl * seq, d), loss_target.reshape(bl * seq, d), ffn, w0, w1,
                                                       final_norm.reshape(1, d), seq)
    big, small = disassemble(g_ffn, g_hy, g_s5, d_final, rep)

    res = exchange("reduce_grads", [(big[n].astype(bf16), "scatter") for n in BIG] + [(_pack([small[n] for n in SMALL]), "gather")])
    big_parts = dict(zip(BIG, res[:-1]))
    small_sum = sum_parts("sum_small", res[-1], tr=_tile_rows(res[-1].shape[1]))

    out_g, out_d, out_m, out_v = {}, {}, {}, {}
    for n in BIG:
        shp = w[n].shape
        w2 = _flat2d(w[n])
        res = adamw("adamw_" + n, big_parts[n].reshape((N_DEV,) + w2.shape), w2, _flat2d(mom[n]), _flat2d(var[n]),
                    tr=_tile_rows(w2.shape[0]))
        out_g[n], out_d[n], out_m[n], out_v[n] = [a.reshape(shp) for a in res]
    g_small = {}
    for n, full in zip(SMALL, _unpack(small_sum, [small[n].shape for n in SMALL])):
        if n in SMALL_SHARDED:
            ax = SMALL_SHARDED[n]
            full = lax.dynamic_slice_in_dim(full, me * w[n].shape[ax], w[n].shape[ax], axis=ax)
        g_small[n] = full
    packs = [_pack([t[n] for n in SMALL]) for t in (g_small, w, mom, var)]
    res = adamw("adamw_small", packs[0][None], packs[1], packs[2], packs[3], tr=_tile_rows(packs[0].shape[0]))
    for dst, a in zip((out_g, out_d, out_m, out_v), res):
        dst.update(zip(SMALL, _unpack(a, [w[n].shape for n in SMALL])))

    total = lax.psum(loss[0, 0], ("x", "y", "c"))
    return (total, dx0.reshape(bl, seq, d), *[out_g[n] for n in WEIGHTS], *[out_d[n] for n in WEIGHTS],
            *[out_m[n] for n in WEIGHTS], *[out_v[n] for n in WEIGHTS])
```

```python
import functools
import math

import jax
import jax.numpy as jnp
from jax import lax
from jax.experimental import pallas as pl
from jax.experimental.pallas import tpu as pltpu

f32 = jnp.float32
bf16 = jnp.bfloat16

N_DEV = 8
D_MODEL = 1024
D_FF = 2816
EPS = 1e-6
FFN_RES = 0.5
CONV_W = 4
SSD_HEADS = 16
SSD_HEAD_DIM = 64
SSD_GROUPS = 2
SSD_STATE = 128
SSD_HG = SSD_HEADS // SSD_GROUPS
SSD_GW = SSD_HG * SSD_HEAD_DIM
CHUNK = 128
ML_HEADS = 4
ML_HD = 256
S5_GROUP = 16
S5_GROUPS = 64
S5_STATE = 64
S5_CB = 8
S5_CH = (S5_GROUPS // S5_CB) * S5_STATE
S5_TL = 128
S5_SUB = 16
LANES = 128
IN_COLS = 4624
PROJ_W = 4864
OFF_Z, OFF_MX, OFF_MZ, OFF_XBC, OFF_DT = 0, 1024, 2048, 3072, 4608
ADAM_LR, ADAM_B1, ADAM_B2, ADAM_EPS, ADAM_WD, ADAM_STEP = 0.001, 0.9, 0.999, 1e-08, 0.01, 10
NEG = -1e30
VMEM_LIMIT = 56 * 1024 * 1024
HI = lax.Precision.HIGHEST


def _cp(n):
    return pltpu.CompilerParams(dimension_semantics=("arbitrary",) * n, vmem_limit_bytes=VMEM_LIMIT)


def _dg(a, b, ca, cb):
    return lax.dot_general(a.astype(bf16), b.astype(bf16), (((ca,), (cb,)), ((), ())), preferred_element_type=f32)


@functools.partial(jax.custom_vjp, nondiff_argnums=(2, 3))
def bdot(a, b, ca, cb):
    return _dg(a, b, ca, cb)


def _bdot_fwd(a, b, ca, cb):
    return _dg(a, b, ca, cb), (a, b)


def _bdot_bwd(ca, cb, res, ct):
    a, b = res
    da = _dg(ct, b, 1, 1 - cb) if ca == 1 else _dg(b, ct, 1 - cb, 1)
    db = _dg(a, ct, 1 - ca, 0) if cb == 0 else _dg(ct, a, 0, 1 - ca)
    return da, db


bdot.defvjp(_bdot_fwd, _bdot_bwd)


def hdot(a, b):
    return jnp.dot(a, b, precision=HI, preferred_element_type=f32)


def _iota(shape, dim):
    return lax.broadcasted_iota(jnp.int32, shape, dim)


def _tri(n):
    return (_iota((n, n), 0) >= _iota((n, n), 1))


@functools.partial(jax.custom_vjp, nondiff_argnums=(1,))
def tshift(x, k):
    return jnp.where(_iota(x.shape, 0) >= k, pltpu.roll(x, k, 0), 0.0)


def _tshift_fwd(x, k):
    return tshift(x, k), None


def _tshift_bwd(k, _, ct):
    n = ct.shape[0]
    return (jnp.where(_iota(ct.shape, 0) < n - k, pltpu.roll(ct, n - k, 0), 0.0),)


tshift.defvjp(_tshift_fwd, _tshift_bwd)


def _lane_pick(a, idx):
    return jnp.sum(jnp.where(_iota(a.shape, 1) == idx, a, 0.0), axis=1, keepdims=True)


def _row_pick(a, idx):
    return jnp.sum(jnp.where(_iota(a.shape, 0) == idx, a, 0.0), axis=0, keepdims=True)


def _silu(x):
    return x * jax.nn.sigmoid(x)


def map_fwd(name, f, grid, ins, in_specs, out_shapes, out_specs):
    n_in = len(ins)

    def body(*refs):
        pids = tuple(pl.program_id(i) for i in range(len(grid)))
        outs = f(pids, *[r[...] for r in refs[:n_in]])
        for r, o in zip(refs[n_in:], outs):
            r[...] = o.astype(r.dtype)

    return pl.pallas_call(body, name=name, grid=grid, in_specs=in_specs, out_specs=out_specs,
                          out_shape=out_shapes, compiler_params=_cp(len(grid)))(*ins)


def scan_fwd(name, f, grid, slot_axis, ins, in_specs, out_shapes, out_specs, state_shapes, state_init, save_shapes, save_specs):
    n_in, n_out, n_st = len(ins), len(out_shapes), len(state_shapes)
    n_slots = grid[slot_axis]
    cax = len(grid) - 1 if slot_axis != len(grid) - 1 else len(grid) - 2

    def body(*refs):
        pids = tuple(pl.program_id(i) for i in range(len(grid)))
        in_refs, out_refs = refs[:n_in], refs[n_in:n_in + n_out]
        save_refs = refs[n_in + n_out:n_in + n_out + n_st]
        st_refs = refs[n_in + n_out + n_st:]
        slot = pids[slot_axis]

        @pl.when(pids[cax] == 0)
        def _():
            for s, init in zip(st_refs, state_init):
                s[slot] = jnp.full(s.shape[1:], init, f32)

        states = tuple(s[slot] for s in st_refs)
        for sv, st in zip(save_refs, states):
            sv[...] = st.reshape(sv.shape)
        outs, new = f(pids, states, *[r[...] for r in in_refs])
        for r, o in zip(out_refs, outs):
            r[...] = o.astype(r.dtype)
        for s, v in zip(st_refs, new):
            s[slot] = v

    scratch = [pltpu.VMEM((n_slots,) + tuple(s), f32) for s in state_shapes]
    return pl.pallas_call(body, name=name, grid=grid, in_specs=in_specs, out_specs=list(out_specs) + list(save_specs),
                          out_shape=list(out_shapes) + list(save_shapes), scratch_shapes=scratch,
                          compiler_params=_cp(len(grid)))(*ins)


def scan_bwd(name, f, grid, slot_axis, ins, in_specs, saves, save_specs, cts, ct_specs, state_shapes, wrt, acc_first):
    n_in, n_st, n_ct = len(ins), len(saves), len(cts)
    n_slots = grid[slot_axis]
    cax = len(grid) - 1 if slot_axis != len(grid) - 1 else len(grid) - 2

    def body(*refs):
        pids = tuple(pl.program_id(i) for i in range(len(grid)))
        in_refs = refs[:n_in]
        save_refs = refs[n_in:n_in + n_st]
        ct_refs = refs[n_in + n_st:n_in + n_st + n_ct]
        out_refs = refs[n_in + n_st + n_ct:n_in + n_st + n_ct + len(wrt)]
        dst_refs = refs[n_in + n_st + n_ct + len(wrt):]
        slot = pids[slot_axis]

        @pl.when(pids[cax] == 0)
        def _():
            for s in dst_refs:
                s[slot] = jnp.zeros(s.shape[1:], f32)

        vals = [r[...] for r in in_refs]
        states = tuple(sv[...].reshape(shp) for sv, shp in zip(save_refs, state_shapes))
        ctv = tuple(r[...].astype(f32) for r in ct_refs)
        dnew = tuple(s[slot] for s in dst_refs)

        def g(st, *dv):
            full = list(vals)
            for i, v in zip(wrt, dv):
                full[i] = v
            outs, new = f(pids, st, *full)
            return tuple(outs), tuple(new)

        _, vjp = jax.vjp(g, states, *[vals[i] for i in wrt])
        grads = vjp((ctv, dnew))
        for s, v in zip(dst_refs, grads[0]):
            s[slot] = v
        for i, o_ref, gr in zip(wrt, out_refs, grads[1:]):
            first = acc_first.get(i)
            if first is None:
                o_ref[...] = gr.astype(o_ref.dtype)
            else:
                @pl.when(first(pids))
                def _():
                    o_ref[...] = jnp.zeros_like(o_ref)
                o_ref[...] += gr

    out_shapes = [jax.ShapeDtypeStruct(ins[i].shape, f32) for i in wrt]
    out_specs = [in_specs[i] for i in wrt]
    scratch = [pltpu.VMEM((n_slots,) + tuple(s), f32) for s in state_shapes]
    return pl.pallas_call(body, name=name, grid=grid, in_specs=list(in_specs) + list(save_specs) + list(ct_specs),
                          out_specs=out_specs, out_shape=out_shapes, scratch_shapes=scratch,
                          compiler_params=_cp(len(grid)))(*ins, *saves, *cts)


def matmul(name, a, b, ca=1, cb=0, add=None, out_dtype=f32, a_off=0, a_width=None, tm=512, tn=512, tk=512, ride=None):
    rider = Rider(ride)
    nr = rider.n
    a_width = a.shape[1] if a_width is None else a_width
    kdim = b.shape[cb]
    n = b.shape[1 - cb]
    m = a.shape[0] if ca == 1 else a_width
    tm, tn, tk = min(tm, m), min(tn, n), min(tk, kdim)
    assert m % tm == 0 and n % tn == 0 and kdim % tk == 0
    nk = kdim // tk
    if ca == 1:
        assert a_off % tk == 0 and a_width == kdim
        koff = a_off // tk
        a_spec = pl.BlockSpec((tm, tk), lambda i, j, k: (i, k + koff))
    else:
        assert a_off % tm == 0 and a.shape[0] == kdim
        ioff = a_off // tm
        a_spec = pl.BlockSpec((tk, tm), lambda i, j, k: (k, i + ioff))
    b_spec = pl.BlockSpec((tk, tn), lambda i, j, k: (k, j)) if cb == 0 else pl.BlockSpec((tn, tk), lambda i, j, k: (j, k))
    o_spec = pl.BlockSpec((tm, tn), lambda i, j, k: (i, j))
    has_add = add is not None

    n_in = 3 if has_add else 2
    grid = (m // tm, n // tn, nk)

    def body(*refs):
        a_ref, b_ref = refs[0], refs[1]
        add_ref = refs[2] if has_add else None
        r_ins, o_ref = refs[n_in:n_in + nr], refs[n_in + nr]
        r_outs, acc, sems = refs[n_in + nr + 1:n_in + 2 * nr + 1], refs[n_in + 2 * nr + 1], refs[n_in + 2 * nr + 2:]
        rider.start(grid, r_ins, r_outs, sems)
        k = pl.program_id(2)

        @pl.when(k == 0)
        def _():
            acc[...] = add_ref[...].astype(f32) if has_add else jnp.zeros_like(acc)

        acc[...] += _dg(a_ref[...], b_ref[...], ca, cb)

        @pl.when(k == nk - 1)
        def _():
            o_ref[...] = acc[...].astype(o_ref.dtype)

        rider.wait(grid, r_ins, r_outs, sems)

    ins = [a, b] + ([add] if has_add else [])
    specs = [a_spec, b_spec] + ([o_spec] if has_add else [])
    res = pl.pallas_call(body, name=name, grid=grid, in_specs=specs + rider.specs(), out_specs=[o_spec] + rider.specs(),
                         out_shape=[jax.ShapeDtypeStruct((m, n), out_dtype)] + rider.out_shapes(),
                         scratch_shapes=[pltpu.VMEM((tm, tn), f32)] + rider.scratch(), compiler_params=_cp(3))(*ins, *rider.arrays())
    return res if nr else res[0]


def f_rms(pids, x, w):
    r = lax.rsqrt(jnp.mean(x * x, axis=-1, keepdims=True) + EPS)
    return (x * r * w,)


def _row_spec(tm, width, col=0):
    return pl.BlockSpec((tm, width), lambda i: (i, col))


def _par_spec(shape):
    return pl.BlockSpec(shape, lambda *p: (0,) * len(shape))


def rms_fwd(x, w, tm=512):
    t, d = x.shape
    return map_fwd("rms_fwd", f_rms, (t // tm,), [x, w], [_row_spec(tm, d), _par_spec((1, d))],
                   [jax.ShapeDtypeStruct((t, d), f32)], [_row_spec(tm, d)])[0]


def rms_bwd(dys, x, w, dres, tm=512):
    t, d = x.shape
    n = len(dys)

    def body(*refs):
        x_ref, w_ref, dres_ref, dx_ref, dw_ref = refs[n:]
        dy = refs[0][...]
        for r in refs[1:n]:
            dy = dy + r[...]
        _, vjp = jax.vjp(lambda xx, ww: f_rms(None, xx, ww)[0], x_ref[...], w_ref[...])
        dx, dw = vjp(dy)
        dx_ref[...] = dx + dres_ref[...]

        @pl.when(pl.program_id(0) == 0)
        def _():
            dw_ref[...] = jnp.zeros_like(dw_ref)
        dw_ref[...] += dw

    return pl.pallas_call(body, name="rms_bwd", grid=(t // tm,),
                          in_specs=[_row_spec(tm, d)] * (n + 1) + [_par_spec((1, d)), _row_spec(tm, d)],
                          out_specs=[_row_spec(tm, d), _par_spec((1, d))],
                          out_shape=[jax.ShapeDtypeStruct((t, d), f32), jax.ShapeDtypeStruct((1, d), f32)],
                          compiler_params=_cp(1))(*dys, x, w, dres)


def loss_head(x, w, tgt, tm=512):
    t, d = x.shape

    def fl(xx, ww, tt):
        y = f_rms(None, xx, ww)[0]
        return 0.5 * jnp.sum(jnp.mean(jnp.square(y - tt), axis=-1, keepdims=True), axis=0, keepdims=True)

    def body(x_ref, w_ref, t_ref, loss_ref, dx_ref, dw_ref):
        val, vjp = jax.vjp(lambda xx, ww: fl(xx, ww, t_ref[...]), x_ref[...], w_ref[...])
        dx, dw = vjp(jnp.ones((1, 1), f32))
        dx_ref[...] = dx

        @pl.when(pl.program_id(0) == 0)
        def _():
            dw_ref[...] = jnp.zeros_like(dw_ref)
            loss_ref[...] = jnp.zeros_like(loss_ref)
        dw_ref[...] += dw
        loss_ref[...] += val

    return pl.pallas_call(body, name="loss_head", grid=(t // tm,),
                          in_specs=[_row_spec(tm, d), _par_spec((1, d)), _row_spec(tm, d)],
                          out_specs=[_par_spec((1, 1)), _row_spec(tm, d), _par_spec((1, d))],
                          out_shape=[jax.ShapeDtypeStruct((1, 1), f32), jax.ShapeDtypeStruct((t, d), f32),
                                     jax.ShapeDtypeStruct((1, d), f32)],
                          compiler_params=_cp(1))(x, w, tgt)


def ffn_fwd(x, nw, wg, wu, wd, layer, tm=512, ride=None):
    t, d = x.shape
    ns, _, _, fs = wg.shape
    rider = Rider(ride)
    nr = rider.n
    grid = (t // tm, ns)

    def body(*refs):
        x_ref, nw_ref, wg_ref, wu_ref, wd_ref = refs[:5]
        r_ins = refs[5:5 + nr]
        xo_ref, h_ref, g_ref, u_ref = refs[5 + nr:9 + nr]
        r_outs, acc, sems = refs[9 + nr:9 + 2 * nr], refs[9 + 2 * nr], refs[10 + 2 * nr:]
        rider.start(grid, r_ins, r_outs, sems)
        j = pl.program_id(1)

        @pl.when(j == 0)
        def _():
            h_ref[...] = f_rms(None, x_ref[...], nw_ref[...])[0].astype(bf16)
            acc[...] = jnp.zeros_like(acc)

        h = h_ref[...]
        g = jnp.dot(h, wg_ref[0, 0], preferred_element_type=f32)
        u = jnp.dot(h, wu_ref[0, 0], preferred_element_type=f32)
        g_ref[0] = g
        u_ref[0] = u
        acc[...] += jnp.dot((_silu(g) * u).astype(bf16), wd_ref[0, 0], preferred_element_type=f32)

        @pl.when(j == ns - 1)
        def _():
            xo_ref[...] = x_ref[...] + FFN_RES * acc[...]

        rider.wait(grid, r_ins, r_outs, sems)

    row = pl.BlockSpec((tm, d), lambda i, j: (i, 0))
    wcol = pl.BlockSpec((1, 1, d, fs), lambda i, j: (j, layer, 0, 0))
    wrow = pl.BlockSpec((1, 1, fs, d), lambda i, j: (j, layer, 0, 0))
    act = pl.BlockSpec((1, tm, fs), lambda i, j: (j, i, 0))
    return pl.pallas_call(body, name="ffn_fwd", grid=grid,
                          in_specs=[row, pl.BlockSpec((1, d), lambda i, j: (0, 0)), wcol, wcol, wrow] + rider.specs(),
                          out_specs=[row, row, act, act] + rider.specs(),
                          out_shape=[jax.ShapeDtypeStruct((t, d), f32), jax.ShapeDtypeStruct((t, d), bf16),
                                     jax.ShapeDtypeStruct((ns, t, fs), f32), jax.ShapeDtypeStruct((ns, t, fs), f32)]
                          + rider.out_shapes(),
                          scratch_shapes=[pltpu.VMEM((tm, d), f32)] + rider.scratch(),
                          compiler_params=_cp(2))(x, nw, wg, wu, wd, *rider.arrays())


def ffn_bwd_act(dy, x, nw, g, u, wg, wu, wd, layer, tm=512, ride=None):
    t, d = x.shape
    ns, _, _, fs = wg.shape
    rider = Rider(ride)
    nr = rider.n
    grid = (t // tm, ns)

    def body(*refs):
        dy_ref, x_ref, nw_ref, g_ref, u_ref, wg_ref, wu_ref, wd_ref = refs[:8]
        r_ins = refs[8:8 + nr]
        dx_ref, dnw_ref, dg_ref, du_ref, a_ref = refs[8 + nr:13 + nr]
        r_outs, acc, sems = refs[13 + nr:13 + 2 * nr], refs[13 + 2 * nr], refs[14 + 2 * nr:]
        rider.start(grid, r_ins, r_outs, sems)
        i, j = pl.program_id(0), pl.program_id(1)

        @pl.when(j == 0)
        def _():
            acc[...] = jnp.zeros_like(acc)

        dyh = (FFN_RES * dy_ref[...]).astype(bf16)
        da = _dg(dyh, wd_ref[0, 0], 1, 1)
        gg, uu = g_ref[0], u_ref[0]
        sg = jax.nn.sigmoid(gg)
        si = gg * sg
        dgv = (da * uu * (sg * (1.0 + gg * (1.0 - sg)))).astype(bf16)
        duv = (da * si).astype(bf16)
        dg_ref[0] = dgv
        du_ref[0] = duv
        a_ref[0] = (si * uu).astype(bf16)
        acc[...] += _dg(dgv, wg_ref[0, 0], 1, 1) + _dg(duv, wu_ref[0, 0], 1, 1)

        @pl.when(j == ns - 1)
        def _():
            _, vjp = jax.vjp(lambda xx, ww: f_rms(None, xx, ww)[0], x_ref[...], nw_ref[...])
            dx, dw = vjp(acc[...])
            dx_ref[...] = dx + dy_ref[...]

            @pl.when(i == 0)
            def _():
                dnw_ref[...] = jnp.zeros_like(dnw_ref)
            dnw_ref[...] += dw

        rider.wait(grid, r_ins, r_outs, sems)

    row = pl.BlockSpec((tm, d), lambda i, j: (i, 0))
    wcol = pl.BlockSpec((1, 1, d, fs), lambda i, j: (j, layer, 0, 0))
    wrow = pl.BlockSpec((1, 1, fs, d), lambda i, j: (j, layer, 0, 0))
    act = pl.BlockSpec((1, tm, fs), lambda i, j: (j, i, 0))
    par = pl.BlockSpec((1, d), lambda i, j: (0, 0))
    return pl.pallas_call(body, name="ffn_bwd_act", grid=grid,
                          in_specs=[row, row, par, act, act, wcol, wcol, wrow] + rider.specs(),
                          out_specs=[row, par, act, act, act] + rider.specs(),
                          out_shape=[jax.ShapeDtypeStruct((t, d), f32), jax.ShapeDtypeStruct((1, d), f32)]
                          + [jax.ShapeDtypeStruct((ns, t, fs), bf16)] * 3 + rider.out_shapes(),
                          scratch_shapes=[pltpu.VMEM((tm, d), f32)] + rider.scratch(),
                          compiler_params=_cp(2))(dy, x, nw, g, u, wg, wu, wd, *rider.arrays())


def ffn_bwd_w(h, dy, dg, du, a, tk=512, ride=None):
    t, d = h.shape
    ns, _, fs = dg.shape
    nk = t // tk
    rider = Rider(ride)
    nr = rider.n
    grid = (ns, nk)

    def body(*refs):
        h_ref, dy_ref, dg_ref, du_ref, a_ref = refs[:5]
        r_ins = refs[5:5 + nr]
        og, ou, od = refs[5 + nr:8 + nr]
        r_outs = refs[8 + nr:8 + 2 * nr]
        ag, au, ad = refs[8 + 2 * nr:11 + 2 * nr]
        sems = refs[11 + 2 * nr:]
        rider.start(grid, r_ins, r_outs, sems)
        k = pl.program_id(1)

        @pl.when(k == 0)
        def _():
            ag[...] = jnp.zeros_like(ag)
            au[...] = jnp.zeros_like(au)
            ad[...] = jnp.zeros_like(ad)

        hh = h_ref[...]
        ag[...] += _dg(hh, dg_ref[0], 0, 0)
        au[...] += _dg(hh, du_ref[0], 0, 0)
        ad[...] += _dg(a_ref[0], FFN_RES * dy_ref[...], 0, 0)

        @pl.when(k == nk - 1)
        def _():
            og[0, 0] = ag[...].astype(og.dtype)
            ou[0, 0] = au[...].astype(ou.dtype)
            od[0, 0] = ad[...].astype(od.dtype)

        rider.wait(grid, r_ins, r_outs, sems)

    row = pl.BlockSpec((tk, d), lambda j, k: (k, 0))
    act = pl.BlockSpec((1, tk, fs), lambda j, k: (j, k, 0))
    wcol = pl.BlockSpec((1, 1, d, fs), lambda j, k: (j, 0, 0, 0))
    wrow = pl.BlockSpec((1, 1, fs, d), lambda j, k: (j, 0, 0, 0))
    return pl.pallas_call(body, name="ffn_bwd_w", grid=grid, in_specs=[row, row, act, act, act] + rider.specs(),
                          out_specs=[wcol, wcol, wrow] + rider.specs(),
                          out_shape=[jax.ShapeDtypeStruct((ns, 1, d, fs), bf16)] * 2
                          + [jax.ShapeDtypeStruct((ns, 1, fs, d), bf16)] + rider.out_shapes(),
                          scratch_shapes=[pltpu.VMEM((d, fs), f32), pltpu.VMEM((d, fs), f32), pltpu.VMEM((fs, d), f32)]
                          + rider.scratch(),
                          compiler_params=_cp(2))(h, dy, dg, du, a, *rider.arrays())


def f_conv(pids, x, w, b):
    y = b + x * w[CONV_W - 1:CONV_W, :]
    for j in range(CONV_W - 1):
        y = y + tshift(x, CONV_W - 1 - j) * w[j:j + 1, :]
    return (_silu(y),)


def _conv_specs(seq, col0, cb):
    xs = pl.BlockSpec((seq, cb), lambda c, b: (b, col0 + c))
    ws = pl.BlockSpec((CONV_W, cb), lambda c, b: (0, c))
    bs = pl.BlockSpec((1, cb), lambda c, b: (0, c))
    ys = pl.BlockSpec((seq, cb), lambda c, b: (b, c))
    return xs, ws, bs, ys


def conv_fwd(name, src, col_off, w, b, seq, cb=256):
    t = src.shape[0]
    c = w.shape[1]
    xs, ws, bs, ys = _conv_specs(seq, col_off // cb, cb)
    return map_fwd(name, f_conv, (c // cb, t // seq), [src, w, b], [xs, ws, bs],
                   [jax.ShapeDtypeStruct((t, c), f32)], [ys])[0]


def conv_bwd(name, dy, src, col_off, w, b, seq, cb=256):
    t = src.shape[0]
    c = w.shape[1]
    xs, ws, bs, ys = _conv_specs(seq, col_off // cb, cb)

    def body(x_ref, w_ref, b_ref, dy_ref, dx_ref, dw_ref, db_ref):
        _, vjp = jax.vjp(lambda xx, ww, bb: f_conv(None, xx, ww, bb)[0], x_ref[...], w_ref[...], b_ref[...])
        dx, dw, db = vjp(dy_ref[...])
        dx_ref[...] = dx

        @pl.when(pl.program_id(1) == 0)
        def _():
            dw_ref[...] = jnp.zeros_like(dw_ref)
            db_ref[...] = jnp.zeros_like(db_ref)
        dw_ref[...] += dw
        db_ref[...] += db

    return pl.pallas_call(body, name=name, grid=(c // cb, t // seq), in_specs=[xs, ws, bs, ys], out_specs=[ys, ws, bs],
                          out_shape=[jax.ShapeDtypeStruct((t, c), f32), jax.ShapeDtypeStruct(w.shape, f32),
                                     jax.ShapeDtypeStruct(b.shape, f32)], compiler_params=_cp(2))(src, w, b, dy)


def f_ssd(pids, states, xs, dtraw, bm, cm, a_log, dt_bias, d_skip):
    g = pids[2]
    (hn,) = states
    l = xs.shape[0]
    head_of_lane = _iota((LANES, SSD_GW), 1) // SSD_HEAD_DIM + SSD_HG * g
    expand = (_iota((LANES, SSD_GW), 0) == head_of_lane).astype(f32)
    tri = _tri(l)
    dt = jax.nn.softplus(dtraw + dt_bias)
    adt = dt * (-jnp.exp(a_log))
    cs = hdot(tri.astype(f32), adt)
    cst = cs.T
    cs_last = cs[l - 1:l, :]
    dt_e, cs_e, csl_e = hdot(dt, expand), hdot(cs, expand), hdot(cs_last, expand)
    xd = xs * dt_e
    gmat = bdot(cm, bm, 1, 1)
    half = _iota((l, LANES), 1) < SSD_HEAD_DIM
    blocks = []
    for pair in range(SSD_HG // 2):
        xb = xd[:, pair * LANES:(pair + 1) * LANES]
        res = []
        for sub in range(2):
            hid = SSD_HG * g + 2 * pair + sub
            col, row = _lane_pick(cs, hid), _row_pick(cst, hid)
            lm = jnp.exp(jnp.where(tri, col - row, NEG))
            res.append(bdot(gmat * lm, xb, 1, 0))
        blocks.append(jnp.where(half, res[0], res[1]))
    y = jnp.concatenate(blocks, axis=1)
    y = y + jnp.exp(cs_e) * bdot(cm, hn, 1, 0)
    y = y + hdot(d_skip, expand) * xs
    hn_new = jnp.exp(csl_e) * hn + bdot(bm, jnp.exp(csl_e - cs_e) * xd, 0, 0)
    return (y,), (hn_new,)


def _ssd_specs(seq, nch, rev):
    cc = (lambda c: nch - 1 - c) if rev else (lambda c: c)
    xs = pl.BlockSpec((CHUNK, SSD_GW), lambda b, c, g: (b * nch + cc(c), g))
    dt = pl.BlockSpec((CHUNK, LANES), lambda b, c, g: (b * nch + cc(c), OFF_DT // LANES))
    bm = pl.BlockSpec((CHUNK, SSD_STATE), lambda b, c, g: (b * nch + cc(c), 1024 // SSD_STATE + g))
    cm = pl.BlockSpec((CHUNK, SSD_STATE), lambda b, c, g: (b * nch + cc(c), 1024 // SSD_STATE + SSD_GROUPS + g))
    par = pl.BlockSpec((1, LANES), lambda b, c, g: (0, 0))
    sv = pl.BlockSpec((1, 1, SSD_STATE, SSD_GW), lambda b, c, g: (b * nch + cc(c), g, 0, 0))
    ddt = pl.BlockSpec((CHUNK, LANES), lambda b, c, g: (b * nch + cc(c), 0))
    dbc = pl.BlockSpec((CHUNK, SSD_STATE), lambda b, c, g: (b * nch + cc(c), g))
    return xs, dt, bm, cm, par, sv, ddt, dbc


def ssd_fwd(xbc, proj, a_log, dt_bias, d_skip, seq):
    t = xbc.shape[0]
    nch = seq // CHUNK
    xs, dt, bm, cm, par, sv, _, _ = _ssd_specs(seq, nch, False)
    grid = (t // seq, nch, SSD_GROUPS)
    y, hsave = scan_fwd("ssd_fwd", f_ssd, grid, 2, [xbc, proj, xbc, xbc, a_log, dt_bias, d_skip],
                        [xs, dt, bm, cm, par, par, par], [jax.ShapeDtypeStruct((t, SSD_GROUPS * SSD_GW), f32)], [xs],
                        [(SSD_STATE, SSD_GW)], [0.0],
                        [jax.ShapeDtypeStruct((t // CHUNK, SSD_GROUPS, SSD_STATE, SSD_GW), f32)], [sv])
    return y, hsave


def ssd_bwd(dy, xbc, proj, a_log, dt_bias, d_skip, hsave, seq):
    t = xbc.shape[0]
    nch = seq // CHUNK
    xs, dt, bm, cm, par, sv, ddt, dbc = _ssd_specs(seq, nch, True)
    grid = (t // seq, nch, SSD_GROUPS)

    def body(x_ref, dt_ref, b_ref, c_ref, al_ref, db_ref, ds_ref, h_ref, dy_ref,
             dxbc_x, dxbc_b, dxbc_c, ddt_ref, dal_ref, ddb_ref, dds_ref, dst):
        pids = tuple(pl.program_id(i) for i in range(3))
        slot = pids[2]

        @pl.when(pids[1] == 0)
        def _():
            dst[slot] = jnp.zeros(dst.shape[1:], f32)

        vals = [x_ref[...], dt_ref[...], b_ref[...], c_ref[...], al_ref[...], db_ref[...], ds_ref[...]]

        def gfun(st, *v):
            outs, new = f_ssd(pids, (st,), *v)
            return outs[0], new[0]

        _, vjp = jax.vjp(gfun, h_ref[0, 0], *vals)
        grads = vjp((dy_ref[...], dst[slot]))
        dst[slot] = grads[0]
        dxbc_x[...] = grads[1]
        dxbc_b[...] = grads[3]
        dxbc_c[...] = grads[4]

        @pl.when(slot == 0)
        def _():
            ddt_ref[...] = jnp.zeros_like(ddt_ref)
        ddt_ref[...] += grads[2]
        first = jnp.logical_and(jnp.logical_and(pids[0] == 0, pids[1] == 0), slot == 0)

        @pl.when(first)
        def _():
            dal_ref[...] = jnp.zeros_like(dal_ref)
            ddb_ref[...] = jnp.zeros_like(ddb_ref)
            dds_ref[...] = jnp.zeros_like(dds_ref)
        dal_ref[...] += grads[5]
        ddb_ref[...] += grads[6]
        dds_ref[...] += grads[7]

    bc_shape = jax.ShapeDtypeStruct((t, SSD_GROUPS * SSD_STATE), f32)
    par_shape = jax.ShapeDtypeStruct((1, LANES), f32)
    outs = pl.pallas_call(body, name="ssd_bwd", grid=grid, in_specs=[xs, dt, bm, cm, par, par, par, sv, xs],
                          out_specs=[xs, dbc, dbc, ddt, par, par, par],
                          out_shape=[jax.ShapeDtypeStruct((t, SSD_GROUPS * SSD_GW), f32),
                                     bc_shape, bc_shape, jax.ShapeDtypeStruct((t, LANES), f32),
                                     par_shape, par_shape, par_shape],
                          scratch_shapes=[pltpu.VMEM((SSD_GROUPS, SSD_STATE, SSD_GW), f32)],
                          compiler_params=_cp(3))(xbc, proj, xbc, xbc, a_log, dt_bias, d_skip, hsave, dy)
    return outs


def f_ssd_epi(pids, y, z, nw):
    yg = y * _silu(z)
    hw = yg.shape[1] // SSD_GROUPS
    parts = []
    for g in range(SSD_GROUPS):
        p = yg[:, g * hw:(g + 1) * hw]
        parts.append(p * lax.rsqrt(jnp.mean(p * p, axis=-1, keepdims=True) + EPS))
    return (jnp.concatenate(parts, axis=1) * nw,)


def f_ml_epi(pids, hm, xc, mz, nw, skip):
    parts = []
    for h in range(ML_HEADS):
        p = hm[:, h * ML_HD:(h + 1) * ML_HD]
        mu = jnp.mean(p, axis=-1, keepdims=True)
        var = jnp.mean(jnp.square(p - mu), axis=-1, keepdims=True)
        parts.append((p - mu) * lax.rsqrt(var + EPS))
    hn = jnp.concatenate(parts, axis=1) * nw
    return ((hn + skip * xc) * _silu(mz),)


def f_s5_post(pids, ys, u, d_skip):
    return (jax.nn.gelu(ys + d_skip * u),)


def f_glu(pids, pab, ba, bb):
    d = ba.shape[1]
    return ((pab[:, :d] + ba) * jax.nn.sigmoid(pab[:, d:] + bb),)


def f_glu_res(pids, pab, xres, ba, bb):
    return (xres + f_glu(pids, pab, ba, bb)[0],)


def rowwise_fwd(name, f, rows, row_cols, pars, out_width, tm=512):
    t = rows[0].shape[0]
    specs = [_row_spec(tm, w, c) for (w, c) in row_cols] + [_par_spec(p.shape) for p in pars]
    return map_fwd(name, f, (t // tm,), list(rows) + list(pars), specs, [jax.ShapeDtypeStruct((t, out_width), f32)],
                   [_row_spec(tm, out_width)])[0]


def rowwise_bwd(name, f, rows, row_cols, pars, dy, tm=256):
    t = rows[0].shape[0]
    n_r, n_p = len(rows), len(pars)
    specs = [_row_spec(tm, w, c) for (w, c) in row_cols] + [_par_spec(p.shape) for p in pars]
    out_w = dy.shape[1]

    def body(*refs):
        vals = [r[...] for r in refs[:n_r + n_p]]
        dy_ref = refs[n_r + n_p]
        outs = refs[n_r + n_p + 1:]
        _, vjp = jax.vjp(lambda *v: f(None, *v)[0], *vals)
        grads = vjp(dy_ref[...])
        for k in range(n_r):
            outs[k][...] = grads[k]

        @pl.when(pl.program_id(0) == 0)
        def _():
            for k in range(n_p):
                outs[n_r + k][...] = jnp.zeros_like(outs[n_r + k])
        for k in range(n_p):
            outs[n_r + k][...] += grads[n_r + k]

    out_shapes = [jax.ShapeDtypeStruct((t, w), f32) for (w, c) in row_cols] + [jax.ShapeDtypeStruct(p.shape, f32) for p in pars]
    out_specs = [_row_spec(tm, w) for (w, c) in row_cols] + [_par_spec(p.shape) for p in pars]
    return pl.pallas_call(body, name=name, grid=(t // tm,), in_specs=specs + [_row_spec(tm, out_w)], out_specs=out_specs,
                          out_shape=out_shapes, compiler_params=_cp(1))(*rows, *pars, dy)


def f_ml(pids, states, q, k, v, g1, g2, g3, b_if):
    h = pids[2]
    cst, nst, mst = states
    l = q.shape[0]
    gt = g1 + g2 + g3 + b_if
    k = k * (1.0 / math.sqrt(ML_HD))
    tri = _tri(l)
    bc_all = hdot(tri.astype(f32), jax.nn.log_sigmoid(gt))
    bcum, ig = _lane_pick(bc_all, ML_HEADS + h), _lane_pick(gt, h)
    bcum_t, ig_t = _row_pick(bc_all.T, ML_HEADS + h), _row_pick(gt.T, h)
    b_last = bcum[l - 1:l, :]
    dlog = jnp.where(tri, bcum - bcum_t + ig_t, NEG)
    ws = b_last - bcum + ig
    m_prev = mst[:, 0:1]
    m_new = lax.stop_gradient(jnp.maximum(b_last + m_prev, jnp.max(ws, axis=0, keepdims=True)))
    decay = jnp.exp(b_last + m_prev - m_new)
    wts = jnp.exp(ws - m_new)
    c_new = decay * cst + bdot(wts * v, k, 0, 0)
    n_new = decay * nst + jnp.sum(wts * k, axis=0, keepdims=True)
    m_inter = bcum + m_prev
    m_t = lax.stop_gradient(jnp.maximum(jnp.max(dlog, axis=1, keepdims=True), m_inter))
    scores = bdot(q, k, 1, 1) * jnp.exp(dlog - m_t)
    inter_w = jnp.exp(m_inter - m_t)
    num = bdot(scores, v, 1, 0) + inter_w * bdot(q, cst, 1, 1)
    den = jnp.sum(scores, axis=1, keepdims=True) + inter_w * jnp.sum(q * nst, axis=1, keepdims=True)
    hout = num / jnp.maximum(jnp.abs(den), jnp.exp(-m_t))
    return (hout,), (c_new, n_new, jnp.broadcast_to(m_new, mst.shape))


def _ml_specs(nch, rev):
    cc = (lambda c: nch - 1 - c) if rev else (lambda c: c)
    hd = pl.BlockSpec((CHUNK, ML_HD), lambda b, c, h: (b * nch + cc(c), h))
    gt = pl.BlockSpec((CHUNK, LANES), lambda b, c, h: (b * nch + cc(c), 0))
    par = pl.BlockSpec((1, LANES), lambda b, c, h: (0, 0))
    sc = pl.BlockSpec((1, 1, ML_HD, ML_HD), lambda b, c, h: (b * nch + cc(c), h, 0, 0))
    sn = pl.BlockSpec((1, 1, 1, ML_HD), lambda b, c, h: (b * nch + cc(c), h, 0, 0))
    sm = pl.BlockSpec((1, 1, 1, LANES), lambda b, c, h: (b * nch + cc(c), h, 0, 0))
    return hd, gt, par, sc, sn, sm


ML_STATE_SHAPES = [(ML_HD, ML_HD), (1, ML_HD), (1, LANES)]


def ml_fwd(q, k, v, g1, g2, g3, b_if, seq):
    t = q.shape[0]
    nch = seq // CHUNK
    hd, gt, par, sc, sn, sm = _ml_specs(nch, False)
    nc = t // CHUNK
    outs = scan_fwd("ml_fwd", f_ml, (t // seq, nch, ML_HEADS), 2, [q, k, v, g1, g2, g3, b_if],
                    [hd, hd, hd, gt, gt, gt, par], [jax.ShapeDtypeStruct((t, ML_HEADS * ML_HD), f32)], [hd],
                    ML_STATE_SHAPES, [0.0, 0.0, NEG],
                    [jax.ShapeDtypeStruct((nc, ML_HEADS, ML_HD, ML_HD), f32), jax.ShapeDtypeStruct((nc, ML_HEADS, 1, ML_HD), f32),
                     jax.ShapeDtypeStruct((nc, ML_HEADS, 1, LANES), f32)], [sc, sn, sm])
    return outs[0], outs[1:]


def ml_bwd(dh, q, k, v, g1, g2, g3, b_if, saves, seq):
    t = q.shape[0]
    nch = seq // CHUNK
    hd, gt, par, sc, sn, sm = _ml_specs(nch, True)

    def f(pids, states, q, k, v, gsum, b_if):
        return f_ml(pids, states, q, k, v, gsum, jnp.zeros_like(gsum), jnp.zeros_like(gsum), b_if)

    def body(q_ref, k_ref, v_ref, g1_ref, g2_ref, g3_ref, b_ref, c_ref, n_ref, m_ref, dh_ref,
             dq_ref, dk_ref, dv_ref, dg_ref, db_ref, dc_s, dn_s):
        pids = tuple(pl.program_id(i) for i in range(3))
        slot = pids[2]

        @pl.when(pids[1] == 0)
        def _():
            dc_s[slot] = jnp.zeros(dc_s.shape[1:], f32)
            dn_s[slot] = jnp.zeros(dn_s.shape[1:], f32)

        gsum = g1_ref[...] + g2_ref[...] + g3_ref[...]
        mst = m_ref[0, 0]

        def gfun(cst, nst, qq, kk, vv, gs, bb):
            outs, new = f(pids, (cst, nst, mst), qq, kk, vv, gs, bb)
            return outs[0], new[0], new[1]

        _, vjp = jax.vjp(gfun, c_ref[0, 0], n_ref[0, 0], q_ref[...], k_ref[...], v_ref[...], gsum, b_ref[...])
        grads = vjp((dh_ref[...], dc_s[slot], dn_s[slot]))
        dc_s[slot] = grads[0]
        dn_s[slot] = grads[1]
        dq_ref[...] = grads[2]
        dk_ref[...] = grads[3]
        dv_ref[...] = grads[4]

        @pl.when(slot == 0)
        def _():
            dg_ref[...] = jnp.zeros_like(dg_ref)
        dg_ref[...] += grads[5]
        first = jnp.logical_and(jnp.logical_and(pids[0] == 0, pids[1] == 0), slot == 0)

        @pl.when(first)
        def _():
            db_ref[...] = jnp.zeros_like(db_ref)
        db_ref[...] += grads[6]

    big = jax.ShapeDtypeStruct((t, ML_HEADS * ML_HD), f32)
    return pl.pallas_call(body, name="ml_bwd", grid=(t // seq, nch, ML_HEADS),
                          in_specs=[hd, hd, hd, gt, gt, gt, par, sc, sn, sm, hd], out_specs=[hd, hd, hd, gt, par],
                          out_shape=[big, big, big, jax.ShapeDtypeStruct((t, LANES), f32), jax.ShapeDtypeStruct((1, LANES), f32)],
                          scratch_shapes=[pltpu.VMEM((ML_HEADS, ML_HD, ML_HD), f32), pltpu.VMEM((ML_HEADS, 1, ML_HD), f32)],
                          compiler_params=_cp(3))(q, k, v, g1, g2, g3, b_if, *saves, dh)


def _block_prefix(z, transpose):
    n = z.shape[0]
    r, c = _iota((n, n), 0), _iota((n, n), 1)
    keep = jnp.logical_and(r // S5_SUB == c // S5_SUB, (c >= r) if transpose else (c <= r))
    m = jnp.where(keep, 1.0, 0.0).astype(bf16)
    hi = z.astype(bf16)
    lo = (z - hi.astype(f32)).astype(bf16)
    return jnp.dot(m, hi, preferred_element_type=f32) + jnp.dot(m, lo, preferred_element_type=f32)


@jax.custom_vjp
def block_prefix(z):
    return _block_prefix(z, False)


block_prefix.defvjp(lambda z: (_block_prefix(z, False), None), lambda _, ct: (_block_prefix(ct, True),))


def _cmul(a, b):
    h = b.shape[1] // 2
    ar, ai, br, bi = a[:, :h], a[:, h:], b[:, :h], b[:, h:]
    return jnp.concatenate([ar * br - ai * bi, ar * bi + ai * br], axis=1)


def f_s5(pids, states, u, bb, cc, tab):
    (carry,) = states
    tl = u.shape[0]
    nsub = tl // S5_SUB
    rep = lambda t: jnp.concatenate([t] * nsub, axis=0)
    p1, p0, q0 = tab[0:S5_SUB], tab[S5_SUB:2 * S5_SUB], tab[2 * S5_SUB:3 * S5_SUB]
    lam_sub = tab[S5_SUB - 1:S5_SUB]
    bu = bdot(u, bb, 1, 0)
    xl = _cmul(rep(p0), block_prefix(_cmul(rep(q0), bu)))
    e, entering = carry, []
    for k in range(nsub):
        entering.append(jnp.broadcast_to(e, (S5_SUB, e.shape[1])))
        e = xl[(k + 1) * S5_SUB - 1:(k + 1) * S5_SUB] + _cmul(lam_sub, e)
    x = xl + _cmul(rep(p1), jnp.concatenate(entering, axis=0))
    y = bdot(x, cc, 1, 0)
    return (y,), (e,)


def _s5_specs(ntl, rev):
    tt = (lambda t: ntl - 1 - t) if rev else (lambda t: t)
    us = pl.BlockSpec((S5_TL, LANES), lambda c, b, t: (b * ntl + tt(t), c))
    bbs = pl.BlockSpec((1, LANES, 2 * S5_CH), lambda c, b, t: (c, 0, 0))
    ccs = pl.BlockSpec((1, 2 * S5_CH, LANES), lambda c, b, t: (c, 0, 0))
    pws = pl.BlockSpec((1, 3 * S5_SUB, 2 * S5_CH), lambda c, b, t: (c, 0, 0))
    sv = pl.BlockSpec((1, 1, 1, 2 * S5_CH), lambda c, b, t: (b * ntl + tt(t), c, 0, 0))
    return us, bbs, ccs, pws, sv


def s5_fwd(u, bb, cc, pw, seq):
    t = u.shape[0]
    ntl = seq // S5_TL
    us, bbs, ccs, pws, sv = _s5_specs(ntl, False)

    def f(pids, states, uu, b3, c3, p3):
        return f_s5(pids, states, uu, b3[0], c3[0], p3[0])

    y, carries = scan_fwd("s5_fwd", f, (S5_CB, t // seq, ntl), 0, [u, bb, cc, pw], [us, bbs, ccs, pws],
                          [jax.ShapeDtypeStruct((t, S5_CB * LANES), f32)], [us], [(1, 2 * S5_CH)], [0.0],
                          [jax.ShapeDtypeStruct((t // S5_TL, S5_CB, 1, 2 * S5_CH), f32)], [sv])
    return y, carries


def s5_bwd(dy, u, bb, cc, pw, carries, seq):
    t = u.shape[0]
    ntl = seq // S5_TL
    us, bbs, ccs, pws, sv = _s5_specs(ntl, True)

    def f(pids, states, uu, b3, c3, p3):
        return f_s5(pids, states, uu, b3[0], c3[0], p3[0])

    first = lambda pids: jnp.logical_and(pids[1] == 0, pids[2] == 0)
    return scan_bwd("s5_bwd", f, (S5_CB, t // seq, ntl), 0, [u, bb, cc, pw], [us, bbs, ccs, pws], [carries], [sv],
                    [dy], [us], [(1, 2 * S5_CH)], [0, 1, 2, 3], {1: first, 2: first, 3: first})


def _adam_math(g, w, m, v):
    m2 = ADAM_B1 * m + (1.0 - ADAM_B1) * g
    v2 = ADAM_B2 * v + (1.0 - ADAM_B2) * jnp.square(g)
    m_hat = m2 / (1.0 - ADAM_B1 ** ADAM_STEP)
    v_hat = v2 / (1.0 - ADAM_B2 ** ADAM_STEP)
    delta = -ADAM_LR * (m_hat / (jnp.sqrt(v_hat) + ADAM_EPS) + ADAM_WD * w)
    return delta, m2, v2


def adamw(name, parts, w, m, v, tr=256):
    n, r, c = parts.shape
    tr = min(tr, r)
    assert r % tr == 0

    def body(p_ref, w_ref, m_ref, v_ref, g_ref, d_ref, m2_ref, v2_ref):
        g = p_ref[0].astype(f32)
        for s in range(1, n):
            g = g + p_ref[s].astype(f32)
        d, m2, v2 = _adam_math(g, w_ref[...], m_ref[...], v_ref[...])
        g_ref[...] = g
        d_ref[...] = d
        m2_ref[...] = m2
        v2_ref[...] = v2

    ps = pl.BlockSpec((n, tr, c), lambda i: (0, i, 0))
    rs = pl.BlockSpec((tr, c), lambda i: (i, 0))
    return pl.pallas_call(body, name=name, grid=(r // tr,), in_specs=[ps, rs, rs, rs], out_specs=[rs] * 4,
                          out_shape=[jax.ShapeDtypeStruct((r, c), f32)] * 4, compiler_params=_cp(1))(parts, w, m, v)


def adamw_layer(name, parts, w, m, v, layer, prev=None, tr=256):
    n, r, c = parts.shape
    nl = w.shape[0]
    tr = min(tr, r)
    assert r % tr == 0 and w.shape[1:] == (r, c)
    n_prev = 0 if prev is None else 4

    def body(*refs):
        p_ref, w_ref, m_ref, v_ref = refs[:4]
        g_ref, d_ref, m2_ref, v2_ref = refs[4 + n_prev:]
        g = p_ref[0].astype(f32)
        for s in range(1, n):
            g = g + p_ref[s].astype(f32)
        d, m2, v2 = _adam_math(g, w_ref[0], m_ref[0], v_ref[0])
        g_ref[0] = g
        d_ref[0] = d
        m2_ref[0] = m2
        v2_ref[0] = v2

    ps = pl.BlockSpec((n, tr, c), lambda i: (0, i, 0))
    rs = pl.BlockSpec((1, tr, c), lambda i: (layer, i, 0))
    anyspec = pl.BlockSpec(memory_space=pl.ANY)
    return pl.pallas_call(body, name=name, grid=(r // tr,), in_specs=[ps, rs, rs, rs] + [anyspec] * n_prev, out_specs=[rs] * 4,
                          out_shape=[jax.ShapeDtypeStruct((nl, r, c), f32)] * 4,
                          input_output_aliases={4 + i: i for i in range(n_prev)},
                          compiler_params=_cp(1))(parts, w, m, v, *(prev or ()))


def sum_parts(name, parts, tr=256):
    n, r, c = parts.shape
    tr = min(tr, r)
    assert r % tr == 0

    def body(p_ref, o_ref):
        g = p_ref[0].astype(f32)
        for s in range(1, n):
            g = g + p_ref[s].astype(f32)
        o_ref[...] = g

    return pl.pallas_call(body, name=name, grid=(r // tr,), in_specs=[pl.BlockSpec((n, tr, c), lambda i: (0, i, 0))],
                          out_specs=pl.BlockSpec((tr, c), lambda i: (i, 0)),
                          out_shape=jax.ShapeDtypeStruct((r, c), f32), compiler_params=_cp(1))(parts)


class Rider:
    def __init__(self, ops):
        self.ops = list(ops or [])
        self.n = len(self.ops)

    def arrays(self):
        return [a for a, _ in self.ops]

    def specs(self):
        return [pl.BlockSpec(memory_space=pl.ANY)] * self.n

    def out_shapes(self):
        return [jax.ShapeDtypeStruct((N_DEV,) + tuple(a.shape) if mode == "gather" else tuple(a.shape), a.dtype)
                for a, mode in self.ops]

    def scratch(self):
        if not self.n:
            return []
        return [pltpu.SemaphoreType.DMA((self.n, N_DEV - 1)), pltpu.SemaphoreType.DMA((self.n, N_DEV - 1)),
                pltpu.SemaphoreType.DMA((self.n,))]

    def _copies(self, ins, outs, sems):
        send_sems, recv_sems, loc_sems = sems
        x, y, c = lax.axis_index("x"), lax.axis_index("y"), lax.axis_index("c")
        me = 4 * x + 2 * y + c
        copies = []
        for k, (_, mode) in enumerate(self.ops):
            src_me = ins[k] if mode == "gather" else ins[k].at[me]
            copies.append(pltpu.make_async_copy(src_me, outs[k].at[me], loc_sems.at[k]))
            for d in range(1, N_DEV):
                px = 1 - x if (d >> 2) & 1 else x
                py = 1 - y if (d >> 1) & 1 else y
                pc = 1 - c if d & 1 else c
                src = ins[k] if mode == "gather" else ins[k].at[4 * px + 2 * py + pc]
                copies.append(pltpu.make_async_remote_copy(
                    src_ref=src, dst_ref=outs[k].at[me], send_sem=send_sems.at[k, d - 1], recv_sem=recv_sems.at[k, d - 1],
                    device_id=(px, py, pc), device_id_type=pl.DeviceIdType.MESH))
        return copies

    def start(self, grid, ins, outs, sems):
        if self.n:
            @pl.when(functools.reduce(jnp.logical_and, [pl.program_id(i) == 0 for i in range(len(grid))]))
            def _():
                for cp in self._copies(ins, outs, sems):
                    cp.start()

    def wait(self, grid, ins, outs, sems):
        if self.n:
            @pl.when(functools.reduce(jnp.logical_and, [pl.program_id(i) == g - 1 for i, g in enumerate(grid)]))
            def _():
                for cp in self._copies(ins, outs, sems):
                    cp.wait()


def exchange(name, ops):
    rider = Rider(ops)
    n = rider.n

    def body(*refs):
        copies = rider._copies(refs[:n], refs[n:2 * n], refs[2 * n:])
        for cp in copies:
            cp.start()
        for cp in copies:
            cp.wait()

    return pl.pallas_call(body, name=name, in_specs=rider.specs(), out_specs=rider.specs(), out_shape=rider.out_shapes(),
                          scratch_shapes=rider.scratch())(*rider.arrays())


def _lanes(v, width=LANES):
    v = v.reshape(1, -1)
    return jnp.pad(v, ((0, 0), (0, width - v.shape[1])))


def win_to_padded(w):
    return jnp.concatenate([w[:, :1024], w[:, 2576:3600], w[:, 3600:4624], w[:, 1024:2560], w[:, 2560:2576],
                            jnp.zeros((w.shape[0], PROJ_W - IN_COLS), w.dtype)], axis=1)


def win_from_padded(wp):
    return jnp.concatenate([wp[:, 0:1024], wp[:, 3072:4608], wp[:, 4608:4624], wp[:, 1024:2048], wp[:, 2048:3072]], axis=1)


def headwise_dense(w):
    nb, o, i = w.shape
    rows = jnp.tile(w.transpose(0, 2, 1).reshape(nb * i, o), (1, nb))
    same = (jnp.arange(nb * i)[:, None] // i) == (jnp.arange(nb * o)[None, :] // o)
    return jnp.where(same, rows, 0.0)


def headwise_from_dense(dd, o=4, i=4):
    nb = dd.shape[0] // i
    idx = (jnp.arange(nb * i)[:, None] // i) * o + jnp.arange(o)[None, :]
    return jnp.take_along_axis(dd, idx, axis=1).reshape(nb, i, o).transpose(0, 2, 1)


def s5_tables(a_re, a_im, log_step, b_re, b_im, c_re, c_im):
    step = jnp.exp(log_step)[:, None]
    j = jnp.arange(S5_SUB, dtype=f32)[:, None, None]
    expo = jnp.concatenate([j + 1.0, j, -j], axis=0)
    mag = jnp.exp(expo * (a_re * step))
    pw_re, pw_im = mag * jnp.cos(expo * (a_im * step)), mag * jnp.sin(expo * (a_im * step))
    lam_re, lam_im = pw_re[0], pw_im[0]
    den = a_re * a_re + a_im * a_im
    coef_re = ((lam_re - 1.0) * a_re + lam_im * a_im) / den
    coef_im = (lam_im * a_re - (lam_re - 1.0) * a_im) / den
    bb_re = coef_re[..., None] * b_re - coef_im[..., None] * b_im
    bb_im = coef_re[..., None] * b_im + coef_im[..., None] * b_re
    gl = S5_GROUPS // S5_CB
    eye = jnp.eye(gl, dtype=f32)

    def blk_b(t):
        t4 = t.transpose(0, 2, 1).reshape(S5_CB, gl, S5_GROUP, S5_STATE)
        return jnp.einsum("kgcn,gh->kgchn", t4, eye).reshape(S5_CB, gl * S5_GROUP, gl * S5_STATE)

    def blk_c(t):
        t4 = t.reshape(S5_CB, gl, S5_GROUP, S5_STATE)
        return jnp.einsum("kgcn,gh->kgnhc", t4, eye).reshape(S5_CB, gl * S5_STATE, gl * S5_GROUP)

    def blk_p(t):
        return t.reshape(t.shape[0], S5_CB, gl * S5_STATE).transpose(1, 0, 2)

    bb = jnp.concatenate([blk_b(bb_re), blk_b(bb_im)], axis=2)
    cc = jnp.concatenate([blk_c(c_re), -blk_c(c_im)], axis=1)
    pw = jnp.concatenate([blk_p(pw_re), blk_p(pw_im)], axis=2)
    return bb, cc, pw


def ffn_step_bwd(dy, x, nw, wts, saved, ride_act=None, ride_w=None):
    h, g, u = saved
    dx, dnw, dg, du, a, *got_act = ffn_bwd_act(dy, x, nw, g, u, *wts, 0, ride=ride_act)
    dwg, dwu, dwd, *got_w = ffn_bwd_w(h, dy, dg, du, a, ride=ride_w)
    return dx, dnw, (dwg, dwu, dwd), got_act, got_w


def hybrid_fwd(x1, p, seq, ride_in=None):
    u = rms_fwd(x1, p["mix_norm"])
    proj, *got_in = matmul("hy_in", u, p["win"], tn=256, ride=ride_in) if ride_in else (matmul("hy_in", u, p["win"], tn=256),)
    xbc = conv_fwd("ssd_conv", proj, OFF_XBC, p["ssd_conv_w"], p["ssd_conv_b"], seq)
    yraw, hsave = ssd_fwd(xbc, proj, p["a_log"], p["dt_bias"], p["ssd_d"], seq)
    yssd = rowwise_fwd("ssd_epi", f_ssd_epi, [yraw, proj], [(D_MODEL, 0), (D_MODEL, OFF_Z // D_MODEL)], [p["ssd_norm_w"]], D_MODEL)
    xc = conv_fwd("ml_conv", proj, OFF_MX, p["ml_conv_w"], p["ml_conv_b"], seq)
    q = matmul("hw_q", xc, p["wq"])
    k = matmul("hw_k", xc, p["wk"])
    v = matmul("hw_v", proj, p["wv"], a_off=OFF_MX, a_width=D_MODEL)
    g1 = matmul("gate_q", q, p["wif_q"])
    g2 = matmul("gate_k", k, p["wif_k"])
    g3 = matmul("gate_v", v, p["wif_v"])
    hm, mlsave = ml_fwd(q, k, v, g1, g2, g3, p["b_if"], seq)
    yml = rowwise_fwd("ml_epi", f_ml_epi, [hm, xc, proj], [(D_MODEL, 0), (D_MODEL, 0), (D_MODEL, OFF_MZ // D_MODEL)],
                      [p["ml_norm_w"], p["ml_skip"]], D_MODEL)
    t = matmul("hy_out1", yssd, p["wo1"], add=x1)
    x2 = matmul("hy_out2", yml, p["wo2"], add=t)
    return x2, (u, proj, xbc, yraw, hsave, yssd, xc, q, k, v, g1, g2, g3, hm, mlsave, yml), got_in


def hybrid_bwd(dx2, x1, p, saved, seq, ride_dwin=None):
    u, proj, xbc, yraw, hsave, yssd, xc, q, k, v, g1, g2, g3, hm, mlsave, yml = saved
    gr = {}
    dyssd = matmul("d_yssd", dx2, p["wo1"], cb=1)
    dyml = matmul("d_yml", dx2, p["wo2"], cb=1)
    gr["wo"] = jnp.concatenate([matmul("dw_o1", yssd, dx2, ca=0), matmul("dw_o2", yml, dx2, ca=0)], axis=0)
    d_hm, d_xc, d_mz, gr["ml_norm_w"], gr["ml_skip"] = rowwise_bwd(
        "ml_epi_bwd", f_ml_epi, [hm, xc, proj], [(D_MODEL, 0), (D_MODEL, 0), (D_MODEL, OFF_MZ // D_MODEL)],
        [p["ml_norm_w"], p["ml_skip"]], dyml)
    dq, dk, dv, dgt, gr["b_if"] = ml_bwd(d_hm, q, k, v, g1, g2, g3, p["b_if"], mlsave, seq)
    dq = matmul("dq_gate", dgt, p["wif_q"], cb=1, add=dq)
    dk = matmul("dk_gate", dgt, p["wif_k"], cb=1, add=dk)
    dv = matmul("dv_gate", dgt, p["wif_v"], cb=1, add=dv)
    gr["wif"] = jnp.concatenate([matmul("dw_if_q", q, dgt, ca=0), matmul("dw_if_k", k, dgt, ca=0),
                                 matmul("dw_if_v", v, dgt, ca=0)], axis=0)
    d_xc = matmul("dxc_q", dq, p["wq"], cb=1, add=d_xc)
    d_xc = matmul("dxc_k", dk, p["wk"], cb=1, add=d_xc)
    gr["wq"] = matmul("dw_q", xc, dq, ca=0)
    gr["wk"] = matmul("dw_k", xc, dk, ca=0)
    gr["wv"] = matmul("dw_v", proj, dv, ca=0, a_off=OFF_MX, a_width=D_MODEL)
    d_mx, gr["ml_conv_w"], gr["ml_conv_b"] = conv_bwd("ml_conv_bwd", d_xc, proj, OFF_MX, p["ml_conv_w"], p["ml_conv_b"], seq)
    d_mx = matmul("dmx_v", dv, p["wv"], cb=1, add=d_mx)
    d_yraw, d_z, gr["ssd_norm_w"] = rowwise_bwd("ssd_epi_bwd", f_ssd_epi, [yraw, proj],
                                                [(D_MODEL, 0), (D_MODEL, OFF_Z // D_MODEL)], [p["ssd_norm_w"]], dyssd)
    d_xs, d_b, d_c, d_dt, gr["a_log"], gr["dt_bias"], gr["ssd_d"] = ssd_bwd(
        d_yraw, xbc, proj, p["a_log"], p["dt_bias"], p["ssd_d"], hsave, seq)
    d_xbc, gr["ssd_conv_w"], gr["ssd_conv_b"] = conv_bwd("ssd_conv_bwd", jnp.concatenate([d_xs, d_b, d_c], axis=1), proj, OFF_XBC,
                                                         p["ssd_conv_w"], p["ssd_conv_b"], seq)
    dproj = jnp.concatenate([d_z, d_mx, d_mz, d_xbc, d_dt, jnp.zeros((d_dt.shape[0], PROJ_W - OFF_DT - LANES), f32)], axis=1)
    dwin, *got_dwin = matmul("dw_in", u, dproj, ca=0, tn=256, ride=ride_dwin) if ride_dwin else (matmul("dw_in", u, dproj, ca=0, tn=256),)
    out_ops = [(_shards(win_from_padded(dwin)[None], 2).astype(bf16), "scatter"), (_shards(gr.pop("wo")[None], 1).astype(bf16), "scatter")]
    du, part_win, part_wo = matmul("d_u", dproj, p["win"], cb=1, tk=256, ride=out_ops)
    dx1, gr["mix_norm"] = rms_bwd([du], x1, p["mix_norm"], dx2)
    return dx1, gr, got_dwin, (part_win, part_wo)


def s5_layer_fwd(x4, p, seq):
    u = rms_fwd(x4, p["mix_norm"])
    ys, carries = s5_fwd(u, p["bb"], p["cc"], p["pw"], seq)
    gg = rowwise_fwd("s5_post", f_s5_post, [ys, u], [(D_MODEL, 0), (D_MODEL, 0)], [p["s5_d"]], D_MODEL)
    pab = matmul("s5_ab", gg, p["wab"])
    x5 = rowwise_fwd("s5_glu", f_glu_res, [pab, x4], [(2 * D_MODEL, 0), (D_MODEL, 0)], [p["b_a"], p["b_b"]], D_MODEL)
    return x5, (u, ys, carries, gg, pab)


def s5_layer_bwd(dx5, x4, p, saved, seq):
    u, ys, carries, gg, pab = saved
    gr = {}
    dpab, gr["b_a"], gr["b_b"] = rowwise_bwd("s5_glu_bwd", f_glu, [pab], [(2 * D_MODEL, 0)], [p["b_a"], p["b_b"]], dx5)
    dgg = matmul("d_gg", dpab, p["wab"], cb=1)
    gr["wab"] = matmul("dw_ab", gg, dpab, ca=0)
    dys, du_a, gr["s5_d"] = rowwise_bwd("s5_post_bwd", f_s5_post, [ys, u], [(D_MODEL, 0), (D_MODEL, 0)], [p["s5_d"]], dgg)
    du_b, gr["bb"], gr["cc"], gr["pw"] = s5_bwd(dys, u, p["bb"], p["cc"], p["pw"], carries, seq)
    dx4, gr["mix_norm"] = rms_bwd([du_a, du_b], x4, p["mix_norm"], dx5)
    return dx4, gr


BIG = ["ffn1_w_gate", "ffn1_w_up", "ffn1_w_down", "ffn2_w_gate", "ffn2_w_up", "ffn2_w_down", "hy_w_in", "hy_w_out", "s5_w_a", "s5_w_b"]
SMALL_SHARDED = {"ssd_conv_w": 2, "ml_conv_w": 2, "ml_w_q": 1, "ml_w_k": 1, "ml_w_v": 1, "ml_w_if": 1, "s5_d": 1, "s5_b_a": 1, "s5_b_b": 1}
WEIGHTS = ["ffn1_norm", "ffn1_w_gate", "ffn1_w_up", "ffn1_w_down", "mix_norm", "ffn2_norm", "ffn2_w_gate", "ffn2_w_up", "ffn2_w_down",
           "hy_w_in", "ssd_conv_w", "ssd_conv_b", "ssd_dt_bias", "ssd_a_log", "ssd_d", "ssd_norm_w", "ml_conv_w", "ml_conv_b",
           "ml_w_q", "ml_w_k", "ml_w_v", "ml_w_if", "ml_b_if", "ml_norm_w", "ml_skip", "hy_w_out", "s5_a_re", "s5_a_im",
           "s5_log_step", "s5_b_re", "s5_b_im", "s5_c_re", "s5_c_im", "s5_d", "s5_w_a", "s5_b_a", "s5_w_b", "s5_b_b", "final_norm"]
SMALL = [n for n in WEIGHTS if n not in BIG]
S5_PARAMS = ["s5_a_re", "s5_a_im", "s5_log_step", "s5_b_re", "s5_b_im", "s5_c_re", "s5_c_im"]


def _unshard(g, axis):
    return jnp.concatenate([g[i] for i in range(N_DEV)], axis=axis)


def assemble(gw, rep):
    padn = lambda w: jnp.pad(w, ((0, 0), (0, LANES - w.shape[1]))).astype(bf16)
    wif = _unshard(gw["ml_w_if"], 1)[0]
    wo = _unshard(gw["hy_w_out"], 1)[0].astype(bf16)
    dense = lambda n: headwise_dense(_unshard(gw[n], 1)[0].astype(f32)).astype(bf16)
    w0 = dict(mix_norm=rep["mix_norm"][0:1],
              win=win_to_padded(_unshard(gw["hy_w_in"], 2)[0]).astype(bf16),
              ssd_conv_w=_unshard(gw["ssd_conv_w"], 2)[0], ssd_conv_b=rep["ssd_conv_b"],
              a_log=_lanes(rep["ssd_a_log"]), dt_bias=_lanes(rep["ssd_dt_bias"]), ssd_d=_lanes(rep["ssd_d"]),
              ssd_norm_w=rep["ssd_norm_w"], ml_conv_w=_unshard(gw["ml_conv_w"], 2)[0], ml_conv_b=rep["ml_conv_b"],
              wq=dense("ml_w_q"), wk=dense("ml_w_k"), wv=dense("ml_w_v"),
              wif_q=padn(wif[0:1024]), wif_k=padn(wif[1024:2048]), wif_v=padn(wif[2048:3072]),
              b_if=_lanes(rep["ml_b_if"]), ml_norm_w=rep["ml_norm_w"], ml_skip=rep["ml_skip"],
              wo1=wo[:D_MODEL], wo2=wo[D_MODEL:])
    bb, cc, pw = s5_tables(*[rep[n][0] for n in S5_PARAMS])
    wab = jnp.concatenate([_unshard(gw["s5_w_a"], 1)[0], _unshard(gw["s5_w_b"], 1)[0]], axis=1).astype(bf16)
    w1 = dict(mix_norm=rep["mix_norm"][1:2], bb=bb, cc=cc, pw=pw,
              s5_d=_unshard(gw["s5_d"], 1), wab=wab, b_a=_unshard(gw["s5_b_a"], 1), b_b=_unshard(gw["s5_b_b"], 1))
    return w0, w1


def _shards(full, axis):
    return jnp.stack(jnp.split(full, N_DEV, axis=axis), axis=0)


def small_grads(g_norms, g_hy, g_s5, d_final, rep):
    small = dict(g_norms)
    small["mix_norm"] = jnp.concatenate([g_hy["mix_norm"], g_s5["mix_norm"]], axis=0)
    small["ssd_conv_w"] = g_hy["ssd_conv_w"][None]
    small["ssd_conv_b"] = g_hy["ssd_conv_b"]
    small["ssd_dt_bias"] = g_hy["dt_bias"][:, :SSD_HEADS]
    small["ssd_a_log"] = g_hy["a_log"][:, :SSD_HEADS]
    small["ssd_d"] = g_hy["ssd_d"][:, :SSD_HEADS]
    small["ssd_norm_w"] = g_hy["ssd_norm_w"]
    small["ml_conv_w"] = g_hy["ml_conv_w"][None]
    small["ml_conv_b"] = g_hy["ml_conv_b"]
    for nm, key in (("ml_w_q", "wq"), ("ml_w_k", "wk"), ("ml_w_v", "wv")):
        small[nm] = headwise_from_dense(g_hy[key])[None]
    small["ml_w_if"] = g_hy["wif"][None, :, :2 * ML_HEADS]
    small["ml_b_if"] = g_hy["b_if"][:, :2 * ML_HEADS]
    small["ml_norm_w"] = g_hy["ml_norm_w"]
    small["ml_skip"] = g_hy["ml_skip"]
    _, tvjp = jax.vjp(s5_tables, *[rep[n][0] for n in S5_PARAMS])
    for n, g in zip(S5_PARAMS, tvjp((g_s5["bb"], g_s5["cc"], g_s5["pw"]))):
        small[n] = g[None]
    small["s5_d"] = g_s5["s5_d"]
    small["s5_b_a"] = g_s5["b_a"]
    small["s5_b_b"] = g_s5["b_b"]
    small["final_norm"] = d_final.reshape(-1)
    return small


ROW = 1024
F32_ROWS = 8


def _piece_rows(size):
    return -(-size // (ROW * F32_ROWS)) * F32_ROWS


def _pack(arrays):
    pieces = []
    for a in arrays:
        flat = a.astype(f32).reshape(-1)
        pieces.append(jnp.pad(flat, (0, _piece_rows(a.size) * ROW - a.size)).reshape(-1, ROW))
    return jnp.concatenate(pieces, axis=0)


def _unpack(buf, shapes):
    out, r0 = [], 0
    lead = buf.shape[:-2]
    for shp in shapes:
        size = math.prod(shp)
        r = _piece_rows(size)
        out.append(buf[..., r0:r0 + r, :].reshape(lead + (-1,))[..., :size].reshape(lead + tuple(shp)))
        r0 += r
    return out


def _tile_rows(r):
    for t in (512, 256, 128, 64, 32, 16, 8):
        if r % t == 0:
            return t
    return r


def _flat2d(a):
    return a.reshape(-1, a.shape[-1])


def kernel(x, ffn1_norm, ffn1_w_gate, ffn1_w_up, ffn1_w_down, mix_norm, ffn2_norm, ffn2_w_gate, ffn2_w_up, ffn2_w_down, hy_w_in, ssd_conv_w, ssd_conv_b, ssd_dt_bias, ssd_a_log, ssd_d, ssd_norm_w, ml_conv_w, ml_conv_b, ml_w_q, ml_w_k, ml_w_v, ml_w_if, ml_b_if, ml_norm_w, ml_skip, hy_w_out, s5_a_re, s5_a_im, s5_log_step, s5_b_re, s5_b_im, s5_c_re, s5_c_im, s5_d, s5_w_a, s5_b_a, s5_w_b, s5_b_b, final_norm, loss_target, m_ffn1_norm, m_ffn1_w_gate, m_ffn1_w_up, m_ffn1_w_down, m_mix_norm, m_ffn2_norm, m_ffn2_w_gate, m_ffn2_w_up, m_ffn2_w_down, m_hy_w_in, m_ssd_conv_w, m_ssd_conv_b, m_ssd_dt_bias, m_ssd_a_log, m_ssd_d, m_ssd_norm_w, m_ml_conv_w, m_ml_conv_b, m_ml_w_q, m_ml_w_k, m_ml_w_v, m_ml_w_if, m_ml_b_if, m_ml_norm_w, m_ml_skip, m_hy_w_out, m_s5_a_re, m_s5_a_im, m_s5_log_step, m_s5_b_re, m_s5_b_im, m_s5_c_re, m_s5_c_im, m_s5_d, m_s5_w_a, m_s5_b_a, m_s5_w_b, m_s5_b_b, m_final_norm, v_ffn1_norm, v_ffn1_w_gate, v_ffn1_w_up, v_ffn1_w_down, v_mix_norm, v_ffn2_norm, v_ffn2_w_gate, v_ffn2_w_up, v_ffn2_w_down, v_hy_w_in, v_ssd_conv_w, v_ssd_conv_b, v_ssd_dt_bias, v_ssd_a_log, v_ssd_d, v_ssd_norm_w, v_ml_conv_w, v_ml_conv_b, v_ml_w_q, v_ml_w_k, v_ml_w_v, v_ml_w_if, v_ml_b_if, v_ml_norm_w, v_ml_skip, v_hy_w_out, v_s5_a_re, v_s5_a_im, v_s5_log_step, v_s5_b_re, v_s5_b_im, v_s5_c_re, v_s5_c_im, v_s5_d, v_s5_w_a, v_s5_b_a, v_s5_w_b, v_s5_b_b, v_final_norm):
    given = dict(locals())
    w = {n: given[n] for n in WEIGHTS}
    mom = {n: given["m_" + n] for n in WEIGHTS}
    var = {n: given["v_" + n] for n in WEIGHTS}
    bl, seq, d = x.shape
    me = 4 * lax.axis_index("x") + 2 * lax.axis_index("y") + lax.axis_index("c")

    x0, tgt = x.reshape(bl * seq, d), loss_target.reshape(bl * seq, d)
    rep = {n: w[n] for n in WEIGHTS if n not in BIG and n not in SMALL_SHARDED}
    ffn_w = ("_w_gate", "_w_up", "_w_down")

    def ffn_gather(pre, l):
        return [(w[pre + s][l:l + 1].astype(bf16), "gather") for s in ffn_w]

    def scatter(parts):
        return [(p, "scatter") for p in parts]

    wf10 = exchange("gather_ffn1_l0", ffn_gather("ffn1", 0))
    mixer_ops = [(w[n].astype(bf16), "gather") for n in ("hy_w_in", "hy_w_out", "s5_w_a", "s5_w_b")]
    mixer_ops.append((_pack([w[n] for n in SMALL_SHARDED]), "gather"))
    x1, *rest = ffn_fwd(x0, ffn1_norm[0:1], *wf10, 0, ride=mixer_ops)
    sv10, got = rest[:3], rest[3:]
    gw = dict(zip(("hy_w_in", "hy_w_out", "s5_w_a", "s5_w_b"), got[:4]))
    gw.update(zip(SMALL_SHARDED, _unpack(got[4], [w[n].shape for n in SMALL_SHARDED])))
    w0, w1 = assemble(gw, rep)
    x2, sv_h, got = hybrid_fwd(x1, w0, seq, ride_in=ffn_gather("ffn2", 0) + ffn_gather("ffn1", 1))
    wf20, wf11 = got[:3], got[3:]
    x3, *rest = ffn_fwd(x2, ffn2_norm[0:1], *wf20, 0, ride=ffn_gather("ffn2", 1))
    sv20, wf21 = rest[:3], rest[3:]
    x4, *sv11 = ffn_fwd(x3, ffn1_norm[1:2], *wf11, 0)
    x5, sv_s = s5_layer_fwd(x4, w1, seq)
    x6, *sv21 = ffn_fwd(x5, ffn2_norm[1:2], *wf21, 0)
    loss, dx6, d_final = loss_head(x6, final_norm.reshape(1, d), tgt)

    dx5, dn21, dw21, _, _ = ffn_step_bwd(dx6, x5, ffn2_norm[1:2], wf21, sv21)
    dx4, g_s5 = s5_layer_bwd(dx5, x4, w1, sv_s, seq)
    dwab = g_s5.pop("wab")
    s5_ops = scatter([_shards(dwab[None, :, :D_MODEL], 1).astype(bf16), _shards(dwab[None, :, D_MODEL:], 1).astype(bf16)])
    dx3, dn11, dw11, p21, p_s5 = ffn_step_bwd(dx4, x3, ffn1_norm[1:2], wf11, sv11, ride_act=scatter(dw21), ride_w=s5_ops)
    dx2, dn20, dw20, p11, _ = ffn_step_bwd(dx3, x2, ffn2_norm[0:1], wf20, sv20, ride_act=scatter(dw11))
    dx1, g_hy, p20, p_hy = hybrid_bwd(dx2, x1, w0, sv_h, seq, ride_dwin=scatter(dw20))
    dx0, dn10, dw10, _, _ = ffn_step_bwd(dx1, x0, ffn1_norm[0:1], wf10, sv10)
    g_norms = {"ffn1_norm": jnp.concatenate([dn10, dn11], axis=0), "ffn2_norm": jnp.concatenate([dn20, dn21], axis=0)}
    small = small_grads(g_norms, g_hy, g_s5, d_final, rep)
    *p10, small_parts = exchange("reduce_tail", scatter(dw10) + [(_pack([small[n] for n in SMALL]), "gather")])
    small_sum = sum_parts("sum_small", small_parts, tr=_tile_rows(small_parts.shape[1]))

    out_g, out_d, out_m, out_v = {}, {}, {}, {}
    ffn_parts = {"ffn1": (p10, p11), "ffn2": (p20, p21)}
    for pre in ("ffn1", "ffn2"):
        for k, s in enumerate(ffn_w):
            n = pre + s
            r, c = w[n].shape[1:]
            res = None
            for l in (1, 0):
                res = adamw_layer("adamw_" + n, ffn_parts[pre][l][k].reshape(N_DEV, r, c), w[n], mom[n], var[n], l, res,
                                  tr=_tile_rows(r))
            out_g[n], out_d[n], out_m[n], out_v[n] = res
    for n, parts in zip(("hy_w_in", "hy_w_out", "s5_w_a", "s5_w_b"), tuple(p_hy) + tuple(p_s5)):
        shp = w[n].shape
        w2 = _flat2d(w[n])
        res = adamw("adamw_" + n, parts.reshape((N_DEV,) + w2.shape), w2, _flat2d(mom[n]), _flat2d(var[n]),
                    tr=_tile_rows(w2.shape[0]))
        out_g[n], out_d[n], out_m[n], out_v[n] = [a.reshape(shp) for a in res]
    g_small = {}
    for n, full in zip(SMALL, _unpack(small_sum, [small[n].shape for n in SMALL])):
        if n in SMALL_SHARDED:
            ax = SMALL_SHARDED[n]
            full = lax.dynamic_slice_in_dim(full, me * w[n].shape[ax], w[n].shape[ax], axis=ax)
        g_small[n] = full
    packs = [_pack([t[n] for n in SMALL]) for t in (g_small, w, mom, var)]
    res = adamw("adamw_small", packs[0][None], packs[1], packs[2], packs[3], tr=_tile_rows(packs[0].shape[0]))
    for dst, a in zip((out_g, out_d, out_m, out_v), res):
        dst.update(zip(SMALL, _unpack(a, [w[n].shape for n in SMALL])))

    total = lax.psum(loss[0, 0], ("x", "y", "c"))
    return (total, dx0.reshape(bl, seq, d), *[out_g[n] for n in WEIGHTS], *[out_d[n] for n in WEIGHTS],
            *[out_m[n] for n in WEIGHTS], *[out_v[n] for n in WEIGHTS])
```

```python
import functools
import math

import jax
import jax.numpy as jnp
from jax import lax
from jax.experimental import pallas as pl
from jax.experimental.pallas import tpu as pltpu

f32 = jnp.float32
bf16 = jnp.bfloat16

N_DEV = 8
D_MODEL = 1024
D_FF = 2816
EPS = 1e-6
FFN_RES = 0.5
CONV_W = 4
SSD_HEADS = 16
SSD_HEAD_DIM = 64
SSD_GROUPS = 2
SSD_STATE = 128
SSD_HG = SSD_HEADS // SSD_GROUPS
SSD_GW = SSD_HG * SSD_HEAD_DIM
CHUNK = 128
ML_HEADS = 4
ML_HD = 256
S5_GROUP = 16
S5_GROUPS = 64
S5_STATE = 64
S5_CB = 8
S5_CH = (S5_GROUPS // S5_CB) * S5_STATE
S5_TL = 256
S5_SUB = 16
LANES = 128
IN_COLS = 4624
PROJ_W = 4864
OFF_Z, OFF_MX, OFF_MZ, OFF_XBC, OFF_DT = 0, 1024, 2048, 3072, 4608
ADAM_LR, ADAM_B1, ADAM_B2, ADAM_EPS, ADAM_WD, ADAM_STEP = 0.001, 0.9, 0.999, 1e-08, 0.01, 10
NEG = -1e30
VMEM_LIMIT = 56 * 1024 * 1024
HI = lax.Precision.HIGHEST


def _cp(n):
    return pltpu.CompilerParams(dimension_semantics=("arbitrary",) * n, vmem_limit_bytes=VMEM_LIMIT)


def _dg(a, b, ca, cb):
    return lax.dot_general(a.astype(bf16), b.astype(bf16), (((ca,), (cb,)), ((), ())), preferred_element_type=f32)


@functools.partial(jax.custom_vjp, nondiff_argnums=(2, 3))
def bdot(a, b, ca, cb):
    return _dg(a, b, ca, cb)


def _bdot_fwd(a, b, ca, cb):
    return _dg(a, b, ca, cb), (a, b)


def _bdot_bwd(ca, cb, res, ct):
    a, b = res
    da = _dg(ct, b, 1, 1 - cb) if ca == 1 else _dg(b, ct, 1 - cb, 1)
    db = _dg(a, ct, 1 - ca, 0) if cb == 0 else _dg(ct, a, 0, 1 - ca)
    return da, db


bdot.defvjp(_bdot_fwd, _bdot_bwd)


def hdot(a, b):
    return jnp.dot(a, b, precision=HI, preferred_element_type=f32)


def _iota(shape, dim):
    return lax.broadcasted_iota(jnp.int32, shape, dim)


def _tri(n):
    return (_iota((n, n), 0) >= _iota((n, n), 1))


@functools.partial(jax.custom_vjp, nondiff_argnums=(1,))
def tshift(x, k):
    return jnp.where(_iota(x.shape, 0) >= k, pltpu.roll(x, k, 0), 0.0)


def _tshift_fwd(x, k):
    return tshift(x, k), None


def _tshift_bwd(k, _, ct):
    n = ct.shape[0]
    return (jnp.where(_iota(ct.shape, 0) < n - k, pltpu.roll(ct, n - k, 0), 0.0),)


tshift.defvjp(_tshift_fwd, _tshift_bwd)


def _lane_pick(a, idx):
    return jnp.sum(jnp.where(_iota(a.shape, 1) == idx, a, 0.0), axis=1, keepdims=True)


def _row_pick(a, idx):
    return jnp.sum(jnp.where(_iota(a.shape, 0) == idx, a, 0.0), axis=0, keepdims=True)


def _silu(x):
    return x * jax.nn.sigmoid(x)


def map_fwd(name, f, grid, ins, in_specs, out_shapes, out_specs):
    n_in = len(ins)

    def body(*refs):
        pids = tuple(pl.program_id(i) for i in range(len(grid)))
        outs = f(pids, *[r[...] for r in refs[:n_in]])
        for r, o in zip(refs[n_in:], outs):
            r[...] = o.astype(r.dtype)

    return pl.pallas_call(body, name=name, grid=grid, in_specs=in_specs, out_specs=out_specs,
                          out_shape=out_shapes, compiler_params=_cp(len(grid)))(*ins)


def scan_fwd(name, f, grid, slot_axis, ins, in_specs, out_shapes, out_specs, state_shapes, state_init, save_shapes, save_specs):
    n_in, n_out, n_st = len(ins), len(out_shapes), len(state_shapes)
    n_slots = grid[slot_axis]
    cax = len(grid) - 1 if slot_axis != len(grid) - 1 else len(grid) - 2

    def body(*refs):
        pids = tuple(pl.program_id(i) for i in range(len(grid)))
        in_refs, out_refs = refs[:n_in], refs[n_in:n_in + n_out]
        save_refs = refs[n_in + n_out:n_in + n_out + n_st]
        st_refs = refs[n_in + n_out + n_st:]
        slot = pids[slot_axis]

        @pl.when(pids[cax] == 0)
        def _():
            for s, init in zip(st_refs, state_init):
                s[slot] = jnp.full(s.shape[1:], init, f32)

        states = tuple(s[slot] for s in st_refs)
        for sv, st in zip(save_refs, states):
            sv[...] = st.reshape(sv.shape)
        outs, new = f(pids, states, *[r[...] for r in in_refs])
        for r, o in zip(out_refs, outs):
            r[...] = o.astype(r.dtype)
        for s, v in zip(st_refs, new):
            s[slot] = v

    scratch = [pltpu.VMEM((n_slots,) + tuple(s), f32) for s in state_shapes]
    return pl.pallas_call(body, name=name, grid=grid, in_specs=in_specs, out_specs=list(out_specs) + list(save_specs),
                          out_shape=list(out_shapes) + list(save_shapes), scratch_shapes=scratch,
                          compiler_params=_cp(len(grid)))(*ins)


def scan_bwd(name, f, grid, slot_axis, ins, in_specs, saves, save_specs, cts, ct_specs, state_shapes, wrt, acc_first):
    n_in, n_st, n_ct = len(ins), len(saves), len(cts)
    n_slots = grid[slot_axis]
    cax = len(grid) - 1 if slot_axis != len(grid) - 1 else len(grid) - 2

    def body(*refs):
        pids = tuple(pl.program_id(i) for i in range(len(grid)))
        in_refs = refs[:n_in]
        save_refs = refs[n_in:n_in + n_st]
        ct_refs = refs[n_in + n_st:n_in + n_st + n_ct]
        out_refs = refs[n_in + n_st + n_ct:n_in + n_st + n_ct + len(wrt)]
        dst_refs = refs[n_in + n_st + n_ct + len(wrt):]
        slot = pids[slot_axis]

        @pl.when(pids[cax] == 0)
        def _():
            for s in dst_refs:
                s[slot] = jnp.zeros(s.shape[1:], f32)

        vals = [r[...] for r in in_refs]
        states = tuple(sv[...].reshape(shp) for sv, shp in zip(save_refs, state_shapes))
        ctv = tuple(r[...].astype(f32) for r in ct_refs)
        dnew = tuple(s[slot] for s in dst_refs)

        def g(st, *dv):
            full = list(vals)
            for i, v in zip(wrt, dv):
                full[i] = v
            outs, new = f(pids, st, *full)
            return tuple(outs), tuple(new)

        _, vjp = jax.vjp(g, states, *[vals[i] for i in wrt])
        grads = vjp((ctv, dnew))
        for s, v in zip(dst_refs, grads[0]):
            s[slot] = v
        for i, o_ref, gr in zip(wrt, out_refs, grads[1:]):
            first = acc_first.get(i)
            if first is None:
                o_ref[...] = gr.astype(o_ref.dtype)
            else:
                @pl.when(first(pids))
                def _():
                    o_ref[...] = jnp.zeros_like(o_ref)
                o_ref[...] += gr

    out_shapes = [jax.ShapeDtypeStruct(ins[i].shape, f32) for i in wrt]
    out_specs = [in_specs[i] for i in wrt]
    scratch = [pltpu.VMEM((n_slots,) + tuple(s), f32) for s in state_shapes]
    return pl.pallas_call(body, name=name, grid=grid, in_specs=list(in_specs) + list(save_specs) + list(ct_specs),
                          out_specs=out_specs, out_shape=out_shapes, scratch_shapes=scratch,
                          compiler_params=_cp(len(grid)))(*ins, *saves, *cts)


def matmul(name, a, b, ca=1, cb=0, add=None, out_dtype=f32, a_off=0, a_width=None, tm=512, tn=512, tk=512, ride=None):
    rider = Rider(ride)
    nr = rider.n
    a_width = a.shape[1] if a_width is None else a_width
    kdim = b.shape[cb]
    n = b.shape[1 - cb]
    m = a.shape[0] if ca == 1 else a_width
    tm, tn, tk = min(tm, m), min(tn, n), min(tk, kdim)
    assert m % tm == 0 and n % tn == 0 and kdim % tk == 0
    nk = kdim // tk
    if ca == 1:
        assert a_off % tk == 0 and a_width == kdim
        koff = a_off // tk
        a_spec = pl.BlockSpec((tm, tk), lambda i, j, k: (i, k + koff))
    else:
        assert a_off % tm == 0 and a.shape[0] == kdim
        ioff = a_off // tm
        a_spec = pl.BlockSpec((tk, tm), lambda i, j, k: (k, i + ioff))
    b_spec = pl.BlockSpec((tk, tn), lambda i, j, k: (k, j)) if cb == 0 else pl.BlockSpec((tn, tk), lambda i, j, k: (j, k))
    o_spec = pl.BlockSpec((tm, tn), lambda i, j, k: (i, j))
    has_add = add is not None

    n_in = 3 if has_add else 2
    grid = (m // tm, n // tn, nk)

    def body(*refs):
        a_ref, b_ref = refs[0], refs[1]
        add_ref = refs[2] if has_add else None
        r_ins, o_ref = refs[n_in:n_in + nr], refs[n_in + nr]
        r_outs, acc, sems = refs[n_in + nr + 1:n_in + 2 * nr + 1], refs[n_in + 2 * nr + 1], refs[n_in + 2 * nr + 2:]
        rider.start(grid, r_ins, r_outs, sems)
        k = pl.program_id(2)

        @pl.when(k == 0)
        def _():
            acc[...] = add_ref[...].astype(f32) if has_add else jnp.zeros_like(acc)

        acc[...] += _dg(a_ref[...], b_ref[...], ca, cb)

        @pl.when(k == nk - 1)
        def _():
            o_ref[...] = acc[...].astype(o_ref.dtype)

        rider.wait(grid, r_ins, r_outs, sems)

    ins = [a, b] + ([add] if has_add else [])
    specs = [a_spec, b_spec] + ([o_spec] if has_add else [])
    res = pl.pallas_call(body, name=name, grid=grid, in_specs=specs + rider.specs(), out_specs=[o_spec] + rider.specs(),
                         out_shape=[jax.ShapeDtypeStruct((m, n), out_dtype)] + rider.out_shapes(),
                         scratch_shapes=[pltpu.VMEM((tm, tn), f32)] + rider.scratch(), compiler_params=_cp(3))(*ins, *rider.arrays())
    return res if nr else res[0]


def f_rms(pids, x, w):
    r = lax.rsqrt(jnp.mean(x * x, axis=-1, keepdims=True) + EPS)
    return (x * r * w,)


def _row_spec(tm, width, col=0):
    return pl.BlockSpec((tm, width), lambda i: (i, col))


def _par_spec(shape):
    return pl.BlockSpec(shape, lambda *p: (0,) * len(shape))


def rms_fwd(x, w, tm=512):
    t, d = x.shape
    return map_fwd("rms_fwd", f_rms, (t // tm,), [x, w], [_row_spec(tm, d), _par_spec((1, d))],
                   [jax.ShapeDtypeStruct((t, d), f32)], [_row_spec(tm, d)])[0]


def rms_bwd(dys, x, w, dres, tm=512):
    t, d = x.shape
    n = len(dys)

    def body(*refs):
        x_ref, w_ref, dres_ref, dx_ref, dw_ref = refs[n:]
        dy = refs[0][...]
        for r in refs[1:n]:
            dy = dy + r[...]
        _, vjp = jax.vjp(lambda xx, ww: f_rms(None, xx, ww)[0], x_ref[...], w_ref[...])
        dx, dw = vjp(dy)
        dx_ref[...] = dx + dres_ref[...]

        @pl.when(pl.program_id(0) == 0)
        def _():
            dw_ref[...] = jnp.zeros_like(dw_ref)
        dw_ref[...] += dw

    return pl.pallas_call(body, name="rms_bwd", grid=(t // tm,),
                          in_specs=[_row_spec(tm, d)] * (n + 1) + [_par_spec((1, d)), _row_spec(tm, d)],
                          out_specs=[_row_spec(tm, d), _par_spec((1, d))],
                          out_shape=[jax.ShapeDtypeStruct((t, d), f32), jax.ShapeDtypeStruct((1, d), f32)],
                          compiler_params=_cp(1))(*dys, x, w, dres)


def loss_head(x, w, tgt, tm=512):
    t, d = x.shape

    def fl(xx, ww, tt):
        y = f_rms(None, xx, ww)[0]
        return 0.5 * jnp.sum(jnp.mean(jnp.square(y - tt), axis=-1, keepdims=True), axis=0, keepdims=True)

    def body(x_ref, w_ref, t_ref, loss_ref, dx_ref, dw_ref):
        val, vjp = jax.vjp(lambda xx, ww: fl(xx, ww, t_ref[...]), x_ref[...], w_ref[...])
        dx, dw = vjp(jnp.ones((1, 1), f32))
        dx_ref[...] = dx

        @pl.when(pl.program_id(0) == 0)
        def _():
            dw_ref[...] = jnp.zeros_like(dw_ref)
            loss_ref[...] = jnp.zeros_like(loss_ref)
        dw_ref[...] += dw
        loss_ref[...] += val

    return pl.pallas_call(body, name="loss_head", grid=(t // tm,),
                          in_specs=[_row_spec(tm, d), _par_spec((1, d)), _row_spec(tm, d)],
                          out_specs=[_par_spec((1, 1)), _row_spec(tm, d), _par_spec((1, d))],
                          out_shape=[jax.ShapeDtypeStruct((1, 1), f32), jax.ShapeDtypeStruct((t, d), f32),
                                     jax.ShapeDtypeStruct((1, d), f32)],
                          compiler_params=_cp(1))(x, w, tgt)


FFN_TF = 256


def ffn_fwd(x, nw, wg, wu, wd, tm=512, ride=None):
    t, d = x.shape
    fs = FFN_TF
    ns = wg.shape[1] // fs
    rider = Rider(ride)
    nr = rider.n
    grid = (t // tm, ns)

    def body(*refs):
        x_ref, nw_ref, wg_ref, wu_ref, wd_ref = refs[:5]
        r_ins = refs[5:5 + nr]
        xo_ref, h_ref, g_ref, u_ref = refs[5 + nr:9 + nr]
        r_outs, acc, sems = refs[9 + nr:9 + 2 * nr], refs[9 + 2 * nr], refs[10 + 2 * nr:]
        rider.start(grid, r_ins, r_outs, sems)
        j = pl.program_id(1)

        @pl.when(j == 0)
        def _():
            h_ref[...] = f_rms(None, x_ref[...], nw_ref[...])[0].astype(bf16)
            acc[...] = jnp.zeros_like(acc)

        h = h_ref[...]
        g = jnp.dot(h, wg_ref[...], preferred_element_type=f32)
        u = jnp.dot(h, wu_ref[...], preferred_element_type=f32)
        g_ref[...] = g
        u_ref[...] = u
        acc[...] += jnp.dot((_silu(g) * u).astype(bf16), wd_ref[...], preferred_element_type=f32)

        @pl.when(j == ns - 1)
        def _():
            xo_ref[...] = x_ref[...] + FFN_RES * acc[...]

        rider.wait(grid, r_ins, r_outs, sems)

    row = pl.BlockSpec((tm, d), lambda i, j: (i, 0))
    wcol = pl.BlockSpec((d, fs), lambda i, j: (0, j))
    wrow = pl.BlockSpec((fs, d), lambda i, j: (j, 0))
    act = pl.BlockSpec((tm, fs), lambda i, j: (i, j))
    return pl.pallas_call(body, name="ffn_fwd", grid=grid,
                          in_specs=[row, pl.BlockSpec((1, d), lambda i, j: (0, 0)), wcol, wcol, wrow] + rider.specs(),
                          out_specs=[row, row, act, act] + rider.specs(),
                          out_shape=[jax.ShapeDtypeStruct((t, d), f32), jax.ShapeDtypeStruct((t, d), bf16),
                                     jax.ShapeDtypeStruct((t, ns * fs), f32), jax.ShapeDtypeStruct((t, ns * fs), f32)]
                          + rider.out_shapes(),
                          scratch_shapes=[pltpu.VMEM((tm, d), f32)] + rider.scratch(),
                          compiler_params=_cp(2))(x, nw, wg, wu, wd, *rider.arrays())


def ffn_bwd_act(dy, x, nw, g, u, wg, wu, wd, tm=512, ride=None):
    t, d = x.shape
    fs = FFN_TF
    ns = wg.shape[1] // fs
    rider = Rider(ride)
    nr = rider.n
    grid = (t // tm, ns)

    def body(*refs):
        dy_ref, x_ref, nw_ref, g_ref, u_ref, wg_ref, wu_ref, wd_ref = refs[:8]
        r_ins = refs[8:8 + nr]
        dx_ref, dnw_ref, dg_ref, du_ref, a_ref = refs[8 + nr:13 + nr]
        r_outs, acc, sems = refs[13 + nr:13 + 2 * nr], refs[13 + 2 * nr], refs[14 + 2 * nr:]
        rider.start(grid, r_ins, r_outs, sems)
        i, j = pl.program_id(0), pl.program_id(1)

        @pl.when(j == 0)
        def _():
            acc[...] = jnp.zeros_like(acc)

        dyh = (FFN_RES * dy_ref[...]).astype(bf16)
        da = _dg(dyh, wd_ref[...], 1, 1)
        gg, uu = g_ref[...], u_ref[...]
        sg = jax.nn.sigmoid(gg)
        si = gg * sg
        dgv = (da * uu * (sg * (1.0 + gg * (1.0 - sg)))).astype(bf16)
        duv = (da * si).astype(bf16)
        dg_ref[...] = dgv
        du_ref[...] = duv
        a_ref[...] = (si * uu).astype(bf16)
        acc[...] += _dg(dgv, wg_ref[...], 1, 1) + _dg(duv, wu_ref[...], 1, 1)

        @pl.when(j == ns - 1)
        def _():
            _, vjp = jax.vjp(lambda xx, ww: f_rms(None, xx, ww)[0], x_ref[...], nw_ref[...])
            dx, dw = vjp(acc[...])
            dx_ref[...] = dx + dy_ref[...]

            @pl.when(i == 0)
            def _():
                dnw_ref[...] = jnp.zeros_like(dnw_ref)
            dnw_ref[...] += dw

        rider.wait(grid, r_ins, r_outs, sems)

    row = pl.BlockSpec((tm, d), lambda i, j: (i, 0))
    wcol = pl.BlockSpec((d, fs), lambda i, j: (0, j))
    wrow = pl.BlockSpec((fs, d), lambda i, j: (j, 0))
    act = pl.BlockSpec((tm, fs), lambda i, j: (i, j))
    par = pl.BlockSpec((1, d), lambda i, j: (0, 0))
    return pl.pallas_call(body, name="ffn_bwd_act", grid=grid,
                          in_specs=[row, row, par, act, act, wcol, wcol, wrow] + rider.specs(),
                          out_specs=[row, par, act, act, act] + rider.specs(),
                          out_shape=[jax.ShapeDtypeStruct((t, d), f32), jax.ShapeDtypeStruct((1, d), f32)]
                          + [jax.ShapeDtypeStruct((t, ns * fs), bf16)] * 3 + rider.out_shapes(),
                          scratch_shapes=[pltpu.VMEM((tm, d), f32)] + rider.scratch(),
                          compiler_params=_cp(2))(dy, x, nw, g, u, wg, wu, wd, *rider.arrays())


def ffn_bwd_w(h, dy, dg, du, a, tk=512, ride=None):
    t, d = h.shape
    fs = FFN_TF
    ns = dg.shape[1] // fs
    nk = t // tk
    rider = Rider(ride)
    nr = rider.n
    grid = (ns, nk)

    def body(*refs):
        h_ref, dy_ref, dg_ref, du_ref, a_ref = refs[:5]
        r_ins = refs[5:5 + nr]
        og, ou, od = refs[5 + nr:8 + nr]
        r_outs = refs[8 + nr:8 + 2 * nr]
        ag, au, ad = refs[8 + 2 * nr:11 + 2 * nr]
        sems = refs[11 + 2 * nr:]
        rider.start(grid, r_ins, r_outs, sems)
        k = pl.program_id(1)

        @pl.when(k == 0)
        def _():
            ag[...] = jnp.zeros_like(ag)
            au[...] = jnp.zeros_like(au)
            ad[...] = jnp.zeros_like(ad)

        hh = h_ref[...]
        ag[...] += _dg(hh, dg_ref[...], 0, 0)
        au[...] += _dg(hh, du_ref[...], 0, 0)
        ad[...] += _dg(a_ref[...], FFN_RES * dy_ref[...], 0, 0)

        @pl.when(k == nk - 1)
        def _():
            og[...] = ag[...].astype(og.dtype)
            ou[...] = au[...].astype(ou.dtype)
            od[...] = ad[...].astype(od.dtype)

        rider.wait(grid, r_ins, r_outs, sems)

    row = pl.BlockSpec((tk, d), lambda j, k: (k, 0))
    act = pl.BlockSpec((tk, fs), lambda j, k: (k, j))
    wcol = pl.BlockSpec((d, fs), lambda j, k: (0, j))
    wrow = pl.BlockSpec((fs, d), lambda j, k: (j, 0))
    return pl.pallas_call(body, name="ffn_bwd_w", grid=grid, in_specs=[row, row, act, act, act] + rider.specs(),
                          out_specs=[wcol, wcol, wrow] + rider.specs(),
                          out_shape=[jax.ShapeDtypeStruct((d, ns * fs), bf16)] * 2
                          + [jax.ShapeDtypeStruct((ns * fs, d), bf16)] + rider.out_shapes(),
                          scratch_shapes=[pltpu.VMEM((d, fs), f32), pltpu.VMEM((d, fs), f32), pltpu.VMEM((fs, d), f32)]
                          + rider.scratch(),
                          compiler_params=_cp(2))(h, dy, dg, du, a, *rider.arrays())


def f_conv(pids, x, w, b):
    y = b + x * w[CONV_W - 1:CONV_W, :]
    for j in range(CONV_W - 1):
        y = y + tshift(x, CONV_W - 1 - j) * w[j:j + 1, :]
    return (_silu(y),)


def _conv_specs(seq, col0, cb):
    xs = pl.BlockSpec((seq, cb), lambda c, b: (b, col0 + c))
    ws = pl.BlockSpec((CONV_W, cb), lambda c, b: (0, c))
    bs = pl.BlockSpec((1, cb), lambda c, b: (0, c))
    ys = pl.BlockSpec((seq, cb), lambda c, b: (b, c))
    return xs, ws, bs, ys


def conv_fwd(name, src, col_off, w, b, seq, cb=256):
    t = src.shape[0]
    c = w.shape[1]
    xs, ws, bs, ys = _conv_specs(seq, col_off // cb, cb)
    return map_fwd(name, f_conv, (c // cb, t // seq), [src, w, b], [xs, ws, bs],
                   [jax.ShapeDtypeStruct((t, c), f32)], [ys])[0]


def conv_bwd(name, dy, src, col_off, w, b, seq, cb=256):
    t = src.shape[0]
    c = w.shape[1]
    xs, ws, bs, ys = _conv_specs(seq, col_off // cb, cb)

    def body(x_ref, w_ref, b_ref, dy_ref, dx_ref, dw_ref, db_ref):
        _, vjp = jax.vjp(lambda xx, ww, bb: f_conv(None, xx, ww, bb)[0], x_ref[...], w_ref[...], b_ref[...])
        dx, dw, db = vjp(dy_ref[...])
        dx_ref[...] = dx

        @pl.when(pl.program_id(1) == 0)
        def _():
            dw_ref[...] = jnp.zeros_like(dw_ref)
            db_ref[...] = jnp.zeros_like(db_ref)
        dw_ref[...] += dw
        db_ref[...] += db

    return pl.pallas_call(body, name=name, grid=(c // cb, t // seq), in_specs=[xs, ws, bs, ys], out_specs=[ys, ws, bs],
                          out_shape=[jax.ShapeDtypeStruct((t, c), f32), jax.ShapeDtypeStruct(w.shape, f32),
                                     jax.ShapeDtypeStruct(b.shape, f32)], compiler_params=_cp(2))(src, w, b, dy)


def f_ssd(pids, states, xs, dtraw, bm, cm, a_log, dt_bias, d_skip):
    g = pids[2]
    (hn,) = states
    l = xs.shape[0]
    head_of_lane = _iota((LANES, SSD_GW), 1) // SSD_HEAD_DIM + SSD_HG * g
    expand = (_iota((LANES, SSD_GW), 0) == head_of_lane).astype(f32)
    tri = _tri(l)
    dt = jax.nn.softplus(dtraw + dt_bias)
    adt = dt * (-jnp.exp(a_log))
    cs = hdot(tri.astype(f32), adt)
    cst = cs.T
    cs_last = cs[l - 1:l, :]
    dt_e, cs_e, csl_e = hdot(dt, expand), hdot(cs, expand), hdot(cs_last, expand)
    xd = xs * dt_e
    gmat = bdot(cm, bm, 1, 1)
    half = _iota((l, LANES), 1) < SSD_HEAD_DIM
    blocks = []
    for pair in range(SSD_HG // 2):
        xb = xd[:, pair * LANES:(pair + 1) * LANES]
        res = []
        for sub in range(2):
            hid = SSD_HG * g + 2 * pair + sub
            col, row = _lane_pick(cs, hid), _row_pick(cst, hid)
            lm = jnp.exp(jnp.where(tri, col - row, NEG))
            res.append(bdot(gmat * lm, xb, 1, 0))
        blocks.append(jnp.where(half, res[0], res[1]))
    y = jnp.concatenate(blocks, axis=1)
    y = y + jnp.exp(cs_e) * bdot(cm, hn, 1, 0)
    y = y + hdot(d_skip, expand) * xs
    hn_new = jnp.exp(csl_e) * hn + bdot(bm, jnp.exp(csl_e - cs_e) * xd, 0, 0)
    return (y,), (hn_new,)


def _ssd_specs(seq, nch, rev):
    cc = (lambda c: nch - 1 - c) if rev else (lambda c: c)
    xs = pl.BlockSpec((CHUNK, SSD_GW), lambda b, c, g: (b * nch + cc(c), g))
    dt = pl.BlockSpec((CHUNK, LANES), lambda b, c, g: (b * nch + cc(c), OFF_DT // LANES))
    bm = pl.BlockSpec((CHUNK, SSD_STATE), lambda b, c, g: (b * nch + cc(c), 1024 // SSD_STATE + g))
    cm = pl.BlockSpec((CHUNK, SSD_STATE), lambda b, c, g: (b * nch + cc(c), 1024 // SSD_STATE + SSD_GROUPS + g))
    par = pl.BlockSpec((1, LANES), lambda b, c, g: (0, 0))
    sv = pl.BlockSpec((1, 1, SSD_STATE, SSD_GW), lambda b, c, g: (b * nch + cc(c), g, 0, 0))
    ddt = pl.BlockSpec((CHUNK, LANES), lambda b, c, g: (b * nch + cc(c), 0))
    dbc = pl.BlockSpec((CHUNK, SSD_STATE), lambda b, c, g: (b * nch + cc(c), g))
    return xs, dt, bm, cm, par, sv, ddt, dbc


def ssd_fwd(xbc, proj, a_log, dt_bias, d_skip, seq):
    t = xbc.shape[0]
    nch = seq // CHUNK
    xs, dt, bm, cm, par, sv, _, _ = _ssd_specs(seq, nch, False)
    grid = (t // seq, nch, SSD_GROUPS)
    y, hsave = scan_fwd("ssd_fwd", f_ssd, grid, 2, [xbc, proj, xbc, xbc, a_log, dt_bias, d_skip],
                        [xs, dt, bm, cm, par, par, par], [jax.ShapeDtypeStruct((t, SSD_GROUPS * SSD_GW), f32)], [xs],
                        [(SSD_STATE, SSD_GW)], [0.0],
                        [jax.ShapeDtypeStruct((t // CHUNK, SSD_GROUPS, SSD_STATE, SSD_GW), f32)], [sv])
    return y, hsave


def ssd_bwd(dy, xbc, proj, a_log, dt_bias, d_skip, hsave, seq):
    t = xbc.shape[0]
    nch = seq // CHUNK
    xs, dt, bm, cm, par, sv, ddt, dbc = _ssd_specs(seq, nch, True)
    grid = (t // seq, nch, SSD_GROUPS)

    def body(x_ref, dt_ref, b_ref, c_ref, al_ref, db_ref, ds_ref, h_ref, dy_ref,
             dxbc_x, dxbc_b, dxbc_c, ddt_ref, dal_ref, ddb_ref, dds_ref, dst):
        pids = tuple(pl.program_id(i) for i in range(3))
        slot = pids[2]

        @pl.when(pids[1] == 0)
        def _():
            dst[slot] = jnp.zeros(dst.shape[1:], f32)

        vals = [x_ref[...], dt_ref[...], b_ref[...], c_ref[...], al_ref[...], db_ref[...], ds_ref[...]]

        def gfun(st, *v):
            outs, new = f_ssd(pids, (st,), *v)
            return outs[0], new[0]

        _, vjp = jax.vjp(gfun, h_ref[0, 0], *vals)
        grads = vjp((dy_ref[...], dst[slot]))
        dst[slot] = grads[0]
        dxbc_x[...] = grads[1]
        dxbc_b[...] = grads[3]
        dxbc_c[...] = grads[4]

        @pl.when(slot == 0)
        def _():
            ddt_ref[...] = jnp.zeros_like(ddt_ref)
        ddt_ref[...] += grads[2]
        first = jnp.logical_and(jnp.logical_and(pids[0] == 0, pids[1] == 0), slot == 0)

        @pl.when(first)
        def _():
            dal_ref[...] = jnp.zeros_like(dal_ref)
            ddb_ref[...] = jnp.zeros_like(ddb_ref)
            dds_ref[...] = jnp.zeros_like(dds_ref)
        dal_ref[...] += grads[5]
        ddb_ref[...] += grads[6]
        dds_ref[...] += grads[7]

    bc_shape = jax.ShapeDtypeStruct((t, SSD_GROUPS * SSD_STATE), f32)
    par_shape = jax.ShapeDtypeStruct((1, LANES), f32)
    outs = pl.pallas_call(body, name="ssd_bwd", grid=grid, in_specs=[xs, dt, bm, cm, par, par, par, sv, xs],
                          out_specs=[xs, dbc, dbc, ddt, par, par, par],
                          out_shape=[jax.ShapeDtypeStruct((t, SSD_GROUPS * SSD_GW), f32),
                                     bc_shape, bc_shape, jax.ShapeDtypeStruct((t, LANES), f32),
                                     par_shape, par_shape, par_shape],
                          scratch_shapes=[pltpu.VMEM((SSD_GROUPS, SSD_STATE, SSD_GW), f32)],
                          compiler_params=_cp(3))(xbc, proj, xbc, xbc, a_log, dt_bias, d_skip, hsave, dy)
    return outs


def f_ssd_epi(pids, y, z, nw):
    yg = y * _silu(z)
    hw = yg.shape[1] // SSD_GROUPS
    parts = []
    for g in range(SSD_GROUPS):
        p = yg[:, g * hw:(g + 1) * hw]
        parts.append(p * lax.rsqrt(jnp.mean(p * p, axis=-1, keepdims=True) + EPS))
    return (jnp.concatenate(parts, axis=1) * nw,)


def f_ml_epi(pids, hm, xc, mz, nw, skip):
    parts = []
    for h in range(ML_HEADS):
        p = hm[:, h * ML_HD:(h + 1) * ML_HD]
        mu = jnp.mean(p, axis=-1, keepdims=True)
        var = jnp.mean(jnp.square(p - mu), axis=-1, keepdims=True)
        parts.append((p - mu) * lax.rsqrt(var + EPS))
    hn = jnp.concatenate(parts, axis=1) * nw
    return ((hn + skip * xc) * _silu(mz),)


def f_s5_post(pids, ys, u, d_skip):
    return (jax.nn.gelu(ys + d_skip * u),)


def f_glu(pids, pab, ba, bb):
    d = ba.shape[1]
    return ((pab[:, :d] + ba) * jax.nn.sigmoid(pab[:, d:] + bb),)


def f_glu_res(pids, pab, xres, ba, bb):
    return (xres + f_glu(pids, pab, ba, bb)[0],)


def rowwise_fwd(name, f, rows, row_cols, pars, out_width, tm=512):
    t = rows[0].shape[0]
    specs = [_row_spec(tm, w, c) for (w, c) in row_cols] + [_par_spec(p.shape) for p in pars]
    return map_fwd(name, f, (t // tm,), list(rows) + list(pars), specs, [jax.ShapeDtypeStruct((t, out_width), f32)],
                   [_row_spec(tm, out_width)])[0]


def rowwise_bwd(name, f, rows, row_cols, pars, dy, tm=256):
    t = rows[0].shape[0]
    n_r, n_p = len(rows), len(pars)
    specs = [_row_spec(tm, w, c) for (w, c) in row_cols] + [_par_spec(p.shape) for p in pars]
    out_w = dy.shape[1]

    def body(*refs):
        vals = [r[...] for r in refs[:n_r + n_p]]
        dy_ref = refs[n_r + n_p]
        outs = refs[n_r + n_p + 1:]
        _, vjp = jax.vjp(lambda *v: f(None, *v)[0], *vals)
        grads = vjp(dy_ref[...])
        for k in range(n_r):
            outs[k][...] = grads[k]

        @pl.when(pl.program_id(0) == 0)
        def _():
            for k in range(n_p):
                outs[n_r + k][...] = jnp.zeros_like(outs[n_r + k])
        for k in range(n_p):
            outs[n_r + k][...] += grads[n_r + k]

    out_shapes = [jax.ShapeDtypeStruct((t, w), f32) for (w, c) in row_cols] + [jax.ShapeDtypeStruct(p.shape, f32) for p in pars]
    out_specs = [_row_spec(tm, w) for (w, c) in row_cols] + [_par_spec(p.shape) for p in pars]
    return pl.pallas_call(body, name=name, grid=(t // tm,), in_specs=specs + [_row_spec(tm, out_w)], out_specs=out_specs,
                          out_shape=out_shapes, compiler_params=_cp(1))(*rows, *pars, dy)


def f_ml(pids, states, q, k, v, g1, g2, g3, b_if):
    h = pids[2]
    cst, nst, mst = states
    l = q.shape[0]
    gt = g1 + g2 + g3 + b_if
    k = k * (1.0 / math.sqrt(ML_HD))
    tri = _tri(l)
    bc_all = hdot(tri.astype(f32), jax.nn.log_sigmoid(gt))
    bcum, ig = _lane_pick(bc_all, ML_HEADS + h), _lane_pick(gt, h)
    bcum_t, ig_t = _row_pick(bc_all.T, ML_HEADS + h), _row_pick(gt.T, h)
    b_last = bcum[l - 1:l, :]
    dlog = jnp.where(tri, bcum - bcum_t + ig_t, NEG)
    ws = b_last - bcum + ig
    m_prev = mst[:, 0:1]
    m_new = lax.stop_gradient(jnp.maximum(b_last + m_prev, jnp.max(ws, axis=0, keepdims=True)))
    decay = jnp.exp(b_last + m_prev - m_new)
    wts = jnp.exp(ws - m_new)
    c_new = decay * cst + bdot(wts * v, k, 0, 0)
    n_new = decay * nst + jnp.sum(wts * k, axis=0, keepdims=True)
    m_inter = bcum + m_prev
    m_t = lax.stop_gradient(jnp.maximum(jnp.max(dlog, axis=1, keepdims=True), m_inter))
    scores = bdot(q, k, 1, 1) * jnp.exp(dlog - m_t)
    inter_w = jnp.exp(m_inter - m_t)
    num = bdot(scores, v, 1, 0) + inter_w * bdot(q, cst, 1, 1)
    den = jnp.sum(scores, axis=1, keepdims=True) + inter_w * jnp.sum(q * nst, axis=1, keepdims=True)
    hout = num / jnp.maximum(jnp.abs(den), jnp.exp(-m_t))
    return (hout,), (c_new, n_new, jnp.broadcast_to(m_new, mst.shape))


def _ml_specs(nch, rev):
    cc = (lambda c: nch - 1 - c) if rev else (lambda c: c)
    hd = pl.BlockSpec((CHUNK, ML_HD), lambda b, c, h: (b * nch + cc(c), h))
    gt = pl.BlockSpec((CHUNK, LANES), lambda b, c, h: (b * nch + cc(c), 0))
    par = pl.BlockSpec((1, LANES), lambda b, c, h: (0, 0))
    sc = pl.BlockSpec((1, 1, ML_HD, ML_HD), lambda b, c, h: (b * nch + cc(c), h, 0, 0))
    sn = pl.BlockSpec((1, 1, 1, ML_HD), lambda b, c, h: (b * nch + cc(c), h, 0, 0))
    sm = pl.BlockSpec((1, 1, 1, LANES), lambda b, c, h: (b * nch + cc(c), h, 0, 0))
    return hd, gt, par, sc, sn, sm


ML_STATE_SHAPES = [(ML_HD, ML_HD), (1, ML_HD), (1, LANES)]


def ml_fwd(q, k, v, g1, g2, g3, b_if, seq):
    t = q.shape[0]
    nch = seq // CHUNK
    hd, gt, par, sc, sn, sm = _ml_specs(nch, False)
    nc = t // CHUNK
    outs = scan_fwd("ml_fwd", f_ml, (t // seq, nch, ML_HEADS), 2, [q, k, v, g1, g2, g3, b_if],
                    [hd, hd, hd, gt, gt, gt, par], [jax.ShapeDtypeStruct((t, ML_HEADS * ML_HD), f32)], [hd],
                    ML_STATE_SHAPES, [0.0, 0.0, NEG],
                    [jax.ShapeDtypeStruct((nc, ML_HEADS, ML_HD, ML_HD), f32), jax.ShapeDtypeStruct((nc, ML_HEADS, 1, ML_HD), f32),
                     jax.ShapeDtypeStruct((nc, ML_HEADS, 1, LANES), f32)], [sc, sn, sm])
    return outs[0], outs[1:]


def ml_bwd(dh, q, k, v, g1, g2, g3, b_if, saves, seq):
    t = q.shape[0]
    nch = seq // CHUNK
    hd, gt, par, sc, sn, sm = _ml_specs(nch, True)

    def f(pids, states, q, k, v, gsum, b_if):
        return f_ml(pids, states, q, k, v, gsum, jnp.zeros_like(gsum), jnp.zeros_like(gsum), b_if)

    def body(q_ref, k_ref, v_ref, g1_ref, g2_ref, g3_ref, b_ref, c_ref, n_ref, m_ref, dh_ref,
             dq_ref, dk_ref, dv_ref, dg_ref, db_ref, dc_s, dn_s):
        pids = tuple(pl.program_id(i) for i in range(3))
        slot = pids[2]

        @pl.when(pids[1] == 0)
        def _():
            dc_s[slot] = jnp.zeros(dc_s.shape[1:], f32)
            dn_s[slot] = jnp.zeros(dn_s.shape[1:], f32)

        gsum = g1_ref[...] + g2_ref[...] + g3_ref[...]
        mst = m_ref[0, 0]

        def gfun(cst, nst, qq, kk, vv, gs, bb):
            outs, new = f(pids, (cst, nst, mst), qq, kk, vv, gs, bb)
            return outs[0], new[0], new[1]

        _, vjp = jax.vjp(gfun, c_ref[0, 0], n_ref[0, 0], q_ref[...], k_ref[...], v_ref[...], gsum, b_ref[...])
        grads = vjp((dh_ref[...], dc_s[slot], dn_s[slot]))
        dc_s[slot] = grads[0]
        dn_s[slot] = grads[1]
        dq_ref[...] = grads[2]
        dk_ref[...] = grads[3]
        dv_ref[...] = grads[4]

        @pl.when(slot == 0)
        def _():
            dg_ref[...] = jnp.zeros_like(dg_ref)
        dg_ref[...] += grads[5]
        first = jnp.logical_and(jnp.logical_and(pids[0] == 0, pids[1] == 0), slot == 0)

        @pl.when(first)
        def _():
            db_ref[...] = jnp.zeros_like(db_ref)
        db_ref[...] += grads[6]

    big = jax.ShapeDtypeStruct((t, ML_HEADS * ML_HD), f32)
    return pl.pallas_call(body, name="ml_bwd", grid=(t // seq, nch, ML_HEADS),
                          in_specs=[hd, hd, hd, gt, gt, gt, par, sc, sn, sm, hd], out_specs=[hd, hd, hd, gt, par],
                          out_shape=[big, big, big, jax.ShapeDtypeStruct((t, LANES), f32), jax.ShapeDtypeStruct((1, LANES), f32)],
                          scratch_shapes=[pltpu.VMEM((ML_HEADS, ML_HD, ML_HD), f32), pltpu.VMEM((ML_HEADS, 1, ML_HD), f32)],
                          compiler_params=_cp(3))(q, k, v, g1, g2, g3, b_if, *saves, dh)


def _block_prefix(z, transpose):
    n = z.shape[0]
    r, c = _iota((n, n), 0), _iota((n, n), 1)
    keep = jnp.logical_and(r // S5_SUB == c // S5_SUB, (c >= r) if transpose else (c <= r))
    m = jnp.where(keep, 1.0, 0.0).astype(bf16)
    hi = z.astype(bf16)
    lo = (z - hi.astype(f32)).astype(bf16)
    return jnp.dot(m, hi, preferred_element_type=f32) + jnp.dot(m, lo, preferred_element_type=f32)


@jax.custom_vjp
def block_prefix(z):
    return _block_prefix(z, False)


block_prefix.defvjp(lambda z: (_block_prefix(z, False), None), lambda _, ct: (_block_prefix(ct, True),))


def _cmul(a, b):
    h = b.shape[1] // 2
    ar, ai, br, bi = a[:, :h], a[:, h:], b[:, :h], b[:, h:]
    return jnp.concatenate([ar * br - ai * bi, ar * bi + ai * br], axis=1)


def f_s5(pids, states, u, bb, cc, tab):
    (carry,) = states
    tl = u.shape[0]
    nsub = tl // S5_SUB
    rep = lambda t: jnp.concatenate([t] * nsub, axis=0)
    p1, p0, q0 = tab[0:S5_SUB], tab[S5_SUB:2 * S5_SUB], tab[2 * S5_SUB:3 * S5_SUB]
    lam_sub = tab[S5_SUB - 1:S5_SUB]
    bu = bdot(u, bb, 1, 0)
    xl = _cmul(rep(p0), block_prefix(_cmul(rep(q0), bu)))
    e, entering = carry, []
    for k in range(nsub):
        entering.append(jnp.broadcast_to(e, (S5_SUB, e.shape[1])))
        e = xl[(k + 1) * S5_SUB - 1:(k + 1) * S5_SUB] + _cmul(lam_sub, e)
    x = xl + _cmul(rep(p1), jnp.concatenate(entering, axis=0))
    y = bdot(x, cc, 1, 0)
    return (y,), (e,)


def _s5_specs(ntl, rev):
    tt = (lambda t: ntl - 1 - t) if rev else (lambda t: t)
    us = pl.BlockSpec((S5_TL, LANES), lambda c, b, t: (b * ntl + tt(t), c))
    bbs = pl.BlockSpec((1, LANES, 2 * S5_CH), lambda c, b, t: (c, 0, 0))
    ccs = pl.BlockSpec((1, 2 * S5_CH, LANES), lambda c, b, t: (c, 0, 0))
    pws = pl.BlockSpec((1, 3 * S5_SUB, 2 * S5_CH), lambda c, b, t: (c, 0, 0))
    sv = pl.BlockSpec((1, 1, 1, 2 * S5_CH), lambda c, b, t: (b * ntl + tt(t), c, 0, 0))
    return us, bbs, ccs, pws, sv


def s5_fwd(u, bb, cc, pw, seq):
    t = u.shape[0]
    ntl = seq // S5_TL
    us, bbs, ccs, pws, sv = _s5_specs(ntl, False)

    def f(pids, states, uu, b3, c3, p3):
        return f_s5(pids, states, uu, b3[0], c3[0], p3[0])

    y, carries = scan_fwd("s5_fwd", f, (S5_CB, t // seq, ntl), 0, [u, bb, cc, pw], [us, bbs, ccs, pws],
                          [jax.ShapeDtypeStruct((t, S5_CB * LANES), f32)], [us], [(1, 2 * S5_CH)], [0.0],
                          [jax.ShapeDtypeStruct((t // S5_TL, S5_CB, 1, 2 * S5_CH), f32)], [sv])
    return y, carries


def s5_bwd(dy, u, bb, cc, pw, carries, seq):
    t = u.shape[0]
    ntl = seq // S5_TL
    us, bbs, ccs, pws, sv = _s5_specs(ntl, True)

    def f(pids, states, uu, b3, c3, p3):
        return f_s5(pids, states, uu, b3[0], c3[0], p3[0])

    first = lambda pids: jnp.logical_and(pids[1] == 0, pids[2] == 0)
    return scan_bwd("s5_bwd", f, (S5_CB, t // seq, ntl), 0, [u, bb, cc, pw], [us, bbs, ccs, pws], [carries], [sv],
                    [dy], [us], [(1, 2 * S5_CH)], [0, 1, 2, 3], {1: first, 2: first, 3: first})


def _adam_math(g, w, m, v):
    m2 = ADAM_B1 * m + (1.0 - ADAM_B1) * g
    v2 = ADAM_B2 * v + (1.0 - ADAM_B2) * jnp.square(g)
    m_hat = m2 / (1.0 - ADAM_B1 ** ADAM_STEP)
    v_hat = v2 / (1.0 - ADAM_B2 ** ADAM_STEP)
    delta = -ADAM_LR * (m_hat / (jnp.sqrt(v_hat) + ADAM_EPS) + ADAM_WD * w)
    return delta, m2, v2


def adamw(name, parts, w, m, v, tr=256):
    n, r, c = parts.shape
    tr = min(tr, r)
    assert r % tr == 0

    def body(p_ref, w_ref, m_ref, v_ref, g_ref, d_ref, m2_ref, v2_ref):
        g = p_ref[0].astype(f32)
        for s in range(1, n):
            g = g + p_ref[s].astype(f32)
        d, m2, v2 = _adam_math(g, w_ref[...], m_ref[...], v_ref[...])
        g_ref[...] = g
        d_ref[...] = d
        m2_ref[...] = m2
        v2_ref[...] = v2

    ps = pl.BlockSpec((n, tr, c), lambda i: (0, i, 0))
    rs = pl.BlockSpec((tr, c), lambda i: (i, 0))
    return pl.pallas_call(body, name=name, grid=(r // tr,), in_specs=[ps, rs, rs, rs], out_specs=[rs] * 4,
                          out_shape=[jax.ShapeDtypeStruct((r, c), f32)] * 4, compiler_params=_cp(1))(parts, w, m, v)


def adamw_layer(name, parts, w, m, v, layer, prev=None, tr=256):
    n, r, c = parts.shape
    nl = w.shape[0]
    tr = min(tr, r)
    assert r % tr == 0 and w.shape[1:] == (r, c)
    n_prev = 0 if prev is None else 4

    def body(*refs):
        p_ref, w_ref, m_ref, v_ref = refs[:4]
        g_ref, d_ref, m2_ref, v2_ref = refs[4 + n_prev:]
        g = p_ref[0].astype(f32)
        for s in range(1, n):
            g = g + p_ref[s].astype(f32)
        d, m2, v2 = _adam_math(g, w_ref[0], m_ref[0], v_ref[0])
        g_ref[0] = g
        d_ref[0] = d
        m2_ref[0] = m2
        v2_ref[0] = v2

    ps = pl.BlockSpec((n, tr, c), lambda i: (0, i, 0))
    rs = pl.BlockSpec((1, tr, c), lambda i: (layer, i, 0))
    anyspec = pl.BlockSpec(memory_space=pl.ANY)
    return pl.pallas_call(body, name=name, grid=(r // tr,), in_specs=[ps, rs, rs, rs] + [anyspec] * n_prev, out_specs=[rs] * 4,
                          out_shape=[jax.ShapeDtypeStruct((nl, r, c), f32)] * 4,
                          input_output_aliases={4 + i: i for i in range(n_prev)},
                          compiler_params=_cp(1))(parts, w, m, v, *(prev or ()))


def sum_parts(name, parts, tr=256):
    n, r, c = parts.shape
    tr = min(tr, r)
    assert r % tr == 0

    def body(p_ref, o_ref):
        g = p_ref[0].astype(f32)
        for s in range(1, n):
            g = g + p_ref[s].astype(f32)
        o_ref[...] = g

    return pl.pallas_call(body, name=name, grid=(r // tr,), in_specs=[pl.BlockSpec((n, tr, c), lambda i: (0, i, 0))],
                          out_specs=pl.BlockSpec((tr, c), lambda i: (i, 0)),
                          out_shape=jax.ShapeDtypeStruct((r, c), f32), compiler_params=_cp(1))(parts)


class Rider:
    def __init__(self, ops):
        self.ops = list(ops or [])
        self.n = len(self.ops)

    def arrays(self):
        return [a for a, _ in self.ops]

    def specs(self):
        return [pl.BlockSpec(memory_space=pl.ANY)] * self.n

    def out_shapes(self):
        return [jax.ShapeDtypeStruct((N_DEV,) + tuple(a.shape) if mode == "gather" else tuple(a.shape), a.dtype)
                for a, mode in self.ops]

    def scratch(self):
        if not self.n:
            return []
        return [pltpu.SemaphoreType.DMA((self.n, N_DEV - 1)), pltpu.SemaphoreType.DMA((self.n, N_DEV - 1)),
                pltpu.SemaphoreType.DMA((self.n,))]

    def _copies(self, ins, outs, sems):
        send_sems, recv_sems, loc_sems = sems
        x, y, c = lax.axis_index("x"), lax.axis_index("y"), lax.axis_index("c")
        me = 4 * x + 2 * y + c
        copies = []
        for k, (_, mode) in enumerate(self.ops):
            src_me = ins[k] if mode == "gather" else ins[k].at[me]
            copies.append(pltpu.make_async_copy(src_me, outs[k].at[me], loc_sems.at[k]))
            for d in range(1, N_DEV):
                px = 1 - x if (d >> 2) & 1 else x
                py = 1 - y if (d >> 1) & 1 else y
                pc = 1 - c if d & 1 else c
                src = ins[k] if mode == "gather" else ins[k].at[4 * px + 2 * py + pc]
                copies.append(pltpu.make_async_remote_copy(
                    src_ref=src, dst_ref=outs[k].at[me], send_sem=send_sems.at[k, d - 1], recv_sem=recv_sems.at[k, d - 1],
                    device_id=(px, py, pc), device_id_type=pl.DeviceIdType.MESH))
        return copies

    def start(self, grid, ins, outs, sems):
        if self.n:
            @pl.when(functools.reduce(jnp.logical_and, [pl.program_id(i) == 0 for i in range(len(grid))]))
            def _():
                for cp in self._copies(ins, outs, sems):
                    cp.start()

    def wait(self, grid, ins, outs, sems):
        if self.n:
            @pl.when(functools.reduce(jnp.logical_and, [pl.program_id(i) == g - 1 for i, g in enumerate(grid)]))
            def _():
                for cp in self._copies(ins, outs, sems):
                    cp.wait()


def exchange(name, ops):
    rider = Rider(ops)
    n = rider.n

    def body(*refs):
        copies = rider._copies(refs[:n], refs[n:2 * n], refs[2 * n:])
        for cp in copies:
            cp.start()
        for cp in copies:
            cp.wait()

    return pl.pallas_call(body, name=name, in_specs=rider.specs(), out_specs=rider.specs(), out_shape=rider.out_shapes(),
                          scratch_shapes=rider.scratch())(*rider.arrays())


def _lanes(v, width=LANES):
    v = v.reshape(1, -1)
    return jnp.pad(v, ((0, 0), (0, width - v.shape[1])))


def win_to_padded(w):
    return jnp.concatenate([w[:, :1024], w[:, 2576:3600], w[:, 3600:4624], w[:, 1024:2560], w[:, 2560:2576],
                            jnp.zeros((w.shape[0], PROJ_W - IN_COLS), w.dtype)], axis=1)


def win_from_padded(wp):
    return jnp.concatenate([wp[:, 0:1024], wp[:, 3072:4608], wp[:, 4608:4624], wp[:, 1024:2048], wp[:, 2048:3072]], axis=1)


def headwise_dense(w):
    nb, o, i = w.shape
    rows = jnp.tile(w.transpose(0, 2, 1).reshape(nb * i, o), (1, nb))
    same = (jnp.arange(nb * i)[:, None] // i) == (jnp.arange(nb * o)[None, :] // o)
    return jnp.where(same, rows, 0.0)


def diag_blocks(name, dd, blk, tm=256):
    n = dd.shape[0]

    def body(d_ref, o_ref):
        rows = _iota((tm, n), 0) + pl.program_id(0) * tm
        masked = jnp.where(rows // blk == _iota((tm, n), 1) // blk, d_ref[...], 0.0)
        sel = (_iota((n, LANES), 0) % blk == _iota((n, LANES), 1)).astype(f32)
        o_ref[...] = hdot(masked, sel)

    return pl.pallas_call(body, name=name, grid=(n // tm,), in_specs=[pl.BlockSpec((tm, n), lambda i: (i, 0))],
                          out_specs=pl.BlockSpec((tm, LANES), lambda i: (i, 0)),
                          out_shape=jax.ShapeDtypeStruct((n, LANES), f32), compiler_params=_cp(1))(dd)


def headwise_from_dense(name, dd, o=4, i=4):
    nb = dd.shape[0] // i
    return diag_blocks(name, dd, i)[:, :o].reshape(nb, i, o).transpose(0, 2, 1)


def s5_tables(a_re, a_im, log_step, b_re, b_im, c_re, c_im):
    step = jnp.exp(log_step)[:, None]
    j = jnp.arange(S5_SUB, dtype=f32)[:, None, None]
    expo = jnp.concatenate([j + 1.0, j, -j], axis=0)
    mag = jnp.exp(expo * (a_re * step))
    pw_re, pw_im = mag * jnp.cos(expo * (a_im * step)), mag * jnp.sin(expo * (a_im * step))
    lam_re, lam_im = pw_re[0], pw_im[0]
    den = a_re * a_re + a_im * a_im
    coef_re = ((lam_re - 1.0) * a_re + lam_im * a_im) / den
    coef_im = (lam_im * a_re - (lam_re - 1.0) * a_im) / den
    bb_re = coef_re[..., None] * b_re - coef_im[..., None] * b_im
    bb_im = coef_re[..., None] * b_im + coef_im[..., None] * b_re
    gl = S5_GROUPS // S5_CB
    eye = jnp.eye(gl, dtype=f32)

    def blk_b(t):
        t4 = t.transpose(0, 2, 1).reshape(S5_CB, gl, S5_GROUP, S5_STATE)
        return jnp.einsum("kgcn,gh->kgchn", t4, eye).reshape(S5_CB, gl * S5_GROUP, gl * S5_STATE)

    def blk_c(t):
        t4 = t.reshape(S5_CB, gl, S5_GROUP, S5_STATE)
        return jnp.einsum("kgcn,gh->kgnhc", t4, eye).reshape(S5_CB, gl * S5_STATE, gl * S5_GROUP)

    def blk_p(t):
        return t.reshape(t.shape[0], S5_CB, gl * S5_STATE).transpose(1, 0, 2)

    bb = jnp.concatenate([blk_b(bb_re), blk_b(bb_im)], axis=2)
    cc = jnp.concatenate([blk_c(c_re), -blk_c(c_im)], axis=1)
    pw = jnp.concatenate([blk_p(pw_re), blk_p(pw_im)], axis=2)
    return bb, cc, pw


def ffn_step_bwd(dy, x, nw, wts, saved, ride_act=None, ride_w=None):
    h, g, u = saved
    dx, dnw, dg, du, a, *got_act = ffn_bwd_act(dy, x, nw, g, u, *wts, ride=ride_act)
    dwg, dwu, dwd, *got_w = ffn_bwd_w(h, dy, dg, du, a, ride=ride_w)
    return dx, dnw, (dwg, dwu, dwd), got_act, got_w


def hybrid_fwd(x1, p, seq, ride_in=None):
    u = rms_fwd(x1, p["mix_norm"])
    proj, *got_in = matmul("hy_in", u, p["win"], tn=256, ride=ride_in) if ride_in else (matmul("hy_in", u, p["win"], tn=256),)
    xbc = conv_fwd("ssd_conv", proj, OFF_XBC, p["ssd_conv_w"], p["ssd_conv_b"], seq)
    yraw, hsave = ssd_fwd(xbc, proj, p["a_log"], p["dt_bias"], p["ssd_d"], seq)
    yssd = rowwise_fwd("ssd_epi", f_ssd_epi, [yraw, proj], [(D_MODEL, 0), (D_MODEL, OFF_Z // D_MODEL)], [p["ssd_norm_w"]], D_MODEL)
    xc = conv_fwd("ml_conv", proj, OFF_MX, p["ml_conv_w"], p["ml_conv_b"], seq)
    q = matmul("hw_q", xc, p["wq"])
    k = matmul("hw_k", xc, p["wk"])
    v = matmul("hw_v", proj, p["wv"], a_off=OFF_MX, a_width=D_MODEL)
    g1 = matmul("gate_q", q, p["wif_q"])
    g2 = matmul("gate_k", k, p["wif_k"])
    g3 = matmul("gate_v", v, p["wif_v"])
    hm, mlsave = ml_fwd(q, k, v, g1, g2, g3, p["b_if"], seq)
    yml = rowwise_fwd("ml_epi", f_ml_epi, [hm, xc, proj], [(D_MODEL, 0), (D_MODEL, 0), (D_MODEL, OFF_MZ // D_MODEL)],
                      [p["ml_norm_w"], p["ml_skip"]], D_MODEL)
    t = matmul("hy_out1", yssd, p["wo1"], add=x1)
    x2 = matmul("hy_out2", yml, p["wo2"], add=t)
    return x2, (u, proj, xbc, yraw, hsave, yssd, xc, q, k, v, g1, g2, g3, hm, mlsave, yml), got_in


def hybrid_bwd(dx2, x1, p, saved, seq, ride_dwin=None):
    u, proj, xbc, yraw, hsave, yssd, xc, q, k, v, g1, g2, g3, hm, mlsave, yml = saved
    gr = {}
    dyssd = matmul("d_yssd", dx2, p["wo1"], cb=1)
    dyml = matmul("d_yml", dx2, p["wo2"], cb=1)
    gr["wo"] = jnp.concatenate([matmul("dw_o1", yssd, dx2, ca=0), matmul("dw_o2", yml, dx2, ca=0)], axis=0)
    d_hm, d_xc, d_mz, gr["ml_norm_w"], gr["ml_skip"] = rowwise_bwd(
        "ml_epi_bwd", f_ml_epi, [hm, xc, proj], [(D_MODEL, 0), (D_MODEL, 0), (D_MODEL, OFF_MZ // D_MODEL)],
        [p["ml_norm_w"], p["ml_skip"]], dyml)
    dq, dk, dv, dgt, gr["b_if"] = ml_bwd(d_hm, q, k, v, g1, g2, g3, p["b_if"], mlsave, seq)
    dq = matmul("dq_gate", dgt, p["wif_q"], cb=1, add=dq)
    dk = matmul("dk_gate", dgt, p["wif_k"], cb=1, add=dk)
    dv = matmul("dv_gate", dgt, p["wif_v"], cb=1, add=dv)
    gr["wif"] = jnp.concatenate([matmul("dw_if_q", q, dgt, ca=0), matmul("dw_if_k", k, dgt, ca=0),
                                 matmul("dw_if_v", v, dgt, ca=0)], axis=0)
    d_xc = matmul("dxc_q", dq, p["wq"], cb=1, add=d_xc)
    d_xc = matmul("dxc_k", dk, p["wk"], cb=1, add=d_xc)
    gr["wq"] = matmul("dw_q", xc, dq, ca=0)
    gr["wk"] = matmul("dw_k", xc, dk, ca=0)
    gr["wv"] = matmul("dw_v", proj, dv, ca=0, a_off=OFF_MX, a_width=D_MODEL)
    d_mx, gr["ml_conv_w"], gr["ml_conv_b"] = conv_bwd("ml_conv_bwd", d_xc, proj, OFF_MX, p["ml_conv_w"], p["ml_conv_b"], seq)
    d_mx = matmul("dmx_v", dv, p["wv"], cb=1, add=d_mx)
    d_yraw, d_z, gr["ssd_norm_w"] = rowwise_bwd("ssd_epi_bwd", f_ssd_epi, [yraw, proj],
                                                [(D_MODEL, 0), (D_MODEL, OFF_Z // D_MODEL)], [p["ssd_norm_w"]], dyssd)
    d_xs, d_b, d_c, d_dt, gr["a_log"], gr["dt_bias"], gr["ssd_d"] = ssd_bwd(
        d_yraw, xbc, proj, p["a_log"], p["dt_bias"], p["ssd_d"], hsave, seq)
    d_xbc, gr["ssd_conv_w"], gr["ssd_conv_b"] = conv_bwd("ssd_conv_bwd", jnp.concatenate([d_xs, d_b, d_c], axis=1), proj, OFF_XBC,
                                                         p["ssd_conv_w"], p["ssd_conv_b"], seq)
    dproj = jnp.concatenate([d_z, d_mx, d_mz, d_xbc, d_dt, jnp.zeros((d_dt.shape[0], PROJ_W - OFF_DT - LANES), f32)], axis=1)
    dwin, *got_dwin = matmul("dw_in", u, dproj, ca=0, tn=256, ride=ride_dwin) if ride_dwin else (matmul("dw_in", u, dproj, ca=0, tn=256),)
    out_ops = [(_shards(win_from_padded(dwin)[None], 2).astype(bf16), "scatter"), (_shards(gr.pop("wo")[None], 1).astype(bf16), "scatter")]
    du, part_win, part_wo = matmul("d_u", dproj, p["win"], cb=1, tk=256, ride=out_ops)
    dx1, gr["mix_norm"] = rms_bwd([du], x1, p["mix_norm"], dx2)
    return dx1, gr, got_dwin, (part_win, part_wo)


def s5_layer_fwd(x4, p, seq):
    u = rms_fwd(x4, p["mix_norm"])
    ys, carries = s5_fwd(u, p["bb"], p["cc"], p["pw"], seq)
    gg = rowwise_fwd("s5_post", f_s5_post, [ys, u], [(D_MODEL, 0), (D_MODEL, 0)], [p["s5_d"]], D_MODEL)
    pab = matmul("s5_ab", gg, p["wab"])
    x5 = rowwise_fwd("s5_glu", f_glu_res, [pab, x4], [(2 * D_MODEL, 0), (D_MODEL, 0)], [p["b_a"], p["b_b"]], D_MODEL)
    return x5, (u, ys, carries, gg, pab)


def s5_layer_bwd(dx5, x4, p, saved, seq):
    u, ys, carries, gg, pab = saved
    gr = {}
    dpab, gr["b_a"], gr["b_b"] = rowwise_bwd("s5_glu_bwd", f_glu, [pab], [(2 * D_MODEL, 0)], [p["b_a"], p["b_b"]], dx5)
    dgg = matmul("d_gg", dpab, p["wab"], cb=1)
    gr["wab"] = matmul("dw_ab", gg, dpab, ca=0)
    dys, du_a, gr["s5_d"] = rowwise_bwd("s5_post_bwd", f_s5_post, [ys, u], [(D_MODEL, 0), (D_MODEL, 0)], [p["s5_d"]], dgg)
    du_b, gr["bb"], gr["cc"], gr["pw"] = s5_bwd(dys, u, p["bb"], p["cc"], p["pw"], carries, seq)
    dx4, gr["mix_norm"] = rms_bwd([du_a, du_b], x4, p["mix_norm"], dx5)
    return dx4, gr


BIG = ["ffn1_w_gate", "ffn1_w_up", "ffn1_w_down", "ffn2_w_gate", "ffn2_w_up", "ffn2_w_down", "hy_w_in", "hy_w_out", "s5_w_a", "s5_w_b"]
SMALL_SHARDED = {"ssd_conv_w": 2, "ml_conv_w": 2, "ml_w_q": 1, "ml_w_k": 1, "ml_w_v": 1, "ml_w_if": 1, "s5_d": 1, "s5_b_a": 1, "s5_b_b": 1}
WEIGHTS = ["ffn1_norm", "ffn1_w_gate", "ffn1_w_up", "ffn1_w_down", "mix_norm", "ffn2_norm", "ffn2_w_gate", "ffn2_w_up", "ffn2_w_down",
           "hy_w_in", "ssd_conv_w", "ssd_conv_b", "ssd_dt_bias", "ssd_a_log", "ssd_d", "ssd_norm_w", "ml_conv_w", "ml_conv_b",
           "ml_w_q", "ml_w_k", "ml_w_v", "ml_w_if", "ml_b_if", "ml_norm_w", "ml_skip", "hy_w_out", "s5_a_re", "s5_a_im",
           "s5_log_step", "s5_b_re", "s5_b_im", "s5_c_re", "s5_c_im", "s5_d", "s5_w_a", "s5_b_a", "s5_w_b", "s5_b_b", "final_norm"]
S5_PARAMS = ["s5_a_re", "s5_a_im", "s5_log_step", "s5_b_re", "s5_b_im", "s5_c_re", "s5_c_im"]
SMALL_S5 = S5_PARAMS + ["s5_d", "s5_b_a", "s5_b_b"]
SMALL_REST = [n for n in WEIGHTS if n not in BIG and n not in SMALL_S5]
SMALL = SMALL_REST + SMALL_S5


def _unshard(g, axis):
    return jnp.concatenate([g[i] for i in range(N_DEV)], axis=axis)


def assemble(gw, rep):
    padn = lambda w: jnp.pad(w, ((0, 0), (0, LANES - w.shape[1]))).astype(bf16)
    wif = _unshard(gw["ml_w_if"], 1)[0]
    wo = _unshard(gw["hy_w_out"], 1)[0].astype(bf16)
    dense = lambda n: headwise_dense(_unshard(gw[n], 1)[0].astype(f32)).astype(bf16)
    w0 = dict(mix_norm=rep["mix_norm"][0:1],
              win=win_to_padded(_unshard(gw["hy_w_in"], 2)[0]).astype(bf16),
              ssd_conv_w=_unshard(gw["ssd_conv_w"], 2)[0], ssd_conv_b=rep["ssd_conv_b"],
              a_log=_lanes(rep["ssd_a_log"]), dt_bias=_lanes(rep["ssd_dt_bias"]), ssd_d=_lanes(rep["ssd_d"]),
              ssd_norm_w=rep["ssd_norm_w"], ml_conv_w=_unshard(gw["ml_conv_w"], 2)[0], ml_conv_b=rep["ml_conv_b"],
              wq=dense("ml_w_q"), wk=dense("ml_w_k"), wv=dense("ml_w_v"),
              wif_q=padn(wif[0:1024]), wif_k=padn(wif[1024:2048]), wif_v=padn(wif[2048:3072]),
              b_if=_lanes(rep["ml_b_if"]), ml_norm_w=rep["ml_norm_w"], ml_skip=rep["ml_skip"],
              wo1=wo[:D_MODEL], wo2=wo[D_MODEL:])
    bb, cc, pw = s5_tables(*[rep[n][0] for n in S5_PARAMS])
    wab = jnp.concatenate([_unshard(gw["s5_w_a"], 1)[0], _unshard(gw["s5_w_b"], 1)[0]], axis=1).astype(bf16)
    w1 = dict(mix_norm=rep["mix_norm"][1:2], bb=bb, cc=cc, pw=pw,
              s5_d=_unshard(gw["s5_d"], 1), wab=wab, b_a=_unshard(gw["s5_b_a"], 1), b_b=_unshard(gw["s5_b_b"], 1))
    return w0, w1


def _shards(full, axis):
    return jnp.stack(jnp.split(full, N_DEV, axis=axis), axis=0)


def small_grads(g_norms, g_hy, g_s5, d_final, rep):
    small = dict(g_norms)
    small["mix_norm"] = jnp.concatenate([g_hy["mix_norm"], g_s5["mix_norm"]], axis=0)
    small["ssd_conv_w"] = g_hy["ssd_conv_w"][None]
    small["ssd_conv_b"] = g_hy["ssd_conv_b"]
    small["ssd_dt_bias"] = g_hy["dt_bias"][:, :SSD_HEADS]
    small["ssd_a_log"] = g_hy["a_log"][:, :SSD_HEADS]
    small["ssd_d"] = g_hy["ssd_d"][:, :SSD_HEADS]
    small["ssd_norm_w"] = g_hy["ssd_norm_w"]
    small["ml_conv_w"] = g_hy["ml_conv_w"][None]
    small["ml_conv_b"] = g_hy["ml_conv_b"]
    for nm, key in (("ml_w_q", "wq"), ("ml_w_k", "wk"), ("ml_w_v", "wv")):
        small[nm] = headwise_from_dense("diag_" + key, g_hy[key])[None]
    small["ml_w_if"] = g_hy["wif"][None, :, :2 * ML_HEADS]
    small["ml_b_if"] = g_hy["b_if"][:, :2 * ML_HEADS]
    small["ml_norm_w"] = g_hy["ml_norm_w"]
    small["ml_skip"] = g_hy["ml_skip"]
    small["final_norm"] = d_final.reshape(-1)
    return small


def s5_small_grads(g_s5, rep):
    small = {}
    _, tvjp = jax.vjp(s5_tables, *[rep[n][0] for n in S5_PARAMS])
    for n, g in zip(S5_PARAMS, tvjp((g_s5["bb"], g_s5["cc"], g_s5["pw"]))):
        small[n] = g[None]
    small["s5_d"] = g_s5["s5_d"]
    small["s5_b_a"] = g_s5["b_a"]
    small["s5_b_b"] = g_s5["b_b"]
    return small


ROW = 1024
F32_ROWS = 8


def _piece_rows(size):
    return -(-size // (ROW * F32_ROWS)) * F32_ROWS


def _pack(arrays):
    pieces = []
    for a in arrays:
        flat = a.astype(f32).reshape(-1)
        pieces.append(jnp.pad(flat, (0, _piece_rows(a.size) * ROW - a.size)).reshape(-1, ROW))
    return jnp.concatenate(pieces, axis=0)


def _unpack(buf, shapes):
    out, r0 = [], 0
    lead = buf.shape[:-2]
    for shp in shapes:
        size = math.prod(shp)
        r = _piece_rows(size)
        out.append(buf[..., r0:r0 + r, :].reshape(lead + (-1,))[..., :size].reshape(lead + tuple(shp)))
        r0 += r
    return out


def _tile_rows(r):
    for t in (512, 256, 128, 64, 32, 16, 8):
        if r % t == 0:
            return t
    return r


def _flat2d(a):
    return a.reshape(-1, a.shape[-1])


def kernel(x, ffn1_norm, ffn1_w_gate, ffn1_w_up, ffn1_w_down, mix_norm, ffn2_norm, ffn2_w_gate, ffn2_w_up, ffn2_w_down, hy_w_in, ssd_conv_w, ssd_conv_b, ssd_dt_bias, ssd_a_log, ssd_d, ssd_norm_w, ml_conv_w, ml_conv_b, ml_w_q, ml_w_k, ml_w_v, ml_w_if, ml_b_if, ml_norm_w, ml_skip, hy_w_out, s5_a_re, s5_a_im, s5_log_step, s5_b_re, s5_b_im, s5_c_re, s5_c_im, s5_d, s5_w_a, s5_b_a, s5_w_b, s5_b_b, final_norm, loss_target, m_ffn1_norm, m_ffn1_w_gate, m_ffn1_w_up, m_ffn1_w_down, m_mix_norm, m_ffn2_norm, m_ffn2_w_gate, m_ffn2_w_up, m_ffn2_w_down, m_hy_w_in, m_ssd_conv_w, m_ssd_conv_b, m_ssd_dt_bias, m_ssd_a_log, m_ssd_d, m_ssd_norm_w, m_ml_conv_w, m_ml_conv_b, m_ml_w_q, m_ml_w_k, m_ml_w_v, m_ml_w_if, m_ml_b_if, m_ml_norm_w, m_ml_skip, m_hy_w_out, m_s5_a_re, m_s5_a_im, m_s5_log_step, m_s5_b_re, m_s5_b_im, m_s5_c_re, m_s5_c_im, m_s5_d, m_s5_w_a, m_s5_b_a, m_s5_w_b, m_s5_b_b, m_final_norm, v_ffn1_norm, v_ffn1_w_gate, v_ffn1_w_up, v_ffn1_w_down, v_mix_norm, v_ffn2_norm, v_ffn2_w_gate, v_ffn2_w_up, v_ffn2_w_down, v_hy_w_in, v_ssd_conv_w, v_ssd_conv_b, v_ssd_dt_bias, v_ssd_a_log, v_ssd_d, v_ssd_norm_w, v_ml_conv_w, v_ml_conv_b, v_ml_w_q, v_ml_w_k, v_ml_w_v, v_ml_w_if, v_ml_b_if, v_ml_norm_w, v_ml_skip, v_hy_w_out, v_s5_a_re, v_s5_a_im, v_s5_log_step, v_s5_b_re, v_s5_b_im, v_s5_c_re, v_s5_c_im, v_s5_d, v_s5_w_a, v_s5_b_a, v_s5_w_b, v_s5_b_b, v_final_norm):
    given = dict(locals())
    w = {n: given[n] for n in WEIGHTS}
    mom = {n: given["m_" + n] for n in WEIGHTS}
    var = {n: given["v_" + n] for n in WEIGHTS}
    bl, seq, d = x.shape
    me = 4 * lax.axis_index("x") + 2 * lax.axis_index("y") + lax.axis_index("c")

    x0, tgt = x.reshape(bl * seq, d), loss_target.reshape(bl * seq, d)
    rep = {n: w[n] for n in WEIGHTS if n not in BIG and n not in SMALL_SHARDED}
    ffn_w = ("_w_gate", "_w_up", "_w_down")

    def ffn_gather(pre, l):
        return [(w[pre + s][l:l + 1].astype(bf16), "gather") for s in ffn_w]

    def ffn_mats(got):
        wg, wu, wd = got
        cols = lambda a: a[:, 0].transpose(1, 0, 2).reshape(a.shape[2], -1)
        return cols(wg), cols(wu), wd.reshape(-1, wd.shape[3])

    def scatter(parts):
        return [(p, "scatter") for p in parts]

    def ffn_scatter(dws):
        dwg, dwu, dwd = dws
        cols = lambda a: a.reshape(a.shape[0], N_DEV, -1).transpose(1, 0, 2)[:, None]
        return scatter([cols(dwg), cols(dwu), dwd.reshape(N_DEV, 1, -1, dwd.shape[1])])

    wf10 = ffn_mats(exchange("gather_ffn1_l0", ffn_gather("ffn1", 0)))
    mixer_ops = [(w[n].astype(bf16), "gather") for n in ("hy_w_in", "hy_w_out", "s5_w_a", "s5_w_b")]
    mixer_ops.append((_pack([w[n] for n in SMALL_SHARDED]), "gather"))
    x1, *rest = ffn_fwd(x0, ffn1_norm[0:1], *wf10, ride=mixer_ops)
    sv10, got = rest[:3], rest[3:]
    gw = dict(zip(("hy_w_in", "hy_w_out", "s5_w_a", "s5_w_b"), got[:4]))
    gw.update(zip(SMALL_SHARDED, _unpack(got[4], [w[n].shape for n in SMALL_SHARDED])))
    w0, w1 = assemble(gw, rep)
    x2, sv_h, got = hybrid_fwd(x1, w0, seq, ride_in=ffn_gather("ffn2", 0) + ffn_gather("ffn1", 1))
    wf20, wf11 = ffn_mats(got[:3]), ffn_mats(got[3:])
    x3, *rest = ffn_fwd(x2, ffn2_norm[0:1], *wf20, ride=ffn_gather("ffn2", 1))
    sv20, wf21 = rest[:3], ffn_mats(rest[3:])
    x4, *sv11 = ffn_fwd(x3, ffn1_norm[1:2], *wf11)
    x5, sv_s = s5_layer_fwd(x4, w1, seq)
    x6, *sv21 = ffn_fwd(x5, ffn2_norm[1:2], *wf21)
    loss, dx6, d_final = loss_head(x6, final_norm.reshape(1, d), tgt)

    dx5, dn21, dw21, _, _ = ffn_step_bwd(dx6, x5, ffn2_norm[1:2], wf21, sv21)
    dx4, g_s5 = s5_layer_bwd(dx5, x4, w1, sv_s, seq)
    dwab = g_s5.pop("wab")
    s5_ops = scatter([_shards(dwab[None, :, :D_MODEL], 1).astype(bf16), _shards(dwab[None, :, D_MODEL:], 1).astype(bf16)])
    dx3, dn11, dw11, p21, p_s5 = ffn_step_bwd(dx4, x3, ffn1_norm[1:2], wf11, sv11, ride_act=ffn_scatter(dw21), ride_w=s5_ops)
    small = s5_small_grads(g_s5, rep)
    dx2, dn20, dw20, p11, (parts_s5,) = ffn_step_bwd(dx3, x2, ffn2_norm[0:1], wf20, sv20, ride_act=ffn_scatter(dw11),
                                                      ride_w=[(_pack([small[n] for n in SMALL_S5]), "gather")])
    dx1, g_hy, p20, p_hy = hybrid_bwd(dx2, x1, w0, sv_h, seq, ride_dwin=ffn_scatter(dw20))
    dx0, dn10, dw10, _, _ = ffn_step_bwd(dx1, x0, ffn1_norm[0:1], wf10, sv10)
    g_norms = {"ffn1_norm": jnp.concatenate([dn10, dn11], axis=0), "ffn2_norm": jnp.concatenate([dn20, dn21], axis=0)}
    small.update(small_grads(g_norms, g_hy, g_s5, d_final, rep))
    *p10, parts_rest = exchange("reduce_tail", ffn_scatter(dw10) + [(_pack([small[n] for n in SMALL_REST]), "gather")])
    small_parts = jnp.concatenate([parts_rest, parts_s5], axis=1)
    small_sum = sum_parts("sum_small", small_parts, tr=_tile_rows(small_parts.shape[1]))

    out_g, out_d, out_m, out_v = {}, {}, {}, {}
    ffn_parts = {"ffn1": (p10, p11), "ffn2": (p20, p21)}
    for pre in ("ffn1", "ffn2"):
        for k, s in enumerate(ffn_w):
            n = pre + s
            r, c = w[n].shape[1:]
            res = None
            for l in (1, 0):
                res = adamw_layer("adamw_" + n, ffn_parts[pre][l][k].reshape(N_DEV, r, c), w[n], mom[n], var[n], l, res,
                                  tr=_tile_rows(r))
            out_g[n], out_d[n], out_m[n], out_v[n] = res
    for n, parts in zip(("hy_w_in", "hy_w_out", "s5_w_a", "s5_w_b"), tuple(p_hy) + tuple(p_s5)):
        shp = w[n].shape
        w2 = _flat2d(w[n])
        res = adamw("adamw_" + n, parts.reshape((N_DEV,) + w2.shape), w2, _flat2d(mom[n]), _flat2d(var[n]),
                    tr=_tile_rows(w2.shape[0]))
        out_g[n], out_d[n], out_m[n], out_v[n] = [a.reshape(shp) for a in res]
    g_small = {}
    for n, full in zip(SMALL, _unpack(small_sum, [small[n].shape for n in SMALL])):
        if n in SMALL_SHARDED:
            ax = SMALL_SHARDED[n]
            full = lax.dynamic_slice_in_dim(full, me * w[n].shape[ax], w[n].shape[ax], axis=ax)
        g_small[n] = full
    packs = [_pack([t[n] for n in SMALL]) for t in (g_small, w, mom, var)]
    res = adamw("adamw_small", packs[0][None], packs[1], packs[2], packs[3], tr=_tile_rows(packs[0].shape[0]))
    for dst, a in zip((out_g, out_d, out_m, out_v), res):
        dst.update(zip(SMALL, _unpack(a, [w[n].shape for n in SMALL])))

    total = lax.psum(loss[0, 0], ("x", "y", "c"))
    return (total, dx0.reshape(bl, seq, d), *[out_g[n] for n in WEIGHTS], *[out_d[n] for n in WEIGHTS],
            *[out_m[n] for n in WEIGHTS], *[out_v[n] for n in WEIGHTS])
```

```python
import functools
import math

import jax
import jax.numpy as jnp
from jax import lax
from jax.experimental import pallas as pl
from jax.experimental.pallas import tpu as pltpu

f32 = jnp.float32
bf16 = jnp.bfloat16

N_DEV = 8
D_MODEL = 1024
D_FF = 2816
EPS = 1e-6
FFN_RES = 0.5
CONV_W = 4
SSD_HEADS = 16
SSD_HEAD_DIM = 64
SSD_GROUPS = 2
SSD_STATE = 128
SSD_HG = SSD_HEADS // SSD_GROUPS
SSD_GW = SSD_HG * SSD_HEAD_DIM
CHUNK = 128
ML_HEADS = 4
ML_HD = 256
S5_GROUP = 16
S5_GROUPS = 64
S5_STATE = 64
S5_CB = 8
S5_CH = (S5_GROUPS // S5_CB) * S5_STATE
S5_TL = 256
S5_SUB = 16
LANES = 128
IN_COLS = 4624
PROJ_W = 4864
OFF_Z, OFF_MX, OFF_MZ, OFF_XBC, OFF_DT = 0, 1024, 2048, 3072, 4608
ADAM_LR, ADAM_B1, ADAM_B2, ADAM_EPS, ADAM_WD, ADAM_STEP = 0.001, 0.9, 0.999, 1e-08, 0.01, 10
NEG = -1e30
VMEM_LIMIT = 56 * 1024 * 1024
HI = lax.Precision.HIGHEST


def _cp(n):
    return pltpu.CompilerParams(dimension_semantics=("arbitrary",) * n, vmem_limit_bytes=VMEM_LIMIT)


def _dg(a, b, ca, cb):
    return lax.dot_general(a.astype(bf16), b.astype(bf16), (((ca,), (cb,)), ((), ())), preferred_element_type=f32)


@functools.partial(jax.custom_vjp, nondiff_argnums=(2, 3))
def bdot(a, b, ca, cb):
    return _dg(a, b, ca, cb)


def _bdot_fwd(a, b, ca, cb):
    return _dg(a, b, ca, cb), (a, b)


def _bdot_bwd(ca, cb, res, ct):
    a, b = res
    da = _dg(ct, b, 1, 1 - cb) if ca == 1 else _dg(b, ct, 1 - cb, 1)
    db = _dg(a, ct, 1 - ca, 0) if cb == 0 else _dg(ct, a, 0, 1 - ca)
    return da, db


bdot.defvjp(_bdot_fwd, _bdot_bwd)


def hdot(a, b):
    return jnp.dot(a, b, precision=HI, preferred_element_type=f32)


def _iota(shape, dim):
    return lax.broadcasted_iota(jnp.int32, shape, dim)


def _tri(n):
    return (_iota((n, n), 0) >= _iota((n, n), 1))


@functools.partial(jax.custom_vjp, nondiff_argnums=(1,))
def tshift(x, k):
    return jnp.where(_iota(x.shape, 0) >= k, pltpu.roll(x, k, 0), 0.0)


def _tshift_fwd(x, k):
    return tshift(x, k), None


def _tshift_bwd(k, _, ct):
    n = ct.shape[0]
    return (jnp.where(_iota(ct.shape, 0) < n - k, pltpu.roll(ct, n - k, 0), 0.0),)


tshift.defvjp(_tshift_fwd, _tshift_bwd)


def _lane_pick(a, idx):
    return jnp.sum(jnp.where(_iota(a.shape, 1) == idx, a, 0.0), axis=1, keepdims=True)


def _row_pick(a, idx):
    return jnp.sum(jnp.where(_iota(a.shape, 0) == idx, a, 0.0), axis=0, keepdims=True)


def _silu(x):
    return x * jax.nn.sigmoid(x)


def map_fwd(name, f, grid, ins, in_specs, out_shapes, out_specs):
    n_in = len(ins)

    def body(*refs):
        pids = tuple(pl.program_id(i) for i in range(len(grid)))
        outs = f(pids, *[r[...] for r in refs[:n_in]])
        for r, o in zip(refs[n_in:], outs):
            r[...] = o.astype(r.dtype)

    return pl.pallas_call(body, name=name, grid=grid, in_specs=in_specs, out_specs=out_specs,
                          out_shape=out_shapes, compiler_params=_cp(len(grid)))(*ins)


def scan_fwd(name, f, grid, slot_axis, ins, in_specs, out_shapes, out_specs, state_shapes, state_init, save_shapes, save_specs):
    n_in, n_out, n_st = len(ins), len(out_shapes), len(state_shapes)
    n_slots = grid[slot_axis]
    cax = len(grid) - 1 if slot_axis != len(grid) - 1 else len(grid) - 2

    def body(*refs):
        pids = tuple(pl.program_id(i) for i in range(len(grid)))
        in_refs, out_refs = refs[:n_in], refs[n_in:n_in + n_out]
        save_refs = refs[n_in + n_out:n_in + n_out + n_st]
        st_refs = refs[n_in + n_out + n_st:]
        slot = pids[slot_axis]

        @pl.when(pids[cax] == 0)
        def _():
            for s, init in zip(st_refs, state_init):
                s[slot] = jnp.full(s.shape[1:], init, f32)

        states = tuple(s[slot] for s in st_refs)
        for sv, st in zip(save_refs, states):
            sv[...] = st.reshape(sv.shape)
        outs, new = f(pids, states, *[r[...] for r in in_refs])
        for r, o in zip(out_refs, outs):
            r[...] = o.astype(r.dtype)
        for s, v in zip(st_refs, new):
            s[slot] = v

    scratch = [pltpu.VMEM((n_slots,) + tuple(s), f32) for s in state_shapes]
    return pl.pallas_call(body, name=name, grid=grid, in_specs=in_specs, out_specs=list(out_specs) + list(save_specs),
                          out_shape=list(out_shapes) + list(save_shapes), scratch_shapes=scratch,
                          compiler_params=_cp(len(grid)))(*ins)


def scan_bwd(name, f, grid, slot_axis, ins, in_specs, saves, save_specs, cts, ct_specs, state_shapes, wrt, acc_first):
    n_in, n_st, n_ct = len(ins), len(saves), len(cts)
    n_slots = grid[slot_axis]
    cax = len(grid) - 1 if slot_axis != len(grid) - 1 else len(grid) - 2

    def body(*refs):
        pids = tuple(pl.program_id(i) for i in range(len(grid)))
        in_refs = refs[:n_in]
        save_refs = refs[n_in:n_in + n_st]
        ct_refs = refs[n_in + n_st:n_in + n_st + n_ct]
        out_refs = refs[n_in + n_st + n_ct:n_in + n_st + n_ct + len(wrt)]
        dst_refs = refs[n_in + n_st + n_ct + len(wrt):]
        slot = pids[slot_axis]

        @pl.when(pids[cax] == 0)
        def _():
            for s in dst_refs:
                s[slot] = jnp.zeros(s.shape[1:], f32)

        vals = [r[...] for r in in_refs]
        states = tuple(sv[...].reshape(shp) for sv, shp in zip(save_refs, state_shapes))
        ctv = tuple(r[...].astype(f32) for r in ct_refs)
        dnew = tuple(s[slot] for s in dst_refs)

        def g(st, *dv):
            full = list(vals)
            for i, v in zip(wrt, dv):
                full[i] = v
            outs, new = f(pids, st, *full)
            return tuple(outs), tuple(new)

        _, vjp = jax.vjp(g, states, *[vals[i] for i in wrt])
        grads = vjp((ctv, dnew))
        for s, v in zip(dst_refs, grads[0]):
            s[slot] = v
        for i, o_ref, gr in zip(wrt, out_refs, grads[1:]):
            first = acc_first.get(i)
            if first is None:
                o_ref[...] = gr.astype(o_ref.dtype)
            else:
                @pl.when(first(pids))
                def _():
                    o_ref[...] = jnp.zeros_like(o_ref)
                o_ref[...] += gr

    out_shapes = [jax.ShapeDtypeStruct(ins[i].shape, f32) for i in wrt]
    out_specs = [in_specs[i] for i in wrt]
    scratch = [pltpu.VMEM((n_slots,) + tuple(s), f32) for s in state_shapes]
    return pl.pallas_call(body, name=name, grid=grid, in_specs=list(in_specs) + list(save_specs) + list(ct_specs),
                          out_specs=out_specs, out_shape=out_shapes, scratch_shapes=scratch,
                          compiler_params=_cp(len(grid)))(*ins, *saves, *cts)


def _fit(dim, cap):
    if dim <= cap:
        return dim
    return max(t for t in range(LANES, cap + 1, LANES) if dim % t == 0)


def _matmul_tiles(m, n, kdim, ca):
    if ca == 1:
        return _fit(m, 512), _fit(n, 2432), _fit(kdim, 2432)
    return _fit(m, 1024), _fit(n, 1280), _fit(kdim, 512)


def matmul(name, a, b, ca=1, cb=0, add=None, out_dtype=f32, a_off=0, a_width=None, ride=None):
    rider = Rider(ride)
    nr = rider.n
    a_width = a.shape[1] if a_width is None else a_width
    kdim = b.shape[cb]
    n = b.shape[1 - cb]
    m = a.shape[0] if ca == 1 else a_width
    tm, tn, tk = _matmul_tiles(m, n, kdim, ca)
    assert m % tm == 0 and n % tn == 0 and kdim % tk == 0
    nk = kdim // tk
    if ca == 1:
        assert a_off % tk == 0 and a_width == kdim
        koff = a_off // tk
        a_spec = pl.BlockSpec((tm, tk), lambda i, j, k: (i, k + koff))
    else:
        assert a_off % tm == 0 and a.shape[0] == kdim
        ioff = a_off // tm
        a_spec = pl.BlockSpec((tk, tm), lambda i, j, k: (k, i + ioff))
    b_spec = pl.BlockSpec((tk, tn), lambda i, j, k: (k, j)) if cb == 0 else pl.BlockSpec((tn, tk), lambda i, j, k: (j, k))
    o_spec = pl.BlockSpec((tm, tn), lambda i, j, k: (i, j))
    has_add = add is not None

    n_in = 3 if has_add else 2
    grid = (m // tm, n // tn, nk)

    def body(*refs):
        a_ref, b_ref = refs[0], refs[1]
        add_ref = refs[2] if has_add else None
        r_ins, o_ref = refs[n_in:n_in + nr], refs[n_in + nr]
        r_outs, acc, sems = refs[n_in + nr + 1:n_in + 2 * nr + 1], refs[n_in + 2 * nr + 1], refs[n_in + 2 * nr + 2:]
        rider.start(grid, r_ins, r_outs, sems)
        k = pl.program_id(2)

        @pl.when(k == 0)
        def _():
            acc[...] = add_ref[...].astype(f32) if has_add else jnp.zeros_like(acc)

        acc[...] += _dg(a_ref[...], b_ref[...], ca, cb)

        @pl.when(k == nk - 1)
        def _():
            o_ref[...] = acc[...].astype(o_ref.dtype)

        rider.wait(grid, r_ins, r_outs, sems)

    ins = [a, b] + ([add] if has_add else [])
    specs = [a_spec, b_spec] + ([o_spec] if has_add else [])
    res = pl.pallas_call(body, name=name, grid=grid, in_specs=specs + rider.specs(), out_specs=[o_spec] + rider.specs(),
                         out_shape=[jax.ShapeDtypeStruct((m, n), out_dtype)] + rider.out_shapes(),
                         scratch_shapes=[pltpu.VMEM((tm, tn), f32)] + rider.scratch(), compiler_params=_cp(3))(*ins, *rider.arrays())
    return res if nr else res[0]


def f_rms(pids, x, w):
    r = lax.rsqrt(jnp.mean(x * x, axis=-1, keepdims=True) + EPS)
    return (x * r * w,)


def _row_spec(tm, width, col=0):
    return pl.BlockSpec((tm, width), lambda i: (i, col))


def _par_spec(shape):
    return pl.BlockSpec(shape, lambda *p: (0,) * len(shape))


def rms_fwd(x, w, tm=512):
    t, d = x.shape
    return map_fwd("rms_fwd", f_rms, (t // tm,), [x, w], [_row_spec(tm, d), _par_spec((1, d))],
                   [jax.ShapeDtypeStruct((t, d), f32)], [_row_spec(tm, d)])[0]


def rms_bwd(dys, x, w, dres, tm=512):
    t, d = x.shape
    n = len(dys)

    def body(*refs):
        x_ref, w_ref, dres_ref, dx_ref, dw_ref = refs[n:]
        dy = refs[0][...]
        for r in refs[1:n]:
            dy = dy + r[...]
        _, vjp = jax.vjp(lambda xx, ww: f_rms(None, xx, ww)[0], x_ref[...], w_ref[...])
        dx, dw = vjp(dy)
        dx_ref[...] = dx + dres_ref[...]

        @pl.when(pl.program_id(0) == 0)
        def _():
            dw_ref[...] = jnp.zeros_like(dw_ref)
        dw_ref[...] += dw

    return pl.pallas_call(body, name="rms_bwd", grid=(t // tm,),
                          in_specs=[_row_spec(tm, d)] * (n + 1) + [_par_spec((1, d)), _row_spec(tm, d)],
                          out_specs=[_row_spec(tm, d), _par_spec((1, d))],
                          out_shape=[jax.ShapeDtypeStruct((t, d), f32), jax.ShapeDtypeStruct((1, d), f32)],
                          compiler_params=_cp(1))(*dys, x, w, dres)


def loss_head(x, w, tgt, tm=512):
    t, d = x.shape

    def fl(xx, ww, tt):
        y = f_rms(None, xx, ww)[0]
        return 0.5 * jnp.sum(jnp.mean(jnp.square(y - tt), axis=-1, keepdims=True), axis=0, keepdims=True)

    def body(x_ref, w_ref, t_ref, loss_ref, dx_ref, dw_ref):
        val, vjp = jax.vjp(lambda xx, ww: fl(xx, ww, t_ref[...]), x_ref[...], w_ref[...])
        dx, dw = vjp(jnp.ones((1, 1), f32))
        dx_ref[...] = dx

        @pl.when(pl.program_id(0) == 0)
        def _():
            dw_ref[...] = jnp.zeros_like(dw_ref)
            loss_ref[...] = jnp.zeros_like(loss_ref)
        dw_ref[...] += dw
        loss_ref[...] += val

    return pl.pallas_call(body, name="loss_head", grid=(t // tm,),
                          in_specs=[_row_spec(tm, d), _par_spec((1, d)), _row_spec(tm, d)],
                          out_specs=[_par_spec((1, 1)), _row_spec(tm, d), _par_spec((1, d))],
                          out_shape=[jax.ShapeDtypeStruct((1, 1), f32), jax.ShapeDtypeStruct((t, d), f32),
                                     jax.ShapeDtypeStruct((1, d), f32)],
                          compiler_params=_cp(1))(x, w, tgt)


def ffn_fwd(x, nw, wg, wu, wd, tm=1024, ride=None):
    t, d = x.shape
    ns, _, _, fs = wg.shape
    rider = Rider(ride)
    nr = rider.n
    grid = (t // tm, ns)

    def body(*refs):
        x_ref, nw_ref, wg_ref, wu_ref, wd_ref = refs[:5]
        r_ins = refs[5:5 + nr]
        xo_ref, h_ref, g_ref, u_ref = refs[5 + nr:9 + nr]
        r_outs, acc, sems = refs[9 + nr:9 + 2 * nr], refs[9 + 2 * nr], refs[10 + 2 * nr:]
        rider.start(grid, r_ins, r_outs, sems)
        j = pl.program_id(1)

        @pl.when(j == 0)
        def _():
            h_ref[...] = f_rms(None, x_ref[...], nw_ref[...])[0].astype(bf16)
            acc[...] = jnp.zeros_like(acc)

        h = h_ref[...]
        g = jnp.dot(h, wg_ref[0, 0], preferred_element_type=f32)
        u = jnp.dot(h, wu_ref[0, 0], preferred_element_type=f32)
        g_ref[0] = g
        u_ref[0] = u
        acc[...] += jnp.dot((_silu(g) * u).astype(bf16), wd_ref[0, 0], preferred_element_type=f32)

        @pl.when(j == ns - 1)
        def _():
            xo_ref[...] = x_ref[...] + FFN_RES * acc[...]

        rider.wait(grid, r_ins, r_outs, sems)

    row = pl.BlockSpec((tm, d), lambda i, j: (i, 0))
    wcol = pl.BlockSpec((1, 1, d, fs), lambda i, j: (j, 0, 0, 0))
    wrow = pl.BlockSpec((1, 1, fs, d), lambda i, j: (j, 0, 0, 0))
    act = pl.BlockSpec((1, tm, fs), lambda i, j: (j, i, 0))
    return pl.pallas_call(body, name="ffn_fwd", grid=grid,
                          in_specs=[row, pl.BlockSpec((1, d), lambda i, j: (0, 0)), wcol, wcol, wrow] + rider.specs(),
                          out_specs=[row, row, act, act] + rider.specs(),
                          out_shape=[jax.ShapeDtypeStruct((t, d), f32), jax.ShapeDtypeStruct((t, d), bf16),
                                     jax.ShapeDtypeStruct((ns, t, fs), f32), jax.ShapeDtypeStruct((ns, t, fs), f32)]
                          + rider.out_shapes(),
                          scratch_shapes=[pltpu.VMEM((tm, d), f32)] + rider.scratch(),
                          compiler_params=_cp(2))(x, nw, wg, wu, wd, *rider.arrays())


def ffn_bwd_act(dy, x, nw, g, u, wg, wu, wd, tm=512, ride=None):
    t, d = x.shape
    ns, _, _, fs = wg.shape
    rider = Rider(ride)
    nr = rider.n
    grid = (t // tm, ns)

    def body(*refs):
        dy_ref, x_ref, nw_ref, g_ref, u_ref, wg_ref, wu_ref, wd_ref = refs[:8]
        r_ins = refs[8:8 + nr]
        dx_ref, dnw_ref, dg_ref, du_ref, a_ref, dyh_ref = refs[8 + nr:14 + nr]
        r_outs, acc, sems = refs[14 + nr:14 + 2 * nr], refs[14 + 2 * nr], refs[15 + 2 * nr:]
        rider.start(grid, r_ins, r_outs, sems)
        i, j = pl.program_id(0), pl.program_id(1)

        @pl.when(j == 0)
        def _():
            acc[...] = jnp.zeros_like(acc)
            dyh_ref[...] = (FFN_RES * dy_ref[...]).astype(bf16)

        dyh = dyh_ref[...]
        da = _dg(dyh, wd_ref[0, 0], 1, 1)
        gg, uu = g_ref[0], u_ref[0]
        sg = jax.nn.sigmoid(gg)
        si = gg * sg
        dgv = (da * uu * (sg * (1.0 + gg * (1.0 - sg)))).astype(bf16)
        duv = (da * si).astype(bf16)
        dg_ref[0] = dgv
        du_ref[0] = duv
        a_ref[0] = (si * uu).astype(bf16)
        acc[...] += _dg(dgv, wg_ref[0, 0], 1, 1) + _dg(duv, wu_ref[0, 0], 1, 1)

        @pl.when(j == ns - 1)
        def _():
            _, vjp = jax.vjp(lambda xx, ww: f_rms(None, xx, ww)[0], x_ref[...], nw_ref[...])
            dx, dw = vjp(acc[...])
            dx_ref[...] = dx + dy_ref[...]

            @pl.when(i == 0)
            def _():
                dnw_ref[...] = jnp.zeros_like(dnw_ref)
            dnw_ref[...] += dw

        rider.wait(grid, r_ins, r_outs, sems)

    row = pl.BlockSpec((tm, d), lambda i, j: (i, 0))
    wcol = pl.BlockSpec((1, 1, d, fs), lambda i, j: (j, 0, 0, 0))
    wrow = pl.BlockSpec((1, 1, fs, d), lambda i, j: (j, 0, 0, 0))
    act = pl.BlockSpec((1, tm, fs), lambda i, j: (j, i, 0))
    par = pl.BlockSpec((1, d), lambda i, j: (0, 0))
    return pl.pallas_call(body, name="ffn_bwd_act", grid=grid,
                          in_specs=[row, row, par, act, act, wcol, wcol, wrow] + rider.specs(),
                          out_specs=[row, par, act, act, act, row] + rider.specs(),
                          out_shape=[jax.ShapeDtypeStruct((t, d), f32), jax.ShapeDtypeStruct((1, d), f32)]
                          + [jax.ShapeDtypeStruct((ns, t, fs), bf16)] * 3 + [jax.ShapeDtypeStruct((t, d), bf16)]
                          + rider.out_shapes(),
                          scratch_shapes=[pltpu.VMEM((tm, d), f32)] + rider.scratch(),
                          compiler_params=_cp(2))(dy, x, nw, g, u, wg, wu, wd, *rider.arrays())


def ffn_bwd_w(h, dyh, dg, du, a, tk=1024, ride=None):
    t, d = h.shape
    ns, _, fs = dg.shape
    nk = t // tk
    rider = Rider(ride)
    nr = rider.n
    grid = (ns, nk)

    def body(*refs):
        h_ref, dy_ref, dg_ref, du_ref, a_ref = refs[:5]
        r_ins = refs[5:5 + nr]
        og, ou, od = refs[5 + nr:8 + nr]
        r_outs = refs[8 + nr:8 + 2 * nr]
        ag, au, ad = refs[8 + 2 * nr:11 + 2 * nr]
        sems = refs[11 + 2 * nr:]
        rider.start(grid, r_ins, r_outs, sems)
        k = pl.program_id(1)

        @pl.when(k == 0)
        def _():
            ag[...] = jnp.zeros_like(ag)
            au[...] = jnp.zeros_like(au)
            ad[...] = jnp.zeros_like(ad)

        hh = h_ref[...]
        ag[...] += _dg(hh, dg_ref[0], 0, 0)
        au[...] += _dg(hh, du_ref[0], 0, 0)
        ad[...] += _dg(a_ref[0], dy_ref[...], 0, 0)

        @pl.when(k == nk - 1)
        def _():
            og[0, 0] = ag[...].astype(og.dtype)
            ou[0, 0] = au[...].astype(ou.dtype)
            od[0, 0] = ad[...].astype(od.dtype)

        rider.wait(grid, r_ins, r_outs, sems)

    row = pl.BlockSpec((tk, d), lambda j, k: (k, 0))
    act = pl.BlockSpec((1, tk, fs), lambda j, k: (j, k, 0))
    wcol = pl.BlockSpec((1, 1, d, fs), lambda j, k: (j, 0, 0, 0))
    wrow = pl.BlockSpec((1, 1, fs, d), lambda j, k: (j, 0, 0, 0))
    return pl.pallas_call(body, name="ffn_bwd_w", grid=grid, in_specs=[row, row, act, act, act] + rider.specs(),
                          out_specs=[wcol, wcol, wrow] + rider.specs(),
                          out_shape=[jax.ShapeDtypeStruct((ns, 1, d, fs), bf16)] * 2
                          + [jax.ShapeDtypeStruct((ns, 1, fs, d), bf16)] + rider.out_shapes(),
                          scratch_shapes=[pltpu.VMEM((d, fs), f32), pltpu.VMEM((d, fs), f32), pltpu.VMEM((fs, d), f32)]
                          + rider.scratch(),
                          compiler_params=_cp(2))(h, dyh, dg, du, a, *rider.arrays())


def f_conv(pids, x, w, b):
    y = b + x * w[CONV_W - 1:CONV_W, :]
    for j in range(CONV_W - 1):
        y = y + tshift(x, CONV_W - 1 - j) * w[j:j + 1, :]
    return (_silu(y),)


def _conv_specs(seq, col0, cb):
    xs = pl.BlockSpec((seq, cb), lambda c, b: (b, col0 + c))
    ws = pl.BlockSpec((CONV_W, cb), lambda c, b: (0, c))
    bs = pl.BlockSpec((1, cb), lambda c, b: (0, c))
    ys = pl.BlockSpec((seq, cb), lambda c, b: (b, c))
    return xs, ws, bs, ys


def conv_fwd(name, src, col_off, w, b, seq, cb=256):
    t = src.shape[0]
    c = w.shape[1]
    xs, ws, bs, ys = _conv_specs(seq, col_off // cb, cb)
    return map_fwd(name, f_conv, (c // cb, t // seq), [src, w, b], [xs, ws, bs],
                   [jax.ShapeDtypeStruct((t, c), f32)], [ys])[0]


def conv_bwd(name, dy, src, col_off, w, b, seq, cb=256):
    t = src.shape[0]
    c = w.shape[1]
    xs, ws, bs, ys = _conv_specs(seq, col_off // cb, cb)

    def body(x_ref, w_ref, b_ref, dy_ref, dx_ref, dw_ref, db_ref):
        _, vjp = jax.vjp(lambda xx, ww, bb: f_conv(None, xx, ww, bb)[0], x_ref[...], w_ref[...], b_ref[...])
        dx, dw, db = vjp(dy_ref[...])
        dx_ref[...] = dx

        @pl.when(pl.program_id(1) == 0)
        def _():
            dw_ref[...] = jnp.zeros_like(dw_ref)
            db_ref[...] = jnp.zeros_like(db_ref)
        dw_ref[...] += dw
        db_ref[...] += db

    return pl.pallas_call(body, name=name, grid=(c // cb, t // seq), in_specs=[xs, ws, bs, ys], out_specs=[ys, ws, bs],
                          out_shape=[jax.ShapeDtypeStruct((t, c), f32), jax.ShapeDtypeStruct(w.shape, f32),
                                     jax.ShapeDtypeStruct(b.shape, f32)], compiler_params=_cp(2))(src, w, b, dy)


def f_ssd(pids, states, xs, dtraw, bm, cm, a_log, dt_bias, d_skip):
    g = pids[2]
    (hn,) = states
    l = xs.shape[0]
    head_of_lane = _iota((LANES, SSD_GW), 1) // SSD_HEAD_DIM + SSD_HG * g
    expand = (_iota((LANES, SSD_GW), 0) == head_of_lane).astype(f32)
    tri = _tri(l)
    dt = jax.nn.softplus(dtraw + dt_bias)
    adt = dt * (-jnp.exp(a_log))
    cs = hdot(tri.astype(f32), adt)
    cst = cs.T
    cs_last = cs[l - 1:l, :]
    dt_e, cs_e, csl_e = hdot(dt, expand), hdot(cs, expand), hdot(cs_last, expand)
    xd = xs * dt_e
    gmat = bdot(cm, bm, 1, 1)
    half = _iota((l, LANES), 1) < SSD_HEAD_DIM
    blocks = []
    for pair in range(SSD_HG // 2):
        xb = xd[:, pair * LANES:(pair + 1) * LANES]
        res = []
        for sub in range(2):
            hid = SSD_HG * g + 2 * pair + sub
            col, row = _lane_pick(cs, hid), _row_pick(cst, hid)
            lm = jnp.exp(jnp.where(tri, col - row, NEG))
            res.append(bdot(gmat * lm, xb, 1, 0))
        blocks.append(jnp.where(half, res[0], res[1]))
    y = jnp.concatenate(blocks, axis=1)
    y = y + jnp.exp(cs_e) * bdot(cm, hn, 1, 0)
    y = y + hdot(d_skip, expand) * xs
    hn_new = jnp.exp(csl_e) * hn + bdot(bm, jnp.exp(csl_e - cs_e) * xd, 0, 0)
    return (y,), (hn_new,)


def _ssd_specs(seq, nch, rev):
    cc = (lambda c: nch - 1 - c) if rev else (lambda c: c)
    xs = pl.BlockSpec((CHUNK, SSD_GW), lambda b, c, g: (b * nch + cc(c), g))
    dt = pl.BlockSpec((CHUNK, LANES), lambda b, c, g: (b * nch + cc(c), OFF_DT // LANES))
    bm = pl.BlockSpec((CHUNK, SSD_STATE), lambda b, c, g: (b * nch + cc(c), 1024 // SSD_STATE + g))
    cm = pl.BlockSpec((CHUNK, SSD_STATE), lambda b, c, g: (b * nch + cc(c), 1024 // SSD_STATE + SSD_GROUPS + g))
    par = pl.BlockSpec((1, LANES), lambda b, c, g: (0, 0))
    sv = pl.BlockSpec((1, 1, SSD_STATE, SSD_GW), lambda b, c, g: (b * nch + cc(c), g, 0, 0))
    ddt = pl.BlockSpec((CHUNK, LANES), lambda b, c, g: (b * nch + cc(c), 0))
    dbc = pl.BlockSpec((CHUNK, SSD_STATE), lambda b, c, g: (b * nch + cc(c), g))
    return xs, dt, bm, cm, par, sv, ddt, dbc


def ssd_fwd(xbc, proj, a_log, dt_bias, d_skip, seq):
    t = xbc.shape[0]
    nch = seq // CHUNK
    xs, dt, bm, cm, par, sv, _, _ = _ssd_specs(seq, nch, False)
    grid = (t // seq, nch, SSD_GROUPS)
    y, hsave = scan_fwd("ssd_fwd", f_ssd, grid, 2, [xbc, proj, xbc, xbc, a_log, dt_bias, d_skip],
                        [xs, dt, bm, cm, par, par, par], [jax.ShapeDtypeStruct((t, SSD_GROUPS * SSD_GW), f32)], [xs],
                        [(SSD_STATE, SSD_GW)], [0.0],
                        [jax.ShapeDtypeStruct((t // CHUNK, SSD_GROUPS, SSD_STATE, SSD_GW), f32)], [sv])
    return y, hsave


def ssd_bwd(dy, xbc, proj, a_log, dt_bias, d_skip, hsave, seq):
    t = xbc.shape[0]
    nch = seq // CHUNK
    xs, dt, bm, cm, par, sv, ddt, dbc = _ssd_specs(seq, nch, True)
    grid = (t // seq, nch, SSD_GROUPS)

    def body(x_ref, dt_ref, b_ref, c_ref, al_ref, db_ref, ds_ref, h_ref, dy_ref,
             dxbc_x, dxbc_b, dxbc_c, ddt_ref, dal_ref, ddb_ref, dds_ref, dst):
        pids = tuple(pl.program_id(i) for i in range(3))
        slot = pids[2]

        @pl.when(pids[1] == 0)
        def _():
            dst[slot] = jnp.zeros(dst.shape[1:], f32)

        vals = [x_ref[...], dt_ref[...], b_ref[...], c_ref[...], al_ref[...], db_ref[...], ds_ref[...]]

        def gfun(st, *v):
            outs, new = f_ssd(pids, (st,), *v)
            return outs[0], new[0]

        _, vjp = jax.vjp(gfun, h_ref[0, 0], *vals)
        grads = vjp((dy_ref[...], dst[slot]))
        dst[slot] = grads[0]
        dxbc_x[...] = grads[1]
        dxbc_b[...] = grads[3]
        dxbc_c[...] = grads[4]

        @pl.when(slot == 0)
        def _():
            ddt_ref[...] = jnp.zeros_like(ddt_ref)
        ddt_ref[...] += grads[2]
        first = jnp.logical_and(jnp.logical_and(pids[0] == 0, pids[1] == 0), slot == 0)

        @pl.when(first)
        def _():
            dal_ref[...] = jnp.zeros_like(dal_ref)
            ddb_ref[...] = jnp.zeros_like(ddb_ref)
            dds_ref[...] = jnp.zeros_like(dds_ref)
        dal_ref[...] += grads[5]
        ddb_ref[...] += grads[6]
        dds_ref[...] += grads[7]

    bc_shape = jax.ShapeDtypeStruct((t, SSD_GROUPS * SSD_STATE), f32)
    par_shape = jax.ShapeDtypeStruct((1, LANES), f32)
    outs = pl.pallas_call(body, name="ssd_bwd", grid=grid, in_specs=[xs, dt, bm, cm, par, par, par, sv, xs],
                          out_specs=[xs, dbc, dbc, ddt, par, par, par],
                          out_shape=[jax.ShapeDtypeStruct((t, SSD_GROUPS * SSD_GW), f32),
                                     bc_shape, bc_shape, jax.ShapeDtypeStruct((t, LANES), f32),
                                     par_shape, par_shape, par_shape],
                          scratch_shapes=[pltpu.VMEM((SSD_GROUPS, SSD_STATE, SSD_GW), f32)],
                          compiler_params=_cp(3))(xbc, proj, xbc, xbc, a_log, dt_bias, d_skip, hsave, dy)
    return outs


def f_ssd_epi(pids, y, z, nw):
    yg = y * _silu(z)
    hw = yg.shape[1] // SSD_GROUPS
    parts = []
    for g in range(SSD_GROUPS):
        p = yg[:, g * hw:(g + 1) * hw]
        parts.append(p * lax.rsqrt(jnp.mean(p * p, axis=-1, keepdims=True) + EPS))
    return (jnp.concatenate(parts, axis=1) * nw,)


def f_ml_epi(pids, hm, xc, mz, nw, skip):
    parts = []
    for h in range(ML_HEADS):
        p = hm[:, h * ML_HD:(h + 1) * ML_HD]
        mu = jnp.mean(p, axis=-1, keepdims=True)
        var = jnp.mean(jnp.square(p - mu), axis=-1, keepdims=True)
        parts.append((p - mu) * lax.rsqrt(var + EPS))
    hn = jnp.concatenate(parts, axis=1) * nw
    return ((hn + skip * xc) * _silu(mz),)


def f_s5_post(pids, ys, u, d_skip):
    return (jax.nn.gelu(ys + d_skip * u),)


def f_glu(pids, pab, ba, bb):
    d = ba.shape[1]
    return ((pab[:, :d] + ba) * jax.nn.sigmoid(pab[:, d:] + bb),)


def f_glu_res(pids, pab, xres, ba, bb):
    return (xres + f_glu(pids, pab, ba, bb)[0],)


def rowwise_fwd(name, f, rows, row_cols, pars, out_width, tm=512):
    t = rows[0].shape[0]
    specs = [_row_spec(tm, w, c) for (w, c) in row_cols] + [_par_spec(p.shape) for p in pars]
    return map_fwd(name, f, (t // tm,), list(rows) + list(pars), specs, [jax.ShapeDtypeStruct((t, out_width), f32)],
                   [_row_spec(tm, out_width)])[0]


def rowwise_bwd(name, f, rows, row_cols, pars, dy, tm=256):
    t = rows[0].shape[0]
    n_r, n_p = len(rows), len(pars)
    specs = [_row_spec(tm, w, c) for (w, c) in row_cols] + [_par_spec(p.shape) for p in pars]
    out_w = dy.shape[1]

    def body(*refs):
        vals = [r[...] for r in refs[:n_r + n_p]]
        dy_ref = refs[n_r + n_p]
        outs = refs[n_r + n_p + 1:]
        _, vjp = jax.vjp(lambda *v: f(None, *v)[0], *vals)
        grads = vjp(dy_ref[...])
        for k in range(n_r):
            outs[k][...] = grads[k]

        @pl.when(pl.program_id(0) == 0)
        def _():
            for k in range(n_p):
                outs[n_r + k][...] = jnp.zeros_like(outs[n_r + k])
        for k in range(n_p):
            outs[n_r + k][...] += grads[n_r + k]

    out_shapes = [jax.ShapeDtypeStruct((t, w), f32) for (w, c) in row_cols] + [jax.ShapeDtypeStruct(p.shape, f32) for p in pars]
    out_specs = [_row_spec(tm, w) for (w, c) in row_cols] + [_par_spec(p.shape) for p in pars]
    return pl.pallas_call(body, name=name, grid=(t // tm,), in_specs=specs + [_row_spec(tm, out_w)], out_specs=out_specs,
                          out_shape=out_shapes, compiler_params=_cp(1))(*rows, *pars, dy)


def f_ml(pids, states, q, k, v, g1, g2, g3, b_if):
    h = pids[2]
    cst, nst, mst = states
    l = q.shape[0]
    gt = g1 + g2 + g3 + b_if
    k = k * (1.0 / math.sqrt(ML_HD))
    tri = _tri(l)
    bc_all = hdot(tri.astype(f32), jax.nn.log_sigmoid(gt))
    bcum, ig = _lane_pick(bc_all, ML_HEADS + h), _lane_pick(gt, h)
    bcum_t, ig_t = _row_pick(bc_all.T, ML_HEADS + h), _row_pick(gt.T, h)
    b_last = bcum[l - 1:l, :]
    dlog = jnp.where(tri, bcum - bcum_t + ig_t, NEG)
    ws = b_last - bcum + ig
    m_prev = mst[:, 0:1]
    m_new = lax.stop_gradient(jnp.maximum(b_last + m_prev, jnp.max(ws, axis=0, keepdims=True)))
    decay = jnp.exp(b_last + m_prev - m_new)
    wts = jnp.exp(ws - m_new)
    c_new = decay * cst + bdot(wts * v, k, 0, 0)
    n_new = decay * nst + jnp.sum(wts * k, axis=0, keepdims=True)
    m_inter = bcum + m_prev
    m_t = lax.stop_gradient(jnp.maximum(jnp.max(dlog, axis=1, keepdims=True), m_inter))
    scores = bdot(q, k, 1, 1) * jnp.exp(dlog - m_t)
    inter_w = jnp.exp(m_inter - m_t)
    num = bdot(scores, v, 1, 0) + inter_w * bdot(q, cst, 1, 1)
    den = jnp.sum(scores, axis=1, keepdims=True) + inter_w * jnp.sum(q * nst, axis=1, keepdims=True)
    hout = num / jnp.maximum(jnp.abs(den), jnp.exp(-m_t))
    return (hout,), (c_new, n_new, jnp.broadcast_to(m_new, mst.shape))


def _ml_specs(nch, rev):
    cc = (lambda c: nch - 1 - c) if rev else (lambda c: c)
    hd = pl.BlockSpec((CHUNK, ML_HD), lambda b, c, h: (b * nch + cc(c), h))
    gt = pl.BlockSpec((CHUNK, LANES), lambda b, c, h: (b * nch + cc(c), 0))
    par = pl.BlockSpec((1, LANES), lambda b, c, h: (0, 0))
    sc = pl.BlockSpec((1, 1, ML_HD, ML_HD), lambda b, c, h: (b * nch + cc(c), h, 0, 0))
    sn = pl.BlockSpec((1, 1, 1, ML_HD), lambda b, c, h: (b * nch + cc(c), h, 0, 0))
    sm = pl.BlockSpec((1, 1, 1, LANES), lambda b, c, h: (b * nch + cc(c), h, 0, 0))
    return hd, gt, par, sc, sn, sm


ML_STATE_SHAPES = [(ML_HD, ML_HD), (1, ML_HD), (1, LANES)]


def ml_fwd(q, k, v, g1, g2, g3, b_if, seq):
    t = q.shape[0]
    nch = seq // CHUNK
    hd, gt, par, sc, sn, sm = _ml_specs(nch, False)
    nc = t // CHUNK
    outs = scan_fwd("ml_fwd", f_ml, (t // seq, nch, ML_HEADS), 2, [q, k, v, g1, g2, g3, b_if],
                    [hd, hd, hd, gt, gt, gt, par], [jax.ShapeDtypeStruct((t, ML_HEADS * ML_HD), f32)], [hd],
                    ML_STATE_SHAPES, [0.0, 0.0, NEG],
                    [jax.ShapeDtypeStruct((nc, ML_HEADS, ML_HD, ML_HD), f32), jax.ShapeDtypeStruct((nc, ML_HEADS, 1, ML_HD), f32),
                     jax.ShapeDtypeStruct((nc, ML_HEADS, 1, LANES), f32)], [sc, sn, sm])
    return outs[0], outs[1:]


def ml_bwd(dh, q, k, v, g1, g2, g3, b_if, saves, seq):
    t = q.shape[0]
    nch = seq // CHUNK
    hd, gt, par, sc, sn, sm = _ml_specs(nch, True)

    def f(pids, states, q, k, v, gsum, b_if):
        return f_ml(pids, states, q, k, v, gsum, jnp.zeros_like(gsum), jnp.zeros_like(gsum), b_if)

    def body(q_ref, k_ref, v_ref, g1_ref, g2_ref, g3_ref, b_ref, c_ref, n_ref, m_ref, dh_ref,
             dq_ref, dk_ref, dv_ref, dg_ref, db_ref, dc_s, dn_s):
        pids = tuple(pl.program_id(i) for i in range(3))
        slot = pids[2]

        @pl.when(pids[1] == 0)
        def _():
            dc_s[slot] = jnp.zeros(dc_s.shape[1:], f32)
            dn_s[slot] = jnp.zeros(dn_s.shape[1:], f32)

        gsum = g1_ref[...] + g2_ref[...] + g3_ref[...]
        mst = m_ref[0, 0]

        def gfun(cst, nst, qq, kk, vv, gs, bb):
            outs, new = f(pids, (cst, nst, mst), qq, kk, vv, gs, bb)
            return outs[0], new[0], new[1]

        _, vjp = jax.vjp(gfun, c_ref[0, 0], n_ref[0, 0], q_ref[...], k_ref[...], v_ref[...], gsum, b_ref[...])
        grads = vjp((dh_ref[...], dc_s[slot], dn_s[slot]))
        dc_s[slot] = grads[0]
        dn_s[slot] = grads[1]
        dq_ref[...] = grads[2]
        dk_ref[...] = grads[3]
        dv_ref[...] = grads[4]

        @pl.when(slot == 0)
        def _():
            dg_ref[...] = jnp.zeros_like(dg_ref)
        dg_ref[...] += grads[5]
        first = jnp.logical_and(jnp.logical_and(pids[0] == 0, pids[1] == 0), slot == 0)

        @pl.when(first)
        def _():
            db_ref[...] = jnp.zeros_like(db_ref)
        db_ref[...] += grads[6]

    big = jax.ShapeDtypeStruct((t, ML_HEADS * ML_HD), f32)
    return pl.pallas_call(body, name="ml_bwd", grid=(t // seq, nch, ML_HEADS),
                          in_specs=[hd, hd, hd, gt, gt, gt, par, sc, sn, sm, hd], out_specs=[hd, hd, hd, gt, par],
                          out_shape=[big, big, big, jax.ShapeDtypeStruct((t, LANES), f32), jax.ShapeDtypeStruct((1, LANES), f32)],
                          scratch_shapes=[pltpu.VMEM((ML_HEADS, ML_HD, ML_HD), f32), pltpu.VMEM((ML_HEADS, 1, ML_HD), f32)],
                          compiler_params=_cp(3))(q, k, v, g1, g2, g3, b_if, *saves, dh)


def _block_prefix(z, transpose):
    n = z.shape[0]
    r, c = _iota((n, n), 0), _iota((n, n), 1)
    keep = jnp.logical_and(r // S5_SUB == c // S5_SUB, (c >= r) if transpose else (c <= r))
    m = jnp.where(keep, 1.0, 0.0).astype(bf16)
    hi = z.astype(bf16)
    lo = (z - hi.astype(f32)).astype(bf16)
    return jnp.dot(m, hi, preferred_element_type=f32) + jnp.dot(m, lo, preferred_element_type=f32)


@jax.custom_vjp
def block_prefix(z):
    return _block_prefix(z, False)


block_prefix.defvjp(lambda z: (_block_prefix(z, False), None), lambda _, ct: (_block_prefix(ct, True),))


def _cmul(a, b):
    h = b.shape[1] // 2
    ar, ai, br, bi = a[:, :h], a[:, h:], b[:, :h], b[:, h:]
    return jnp.concatenate([ar * br - ai * bi, ar * bi + ai * br], axis=1)


def f_s5(pids, states, u, bb, cc, tab):
    (carry,) = states
    tl = u.shape[0]
    nsub = tl // S5_SUB
    rep = lambda t: jnp.concatenate([t] * nsub, axis=0)
    p1, p0, q0 = tab[0:S5_SUB], tab[S5_SUB:2 * S5_SUB], tab[2 * S5_SUB:3 * S5_SUB]
    lam_sub = tab[S5_SUB - 1:S5_SUB]
    bu = bdot(u, bb, 1, 0)
    xl = _cmul(rep(p0), block_prefix(_cmul(rep(q0), bu)))
    e, entering = carry, []
    for k in range(nsub):
        entering.append(jnp.broadcast_to(e, (S5_SUB, e.shape[1])))
        e = xl[(k + 1) * S5_SUB - 1:(k + 1) * S5_SUB] + _cmul(lam_sub, e)
    x = xl + _cmul(rep(p1), jnp.concatenate(entering, axis=0))
    y = bdot(x, cc, 1, 0)
    return (y,), (e,)


def _s5_specs(ntl, rev):
    tt = (lambda t: ntl - 1 - t) if rev else (lambda t: t)
    us = pl.BlockSpec((S5_TL, LANES), lambda c, b, t: (b * ntl + tt(t), c))
    bbs = pl.BlockSpec((1, LANES, 2 * S5_CH), lambda c, b, t: (c, 0, 0))
    ccs = pl.BlockSpec((1, 2 * S5_CH, LANES), lambda c, b, t: (c, 0, 0))
    pws = pl.BlockSpec((1, 3 * S5_SUB, 2 * S5_CH), lambda c, b, t: (c, 0, 0))
    sv = pl.BlockSpec((1, 1, 1, 2 * S5_CH), lambda c, b, t: (b * ntl + tt(t), c, 0, 0))
    return us, bbs, ccs, pws, sv


def s5_fwd(u, bb, cc, pw, seq):
    t = u.shape[0]
    ntl = seq // S5_TL
    us, bbs, ccs, pws, sv = _s5_specs(ntl, False)

    def f(pids, states, uu, b3, c3, p3):
        return f_s5(pids, states, uu, b3[0], c3[0], p3[0])

    y, carries = scan_fwd("s5_fwd", f, (S5_CB, t // seq, ntl), 0, [u, bb, cc, pw], [us, bbs, ccs, pws],
                          [jax.ShapeDtypeStruct((t, S5_CB * LANES), f32)], [us], [(1, 2 * S5_CH)], [0.0],
                          [jax.ShapeDtypeStruct((t // S5_TL, S5_CB, 1, 2 * S5_CH), f32)], [sv])
    return y, carries


def s5_bwd(dy, u, bb, cc, pw, carries, seq):
    t = u.shape[0]
    ntl = seq // S5_TL
    us, bbs, ccs, pws, sv = _s5_specs(ntl, True)

    def f(pids, states, uu, b3, c3, p3):
        return f_s5(pids, states, uu, b3[0], c3[0], p3[0])

    first = lambda pids: jnp.logical_and(pids[1] == 0, pids[2] == 0)
    return scan_bwd("s5_bwd", f, (S5_CB, t // seq, ntl), 0, [u, bb, cc, pw], [us, bbs, ccs, pws], [carries], [sv],
                    [dy], [us], [(1, 2 * S5_CH)], [0, 1, 2, 3], {1: first, 2: first, 3: first})


def _adam_math(g, w, m, v):
    m2 = ADAM_B1 * m + (1.0 - ADAM_B1) * g
    v2 = ADAM_B2 * v + (1.0 - ADAM_B2) * jnp.square(g)
    m_hat = m2 / (1.0 - ADAM_B1 ** ADAM_STEP)
    v_hat = v2 / (1.0 - ADAM_B2 ** ADAM_STEP)
    delta = -ADAM_LR * (m_hat / (jnp.sqrt(v_hat) + ADAM_EPS) + ADAM_WD * w)
    return delta, m2, v2


def adamw(name, parts, w, m, v, tr=256):
    n, r, c = parts.shape
    tr = min(tr, r)
    assert r % tr == 0

    def body(p_ref, w_ref, m_ref, v_ref, g_ref, d_ref, m2_ref, v2_ref):
        g = p_ref[0].astype(f32)
        for s in range(1, n):
            g = g + p_ref[s].astype(f32)
        d, m2, v2 = _adam_math(g, w_ref[...], m_ref[...], v_ref[...])
        g_ref[...] = g
        d_ref[...] = d
        m2_ref[...] = m2
        v2_ref[...] = v2

    ps = pl.BlockSpec((n, tr, c), lambda i: (0, i, 0))
    rs = pl.BlockSpec((tr, c), lambda i: (i, 0))
    return pl.pallas_call(body, name=name, grid=(r // tr,), in_specs=[ps, rs, rs, rs], out_specs=[rs] * 4,
                          out_shape=[jax.ShapeDtypeStruct((r, c), f32)] * 4, compiler_params=_cp(1))(parts, w, m, v)


def adamw_layer(name, parts, w, m, v, layer, prev=None, tr=256):
    n, r, c = parts.shape
    nl = w.shape[0]
    tr = min(tr, r)
    assert r % tr == 0 and w.shape[1:] == (r, c)
    n_prev = 0 if prev is None else 4

    def body(*refs):
        p_ref, w_ref, m_ref, v_ref = refs[:4]
        g_ref, d_ref, m2_ref, v2_ref = refs[4 + n_prev:]
        g = p_ref[0].astype(f32)
        for s in range(1, n):
            g = g + p_ref[s].astype(f32)
        d, m2, v2 = _adam_math(g, w_ref[0], m_ref[0], v_ref[0])
        g_ref[0] = g
        d_ref[0] = d
        m2_ref[0] = m2
        v2_ref[0] = v2

    ps = pl.BlockSpec((n, tr, c), lambda i: (0, i, 0))
    rs = pl.BlockSpec((1, tr, c), lambda i: (layer, i, 0))
    anyspec = pl.BlockSpec(memory_space=pl.ANY)
    return pl.pallas_call(body, name=name, grid=(r // tr,), in_specs=[ps, rs, rs, rs] + [anyspec] * n_prev, out_specs=[rs] * 4,
                          out_shape=[jax.ShapeDtypeStruct((nl, r, c), f32)] * 4,
                          input_output_aliases={4 + i: i for i in range(n_prev)},
                          compiler_params=_cp(1))(parts, w, m, v, *(prev or ()))


def sum_parts(name, parts, tr=256):
    n, r, c = parts.shape
    tr = min(tr, r)
    assert r % tr == 0

    def body(p_ref, o_ref):
        g = p_ref[0].astype(f32)
        for s in range(1, n):
            g = g + p_ref[s].astype(f32)
        o_ref[...] = g

    return pl.pallas_call(body, name=name, grid=(r // tr,), in_specs=[pl.BlockSpec((n, tr, c), lambda i: (0, i, 0))],
                          out_specs=pl.BlockSpec((tr, c), lambda i: (i, 0)),
                          out_shape=jax.ShapeDtypeStruct((r, c), f32), compiler_params=_cp(1))(parts)


class Rider:
    def __init__(self, ops):
        self.ops = list(ops or [])
        self.n = len(self.ops)

    def arrays(self):
        return [a for a, _ in self.ops]

    def specs(self):
        return [pl.BlockSpec(memory_space=pl.ANY)] * self.n

    def out_shapes(self):
        return [jax.ShapeDtypeStruct((N_DEV,) + tuple(a.shape) if mode == "gather" else tuple(a.shape), a.dtype)
                for a, mode in self.ops]

    def scratch(self):
        if not self.n:
            return []
        return [pltpu.SemaphoreType.DMA((self.n, N_DEV - 1)), pltpu.SemaphoreType.DMA((self.n, N_DEV - 1)),
                pltpu.SemaphoreType.DMA((self.n,))]

    def _copies(self, ins, outs, sems):
        send_sems, recv_sems, loc_sems = sems
        x, y, c = lax.axis_index("x"), lax.axis_index("y"), lax.axis_index("c")
        me = 4 * x + 2 * y + c
        copies = []
        for k, (_, mode) in enumerate(self.ops):
            src_me = ins[k] if mode == "gather" else ins[k].at[me]
            copies.append(pltpu.make_async_copy(src_me, outs[k].at[me], loc_sems.at[k]))
            for d in range(1, N_DEV):
                px = 1 - x if (d >> 2) & 1 else x
                py = 1 - y if (d >> 1) & 1 else y
                pc = 1 - c if d & 1 else c
                src = ins[k] if mode == "gather" else ins[k].at[4 * px + 2 * py + pc]
                copies.append(pltpu.make_async_remote_copy(
                    src_ref=src, dst_ref=outs[k].at[me], send_sem=send_sems.at[k, d - 1], recv_sem=recv_sems.at[k, d - 1],
                    device_id=(px, py, pc), device_id_type=pl.DeviceIdType.MESH))
        return copies

    def start(self, grid, ins, outs, sems):
        if self.n:
            @pl.when(functools.reduce(jnp.logical_and, [pl.program_id(i) == 0 for i in range(len(grid))]))
            def _():
                for cp in self._copies(ins, outs, sems):
                    cp.start()

    def wait(self, grid, ins, outs, sems):
        if self.n:
            @pl.when(functools.reduce(jnp.logical_and, [pl.program_id(i) == g - 1 for i, g in enumerate(grid)]))
            def _():
                for cp in self._copies(ins, outs, sems):
                    cp.wait()


def exchange(name, ops):
    rider = Rider(ops)
    n = rider.n

    def body(*refs):
        copies = rider._copies(refs[:n], refs[n:2 * n], refs[2 * n:])
        for cp in copies:
            cp.start()
        for cp in copies:
            cp.wait()

    return pl.pallas_call(body, name=name, in_specs=rider.specs(), out_specs=rider.specs(), out_shape=rider.out_shapes(),
                          scratch_shapes=rider.scratch())(*rider.arrays())


def _lanes(v, width=LANES):
    v = v.reshape(1, -1)
    return jnp.pad(v, ((0, 0), (0, width - v.shape[1])))


def win_to_padded(w):
    return jnp.concatenate([w[:, :1024], w[:, 2576:3600], w[:, 3600:4624], w[:, 1024:2560], w[:, 2560:2576],
                            jnp.zeros((w.shape[0], PROJ_W - IN_COLS), w.dtype)], axis=1)


def win_from_padded(wp):
    return jnp.concatenate([wp[:, 0:1024], wp[:, 3072:4608], wp[:, 4608:4624], wp[:, 1024:2048], wp[:, 2048:3072]], axis=1)


def headwise_dense(w):
    nb, o, i = w.shape
    rows = jnp.tile(w.transpose(0, 2, 1).reshape(nb * i, o), (1, nb))
    same = (jnp.arange(nb * i)[:, None] // i) == (jnp.arange(nb * o)[None, :] // o)
    return jnp.where(same, rows, 0.0)


def diag_blocks(name, dd, blk, tm=256):
    n = dd.shape[0]

    def body(d_ref, o_ref):
        rows = _iota((tm, n), 0) + pl.program_id(0) * tm
        masked = jnp.where(rows // blk == _iota((tm, n), 1) // blk, d_ref[...], 0.0)
        sel = (_iota((n, LANES), 0) % blk == _iota((n, LANES), 1)).astype(f32)
        o_ref[...] = hdot(masked, sel)

    return pl.pallas_call(body, name=name, grid=(n // tm,), in_specs=[pl.BlockSpec((tm, n), lambda i: (i, 0))],
                          out_specs=pl.BlockSpec((tm, LANES), lambda i: (i, 0)),
                          out_shape=jax.ShapeDtypeStruct((n, LANES), f32), compiler_params=_cp(1))(dd)


def headwise_from_dense(name, dd, o=4, i=4):
    nb = dd.shape[0] // i
    return diag_blocks(name, dd, i)[:, :o].reshape(nb, i, o).transpose(0, 2, 1)


def s5_tables(a_re, a_im, log_step, b_re, b_im, c_re, c_im):
    step = jnp.exp(log_step)[:, None]
    j = jnp.arange(S5_SUB, dtype=f32)[:, None, None]
    expo = jnp.concatenate([j + 1.0, j, -j], axis=0)
    mag = jnp.exp(expo * (a_re * step))
    pw_re, pw_im = mag * jnp.cos(expo * (a_im * step)), mag * jnp.sin(expo * (a_im * step))
    lam_re, lam_im = pw_re[0], pw_im[0]
    den = a_re * a_re + a_im * a_im
    coef_re = ((lam_re - 1.0) * a_re + lam_im * a_im) / den
    coef_im = (lam_im * a_re - (lam_re - 1.0) * a_im) / den
    bb_re = coef_re[..., None] * b_re - coef_im[..., None] * b_im
    bb_im = coef_re[..., None] * b_im + coef_im[..., None] * b_re
    gl = S5_GROUPS // S5_CB
    eye = jnp.eye(gl, dtype=f32)

    def blk_b(t):
        t4 = t.transpose(0, 2, 1).reshape(S5_CB, gl, S5_GROUP, S5_STATE)
        return jnp.einsum("kgcn,gh->kgchn", t4, eye).reshape(S5_CB, gl * S5_GROUP, gl * S5_STATE)

    def blk_c(t):
        t4 = t.reshape(S5_CB, gl, S5_GROUP, S5_STATE)
        return jnp.einsum("kgcn,gh->kgnhc", t4, eye).reshape(S5_CB, gl * S5_STATE, gl * S5_GROUP)

    def blk_p(t):
        return t.reshape(t.shape[0], S5_CB, gl * S5_STATE).transpose(1, 0, 2)

    bb = jnp.concatenate([blk_b(bb_re), blk_b(bb_im)], axis=2)
    cc = jnp.concatenate([blk_c(c_re), -blk_c(c_im)], axis=1)
    pw = jnp.concatenate([blk_p(pw_re), blk_p(pw_im)], axis=2)
    return bb, cc, pw


def ffn_step_bwd(dy, x, nw, wts, saved, ride_act=None, ride_w=None):
    h, g, u = saved
    dx, dnw, dg, du, a, dyh, *got_act = ffn_bwd_act(dy, x, nw, g, u, *wts, ride=ride_act)
    dwg, dwu, dwd, *got_w = ffn_bwd_w(h, dyh, dg, du, a, ride=ride_w)
    return dx, dnw, (dwg, dwu, dwd), got_act, got_w


def hybrid_fwd(x1, p, seq, ride_in=None):
    u = rms_fwd(x1, p["mix_norm"])
    proj, *got_in = matmul("hy_in", u, p["win"], ride=ride_in) if ride_in else (matmul("hy_in", u, p["win"]),)
    xbc = conv_fwd("ssd_conv", proj, OFF_XBC, p["ssd_conv_w"], p["ssd_conv_b"], seq)
    yraw, hsave = ssd_fwd(xbc, proj, p["a_log"], p["dt_bias"], p["ssd_d"], seq)
    yssd = rowwise_fwd("ssd_epi", f_ssd_epi, [yraw, proj], [(D_MODEL, 0), (D_MODEL, OFF_Z // D_MODEL)], [p["ssd_norm_w"]], D_MODEL)
    xc = conv_fwd("ml_conv", proj, OFF_MX, p["ml_conv_w"], p["ml_conv_b"], seq)
    q = matmul("hw_q", xc, p["wq"])
    k = matmul("hw_k", xc, p["wk"])
    v = matmul("hw_v", proj, p["wv"], a_off=OFF_MX, a_width=D_MODEL)
    g1 = matmul("gate_q", q, p["wif_q"])
    g2 = matmul("gate_k", k, p["wif_k"])
    g3 = matmul("gate_v", v, p["wif_v"])
    hm, mlsave = ml_fwd(q, k, v, g1, g2, g3, p["b_if"], seq)
    yml = rowwise_fwd("ml_epi", f_ml_epi, [hm, xc, proj], [(D_MODEL, 0), (D_MODEL, 0), (D_MODEL, OFF_MZ // D_MODEL)],
                      [p["ml_norm_w"], p["ml_skip"]], D_MODEL)
    t = matmul("hy_out1", yssd, p["wo1"], add=x1)
    x2 = matmul("hy_out2", yml, p["wo2"], add=t)
    return x2, (u, proj, xbc, yraw, hsave, yssd, xc, q, k, v, g1, g2, g3, hm, mlsave, yml), got_in


def hybrid_bwd(dx2, x1, p, saved, seq, ride_dwin=None):
    u, proj, xbc, yraw, hsave, yssd, xc, q, k, v, g1, g2, g3, hm, mlsave, yml = saved
    gr = {}
    dyssd = matmul("d_yssd", dx2, p["wo1"], cb=1)
    dyml = matmul("d_yml", dx2, p["wo2"], cb=1)
    gr["wo"] = jnp.concatenate([matmul("dw_o1", yssd, dx2, ca=0), matmul("dw_o2", yml, dx2, ca=0)], axis=0)
    d_hm, d_xc, d_mz, gr["ml_norm_w"], gr["ml_skip"] = rowwise_bwd(
        "ml_epi_bwd", f_ml_epi, [hm, xc, proj], [(D_MODEL, 0), (D_MODEL, 0), (D_MODEL, OFF_MZ // D_MODEL)],
        [p["ml_norm_w"], p["ml_skip"]], dyml)
    dq, dk, dv, dgt, gr["b_if"] = ml_bwd(d_hm, q, k, v, g1, g2, g3, p["b_if"], mlsave, seq)
    dq = matmul("dq_gate", dgt, p["wif_q"], cb=1, add=dq)
    dk = matmul("dk_gate", dgt, p["wif_k"], cb=1, add=dk)
    dv = matmul("dv_gate", dgt, p["wif_v"], cb=1, add=dv)
    gr["wif"] = jnp.concatenate([matmul("dw_if_q", q, dgt, ca=0), matmul("dw_if_k", k, dgt, ca=0),
                                 matmul("dw_if_v", v, dgt, ca=0)], axis=0)
    d_xc = matmul("dxc_q", dq, p["wq"], cb=1, add=d_xc)
    d_xc = matmul("dxc_k", dk, p["wk"], cb=1, add=d_xc)
    gr["wq"] = matmul("dw_q", xc, dq, ca=0)
    gr["wk"] = matmul("dw_k", xc, dk, ca=0)
    gr["wv"] = matmul("dw_v", proj, dv, ca=0, a_off=OFF_MX, a_width=D_MODEL)
    d_mx, gr["ml_conv_w"], gr["ml_conv_b"] = conv_bwd("ml_conv_bwd", d_xc, proj, OFF_MX, p["ml_conv_w"], p["ml_conv_b"], seq)
    d_mx = matmul("dmx_v", dv, p["wv"], cb=1, add=d_mx)
    d_yraw, d_z, gr["ssd_norm_w"] = rowwise_bwd("ssd_epi_bwd", f_ssd_epi, [yraw, proj],
                                                [(D_MODEL, 0), (D_MODEL, OFF_Z // D_MODEL)], [p["ssd_norm_w"]], dyssd)
    d_xs, d_b, d_c, d_dt, gr["a_log"], gr["dt_bias"], gr["ssd_d"] = ssd_bwd(
        d_yraw, xbc, proj, p["a_log"], p["dt_bias"], p["ssd_d"], hsave, seq)
    d_xbc, gr["ssd_conv_w"], gr["ssd_conv_b"] = conv_bwd("ssd_conv_bwd", jnp.concatenate([d_xs, d_b, d_c], axis=1), proj, OFF_XBC,
                                                         p["ssd_conv_w"], p["ssd_conv_b"], seq)
    dproj = jnp.concatenate([d_z, d_mx, d_mz, d_xbc, d_dt, jnp.zeros((d_dt.shape[0], PROJ_W - OFF_DT - LANES), f32)], axis=1)
    dwin, *got_dwin = matmul("dw_in", u, dproj, ca=0, ride=ride_dwin) if ride_dwin else (matmul("dw_in", u, dproj, ca=0),)
    out_ops = [(_shards(win_from_padded(dwin)[None], 2).astype(bf16), "scatter"), (_shards(gr.pop("wo")[None], 1).astype(bf16), "scatter")]
    du, part_win, part_wo = matmul("d_u", dproj, p["win"], cb=1, ride=out_ops)
    dx1, gr["mix_norm"] = rms_bwd([du], x1, p["mix_norm"], dx2)
    return dx1, gr, got_dwin, (part_win, part_wo)


def s5_layer_fwd(x4, p, seq):
    u = rms_fwd(x4, p["mix_norm"])
    ys, carries = s5_fwd(u, p["bb"], p["cc"], p["pw"], seq)
    gg = rowwise_fwd("s5_post", f_s5_post, [ys, u], [(D_MODEL, 0), (D_MODEL, 0)], [p["s5_d"]], D_MODEL)
    pab = matmul("s5_ab", gg, p["wab"])
    x5 = rowwise_fwd("s5_glu", f_glu_res, [pab, x4], [(2 * D_MODEL, 0), (D_MODEL, 0)], [p["b_a"], p["b_b"]], D_MODEL)
    return x5, (u, ys, carries, gg, pab)


def s5_layer_bwd(dx5, x4, p, saved, seq):
    u, ys, carries, gg, pab = saved
    gr = {}
    dpab, gr["b_a"], gr["b_b"] = rowwise_bwd("s5_glu_bwd", f_glu, [pab], [(2 * D_MODEL, 0)], [p["b_a"], p["b_b"]], dx5)
    dgg = matmul("d_gg", dpab, p["wab"], cb=1)
    gr["wab"] = matmul("dw_ab", gg, dpab, ca=0)
    dys, du_a, gr["s5_d"] = rowwise_bwd("s5_post_bwd", f_s5_post, [ys, u], [(D_MODEL, 0), (D_MODEL, 0)], [p["s5_d"]], dgg)
    du_b, gr["bb"], gr["cc"], gr["pw"] = s5_bwd(dys, u, p["bb"], p["cc"], p["pw"], carries, seq)
    dx4, gr["mix_norm"] = rms_bwd([du_a, du_b], x4, p["mix_norm"], dx5)
    return dx4, gr


BIG = ["ffn1_w_gate", "ffn1_w_up", "ffn1_w_down", "ffn2_w_gate", "ffn2_w_up", "ffn2_w_down", "hy_w_in", "hy_w_out", "s5_w_a", "s5_w_b"]
SMALL_SHARDED = {"ssd_conv_w": 2, "ml_conv_w": 2, "ml_w_q": 1, "ml_w_k": 1, "ml_w_v": 1, "ml_w_if": 1, "s5_d": 1, "s5_b_a": 1, "s5_b_b": 1}
WEIGHTS = ["ffn1_norm", "ffn1_w_gate", "ffn1_w_up", "ffn1_w_down", "mix_norm", "ffn2_norm", "ffn2_w_gate", "ffn2_w_up", "ffn2_w_down",
           "hy_w_in", "ssd_conv_w", "ssd_conv_b", "ssd_dt_bias", "ssd_a_log", "ssd_d", "ssd_norm_w", "ml_conv_w", "ml_conv_b",
           "ml_w_q", "ml_w_k", "ml_w_v", "ml_w_if", "ml_b_if", "ml_norm_w", "ml_skip", "hy_w_out", "s5_a_re", "s5_a_im",
           "s5_log_step", "s5_b_re", "s5_b_im", "s5_c_re", "s5_c_im", "s5_d", "s5_w_a", "s5_b_a", "s5_w_b", "s5_b_b", "final_norm"]
S5_PARAMS = ["s5_a_re", "s5_a_im", "s5_log_step", "s5_b_re", "s5_b_im", "s5_c_re", "s5_c_im"]
SMALL_S5 = S5_PARAMS + ["s5_d", "s5_b_a", "s5_b_b"]
SMALL_REST = [n for n in WEIGHTS if n not in BIG and n not in SMALL_S5]
SMALL = SMALL_REST + SMALL_S5


def _unshard(g, axis):
    return jnp.concatenate([g[i] for i in range(N_DEV)], axis=axis)


def assemble(gw, rep):
    padn = lambda w: jnp.pad(w, ((0, 0), (0, LANES - w.shape[1]))).astype(bf16)
    wif = _unshard(gw["ml_w_if"], 1)[0]
    wo = _unshard(gw["hy_w_out"], 1)[0].astype(bf16)
    dense = lambda n: headwise_dense(_unshard(gw[n], 1)[0].astype(f32)).astype(bf16)
    w0 = dict(mix_norm=rep["mix_norm"][0:1],
              win=win_to_padded(_unshard(gw["hy_w_in"], 2)[0]).astype(bf16),
              ssd_conv_w=_unshard(gw["ssd_conv_w"], 2)[0], ssd_conv_b=rep["ssd_conv_b"],
              a_log=_lanes(rep["ssd_a_log"]), dt_bias=_lanes(rep["ssd_dt_bias"]), ssd_d=_lanes(rep["ssd_d"]),
              ssd_norm_w=rep["ssd_norm_w"], ml_conv_w=_unshard(gw["ml_conv_w"], 2)[0], ml_conv_b=rep["ml_conv_b"],
              wq=dense("ml_w_q"), wk=dense("ml_w_k"), wv=dense("ml_w_v"),
              wif_q=padn(wif[0:1024]), wif_k=padn(wif[1024:2048]), wif_v=padn(wif[2048:3072]),
              b_if=_lanes(rep["ml_b_if"]), ml_norm_w=rep["ml_norm_w"], ml_skip=rep["ml_skip"],
              wo1=wo[:D_MODEL], wo2=wo[D_MODEL:])
    bb, cc, pw = s5_tables(*[rep[n][0] for n in S5_PARAMS])
    wab = jnp.concatenate([_unshard(gw["s5_w_a"], 1)[0], _unshard(gw["s5_w_b"], 1)[0]], axis=1).astype(bf16)
    w1 = dict(mix_norm=rep["mix_norm"][1:2], bb=bb, cc=cc, pw=pw,
              s5_d=_unshard(gw["s5_d"], 1), wab=wab, b_a=_unshard(gw["s5_b_a"], 1), b_b=_unshard(gw["s5_b_b"], 1))
    return w0, w1


def _shards(full, axis):
    return jnp.stack(jnp.split(full, N_DEV, axis=axis), axis=0)


def small_grads(g_norms, g_hy, g_s5, d_final, rep):
    small = dict(g_norms)
    small["mix_norm"] = jnp.concatenate([g_hy["mix_norm"], g_s5["mix_norm"]], axis=0)
    small["ssd_conv_w"] = g_hy["ssd_conv_w"][None]
    small["ssd_conv_b"] = g_hy["ssd_conv_b"]
    small["ssd_dt_bias"] = g_hy["dt_bias"][:, :SSD_HEADS]
    small["ssd_a_log"] = g_hy["a_log"][:, :SSD_HEADS]
    small["ssd_d"] = g_hy["ssd_d"][:, :SSD_HEADS]
    small["ssd_norm_w"] = g_hy["ssd_norm_w"]
    small["ml_conv_w"] = g_hy["ml_conv_w"][None]
    small["ml_conv_b"] = g_hy["ml_conv_b"]
    for nm, key in (("ml_w_q", "wq"), ("ml_w_k", "wk"), ("ml_w_v", "wv")):
        small[nm] = headwise_from_dense("diag_" + key, g_hy[key])[None]
    small["ml_w_if"] = g_hy["wif"][None, :, :2 * ML_HEADS]
    small["ml_b_if"] = g_hy["b_if"][:, :2 * ML_HEADS]
    small["ml_norm_w"] = g_hy["ml_norm_w"]
    small["ml_skip"] = g_hy["ml_skip"]
    small["final_norm"] = d_final.reshape(-1)
    return small


def s5_small_grads(g_s5, rep):
    small = {}
    _, tvjp = jax.vjp(s5_tables, *[rep[n][0] for n in S5_PARAMS])
    for n, g in zip(S5_PARAMS, tvjp((g_s5["bb"], g_s5["cc"], g_s5["pw"]))):
        small[n] = g[None]
    small["s5_d"] = g_s5["s5_d"]
    small["s5_b_a"] = g_s5["b_a"]
    small["s5_b_b"] = g_s5["b_b"]
    return small


ROW = 1024
F32_ROWS = 8


def _piece_rows(size):
    return -(-size // (ROW * F32_ROWS)) * F32_ROWS


def _pack(arrays):
    pieces = []
    for a in arrays:
        flat = a.astype(f32).reshape(-1)
        pieces.append(jnp.pad(flat, (0, _piece_rows(a.size) * ROW - a.size)).reshape(-1, ROW))
    return jnp.concatenate(pieces, axis=0)


def _unpack(buf, shapes):
    out, r0 = [], 0
    lead = buf.shape[:-2]
    for shp in shapes:
        size = math.prod(shp)
        r = _piece_rows(size)
        out.append(buf[..., r0:r0 + r, :].reshape(lead + (-1,))[..., :size].reshape(lead + tuple(shp)))
        r0 += r
    return out


def _tile_rows(r):
    for t in (512, 256, 128, 64, 32, 16, 8):
        if r % t == 0:
            return t
    return r


def _flat2d(a):
    return a.reshape(-1, a.shape[-1])


def kernel(x, ffn1_norm, ffn1_w_gate, ffn1_w_up, ffn1_w_down, mix_norm, ffn2_norm, ffn2_w_gate, ffn2_w_up, ffn2_w_down, hy_w_in, ssd_conv_w, ssd_conv_b, ssd_dt_bias, ssd_a_log, ssd_d, ssd_norm_w, ml_conv_w, ml_conv_b, ml_w_q, ml_w_k, ml_w_v, ml_w_if, ml_b_if, ml_norm_w, ml_skip, hy_w_out, s5_a_re, s5_a_im, s5_log_step, s5_b_re, s5_b_im, s5_c_re, s5_c_im, s5_d, s5_w_a, s5_b_a, s5_w_b, s5_b_b, final_norm, loss_target, m_ffn1_norm, m_ffn1_w_gate, m_ffn1_w_up, m_ffn1_w_down, m_mix_norm, m_ffn2_norm, m_ffn2_w_gate, m_ffn2_w_up, m_ffn2_w_down, m_hy_w_in, m_ssd_conv_w, m_ssd_conv_b, m_ssd_dt_bias, m_ssd_a_log, m_ssd_d, m_ssd_norm_w, m_ml_conv_w, m_ml_conv_b, m_ml_w_q, m_ml_w_k, m_ml_w_v, m_ml_w_if, m_ml_b_if, m_ml_norm_w, m_ml_skip, m_hy_w_out, m_s5_a_re, m_s5_a_im, m_s5_log_step, m_s5_b_re, m_s5_b_im, m_s5_c_re, m_s5_c_im, m_s5_d, m_s5_w_a, m_s5_b_a, m_s5_w_b, m_s5_b_b, m_final_norm, v_ffn1_norm, v_ffn1_w_gate, v_ffn1_w_up, v_ffn1_w_down, v_mix_norm, v_ffn2_norm, v_ffn2_w_gate, v_ffn2_w_up, v_ffn2_w_down, v_hy_w_in, v_ssd_conv_w, v_ssd_conv_b, v_ssd_dt_bias, v_ssd_a_log, v_ssd_d, v_ssd_norm_w, v_ml_conv_w, v_ml_conv_b, v_ml_w_q, v_ml_w_k, v_ml_w_v, v_ml_w_if, v_ml_b_if, v_ml_norm_w, v_ml_skip, v_hy_w_out, v_s5_a_re, v_s5_a_im, v_s5_log_step, v_s5_b_re, v_s5_b_im, v_s5_c_re, v_s5_c_im, v_s5_d, v_s5_w_a, v_s5_b_a, v_s5_w_b, v_s5_b_b, v_final_norm):
    given = dict(locals())
    w = {n: given[n] for n in WEIGHTS}
    mom = {n: given["m_" + n] for n in WEIGHTS}
    var = {n: given["v_" + n] for n in WEIGHTS}
    bl, seq, d = x.shape
    me = 4 * lax.axis_index("x") + 2 * lax.axis_index("y") + lax.axis_index("c")

    x0, tgt = x.reshape(bl * seq, d), loss_target.reshape(bl * seq, d)
    rep = {n: w[n] for n in WEIGHTS if n not in BIG and n not in SMALL_SHARDED}
    ffn_w = ("_w_gate", "_w_up", "_w_down")

    def ffn_gather(pre, l):
        return [(w[pre + s][l:l + 1].astype(bf16), "gather") for s in ffn_w]

    def scatter(parts):
        return [(p, "scatter") for p in parts]

    wf10 = tuple(exchange("gather_ffn1_l0", ffn_gather("ffn1", 0)))
    mixer_ops = [(w[n].astype(bf16), "gather") for n in ("hy_w_in", "hy_w_out", "s5_w_a", "s5_w_b")]
    mixer_ops.append((_pack([w[n] for n in SMALL_SHARDED]), "gather"))
    x1, *rest = ffn_fwd(x0, ffn1_norm[0:1], *wf10, ride=mixer_ops)
    sv10, got = rest[:3], rest[3:]
    gw = dict(zip(("hy_w_in", "hy_w_out", "s5_w_a", "s5_w_b"), got[:4]))
    gw.update(zip(SMALL_SHARDED, _unpack(got[4], [w[n].shape for n in SMALL_SHARDED])))
    w0, w1 = assemble(gw, rep)
    x2, sv_h, got = hybrid_fwd(x1, w0, seq, ride_in=ffn_gather("ffn2", 0) + ffn_gather("ffn1", 1))
    wf20, wf11 = tuple(got[:3]), tuple(got[3:])
    x3, *rest = ffn_fwd(x2, ffn2_norm[0:1], *wf20, ride=ffn_gather("ffn2", 1))
    sv20, wf21 = rest[:3], tuple(rest[3:])
    x4, *sv11 = ffn_fwd(x3, ffn1_norm[1:2], *wf11)
    x5, sv_s = s5_layer_fwd(x4, w1, seq)
    x6, *sv21 = ffn_fwd(x5, ffn2_norm[1:2], *wf21)
    loss, dx6, d_final = loss_head(x6, final_norm.reshape(1, d), tgt)

    dx5, dn21, dw21, _, _ = ffn_step_bwd(dx6, x5, ffn2_norm[1:2], wf21, sv21)
    dx4, g_s5 = s5_layer_bwd(dx5, x4, w1, sv_s, seq)
    dwab = g_s5.pop("wab")
    s5_ops = scatter([_shards(dwab[None, :, :D_MODEL], 1).astype(bf16), _shards(dwab[None, :, D_MODEL:], 1).astype(bf16)])
    dx3, dn11, dw11, p21, p_s5 = ffn_step_bwd(dx4, x3, ffn1_norm[1:2], wf11, sv11, ride_act=scatter(dw21), ride_w=s5_ops)
    small = s5_small_grads(g_s5, rep)
    dx2, dn20, dw20, p11, (parts_s5,) = ffn_step_bwd(dx3, x2, ffn2_norm[0:1], wf20, sv20, ride_act=scatter(dw11),
                                                      ride_w=[(_pack([small[n] for n in SMALL_S5]), "gather")])
    dx1, g_hy, p20, p_hy = hybrid_bwd(dx2, x1, w0, sv_h, seq, ride_dwin=scatter(dw20))
    dx0, dn10, dw10, _, _ = ffn_step_bwd(dx1, x0, ffn1_norm[0:1], wf10, sv10)
    g_norms = {"ffn1_norm": jnp.concatenate([dn10, dn11], axis=0), "ffn2_norm": jnp.concatenate([dn20, dn21], axis=0)}
    small.update(small_grads(g_norms, g_hy, g_s5, d_final, rep))
    *p10, parts_rest = exchange("reduce_tail", scatter(dw10) + [(_pack([small[n] for n in SMALL_REST]), "gather")])
    small_parts = jnp.concatenate([parts_rest, parts_s5], axis=1)
    small_sum = sum_parts("sum_small", small_parts, tr=_tile_rows(small_parts.shape[1]))

    out_g, out_d, out_m, out_v = {}, {}, {}, {}
    ffn_parts = {"ffn1": (p10, p11), "ffn2": (p20, p21)}
    for pre in ("ffn1", "ffn2"):
        for k, s in enumerate(ffn_w):
            n = pre + s
            r, c = w[n].shape[1:]
            res = None
            for l in (1, 0):
                res = adamw_layer("adamw_" + n, ffn_parts[pre][l][k].reshape(N_DEV, r, c), w[n], mom[n], var[n], l, res,
                                  tr=_tile_rows(r))
            out_g[n], out_d[n], out_m[n], out_v[n] = res
    for n, parts in zip(("hy_w_in", "hy_w_out", "s5_w_a", "s5_w_b"), tuple(p_hy) + tuple(p_s5)):
        shp = w[n].shape
        w2 = _flat2d(w[n])
        res = adamw("adamw_" + n, parts.reshape((N_DEV,) + w2.shape), w2, _flat2d(mom[n]), _flat2d(var[n]),
                    tr=_tile_rows(w2.shape[0]))
        out_g[n], out_d[n], out_m[n], out_v[n] = [a.reshape(shp) for a in res]
    g_small = {}
    for n, full in zip(SMALL, _unpack(small_sum, [small[n].shape for n in SMALL])):
        if n in SMALL_SHARDED:
            ax = SMALL_SHARDED[n]
            full = lax.dynamic_slice_in_dim(full, me * w[n].shape[ax], w[n].shape[ax], axis=ax)
        g_small[n] = full
    packs = [_pack([t[n] for n in SMALL]) for t in (g_small, w, mom, var)]
    res = adamw("adamw_small", packs[0][None], packs[1], packs[2], packs[3], tr=_tile_rows(packs[0].shape[0]))
    for dst, a in zip((out_g, out_d, out_m, out_v), res):
        dst.update(zip(SMALL, _unpack(a, [w[n].shape for n in SMALL])))

    total = lax.psum(loss[0, 0], ("x", "y", "c"))
    return (total, dx0.reshape(bl, seq, d), *[out_g[n] for n in WEIGHTS], *[out_d[n] for n in WEIGHTS],
            *[out_m[n] for n in WEIGHTS], *[out_v[n] for n in WEIGHTS])
```

```python
import functools
import math

import jax
import jax.numpy as jnp
from jax import lax
from jax.experimental import pallas as pl
from jax.experimental.pallas import tpu as pltpu

f32 = jnp.float32
bf16 = jnp.bfloat16

N_DEV = 8
D_MODEL = 1024
D_FF = 2816
EPS = 1e-6
FFN_RES = 0.5
CONV_W = 4
SSD_HEADS = 16
SSD_HEAD_DIM = 64
SSD_GROUPS = 2
SSD_STATE = 128
SSD_HG = SSD_HEADS // SSD_GROUPS
SSD_GW = SSD_HG * SSD_HEAD_DIM
CHUNK = 128
ML_HEADS = 4
ML_HD = 256
S5_GROUP = 16
S5_GROUPS = 64
S5_STATE = 64
S5_CB = 8
S5_CH = (S5_GROUPS // S5_CB) * S5_STATE
S5_TL = 256
S5_SUB = 16
LANES = 128
IN_COLS = 4624
PROJ_W = 4864
OFF_Z, OFF_MX, OFF_MZ, OFF_XBC, OFF_DT = 0, 1024, 2048, 3072, 4608
ADAM_LR, ADAM_B1, ADAM_B2, ADAM_EPS, ADAM_WD, ADAM_STEP = 0.001, 0.9, 0.999, 1e-08, 0.01, 10
NEG = -1e30
VMEM_LIMIT = 56 * 1024 * 1024
HI = lax.Precision.HIGHEST


def _cp(n):
    return pltpu.CompilerParams(dimension_semantics=("arbitrary",) * n, vmem_limit_bytes=VMEM_LIMIT)


def _dg(a, b, ca, cb):
    return lax.dot_general(a.astype(bf16), b.astype(bf16), (((ca,), (cb,)), ((), ())), preferred_element_type=f32)


@functools.partial(jax.custom_vjp, nondiff_argnums=(2, 3))
def bdot(a, b, ca, cb):
    return _dg(a, b, ca, cb)


def _bdot_fwd(a, b, ca, cb):
    return _dg(a, b, ca, cb), (a, b)


def _bdot_bwd(ca, cb, res, ct):
    a, b = res
    da = _dg(ct, b, 1, 1 - cb) if ca == 1 else _dg(b, ct, 1 - cb, 1)
    db = _dg(a, ct, 1 - ca, 0) if cb == 0 else _dg(ct, a, 0, 1 - ca)
    return da, db


bdot.defvjp(_bdot_fwd, _bdot_bwd)


def hdot(a, b):
    return jnp.dot(a, b, precision=HI, preferred_element_type=f32)


def _iota(shape, dim):
    return lax.broadcasted_iota(jnp.int32, shape, dim)


def _tri(n):
    return (_iota((n, n), 0) >= _iota((n, n), 1))


@functools.partial(jax.custom_vjp, nondiff_argnums=(1,))
def tshift(x, k):
    return jnp.where(_iota(x.shape, 0) >= k, pltpu.roll(x, k, 0), 0.0)


def _tshift_fwd(x, k):
    return tshift(x, k), None


def _tshift_bwd(k, _, ct):
    n = ct.shape[0]
    return (jnp.where(_iota(ct.shape, 0) < n - k, pltpu.roll(ct, n - k, 0), 0.0),)


tshift.defvjp(_tshift_fwd, _tshift_bwd)


def _lane_pick(a, idx):
    return jnp.sum(jnp.where(_iota(a.shape, 1) == idx, a, 0.0), axis=1, keepdims=True)


def _row_pick(a, idx):
    return jnp.sum(jnp.where(_iota(a.shape, 0) == idx, a, 0.0), axis=0, keepdims=True)


def _silu(x):
    return x * jax.nn.sigmoid(x)


def map_fwd(name, f, grid, ins, in_specs, out_shapes, out_specs):
    n_in = len(ins)

    def body(*refs):
        pids = tuple(pl.program_id(i) for i in range(len(grid)))
        outs = f(pids, *[r[...] for r in refs[:n_in]])
        for r, o in zip(refs[n_in:], outs):
            r[...] = o.astype(r.dtype)

    return pl.pallas_call(body, name=name, grid=grid, in_specs=in_specs, out_specs=out_specs,
                          out_shape=out_shapes, compiler_params=_cp(len(grid)))(*ins)


def scan_fwd(name, f, grid, slot_axis, ins, in_specs, out_shapes, out_specs, state_shapes, state_init, save_shapes, save_specs,
             ride=None):
    n_in, n_out, n_st = len(ins), len(out_shapes), len(state_shapes)
    n_slots = grid[slot_axis]
    cax = len(grid) - 1 if slot_axis != len(grid) - 1 else len(grid) - 2
    rider = Rider(ride)
    nr = rider.n

    def body(*refs):
        pids = tuple(pl.program_id(i) for i in range(len(grid)))
        in_refs, r_ins = refs[:n_in], refs[n_in:n_in + nr]
        o0 = n_in + nr
        out_refs, save_refs = refs[o0:o0 + n_out], refs[o0 + n_out:o0 + n_out + n_st]
        r_outs = refs[o0 + n_out + n_st:o0 + n_out + n_st + nr]
        st_refs = refs[o0 + n_out + n_st + nr:o0 + n_out + 2 * n_st + nr]
        sems = refs[o0 + n_out + 2 * n_st + nr:]
        rider.start(grid, r_ins, r_outs, sems)
        slot = pids[slot_axis]

        @pl.when(pids[cax] == 0)
        def _():
            for s, init in zip(st_refs, state_init):
                s[slot] = jnp.full(s.shape[1:], init, f32)

        states = tuple(s[slot] for s in st_refs)
        for sv, st in zip(save_refs, states):
            sv[...] = st.reshape(sv.shape)
        outs, new = f(pids, states, *[r[...] for r in in_refs])
        for r, o in zip(out_refs, outs):
            r[...] = o.astype(r.dtype)
        for s, v in zip(st_refs, new):
            s[slot] = v
        rider.wait(grid, r_ins, r_outs, sems)

    scratch = [pltpu.VMEM((n_slots,) + tuple(s), f32) for s in state_shapes]
    return pl.pallas_call(body, name=name, grid=grid, in_specs=list(in_specs) + rider.specs(),
                          out_specs=list(out_specs) + list(save_specs) + rider.specs(),
                          out_shape=list(out_shapes) + list(save_shapes) + rider.out_shapes(),
                          scratch_shapes=scratch + rider.scratch(), compiler_params=_cp(len(grid)))(*ins, *rider.arrays())


def scan_bwd(name, f, grid, slot_axis, ins, in_specs, saves, save_specs, cts, ct_specs, state_shapes, wrt, acc_first):
    n_in, n_st, n_ct = len(ins), len(saves), len(cts)
    n_slots = grid[slot_axis]
    cax = len(grid) - 1 if slot_axis != len(grid) - 1 else len(grid) - 2

    def body(*refs):
        pids = tuple(pl.program_id(i) for i in range(len(grid)))
        in_refs = refs[:n_in]
        save_refs = refs[n_in:n_in + n_st]
        ct_refs = refs[n_in + n_st:n_in + n_st + n_ct]
        out_refs = refs[n_in + n_st + n_ct:n_in + n_st + n_ct + len(wrt)]
        dst_refs = refs[n_in + n_st + n_ct + len(wrt):]
        slot = pids[slot_axis]

        @pl.when(pids[cax] == 0)
        def _():
            for s in dst_refs:
                s[slot] = jnp.zeros(s.shape[1:], f32)

        vals = [r[...] for r in in_refs]
        states = tuple(sv[...].reshape(shp) for sv, shp in zip(save_refs, state_shapes))
        ctv = tuple(r[...].astype(f32) for r in ct_refs)
        dnew = tuple(s[slot] for s in dst_refs)

        def g(st, *dv):
            full = list(vals)
            for i, v in zip(wrt, dv):
                full[i] = v
            outs, new = f(pids, st, *full)
            return tuple(outs), tuple(new)

        _, vjp = jax.vjp(g, states, *[vals[i] for i in wrt])
        grads = vjp((ctv, dnew))
        for s, v in zip(dst_refs, grads[0]):
            s[slot] = v
        for i, o_ref, gr in zip(wrt, out_refs, grads[1:]):
            first = acc_first.get(i)
            if first is None:
                o_ref[...] = gr.astype(o_ref.dtype)
            else:
                @pl.when(first(pids))
                def _():
                    o_ref[...] = jnp.zeros_like(o_ref)
                o_ref[...] += gr

    out_shapes = [jax.ShapeDtypeStruct(ins[i].shape, f32) for i in wrt]
    out_specs = [in_specs[i] for i in wrt]
    scratch = [pltpu.VMEM((n_slots,) + tuple(s), f32) for s in state_shapes]
    return pl.pallas_call(body, name=name, grid=grid, in_specs=list(in_specs) + list(save_specs) + list(ct_specs),
                          out_specs=out_specs, out_shape=out_shapes, scratch_shapes=scratch,
                          compiler_params=_cp(len(grid)))(*ins, *saves, *cts)


def _fit(dim, cap):
    if dim <= cap:
        return dim
    return max(t for t in range(LANES, cap + 1, LANES) if dim % t == 0)


def _matmul_tiles(m, n, kdim, ca):
    if ca == 1:
        return _fit(m, 512), _fit(n, 2432), _fit(kdim, 2432)
    return _fit(m, 1024), _fit(n, 1280), _fit(kdim, 512)


def matmul(name, a, b, ca=1, cb=0, add=None, out_dtype=f32, a_off=0, a_width=None, ride=None):
    rider = Rider(ride)
    nr = rider.n
    a_width = a.shape[1] if a_width is None else a_width
    kdim = b.shape[cb]
    n = b.shape[1 - cb]
    m = a.shape[0] if ca == 1 else a_width
    tm, tn, tk = _matmul_tiles(m, n, kdim, ca)
    assert m % tm == 0 and n % tn == 0 and kdim % tk == 0
    nk = kdim // tk
    if ca == 1:
        assert a_off % tk == 0 and a_width == kdim
        koff = a_off // tk
        a_spec = pl.BlockSpec((tm, tk), lambda i, j, k: (i, k + koff))
    else:
        assert a_off % tm == 0 and a.shape[0] == kdim
        ioff = a_off // tm
        a_spec = pl.BlockSpec((tk, tm), lambda i, j, k: (k, i + ioff))
    b_spec = pl.BlockSpec((tk, tn), lambda i, j, k: (k, j)) if cb == 0 else pl.BlockSpec((tn, tk), lambda i, j, k: (j, k))
    o_spec = pl.BlockSpec((tm, tn), lambda i, j, k: (i, j))
    has_add = add is not None

    n_in = 3 if has_add else 2
    grid = (m // tm, n // tn, nk)

    def body(*refs):
        a_ref, b_ref = refs[0], refs[1]
        add_ref = refs[2] if has_add else None
        r_ins, o_ref = refs[n_in:n_in + nr], refs[n_in + nr]
        r_outs, acc, sems = refs[n_in + nr + 1:n_in + 2 * nr + 1], refs[n_in + 2 * nr + 1], refs[n_in + 2 * nr + 2:]
        rider.start(grid, r_ins, r_outs, sems)
        k = pl.program_id(2)

        @pl.when(k == 0)
        def _():
            acc[...] = add_ref[...].astype(f32) if has_add else jnp.zeros_like(acc)

        acc[...] += _dg(a_ref[...], b_ref[...], ca, cb)

        @pl.when(k == nk - 1)
        def _():
            o_ref[...] = acc[...].astype(o_ref.dtype)

        rider.wait(grid, r_ins, r_outs, sems)

    ins = [a, b] + ([add] if has_add else [])
    specs = [a_spec, b_spec] + ([o_spec] if has_add else [])
    res = pl.pallas_call(body, name=name, grid=grid, in_specs=specs + rider.specs(), out_specs=[o_spec] + rider.specs(),
                         out_shape=[jax.ShapeDtypeStruct((m, n), out_dtype)] + rider.out_shapes(),
                         scratch_shapes=[pltpu.VMEM((tm, tn), f32)] + rider.scratch(), compiler_params=_cp(3))(*ins, *rider.arrays())
    return res if nr else res[0]


def f_rms(pids, x, w):
    r = lax.rsqrt(jnp.mean(x * x, axis=-1, keepdims=True) + EPS)
    return (x * r * w,)


def _row_spec(tm, width, col=0):
    return pl.BlockSpec((tm, width), lambda i: (i, col))


def _par_spec(shape):
    return pl.BlockSpec(shape, lambda *p: (0,) * len(shape))


def rms_fwd(x, w, tm=512):
    t, d = x.shape
    return map_fwd("rms_fwd", f_rms, (t // tm,), [x, w], [_row_spec(tm, d), _par_spec((1, d))],
                   [jax.ShapeDtypeStruct((t, d), f32)], [_row_spec(tm, d)])[0]


def rms_bwd(dys, x, w, dres, tm=512):
    t, d = x.shape
    n = len(dys)

    def body(*refs):
        x_ref, w_ref, dres_ref, dx_ref, dw_ref = refs[n:]
        dy = refs[0][...]
        for r in refs[1:n]:
            dy = dy + r[...]
        _, vjp = jax.vjp(lambda xx, ww: f_rms(None, xx, ww)[0], x_ref[...], w_ref[...])
        dx, dw = vjp(dy)
        dx_ref[...] = dx + dres_ref[...]

        @pl.when(pl.program_id(0) == 0)
        def _():
            dw_ref[...] = jnp.zeros_like(dw_ref)
        dw_ref[...] += dw

    return pl.pallas_call(body, name="rms_bwd", grid=(t // tm,),
                          in_specs=[_row_spec(tm, d)] * (n + 1) + [_par_spec((1, d)), _row_spec(tm, d)],
                          out_specs=[_row_spec(tm, d), _par_spec((1, d))],
                          out_shape=[jax.ShapeDtypeStruct((t, d), f32), jax.ShapeDtypeStruct((1, d), f32)],
                          compiler_params=_cp(1))(*dys, x, w, dres)


def loss_head(x, w, tgt, tm=512):
    t, d = x.shape

    def fl(xx, ww, tt):
        y = f_rms(None, xx, ww)[0]
        return 0.5 * jnp.sum(jnp.mean(jnp.square(y - tt), axis=-1, keepdims=True), axis=0, keepdims=True)

    def body(x_ref, w_ref, t_ref, loss_ref, dx_ref, dw_ref):
        val, vjp = jax.vjp(lambda xx, ww: fl(xx, ww, t_ref[...]), x_ref[...], w_ref[...])
        dx, dw = vjp(jnp.ones((1, 1), f32))
        dx_ref[...] = dx

        @pl.when(pl.program_id(0) == 0)
        def _():
            dw_ref[...] = jnp.zeros_like(dw_ref)
            loss_ref[...] = jnp.zeros_like(loss_ref)
        dw_ref[...] += dw
        loss_ref[...] += val

    return pl.pallas_call(body, name="loss_head", grid=(t // tm,),
                          in_specs=[_row_spec(tm, d), _par_spec((1, d)), _row_spec(tm, d)],
                          out_specs=[_par_spec((1, 1)), _row_spec(tm, d), _par_spec((1, d))],
                          out_shape=[jax.ShapeDtypeStruct((1, 1), f32), jax.ShapeDtypeStruct((t, d), f32),
                                     jax.ShapeDtypeStruct((1, d), f32)],
                          compiler_params=_cp(1))(x, w, tgt)


def ffn_fwd(x, nw, wg, wu, wd, tm=1024, ride=None):
    t, d = x.shape
    ns, _, _, fs = wg.shape
    rider = Rider(ride)
    nr = rider.n
    grid = (t // tm, ns)

    def body(*refs):
        x_ref, nw_ref, wg_ref, wu_ref, wd_ref = refs[:5]
        r_ins = refs[5:5 + nr]
        xo_ref, h_ref, g_ref, u_ref = refs[5 + nr:9 + nr]
        r_outs, acc, sems = refs[9 + nr:9 + 2 * nr], refs[9 + 2 * nr], refs[10 + 2 * nr:]
        rider.start(grid, r_ins, r_outs, sems)
        j = pl.program_id(1)

        @pl.when(j == 0)
        def _():
            h_ref[...] = f_rms(None, x_ref[...], nw_ref[...])[0].astype(bf16)
            acc[...] = jnp.zeros_like(acc)

        h = h_ref[...]
        g = jnp.dot(h, wg_ref[0, 0], preferred_element_type=f32)
        u = jnp.dot(h, wu_ref[0, 0], preferred_element_type=f32)
        g_ref[0] = g
        u_ref[0] = u
        acc[...] += jnp.dot((_silu(g) * u).astype(bf16), wd_ref[0, 0], preferred_element_type=f32)

        @pl.when(j == ns - 1)
        def _():
            xo_ref[...] = x_ref[...] + FFN_RES * acc[...]

        rider.wait(grid, r_ins, r_outs, sems)

    row = pl.BlockSpec((tm, d), lambda i, j: (i, 0))
    wcol = pl.BlockSpec((1, 1, d, fs), lambda i, j: (j, 0, 0, 0))
    wrow = pl.BlockSpec((1, 1, fs, d), lambda i, j: (j, 0, 0, 0))
    act = pl.BlockSpec((1, tm, fs), lambda i, j: (j, i, 0))
    return pl.pallas_call(body, name="ffn_fwd", grid=grid,
                          in_specs=[row, pl.BlockSpec((1, d), lambda i, j: (0, 0)), wcol, wcol, wrow] + rider.specs(),
                          out_specs=[row, row, act, act] + rider.specs(),
                          out_shape=[jax.ShapeDtypeStruct((t, d), f32), jax.ShapeDtypeStruct((t, d), bf16),
                                     jax.ShapeDtypeStruct((ns, t, fs), f32), jax.ShapeDtypeStruct((ns, t, fs), f32)]
                          + rider.out_shapes(),
                          scratch_shapes=[pltpu.VMEM((tm, d), f32)] + rider.scratch(),
                          compiler_params=_cp(2))(x, nw, wg, wu, wd, *rider.arrays())


def ffn_bwd_act(dy, x, nw, g, u, wg, wu, wd, tm=512, ride=None):
    t, d = x.shape
    ns, _, _, fs = wg.shape
    rider = Rider(ride)
    nr = rider.n
    grid = (t // tm, ns)

    def body(*refs):
        dy_ref, x_ref, nw_ref, g_ref, u_ref, wg_ref, wu_ref, wd_ref = refs[:8]
        r_ins = refs[8:8 + nr]
        dx_ref, dnw_ref, dg_ref, du_ref, a_ref, dyh_ref = refs[8 + nr:14 + nr]
        r_outs, acc, sems = refs[14 + nr:14 + 2 * nr], refs[14 + 2 * nr], refs[15 + 2 * nr:]
        rider.start(grid, r_ins, r_outs, sems)
        i, j = pl.program_id(0), pl.program_id(1)

        @pl.when(j == 0)
        def _():
            acc[...] = jnp.zeros_like(acc)
            dyh_ref[...] = (FFN_RES * dy_ref[...]).astype(bf16)

        dyh = dyh_ref[...]
        da = _dg(dyh, wd_ref[0, 0], 1, 1)
        gg, uu = g_ref[0], u_ref[0]
        sg = jax.nn.sigmoid(gg)
        si = gg * sg
        dgv = (da * uu * (sg * (1.0 + gg * (1.0 - sg)))).astype(bf16)
        duv = (da * si).astype(bf16)
        dg_ref[0] = dgv
        du_ref[0] = duv
        a_ref[0] = (si * uu).astype(bf16)
        acc[...] += _dg(dgv, wg_ref[0, 0], 1, 1) + _dg(duv, wu_ref[0, 0], 1, 1)

        @pl.when(j == ns - 1)
        def _():
            _, vjp = jax.vjp(lambda xx, ww: f_rms(None, xx, ww)[0], x_ref[...], nw_ref[...])
            dx, dw = vjp(acc[...])
            dx_ref[...] = dx + dy_ref[...]

            @pl.when(i == 0)
            def _():
                dnw_ref[...] = jnp.zeros_like(dnw_ref)
            dnw_ref[...] += dw

        rider.wait(grid, r_ins, r_outs, sems)

    row = pl.BlockSpec((tm, d), lambda i, j: (i, 0))
    wcol = pl.BlockSpec((1, 1, d, fs), lambda i, j: (j, 0, 0, 0))
    wrow = pl.BlockSpec((1, 1, fs, d), lambda i, j: (j, 0, 0, 0))
    act = pl.BlockSpec((1, tm, fs), lambda i, j: (j, i, 0))
    par = pl.BlockSpec((1, d), lambda i, j: (0, 0))
    return pl.pallas_call(body, name="ffn_bwd_act", grid=grid,
                          in_specs=[row, row, par, act, act, wcol, wcol, wrow] + rider.specs(),
                          out_specs=[row, par, act, act, act, row] + rider.specs(),
                          out_shape=[jax.ShapeDtypeStruct((t, d), f32), jax.ShapeDtypeStruct((1, d), f32)]
                          + [jax.ShapeDtypeStruct((ns, t, fs), bf16)] * 3 + [jax.ShapeDtypeStruct((t, d), bf16)]
                          + rider.out_shapes(),
                          scratch_shapes=[pltpu.VMEM((tm, d), f32)] + rider.scratch(),
                          compiler_params=_cp(2))(dy, x, nw, g, u, wg, wu, wd, *rider.arrays())


def ffn_bwd_w(h, dyh, dg, du, a, tk=1024, ride=None):
    t, d = h.shape
    ns, _, fs = dg.shape
    nk = t // tk
    rider = Rider(ride)
    nr = rider.n
    grid = (ns, nk)

    def body(*refs):
        h_ref, dy_ref, dg_ref, du_ref, a_ref = refs[:5]
        r_ins = refs[5:5 + nr]
        og, ou, od = refs[5 + nr:8 + nr]
        r_outs = refs[8 + nr:8 + 2 * nr]
        ag, au, ad = refs[8 + 2 * nr:11 + 2 * nr]
        sems = refs[11 + 2 * nr:]
        rider.start(grid, r_ins, r_outs, sems)
        k = pl.program_id(1)

        @pl.when(k == 0)
        def _():
            ag[...] = jnp.zeros_like(ag)
            au[...] = jnp.zeros_like(au)
            ad[...] = jnp.zeros_like(ad)

        hh = h_ref[...]
        ag[...] += _dg(hh, dg_ref[0], 0, 0)
        au[...] += _dg(hh, du_ref[0], 0, 0)
        ad[...] += _dg(a_ref[0], dy_ref[...], 0, 0)

        @pl.when(k == nk - 1)
        def _():
            og[0, 0] = ag[...].astype(og.dtype)
            ou[0, 0] = au[...].astype(ou.dtype)
            od[0, 0] = ad[...].astype(od.dtype)

        rider.wait(grid, r_ins, r_outs, sems)

    row = pl.BlockSpec((tk, d), lambda j, k: (k, 0))
    act = pl.BlockSpec((1, tk, fs), lambda j, k: (j, k, 0))
    wcol = pl.BlockSpec((1, 1, d, fs), lambda j, k: (j, 0, 0, 0))
    wrow = pl.BlockSpec((1, 1, fs, d), lambda j, k: (j, 0, 0, 0))
    return pl.pallas_call(body, name="ffn_bwd_w", grid=grid, in_specs=[row, row, act, act, act] + rider.specs(),
                          out_specs=[wcol, wcol, wrow] + rider.specs(),
                          out_shape=[jax.ShapeDtypeStruct((ns, 1, d, fs), bf16)] * 2
                          + [jax.ShapeDtypeStruct((ns, 1, fs, d), bf16)] + rider.out_shapes(),
                          scratch_shapes=[pltpu.VMEM((d, fs), f32), pltpu.VMEM((d, fs), f32), pltpu.VMEM((fs, d), f32)]
                          + rider.scratch(),
                          compiler_params=_cp(2))(h, dyh, dg, du, a, *rider.arrays())


def f_conv(pids, x, w, b):
    y = b + x * w[CONV_W - 1:CONV_W, :]
    for j in range(CONV_W - 1):
        y = y + tshift(x, CONV_W - 1 - j) * w[j:j + 1, :]
    return (_silu(y),)


def _conv_specs(seq, col0, cb):
    xs = pl.BlockSpec((seq, cb), lambda c, b: (b, col0 + c))
    ws = pl.BlockSpec((CONV_W, cb), lambda c, b: (0, c))
    bs = pl.BlockSpec((1, cb), lambda c, b: (0, c))
    ys = pl.BlockSpec((seq, cb), lambda c, b: (b, c))
    return xs, ws, bs, ys


def conv_fwd(name, src, col_off, w, b, seq, cb=256):
    t = src.shape[0]
    c = w.shape[1]
    xs, ws, bs, ys = _conv_specs(seq, col_off // cb, cb)
    return map_fwd(name, f_conv, (c // cb, t // seq), [src, w, b], [xs, ws, bs],
                   [jax.ShapeDtypeStruct((t, c), f32)], [ys])[0]


def conv_bwd(name, dy, src, col_off, w, b, seq, cb=256):
    t = src.shape[0]
    c = w.shape[1]
    xs, ws, bs, ys = _conv_specs(seq, col_off // cb, cb)

    def body(x_ref, w_ref, b_ref, dy_ref, dx_ref, dw_ref, db_ref):
        _, vjp = jax.vjp(lambda xx, ww, bb: f_conv(None, xx, ww, bb)[0], x_ref[...], w_ref[...], b_ref[...])
        dx, dw, db = vjp(dy_ref[...])
        dx_ref[...] = dx

        @pl.when(pl.program_id(1) == 0)
        def _():
            dw_ref[...] = jnp.zeros_like(dw_ref)
            db_ref[...] = jnp.zeros_like(db_ref)
        dw_ref[...] += dw
        db_ref[...] += db

    return pl.pallas_call(body, name=name, grid=(c // cb, t // seq), in_specs=[xs, ws, bs, ys], out_specs=[ys, ws, bs],
                          out_shape=[jax.ShapeDtypeStruct((t, c), f32), jax.ShapeDtypeStruct(w.shape, f32),
                                     jax.ShapeDtypeStruct(b.shape, f32)], compiler_params=_cp(2))(src, w, b, dy)


def f_ssd(pids, states, xs, dtraw, bm, cm, a_log, dt_bias, d_skip):
    g = pids[2]
    (hn,) = states
    l = xs.shape[0]
    head_of_lane = _iota((LANES, SSD_GW), 1) // SSD_HEAD_DIM + SSD_HG * g
    expand = (_iota((LANES, SSD_GW), 0) == head_of_lane).astype(f32)
    tri = _tri(l)
    dt = jax.nn.softplus(dtraw + dt_bias)
    adt = dt * (-jnp.exp(a_log))
    cs = hdot(tri.astype(f32), adt)
    cst = cs.T
    cs_last = cs[l - 1:l, :]
    dt_e, cs_e, csl_e = hdot(dt, expand), hdot(cs, expand), hdot(cs_last, expand)
    xd = xs * dt_e
    gmat = bdot(cm, bm, 1, 1)
    half = _iota((l, LANES), 1) < SSD_HEAD_DIM
    blocks = []
    for pair in range(SSD_HG // 2):
        xb = xd[:, pair * LANES:(pair + 1) * LANES]
        res = []
        for sub in range(2):
            hid = SSD_HG * g + 2 * pair + sub
            col, row = _lane_pick(cs, hid), _row_pick(cst, hid)
            lm = jnp.exp(jnp.where(tri, col - row, NEG))
            res.append(bdot(gmat * lm, xb, 1, 0))
        blocks.append(jnp.where(half, res[0], res[1]))
    y = jnp.concatenate(blocks, axis=1)
    y = y + jnp.exp(cs_e) * bdot(cm, hn, 1, 0)
    y = y + hdot(d_skip, expand) * xs
    hn_new = jnp.exp(csl_e) * hn + bdot(bm, jnp.exp(csl_e - cs_e) * xd, 0, 0)
    return (y,), (hn_new,)


def _ssd_specs(seq, nch, rev):
    cc = (lambda c: nch - 1 - c) if rev else (lambda c: c)
    xs = pl.BlockSpec((CHUNK, SSD_GW), lambda b, c, g: (b * nch + cc(c), g))
    dt = pl.BlockSpec((CHUNK, LANES), lambda b, c, g: (b * nch + cc(c), OFF_DT // LANES))
    bm = pl.BlockSpec((CHUNK, SSD_STATE), lambda b, c, g: (b * nch + cc(c), 1024 // SSD_STATE + g))
    cm = pl.BlockSpec((CHUNK, SSD_STATE), lambda b, c, g: (b * nch + cc(c), 1024 // SSD_STATE + SSD_GROUPS + g))
    par = pl.BlockSpec((1, LANES), lambda b, c, g: (0, 0))
    sv = pl.BlockSpec((1, 1, SSD_STATE, SSD_GW), lambda b, c, g: (b * nch + cc(c), g, 0, 0))
    ddt = pl.BlockSpec((CHUNK, LANES), lambda b, c, g: (b * nch + cc(c), 0))
    dbc = pl.BlockSpec((CHUNK, SSD_STATE), lambda b, c, g: (b * nch + cc(c), g))
    return xs, dt, bm, cm, par, sv, ddt, dbc


def ssd_fwd(xbc, proj, a_log, dt_bias, d_skip, seq, ride=None):
    t = xbc.shape[0]
    nch = seq // CHUNK
    xs, dt, bm, cm, par, sv, _, _ = _ssd_specs(seq, nch, False)
    grid = (t // seq, nch, SSD_GROUPS)
    y, hsave, *got = scan_fwd("ssd_fwd", f_ssd, grid, 2, [xbc, proj, xbc, xbc, a_log, dt_bias, d_skip],
                              [xs, dt, bm, cm, par, par, par], [jax.ShapeDtypeStruct((t, SSD_GROUPS * SSD_GW), f32)], [xs],
                              [(SSD_STATE, SSD_GW)], [0.0],
                              [jax.ShapeDtypeStruct((t // CHUNK, SSD_GROUPS, SSD_STATE, SSD_GW), f32)], [sv], ride=ride)
    return y, hsave, got


def ssd_bwd(dy, xbc, proj, a_log, dt_bias, d_skip, hsave, seq):
    t = xbc.shape[0]
    nch = seq // CHUNK
    xs, dt, bm, cm, par, sv, ddt, dbc = _ssd_specs(seq, nch, True)
    grid = (t // seq, nch, SSD_GROUPS)

    def body(x_ref, dt_ref, b_ref, c_ref, al_ref, db_ref, ds_ref, h_ref, dy_ref,
             dxbc_x, dxbc_b, dxbc_c, ddt_ref, dal_ref, ddb_ref, dds_ref, dst):
        pids = tuple(pl.program_id(i) for i in range(3))
        slot = pids[2]

        @pl.when(pids[1] == 0)
        def _():
            dst[slot] = jnp.zeros(dst.shape[1:], f32)

        vals = [x_ref[...], dt_ref[...], b_ref[...], c_ref[...], al_ref[...], db_ref[...], ds_ref[...]]

        def gfun(st, *v):
            outs, new = f_ssd(pids, (st,), *v)
            return outs[0], new[0]

        _, vjp = jax.vjp(gfun, h_ref[0, 0], *vals)
        grads = vjp((dy_ref[...], dst[slot]))
        dst[slot] = grads[0]
        dxbc_x[...] = grads[1]
        dxbc_b[...] = grads[3]
        dxbc_c[...] = grads[4]

        @pl.when(slot == 0)
        def _():
            ddt_ref[...] = jnp.zeros_like(ddt_ref)
        ddt_ref[...] += grads[2]
        first = jnp.logical_and(jnp.logical_and(pids[0] == 0, pids[1] == 0), slot == 0)

        @pl.when(first)
        def _():
            dal_ref[...] = jnp.zeros_like(dal_ref)
            ddb_ref[...] = jnp.zeros_like(ddb_ref)
            dds_ref[...] = jnp.zeros_like(dds_ref)
        dal_ref[...] += grads[5]
        ddb_ref[...] += grads[6]
        dds_ref[...] += grads[7]

    bc_shape = jax.ShapeDtypeStruct((t, SSD_GROUPS * SSD_STATE), f32)
    par_shape = jax.ShapeDtypeStruct((1, LANES), f32)
    outs = pl.pallas_call(body, name="ssd_bwd", grid=grid, in_specs=[xs, dt, bm, cm, par, par, par, sv, xs],
                          out_specs=[xs, dbc, dbc, ddt, par, par, par],
                          out_shape=[jax.ShapeDtypeStruct((t, SSD_GROUPS * SSD_GW), f32),
                                     bc_shape, bc_shape, jax.ShapeDtypeStruct((t, LANES), f32),
                                     par_shape, par_shape, par_shape],
                          scratch_shapes=[pltpu.VMEM((SSD_GROUPS, SSD_STATE, SSD_GW), f32)],
                          compiler_params=_cp(3))(xbc, proj, xbc, xbc, a_log, dt_bias, d_skip, hsave, dy)
    return outs


def f_ssd_epi(pids, y, z, nw):
    yg = y * _silu(z)
    hw = yg.shape[1] // SSD_GROUPS
    parts = []
    for g in range(SSD_GROUPS):
        p = yg[:, g * hw:(g + 1) * hw]
        parts.append(p * lax.rsqrt(jnp.mean(p * p, axis=-1, keepdims=True) + EPS))
    return (jnp.concatenate(parts, axis=1) * nw,)


def f_ml_epi(pids, hm, xc, mz, nw, skip):
    parts = []
    for h in range(ML_HEADS):
        p = hm[:, h * ML_HD:(h + 1) * ML_HD]
        mu = jnp.mean(p, axis=-1, keepdims=True)
        var = jnp.mean(jnp.square(p - mu), axis=-1, keepdims=True)
        parts.append((p - mu) * lax.rsqrt(var + EPS))
    hn = jnp.concatenate(parts, axis=1) * nw
    return ((hn + skip * xc) * _silu(mz),)


def f_s5_post(pids, ys, u, d_skip):
    return (jax.nn.gelu(ys + d_skip * u),)


def f_glu(pids, pab, ba, bb):
    d = ba.shape[1]
    return ((pab[:, :d] + ba) * jax.nn.sigmoid(pab[:, d:] + bb),)


def f_glu_res(pids, pab, xres, ba, bb):
    return (xres + f_glu(pids, pab, ba, bb)[0],)


def rowwise_fwd(name, f, rows, row_cols, pars, out_width, tm=512):
    t = rows[0].shape[0]
    specs = [_row_spec(tm, w, c) for (w, c) in row_cols] + [_par_spec(p.shape) for p in pars]
    return map_fwd(name, f, (t // tm,), list(rows) + list(pars), specs, [jax.ShapeDtypeStruct((t, out_width), f32)],
                   [_row_spec(tm, out_width)])[0]


def rowwise_bwd(name, f, rows, row_cols, pars, dy, tm=256):
    t = rows[0].shape[0]
    n_r, n_p = len(rows), len(pars)
    specs = [_row_spec(tm, w, c) for (w, c) in row_cols] + [_par_spec(p.shape) for p in pars]
    out_w = dy.shape[1]

    def body(*refs):
        vals = [r[...] for r in refs[:n_r + n_p]]
        dy_ref = refs[n_r + n_p]
        outs = refs[n_r + n_p + 1:]
        _, vjp = jax.vjp(lambda *v: f(None, *v)[0], *vals)
        grads = vjp(dy_ref[...])
        for k in range(n_r):
            outs[k][...] = grads[k]

        @pl.when(pl.program_id(0) == 0)
        def _():
            for k in range(n_p):
                outs[n_r + k][...] = jnp.zeros_like(outs[n_r + k])
        for k in range(n_p):
            outs[n_r + k][...] += grads[n_r + k]

    out_shapes = [jax.ShapeDtypeStruct((t, w), f32) for (w, c) in row_cols] + [jax.ShapeDtypeStruct(p.shape, f32) for p in pars]
    out_specs = [_row_spec(tm, w) for (w, c) in row_cols] + [_par_spec(p.shape) for p in pars]
    return pl.pallas_call(body, name=name, grid=(t // tm,), in_specs=specs + [_row_spec(tm, out_w)], out_specs=out_specs,
                          out_shape=out_shapes, compiler_params=_cp(1))(*rows, *pars, dy)


def f_ml(pids, states, q, k, v, g1, g2, g3, b_if):
    h = pids[2]
    cst, nst, mst = states
    l = q.shape[0]
    gt = g1 + g2 + g3 + b_if
    k = k * (1.0 / math.sqrt(ML_HD))
    tri = _tri(l)
    bc_all = hdot(tri.astype(f32), jax.nn.log_sigmoid(gt))
    bcum, ig = _lane_pick(bc_all, ML_HEADS + h), _lane_pick(gt, h)
    bcum_t, ig_t = _row_pick(bc_all.T, ML_HEADS + h), _row_pick(gt.T, h)
    b_last = bcum[l - 1:l, :]
    dlog = jnp.where(tri, bcum - bcum_t + ig_t, NEG)
    ws = b_last - bcum + ig
    m_prev = mst[:, 0:1]
    m_new = lax.stop_gradient(jnp.maximum(b_last + m_prev, jnp.max(ws, axis=0, keepdims=True)))
    decay = jnp.exp(b_last + m_prev - m_new)
    wts = jnp.exp(ws - m_new)
    c_new = decay * cst + bdot(wts * v, k, 0, 0)
    n_new = decay * nst + jnp.sum(wts * k, axis=0, keepdims=True)
    m_inter = bcum + m_prev
    m_t = lax.stop_gradient(jnp.maximum(jnp.max(dlog, axis=1, keepdims=True), m_inter))
    scores = bdot(q, k, 1, 1) * jnp.exp(dlog - m_t)
    inter_w = jnp.exp(m_inter - m_t)
    num = bdot(scores, v, 1, 0) + inter_w * bdot(q, cst, 1, 1)
    den = jnp.sum(scores, axis=1, keepdims=True) + inter_w * jnp.sum(q * nst, axis=1, keepdims=True)
    hout = num / jnp.maximum(jnp.abs(den), jnp.exp(-m_t))
    return (hout,), (c_new, n_new, jnp.broadcast_to(m_new, mst.shape))


def _ml_specs(nch, rev):
    cc = (lambda c: nch - 1 - c) if rev else (lambda c: c)
    hd = pl.BlockSpec((CHUNK, ML_HD), lambda b, c, h: (b * nch + cc(c), h))
    gt = pl.BlockSpec((CHUNK, LANES), lambda b, c, h: (b * nch + cc(c), 0))
    par = pl.BlockSpec((1, LANES), lambda b, c, h: (0, 0))
    sc = pl.BlockSpec((1, 1, ML_HD, ML_HD), lambda b, c, h: (b * nch + cc(c), h, 0, 0))
    sn = pl.BlockSpec((1, 1, 1, ML_HD), lambda b, c, h: (b * nch + cc(c), h, 0, 0))
    sm = pl.BlockSpec((1, 1, 1, LANES), lambda b, c, h: (b * nch + cc(c), h, 0, 0))
    return hd, gt, par, sc, sn, sm


ML_STATE_SHAPES = [(ML_HD, ML_HD), (1, ML_HD), (1, LANES)]


def ml_fwd(q, k, v, g1, g2, g3, b_if, seq, ride=None):
    t = q.shape[0]
    nch = seq // CHUNK
    hd, gt, par, sc, sn, sm = _ml_specs(nch, False)
    nc = t // CHUNK
    outs = scan_fwd("ml_fwd", f_ml, (t // seq, nch, ML_HEADS), 2, [q, k, v, g1, g2, g3, b_if],
                    [hd, hd, hd, gt, gt, gt, par], [jax.ShapeDtypeStruct((t, ML_HEADS * ML_HD), f32)], [hd],
                    ML_STATE_SHAPES, [0.0, 0.0, NEG],
                    [jax.ShapeDtypeStruct((nc, ML_HEADS, ML_HD, ML_HD), f32), jax.ShapeDtypeStruct((nc, ML_HEADS, 1, ML_HD), f32),
                     jax.ShapeDtypeStruct((nc, ML_HEADS, 1, LANES), f32)], [sc, sn, sm], ride=ride)
    return outs[0], outs[1:4], outs[4:]


def ml_bwd(dh, q, k, v, g1, g2, g3, b_if, saves, seq, ride=None):
    t = q.shape[0]
    nch = seq // CHUNK
    hd, gt, par, sc, sn, sm = _ml_specs(nch, True)
    rider = Rider(ride)
    nr = rider.n
    grid = (t // seq, nch, ML_HEADS)

    def f(pids, states, q, k, v, gsum, b_if):
        return f_ml(pids, states, q, k, v, gsum, jnp.zeros_like(gsum), jnp.zeros_like(gsum), b_if)

    def body(*refs):
        q_ref, k_ref, v_ref, g1_ref, g2_ref, g3_ref, b_ref, c_ref, n_ref, m_ref, dh_ref = refs[:11]
        r_ins = refs[11:11 + nr]
        dq_ref, dk_ref, dv_ref, dg_ref, db_ref = refs[11 + nr:16 + nr]
        r_outs = refs[16 + nr:16 + 2 * nr]
        dc_s, dn_s = refs[16 + 2 * nr:18 + 2 * nr]
        sems = refs[18 + 2 * nr:]
        rider.start(grid, r_ins, r_outs, sems)
        pids = tuple(pl.program_id(i) for i in range(3))
        slot = pids[2]

        @pl.when(pids[1] == 0)
        def _():
            dc_s[slot] = jnp.zeros(dc_s.shape[1:], f32)
            dn_s[slot] = jnp.zeros(dn_s.shape[1:], f32)

        gsum = g1_ref[...] + g2_ref[...] + g3_ref[...]
        mst = m_ref[0, 0]

        def gfun(cst, nst, qq, kk, vv, gs, bb):
            outs, new = f(pids, (cst, nst, mst), qq, kk, vv, gs, bb)
            return outs[0], new[0], new[1]

        _, vjp = jax.vjp(gfun, c_ref[0, 0], n_ref[0, 0], q_ref[...], k_ref[...], v_ref[...], gsum, b_ref[...])
        grads = vjp((dh_ref[...], dc_s[slot], dn_s[slot]))
        dc_s[slot] = grads[0]
        dn_s[slot] = grads[1]
        dq_ref[...] = grads[2]
        dk_ref[...] = grads[3]
        dv_ref[...] = grads[4]

        @pl.when(slot == 0)
        def _():
            dg_ref[...] = jnp.zeros_like(dg_ref)
        dg_ref[...] += grads[5]
        first = jnp.logical_and(jnp.logical_and(pids[0] == 0, pids[1] == 0), slot == 0)

        @pl.when(first)
        def _():
            db_ref[...] = jnp.zeros_like(db_ref)
        db_ref[...] += grads[6]
        rider.wait(grid, r_ins, r_outs, sems)

    big = jax.ShapeDtypeStruct((t, ML_HEADS * ML_HD), f32)
    return pl.pallas_call(body, name="ml_bwd", grid=grid,
                          in_specs=[hd, hd, hd, gt, gt, gt, par, sc, sn, sm, hd] + rider.specs(),
                          out_specs=[hd, hd, hd, gt, par] + rider.specs(),
                          out_shape=[big, big, big, jax.ShapeDtypeStruct((t, LANES), f32), jax.ShapeDtypeStruct((1, LANES), f32)]
                          + rider.out_shapes(),
                          scratch_shapes=[pltpu.VMEM((ML_HEADS, ML_HD, ML_HD), f32), pltpu.VMEM((ML_HEADS, 1, ML_HD), f32)]
                          + rider.scratch(),
                          compiler_params=_cp(3))(q, k, v, g1, g2, g3, b_if, *saves, dh, *rider.arrays())


def _block_prefix(z, transpose):
    n = z.shape[0]
    r, c = _iota((n, n), 0), _iota((n, n), 1)
    keep = jnp.logical_and(r // S5_SUB == c // S5_SUB, (c >= r) if transpose else (c <= r))
    m = jnp.where(keep, 1.0, 0.0).astype(bf16)
    hi = z.astype(bf16)
    lo = (z - hi.astype(f32)).astype(bf16)
    return jnp.dot(m, hi, preferred_element_type=f32) + jnp.dot(m, lo, preferred_element_type=f32)


@jax.custom_vjp
def block_prefix(z):
    return _block_prefix(z, False)


block_prefix.defvjp(lambda z: (_block_prefix(z, False), None), lambda _, ct: (_block_prefix(ct, True),))


def _cmul(a, b):
    h = b.shape[1] // 2
    ar, ai, br, bi = a[:, :h], a[:, h:], b[:, :h], b[:, h:]
    return jnp.concatenate([ar * br - ai * bi, ar * bi + ai * br], axis=1)


def f_s5(pids, states, u, bb, cc, tab):
    (carry,) = states
    tl = u.shape[0]
    nsub = tl // S5_SUB
    rep = lambda t: jnp.concatenate([t] * nsub, axis=0)
    p1, p0, q0 = tab[0:S5_SUB], tab[S5_SUB:2 * S5_SUB], tab[2 * S5_SUB:3 * S5_SUB]
    lam_sub = tab[S5_SUB - 1:S5_SUB]
    bu = bdot(u, bb, 1, 0)
    xl = _cmul(rep(p0), block_prefix(_cmul(rep(q0), bu)))
    e, entering = carry, []
    for k in range(nsub):
        entering.append(jnp.broadcast_to(e, (S5_SUB, e.shape[1])))
        e = xl[(k + 1) * S5_SUB - 1:(k + 1) * S5_SUB] + _cmul(lam_sub, e)
    x = xl + _cmul(rep(p1), jnp.concatenate(entering, axis=0))
    y = bdot(x, cc, 1, 0)
    return (y,), (e,)


def _s5_specs(ntl, rev):
    tt = (lambda t: ntl - 1 - t) if rev else (lambda t: t)
    us = pl.BlockSpec((S5_TL, LANES), lambda c, b, t: (b * ntl + tt(t), c))
    bbs = pl.BlockSpec((1, LANES, 2 * S5_CH), lambda c, b, t: (c, 0, 0))
    ccs = pl.BlockSpec((1, 2 * S5_CH, LANES), lambda c, b, t: (c, 0, 0))
    pws = pl.BlockSpec((1, 3 * S5_SUB, 2 * S5_CH), lambda c, b, t: (c, 0, 0))
    sv = pl.BlockSpec((1, 1, 1, 2 * S5_CH), lambda c, b, t: (b * ntl + tt(t), c, 0, 0))
    return us, bbs, ccs, pws, sv


def s5_fwd(u, bb, cc, pw, seq):
    t = u.shape[0]
    ntl = seq // S5_TL
    us, bbs, ccs, pws, sv = _s5_specs(ntl, False)

    def f(pids, states, uu, b3, c3, p3):
        return f_s5(pids, states, uu, b3[0], c3[0], p3[0])

    y, carries = scan_fwd("s5_fwd", f, (S5_CB, t // seq, ntl), 0, [u, bb, cc, pw], [us, bbs, ccs, pws],
                          [jax.ShapeDtypeStruct((t, S5_CB * LANES), f32)], [us], [(1, 2 * S5_CH)], [0.0],
                          [jax.ShapeDtypeStruct((t // S5_TL, S5_CB, 1, 2 * S5_CH), f32)], [sv])
    return y, carries


def s5_bwd(dy, u, bb, cc, pw, carries, seq):
    t = u.shape[0]
    ntl = seq // S5_TL
    us, bbs, ccs, pws, sv = _s5_specs(ntl, True)

    def f(pids, states, uu, b3, c3, p3):
        return f_s5(pids, states, uu, b3[0], c3[0], p3[0])

    first = lambda pids: jnp.logical_and(pids[1] == 0, pids[2] == 0)
    return scan_bwd("s5_bwd", f, (S5_CB, t // seq, ntl), 0, [u, bb, cc, pw], [us, bbs, ccs, pws], [carries], [sv],
                    [dy], [us], [(1, 2 * S5_CH)], [0, 1, 2, 3], {1: first, 2: first, 3: first})


def _adam_math(g, w, m, v):
    m2 = ADAM_B1 * m + (1.0 - ADAM_B1) * g
    v2 = ADAM_B2 * v + (1.0 - ADAM_B2) * jnp.square(g)
    m_hat = m2 / (1.0 - ADAM_B1 ** ADAM_STEP)
    v_hat = v2 / (1.0 - ADAM_B2 ** ADAM_STEP)
    delta = -ADAM_LR * (m_hat / (jnp.sqrt(v_hat) + ADAM_EPS) + ADAM_WD * w)
    return delta, m2, v2


def adamw(name, parts, w, m, v, tr=256):
    n, r, c = parts.shape
    tr = min(tr, r)
    assert r % tr == 0

    def body(p_ref, w_ref, m_ref, v_ref, g_ref, d_ref, m2_ref, v2_ref):
        g = p_ref[0].astype(f32)
        for s in range(1, n):
            g = g + p_ref[s].astype(f32)
        d, m2, v2 = _adam_math(g, w_ref[...], m_ref[...], v_ref[...])
        g_ref[...] = g
        d_ref[...] = d
        m2_ref[...] = m2
        v2_ref[...] = v2

    ps = pl.BlockSpec((n, tr, c), lambda i: (0, i, 0))
    rs = pl.BlockSpec((tr, c), lambda i: (i, 0))
    return pl.pallas_call(body, name=name, grid=(r // tr,), in_specs=[ps, rs, rs, rs], out_specs=[rs] * 4,
                          out_shape=[jax.ShapeDtypeStruct((r, c), f32)] * 4, compiler_params=_cp(1))(parts, w, m, v)


def adamw_layer(name, parts, w, m, v, layer, prev=None, tr=256):
    n, r, c = parts.shape
    nl = w.shape[0]
    tr = min(tr, r)
    assert r % tr == 0 and w.shape[1:] == (r, c)
    n_prev = 0 if prev is None else 4

    def body(*refs):
        p_ref, w_ref, m_ref, v_ref = refs[:4]
        g_ref, d_ref, m2_ref, v2_ref = refs[4 + n_prev:]
        g = p_ref[0].astype(f32)
        for s in range(1, n):
            g = g + p_ref[s].astype(f32)
        d, m2, v2 = _adam_math(g, w_ref[0], m_ref[0], v_ref[0])
        g_ref[0] = g
        d_ref[0] = d
        m2_ref[0] = m2
        v2_ref[0] = v2

    ps = pl.BlockSpec((n, tr, c), lambda i: (0, i, 0))
    rs = pl.BlockSpec((1, tr, c), lambda i: (layer, i, 0))
    anyspec = pl.BlockSpec(memory_space=pl.ANY)
    return pl.pallas_call(body, name=name, grid=(r // tr,), in_specs=[ps, rs, rs, rs] + [anyspec] * n_prev, out_specs=[rs] * 4,
                          out_shape=[jax.ShapeDtypeStruct((nl, r, c), f32)] * 4,
                          input_output_aliases={4 + i: i for i in range(n_prev)},
                          compiler_params=_cp(1))(parts, w, m, v, *(prev or ()))


def sum_parts(name, parts, tr=256):
    n, r, c = parts.shape
    tr = min(tr, r)
    assert r % tr == 0

    def body(p_ref, o_ref):
        g = p_ref[0].astype(f32)
        for s in range(1, n):
            g = g + p_ref[s].astype(f32)
        o_ref[...] = g

    return pl.pallas_call(body, name=name, grid=(r // tr,), in_specs=[pl.BlockSpec((n, tr, c), lambda i: (0, i, 0))],
                          out_specs=pl.BlockSpec((tr, c), lambda i: (i, 0)),
                          out_shape=jax.ShapeDtypeStruct((r, c), f32), compiler_params=_cp(1))(parts)


class Rider:
    def __init__(self, ops):
        self.ops = list(ops or [])
        self.n = len(self.ops)

    def arrays(self):
        return [a for a, _ in self.ops]

    def specs(self):
        return [pl.BlockSpec(memory_space=pl.ANY)] * self.n

    def out_shapes(self):
        return [jax.ShapeDtypeStruct((N_DEV,) + tuple(a.shape) if mode == "gather" else tuple(a.shape), a.dtype)
                for a, mode in self.ops]

    def scratch(self):
        if not self.n:
            return []
        return [pltpu.SemaphoreType.DMA((self.n, N_DEV - 1)), pltpu.SemaphoreType.DMA((self.n, N_DEV - 1)),
                pltpu.SemaphoreType.DMA((self.n,))]

    def _copies(self, ins, outs, sems):
        send_sems, recv_sems, loc_sems = sems
        x, y, c = lax.axis_index("x"), lax.axis_index("y"), lax.axis_index("c")
        me = 4 * x + 2 * y + c
        copies = []
        for k, (_, mode) in enumerate(self.ops):
            src_me = ins[k] if mode == "gather" else ins[k].at[me]
            copies.append(pltpu.make_async_copy(src_me, outs[k].at[me], loc_sems.at[k]))
            for d in range(1, N_DEV):
                px = 1 - x if (d >> 2) & 1 else x
                py = 1 - y if (d >> 1) & 1 else y
                pc = 1 - c if d & 1 else c
                src = ins[k] if mode == "gather" else ins[k].at[4 * px + 2 * py + pc]
                copies.append(pltpu.make_async_remote_copy(
                    src_ref=src, dst_ref=outs[k].at[me], send_sem=send_sems.at[k, d - 1], recv_sem=recv_sems.at[k, d - 1],
                    device_id=(px, py, pc), device_id_type=pl.DeviceIdType.MESH))
        return copies

    def start(self, grid, ins, outs, sems):
        if self.n:
            @pl.when(functools.reduce(jnp.logical_and, [pl.program_id(i) == 0 for i in range(len(grid))]))
            def _():
                for cp in self._copies(ins, outs, sems):
                    cp.start()

    def wait(self, grid, ins, outs, sems):
        if self.n:
            @pl.when(functools.reduce(jnp.logical_and, [pl.program_id(i) == g - 1 for i, g in enumerate(grid)]))
            def _():
                for cp in self._copies(ins, outs, sems):
                    cp.wait()


def exchange(name, ops):
    rider = Rider(ops)
    n = rider.n

    def body(*refs):
        copies = rider._copies(refs[:n], refs[n:2 * n], refs[2 * n:])
        for cp in copies:
            cp.start()
        for cp in copies:
            cp.wait()

    return pl.pallas_call(body, name=name, in_specs=rider.specs(), out_specs=rider.specs(), out_shape=rider.out_shapes(),
                          scratch_shapes=rider.scratch())(*rider.arrays())


def _lanes(v, width=LANES):
    v = v.reshape(1, -1)
    return jnp.pad(v, ((0, 0), (0, width - v.shape[1])))


def win_to_padded(w):
    return jnp.concatenate([w[:, :1024], w[:, 2576:3600], w[:, 3600:4624], w[:, 1024:2560], w[:, 2560:2576],
                            jnp.zeros((w.shape[0], PROJ_W - IN_COLS), w.dtype)], axis=1)


def win_from_padded(wp):
    return jnp.concatenate([wp[:, 0:1024], wp[:, 3072:4608], wp[:, 4608:4624], wp[:, 1024:2048], wp[:, 2048:3072]], axis=1)


def headwise_dense(w):
    nb, o, i = w.shape
    rows = jnp.tile(w.transpose(0, 2, 1).reshape(nb * i, o), (1, nb))
    same = (jnp.arange(nb * i)[:, None] // i) == (jnp.arange(nb * o)[None, :] // o)
    return jnp.where(same, rows, 0.0)


def diag_blocks(name, dd, blk, tm=256):
    n = dd.shape[0]

    def body(d_ref, o_ref):
        rows = _iota((tm, n), 0) + pl.program_id(0) * tm
        masked = jnp.where(rows // blk == _iota((tm, n), 1) // blk, d_ref[...], 0.0)
        sel = (_iota((n, LANES), 0) % blk == _iota((n, LANES), 1)).astype(f32)
        o_ref[...] = hdot(masked, sel)

    return pl.pallas_call(body, name=name, grid=(n // tm,), in_specs=[pl.BlockSpec((tm, n), lambda i: (i, 0))],
                          out_specs=pl.BlockSpec((tm, LANES), lambda i: (i, 0)),
                          out_shape=jax.ShapeDtypeStruct((n, LANES), f32), compiler_params=_cp(1))(dd)


def headwise_from_dense(name, dd, o=4, i=4):
    nb = dd.shape[0] // i
    return diag_blocks(name, dd, i)[:, :o].reshape(nb, i, o).transpose(0, 2, 1)


def s5_tables(a_re, a_im, log_step, b_re, b_im, c_re, c_im):
    step = jnp.exp(log_step)[:, None]
    j = jnp.arange(S5_SUB, dtype=f32)[:, None, None]
    expo = jnp.concatenate([j + 1.0, j, -j], axis=0)
    mag = jnp.exp(expo * (a_re * step))
    pw_re, pw_im = mag * jnp.cos(expo * (a_im * step)), mag * jnp.sin(expo * (a_im * step))
    lam_re, lam_im = pw_re[0], pw_im[0]
    den = a_re * a_re + a_im * a_im
    coef_re = ((lam_re - 1.0) * a_re + lam_im * a_im) / den
    coef_im = (lam_im * a_re - (lam_re - 1.0) * a_im) / den
    bb_re = coef_re[..., None] * b_re - coef_im[..., None] * b_im
    bb_im = coef_re[..., None] * b_im + coef_im[..., None] * b_re
    gl = S5_GROUPS // S5_CB
    eye = jnp.eye(gl, dtype=f32)

    def blk_b(t):
        t4 = t.transpose(0, 2, 1).reshape(S5_CB, gl, S5_GROUP, S5_STATE)
        return jnp.einsum("kgcn,gh->kgchn", t4, eye).reshape(S5_CB, gl * S5_GROUP, gl * S5_STATE)

    def blk_c(t):
        t4 = t.reshape(S5_CB, gl, S5_GROUP, S5_STATE)
        return jnp.einsum("kgcn,gh->kgnhc", t4, eye).reshape(S5_CB, gl * S5_STATE, gl * S5_GROUP)

    def blk_p(t):
        return t.reshape(t.shape[0], S5_CB, gl * S5_STATE).transpose(1, 0, 2)

    bb = jnp.concatenate([blk_b(bb_re), blk_b(bb_im)], axis=2)
    cc = jnp.concatenate([blk_c(c_re), -blk_c(c_im)], axis=1)
    pw = jnp.concatenate([blk_p(pw_re), blk_p(pw_im)], axis=2)
    return bb, cc, pw


def ffn_step_bwd(dy, x, nw, wts, saved, ride_act=None, ride_w=None):
    h, g, u = saved
    dx, dnw, dg, du, a, dyh, *got_act = ffn_bwd_act(dy, x, nw, g, u, *wts, ride=ride_act)
    dwg, dwu, dwd, *got_w = ffn_bwd_w(h, dyh, dg, du, a, ride=ride_w)
    return dx, dnw, (dwg, dwu, dwd), got_act, got_w


def hybrid_fwd(x1, p, seq, ride_in, ride_ssd, ride_ml):
    u = rms_fwd(x1, p["mix_norm"])
    proj, *got_in = matmul("hy_in", u, p["win"], ride=ride_in)
    xbc = conv_fwd("ssd_conv", proj, OFF_XBC, p["ssd_conv_w"], p["ssd_conv_b"], seq)
    yraw, hsave, got_ssd = ssd_fwd(xbc, proj, p["a_log"], p["dt_bias"], p["ssd_d"], seq, ride=ride_ssd)
    yssd = rowwise_fwd("ssd_epi", f_ssd_epi, [yraw, proj], [(D_MODEL, 0), (D_MODEL, OFF_Z // D_MODEL)], [p["ssd_norm_w"]], D_MODEL)
    xc = conv_fwd("ml_conv", proj, OFF_MX, p["ml_conv_w"], p["ml_conv_b"], seq)
    q = matmul("hw_q", xc, p["wq"])
    k = matmul("hw_k", xc, p["wk"])
    v = matmul("hw_v", proj, p["wv"], a_off=OFF_MX, a_width=D_MODEL)
    g1 = matmul("gate_q", q, p["wif_q"])
    g2 = matmul("gate_k", k, p["wif_k"])
    g3 = matmul("gate_v", v, p["wif_v"])
    hm, mlsave, got_ml = ml_fwd(q, k, v, g1, g2, g3, p["b_if"], seq, ride=ride_ml)
    yml = rowwise_fwd("ml_epi", f_ml_epi, [hm, xc, proj], [(D_MODEL, 0), (D_MODEL, 0), (D_MODEL, OFF_MZ // D_MODEL)],
                      [p["ml_norm_w"], p["ml_skip"]], D_MODEL)
    t = matmul("hy_out1", yssd, p["wo1"], add=x1)
    x2 = matmul("hy_out2", yml, p["wo2"], add=t)
    return x2, (u, proj, xbc, yraw, hsave, yssd, xc, q, k, v, g1, g2, g3, hm, mlsave, yml), got_in, got_ssd, got_ml


def hybrid_bwd(dx2, x1, p, saved, seq, ride_ml):
    u, proj, xbc, yraw, hsave, yssd, xc, q, k, v, g1, g2, g3, hm, mlsave, yml = saved
    gr = {}
    dyssd = matmul("d_yssd", dx2, p["wo1"], cb=1)
    dyml = matmul("d_yml", dx2, p["wo2"], cb=1)
    gr["wo"] = jnp.concatenate([matmul("dw_o1", yssd, dx2, ca=0), matmul("dw_o2", yml, dx2, ca=0)], axis=0)
    d_hm, d_xc, d_mz, gr["ml_norm_w"], gr["ml_skip"] = rowwise_bwd(
        "ml_epi_bwd", f_ml_epi, [hm, xc, proj], [(D_MODEL, 0), (D_MODEL, 0), (D_MODEL, OFF_MZ // D_MODEL)],
        [p["ml_norm_w"], p["ml_skip"]], dyml)
    dq, dk, dv, dgt, gr["b_if"], *got_ml = ml_bwd(d_hm, q, k, v, g1, g2, g3, p["b_if"], mlsave, seq, ride=ride_ml)
    dq = matmul("dq_gate", dgt, p["wif_q"], cb=1, add=dq)
    dk = matmul("dk_gate", dgt, p["wif_k"], cb=1, add=dk)
    dv = matmul("dv_gate", dgt, p["wif_v"], cb=1, add=dv)
    gr["wif"] = jnp.concatenate([matmul("dw_if_q", q, dgt, ca=0), matmul("dw_if_k", k, dgt, ca=0),
                                 matmul("dw_if_v", v, dgt, ca=0)], axis=0)
    d_xc = matmul("dxc_q", dq, p["wq"], cb=1, add=d_xc)
    d_xc = matmul("dxc_k", dk, p["wk"], cb=1, add=d_xc)
    gr["wq"] = matmul("dw_q", xc, dq, ca=0)
    gr["wk"] = matmul("dw_k", xc, dk, ca=0)
    gr["wv"] = matmul("dw_v", proj, dv, ca=0, a_off=OFF_MX, a_width=D_MODEL)
    d_mx, gr["ml_conv_w"], gr["ml_conv_b"] = conv_bwd("ml_conv_bwd", d_xc, proj, OFF_MX, p["ml_conv_w"], p["ml_conv_b"], seq)
    d_mx = matmul("dmx_v", dv, p["wv"], cb=1, add=d_mx)
    d_yraw, d_z, gr["ssd_norm_w"] = rowwise_bwd("ssd_epi_bwd", f_ssd_epi, [yraw, proj],
                                                [(D_MODEL, 0), (D_MODEL, OFF_Z // D_MODEL)], [p["ssd_norm_w"]], dyssd)
    d_xs, d_b, d_c, d_dt, gr["a_log"], gr["dt_bias"], gr["ssd_d"] = ssd_bwd(
        d_yraw, xbc, proj, p["a_log"], p["dt_bias"], p["ssd_d"], hsave, seq)
    d_xbc, gr["ssd_conv_w"], gr["ssd_conv_b"] = conv_bwd("ssd_conv_bwd", jnp.concatenate([d_xs, d_b, d_c], axis=1), proj, OFF_XBC,
                                                         p["ssd_conv_w"], p["ssd_conv_b"], seq)
    dproj = jnp.concatenate([d_z, d_mx, d_mz, d_xbc, d_dt, jnp.zeros((d_dt.shape[0], PROJ_W - OFF_DT - LANES), f32)], axis=1)
    gr["win"] = matmul("dw_in", u, dproj, ca=0)
    du = matmul("d_u", dproj, p["win"], cb=1)
    dx1, gr["mix_norm"] = rms_bwd([du], x1, p["mix_norm"], dx2)
    return dx1, gr, got_ml


def s5_layer_fwd(x4, p, seq):
    u = rms_fwd(x4, p["mix_norm"])
    ys, carries = s5_fwd(u, p["bb"], p["cc"], p["pw"], seq)
    gg = rowwise_fwd("s5_post", f_s5_post, [ys, u], [(D_MODEL, 0), (D_MODEL, 0)], [p["s5_d"]], D_MODEL)
    pab = matmul("s5_ab", gg, p["wab"])
    x5 = rowwise_fwd("s5_glu", f_glu_res, [pab, x4], [(2 * D_MODEL, 0), (D_MODEL, 0)], [p["b_a"], p["b_b"]], D_MODEL)
    return x5, (u, ys, carries, gg, pab)


def s5_layer_bwd(dx5, x4, p, saved, seq):
    u, ys, carries, gg, pab = saved
    gr = {}
    dpab, gr["b_a"], gr["b_b"] = rowwise_bwd("s5_glu_bwd", f_glu, [pab], [(2 * D_MODEL, 0)], [p["b_a"], p["b_b"]], dx5)
    dgg = matmul("d_gg", dpab, p["wab"], cb=1)
    gr["wab"] = matmul("dw_ab", gg, dpab, ca=0)
    dys, du_a, gr["s5_d"] = rowwise_bwd("s5_post_bwd", f_s5_post, [ys, u], [(D_MODEL, 0), (D_MODEL, 0)], [p["s5_d"]], dgg)
    du_b, gr["bb"], gr["cc"], gr["pw"] = s5_bwd(dys, u, p["bb"], p["cc"], p["pw"], carries, seq)
    dx4, gr["mix_norm"] = rms_bwd([du_a, du_b], x4, p["mix_norm"], dx5)
    return dx4, gr


BIG = ["ffn1_w_gate", "ffn1_w_up", "ffn1_w_down", "ffn2_w_gate", "ffn2_w_up", "ffn2_w_down", "hy_w_in", "hy_w_out", "s5_w_a", "s5_w_b"]
SMALL_SHARDED = {"ssd_conv_w": 2, "ml_conv_w": 2, "ml_w_q": 1, "ml_w_k": 1, "ml_w_v": 1, "ml_w_if": 1, "s5_d": 1, "s5_b_a": 1, "s5_b_b": 1}
WEIGHTS = ["ffn1_norm", "ffn1_w_gate", "ffn1_w_up", "ffn1_w_down", "mix_norm", "ffn2_norm", "ffn2_w_gate", "ffn2_w_up", "ffn2_w_down",
           "hy_w_in", "ssd_conv_w", "ssd_conv_b", "ssd_dt_bias", "ssd_a_log", "ssd_d", "ssd_norm_w", "ml_conv_w", "ml_conv_b",
           "ml_w_q", "ml_w_k", "ml_w_v", "ml_w_if", "ml_b_if", "ml_norm_w", "ml_skip", "hy_w_out", "s5_a_re", "s5_a_im",
           "s5_log_step", "s5_b_re", "s5_b_im", "s5_c_re", "s5_c_im", "s5_d", "s5_w_a", "s5_b_a", "s5_w_b", "s5_b_b", "final_norm"]
S5_PARAMS = ["s5_a_re", "s5_a_im", "s5_log_step", "s5_b_re", "s5_b_im", "s5_c_re", "s5_c_im"]
SMALL_S5 = S5_PARAMS + ["s5_d", "s5_b_a", "s5_b_b"]
SMALL_REST = [n for n in WEIGHTS if n not in BIG and n not in SMALL_S5]
SMALL = SMALL_REST + SMALL_S5


def _unshard(g, axis):
    return jnp.concatenate([g[i] for i in range(N_DEV)], axis=axis)


def assemble_hybrid(gw, rep):
    padn = lambda w: jnp.pad(w, ((0, 0), (0, LANES - w.shape[1]))).astype(bf16)
    wif = _unshard(gw["ml_w_if"], 1)[0]
    wo = _unshard(gw["hy_w_out"], 1)[0].astype(bf16)
    dense = lambda n: headwise_dense(_unshard(gw[n], 1)[0].astype(f32)).astype(bf16)
    w0 = dict(mix_norm=rep["mix_norm"][0:1],
              win=win_to_padded(_unshard(gw["hy_w_in"], 2)[0]).astype(bf16),
              ssd_conv_w=_unshard(gw["ssd_conv_w"], 2)[0], ssd_conv_b=rep["ssd_conv_b"],
              a_log=_lanes(rep["ssd_a_log"]), dt_bias=_lanes(rep["ssd_dt_bias"]), ssd_d=_lanes(rep["ssd_d"]),
              ssd_norm_w=rep["ssd_norm_w"], ml_conv_w=_unshard(gw["ml_conv_w"], 2)[0], ml_conv_b=rep["ml_conv_b"],
              wq=dense("ml_w_q"), wk=dense("ml_w_k"), wv=dense("ml_w_v"),
              wif_q=padn(wif[0:1024]), wif_k=padn(wif[1024:2048]), wif_v=padn(wif[2048:3072]),
              b_if=_lanes(rep["ml_b_if"]), ml_norm_w=rep["ml_norm_w"], ml_skip=rep["ml_skip"],
              wo1=wo[:D_MODEL], wo2=wo[D_MODEL:])
    return w0


def assemble_s5(gw, rep):
    bb, cc, pw = s5_tables(*[rep[n][0] for n in S5_PARAMS])
    wab = jnp.concatenate([_unshard(gw["s5_w_a"], 1)[0], _unshard(gw["s5_w_b"], 1)[0]], axis=1).astype(bf16)
    return dict(mix_norm=rep["mix_norm"][1:2], bb=bb, cc=cc, pw=pw,
                s5_d=_unshard(gw["s5_d"], 1), wab=wab, b_a=_unshard(gw["s5_b_a"], 1), b_b=_unshard(gw["s5_b_b"], 1))


def _shards(full, axis):
    return jnp.stack(jnp.split(full, N_DEV, axis=axis), axis=0)


def small_grads(g_norms, g_hy, g_s5, d_final, rep):
    small = dict(g_norms)
    small["mix_norm"] = jnp.concatenate([g_hy["mix_norm"], g_s5["mix_norm"]], axis=0)
    small["ssd_conv_w"] = g_hy["ssd_conv_w"][None]
    small["ssd_conv_b"] = g_hy["ssd_conv_b"]
    small["ssd_dt_bias"] = g_hy["dt_bias"][:, :SSD_HEADS]
    small["ssd_a_log"] = g_hy["a_log"][:, :SSD_HEADS]
    small["ssd_d"] = g_hy["ssd_d"][:, :SSD_HEADS]
    small["ssd_norm_w"] = g_hy["ssd_norm_w"]
    small["ml_conv_w"] = g_hy["ml_conv_w"][None]
    small["ml_conv_b"] = g_hy["ml_conv_b"]
    for nm, key in (("ml_w_q", "wq"), ("ml_w_k", "wk"), ("ml_w_v", "wv")):
        small[nm] = headwise_from_dense("diag_" + key, g_hy[key])[None]
    small["ml_w_if"] = g_hy["wif"][None, :, :2 * ML_HEADS]
    small["ml_b_if"] = g_hy["b_if"][:, :2 * ML_HEADS]
    small["ml_norm_w"] = g_hy["ml_norm_w"]
    small["ml_skip"] = g_hy["ml_skip"]
    small["final_norm"] = d_final.reshape(-1)
    return small


def s5_small_grads(g_s5, rep):
    small = {}
    _, tvjp = jax.vjp(s5_tables, *[rep[n][0] for n in S5_PARAMS])
    for n, g in zip(S5_PARAMS, tvjp((g_s5["bb"], g_s5["cc"], g_s5["pw"]))):
        small[n] = g[None]
    small["s5_d"] = g_s5["s5_d"]
    small["s5_b_a"] = g_s5["b_a"]
    small["s5_b_b"] = g_s5["b_b"]
    return small


ROW = 1024
F32_ROWS = 8


def _piece_rows(size):
    return -(-size // (ROW * F32_ROWS)) * F32_ROWS


def _pack(arrays):
    pieces = []
    for a in arrays:
        flat = a.astype(f32).reshape(-1)
        pieces.append(jnp.pad(flat, (0, _piece_rows(a.size) * ROW - a.size)).reshape(-1, ROW))
    return jnp.concatenate(pieces, axis=0)


def _unpack(buf, shapes):
    out, r0 = [], 0
    lead = buf.shape[:-2]
    for shp in shapes:
        size = math.prod(shp)
        r = _piece_rows(size)
        out.append(buf[..., r0:r0 + r, :].reshape(lead + (-1,))[..., :size].reshape(lead + tuple(shp)))
        r0 += r
    return out


def _tile_rows(r):
    for t in (512, 256, 128, 64, 32, 16, 8):
        if r % t == 0:
            return t
    return r


def _flat2d(a):
    return a.reshape(-1, a.shape[-1])


def kernel(x, ffn1_norm, ffn1_w_gate, ffn1_w_up, ffn1_w_down, mix_norm, ffn2_norm, ffn2_w_gate, ffn2_w_up, ffn2_w_down, hy_w_in, ssd_conv_w, ssd_conv_b, ssd_dt_bias, ssd_a_log, ssd_d, ssd_norm_w, ml_conv_w, ml_conv_b, ml_w_q, ml_w_k, ml_w_v, ml_w_if, ml_b_if, ml_norm_w, ml_skip, hy_w_out, s5_a_re, s5_a_im, s5_log_step, s5_b_re, s5_b_im, s5_c_re, s5_c_im, s5_d, s5_w_a, s5_b_a, s5_w_b, s5_b_b, final_norm, loss_target, m_ffn1_norm, m_ffn1_w_gate, m_ffn1_w_up, m_ffn1_w_down, m_mix_norm, m_ffn2_norm, m_ffn2_w_gate, m_ffn2_w_up, m_ffn2_w_down, m_hy_w_in, m_ssd_conv_w, m_ssd_conv_b, m_ssd_dt_bias, m_ssd_a_log, m_ssd_d, m_ssd_norm_w, m_ml_conv_w, m_ml_conv_b, m_ml_w_q, m_ml_w_k, m_ml_w_v, m_ml_w_if, m_ml_b_if, m_ml_norm_w, m_ml_skip, m_hy_w_out, m_s5_a_re, m_s5_a_im, m_s5_log_step, m_s5_b_re, m_s5_b_im, m_s5_c_re, m_s5_c_im, m_s5_d, m_s5_w_a, m_s5_b_a, m_s5_w_b, m_s5_b_b, m_final_norm, v_ffn1_norm, v_ffn1_w_gate, v_ffn1_w_up, v_ffn1_w_down, v_mix_norm, v_ffn2_norm, v_ffn2_w_gate, v_ffn2_w_up, v_ffn2_w_down, v_hy_w_in, v_ssd_conv_w, v_ssd_conv_b, v_ssd_dt_bias, v_ssd_a_log, v_ssd_d, v_ssd_norm_w, v_ml_conv_w, v_ml_conv_b, v_ml_w_q, v_ml_w_k, v_ml_w_v, v_ml_w_if, v_ml_b_if, v_ml_norm_w, v_ml_skip, v_hy_w_out, v_s5_a_re, v_s5_a_im, v_s5_log_step, v_s5_b_re, v_s5_b_im, v_s5_c_re, v_s5_c_im, v_s5_d, v_s5_w_a, v_s5_b_a, v_s5_w_b, v_s5_b_b, v_final_norm):
    given = dict(locals())
    w = {n: given[n] for n in WEIGHTS}
    mom = {n: given["m_" + n] for n in WEIGHTS}
    var = {n: given["v_" + n] for n in WEIGHTS}
    bl, seq, d = x.shape
    me = 4 * lax.axis_index("x") + 2 * lax.axis_index("y") + lax.axis_index("c")

    x0, tgt = x.reshape(bl * seq, d), loss_target.reshape(bl * seq, d)
    rep = {n: w[n] for n in WEIGHTS if n not in BIG and n not in SMALL_SHARDED}
    ffn_w = ("_w_gate", "_w_up", "_w_down")

    def ffn_gather(pre, l):
        return [(w[pre + s][l:l + 1].astype(bf16), "gather") for s in ffn_w]

    def scatter(parts):
        return [(p, "scatter") for p in parts]

    wf10 = tuple(exchange("gather_ffn1_l0", ffn_gather("ffn1", 0)))
    mixer_ops = [(w[n].astype(bf16), "gather") for n in ("hy_w_in", "hy_w_out")]
    mixer_ops.append((_pack([w[n] for n in SMALL_SHARDED]), "gather"))
    x1, *rest = ffn_fwd(x0, ffn1_norm[0:1], *wf10, ride=mixer_ops)
    sv10, got = rest[:3], rest[3:]
    gw = dict(zip(("hy_w_in", "hy_w_out"), got[:2]))
    gw.update(zip(SMALL_SHARDED, _unpack(got[2], [w[n].shape for n in SMALL_SHARDED])))
    w0 = assemble_hybrid(gw, rep)
    x2, sv_h, wf20, got, wf11 = hybrid_fwd(x1, w0, seq, ride_in=ffn_gather("ffn2", 0),
                                           ride_ssd=[(w[n].astype(bf16), "gather") for n in ("s5_w_a", "s5_w_b")],
                                           ride_ml=ffn_gather("ffn1", 1))
    gw.update(zip(("s5_w_a", "s5_w_b"), got))
    w1 = assemble_s5(gw, rep)
    x3, *rest = ffn_fwd(x2, ffn2_norm[0:1], *wf20, ride=ffn_gather("ffn2", 1))
    sv20, wf21 = rest[:3], tuple(rest[3:])
    x4, *sv11 = ffn_fwd(x3, ffn1_norm[1:2], *wf11)
    x5, sv_s = s5_layer_fwd(x4, w1, seq)
    x6, *sv21 = ffn_fwd(x5, ffn2_norm[1:2], *wf21)
    loss, dx6, d_final = loss_head(x6, final_norm.reshape(1, d), tgt)

    dx5, dn21, dw21, _, _ = ffn_step_bwd(dx6, x5, ffn2_norm[1:2], wf21, sv21)
    dx4, g_s5 = s5_layer_bwd(dx5, x4, w1, sv_s, seq)
    dwab = g_s5.pop("wab")
    s5_ops = scatter([_shards(dwab[None, :, :D_MODEL], 1).astype(bf16), _shards(dwab[None, :, D_MODEL:], 1).astype(bf16)])
    dx3, dn11, dw11, p21, p_s5 = ffn_step_bwd(dx4, x3, ffn1_norm[1:2], wf11, sv11, ride_act=scatter(dw21), ride_w=s5_ops)
    small = s5_small_grads(g_s5, rep)
    dx2, dn20, dw20, p11, (parts_s5,) = ffn_step_bwd(dx3, x2, ffn2_norm[0:1], wf20, sv20, ride_act=scatter(dw11),
                                                      ride_w=[(_pack([small[n] for n in SMALL_S5]), "gather")])
    dx1, g_hy, p20 = hybrid_bwd(dx2, x1, w0, sv_h, seq, ride_ml=scatter(dw20))
    hy_ops = scatter([_shards(win_from_padded(g_hy.pop("win"))[None], 2).astype(bf16), _shards(g_hy.pop("wo")[None], 1).astype(bf16)])
    h10, g10, u10 = sv10
    dx0, dn10, dg, du, a, dyh, *p_hy = ffn_bwd_act(dx1, x0, ffn1_norm[0:1], g10, u10, *wf10, ride=hy_ops)
    g_norms = {"ffn1_norm": jnp.concatenate([dn10, dn11], axis=0), "ffn2_norm": jnp.concatenate([dn20, dn21], axis=0)}
    small.update(small_grads(g_norms, g_hy, g_s5, d_final, rep))
    *dw10, parts_rest = ffn_bwd_w(h10, dyh, dg, du, a, ride=[(_pack([small[n] for n in SMALL_REST]), "gather")])
    p10 = exchange("reduce_tail", scatter(dw10))
    small_parts = jnp.concatenate([parts_rest, parts_s5], axis=1)
    small_sum = sum_parts("sum_small", small_parts, tr=_tile_rows(small_parts.shape[1]))

    out_g, out_d, out_m, out_v = {}, {}, {}, {}
    ffn_parts = {"ffn1": (p10, p11), "ffn2": (p20, p21)}
    for pre in ("ffn1", "ffn2"):
        for k, s in enumerate(ffn_w):
            n = pre + s
            r, c = w[n].shape[1:]
            res = None
            for l in (1, 0):
                res = adamw_layer("adamw_" + n, ffn_parts[pre][l][k].reshape(N_DEV, r, c), w[n], mom[n], var[n], l, res,
                                  tr=_tile_rows(r))
            out_g[n], out_d[n], out_m[n], out_v[n] = res
    for n, parts in zip(("hy_w_in", "hy_w_out", "s5_w_a", "s5_w_b"), tuple(p_hy) + tuple(p_s5)):
        shp = w[n].shape
        w2 = _flat2d(w[n])
        res = adamw("adamw_" + n, parts.reshape((N_DEV,) + w2.shape), w2, _flat2d(mom[n]), _flat2d(var[n]),
                    tr=_tile_rows(w2.shape[0]))
        out_g[n], out_d[n], out_m[n], out_v[n] = [a.reshape(shp) for a in res]
    g_small = {}
    for n, full in zip(SMALL, _unpack(small_sum, [small[n].shape for n in SMALL])):
        if n in SMALL_SHARDED:
            ax = SMALL_SHARDED[n]
            full = lax.dynamic_slice_in_dim(full, me * w[n].shape[ax], w[n].shape[ax], axis=ax)
        g_small[n] = full
    packs = [_pack([t[n] for n in SMALL]) for t in (g_small, w, mom, var)]
    res = adamw("adamw_small", packs[0][None], packs[1], packs[2], packs[3], tr=_tile_rows(packs[0].shape[0]))
    for dst, a in zip((out_g, out_d, out_m, out_v), res):
        dst.update(zip(SMALL, _unpack(a, [w[n].shape for n in SMALL])))

    total = lax.psum(loss[0, 0], ("x", "y", "c"))
    return (total, dx0.reshape(bl, seq, d), *[out_g[n] for n in WEIGHTS], *[out_d[n] for n in WEIGHTS],
            *[out_m[n] for n in WEIGHTS], *[out_v[n] for n in WEIGHTS])
```

```python
import functools
import math

import jax
import jax.numpy as jnp
from jax import lax
from jax.experimental import pallas as pl
from jax.experimental.pallas import tpu as pltpu

f32 = jnp.float32
bf16 = jnp.bfloat16

N_DEV = 8
D_MODEL = 1024
D_FF = 2816
EPS = 1e-6
FFN_RES = 0.5
CONV_W = 4
SSD_HEADS = 16
SSD_HEAD_DIM = 64
SSD_GROUPS = 2
SSD_STATE = 128
SSD_HG = SSD_HEADS // SSD_GROUPS
SSD_GW = SSD_HG * SSD_HEAD_DIM
CHUNK = 128
ML_HEADS = 4
ML_HD = 256
S5_GROUP = 16
S5_GROUPS = 64
S5_STATE = 64
S5_CB = 8
S5_CH = (S5_GROUPS // S5_CB) * S5_STATE
S5_TL = 256
S5_SUB = 16
LANES = 128
IN_COLS = 4624
PROJ_W = 4864
OFF_Z, OFF_MX, OFF_MZ, OFF_XBC, OFF_DT = 0, 1024, 2048, 3072, 4608
ADAM_LR, ADAM_B1, ADAM_B2, ADAM_EPS, ADAM_WD, ADAM_STEP = 0.001, 0.9, 0.999, 1e-08, 0.01, 10
NEG = -1e30
VMEM_LIMIT = 56 * 1024 * 1024
HI = lax.Precision.HIGHEST


def _cp(n):
    return pltpu.CompilerParams(dimension_semantics=("arbitrary",) * n, vmem_limit_bytes=VMEM_LIMIT)


def _dg(a, b, ca, cb):
    return lax.dot_general(a.astype(bf16), b.astype(bf16), (((ca,), (cb,)), ((), ())), preferred_element_type=f32)


@functools.partial(jax.custom_vjp, nondiff_argnums=(2, 3))
def bdot(a, b, ca, cb):
    return _dg(a, b, ca, cb)


def _bdot_fwd(a, b, ca, cb):
    return _dg(a, b, ca, cb), (a, b)


def _bdot_bwd(ca, cb, res, ct):
    a, b = res
    da = _dg(ct, b, 1, 1 - cb) if ca == 1 else _dg(b, ct, 1 - cb, 1)
    db = _dg(a, ct, 1 - ca, 0) if cb == 0 else _dg(ct, a, 0, 1 - ca)
    return da, db


bdot.defvjp(_bdot_fwd, _bdot_bwd)


def hdot(a, b):
    return jnp.dot(a, b, precision=HI, preferred_element_type=f32)


def _iota(shape, dim):
    return lax.broadcasted_iota(jnp.int32, shape, dim)


def _tri(n):
    return (_iota((n, n), 0) >= _iota((n, n), 1))


@functools.partial(jax.custom_vjp, nondiff_argnums=(1,))
def tshift(x, k):
    return jnp.where(_iota(x.shape, 0) >= k, pltpu.roll(x, k, 0), 0.0)


def _tshift_fwd(x, k):
    return tshift(x, k), None


def _tshift_bwd(k, _, ct):
    n = ct.shape[0]
    return (jnp.where(_iota(ct.shape, 0) < n - k, pltpu.roll(ct, n - k, 0), 0.0),)


tshift.defvjp(_tshift_fwd, _tshift_bwd)


def _lane_pick(a, idx):
    return jnp.sum(jnp.where(_iota(a.shape, 1) == idx, a, 0.0), axis=1, keepdims=True)


def _row_pick(a, idx):
    return jnp.sum(jnp.where(_iota(a.shape, 0) == idx, a, 0.0), axis=0, keepdims=True)


def _silu(x):
    return x * jax.nn.sigmoid(x)


def map_fwd(name, f, grid, ins, in_specs, out_shapes, out_specs):
    n_in = len(ins)

    def body(*refs):
        pids = tuple(pl.program_id(i) for i in range(len(grid)))
        outs = f(pids, *[r[...] for r in refs[:n_in]])
        for r, o in zip(refs[n_in:], outs):
            r[...] = o.astype(r.dtype)

    return pl.pallas_call(body, name=name, grid=grid, in_specs=in_specs, out_specs=out_specs,
                          out_shape=out_shapes, compiler_params=_cp(len(grid)))(*ins)


def scan_fwd(name, f, grid, slot_axis, ins, in_specs, out_shapes, out_specs, state_shapes, state_init, save_shapes, save_specs,
             ride=None):
    n_in, n_out, n_st = len(ins), len(out_shapes), len(state_shapes)
    n_slots = grid[slot_axis]
    cax = len(grid) - 1 if slot_axis != len(grid) - 1 else len(grid) - 2
    rider = Rider(ride)
    nr = rider.n

    def body(*refs):
        pids = tuple(pl.program_id(i) for i in range(len(grid)))
        in_refs, r_ins = refs[:n_in], refs[n_in:n_in + nr]
        o0 = n_in + nr
        out_refs, save_refs = refs[o0:o0 + n_out], refs[o0 + n_out:o0 + n_out + n_st]
        r_outs = refs[o0 + n_out + n_st:o0 + n_out + n_st + nr]
        st_refs = refs[o0 + n_out + n_st + nr:o0 + n_out + 2 * n_st + nr]
        sems = refs[o0 + n_out + 2 * n_st + nr:]
        rider.start(grid, r_ins, r_outs, sems)
        slot = pids[slot_axis]

        @pl.when(pids[cax] == 0)
        def _():
            for s, init in zip(st_refs, state_init):
                s[slot] = jnp.full(s.shape[1:], init, f32)

        states = tuple(s[slot] for s in st_refs)
        for sv, st in zip(save_refs, states):
            sv[...] = st.reshape(sv.shape)
        outs, new = f(pids, states, *[r[...] for r in in_refs])
        for r, o in zip(out_refs, outs):
            r[...] = o.astype(r.dtype)
        for s, v in zip(st_refs, new):
            s[slot] = v
        rider.wait(grid, r_ins, r_outs, sems)

    scratch = [pltpu.VMEM((n_slots,) + tuple(s), f32) for s in state_shapes]
    return pl.pallas_call(body, name=name, grid=grid, in_specs=list(in_specs) + rider.specs(),
                          out_specs=list(out_specs) + list(save_specs) + rider.specs(),
                          out_shape=list(out_shapes) + list(save_shapes) + rider.out_shapes(),
                          scratch_shapes=scratch + rider.scratch(), compiler_params=_cp(len(grid)))(*ins, *rider.arrays())


def scan_bwd(name, f, grid, slot_axis, ins, in_specs, saves, save_specs, cts, ct_specs, state_shapes, wrt, acc_first):
    n_in, n_st, n_ct = len(ins), len(saves), len(cts)
    n_slots = grid[slot_axis]
    cax = len(grid) - 1 if slot_axis != len(grid) - 1 else len(grid) - 2

    def body(*refs):
        pids = tuple(pl.program_id(i) for i in range(len(grid)))
        in_refs = refs[:n_in]
        save_refs = refs[n_in:n_in + n_st]
        ct_refs = refs[n_in + n_st:n_in + n_st + n_ct]
        out_refs = refs[n_in + n_st + n_ct:n_in + n_st + n_ct + len(wrt)]
        dst_refs = refs[n_in + n_st + n_ct + len(wrt):]
        slot = pids[slot_axis]

        @pl.when(pids[cax] == 0)
        def _():
            for s in dst_refs:
                s[slot] = jnp.zeros(s.shape[1:], f32)

        vals = [r[...] for r in in_refs]
        states = tuple(sv[...].reshape(shp) for sv, shp in zip(save_refs, state_shapes))
        ctv = tuple(r[...].astype(f32) for r in ct_refs)
        dnew = tuple(s[slot] for s in dst_refs)

        def g(st, *dv):
            full = list(vals)
            for i, v in zip(wrt, dv):
                full[i] = v
            outs, new = f(pids, st, *full)
            return tuple(outs), tuple(new)

        _, vjp = jax.vjp(g, states, *[vals[i] for i in wrt])
        grads = vjp((ctv, dnew))
        for s, v in zip(dst_refs, grads[0]):
            s[slot] = v
        for i, o_ref, gr in zip(wrt, out_refs, grads[1:]):
            first = acc_first.get(i)
            if first is None:
                o_ref[...] = gr.astype(o_ref.dtype)
            else:
                @pl.when(first(pids))
                def _():
                    o_ref[...] = jnp.zeros_like(o_ref)
                o_ref[...] += gr

    out_shapes = [jax.ShapeDtypeStruct(ins[i].shape, f32) for i in wrt]
    out_specs = [in_specs[i] for i in wrt]
    scratch = [pltpu.VMEM((n_slots,) + tuple(s), f32) for s in state_shapes]
    return pl.pallas_call(body, name=name, grid=grid, in_specs=list(in_specs) + list(save_specs) + list(ct_specs),
                          out_specs=out_specs, out_shape=out_shapes, scratch_shapes=scratch,
                          compiler_params=_cp(len(grid)))(*ins, *saves, *cts)


def _fit(dim, cap):
    if dim <= cap:
        return dim
    return max(t for t in range(LANES, cap + 1, LANES) if dim % t == 0)


def _matmul_tiles(m, n, kdim, ca):
    if ca == 1:
        return _fit(m, 512), _fit(n, 2432), _fit(kdim, 2432)
    return _fit(m, 1024), _fit(n, 1280), _fit(kdim, 512)


def matmul(name, a, b, ca=1, cb=0, add=None, out_dtype=f32, a_off=0, a_width=None, ride=None, tiles=None):
    rider = Rider(ride)
    nr = rider.n
    a_width = a.shape[1] if a_width is None else a_width
    kdim = b.shape[cb]
    n = b.shape[1 - cb]
    m = a.shape[0] if ca == 1 else a_width
    tm, tn, tk = tiles or _matmul_tiles(m, n, kdim, ca)
    assert m % tm == 0 and n % tn == 0 and kdim % tk == 0
    nk = kdim // tk
    if ca == 1:
        assert a_off % tk == 0 and a_width == kdim
        koff = a_off // tk
        a_spec = pl.BlockSpec((tm, tk), lambda i, j, k: (i, k + koff))
    else:
        assert a_off % tm == 0 and a.shape[0] == kdim
        ioff = a_off // tm
        a_spec = pl.BlockSpec((tk, tm), lambda i, j, k: (k, i + ioff))
    b_spec = pl.BlockSpec((tk, tn), lambda i, j, k: (k, j)) if cb == 0 else pl.BlockSpec((tn, tk), lambda i, j, k: (j, k))
    o_spec = pl.BlockSpec((tm, tn), lambda i, j, k: (i, j))
    has_add = add is not None

    n_in = 3 if has_add else 2
    grid = (m // tm, n // tn, nk)

    def body(*refs):
        a_ref, b_ref = refs[0], refs[1]
        add_ref = refs[2] if has_add else None
        r_ins, o_ref = refs[n_in:n_in + nr], refs[n_in + nr]
        r_outs, acc, sems = refs[n_in + nr + 1:n_in + 2 * nr + 1], refs[n_in + 2 * nr + 1], refs[n_in + 2 * nr + 2:]
        rider.start(grid, r_ins, r_outs, sems)
        k = pl.program_id(2)

        @pl.when(k == 0)
        def _():
            acc[...] = add_ref[...].astype(f32) if has_add else jnp.zeros_like(acc)

        acc[...] += _dg(a_ref[...], b_ref[...], ca, cb)

        @pl.when(k == nk - 1)
        def _():
            o_ref[...] = acc[...].astype(o_ref.dtype)

        rider.wait(grid, r_ins, r_outs, sems)

    ins = [a, b] + ([add] if has_add else [])
    specs = [a_spec, b_spec] + ([o_spec] if has_add else [])
    res = pl.pallas_call(body, name=name, grid=grid, in_specs=specs + rider.specs(), out_specs=[o_spec] + rider.specs(),
                         out_shape=[jax.ShapeDtypeStruct((m, n), out_dtype)] + rider.out_shapes(),
                         scratch_shapes=[pltpu.VMEM((tm, tn), f32)] + rider.scratch(), compiler_params=_cp(3))(*ins, *rider.arrays())
    return res if nr else res[0]


def f_rms(pids, x, w):
    r = lax.rsqrt(jnp.mean(x * x, axis=-1, keepdims=True) + EPS)
    return (x * r * w,)


def _row_spec(tm, width, col=0):
    return pl.BlockSpec((tm, width), lambda i: (i, col))


def _par_spec(shape):
    return pl.BlockSpec(shape, lambda *p: (0,) * len(shape))


def rms_fwd(x, w, tm=512):
    t, d = x.shape
    return map_fwd("rms_fwd", f_rms, (t // tm,), [x, w], [_row_spec(tm, d), _par_spec((1, d))],
                   [jax.ShapeDtypeStruct((t, d), f32)], [_row_spec(tm, d)])[0]


def rms_bwd(dys, x, w, dres, tm=512):
    t, d = x.shape
    n = len(dys)

    def body(*refs):
        x_ref, w_ref, dres_ref, dx_ref, dw_ref = refs[n:]
        dy = refs[0][...]
        for r in refs[1:n]:
            dy = dy + r[...]
        _, vjp = jax.vjp(lambda xx, ww: f_rms(None, xx, ww)[0], x_ref[...], w_ref[...])
        dx, dw = vjp(dy)
        dx_ref[...] = dx + dres_ref[...]

        @pl.when(pl.program_id(0) == 0)
        def _():
            dw_ref[...] = jnp.zeros_like(dw_ref)
        dw_ref[...] += dw

    return pl.pallas_call(body, name="rms_bwd", grid=(t // tm,),
                          in_specs=[_row_spec(tm, d)] * (n + 1) + [_par_spec((1, d)), _row_spec(tm, d)],
                          out_specs=[_row_spec(tm, d), _par_spec((1, d))],
                          out_shape=[jax.ShapeDtypeStruct((t, d), f32), jax.ShapeDtypeStruct((1, d), f32)],
                          compiler_params=_cp(1))(*dys, x, w, dres)


def loss_head(x, w, tgt, tm=512):
    t, d = x.shape

    def fl(xx, ww, tt):
        y = f_rms(None, xx, ww)[0]
        return 0.5 * jnp.sum(jnp.mean(jnp.square(y - tt), axis=-1, keepdims=True), axis=0, keepdims=True)

    def body(x_ref, w_ref, t_ref, loss_ref, dx_ref, dw_ref):
        val, vjp = jax.vjp(lambda xx, ww: fl(xx, ww, t_ref[...]), x_ref[...], w_ref[...])
        dx, dw = vjp(jnp.ones((1, 1), f32))
        dx_ref[...] = dx

        @pl.when(pl.program_id(0) == 0)
        def _():
            dw_ref[...] = jnp.zeros_like(dw_ref)
            loss_ref[...] = jnp.zeros_like(loss_ref)
        dw_ref[...] += dw
        loss_ref[...] += val

    return pl.pallas_call(body, name="loss_head", grid=(t // tm,),
                          in_specs=[_row_spec(tm, d), _par_spec((1, d)), _row_spec(tm, d)],
                          out_specs=[_par_spec((1, 1)), _row_spec(tm, d), _par_spec((1, d))],
                          out_shape=[jax.ShapeDtypeStruct((1, 1), f32), jax.ShapeDtypeStruct((t, d), f32),
                                     jax.ShapeDtypeStruct((1, d), f32)],
                          compiler_params=_cp(1))(x, w, tgt)


def ffn_fwd(x, nw, wg, wu, wd, tm=1024, ride=None):
    t, d = x.shape
    ns, _, _, fs = wg.shape
    rider = Rider(ride)
    nr = rider.n
    grid = (t // tm, ns)

    def body(*refs):
        x_ref, nw_ref, wg_ref, wu_ref, wd_ref = refs[:5]
        r_ins = refs[5:5 + nr]
        xo_ref, h_ref, g_ref, u_ref = refs[5 + nr:9 + nr]
        r_outs, acc, sems = refs[9 + nr:9 + 2 * nr], refs[9 + 2 * nr], refs[10 + 2 * nr:]
        rider.start(grid, r_ins, r_outs, sems)
        j = pl.program_id(1)

        @pl.when(j == 0)
        def _():
            h_ref[...] = f_rms(None, x_ref[...], nw_ref[...])[0].astype(bf16)
            acc[...] = jnp.zeros_like(acc)

        h = h_ref[...]
        g = jnp.dot(h, wg_ref[0, 0], preferred_element_type=f32)
        u = jnp.dot(h, wu_ref[0, 0], preferred_element_type=f32)
        g_ref[0] = g
        u_ref[0] = u
        acc[...] += jnp.dot((_silu(g) * u).astype(bf16), wd_ref[0, 0], preferred_element_type=f32)

        @pl.when(j == ns - 1)
        def _():
            xo_ref[...] = x_ref[...] + FFN_RES * acc[...]

        rider.wait(grid, r_ins, r_outs, sems)

    row = pl.BlockSpec((tm, d), lambda i, j: (i, 0))
    wcol = pl.BlockSpec((1, 1, d, fs), lambda i, j: (j, 0, 0, 0))
    wrow = pl.BlockSpec((1, 1, fs, d), lambda i, j: (j, 0, 0, 0))
    act = pl.BlockSpec((1, tm, fs), lambda i, j: (j, i, 0))
    return pl.pallas_call(body, name="ffn_fwd", grid=grid,
                          in_specs=[row, pl.BlockSpec((1, d), lambda i, j: (0, 0)), wcol, wcol, wrow] + rider.specs(),
                          out_specs=[row, row, act, act] + rider.specs(),
                          out_shape=[jax.ShapeDtypeStruct((t, d), f32), jax.ShapeDtypeStruct((t, d), bf16),
                                     jax.ShapeDtypeStruct((ns, t, fs), f32), jax.ShapeDtypeStruct((ns, t, fs), f32)]
                          + rider.out_shapes(),
                          scratch_shapes=[pltpu.VMEM((tm, d), f32)] + rider.scratch(),
                          compiler_params=_cp(2))(x, nw, wg, wu, wd, *rider.arrays())


def ffn_bwd_act(dy, x, nw, g, u, wg, wu, wd, tm=512, ride=None):
    t, d = x.shape
    ns, _, _, fs = wg.shape
    rider = Rider(ride)
    nr = rider.n
    grid = (t // tm, ns)

    def body(*refs):
        dy_ref, x_ref, nw_ref, g_ref, u_ref, wg_ref, wu_ref, wd_ref = refs[:8]
        r_ins = refs[8:8 + nr]
        dx_ref, dnw_ref, dg_ref, du_ref, a_ref, dyh_ref = refs[8 + nr:14 + nr]
        r_outs, acc, sems = refs[14 + nr:14 + 2 * nr], refs[14 + 2 * nr], refs[15 + 2 * nr:]
        rider.start(grid, r_ins, r_outs, sems)
        i, j = pl.program_id(0), pl.program_id(1)

        @pl.when(j == 0)
        def _():
            acc[...] = jnp.zeros_like(acc)
            dyh_ref[...] = (FFN_RES * dy_ref[...]).astype(bf16)

        dyh = dyh_ref[...]
        da = _dg(dyh, wd_ref[0, 0], 1, 1)
        gg, uu = g_ref[0], u_ref[0]
        sg = jax.nn.sigmoid(gg)
        si = gg * sg
        dgv = (da * uu * (sg * (1.0 + gg * (1.0 - sg)))).astype(bf16)
        duv = (da * si).astype(bf16)
        dg_ref[0] = dgv
        du_ref[0] = duv
        a_ref[0] = (si * uu).astype(bf16)
        acc[...] += _dg(dgv, wg_ref[0, 0], 1, 1) + _dg(duv, wu_ref[0, 0], 1, 1)

        @pl.when(j == ns - 1)
        def _():
            _, vjp = jax.vjp(lambda xx, ww: f_rms(None, xx, ww)[0], x_ref[...], nw_ref[...])
            dx, dw = vjp(acc[...])
            dx_ref[...] = dx + dy_ref[...]

            @pl.when(i == 0)
            def _():
                dnw_ref[...] = jnp.zeros_like(dnw_ref)
            dnw_ref[...] += dw

        rider.wait(grid, r_ins, r_outs, sems)

    row = pl.BlockSpec((tm, d), lambda i, j: (i, 0))
    wcol = pl.BlockSpec((1, 1, d, fs), lambda i, j: (j, 0, 0, 0))
    wrow = pl.BlockSpec((1, 1, fs, d), lambda i, j: (j, 0, 0, 0))
    act = pl.BlockSpec((1, tm, fs), lambda i, j: (j, i, 0))
    par = pl.BlockSpec((1, d), lambda i, j: (0, 0))
    return pl.pallas_call(body, name="ffn_bwd_act", grid=grid,
                          in_specs=[row, row, par, act, act, wcol, wcol, wrow] + rider.specs(),
                          out_specs=[row, par, act, act, act, row] + rider.specs(),
                          out_shape=[jax.ShapeDtypeStruct((t, d), f32), jax.ShapeDtypeStruct((1, d), f32)]
                          + [jax.ShapeDtypeStruct((ns, t, fs), bf16)] * 3 + [jax.ShapeDtypeStruct((t, d), bf16)]
                          + rider.out_shapes(),
                          scratch_shapes=[pltpu.VMEM((tm, d), f32)] + rider.scratch(),
                          compiler_params=_cp(2))(dy, x, nw, g, u, wg, wu, wd, *rider.arrays())


def ffn_bwd_w(h, dyh, dg, du, a, tk=1024, ride=None):
    t, d = h.shape
    ns, _, fs = dg.shape
    nk = t // tk
    rider = Rider(ride)
    nr = rider.n
    grid = (ns, nk)

    def body(*refs):
        h_ref, dy_ref, dg_ref, du_ref, a_ref = refs[:5]
        r_ins = refs[5:5 + nr]
        og, ou, od = refs[5 + nr:8 + nr]
        r_outs = refs[8 + nr:8 + 2 * nr]
        ag, au, ad = refs[8 + 2 * nr:11 + 2 * nr]
        sems = refs[11 + 2 * nr:]
        rider.start(grid, r_ins, r_outs, sems)
        k = pl.program_id(1)

        @pl.when(k == 0)
        def _():
            ag[...] = jnp.zeros_like(ag)
            au[...] = jnp.zeros_like(au)
            ad[...] = jnp.zeros_like(ad)

        hh = h_ref[...]
        ag[...] += _dg(hh, dg_ref[0], 0, 0)
        au[...] += _dg(hh, du_ref[0], 0, 0)
        ad[...] += _dg(a_ref[0], dy_ref[...], 0, 0)

        @pl.when(k == nk - 1)
        def _():
            og[0, 0] = ag[...].astype(og.dtype)
            ou[0, 0] = au[...].astype(ou.dtype)
            od[0, 0] = ad[...].astype(od.dtype)

        rider.wait(grid, r_ins, r_outs, sems)

    row = pl.BlockSpec((tk, d), lambda j, k: (k, 0))
    act = pl.BlockSpec((1, tk, fs), lambda j, k: (j, k, 0))
    wcol = pl.BlockSpec((1, 1, d, fs), lambda j, k: (j, 0, 0, 0))
    wrow = pl.BlockSpec((1, 1, fs, d), lambda j, k: (j, 0, 0, 0))
    return pl.pallas_call(body, name="ffn_bwd_w", grid=grid, in_specs=[row, row, act, act, act] + rider.specs(),
                          out_specs=[wcol, wcol, wrow] + rider.specs(),
                          out_shape=[jax.ShapeDtypeStruct((ns, 1, d, fs), bf16)] * 2
                          + [jax.ShapeDtypeStruct((ns, 1, fs, d), bf16)] + rider.out_shapes(),
                          scratch_shapes=[pltpu.VMEM((d, fs), f32), pltpu.VMEM((d, fs), f32), pltpu.VMEM((fs, d), f32)]
                          + rider.scratch(),
                          compiler_params=_cp(2))(h, dyh, dg, du, a, *rider.arrays())


def f_conv(pids, x, w, b):
    y = b + x * w[CONV_W - 1:CONV_W, :]
    for j in range(CONV_W - 1):
        y = y + tshift(x, CONV_W - 1 - j) * w[j:j + 1, :]
    return (_silu(y),)


def _conv_specs(seq, col0, cb):
    xs = pl.BlockSpec((seq, cb), lambda c, b: (b, col0 + c))
    ws = pl.BlockSpec((CONV_W, cb), lambda c, b: (0, c))
    bs = pl.BlockSpec((1, cb), lambda c, b: (0, c))
    ys = pl.BlockSpec((seq, cb), lambda c, b: (b, c))
    return xs, ws, bs, ys


def conv_fwd(name, src, col_off, w, b, seq, cb=256):
    t = src.shape[0]
    c = w.shape[1]
    xs, ws, bs, ys = _conv_specs(seq, col_off // cb, cb)
    return map_fwd(name, f_conv, (c // cb, t // seq), [src, w, b], [xs, ws, bs],
                   [jax.ShapeDtypeStruct((t, c), f32)], [ys])[0]


def conv_bwd(name, dy, src, col_off, w, b, seq, cb=256):
    t = src.shape[0]
    c = w.shape[1]
    xs, ws, bs, ys = _conv_specs(seq, col_off // cb, cb)

    def body(x_ref, w_ref, b_ref, dy_ref, dx_ref, dw_ref, db_ref):
        _, vjp = jax.vjp(lambda xx, ww, bb: f_conv(None, xx, ww, bb)[0], x_ref[...], w_ref[...], b_ref[...])
        dx, dw, db = vjp(dy_ref[...])
        dx_ref[...] = dx

        @pl.when(pl.program_id(1) == 0)
        def _():
            dw_ref[...] = jnp.zeros_like(dw_ref)
            db_ref[...] = jnp.zeros_like(db_ref)
        dw_ref[...] += dw
        db_ref[...] += db

    return pl.pallas_call(body, name=name, grid=(c // cb, t // seq), in_specs=[xs, ws, bs, ys], out_specs=[ys, ws, bs],
                          out_shape=[jax.ShapeDtypeStruct((t, c), f32), jax.ShapeDtypeStruct(w.shape, f32),
                                     jax.ShapeDtypeStruct(b.shape, f32)], compiler_params=_cp(2))(src, w, b, dy)


def f_ssd(pids, states, xs, dtraw, bm, cm, a_log, dt_bias, d_skip):
    g = pids[2]
    (hn,) = states
    l = xs.shape[0]
    head_of_lane = _iota((LANES, SSD_GW), 1) // SSD_HEAD_DIM + SSD_HG * g
    expand = (_iota((LANES, SSD_GW), 0) == head_of_lane).astype(f32)
    tri = _tri(l)
    dt = jax.nn.softplus(dtraw + dt_bias)
    adt = dt * (-jnp.exp(a_log))
    cs = hdot(tri.astype(f32), adt)
    cst = cs.T
    cs_last = cs[l - 1:l, :]
    dt_e, cs_e, csl_e = hdot(dt, expand), hdot(cs, expand), hdot(cs_last, expand)
    xd = xs * dt_e
    gmat = bdot(cm, bm, 1, 1)
    half = _iota((l, LANES), 1) < SSD_HEAD_DIM
    blocks = []
    for pair in range(SSD_HG // 2):
        xb = xd[:, pair * LANES:(pair + 1) * LANES]
        res = []
        for sub in range(2):
            hid = SSD_HG * g + 2 * pair + sub
            col, row = _lane_pick(cs, hid), _row_pick(cst, hid)
            lm = jnp.exp(jnp.where(tri, col - row, NEG))
            res.append(bdot(gmat * lm, xb, 1, 0))
        blocks.append(jnp.where(half, res[0], res[1]))
    y = jnp.concatenate(blocks, axis=1)
    y = y + jnp.exp(cs_e) * bdot(cm, hn, 1, 0)
    y = y + hdot(d_skip, expand) * xs
    hn_new = jnp.exp(csl_e) * hn + bdot(bm, jnp.exp(csl_e - cs_e) * xd, 0, 0)
    return (y,), (hn_new,)


def _ssd_specs(seq, nch, rev):
    cc = (lambda c: nch - 1 - c) if rev else (lambda c: c)
    xs = pl.BlockSpec((CHUNK, SSD_GW), lambda b, c, g: (b * nch + cc(c), g))
    dt = pl.BlockSpec((CHUNK, LANES), lambda b, c, g: (b * nch + cc(c), OFF_DT // LANES))
    bm = pl.BlockSpec((CHUNK, SSD_STATE), lambda b, c, g: (b * nch + cc(c), 1024 // SSD_STATE + g))
    cm = pl.BlockSpec((CHUNK, SSD_STATE), lambda b, c, g: (b * nch + cc(c), 1024 // SSD_STATE + SSD_GROUPS + g))
    par = pl.BlockSpec((1, LANES), lambda b, c, g: (0, 0))
    sv = pl.BlockSpec((1, 1, SSD_STATE, SSD_GW), lambda b, c, g: (b * nch + cc(c), g, 0, 0))
    ddt = pl.BlockSpec((CHUNK, LANES), lambda b, c, g: (b * nch + cc(c), 0))
    dbc = pl.BlockSpec((CHUNK, SSD_STATE), lambda b, c, g: (b * nch + cc(c), g))
    return xs, dt, bm, cm, par, sv, ddt, dbc


def ssd_fwd(xbc, proj, a_log, dt_bias, d_skip, seq, ride=None):
    t = xbc.shape[0]
    nch = seq // CHUNK
    xs, dt, bm, cm, par, sv, _, _ = _ssd_specs(seq, nch, False)
    grid = (t // seq, nch, SSD_GROUPS)
    y, hsave, *got = scan_fwd("ssd_fwd", f_ssd, grid, 2, [xbc, proj, xbc, xbc, a_log, dt_bias, d_skip],
                              [xs, dt, bm, cm, par, par, par], [jax.ShapeDtypeStruct((t, SSD_GROUPS * SSD_GW), f32)], [xs],
                              [(SSD_STATE, SSD_GW)], [0.0],
                              [jax.ShapeDtypeStruct((t // CHUNK, SSD_GROUPS, SSD_STATE, SSD_GW), f32)], [sv], ride=ride)
    return y, hsave, got


def ssd_bwd(dy, xbc, proj, a_log, dt_bias, d_skip, hsave, seq):
    t = xbc.shape[0]
    nch = seq // CHUNK
    xs, dt, bm, cm, par, sv, ddt, dbc = _ssd_specs(seq, nch, True)
    grid = (t // seq, nch, SSD_GROUPS)

    def body(x_ref, dt_ref, b_ref, c_ref, al_ref, db_ref, ds_ref, h_ref, dy_ref,
             dxbc_x, dxbc_b, dxbc_c, ddt_ref, dal_ref, ddb_ref, dds_ref, dst):
        pids = tuple(pl.program_id(i) for i in range(3))
        slot = pids[2]

        @pl.when(pids[1] == 0)
        def _():
            dst[slot] = jnp.zeros(dst.shape[1:], f32)

        vals = [x_ref[...], dt_ref[...], b_ref[...], c_ref[...], al_ref[...], db_ref[...], ds_ref[...]]

        def gfun(st, *v):
            outs, new = f_ssd(pids, (st,), *v)
            return outs[0], new[0]

        _, vjp = jax.vjp(gfun, h_ref[0, 0], *vals)
        grads = vjp((dy_ref[...], dst[slot]))
        dst[slot] = grads[0]
        dxbc_x[...] = grads[1]
        dxbc_b[...] = grads[3]
        dxbc_c[...] = grads[4]

        @pl.when(slot == 0)
        def _():
            ddt_ref[...] = jnp.zeros_like(ddt_ref)
        ddt_ref[...] += grads[2]
        first = jnp.logical_and(jnp.logical_and(pids[0] == 0, pids[1] == 0), slot == 0)

        @pl.when(first)
        def _():
            dal_ref[...] = jnp.zeros_like(dal_ref)
            ddb_ref[...] = jnp.zeros_like(ddb_ref)
            dds_ref[...] = jnp.zeros_like(dds_ref)
        dal_ref[...] += grads[5]
        ddb_ref[...] += grads[6]
        dds_ref[...] += grads[7]

    bc_shape = jax.ShapeDtypeStruct((t, SSD_GROUPS * SSD_STATE), f32)
    par_shape = jax.ShapeDtypeStruct((1, LANES), f32)
    outs = pl.pallas_call(body, name="ssd_bwd", grid=grid, in_specs=[xs, dt, bm, cm, par, par, par, sv, xs],
                          out_specs=[xs, dbc, dbc, ddt, par, par, par],
                          out_shape=[jax.ShapeDtypeStruct((t, SSD_GROUPS * SSD_GW), f32),
                                     bc_shape, bc_shape, jax.ShapeDtypeStruct((t, LANES), f32),
                                     par_shape, par_shape, par_shape],
                          scratch_shapes=[pltpu.VMEM((SSD_GROUPS, SSD_STATE, SSD_GW), f32)],
                          compiler_params=_cp(3))(xbc, proj, xbc, xbc, a_log, dt_bias, d_skip, hsave, dy)
    return outs


def f_ssd_epi(pids, y, z, nw):
    yg = y * _silu(z)
    hw = yg.shape[1] // SSD_GROUPS
    parts = []
    for g in range(SSD_GROUPS):
        p = yg[:, g * hw:(g + 1) * hw]
        parts.append(p * lax.rsqrt(jnp.mean(p * p, axis=-1, keepdims=True) + EPS))
    return (jnp.concatenate(parts, axis=1) * nw,)


def f_ml_epi(pids, hm, xc, mz, nw, skip):
    parts = []
    for h in range(ML_HEADS):
        p = hm[:, h * ML_HD:(h + 1) * ML_HD]
        mu = jnp.mean(p, axis=-1, keepdims=True)
        var = jnp.mean(jnp.square(p - mu), axis=-1, keepdims=True)
        parts.append((p - mu) * lax.rsqrt(var + EPS))
    hn = jnp.concatenate(parts, axis=1) * nw
    return ((hn + skip * xc) * _silu(mz),)


def f_s5_post(pids, ys, u, d_skip):
    return (jax.nn.gelu(ys + d_skip * u),)


def f_glu(pids, pab, ba, bb):
    d = ba.shape[1]
    return ((pab[:, :d] + ba) * jax.nn.sigmoid(pab[:, d:] + bb),)


def f_glu_res(pids, pab, xres, ba, bb):
    return (xres + f_glu(pids, pab, ba, bb)[0],)


def rowwise_fwd(name, f, rows, row_cols, pars, out_width, tm=512):
    t = rows[0].shape[0]
    specs = [_row_spec(tm, w, c) for (w, c) in row_cols] + [_par_spec(p.shape) for p in pars]
    return map_fwd(name, f, (t // tm,), list(rows) + list(pars), specs, [jax.ShapeDtypeStruct((t, out_width), f32)],
                   [_row_spec(tm, out_width)])[0]


def rowwise_bwd(name, f, rows, row_cols, pars, dy, tm=256):
    t = rows[0].shape[0]
    n_r, n_p = len(rows), len(pars)
    specs = [_row_spec(tm, w, c) for (w, c) in row_cols] + [_par_spec(p.shape) for p in pars]
    out_w = dy.shape[1]

    def body(*refs):
        vals = [r[...] for r in refs[:n_r + n_p]]
        dy_ref = refs[n_r + n_p]
        outs = refs[n_r + n_p + 1:]
        _, vjp = jax.vjp(lambda *v: f(None, *v)[0], *vals)
        grads = vjp(dy_ref[...])
        for k in range(n_r):
            outs[k][...] = grads[k]

        @pl.when(pl.program_id(0) == 0)
        def _():
            for k in range(n_p):
                outs[n_r + k][...] = jnp.zeros_like(outs[n_r + k])
        for k in range(n_p):
            outs[n_r + k][...] += grads[n_r + k]

    out_shapes = [jax.ShapeDtypeStruct((t, w), f32) for (w, c) in row_cols] + [jax.ShapeDtypeStruct(p.shape, f32) for p in pars]
    out_specs = [_row_spec(tm, w) for (w, c) in row_cols] + [_par_spec(p.shape) for p in pars]
    return pl.pallas_call(body, name=name, grid=(t // tm,), in_specs=specs + [_row_spec(tm, out_w)], out_specs=out_specs,
                          out_shape=out_shapes, compiler_params=_cp(1))(*rows, *pars, dy)


def f_ml(pids, states, q, k, v, g1, g2, g3, b_if):
    h = pids[2]
    cst, nst, mst = states
    l = q.shape[0]
    gt = g1 + g2 + g3 + b_if
    k = k * (1.0 / math.sqrt(ML_HD))
    tri = _tri(l)
    bc_all = hdot(tri.astype(f32), jax.nn.log_sigmoid(gt))
    bcum, ig = _lane_pick(bc_all, ML_HEADS + h), _lane_pick(gt, h)
    bcum_t, ig_t = _row_pick(bc_all.T, ML_HEADS + h), _row_pick(gt.T, h)
    b_last = bcum[l - 1:l, :]
    dlog = jnp.where(tri, bcum - bcum_t + ig_t, NEG)
    ws = b_last - bcum + ig
    m_prev = mst[:, 0:1]
    m_new = lax.stop_gradient(jnp.maximum(b_last + m_prev, jnp.max(ws, axis=0, keepdims=True)))
    decay = jnp.exp(b_last + m_prev - m_new)
    wts = jnp.exp(ws - m_new)
    c_new = decay * cst + bdot(wts * v, k, 0, 0)
    n_new = decay * nst + jnp.sum(wts * k, axis=0, keepdims=True)
    m_inter = bcum + m_prev
    m_t = lax.stop_gradient(jnp.maximum(jnp.max(dlog, axis=1, keepdims=True), m_inter))
    scores = bdot(q, k, 1, 1) * jnp.exp(dlog - m_t)
    inter_w = jnp.exp(m_inter - m_t)
    num = bdot(scores, v, 1, 0) + inter_w * bdot(q, cst, 1, 1)
    den = jnp.sum(scores, axis=1, keepdims=True) + inter_w * jnp.sum(q * nst, axis=1, keepdims=True)
    hout = num / jnp.maximum(jnp.abs(den), jnp.exp(-m_t))
    return (hout,), (c_new, n_new, jnp.broadcast_to(m_new, mst.shape))


def _ml_specs(nch, rev):
    cc = (lambda c: nch - 1 - c) if rev else (lambda c: c)
    hd = pl.BlockSpec((CHUNK, ML_HD), lambda b, c, h: (b * nch + cc(c), h))
    gt = pl.BlockSpec((CHUNK, LANES), lambda b, c, h: (b * nch + cc(c), 0))
    par = pl.BlockSpec((1, LANES), lambda b, c, h: (0, 0))
    sc = pl.BlockSpec((1, 1, ML_HD, ML_HD), lambda b, c, h: (b * nch + cc(c), h, 0, 0))
    sn = pl.BlockSpec((1, 1, 1, ML_HD), lambda b, c, h: (b * nch + cc(c), h, 0, 0))
    sm = pl.BlockSpec((1, 1, 1, LANES), lambda b, c, h: (b * nch + cc(c), h, 0, 0))
    return hd, gt, par, sc, sn, sm


ML_STATE_SHAPES = [(ML_HD, ML_HD), (1, ML_HD), (1, LANES)]


def ml_fwd(q, k, v, g1, g2, g3, b_if, seq, ride=None):
    t = q.shape[0]
    nch = seq // CHUNK
    hd, gt, par, sc, sn, sm = _ml_specs(nch, False)
    nc = t // CHUNK
    outs = scan_fwd("ml_fwd", f_ml, (t // seq, nch, ML_HEADS), 2, [q, k, v, g1, g2, g3, b_if],
                    [hd, hd, hd, gt, gt, gt, par], [jax.ShapeDtypeStruct((t, ML_HEADS * ML_HD), f32)], [hd],
                    ML_STATE_SHAPES, [0.0, 0.0, NEG],
                    [jax.ShapeDtypeStruct((nc, ML_HEADS, ML_HD, ML_HD), f32), jax.ShapeDtypeStruct((nc, ML_HEADS, 1, ML_HD), f32),
                     jax.ShapeDtypeStruct((nc, ML_HEADS, 1, LANES), f32)], [sc, sn, sm], ride=ride)
    return outs[0], outs[1:4], outs[4:]


def ml_bwd(dh, q, k, v, g1, g2, g3, b_if, saves, seq, ride=None):
    t = q.shape[0]
    nch = seq // CHUNK
    hd, gt, par, sc, sn, sm = _ml_specs(nch, True)
    rider = Rider(ride)
    nr = rider.n
    grid = (t // seq, nch, ML_HEADS)

    def f(pids, states, q, k, v, gsum, b_if):
        return f_ml(pids, states, q, k, v, gsum, jnp.zeros_like(gsum), jnp.zeros_like(gsum), b_if)

    def body(*refs):
        q_ref, k_ref, v_ref, g1_ref, g2_ref, g3_ref, b_ref, c_ref, n_ref, m_ref, dh_ref = refs[:11]
        r_ins = refs[11:11 + nr]
        dq_ref, dk_ref, dv_ref, dg_ref, db_ref = refs[11 + nr:16 + nr]
        r_outs = refs[16 + nr:16 + 2 * nr]
        dc_s, dn_s = refs[16 + 2 * nr:18 + 2 * nr]
        sems = refs[18 + 2 * nr:]
        rider.start(grid, r_ins, r_outs, sems)
        pids = tuple(pl.program_id(i) for i in range(3))
        slot = pids[2]

        @pl.when(pids[1] == 0)
        def _():
            dc_s[slot] = jnp.zeros(dc_s.shape[1:], f32)
            dn_s[slot] = jnp.zeros(dn_s.shape[1:], f32)

        gsum = g1_ref[...] + g2_ref[...] + g3_ref[...]
        mst = m_ref[0, 0]

        def gfun(cst, nst, qq, kk, vv, gs, bb):
            outs, new = f(pids, (cst, nst, mst), qq, kk, vv, gs, bb)
            return outs[0], new[0], new[1]

        _, vjp = jax.vjp(gfun, c_ref[0, 0], n_ref[0, 0], q_ref[...], k_ref[...], v_ref[...], gsum, b_ref[...])
        grads = vjp((dh_ref[...], dc_s[slot], dn_s[slot]))
        dc_s[slot] = grads[0]
        dn_s[slot] = grads[1]
        dq_ref[...] = grads[2]
        dk_ref[...] = grads[3]
        dv_ref[...] = grads[4]

        @pl.when(slot == 0)
        def _():
            dg_ref[...] = jnp.zeros_like(dg_ref)
        dg_ref[...] += grads[5]
        first = jnp.logical_and(jnp.logical_and(pids[0] == 0, pids[1] == 0), slot == 0)

        @pl.when(first)
        def _():
            db_ref[...] = jnp.zeros_like(db_ref)
        db_ref[...] += grads[6]
        rider.wait(grid, r_ins, r_outs, sems)

    big = jax.ShapeDtypeStruct((t, ML_HEADS * ML_HD), f32)
    return pl.pallas_call(body, name="ml_bwd", grid=grid,
                          in_specs=[hd, hd, hd, gt, gt, gt, par, sc, sn, sm, hd] + rider.specs(),
                          out_specs=[hd, hd, hd, gt, par] + rider.specs(),
                          out_shape=[big, big, big, jax.ShapeDtypeStruct((t, LANES), f32), jax.ShapeDtypeStruct((1, LANES), f32)]
                          + rider.out_shapes(),
                          scratch_shapes=[pltpu.VMEM((ML_HEADS, ML_HD, ML_HD), f32), pltpu.VMEM((ML_HEADS, 1, ML_HD), f32)]
                          + rider.scratch(),
                          compiler_params=_cp(3))(q, k, v, g1, g2, g3, b_if, *saves, dh, *rider.arrays())


def _block_prefix(z, transpose):
    n = z.shape[0]
    r, c = _iota((n, n), 0), _iota((n, n), 1)
    keep = jnp.logical_and(r // S5_SUB == c // S5_SUB, (c >= r) if transpose else (c <= r))
    m = jnp.where(keep, 1.0, 0.0).astype(bf16)
    hi = z.astype(bf16)
    lo = (z - hi.astype(f32)).astype(bf16)
    return jnp.dot(m, hi, preferred_element_type=f32) + jnp.dot(m, lo, preferred_element_type=f32)


@jax.custom_vjp
def block_prefix(z):
    return _block_prefix(z, False)


block_prefix.defvjp(lambda z: (_block_prefix(z, False), None), lambda _, ct: (_block_prefix(ct, True),))


def _cmul(a, b):
    h = b.shape[1] // 2
    ar, ai, br, bi = a[:, :h], a[:, h:], b[:, :h], b[:, h:]
    return jnp.concatenate([ar * br - ai * bi, ar * bi + ai * br], axis=1)


def f_s5(pids, states, u, bb, cc, tab):
    (carry,) = states
    tl = u.shape[0]
    nsub = tl // S5_SUB
    rep = lambda t: jnp.concatenate([t] * nsub, axis=0)
    p1, p0, q0 = tab[0:S5_SUB], tab[S5_SUB:2 * S5_SUB], tab[2 * S5_SUB:3 * S5_SUB]
    lam_sub = tab[S5_SUB - 1:S5_SUB]
    bu = bdot(u, bb, 1, 0)
    xl = _cmul(rep(p0), block_prefix(_cmul(rep(q0), bu)))
    e, entering = carry, []
    for k in range(nsub):
        entering.append(jnp.broadcast_to(e, (S5_SUB, e.shape[1])))
        e = xl[(k + 1) * S5_SUB - 1:(k + 1) * S5_SUB] + _cmul(lam_sub, e)
    x = xl + _cmul(rep(p1), jnp.concatenate(entering, axis=0))
    y = bdot(x, cc, 1, 0)
    return (y,), (e,)


def _s5_specs(ntl, rev):
    tt = (lambda t: ntl - 1 - t) if rev else (lambda t: t)
    us = pl.BlockSpec((S5_TL, LANES), lambda c, b, t: (b * ntl + tt(t), c))
    bbs = pl.BlockSpec((1, LANES, 2 * S5_CH), lambda c, b, t: (c, 0, 0))
    ccs = pl.BlockSpec((1, 2 * S5_CH, LANES), lambda c, b, t: (c, 0, 0))
    pws = pl.BlockSpec((1, 3 * S5_SUB, 2 * S5_CH), lambda c, b, t: (c, 0, 0))
    sv = pl.BlockSpec((1, 1, 1, 2 * S5_CH), lambda c, b, t: (b * ntl + tt(t), c, 0, 0))
    return us, bbs, ccs, pws, sv


def s5_fwd(u, bb, cc, pw, seq):
    t = u.shape[0]
    ntl = seq // S5_TL
    us, bbs, ccs, pws, sv = _s5_specs(ntl, False)

    def f(pids, states, uu, b3, c3, p3):
        return f_s5(pids, states, uu, b3[0], c3[0], p3[0])

    y, carries = scan_fwd("s5_fwd", f, (S5_CB, t // seq, ntl), 0, [u, bb, cc, pw], [us, bbs, ccs, pws],
                          [jax.ShapeDtypeStruct((t, S5_CB * LANES), f32)], [us], [(1, 2 * S5_CH)], [0.0],
                          [jax.ShapeDtypeStruct((t // S5_TL, S5_CB, 1, 2 * S5_CH), f32)], [sv])
    return y, carries


def s5_bwd(dy, u, bb, cc, pw, carries, seq):
    t = u.shape[0]
    ntl = seq // S5_TL
    us, bbs, ccs, pws, sv = _s5_specs(ntl, True)

    def f(pids, states, uu, b3, c3, p3):
        return f_s5(pids, states, uu, b3[0], c3[0], p3[0])

    first = lambda pids: jnp.logical_and(pids[1] == 0, pids[2] == 0)
    return scan_bwd("s5_bwd", f, (S5_CB, t // seq, ntl), 0, [u, bb, cc, pw], [us, bbs, ccs, pws], [carries], [sv],
                    [dy], [us], [(1, 2 * S5_CH)], [0, 1, 2, 3], {1: first, 2: first, 3: first})


def _adam_math(g, w, m, v):
    m2 = ADAM_B1 * m + (1.0 - ADAM_B1) * g
    v2 = ADAM_B2 * v + (1.0 - ADAM_B2) * jnp.square(g)
    m_hat = m2 / (1.0 - ADAM_B1 ** ADAM_STEP)
    v_hat = v2 / (1.0 - ADAM_B2 ** ADAM_STEP)
    delta = -ADAM_LR * (m_hat / (jnp.sqrt(v_hat) + ADAM_EPS) + ADAM_WD * w)
    return delta, m2, v2


def adamw(name, parts, w, m, v, tr=256):
    n, r, c = parts.shape
    tr = min(tr, r)
    assert r % tr == 0

    def body(p_ref, w_ref, m_ref, v_ref, g_ref, d_ref, m2_ref, v2_ref):
        g = p_ref[0].astype(f32)
        for s in range(1, n):
            g = g + p_ref[s].astype(f32)
        d, m2, v2 = _adam_math(g, w_ref[...], m_ref[...], v_ref[...])
        g_ref[...] = g
        d_ref[...] = d
        m2_ref[...] = m2
        v2_ref[...] = v2

    ps = pl.BlockSpec((n, tr, c), lambda i: (0, i, 0))
    rs = pl.BlockSpec((tr, c), lambda i: (i, 0))
    return pl.pallas_call(body, name=name, grid=(r // tr,), in_specs=[ps, rs, rs, rs], out_specs=[rs] * 4,
                          out_shape=[jax.ShapeDtypeStruct((r, c), f32)] * 4, compiler_params=_cp(1))(parts, w, m, v)


def adamw_layer(name, parts, w, m, v, layer, prev=None, tr=256):
    n, r, c = parts.shape
    nl = w.shape[0]
    tr = min(tr, r)
    assert r % tr == 0 and w.shape[1:] == (r, c)
    n_prev = 0 if prev is None else 4

    def body(*refs):
        p_ref, w_ref, m_ref, v_ref = refs[:4]
        g_ref, d_ref, m2_ref, v2_ref = refs[4 + n_prev:]
        g = p_ref[0].astype(f32)
        for s in range(1, n):
            g = g + p_ref[s].astype(f32)
        d, m2, v2 = _adam_math(g, w_ref[0], m_ref[0], v_ref[0])
        g_ref[0] = g
        d_ref[0] = d
        m2_ref[0] = m2
        v2_ref[0] = v2

    ps = pl.BlockSpec((n, tr, c), lambda i: (0, i, 0))
    rs = pl.BlockSpec((1, tr, c), lambda i: (layer, i, 0))
    anyspec = pl.BlockSpec(memory_space=pl.ANY)
    return pl.pallas_call(body, name=name, grid=(r // tr,), in_specs=[ps, rs, rs, rs] + [anyspec] * n_prev, out_specs=[rs] * 4,
                          out_shape=[jax.ShapeDtypeStruct((nl, r, c), f32)] * 4,
                          input_output_aliases={4 + i: i for i in range(n_prev)},
                          compiler_params=_cp(1))(parts, w, m, v, *(prev or ()))


def sum_parts(name, parts, tr=256):
    n, r, c = parts.shape
    tr = min(tr, r)
    assert r % tr == 0

    def body(p_ref, o_ref):
        g = p_ref[0].astype(f32)
        for s in range(1, n):
            g = g + p_ref[s].astype(f32)
        o_ref[...] = g

    return pl.pallas_call(body, name=name, grid=(r // tr,), in_specs=[pl.BlockSpec((n, tr, c), lambda i: (0, i, 0))],
                          out_specs=pl.BlockSpec((tr, c), lambda i: (i, 0)),
                          out_shape=jax.ShapeDtypeStruct((r, c), f32), compiler_params=_cp(1))(parts)


class Rider:
    def __init__(self, ops):
        self.ops = list(ops or [])
        self.n = len(self.ops)

    def arrays(self):
        return [a for a, _ in self.ops]

    def specs(self):
        return [pl.BlockSpec(memory_space=pl.ANY)] * self.n

    def out_shapes(self):
        return [jax.ShapeDtypeStruct((N_DEV,) + tuple(a.shape) if mode == "gather" else tuple(a.shape), a.dtype)
                for a, mode in self.ops]

    def scratch(self):
        if not self.n:
            return []
        return [pltpu.SemaphoreType.DMA((self.n, N_DEV - 1)), pltpu.SemaphoreType.DMA((self.n, N_DEV - 1)),
                pltpu.SemaphoreType.DMA((self.n,))]

    def _copies(self, ins, outs, sems):
        send_sems, recv_sems, loc_sems = sems
        x, y, c = lax.axis_index("x"), lax.axis_index("y"), lax.axis_index("c")
        me = 4 * x + 2 * y + c
        first, crossing, relays = [], [], []

        def remote(src, dst, k, idx, peer):
            return pltpu.make_async_remote_copy(src_ref=src, dst_ref=dst, send_sem=send_sems.at[k, idx], recv_sem=recv_sems.at[k, idx],
                                                device_id=peer, device_id_type=pl.DeviceIdType.MESH)

        for k, (_, mode) in enumerate(self.ops):
            src_me = ins[k] if mode == "gather" else ins[k].at[me]
            first.append(pltpu.make_async_copy(src_me, outs[k].at[me], loc_sems.at[k]))
            if mode == "scatter":
                for d in range(1, N_DEV):
                    px = 1 - x if (d >> 2) & 1 else x
                    py = 1 - y if (d >> 1) & 1 else y
                    pc = 1 - c if d & 1 else c
                    first.append(remote(ins[k].at[4 * px + 2 * py + pc], outs[k].at[me], k, d - 1, (px, py, pc)))
            else:
                first.append(remote(ins[k], outs[k].at[me], k, 0, (x, y, 1 - c)))
                for q in range(1, 4):
                    px = 1 - x if (q >> 1) & 1 else x
                    py = 1 - y if q & 1 else y
                    crossing.append(remote(ins[k], outs[k].at[me], k, q, (px, py, c)))
                    block = outs[k].at[4 * px + 2 * py + c]
                    relays.append(remote(block, block, k, 3 + q, (x, y, 1 - c)))
        return first, crossing, relays

    def _start(self, ins, outs, sems):
        first, crossing, _ = self._copies(ins, outs, sems)
        for cp in first + crossing:
            cp.start()

    def _finish(self, ins, outs, sems):
        first, crossing, relays = self._copies(ins, outs, sems)
        for cp, relay in zip(crossing, relays):
            cp.wait_recv()
            relay.start()
        for cp in first + relays:
            cp.wait()
        for cp in crossing:
            cp.wait_send()

    def start(self, grid, ins, outs, sems):
        if self.n:
            @pl.when(functools.reduce(jnp.logical_and, [pl.program_id(i) == 0 for i in range(len(grid))]))
            def _():
                self._start(ins, outs, sems)

    def wait(self, grid, ins, outs, sems):
        if self.n:
            @pl.when(functools.reduce(jnp.logical_and, [pl.program_id(i) == g - 1 for i, g in enumerate(grid)]))
            def _():
                self._finish(ins, outs, sems)


def exchange(name, ops):
    rider = Rider(ops)
    n = rider.n

    def body(*refs):
        rider._start(refs[:n], refs[n:2 * n], refs[2 * n:])
        rider._finish(refs[:n], refs[n:2 * n], refs[2 * n:])

    return pl.pallas_call(body, name=name, in_specs=rider.specs(), out_specs=rider.specs(), out_shape=rider.out_shapes(),
                          scratch_shapes=rider.scratch())(*rider.arrays())


def _lanes(v, width=LANES):
    v = v.reshape(1, -1)
    return jnp.pad(v, ((0, 0), (0, width - v.shape[1])))


def win_to_padded(w):
    return jnp.concatenate([w[:, :1024], w[:, 2576:3600], w[:, 3600:4624], w[:, 1024:2560], w[:, 2560:2576],
                            jnp.zeros((w.shape[0], PROJ_W - IN_COLS), w.dtype)], axis=1)


def win_from_padded(wp):
    return jnp.concatenate([wp[:, 0:1024], wp[:, 3072:4608], wp[:, 4608:4624], wp[:, 1024:2048], wp[:, 2048:3072]], axis=1)


def headwise_dense(w):
    nb, o, i = w.shape
    rows = jnp.tile(w.transpose(0, 2, 1).reshape(nb * i, o), (1, nb))
    same = (jnp.arange(nb * i)[:, None] // i) == (jnp.arange(nb * o)[None, :] // o)
    return jnp.where(same, rows, 0.0)


def diag_blocks(name, dd, blk, tm=256):
    n = dd.shape[0]

    def body(d_ref, o_ref):
        rows = _iota((tm, n), 0) + pl.program_id(0) * tm
        masked = jnp.where(rows // blk == _iota((tm, n), 1) // blk, d_ref[...], 0.0)
        sel = (_iota((n, LANES), 0) % blk == _iota((n, LANES), 1)).astype(f32)
        o_ref[...] = hdot(masked, sel)

    return pl.pallas_call(body, name=name, grid=(n // tm,), in_specs=[pl.BlockSpec((tm, n), lambda i: (i, 0))],
                          out_specs=pl.BlockSpec((tm, LANES), lambda i: (i, 0)),
                          out_shape=jax.ShapeDtypeStruct((n, LANES), f32), compiler_params=_cp(1))(dd)


def headwise_from_dense(name, dd, o=4, i=4):
    nb = dd.shape[0] // i
    return diag_blocks(name, dd, i)[:, :o].reshape(nb, i, o).transpose(0, 2, 1)


def s5_tables(a_re, a_im, log_step, b_re, b_im, c_re, c_im):
    step = jnp.exp(log_step)[:, None]
    j = jnp.arange(S5_SUB, dtype=f32)[:, None, None]
    expo = jnp.concatenate([j + 1.0, j, -j], axis=0)
    mag = jnp.exp(expo * (a_re * step))
    pw_re, pw_im = mag * jnp.cos(expo * (a_im * step)), mag * jnp.sin(expo * (a_im * step))
    lam_re, lam_im = pw_re[0], pw_im[0]
    den = a_re * a_re + a_im * a_im
    coef_re = ((lam_re - 1.0) * a_re + lam_im * a_im) / den
    coef_im = (lam_im * a_re - (lam_re - 1.0) * a_im) / den
    bb_re = coef_re[..., None] * b_re - coef_im[..., None] * b_im
    bb_im = coef_re[..., None] * b_im + coef_im[..., None] * b_re
    gl = S5_GROUPS // S5_CB
    eye = jnp.eye(gl, dtype=f32)

    def blk_b(t):
        t4 = t.transpose(0, 2, 1).reshape(S5_CB, gl, S5_GROUP, S5_STATE)
        return jnp.einsum("kgcn,gh->kgchn", t4, eye).reshape(S5_CB, gl * S5_GROUP, gl * S5_STATE)

    def blk_c(t):
        t4 = t.reshape(S5_CB, gl, S5_GROUP, S5_STATE)
        return jnp.einsum("kgcn,gh->kgnhc", t4, eye).reshape(S5_CB, gl * S5_STATE, gl * S5_GROUP)

    def blk_p(t):
        return t.reshape(t.shape[0], S5_CB, gl * S5_STATE).transpose(1, 0, 2)

    bb = jnp.concatenate([blk_b(bb_re), blk_b(bb_im)], axis=2)
    cc = jnp.concatenate([blk_c(c_re), -blk_c(c_im)], axis=1)
    pw = jnp.concatenate([blk_p(pw_re), blk_p(pw_im)], axis=2)
    return bb, cc, pw


def ffn_step_bwd(dy, x, nw, wts, saved, ride_act=None, ride_w=None):
    h, g, u = saved
    dx, dnw, dg, du, a, dyh, *got_act = ffn_bwd_act(dy, x, nw, g, u, *wts, ride=ride_act)
    dwg, dwu, dwd, *got_w = ffn_bwd_w(h, dyh, dg, du, a, ride=ride_w)
    return dx, dnw, (dwg, dwu, dwd), got_act, got_w


def hybrid_fwd(x1, p, seq, ride_in, ride_ssd, ride_ml):
    u = rms_fwd(x1, p["mix_norm"])
    proj, *got_in = matmul("hy_in", u, p["win"], ride=ride_in)
    xbc = conv_fwd("ssd_conv", proj, OFF_XBC, p["ssd_conv_w"], p["ssd_conv_b"], seq)
    yraw, hsave, got_ssd = ssd_fwd(xbc, proj, p["a_log"], p["dt_bias"], p["ssd_d"], seq, ride=ride_ssd)
    yssd = rowwise_fwd("ssd_epi", f_ssd_epi, [yraw, proj], [(D_MODEL, 0), (D_MODEL, OFF_Z // D_MODEL)], [p["ssd_norm_w"]], D_MODEL)
    xc = conv_fwd("ml_conv", proj, OFF_MX, p["ml_conv_w"], p["ml_conv_b"], seq)
    q = matmul("hw_q", xc, p["wq"])
    k = matmul("hw_k", xc, p["wk"])
    v = matmul("hw_v", proj, p["wv"], a_off=OFF_MX, a_width=D_MODEL)
    g1 = matmul("gate_q", q, p["wif_q"])
    g2 = matmul("gate_k", k, p["wif_k"])
    g3 = matmul("gate_v", v, p["wif_v"])
    hm, mlsave, got_ml = ml_fwd(q, k, v, g1, g2, g3, p["b_if"], seq, ride=ride_ml)
    yml = rowwise_fwd("ml_epi", f_ml_epi, [hm, xc, proj], [(D_MODEL, 0), (D_MODEL, 0), (D_MODEL, OFF_MZ // D_MODEL)],
                      [p["ml_norm_w"], p["ml_skip"]], D_MODEL)
    t = matmul("hy_out1", yssd, p["wo1"], add=x1)
    x2 = matmul("hy_out2", yml, p["wo2"], add=t)
    return x2, (u, proj, xbc, yraw, hsave, yssd, xc, q, k, v, g1, g2, g3, hm, mlsave, yml), got_in, got_ssd, got_ml


def hybrid_bwd(dx2, x1, p, saved, seq, ride_ml):
    u, proj, xbc, yraw, hsave, yssd, xc, q, k, v, g1, g2, g3, hm, mlsave, yml = saved
    gr = {}
    dyssd = matmul("d_yssd", dx2, p["wo1"], cb=1)
    dyml = matmul("d_yml", dx2, p["wo2"], cb=1)
    gr["wo"] = jnp.concatenate([matmul("dw_o1", yssd, dx2, ca=0), matmul("dw_o2", yml, dx2, ca=0)], axis=0)
    d_hm, d_xc, d_mz, gr["ml_norm_w"], gr["ml_skip"] = rowwise_bwd(
        "ml_epi_bwd", f_ml_epi, [hm, xc, proj], [(D_MODEL, 0), (D_MODEL, 0), (D_MODEL, OFF_MZ // D_MODEL)],
        [p["ml_norm_w"], p["ml_skip"]], dyml)
    dq, dk, dv, dgt, gr["b_if"], *got_ml = ml_bwd(d_hm, q, k, v, g1, g2, g3, p["b_if"], mlsave, seq, ride=ride_ml)
    dq = matmul("dq_gate", dgt, p["wif_q"], cb=1, add=dq)
    dk = matmul("dk_gate", dgt, p["wif_k"], cb=1, add=dk)
    dv = matmul("dv_gate", dgt, p["wif_v"], cb=1, add=dv)
    gr["wif"] = jnp.concatenate([matmul("dw_if_q", q, dgt, ca=0), matmul("dw_if_k", k, dgt, ca=0),
                                 matmul("dw_if_v", v, dgt, ca=0)], axis=0)
    d_xc = matmul("dxc_q", dq, p["wq"], cb=1, add=d_xc)
    d_xc = matmul("dxc_k", dk, p["wk"], cb=1, add=d_xc)
    gr["wq"] = matmul("dw_q", xc, dq, ca=0)
    gr["wk"] = matmul("dw_k", xc, dk, ca=0)
    gr["wv"] = matmul("dw_v", proj, dv, ca=0, a_off=OFF_MX, a_width=D_MODEL)
    d_mx, gr["ml_conv_w"], gr["ml_conv_b"] = conv_bwd("ml_conv_bwd", d_xc, proj, OFF_MX, p["ml_conv_w"], p["ml_conv_b"], seq)
    d_mx = matmul("dmx_v", dv, p["wv"], cb=1, add=d_mx)
    d_yraw, d_z, gr["ssd_norm_w"] = rowwise_bwd("ssd_epi_bwd", f_ssd_epi, [yraw, proj],
                                                [(D_MODEL, 0), (D_MODEL, OFF_Z // D_MODEL)], [p["ssd_norm_w"]], dyssd)
    d_xs, d_b, d_c, d_dt, gr["a_log"], gr["dt_bias"], gr["ssd_d"] = ssd_bwd(
        d_yraw, xbc, proj, p["a_log"], p["dt_bias"], p["ssd_d"], hsave, seq)
    d_xbc, gr["ssd_conv_w"], gr["ssd_conv_b"] = conv_bwd("ssd_conv_bwd", jnp.concatenate([d_xs, d_b, d_c], axis=1), proj, OFF_XBC,
                                                         p["ssd_conv_w"], p["ssd_conv_b"], seq)
    dproj = jnp.concatenate([d_z, d_mx, d_mz, d_xbc, d_dt, jnp.zeros((d_dt.shape[0], PROJ_W - OFF_DT - LANES), f32)], axis=1)
    gr["win"] = matmul("dw_in", u.astype(bf16).T, dproj, tiles=(D_MODEL, PROJ_W // 2, min(512, u.shape[0])))
    du = matmul("d_u", dproj, p["win"], cb=1)
    dx1, gr["mix_norm"] = rms_bwd([du], x1, p["mix_norm"], dx2)
    return dx1, gr, got_ml


def s5_layer_fwd(x4, p, seq):
    u = rms_fwd(x4, p["mix_norm"])
    ys, carries = s5_fwd(u, p["bb"], p["cc"], p["pw"], seq)
    gg = rowwise_fwd("s5_post", f_s5_post, [ys, u], [(D_MODEL, 0), (D_MODEL, 0)], [p["s5_d"]], D_MODEL)
    pab = matmul("s5_ab", gg, p["wab"])
    x5 = rowwise_fwd("s5_glu", f_glu_res, [pab, x4], [(2 * D_MODEL, 0), (D_MODEL, 0)], [p["b_a"], p["b_b"]], D_MODEL)
    return x5, (u, ys, carries, gg, pab)


def s5_layer_bwd(dx5, x4, p, saved, seq):
    u, ys, carries, gg, pab = saved
    gr = {}
    dpab, gr["b_a"], gr["b_b"] = rowwise_bwd("s5_glu_bwd", f_glu, [pab], [(2 * D_MODEL, 0)], [p["b_a"], p["b_b"]], dx5)
    dgg = matmul("d_gg", dpab, p["wab"], cb=1)
    gr["wab"] = matmul("dw_ab", gg, dpab, ca=0)
    dys, du_a, gr["s5_d"] = rowwise_bwd("s5_post_bwd", f_s5_post, [ys, u], [(D_MODEL, 0), (D_MODEL, 0)], [p["s5_d"]], dgg)
    du_b, gr["bb"], gr["cc"], gr["pw"] = s5_bwd(dys, u, p["bb"], p["cc"], p["pw"], carries, seq)
    dx4, gr["mix_norm"] = rms_bwd([du_a, du_b], x4, p["mix_norm"], dx5)
    return dx4, gr


BIG = ["ffn1_w_gate", "ffn1_w_up", "ffn1_w_down", "ffn2_w_gate", "ffn2_w_up", "ffn2_w_down", "hy_w_in", "hy_w_out", "s5_w_a", "s5_w_b"]
SMALL_SHARDED = {"ssd_conv_w": 2, "ml_conv_w": 2, "ml_w_q": 1, "ml_w_k": 1, "ml_w_v": 1, "ml_w_if": 1, "s5_d": 1, "s5_b_a": 1, "s5_b_b": 1}
WEIGHTS = ["ffn1_norm", "ffn1_w_gate", "ffn1_w_up", "ffn1_w_down", "mix_norm", "ffn2_norm", "ffn2_w_gate", "ffn2_w_up", "ffn2_w_down",
           "hy_w_in", "ssd_conv_w", "ssd_conv_b", "ssd_dt_bias", "ssd_a_log", "ssd_d", "ssd_norm_w", "ml_conv_w", "ml_conv_b",
           "ml_w_q", "ml_w_k", "ml_w_v", "ml_w_if", "ml_b_if", "ml_norm_w", "ml_skip", "hy_w_out", "s5_a_re", "s5_a_im",
           "s5_log_step", "s5_b_re", "s5_b_im", "s5_c_re", "s5_c_im", "s5_d", "s5_w_a", "s5_b_a", "s5_w_b", "s5_b_b", "final_norm"]
S5_PARAMS = ["s5_a_re", "s5_a_im", "s5_log_step", "s5_b_re", "s5_b_im", "s5_c_re", "s5_c_im"]
SMALL_S5 = S5_PARAMS + ["s5_d", "s5_b_a", "s5_b_b"]
SMALL_REST = [n for n in WEIGHTS if n not in BIG and n not in SMALL_S5]
SMALL = SMALL_REST + SMALL_S5


def _unshard(g, axis):
    return jnp.concatenate([g[i] for i in range(N_DEV)], axis=axis)


def assemble_hybrid(gw, rep):
    padn = lambda w: jnp.pad(w, ((0, 0), (0, LANES - w.shape[1]))).astype(bf16)
    wif = _unshard(gw["ml_w_if"], 1)[0]
    wo = _unshard(gw["hy_w_out"], 1)[0].astype(bf16)
    dense = lambda n: headwise_dense(_unshard(gw[n], 1)[0].astype(f32)).astype(bf16)
    w0 = dict(mix_norm=rep["mix_norm"][0:1],
              win=win_to_padded(_unshard(gw["hy_w_in"], 2)[0]).astype(bf16),
              ssd_conv_w=_unshard(gw["ssd_conv_w"], 2)[0], ssd_conv_b=rep["ssd_conv_b"],
              a_log=_lanes(rep["ssd_a_log"]), dt_bias=_lanes(rep["ssd_dt_bias"]), ssd_d=_lanes(rep["ssd_d"]),
              ssd_norm_w=rep["ssd_norm_w"], ml_conv_w=_unshard(gw["ml_conv_w"], 2)[0], ml_conv_b=rep["ml_conv_b"],
              wq=dense("ml_w_q"), wk=dense("ml_w_k"), wv=dense("ml_w_v"),
              wif_q=padn(wif[0:1024]), wif_k=padn(wif[1024:2048]), wif_v=padn(wif[2048:3072]),
              b_if=_lanes(rep["ml_b_if"]), ml_norm_w=rep["ml_norm_w"], ml_skip=rep["ml_skip"],
              wo1=wo[:D_MODEL], wo2=wo[D_MODEL:])
    return w0


def assemble_s5(gw, rep):
    bb, cc, pw = s5_tables(*[rep[n][0] for n in S5_PARAMS])
    wab = jnp.concatenate([_unshard(gw["s5_w_a"], 1)[0], _unshard(gw["s5_w_b"], 1)[0]], axis=1).astype(bf16)
    return dict(mix_norm=rep["mix_norm"][1:2], bb=bb, cc=cc, pw=pw,
                s5_d=_unshard(gw["s5_d"], 1), wab=wab, b_a=_unshard(gw["s5_b_a"], 1), b_b=_unshard(gw["s5_b_b"], 1))


def _shards(full, axis):
    return jnp.stack(jnp.split(full, N_DEV, axis=axis), axis=0)


def small_grads(g_norms, g_hy, g_s5, d_final, rep):
    small = dict(g_norms)
    small["mix_norm"] = jnp.concatenate([g_hy["mix_norm"], g_s5["mix_norm"]], axis=0)
    small["ssd_conv_w"] = g_hy["ssd_conv_w"][None]
    small["ssd_conv_b"] = g_hy["ssd_conv_b"]
    small["ssd_dt_bias"] = g_hy["dt_bias"][:, :SSD_HEADS]
    small["ssd_a_log"] = g_hy["a_log"][:, :SSD_HEADS]
    small["ssd_d"] = g_hy["ssd_d"][:, :SSD_HEADS]
    small["ssd_norm_w"] = g_hy["ssd_norm_w"]
    small["ml_conv_w"] = g_hy["ml_conv_w"][None]
    small["ml_conv_b"] = g_hy["ml_conv_b"]
    for nm, key in (("ml_w_q", "wq"), ("ml_w_k", "wk"), ("ml_w_v", "wv")):
        small[nm] = headwise_from_dense("diag_" + key, g_hy[key])[None]
    small["ml_w_if"] = g_hy["wif"][None, :, :2 * ML_HEADS]
    small["ml_b_if"] = g_hy["b_if"][:, :2 * ML_HEADS]
    small["ml_norm_w"] = g_hy["ml_norm_w"]
    small["ml_skip"] = g_hy["ml_skip"]
    small["final_norm"] = d_final.reshape(-1)
    return small


def s5_small_grads(g_s5, rep):
    small = {}
    _, tvjp = jax.vjp(s5_tables, *[rep[n][0] for n in S5_PARAMS])
    for n, g in zip(S5_PARAMS, tvjp((g_s5["bb"], g_s5["cc"], g_s5["pw"]))):
        small[n] = g[None]
    small["s5_d"] = g_s5["s5_d"]
    small["s5_b_a"] = g_s5["b_a"]
    small["s5_b_b"] = g_s5["b_b"]
    return small


ROW = 1024
F32_ROWS = 8


def _piece_rows(size):
    return -(-size // (ROW * F32_ROWS)) * F32_ROWS


def _pack(arrays):
    pieces = []
    for a in arrays:
        flat = a.astype(f32).reshape(-1)
        pieces.append(jnp.pad(flat, (0, _piece_rows(a.size) * ROW - a.size)).reshape(-1, ROW))
    return jnp.concatenate(pieces, axis=0)


def _unpack(buf, shapes):
    out, r0 = [], 0
    lead = buf.shape[:-2]
    for shp in shapes:
        size = math.prod(shp)
        r = _piece_rows(size)
        out.append(buf[..., r0:r0 + r, :].reshape(lead + (-1,))[..., :size].reshape(lead + tuple(shp)))
        r0 += r
    return out


def _tile_rows(r):
    for t in (512, 256, 128, 64, 32, 16, 8):
        if r % t == 0:
            return t
    return r


def _flat2d(a):
    return a.reshape(-1, a.shape[-1])


def kernel(x, ffn1_norm, ffn1_w_gate, ffn1_w_up, ffn1_w_down, mix_norm, ffn2_norm, ffn2_w_gate, ffn2_w_up, ffn2_w_down, hy_w_in, ssd_conv_w, ssd_conv_b, ssd_dt_bias, ssd_a_log, ssd_d, ssd_norm_w, ml_conv_w, ml_conv_b, ml_w_q, ml_w_k, ml_w_v, ml_w_if, ml_b_if, ml_norm_w, ml_skip, hy_w_out, s5_a_re, s5_a_im, s5_log_step, s5_b_re, s5_b_im, s5_c_re, s5_c_im, s5_d, s5_w_a, s5_b_a, s5_w_b, s5_b_b, final_norm, loss_target, m_ffn1_norm, m_ffn1_w_gate, m_ffn1_w_up, m_ffn1_w_down, m_mix_norm, m_ffn2_norm, m_ffn2_w_gate, m_ffn2_w_up, m_ffn2_w_down, m_hy_w_in, m_ssd_conv_w, m_ssd_conv_b, m_ssd_dt_bias, m_ssd_a_log, m_ssd_d, m_ssd_norm_w, m_ml_conv_w, m_ml_conv_b, m_ml_w_q, m_ml_w_k, m_ml_w_v, m_ml_w_if, m_ml_b_if, m_ml_norm_w, m_ml_skip, m_hy_w_out, m_s5_a_re, m_s5_a_im, m_s5_log_step, m_s5_b_re, m_s5_b_im, m_s5_c_re, m_s5_c_im, m_s5_d, m_s5_w_a, m_s5_b_a, m_s5_w_b, m_s5_b_b, m_final_norm, v_ffn1_norm, v_ffn1_w_gate, v_ffn1_w_up, v_ffn1_w_down, v_mix_norm, v_ffn2_norm, v_ffn2_w_gate, v_ffn2_w_up, v_ffn2_w_down, v_hy_w_in, v_ssd_conv_w, v_ssd_conv_b, v_ssd_dt_bias, v_ssd_a_log, v_ssd_d, v_ssd_norm_w, v_ml_conv_w, v_ml_conv_b, v_ml_w_q, v_ml_w_k, v_ml_w_v, v_ml_w_if, v_ml_b_if, v_ml_norm_w, v_ml_skip, v_hy_w_out, v_s5_a_re, v_s5_a_im, v_s5_log_step, v_s5_b_re, v_s5_b_im, v_s5_c_re, v_s5_c_im, v_s5_d, v_s5_w_a, v_s5_b_a, v_s5_w_b, v_s5_b_b, v_final_norm):
    given = dict(locals())
    w = {n: given[n] for n in WEIGHTS}
    mom = {n: given["m_" + n] for n in WEIGHTS}
    var = {n: given["v_" + n] for n in WEIGHTS}
    bl, seq, d = x.shape
    me = 4 * lax.axis_index("x") + 2 * lax.axis_index("y") + lax.axis_index("c")

    x0, tgt = x.reshape(bl * seq, d), loss_target.reshape(bl * seq, d)
    rep = {n: w[n] for n in WEIGHTS if n not in BIG and n not in SMALL_SHARDED}
    ffn_w = ("_w_gate", "_w_up", "_w_down")

    def ffn_gather(pre, l):
        return [(w[pre + s][l:l + 1].astype(bf16), "gather") for s in ffn_w]

    def scatter(parts):
        return [(p, "scatter") for p in parts]

    wf10 = tuple(exchange("gather_ffn1_l0", ffn_gather("ffn1", 0)))
    mixer_ops = [(w[n].astype(bf16), "gather") for n in ("hy_w_in", "hy_w_out")]
    mixer_ops.append((_pack([w[n] for n in SMALL_SHARDED]), "gather"))
    x1, *rest = ffn_fwd(x0, ffn1_norm[0:1], *wf10, ride=mixer_ops)
    sv10, got = rest[:3], rest[3:]
    gw = dict(zip(("hy_w_in", "hy_w_out"), got[:2]))
    gw.update(zip(SMALL_SHARDED, _unpack(got[2], [w[n].shape for n in SMALL_SHARDED])))
    w0 = assemble_hybrid(gw, rep)
    x2, sv_h, wf20, got, wf11 = hybrid_fwd(x1, w0, seq, ride_in=ffn_gather("ffn2", 0),
                                           ride_ssd=[(w[n].astype(bf16), "gather") for n in ("s5_w_a", "s5_w_b")],
                                           ride_ml=ffn_gather("ffn1", 1))
    gw.update(zip(("s5_w_a", "s5_w_b"), got))
    w1 = assemble_s5(gw, rep)
    x3, *rest = ffn_fwd(x2, ffn2_norm[0:1], *wf20, ride=ffn_gather("ffn2", 1))
    sv20, wf21 = rest[:3], tuple(rest[3:])
    x4, *sv11 = ffn_fwd(x3, ffn1_norm[1:2], *wf11)
    x5, sv_s = s5_layer_fwd(x4, w1, seq)
    x6, *sv21 = ffn_fwd(x5, ffn2_norm[1:2], *wf21)
    loss, dx6, d_final = loss_head(x6, final_norm.reshape(1, d), tgt)

    dx5, dn21, dw21, _, _ = ffn_step_bwd(dx6, x5, ffn2_norm[1:2], wf21, sv21)
    dx4, g_s5 = s5_layer_bwd(dx5, x4, w1, sv_s, seq)
    dwab = g_s5.pop("wab")
    s5_ops = scatter([_shards(dwab[None, :, :D_MODEL], 1).astype(bf16), _shards(dwab[None, :, D_MODEL:], 1).astype(bf16)])
    dx3, dn11, dw11, p21, p_s5 = ffn_step_bwd(dx4, x3, ffn1_norm[1:2], wf11, sv11, ride_act=scatter(dw21), ride_w=s5_ops)
    small = s5_small_grads(g_s5, rep)
    dx2, dn20, dw20, p11, (parts_s5,) = ffn_step_bwd(dx3, x2, ffn2_norm[0:1], wf20, sv20, ride_act=scatter(dw11),
                                                      ride_w=[(_pack([small[n] for n in SMALL_S5]), "gather")])
    dx1, g_hy, p20 = hybrid_bwd(dx2, x1, w0, sv_h, seq, ride_ml=scatter(dw20))
    hy_ops = scatter([_shards(win_from_padded(g_hy.pop("win"))[None], 2).astype(bf16), _shards(g_hy.pop("wo")[None], 1).astype(bf16)])
    h10, g10, u10 = sv10
    dx0, dn10, dg, du, a, dyh, *p_hy = ffn_bwd_act(dx1, x0, ffn1_norm[0:1], g10, u10, *wf10, ride=hy_ops)
    g_norms = {"ffn1_norm": jnp.concatenate([dn10, dn11], axis=0), "ffn2_norm": jnp.concatenate([dn20, dn21], axis=0)}
    small.update(small_grads(g_norms, g_hy, g_s5, d_final, rep))
    *dw10, parts_rest = ffn_bwd_w(h10, dyh, dg, du, a, ride=[(_pack([small[n] for n in SMALL_REST]), "gather")])
    p10 = exchange("reduce_tail", scatter(dw10))
    small_parts = jnp.concatenate([parts_rest, parts_s5], axis=1)
    small_sum = sum_parts("sum_small", small_parts, tr=_tile_rows(small_parts.shape[1]))

    out_g, out_d, out_m, out_v = {}, {}, {}, {}
    ffn_parts = {"ffn1": (p10, p11), "ffn2": (p20, p21)}
    for pre in ("ffn1", "ffn2"):
        for k, s in enumerate(ffn_w):
            n = pre + s
            r, c = w[n].shape[1:]
            res = None
            for l in (1, 0):
                res = adamw_layer("adamw_" + n, ffn_parts[pre][l][k].reshape(N_DEV, r, c), w[n], mom[n], var[n], l, res,
                                  tr=_tile_rows(r))
            out_g[n], out_d[n], out_m[n], out_v[n] = res
    for n, parts in zip(("hy_w_in", "hy_w_out", "s5_w_a", "s5_w_b"), tuple(p_hy) + tuple(p_s5)):
        shp = w[n].shape
        w2 = _flat2d(w[n])
        res = adamw("adamw_" + n, parts.reshape((N_DEV,) + w2.shape), w2, _flat2d(mom[n]), _flat2d(var[n]),
                    tr=_tile_rows(w2.shape[0]))
        out_g[n], out_d[n], out_m[n], out_v[n] = [a.reshape(shp) for a in res]
    g_small = {}
    for n, full in zip(SMALL, _unpack(small_sum, [small[n].shape for n in SMALL])):
        if n in SMALL_SHARDED:
            ax = SMALL_SHARDED[n]
            full = lax.dynamic_slice_in_dim(full, me * w[n].shape[ax], w[n].shape[ax], axis=ax)
        g_small[n] = full
    packs = [_pack([t[n] for n in SMALL]) for t in (g_small, w, mom, var)]
    res = adamw("adamw_small", packs[0][None], packs[1], packs[2], packs[3], tr=_tile_rows(packs[0].shape[0]))
    for dst, a in zip((out_g, out_d, out_m, out_v), res):
        dst.update(zip(SMALL, _unpack(a, [w[n].shape for n in SMALL])))

    total = lax.psum(loss[0, 0], ("x", "y", "c"))
    return (total, dx0.reshape(bl, seq, d), *[out_g[n] for n in WEIGHTS], *[out_d[n] for n in WEIGHTS],
            *[out_m[n] for n in WEIGHTS], *[out_v[n] for n in WEIGHTS])
```

```python
import functools
import math

import jax
import jax.numpy as jnp
from jax import lax
from jax.experimental import pallas as pl
from jax.experimental.pallas import tpu as pltpu

f32 = jnp.float32
bf16 = jnp.bfloat16

N_DEV = 8
D_MODEL = 1024
D_FF = 2816
EPS = 1e-6
FFN_RES = 0.5
CONV_W = 4
SSD_HEADS = 16
SSD_HEAD_DIM = 64
SSD_GROUPS = 2
SSD_STATE = 128
SSD_HG = SSD_HEADS // SSD_GROUPS
SSD_GW = SSD_HG * SSD_HEAD_DIM
CHUNK = 128
ML_HEADS = 4
ML_HD = 256
S5_GROUP = 16
S5_GROUPS = 64
S5_STATE = 64
S5_CB = 8
S5_CH = (S5_GROUPS // S5_CB) * S5_STATE
S5_TL = 256
S5_SUB = 16
LANES = 128
IN_COLS = 4624
PROJ_W = 4864
OFF_Z, OFF_MX, OFF_MZ, OFF_XBC, OFF_DT = 0, 1024, 2048, 3072, 4608
ADAM_LR, ADAM_B1, ADAM_B2, ADAM_EPS, ADAM_WD, ADAM_STEP = 0.001, 0.9, 0.999, 1e-08, 0.01, 10
NEG = -1e30
VMEM_LIMIT = 56 * 1024 * 1024
HI = lax.Precision.HIGHEST


def _cp(n):
    return pltpu.CompilerParams(dimension_semantics=("arbitrary",) * n, vmem_limit_bytes=VMEM_LIMIT)


def _dg(a, b, ca, cb):
    return lax.dot_general(a.astype(bf16), b.astype(bf16), (((ca,), (cb,)), ((), ())), preferred_element_type=f32)


@functools.partial(jax.custom_vjp, nondiff_argnums=(2, 3))
def bdot(a, b, ca, cb):
    return _dg(a, b, ca, cb)


def _bdot_fwd(a, b, ca, cb):
    return _dg(a, b, ca, cb), (a, b)


def _bdot_bwd(ca, cb, res, ct):
    a, b = res
    da = _dg(ct, b, 1, 1 - cb) if ca == 1 else _dg(b, ct, 1 - cb, 1)
    db = _dg(a, ct, 1 - ca, 0) if cb == 0 else _dg(ct, a, 0, 1 - ca)
    return da, db


bdot.defvjp(_bdot_fwd, _bdot_bwd)


def hdot(a, b):
    return jnp.dot(a, b, precision=HI, preferred_element_type=f32)


def _iota(shape, dim):
    return lax.broadcasted_iota(jnp.int32, shape, dim)


def _tri(n):
    return (_iota((n, n), 0) >= _iota((n, n), 1))


@functools.partial(jax.custom_vjp, nondiff_argnums=(1,))
def tshift(x, k):
    return jnp.where(_iota(x.shape, 0) >= k, pltpu.roll(x, k, 0), 0.0)


def _tshift_fwd(x, k):
    return tshift(x, k), None


def _tshift_bwd(k, _, ct):
    n = ct.shape[0]
    return (jnp.where(_iota(ct.shape, 0) < n - k, pltpu.roll(ct, n - k, 0), 0.0),)


tshift.defvjp(_tshift_fwd, _tshift_bwd)


def _lane_pick(a, idx):
    return jnp.sum(jnp.where(_iota(a.shape, 1) == idx, a, 0.0), axis=1, keepdims=True)


def _row_pick(a, idx):
    return jnp.sum(jnp.where(_iota(a.shape, 0) == idx, a, 0.0), axis=0, keepdims=True)


def _silu(x):
    return x * jax.nn.sigmoid(x)


def map_fwd(name, f, grid, ins, in_specs, out_shapes, out_specs):
    n_in = len(ins)

    def body(*refs):
        pids = tuple(pl.program_id(i) for i in range(len(grid)))
        outs = f(pids, *[r[...] for r in refs[:n_in]])
        for r, o in zip(refs[n_in:], outs):
            r[...] = o.astype(r.dtype)

    return pl.pallas_call(body, name=name, grid=grid, in_specs=in_specs, out_specs=out_specs,
                          out_shape=out_shapes, compiler_params=_cp(len(grid)))(*ins)


def scan_fwd(name, f, grid, slot_axis, ins, in_specs, out_shapes, out_specs, state_shapes, state_init, save_shapes, save_specs,
             ride=None):
    n_in, n_out, n_st = len(ins), len(out_shapes), len(state_shapes)
    n_slots = grid[slot_axis]
    cax = len(grid) - 1 if slot_axis != len(grid) - 1 else len(grid) - 2
    rider = Rider(ride)
    nr = rider.n

    def body(*refs):
        pids = tuple(pl.program_id(i) for i in range(len(grid)))
        in_refs, r_ins = refs[:n_in], refs[n_in:n_in + nr]
        o0 = n_in + nr
        out_refs, save_refs = refs[o0:o0 + n_out], refs[o0 + n_out:o0 + n_out + n_st]
        r_outs = refs[o0 + n_out + n_st:o0 + n_out + n_st + nr]
        st_refs = refs[o0 + n_out + n_st + nr:o0 + n_out + 2 * n_st + nr]
        sems = refs[o0 + n_out + 2 * n_st + nr:]
        rider.start(grid, r_ins, r_outs, sems)
        slot = pids[slot_axis]

        @pl.when(pids[cax] == 0)
        def _():
            for s, init in zip(st_refs, state_init):
                s[slot] = jnp.full(s.shape[1:], init, f32)

        states = tuple(s[slot] for s in st_refs)
        for sv, st in zip(save_refs, states):
            sv[...] = st.reshape(sv.shape)
        outs, new = f(pids, states, *[r[...] for r in in_refs])
        for r, o in zip(out_refs, outs):
            r[...] = o.astype(r.dtype)
        for s, v in zip(st_refs, new):
            s[slot] = v
        rider.wait(grid, r_ins, r_outs, sems)

    scratch = [pltpu.VMEM((n_slots,) + tuple(s), f32) for s in state_shapes]
    return pl.pallas_call(body, name=name, grid=grid, in_specs=list(in_specs) + rider.specs(),
                          out_specs=list(out_specs) + list(save_specs) + rider.specs(),
                          out_shape=list(out_shapes) + list(save_shapes) + rider.out_shapes(),
                          scratch_shapes=scratch + rider.scratch(), compiler_params=_cp(len(grid)))(*ins, *rider.arrays())


def scan_bwd(name, f, grid, slot_axis, ins, in_specs, saves, save_specs, cts, ct_specs, state_shapes, wrt, acc_first):
    n_in, n_st, n_ct = len(ins), len(saves), len(cts)
    n_slots = grid[slot_axis]
    cax = len(grid) - 1 if slot_axis != len(grid) - 1 else len(grid) - 2

    def body(*refs):
        pids = tuple(pl.program_id(i) for i in range(len(grid)))
        in_refs = refs[:n_in]
        save_refs = refs[n_in:n_in + n_st]
        ct_refs = refs[n_in + n_st:n_in + n_st + n_ct]
        out_refs = refs[n_in + n_st + n_ct:n_in + n_st + n_ct + len(wrt)]
        dst_refs = refs[n_in + n_st + n_ct + len(wrt):]
        slot = pids[slot_axis]

        @pl.when(pids[cax] == 0)
        def _():
            for s in dst_refs:
                s[slot] = jnp.zeros(s.shape[1:], f32)

        vals = [r[...] for r in in_refs]
        states = tuple(sv[...].reshape(shp) for sv, shp in zip(save_refs, state_shapes))
        ctv = tuple(r[...].astype(f32) for r in ct_refs)
        dnew = tuple(s[slot] for s in dst_refs)

        def g(st, *dv):
            full = list(vals)
            for i, v in zip(wrt, dv):
                full[i] = v
            outs, new = f(pids, st, *full)
            return tuple(outs), tuple(new)

        _, vjp = jax.vjp(g, states, *[vals[i] for i in wrt])
        grads = vjp((ctv, dnew))
        for s, v in zip(dst_refs, grads[0]):
            s[slot] = v
        for i, o_ref, gr in zip(wrt, out_refs, grads[1:]):
            first = acc_first.get(i)
            if first is None:
                o_ref[...] = gr.astype(o_ref.dtype)
            else:
                @pl.when(first(pids))
                def _():
                    o_ref[...] = jnp.zeros_like(o_ref)
                o_ref[...] += gr

    out_shapes = [jax.ShapeDtypeStruct(ins[i].shape, f32) for i in wrt]
    out_specs = [in_specs[i] for i in wrt]
    scratch = [pltpu.VMEM((n_slots,) + tuple(s), f32) for s in state_shapes]
    return pl.pallas_call(body, name=name, grid=grid, in_specs=list(in_specs) + list(save_specs) + list(ct_specs),
                          out_specs=out_specs, out_shape=out_shapes, scratch_shapes=scratch,
                          compiler_params=_cp(len(grid)))(*ins, *saves, *cts)


def _fit(dim, cap):
    if dim <= cap:
        return dim
    return max(t for t in range(LANES, cap + 1, LANES) if dim % t == 0)


def _matmul_tiles(m, n, kdim, ca):
    if ca == 1:
        return _fit(m, 512), _fit(n, 2432), _fit(kdim, 2432)
    return _fit(m, 1024), _fit(n, 1280), _fit(kdim, 512)


def matmul(name, a, b, ca=1, cb=0, add=None, out_dtype=f32, a_off=0, a_width=None, ride=None, tiles=None):
    rider = Rider(ride)
    nr = rider.n
    a_width = a.shape[1] if a_width is None else a_width
    kdim = b.shape[cb]
    n = b.shape[1 - cb]
    m = a.shape[0] if ca == 1 else a_width
    tm, tn, tk = tiles or _matmul_tiles(m, n, kdim, ca)
    assert m % tm == 0 and n % tn == 0 and kdim % tk == 0
    nk = kdim // tk
    if ca == 1:
        assert a_off % tk == 0 and a_width == kdim
        koff = a_off // tk
        a_spec = pl.BlockSpec((tm, tk), lambda i, j, k: (i, k + koff))
    else:
        assert a_off % tm == 0 and a.shape[0] == kdim
        ioff = a_off // tm
        a_spec = pl.BlockSpec((tk, tm), lambda i, j, k: (k, i + ioff))
    b_spec = pl.BlockSpec((tk, tn), lambda i, j, k: (k, j)) if cb == 0 else pl.BlockSpec((tn, tk), lambda i, j, k: (j, k))
    o_spec = pl.BlockSpec((tm, tn), lambda i, j, k: (i, j))
    has_add = add is not None

    n_in = 3 if has_add else 2
    grid = (m // tm, n // tn, nk)

    def body(*refs):
        a_ref, b_ref = refs[0], refs[1]
        add_ref = refs[2] if has_add else None
        r_ins, o_ref = refs[n_in:n_in + nr], refs[n_in + nr]
        r_outs, acc, sems = refs[n_in + nr + 1:n_in + 2 * nr + 1], refs[n_in + 2 * nr + 1], refs[n_in + 2 * nr + 2:]
        rider.start(grid, r_ins, r_outs, sems)
        k = pl.program_id(2)

        @pl.when(k == 0)
        def _():
            acc[...] = add_ref[...].astype(f32) if has_add else jnp.zeros_like(acc)

        acc[...] += _dg(a_ref[...], b_ref[...], ca, cb)

        @pl.when(k == nk - 1)
        def _():
            o_ref[...] = acc[...].astype(o_ref.dtype)

        rider.wait(grid, r_ins, r_outs, sems)

    ins = [a, b] + ([add] if has_add else [])
    specs = [a_spec, b_spec] + ([o_spec] if has_add else [])
    res = pl.pallas_call(body, name=name, grid=grid, in_specs=specs + rider.specs(), out_specs=[o_spec] + rider.specs(),
                         out_shape=[jax.ShapeDtypeStruct((m, n), out_dtype)] + rider.out_shapes(),
                         scratch_shapes=[pltpu.VMEM((tm, tn), f32)] + rider.scratch(), compiler_params=_cp(3))(*ins, *rider.arrays())
    return res if nr else res[0]


def f_rms(pids, x, w):
    r = lax.rsqrt(jnp.mean(x * x, axis=-1, keepdims=True) + EPS)
    return (x * r * w,)


def _row_spec(tm, width, col=0):
    return pl.BlockSpec((tm, width), lambda i: (i, col))


def _par_spec(shape):
    return pl.BlockSpec(shape, lambda *p: (0,) * len(shape))


def rms_fwd(x, w, tm=512):
    t, d = x.shape
    return map_fwd("rms_fwd", f_rms, (t // tm,), [x, w], [_row_spec(tm, d), _par_spec((1, d))],
                   [jax.ShapeDtypeStruct((t, d), f32)], [_row_spec(tm, d)])[0]


def rms_bwd(dys, x, w, dres, tm=512):
    t, d = x.shape
    n = len(dys)

    def body(*refs):
        x_ref, w_ref, dres_ref, dx_ref, dw_ref = refs[n:]
        dy = refs[0][...]
        for r in refs[1:n]:
            dy = dy + r[...]
        _, vjp = jax.vjp(lambda xx, ww: f_rms(None, xx, ww)[0], x_ref[...], w_ref[...])
        dx, dw = vjp(dy)
        dx_ref[...] = dx + dres_ref[...]

        @pl.when(pl.program_id(0) == 0)
        def _():
            dw_ref[...] = jnp.zeros_like(dw_ref)
        dw_ref[...] += dw

    return pl.pallas_call(body, name="rms_bwd", grid=(t // tm,),
                          in_specs=[_row_spec(tm, d)] * (n + 1) + [_par_spec((1, d)), _row_spec(tm, d)],
                          out_specs=[_row_spec(tm, d), _par_spec((1, d))],
                          out_shape=[jax.ShapeDtypeStruct((t, d), f32), jax.ShapeDtypeStruct((1, d), f32)],
                          compiler_params=_cp(1))(*dys, x, w, dres)


def loss_head(x, w, tgt, tm=512):
    t, d = x.shape

    def fl(xx, ww, tt):
        y = f_rms(None, xx, ww)[0]
        return 0.5 * jnp.sum(jnp.mean(jnp.square(y - tt), axis=-1, keepdims=True), axis=0, keepdims=True)

    def body(x_ref, w_ref, t_ref, loss_ref, dx_ref, dw_ref):
        val, vjp = jax.vjp(lambda xx, ww: fl(xx, ww, t_ref[...]), x_ref[...], w_ref[...])
        dx, dw = vjp(jnp.ones((1, 1), f32))
        dx_ref[...] = dx

        @pl.when(pl.program_id(0) == 0)
        def _():
            dw_ref[...] = jnp.zeros_like(dw_ref)
            loss_ref[...] = jnp.zeros_like(loss_ref)
        dw_ref[...] += dw
        loss_ref[...] += val

    return pl.pallas_call(body, name="loss_head", grid=(t // tm,),
                          in_specs=[_row_spec(tm, d), _par_spec((1, d)), _row_spec(tm, d)],
                          out_specs=[_par_spec((1, 1)), _row_spec(tm, d), _par_spec((1, d))],
                          out_shape=[jax.ShapeDtypeStruct((1, 1), f32), jax.ShapeDtypeStruct((t, d), f32),
                                     jax.ShapeDtypeStruct((1, d), f32)],
                          compiler_params=_cp(1))(x, w, tgt)


def ffn_fwd(x, nw, wg, wu, wd, tm=1024, ride=None):
    t, d = x.shape
    ns, _, _, fs = wg.shape
    rider = Rider(ride)
    nr = rider.n
    grid = (t // tm, ns)

    def body(*refs):
        x_ref, nw_ref, wg_ref, wu_ref, wd_ref = refs[:5]
        r_ins = refs[5:5 + nr]
        xo_ref, h_ref, g_ref, u_ref = refs[5 + nr:9 + nr]
        r_outs, acc, sems = refs[9 + nr:9 + 2 * nr], refs[9 + 2 * nr], refs[10 + 2 * nr:]
        rider.start(grid, r_ins, r_outs, sems)
        j = pl.program_id(1)

        @pl.when(j == 0)
        def _():
            h_ref[...] = f_rms(None, x_ref[...], nw_ref[...])[0].astype(bf16)
            acc[...] = jnp.zeros_like(acc)

        h = h_ref[...]
        g = jnp.dot(h, wg_ref[0, 0], preferred_element_type=f32)
        u = jnp.dot(h, wu_ref[0, 0], preferred_element_type=f32)
        g_ref[0] = g
        u_ref[0] = u
        acc[...] += jnp.dot((_silu(g) * u).astype(bf16), wd_ref[0, 0], preferred_element_type=f32)

        @pl.when(j == ns - 1)
        def _():
            xo_ref[...] = x_ref[...] + FFN_RES * acc[...]

        rider.wait(grid, r_ins, r_outs, sems)

    row = pl.BlockSpec((tm, d), lambda i, j: (i, 0))
    wcol = pl.BlockSpec((1, 1, d, fs), lambda i, j: (j, 0, 0, 0))
    wrow = pl.BlockSpec((1, 1, fs, d), lambda i, j: (j, 0, 0, 0))
    act = pl.BlockSpec((1, tm, fs), lambda i, j: (j, i, 0))
    return pl.pallas_call(body, name="ffn_fwd", grid=grid,
                          in_specs=[row, pl.BlockSpec((1, d), lambda i, j: (0, 0)), wcol, wcol, wrow] + rider.specs(),
                          out_specs=[row, row, act, act] + rider.specs(),
                          out_shape=[jax.ShapeDtypeStruct((t, d), f32), jax.ShapeDtypeStruct((t, d), bf16),
                                     jax.ShapeDtypeStruct((ns, t, fs), f32), jax.ShapeDtypeStruct((ns, t, fs), f32)]
                          + rider.out_shapes(),
                          scratch_shapes=[pltpu.VMEM((tm, d), f32)] + rider.scratch(),
                          compiler_params=_cp(2))(x, nw, wg, wu, wd, *rider.arrays())


def ffn_bwd_act(dy, x, nw, g, u, wg, wu, wd, tm=512, ride=None):
    t, d = x.shape
    ns, _, _, fs = wg.shape
    rider = Rider(ride)
    nr = rider.n
    grid = (t // tm, ns)

    def body(*refs):
        dy_ref, x_ref, nw_ref, g_ref, u_ref, wg_ref, wu_ref, wd_ref = refs[:8]
        r_ins = refs[8:8 + nr]
        dx_ref, dnw_ref, dg_ref, du_ref, a_ref, dyh_ref = refs[8 + nr:14 + nr]
        r_outs, acc, sems = refs[14 + nr:14 + 2 * nr], refs[14 + 2 * nr], refs[15 + 2 * nr:]
        rider.start(grid, r_ins, r_outs, sems)
        i, j = pl.program_id(0), pl.program_id(1)

        @pl.when(j == 0)
        def _():
            acc[...] = jnp.zeros_like(acc)
            dyh_ref[...] = (FFN_RES * dy_ref[...]).astype(bf16)

        dyh = dyh_ref[...]
        da = _dg(dyh, wd_ref[0, 0], 1, 1)
        gg, uu = g_ref[0], u_ref[0]
        sg = jax.nn.sigmoid(gg)
        si = gg * sg
        dgv = (da * uu * (sg * (1.0 + gg * (1.0 - sg)))).astype(bf16)
        duv = (da * si).astype(bf16)
        dg_ref[0] = dgv
        du_ref[0] = duv
        a_ref[0] = (si * uu).astype(bf16)
        acc[...] += _dg(dgv, wg_ref[0, 0], 1, 1) + _dg(duv, wu_ref[0, 0], 1, 1)

        @pl.when(j == ns - 1)
        def _():
            _, vjp = jax.vjp(lambda xx, ww: f_rms(None, xx, ww)[0], x_ref[...], nw_ref[...])
            dx, dw = vjp(acc[...])
            dx_ref[...] = dx + dy_ref[...]

            @pl.when(i == 0)
            def _():
                dnw_ref[...] = jnp.zeros_like(dnw_ref)
            dnw_ref[...] += dw

        rider.wait(grid, r_ins, r_outs, sems)

    row = pl.BlockSpec((tm, d), lambda i, j: (i, 0))
    wcol = pl.BlockSpec((1, 1, d, fs), lambda i, j: (j, 0, 0, 0))
    wrow = pl.BlockSpec((1, 1, fs, d), lambda i, j: (j, 0, 0, 0))
    act = pl.BlockSpec((1, tm, fs), lambda i, j: (j, i, 0))
    par = pl.BlockSpec((1, d), lambda i, j: (0, 0))
    return pl.pallas_call(body, name="ffn_bwd_act", grid=grid,
                          in_specs=[row, row, par, act, act, wcol, wcol, wrow] + rider.specs(),
                          out_specs=[row, par, act, act, act, row] + rider.specs(),
                          out_shape=[jax.ShapeDtypeStruct((t, d), f32), jax.ShapeDtypeStruct((1, d), f32)]
                          + [jax.ShapeDtypeStruct((ns, t, fs), bf16)] * 3 + [jax.ShapeDtypeStruct((t, d), bf16)]
                          + rider.out_shapes(),
                          scratch_shapes=[pltpu.VMEM((tm, d), f32)] + rider.scratch(),
                          compiler_params=_cp(2))(dy, x, nw, g, u, wg, wu, wd, *rider.arrays())


def ffn_bwd_w(h, dyh, dg, du, a, tk=1024, ride=None):
    t, d = h.shape
    ns, _, fs = dg.shape
    nk = t // tk
    rider = Rider(ride)
    nr = rider.n
    grid = (ns, nk)

    def body(*refs):
        h_ref, dy_ref, dg_ref, du_ref, a_ref = refs[:5]
        r_ins = refs[5:5 + nr]
        og, ou, od = refs[5 + nr:8 + nr]
        r_outs = refs[8 + nr:8 + 2 * nr]
        ag, au, ad = refs[8 + 2 * nr:11 + 2 * nr]
        sems = refs[11 + 2 * nr:]
        rider.start(grid, r_ins, r_outs, sems)
        k = pl.program_id(1)

        @pl.when(k == 0)
        def _():
            ag[...] = jnp.zeros_like(ag)
            au[...] = jnp.zeros_like(au)
            ad[...] = jnp.zeros_like(ad)

        hh = h_ref[...]
        ag[...] += _dg(hh, dg_ref[0], 0, 0)
        au[...] += _dg(hh, du_ref[0], 0, 0)
        ad[...] += _dg(a_ref[0], dy_ref[...], 0, 0)

        @pl.when(k == nk - 1)
        def _():
            og[0, 0] = ag[...].astype(og.dtype)
            ou[0, 0] = au[...].astype(ou.dtype)
            od[0, 0] = ad[...].astype(od.dtype)

        rider.wait(grid, r_ins, r_outs, sems)

    row = pl.BlockSpec((tk, d), lambda j, k: (k, 0))
    act = pl.BlockSpec((1, tk, fs), lambda j, k: (j, k, 0))
    wcol = pl.BlockSpec((1, 1, d, fs), lambda j, k: (j, 0, 0, 0))
    wrow = pl.BlockSpec((1, 1, fs, d), lambda j, k: (j, 0, 0, 0))
    return pl.pallas_call(body, name="ffn_bwd_w", grid=grid, in_specs=[row, row, act, act, act] + rider.specs(),
                          out_specs=[wcol, wcol, wrow] + rider.specs(),
                          out_shape=[jax.ShapeDtypeStruct((ns, 1, d, fs), bf16)] * 2
                          + [jax.ShapeDtypeStruct((ns, 1, fs, d), bf16)] + rider.out_shapes(),
                          scratch_shapes=[pltpu.VMEM((d, fs), f32), pltpu.VMEM((d, fs), f32), pltpu.VMEM((fs, d), f32)]
                          + rider.scratch(),
                          compiler_params=_cp(2))(h, dyh, dg, du, a, *rider.arrays())


def f_conv(pids, x, w, b):
    y = b + x * w[CONV_W - 1:CONV_W, :]
    for j in range(CONV_W - 1):
        y = y + tshift(x, CONV_W - 1 - j) * w[j:j + 1, :]
    return (_silu(y),)


def _conv_specs(seq, col0, cb):
    xs = pl.BlockSpec((seq, cb), lambda c, b: (b, col0 + c))
    ws = pl.BlockSpec((CONV_W, cb), lambda c, b: (0, c))
    bs = pl.BlockSpec((1, cb), lambda c, b: (0, c))
    ys = pl.BlockSpec((seq, cb), lambda c, b: (b, c))
    return xs, ws, bs, ys


def conv_fwd(name, src, col_off, w, b, seq, cb=256):
    t = src.shape[0]
    c = w.shape[1]
    xs, ws, bs, ys = _conv_specs(seq, col_off // cb, cb)
    return map_fwd(name, f_conv, (c // cb, t // seq), [src, w, b], [xs, ws, bs],
                   [jax.ShapeDtypeStruct((t, c), f32)], [ys])[0]


def conv_bwd(name, dy, src, col_off, w, b, seq, cb=256):
    t = src.shape[0]
    c = w.shape[1]
    xs, ws, bs, ys = _conv_specs(seq, col_off // cb, cb)

    def body(x_ref, w_ref, b_ref, dy_ref, dx_ref, dw_ref, db_ref):
        _, vjp = jax.vjp(lambda xx, ww, bb: f_conv(None, xx, ww, bb)[0], x_ref[...], w_ref[...], b_ref[...])
        dx, dw, db = vjp(dy_ref[...])
        dx_ref[...] = dx

        @pl.when(pl.program_id(1) == 0)
        def _():
            dw_ref[...] = jnp.zeros_like(dw_ref)
            db_ref[...] = jnp.zeros_like(db_ref)
        dw_ref[...] += dw
        db_ref[...] += db

    return pl.pallas_call(body, name=name, grid=(c // cb, t // seq), in_specs=[xs, ws, bs, ys], out_specs=[ys, ws, bs],
                          out_shape=[jax.ShapeDtypeStruct((t, c), f32), jax.ShapeDtypeStruct(w.shape, f32),
                                     jax.ShapeDtypeStruct(b.shape, f32)], compiler_params=_cp(2))(src, w, b, dy)


def f_ssd(pids, states, xs, dtraw, bm, cm, a_log, dt_bias, d_skip):
    g = pids[2]
    (hn,) = states
    l = xs.shape[0]
    head_of_lane = _iota((LANES, SSD_GW), 1) // SSD_HEAD_DIM + SSD_HG * g
    expand = (_iota((LANES, SSD_GW), 0) == head_of_lane).astype(f32)
    tri = _tri(l)
    dt = jax.nn.softplus(dtraw + dt_bias)
    adt = dt * (-jnp.exp(a_log))
    cs = hdot(tri.astype(f32), adt)
    cst = cs.T
    cs_last = cs[l - 1:l, :]
    dt_e, cs_e, csl_e = hdot(dt, expand), hdot(cs, expand), hdot(cs_last, expand)
    xd = xs * dt_e
    gmat = bdot(cm, bm, 1, 1)
    half = _iota((l, LANES), 1) < SSD_HEAD_DIM
    blocks = []
    for pair in range(SSD_HG // 2):
        xb = xd[:, pair * LANES:(pair + 1) * LANES]
        res = []
        for sub in range(2):
            hid = SSD_HG * g + 2 * pair + sub
            col, row = _lane_pick(cs, hid), _row_pick(cst, hid)
            lm = jnp.exp(jnp.where(tri, col - row, NEG))
            res.append(bdot(gmat * lm, xb, 1, 0))
        blocks.append(jnp.where(half, res[0], res[1]))
    y = jnp.concatenate(blocks, axis=1)
    y = y + jnp.exp(cs_e) * bdot(cm, hn, 1, 0)
    y = y + hdot(d_skip, expand) * xs
    hn_new = jnp.exp(csl_e) * hn + bdot(bm, jnp.exp(csl_e - cs_e) * xd, 0, 0)
    return (y,), (hn_new,)


def _ssd_specs(seq, nch, rev):
    cc = (lambda c: nch - 1 - c) if rev else (lambda c: c)
    xs = pl.BlockSpec((CHUNK, SSD_GW), lambda b, c, g: (b * nch + cc(c), g))
    dt = pl.BlockSpec((CHUNK, LANES), lambda b, c, g: (b * nch + cc(c), OFF_DT // LANES))
    bm = pl.BlockSpec((CHUNK, SSD_STATE), lambda b, c, g: (b * nch + cc(c), 1024 // SSD_STATE + g))
    cm = pl.BlockSpec((CHUNK, SSD_STATE), lambda b, c, g: (b * nch + cc(c), 1024 // SSD_STATE + SSD_GROUPS + g))
    par = pl.BlockSpec((1, LANES), lambda b, c, g: (0, 0))
    sv = pl.BlockSpec((1, 1, SSD_STATE, SSD_GW), lambda b, c, g: (b * nch + cc(c), g, 0, 0))
    ddt = pl.BlockSpec((CHUNK, LANES), lambda b, c, g: (b * nch + cc(c), 0))
    dbc = pl.BlockSpec((CHUNK, SSD_STATE), lambda b, c, g: (b * nch + cc(c), g))
    return xs, dt, bm, cm, par, sv, ddt, dbc


def ssd_fwd(xbc, proj, a_log, dt_bias, d_skip, seq, ride=None):
    t = xbc.shape[0]
    nch = seq // CHUNK
    xs, dt, bm, cm, par, sv, _, _ = _ssd_specs(seq, nch, False)
    grid = (t // seq, nch, SSD_GROUPS)
    y, hsave, *got = scan_fwd("ssd_fwd", f_ssd, grid, 2, [xbc, proj, xbc, xbc, a_log, dt_bias, d_skip],
                              [xs, dt, bm, cm, par, par, par], [jax.ShapeDtypeStruct((t, SSD_GROUPS * SSD_GW), f32)], [xs],
                              [(SSD_STATE, SSD_GW)], [0.0],
                              [jax.ShapeDtypeStruct((t // CHUNK, SSD_GROUPS, SSD_STATE, SSD_GW), f32)], [sv], ride=ride)
    return y, hsave, got


def ssd_bwd(dy, xbc, proj, a_log, dt_bias, d_skip, hsave, seq):
    t = xbc.shape[0]
    nch = seq // CHUNK
    xs, dt, bm, cm, par, sv, ddt, dbc = _ssd_specs(seq, nch, True)
    grid = (t // seq, nch, SSD_GROUPS)

    def body(x_ref, dt_ref, b_ref, c_ref, al_ref, db_ref, ds_ref, h_ref, dy_ref,
             dxbc_x, dxbc_b, dxbc_c, ddt_ref, dal_ref, ddb_ref, dds_ref, dst):
        pids = tuple(pl.program_id(i) for i in range(3))
        slot = pids[2]

        @pl.when(pids[1] == 0)
        def _():
            dst[slot] = jnp.zeros(dst.shape[1:], f32)

        vals = [x_ref[...], dt_ref[...], b_ref[...], c_ref[...], al_ref[...], db_ref[...], ds_ref[...]]

        def gfun(st, *v):
            outs, new = f_ssd(pids, (st,), *v)
            return outs[0], new[0]

        _, vjp = jax.vjp(gfun, h_ref[0, 0], *vals)
        grads = vjp((dy_ref[...], dst[slot]))
        dst[slot] = grads[0]
        dxbc_x[...] = grads[1]
        dxbc_b[...] = grads[3]
        dxbc_c[...] = grads[4]

        @pl.when(slot == 0)
        def _():
            ddt_ref[...] = jnp.zeros_like(ddt_ref)
        ddt_ref[...] += grads[2]
        first = jnp.logical_and(jnp.logical_and(pids[0] == 0, pids[1] == 0), slot == 0)

        @pl.when(first)
        def _():
            dal_ref[...] = jnp.zeros_like(dal_ref)
            ddb_ref[...] = jnp.zeros_like(ddb_ref)
            dds_ref[...] = jnp.zeros_like(dds_ref)
        dal_ref[...] += grads[5]
        ddb_ref[...] += grads[6]
        dds_ref[...] += grads[7]

    bc_shape = jax.ShapeDtypeStruct((t, SSD_GROUPS * SSD_STATE), f32)
    par_shape = jax.ShapeDtypeStruct((1, LANES), f32)
    outs = pl.pallas_call(body, name="ssd_bwd", grid=grid, in_specs=[xs, dt, bm, cm, par, par, par, sv, xs],
                          out_specs=[xs, dbc, dbc, ddt, par, par, par],
                          out_shape=[jax.ShapeDtypeStruct((t, SSD_GROUPS * SSD_GW), f32),
                                     bc_shape, bc_shape, jax.ShapeDtypeStruct((t, LANES), f32),
                                     par_shape, par_shape, par_shape],
                          scratch_shapes=[pltpu.VMEM((SSD_GROUPS, SSD_STATE, SSD_GW), f32)],
                          compiler_params=_cp(3))(xbc, proj, xbc, xbc, a_log, dt_bias, d_skip, hsave, dy)
    return outs


def f_ssd_epi(pids, y, z, nw):
    yg = y * _silu(z)
    hw = yg.shape[1] // SSD_GROUPS
    parts = []
    for g in range(SSD_GROUPS):
        p = yg[:, g * hw:(g + 1) * hw]
        parts.append(p * lax.rsqrt(jnp.mean(p * p, axis=-1, keepdims=True) + EPS))
    return (jnp.concatenate(parts, axis=1) * nw,)


def f_ml_epi(pids, hm, xc, mz, nw, skip):
    parts = []
    for h in range(ML_HEADS):
        p = hm[:, h * ML_HD:(h + 1) * ML_HD]
        mu = jnp.mean(p, axis=-1, keepdims=True)
        var = jnp.mean(jnp.square(p - mu), axis=-1, keepdims=True)
        parts.append((p - mu) * lax.rsqrt(var + EPS))
    hn = jnp.concatenate(parts, axis=1) * nw
    return ((hn + skip * xc) * _silu(mz),)


def f_s5_post(pids, ys, u, d_skip):
    return (jax.nn.gelu(ys + d_skip * u),)


def f_glu(pids, pab, ba, bb):
    d = ba.shape[1]
    return ((pab[:, :d] + ba) * jax.nn.sigmoid(pab[:, d:] + bb),)


def f_glu_res(pids, pab, xres, ba, bb):
    return (xres + f_glu(pids, pab, ba, bb)[0],)


def rowwise_fwd(name, f, rows, row_cols, pars, out_width, tm=512):
    t = rows[0].shape[0]
    specs = [_row_spec(tm, w, c) for (w, c) in row_cols] + [_par_spec(p.shape) for p in pars]
    return map_fwd(name, f, (t // tm,), list(rows) + list(pars), specs, [jax.ShapeDtypeStruct((t, out_width), f32)],
                   [_row_spec(tm, out_width)])[0]


def rowwise_bwd(name, f, rows, row_cols, pars, dy, tm=256):
    t = rows[0].shape[0]
    n_r, n_p = len(rows), len(pars)
    specs = [_row_spec(tm, w, c) for (w, c) in row_cols] + [_par_spec(p.shape) for p in pars]
    out_w = dy.shape[1]

    def body(*refs):
        vals = [r[...] for r in refs[:n_r + n_p]]
        dy_ref = refs[n_r + n_p]
        outs = refs[n_r + n_p + 1:]
        _, vjp = jax.vjp(lambda *v: f(None, *v)[0], *vals)
        grads = vjp(dy_ref[...])
        for k in range(n_r):
            outs[k][...] = grads[k]

        @pl.when(pl.program_id(0) == 0)
        def _():
            for k in range(n_p):
                outs[n_r + k][...] = jnp.zeros_like(outs[n_r + k])
        for k in range(n_p):
            outs[n_r + k][...] += grads[n_r + k]

    out_shapes = [jax.ShapeDtypeStruct((t, w), f32) for (w, c) in row_cols] + [jax.ShapeDtypeStruct(p.shape, f32) for p in pars]
    out_specs = [_row_spec(tm, w) for (w, c) in row_cols] + [_par_spec(p.shape) for p in pars]
    return pl.pallas_call(body, name=name, grid=(t // tm,), in_specs=specs + [_row_spec(tm, out_w)], out_specs=out_specs,
                          out_shape=out_shapes, compiler_params=_cp(1))(*rows, *pars, dy)


def f_ml(pids, states, q, k, v, g1, g2, g3, b_if):
    h = pids[2]
    cst, nst, mst = states
    l = q.shape[0]
    gt = g1 + g2 + g3 + b_if
    k = k * (1.0 / math.sqrt(ML_HD))
    tri = _tri(l)
    bc_all = hdot(tri.astype(f32), jax.nn.log_sigmoid(gt))
    bcum, ig = _lane_pick(bc_all, ML_HEADS + h), _lane_pick(gt, h)
    bcum_t, ig_t = _row_pick(bc_all.T, ML_HEADS + h), _row_pick(gt.T, h)
    b_last = bcum[l - 1:l, :]
    dlog = jnp.where(tri, bcum - bcum_t + ig_t, NEG)
    ws = b_last - bcum + ig
    m_prev = mst[:, 0:1]
    m_new = lax.stop_gradient(jnp.maximum(b_last + m_prev, jnp.max(ws, axis=0, keepdims=True)))
    decay = jnp.exp(b_last + m_prev - m_new)
    wts = jnp.exp(ws - m_new)
    c_new = decay * cst + bdot(wts * v, k, 0, 0)
    n_new = decay * nst + jnp.sum(wts * k, axis=0, keepdims=True)
    m_inter = bcum + m_prev
    m_t = lax.stop_gradient(jnp.maximum(jnp.max(dlog, axis=1, keepdims=True), m_inter))
    scores = bdot(q, k, 1, 1) * jnp.exp(dlog - m_t)
    inter_w = jnp.exp(m_inter - m_t)
    num = bdot(scores, v, 1, 0) + inter_w * bdot(q, cst, 1, 1)
    den = jnp.sum(scores, axis=1, keepdims=True) + inter_w * jnp.sum(q * nst, axis=1, keepdims=True)
    hout = num / jnp.maximum(jnp.abs(den), jnp.exp(-m_t))
    return (hout,), (c_new, n_new, jnp.broadcast_to(m_new, mst.shape))


def _ml_specs(nch, rev):
    cc = (lambda c: nch - 1 - c) if rev else (lambda c: c)
    hd = pl.BlockSpec((CHUNK, ML_HD), lambda b, c, h: (b * nch + cc(c), h))
    gt = pl.BlockSpec((CHUNK, LANES), lambda b, c, h: (b * nch + cc(c), 0))
    par = pl.BlockSpec((1, LANES), lambda b, c, h: (0, 0))
    sc = pl.BlockSpec((1, 1, ML_HD, ML_HD), lambda b, c, h: (b * nch + cc(c), h, 0, 0))
    sn = pl.BlockSpec((1, 1, 1, ML_HD), lambda b, c, h: (b * nch + cc(c), h, 0, 0))
    sm = pl.BlockSpec((1, 1, 1, LANES), lambda b, c, h: (b * nch + cc(c), h, 0, 0))
    return hd, gt, par, sc, sn, sm


ML_STATE_SHAPES = [(ML_HD, ML_HD), (1, ML_HD), (1, LANES)]


def ml_fwd(q, k, v, g1, g2, g3, b_if, seq, ride=None):
    t = q.shape[0]
    nch = seq // CHUNK
    hd, gt, par, sc, sn, sm = _ml_specs(nch, False)
    nc = t // CHUNK
    outs = scan_fwd("ml_fwd", f_ml, (t // seq, nch, ML_HEADS), 2, [q, k, v, g1, g2, g3, b_if],
                    [hd, hd, hd, gt, gt, gt, par], [jax.ShapeDtypeStruct((t, ML_HEADS * ML_HD), f32)], [hd],
                    ML_STATE_SHAPES, [0.0, 0.0, NEG],
                    [jax.ShapeDtypeStruct((nc, ML_HEADS, ML_HD, ML_HD), f32), jax.ShapeDtypeStruct((nc, ML_HEADS, 1, ML_HD), f32),
                     jax.ShapeDtypeStruct((nc, ML_HEADS, 1, LANES), f32)], [sc, sn, sm], ride=ride)
    return outs[0], outs[1:4], outs[4:]


def ml_bwd(dh, q, k, v, g1, g2, g3, b_if, saves, seq, ride=None):
    t = q.shape[0]
    nch = seq // CHUNK
    hd, gt, par, sc, sn, sm = _ml_specs(nch, True)
    rider = Rider(ride)
    nr = rider.n
    grid = (t // seq, nch, ML_HEADS)

    def f(pids, states, q, k, v, gsum, b_if):
        return f_ml(pids, states, q, k, v, gsum, jnp.zeros_like(gsum), jnp.zeros_like(gsum), b_if)

    def body(*refs):
        q_ref, k_ref, v_ref, g1_ref, g2_ref, g3_ref, b_ref, c_ref, n_ref, m_ref, dh_ref = refs[:11]
        r_ins = refs[11:11 + nr]
        dq_ref, dk_ref, dv_ref, dg_ref, db_ref = refs[11 + nr:16 + nr]
        r_outs = refs[16 + nr:16 + 2 * nr]
        dc_s, dn_s = refs[16 + 2 * nr:18 + 2 * nr]
        sems = refs[18 + 2 * nr:]
        rider.start(grid, r_ins, r_outs, sems)
        pids = tuple(pl.program_id(i) for i in range(3))
        slot = pids[2]

        @pl.when(pids[1] == 0)
        def _():
            dc_s[slot] = jnp.zeros(dc_s.shape[1:], f32)
            dn_s[slot] = jnp.zeros(dn_s.shape[1:], f32)

        gsum = g1_ref[...] + g2_ref[...] + g3_ref[...]
        mst = m_ref[0, 0]

        def gfun(cst, nst, qq, kk, vv, gs, bb):
            outs, new = f(pids, (cst, nst, mst), qq, kk, vv, gs, bb)
            return outs[0], new[0], new[1]

        _, vjp = jax.vjp(gfun, c_ref[0, 0], n_ref[0, 0], q_ref[...], k_ref[...], v_ref[...], gsum, b_ref[...])
        grads = vjp((dh_ref[...], dc_s[slot], dn_s[slot]))
        dc_s[slot] = grads[0]
        dn_s[slot] = grads[1]
        dq_ref[...] = grads[2]
        dk_ref[...] = grads[3]
        dv_ref[...] = grads[4]

        @pl.when(slot == 0)
        def _():
            dg_ref[...] = jnp.zeros_like(dg_ref)
        dg_ref[...] += grads[5]
        first = jnp.logical_and(jnp.logical_and(pids[0] == 0, pids[1] == 0), slot == 0)

        @pl.when(first)
        def _():
            db_ref[...] = jnp.zeros_like(db_ref)
        db_ref[...] += grads[6]
        rider.wait(grid, r_ins, r_outs, sems)

    big = jax.ShapeDtypeStruct((t, ML_HEADS * ML_HD), f32)
    return pl.pallas_call(body, name="ml_bwd", grid=grid,
                          in_specs=[hd, hd, hd, gt, gt, gt, par, sc, sn, sm, hd] + rider.specs(),
                          out_specs=[hd, hd, hd, gt, par] + rider.specs(),
                          out_shape=[big, big, big, jax.ShapeDtypeStruct((t, LANES), f32), jax.ShapeDtypeStruct((1, LANES), f32)]
                          + rider.out_shapes(),
                          scratch_shapes=[pltpu.VMEM((ML_HEADS, ML_HD, ML_HD), f32), pltpu.VMEM((ML_HEADS, 1, ML_HD), f32)]
                          + rider.scratch(),
                          compiler_params=_cp(3))(q, k, v, g1, g2, g3, b_if, *saves, dh, *rider.arrays())


def _block_prefix(z, transpose):
    n = z.shape[0]
    r, c = _iota((n, n), 0), _iota((n, n), 1)
    keep = jnp.logical_and(r // S5_SUB == c // S5_SUB, (c >= r) if transpose else (c <= r))
    m = jnp.where(keep, 1.0, 0.0).astype(bf16)
    hi = z.astype(bf16)
    lo = (z - hi.astype(f32)).astype(bf16)
    return jnp.dot(m, hi, preferred_element_type=f32) + jnp.dot(m, lo, preferred_element_type=f32)


@jax.custom_vjp
def block_prefix(z):
    return _block_prefix(z, False)


block_prefix.defvjp(lambda z: (_block_prefix(z, False), None), lambda _, ct: (_block_prefix(ct, True),))


def _cmul(a, b):
    h = b.shape[1] // 2
    ar, ai, br, bi = a[:, :h], a[:, h:], b[:, :h], b[:, h:]
    return jnp.concatenate([ar * br - ai * bi, ar * bi + ai * br], axis=1)


def f_s5(pids, states, u, bb, cc, tab):
    (carry,) = states
    tl = u.shape[0]
    nsub = tl // S5_SUB
    rep = lambda t: jnp.concatenate([t] * nsub, axis=0)
    p1, p0, q0 = tab[0:S5_SUB], tab[S5_SUB:2 * S5_SUB], tab[2 * S5_SUB:3 * S5_SUB]
    lam_sub = tab[S5_SUB - 1:S5_SUB]
    bu = bdot(u, bb, 1, 0)
    xl = _cmul(rep(p0), block_prefix(_cmul(rep(q0), bu)))
    e, entering = carry, []
    for k in range(nsub):
        entering.append(jnp.broadcast_to(e, (S5_SUB, e.shape[1])))
        e = xl[(k + 1) * S5_SUB - 1:(k + 1) * S5_SUB] + _cmul(lam_sub, e)
    x = xl + _cmul(rep(p1), jnp.concatenate(entering, axis=0))
    y = bdot(x, cc, 1, 0)
    return (y,), (e,)


def _s5_specs(ntl, rev):
    tt = (lambda t: ntl - 1 - t) if rev else (lambda t: t)
    us = pl.BlockSpec((S5_TL, LANES), lambda c, b, t: (b * ntl + tt(t), c))
    bbs = pl.BlockSpec((1, LANES, 2 * S5_CH), lambda c, b, t: (c, 0, 0))
    ccs = pl.BlockSpec((1, 2 * S5_CH, LANES), lambda c, b, t: (c, 0, 0))
    pws = pl.BlockSpec((1, 3 * S5_SUB, 2 * S5_CH), lambda c, b, t: (c, 0, 0))
    sv = pl.BlockSpec((1, 1, 1, 2 * S5_CH), lambda c, b, t: (b * ntl + tt(t), c, 0, 0))
    return us, bbs, ccs, pws, sv


def s5_fwd(u, bb, cc, pw, seq):
    t = u.shape[0]
    ntl = seq // S5_TL
    us, bbs, ccs, pws, sv = _s5_specs(ntl, False)

    def f(pids, states, uu, b3, c3, p3):
        return f_s5(pids, states, uu, b3[0], c3[0], p3[0])

    y, carries = scan_fwd("s5_fwd", f, (S5_CB, t // seq, ntl), 0, [u, bb, cc, pw], [us, bbs, ccs, pws],
                          [jax.ShapeDtypeStruct((t, S5_CB * LANES), f32)], [us], [(1, 2 * S5_CH)], [0.0],
                          [jax.ShapeDtypeStruct((t // S5_TL, S5_CB, 1, 2 * S5_CH), f32)], [sv])
    return y, carries


def s5_bwd(dy, u, bb, cc, pw, carries, seq):
    t = u.shape[0]
    ntl = seq // S5_TL
    us, bbs, ccs, pws, sv = _s5_specs(ntl, True)

    def f(pids, states, uu, b3, c3, p3):
        return f_s5(pids, states, uu, b3[0], c3[0], p3[0])

    first = lambda pids: jnp.logical_and(pids[1] == 0, pids[2] == 0)
    return scan_bwd("s5_bwd", f, (S5_CB, t // seq, ntl), 0, [u, bb, cc, pw], [us, bbs, ccs, pws], [carries], [sv],
                    [dy], [us], [(1, 2 * S5_CH)], [0, 1, 2, 3], {1: first, 2: first, 3: first})


def _adam_math(g, w, m, v):
    m2 = ADAM_B1 * m + (1.0 - ADAM_B1) * g
    v2 = ADAM_B2 * v + (1.0 - ADAM_B2) * jnp.square(g)
    m_hat = m2 / (1.0 - ADAM_B1 ** ADAM_STEP)
    v_hat = v2 / (1.0 - ADAM_B2 ** ADAM_STEP)
    delta = -ADAM_LR * (m_hat / (jnp.sqrt(v_hat) + ADAM_EPS) + ADAM_WD * w)
    return delta, m2, v2


def adamw(name, parts, w, m, v, tr=256):
    n, r, c = parts.shape
    tr = min(tr, r)
    assert r % tr == 0

    def body(p_ref, w_ref, m_ref, v_ref, g_ref, d_ref, m2_ref, v2_ref):
        g = p_ref[0].astype(f32)
        for s in range(1, n):
            g = g + p_ref[s].astype(f32)
        d, m2, v2 = _adam_math(g, w_ref[...], m_ref[...], v_ref[...])
        g_ref[...] = g
        d_ref[...] = d
        m2_ref[...] = m2
        v2_ref[...] = v2

    ps = pl.BlockSpec((n, tr, c), lambda i: (0, i, 0))
    rs = pl.BlockSpec((tr, c), lambda i: (i, 0))
    return pl.pallas_call(body, name=name, grid=(r // tr,), in_specs=[ps, rs, rs, rs], out_specs=[rs] * 4,
                          out_shape=[jax.ShapeDtypeStruct((r, c), f32)] * 4, compiler_params=_cp(1))(parts, w, m, v)


def adamw_layer(name, parts, w, m, v, layer, prev=None, tr=256):
    n, r, c = parts.shape
    nl = w.shape[0]
    tr = min(tr, r)
    assert r % tr == 0 and w.shape[1:] == (r, c)
    n_prev = 0 if prev is None else 4

    def body(*refs):
        p_ref, w_ref, m_ref, v_ref = refs[:4]
        g_ref, d_ref, m2_ref, v2_ref = refs[4 + n_prev:]
        g = p_ref[0].astype(f32)
        for s in range(1, n):
            g = g + p_ref[s].astype(f32)
        d, m2, v2 = _adam_math(g, w_ref[0], m_ref[0], v_ref[0])
        g_ref[0] = g
        d_ref[0] = d
        m2_ref[0] = m2
        v2_ref[0] = v2

    ps = pl.BlockSpec((n, tr, c), lambda i: (0, i, 0))
    rs = pl.BlockSpec((1, tr, c), lambda i: (layer, i, 0))
    anyspec = pl.BlockSpec(memory_space=pl.ANY)
    return pl.pallas_call(body, name=name, grid=(r // tr,), in_specs=[ps, rs, rs, rs] + [anyspec] * n_prev, out_specs=[rs] * 4,
                          out_shape=[jax.ShapeDtypeStruct((nl, r, c), f32)] * 4,
                          input_output_aliases={4 + i: i for i in range(n_prev)},
                          compiler_params=_cp(1))(parts, w, m, v, *(prev or ()))


def sum_parts(name, parts, tr=256):
    n, r, c = parts.shape
    tr = min(tr, r)
    assert r % tr == 0

    def body(p_ref, o_ref):
        g = p_ref[0].astype(f32)
        for s in range(1, n):
            g = g + p_ref[s].astype(f32)
        o_ref[...] = g

    return pl.pallas_call(body, name=name, grid=(r // tr,), in_specs=[pl.BlockSpec((n, tr, c), lambda i: (0, i, 0))],
                          out_specs=pl.BlockSpec((tr, c), lambda i: (i, 0)),
                          out_shape=jax.ShapeDtypeStruct((r, c), f32), compiler_params=_cp(1))(parts)


class Rider:
    def __init__(self, ops):
        self.ops = list(ops or [])
        self.n = len(self.ops)

    def arrays(self):
        return [a for a, _ in self.ops]

    def specs(self):
        return [pl.BlockSpec(memory_space=pl.ANY)] * self.n

    def out_shapes(self):
        return [jax.ShapeDtypeStruct((N_DEV,) + tuple(a.shape) if mode == "gather" else tuple(a.shape), a.dtype)
                for a, mode in self.ops]

    def scratch(self):
        if not self.n:
            return []
        return [pltpu.SemaphoreType.DMA((self.n, N_DEV - 1)), pltpu.SemaphoreType.DMA((self.n, N_DEV - 1)),
                pltpu.SemaphoreType.DMA((self.n,))]

    def _copies(self, ins, outs, sems):
        send_sems, recv_sems, loc_sems = sems
        x, y, c = lax.axis_index("x"), lax.axis_index("y"), lax.axis_index("c")
        me = 4 * x + 2 * y + c
        first, crossing, relays = [], [], []

        def remote(src, dst, k, idx, peer):
            return pltpu.make_async_remote_copy(src_ref=src, dst_ref=dst, send_sem=send_sems.at[k, idx], recv_sem=recv_sems.at[k, idx],
                                                device_id=peer, device_id_type=pl.DeviceIdType.MESH)

        for k, (_, mode) in enumerate(self.ops):
            src_me = ins[k] if mode == "gather" else ins[k].at[me]
            first.append(pltpu.make_async_copy(src_me, outs[k].at[me], loc_sems.at[k]))
            if mode == "scatter":
                for d in range(1, N_DEV):
                    px = 1 - x if (d >> 2) & 1 else x
                    py = 1 - y if (d >> 1) & 1 else y
                    pc = 1 - c if d & 1 else c
                    first.append(remote(ins[k].at[4 * px + 2 * py + pc], outs[k].at[me], k, d - 1, (px, py, pc)))
            else:
                first.append(remote(ins[k], outs[k].at[me], k, 0, (x, y, 1 - c)))
                for q in range(1, 4):
                    px = 1 - x if (q >> 1) & 1 else x
                    py = 1 - y if q & 1 else y
                    crossing.append(remote(ins[k], outs[k].at[me], k, q, (px, py, c)))
                    block = outs[k].at[4 * px + 2 * py + c]
                    relays.append(remote(block, block, k, 3 + q, (x, y, 1 - c)))
        return first, crossing, relays

    def _start(self, ins, outs, sems):
        first, crossing, _ = self._copies(ins, outs, sems)
        for cp in first + crossing:
            cp.start()

    def _finish(self, ins, outs, sems):
        first, crossing, relays = self._copies(ins, outs, sems)
        for cp, relay in zip(crossing, relays):
            cp.wait_recv()
            relay.start()
        for cp in first + relays:
            cp.wait()
        for cp in crossing:
            cp.wait_send()

    def start(self, grid, ins, outs, sems):
        if self.n:
            @pl.when(functools.reduce(jnp.logical_and, [pl.program_id(i) == 0 for i in range(len(grid))]))
            def _():
                self._start(ins, outs, sems)

    def wait(self, grid, ins, outs, sems):
        if self.n:
            @pl.when(functools.reduce(jnp.logical_and, [pl.program_id(i) == g - 1 for i, g in enumerate(grid)]))
            def _():
                self._finish(ins, outs, sems)


def exchange(name, ops):
    rider = Rider(ops)
    n = rider.n

    def body(*refs):
        rider._start(refs[:n], refs[n:2 * n], refs[2 * n:])
        rider._finish(refs[:n], refs[n:2 * n], refs[2 * n:])

    return pl.pallas_call(body, name=name, in_specs=rider.specs(), out_specs=rider.specs(), out_shape=rider.out_shapes(),
                          scratch_shapes=rider.scratch())(*rider.arrays())


def _lanes(v, width=LANES):
    v = v.reshape(1, -1)
    return jnp.pad(v, ((0, 0), (0, width - v.shape[1])))


def win_to_padded(w):
    return jnp.concatenate([w[:, :1024], w[:, 2576:3600], w[:, 3600:4624], w[:, 1024:2560], w[:, 2560:2576],
                            jnp.zeros((w.shape[0], PROJ_W - IN_COLS), w.dtype)], axis=1)


def win_from_padded(wp):
    return jnp.concatenate([wp[:, 0:1024], wp[:, 3072:4608], wp[:, 4608:4624], wp[:, 1024:2048], wp[:, 2048:3072]], axis=1)


def headwise_dense(w):
    nb, o, i = w.shape
    rows = jnp.tile(w.transpose(0, 2, 1).reshape(nb * i, o), (1, nb))
    same = (jnp.arange(nb * i)[:, None] // i) == (jnp.arange(nb * o)[None, :] // o)
    return jnp.where(same, rows, 0.0)


def diag_blocks(name, dd, blk, tm=256):
    n = dd.shape[0]

    def body(d_ref, o_ref):
        rows = _iota((tm, n), 0) + pl.program_id(0) * tm
        masked = jnp.where(rows // blk == _iota((tm, n), 1) // blk, d_ref[...], 0.0)
        sel = (_iota((n, LANES), 0) % blk == _iota((n, LANES), 1)).astype(f32)
        o_ref[...] = hdot(masked, sel)

    return pl.pallas_call(body, name=name, grid=(n // tm,), in_specs=[pl.BlockSpec((tm, n), lambda i: (i, 0))],
                          out_specs=pl.BlockSpec((tm, LANES), lambda i: (i, 0)),
                          out_shape=jax.ShapeDtypeStruct((n, LANES), f32), compiler_params=_cp(1))(dd)


def headwise_from_dense(name, dd, o=4, i=4):
    nb = dd.shape[0] // i
    return diag_blocks(name, dd, i)[:, :o].reshape(nb, i, o).transpose(0, 2, 1)


def s5_tables(a_re, a_im, log_step, b_re, b_im, c_re, c_im):
    step = jnp.exp(log_step)[:, None]
    j = jnp.arange(S5_SUB, dtype=f32)[:, None, None]
    expo = jnp.concatenate([j + 1.0, j, -j], axis=0)
    mag = jnp.exp(expo * (a_re * step))
    pw_re, pw_im = mag * jnp.cos(expo * (a_im * step)), mag * jnp.sin(expo * (a_im * step))
    lam_re, lam_im = pw_re[0], pw_im[0]
    den = a_re * a_re + a_im * a_im
    coef_re = ((lam_re - 1.0) * a_re + lam_im * a_im) / den
    coef_im = (lam_im * a_re - (lam_re - 1.0) * a_im) / den
    bb_re = coef_re[..., None] * b_re - coef_im[..., None] * b_im
    bb_im = coef_re[..., None] * b_im + coef_im[..., None] * b_re
    gl = S5_GROUPS // S5_CB
    eye = jnp.eye(gl, dtype=f32)

    def blk_b(t):
        t4 = t.transpose(0, 2, 1).reshape(S5_CB, gl, S5_GROUP, S5_STATE)
        return jnp.einsum("kgcn,gh->kgchn", t4, eye).reshape(S5_CB, gl * S5_GROUP, gl * S5_STATE)

    def blk_c(t):
        t4 = t.reshape(S5_CB, gl, S5_GROUP, S5_STATE)
        return jnp.einsum("kgcn,gh->kgnhc", t4, eye).reshape(S5_CB, gl * S5_STATE, gl * S5_GROUP)

    def blk_p(t):
        return t.reshape(t.shape[0], S5_CB, gl * S5_STATE).transpose(1, 0, 2)

    bb = jnp.concatenate([blk_b(bb_re), blk_b(bb_im)], axis=2)
    cc = jnp.concatenate([blk_c(c_re), -blk_c(c_im)], axis=1)
    pw = jnp.concatenate([blk_p(pw_re), blk_p(pw_im)], axis=2)
    return bb, cc, pw


def ffn_step_bwd(dy, x, nw, wts, saved, ride_act=None, ride_w=None):
    h, g, u = saved
    dx, dnw, dg, du, a, dyh, *got_act = ffn_bwd_act(dy, x, nw, g, u, *wts, ride=ride_act)
    dwg, dwu, dwd, *got_w = ffn_bwd_w(h, dyh, dg, du, a, ride=ride_w)
    return dx, dnw, (dwg, dwu, dwd), got_act, got_w


def hybrid_fwd(x1, p, seq, ride_in, ride_ssd, ride_ml):
    u = rms_fwd(x1, p["mix_norm"])
    proj, *got_in = matmul("hy_in", u, p["win"], ride=ride_in)
    xbc = conv_fwd("ssd_conv", proj, OFF_XBC, p["ssd_conv_w"], p["ssd_conv_b"], seq)
    yraw, hsave, got_ssd = ssd_fwd(xbc, proj, p["a_log"], p["dt_bias"], p["ssd_d"], seq, ride=ride_ssd)
    yssd = rowwise_fwd("ssd_epi", f_ssd_epi, [yraw, proj], [(D_MODEL, 0), (D_MODEL, OFF_Z // D_MODEL)], [p["ssd_norm_w"]], D_MODEL)
    xc = conv_fwd("ml_conv", proj, OFF_MX, p["ml_conv_w"], p["ml_conv_b"], seq)
    q = matmul("hw_q", xc, p["wq"])
    k = matmul("hw_k", xc, p["wk"])
    v = matmul("hw_v", proj, p["wv"], a_off=OFF_MX, a_width=D_MODEL)
    g1 = matmul("gate_q", q, p["wif_q"])
    g2 = matmul("gate_k", k, p["wif_k"])
    g3 = matmul("gate_v", v, p["wif_v"])
    hm, mlsave, got_ml = ml_fwd(q, k, v, g1, g2, g3, p["b_if"], seq, ride=ride_ml)
    yml = rowwise_fwd("ml_epi", f_ml_epi, [hm, xc, proj], [(D_MODEL, 0), (D_MODEL, 0), (D_MODEL, OFF_MZ // D_MODEL)],
                      [p["ml_norm_w"], p["ml_skip"]], D_MODEL)
    t = matmul("hy_out1", yssd, p["wo1"], add=x1)
    x2 = matmul("hy_out2", yml, p["wo2"], add=t)
    return x2, (u, proj, xbc, yraw, hsave, yssd, xc, q, k, v, g1, g2, g3, hm, mlsave, yml), got_in, got_ssd, got_ml


def hybrid_bwd(dx2, x1, p, saved, seq, ride_ml):
    u, proj, xbc, yraw, hsave, yssd, xc, q, k, v, g1, g2, g3, hm, mlsave, yml = saved
    gr = {}
    dyssd = matmul("d_yssd", dx2, p["wo1"], cb=1)
    dyml = matmul("d_yml", dx2, p["wo2"], cb=1)
    gr["wo"] = jnp.concatenate([matmul("dw_o1", yssd, dx2, ca=0), matmul("dw_o2", yml, dx2, ca=0)], axis=0)
    d_hm, d_xc, d_mz, gr["ml_norm_w"], gr["ml_skip"] = rowwise_bwd(
        "ml_epi_bwd", f_ml_epi, [hm, xc, proj], [(D_MODEL, 0), (D_MODEL, 0), (D_MODEL, OFF_MZ // D_MODEL)],
        [p["ml_norm_w"], p["ml_skip"]], dyml)
    dq, dk, dv, dgt, gr["b_if"], *got_ml = ml_bwd(d_hm, q, k, v, g1, g2, g3, p["b_if"], mlsave, seq, ride=ride_ml)
    dq = matmul("dq_gate", dgt, p["wif_q"], cb=1, add=dq)
    dk = matmul("dk_gate", dgt, p["wif_k"], cb=1, add=dk)
    dv = matmul("dv_gate", dgt, p["wif_v"], cb=1, add=dv)
    gr["wif"] = jnp.concatenate([matmul("dw_if_q", q, dgt, ca=0), matmul("dw_if_k", k, dgt, ca=0),
                                 matmul("dw_if_v", v, dgt, ca=0)], axis=0)
    d_xc = matmul("dxc_q", dq, p["wq"], cb=1, add=d_xc)
    d_xc = matmul("dxc_k", dk, p["wk"], cb=1, add=d_xc)
    gr["wq"] = matmul("dw_q", xc, dq, ca=0)
    gr["wk"] = matmul("dw_k", xc, dk, ca=0)
    gr["wv"] = matmul("dw_v", proj, dv, ca=0, a_off=OFF_MX, a_width=D_MODEL)
    d_mx, gr["ml_conv_w"], gr["ml_conv_b"] = conv_bwd("ml_conv_bwd", d_xc, proj, OFF_MX, p["ml_conv_w"], p["ml_conv_b"], seq)
    d_mx = matmul("dmx_v", dv, p["wv"], cb=1, add=d_mx)
    d_yraw, d_z, gr["ssd_norm_w"] = rowwise_bwd("ssd_epi_bwd", f_ssd_epi, [yraw, proj],
                                                [(D_MODEL, 0), (D_MODEL, OFF_Z // D_MODEL)], [p["ssd_norm_w"]], dyssd)
    d_xs, d_b, d_c, d_dt, gr["a_log"], gr["dt_bias"], gr["ssd_d"] = ssd_bwd(
        d_yraw, xbc, proj, p["a_log"], p["dt_bias"], p["ssd_d"], hsave, seq)
    d_xbc, gr["ssd_conv_w"], gr["ssd_conv_b"] = conv_bwd("ssd_conv_bwd", jnp.concatenate([d_xs, d_b, d_c], axis=1), proj, OFF_XBC,
                                                         p["ssd_conv_w"], p["ssd_conv_b"], seq)
    dproj = jnp.concatenate([d_z, d_mx, d_mz, d_xbc, d_dt, jnp.zeros((d_dt.shape[0], PROJ_W - OFF_DT - LANES), f32)], axis=1)
    gr["win"] = matmul("dw_in", u.astype(bf16).T, dproj, tiles=(D_MODEL, PROJ_W // 2, min(512, u.shape[0])))
    du = matmul("d_u", dproj, p["win"], cb=1)
    dx1, gr["mix_norm"] = rms_bwd([du], x1, p["mix_norm"], dx2)
    return dx1, gr, got_ml


def s5_layer_fwd(x4, p, seq):
    u = rms_fwd(x4, p["mix_norm"])
    ys, carries = s5_fwd(u, p["bb"], p["cc"], p["pw"], seq)
    gg = rowwise_fwd("s5_post", f_s5_post, [ys, u], [(D_MODEL, 0), (D_MODEL, 0)], [p["s5_d"]], D_MODEL)
    pab = matmul("s5_ab", gg, p["wab"])
    x5 = rowwise_fwd("s5_glu", f_glu_res, [pab, x4], [(2 * D_MODEL, 0), (D_MODEL, 0)], [p["b_a"], p["b_b"]], D_MODEL)
    return x5, (u, ys, carries, gg, pab)


def s5_layer_bwd(dx5, x4, p, saved, seq):
    u, ys, carries, gg, pab = saved
    gr = {}
    dpab, gr["b_a"], gr["b_b"] = rowwise_bwd("s5_glu_bwd", f_glu, [pab], [(2 * D_MODEL, 0)], [p["b_a"], p["b_b"]], dx5)
    dgg = matmul("d_gg", dpab, p["wab"], cb=1)
    gr["wab"] = matmul("dw_ab", gg, dpab, ca=0)
    dys, du_a, gr["s5_d"] = rowwise_bwd("s5_post_bwd", f_s5_post, [ys, u], [(D_MODEL, 0), (D_MODEL, 0)], [p["s5_d"]], dgg)
    du_b, gr["bb"], gr["cc"], gr["pw"] = s5_bwd(dys, u, p["bb"], p["cc"], p["pw"], carries, seq)
    dx4, gr["mix_norm"] = rms_bwd([du_a, du_b], x4, p["mix_norm"], dx5)
    return dx4, gr


BIG = ["ffn1_w_gate", "ffn1_w_up", "ffn1_w_down", "ffn2_w_gate", "ffn2_w_up", "ffn2_w_down", "hy_w_in", "hy_w_out", "s5_w_a", "s5_w_b"]
SMALL_SHARDED = {"ssd_conv_w": 2, "ml_conv_w": 2, "ml_w_q": 1, "ml_w_k": 1, "ml_w_v": 1, "ml_w_if": 1, "s5_d": 1, "s5_b_a": 1, "s5_b_b": 1}
WEIGHTS = ["ffn1_norm", "ffn1_w_gate", "ffn1_w_up", "ffn1_w_down", "mix_norm", "ffn2_norm", "ffn2_w_gate", "ffn2_w_up", "ffn2_w_down",
           "hy_w_in", "ssd_conv_w", "ssd_conv_b", "ssd_dt_bias", "ssd_a_log", "ssd_d", "ssd_norm_w", "ml_conv_w", "ml_conv_b",
           "ml_w_q", "ml_w_k", "ml_w_v", "ml_w_if", "ml_b_if", "ml_norm_w", "ml_skip", "hy_w_out", "s5_a_re", "s5_a_im",
           "s5_log_step", "s5_b_re", "s5_b_im", "s5_c_re", "s5_c_im", "s5_d", "s5_w_a", "s5_b_a", "s5_w_b", "s5_b_b", "final_norm"]
S5_PARAMS = ["s5_a_re", "s5_a_im", "s5_log_step", "s5_b_re", "s5_b_im", "s5_c_re", "s5_c_im"]
SMALL_S5 = S5_PARAMS + ["s5_d", "s5_b_a", "s5_b_b"]
SMALL_REST = [n for n in WEIGHTS if n not in BIG and n not in SMALL_S5]
SMALL = SMALL_REST + SMALL_S5


def _unshard(g, axis):
    return jnp.concatenate([g[i] for i in range(N_DEV)], axis=axis)


def assemble_hybrid(gw, rep):
    padn = lambda w: jnp.pad(w, ((0, 0), (0, LANES - w.shape[1]))).astype(bf16)
    wif = _unshard(gw["ml_w_if"], 1)[0]
    wo = _unshard(gw["hy_w_out"], 1)[0].astype(bf16)
    dense = lambda n: headwise_dense(_unshard(gw[n], 1)[0].astype(f32)).astype(bf16)
    w0 = dict(mix_norm=rep["mix_norm"][0:1],
              win=win_to_padded(_unshard(gw["hy_w_in"], 2)[0]).astype(bf16),
              ssd_conv_w=_unshard(gw["ssd_conv_w"], 2)[0], ssd_conv_b=rep["ssd_conv_b"],
              a_log=_lanes(rep["ssd_a_log"]), dt_bias=_lanes(rep["ssd_dt_bias"]), ssd_d=_lanes(rep["ssd_d"]),
              ssd_norm_w=rep["ssd_norm_w"], ml_conv_w=_unshard(gw["ml_conv_w"], 2)[0], ml_conv_b=rep["ml_conv_b"],
              wq=dense("ml_w_q"), wk=dense("ml_w_k"), wv=dense("ml_w_v"),
              wif_q=padn(wif[0:1024]), wif_k=padn(wif[1024:2048]), wif_v=padn(wif[2048:3072]),
              b_if=_lanes(rep["ml_b_if"]), ml_norm_w=rep["ml_norm_w"], ml_skip=rep["ml_skip"],
              wo1=wo[:D_MODEL], wo2=wo[D_MODEL:])
    return w0


def assemble_s5(gw, rep):
    bb, cc, pw = s5_tables(*[rep[n][0] for n in S5_PARAMS])
    wab = jnp.concatenate([_unshard(gw["s5_w_a"], 1)[0], _unshard(gw["s5_w_b"], 1)[0]], axis=1).astype(bf16)
    return dict(mix_norm=rep["mix_norm"][1:2], bb=bb, cc=cc, pw=pw,
                s5_d=_unshard(gw["s5_d"], 1), wab=wab, b_a=_unshard(gw["s5_b_a"], 1), b_b=_unshard(gw["s5_b_b"], 1))


def _shards(full, axis):
    return jnp.stack(jnp.split(full, N_DEV, axis=axis), axis=0)


def small_grads(g_norms, g_hy, g_s5, d_final, rep):
    small = dict(g_norms)
    small["mix_norm"] = jnp.concatenate([g_hy["mix_norm"], g_s5["mix_norm"]], axis=0)
    small["ssd_conv_w"] = g_hy["ssd_conv_w"][None]
    small["ssd_conv_b"] = g_hy["ssd_conv_b"]
    small["ssd_dt_bias"] = g_hy["dt_bias"][:, :SSD_HEADS]
    small["ssd_a_log"] = g_hy["a_log"][:, :SSD_HEADS]
    small["ssd_d"] = g_hy["ssd_d"][:, :SSD_HEADS]
    small["ssd_norm_w"] = g_hy["ssd_norm_w"]
    small["ml_conv_w"] = g_hy["ml_conv_w"][None]
    small["ml_conv_b"] = g_hy["ml_conv_b"]
    for nm, key in (("ml_w_q", "wq"), ("ml_w_k", "wk"), ("ml_w_v", "wv")):
        small[nm] = headwise_from_dense("diag_" + key, g_hy[key])[None]
    small["ml_w_if"] = g_hy["wif"][None, :, :2 * ML_HEADS]
    small["ml_b_if"] = g_hy["b_if"][:, :2 * ML_HEADS]
    small["ml_norm_w"] = g_hy["ml_norm_w"]
    small["ml_skip"] = g_hy["ml_skip"]
    small["final_norm"] = d_final.reshape(-1)
    return small


def s5_small_grads(g_s5, rep):
    small = {}
    _, tvjp = jax.vjp(s5_tables, *[rep[n][0] for n in S5_PARAMS])
    for n, g in zip(S5_PARAMS, tvjp((g_s5["bb"], g_s5["cc"], g_s5["pw"]))):
        small[n] = g[None]
    small["s5_d"] = g_s5["s5_d"]
    small["s5_b_a"] = g_s5["b_a"]
    small["s5_b_b"] = g_s5["b_b"]
    return small


ROW = 1024
F32_ROWS = 8


def _piece_rows(size):
    return -(-size // (ROW * F32_ROWS)) * F32_ROWS


def _pack(arrays):
    pieces = []
    for a in arrays:
        flat = a.astype(f32).reshape(-1)
        pieces.append(jnp.pad(flat, (0, _piece_rows(a.size) * ROW - a.size)).reshape(-1, ROW))
    return jnp.concatenate(pieces, axis=0)


def _unpack(buf, shapes):
    out, r0 = [], 0
    lead = buf.shape[:-2]
    for shp in shapes:
        size = math.prod(shp)
        r = _piece_rows(size)
        out.append(buf[..., r0:r0 + r, :].reshape(lead + (-1,))[..., :size].reshape(lead + tuple(shp)))
        r0 += r
    return out


ADAM_BLOCK_ELEMS = 500_000


def _tile_rows(r, c):
    cap = ADAM_BLOCK_ELEMS // (-(-c // LANES) * LANES)
    if r <= cap:
        return r
    return max(t for t in range(F32_ROWS, cap + 1, F32_ROWS) if r % t == 0)


def _flat2d(a):
    return a.reshape(-1, a.shape[-1])


def kernel(x, ffn1_norm, ffn1_w_gate, ffn1_w_up, ffn1_w_down, mix_norm, ffn2_norm, ffn2_w_gate, ffn2_w_up, ffn2_w_down, hy_w_in, ssd_conv_w, ssd_conv_b, ssd_dt_bias, ssd_a_log, ssd_d, ssd_norm_w, ml_conv_w, ml_conv_b, ml_w_q, ml_w_k, ml_w_v, ml_w_if, ml_b_if, ml_norm_w, ml_skip, hy_w_out, s5_a_re, s5_a_im, s5_log_step, s5_b_re, s5_b_im, s5_c_re, s5_c_im, s5_d, s5_w_a, s5_b_a, s5_w_b, s5_b_b, final_norm, loss_target, m_ffn1_norm, m_ffn1_w_gate, m_ffn1_w_up, m_ffn1_w_down, m_mix_norm, m_ffn2_norm, m_ffn2_w_gate, m_ffn2_w_up, m_ffn2_w_down, m_hy_w_in, m_ssd_conv_w, m_ssd_conv_b, m_ssd_dt_bias, m_ssd_a_log, m_ssd_d, m_ssd_norm_w, m_ml_conv_w, m_ml_conv_b, m_ml_w_q, m_ml_w_k, m_ml_w_v, m_ml_w_if, m_ml_b_if, m_ml_norm_w, m_ml_skip, m_hy_w_out, m_s5_a_re, m_s5_a_im, m_s5_log_step, m_s5_b_re, m_s5_b_im, m_s5_c_re, m_s5_c_im, m_s5_d, m_s5_w_a, m_s5_b_a, m_s5_w_b, m_s5_b_b, m_final_norm, v_ffn1_norm, v_ffn1_w_gate, v_ffn1_w_up, v_ffn1_w_down, v_mix_norm, v_ffn2_norm, v_ffn2_w_gate, v_ffn2_w_up, v_ffn2_w_down, v_hy_w_in, v_ssd_conv_w, v_ssd_conv_b, v_ssd_dt_bias, v_ssd_a_log, v_ssd_d, v_ssd_norm_w, v_ml_conv_w, v_ml_conv_b, v_ml_w_q, v_ml_w_k, v_ml_w_v, v_ml_w_if, v_ml_b_if, v_ml_norm_w, v_ml_skip, v_hy_w_out, v_s5_a_re, v_s5_a_im, v_s5_log_step, v_s5_b_re, v_s5_b_im, v_s5_c_re, v_s5_c_im, v_s5_d, v_s5_w_a, v_s5_b_a, v_s5_w_b, v_s5_b_b, v_final_norm):
    given = dict(locals())
    w = {n: given[n] for n in WEIGHTS}
    mom = {n: given["m_" + n] for n in WEIGHTS}
    var = {n: given["v_" + n] for n in WEIGHTS}
    bl, seq, d = x.shape
    me = 4 * lax.axis_index("x") + 2 * lax.axis_index("y") + lax.axis_index("c")

    x0, tgt = x.reshape(bl * seq, d), loss_target.reshape(bl * seq, d)
    rep = {n: w[n] for n in WEIGHTS if n not in BIG and n not in SMALL_SHARDED}
    ffn_w = ("_w_gate", "_w_up", "_w_down")

    def ffn_gather(pre, l):
        return [(w[pre + s][l:l + 1].astype(bf16), "gather") for s in ffn_w]

    def scatter(parts):
        return [(p, "scatter") for p in parts]

    wf10 = tuple(exchange("gather_ffn1_l0", ffn_gather("ffn1", 0)))
    mixer_ops = [(w[n].astype(bf16), "gather") for n in ("hy_w_in", "hy_w_out")]
    mixer_ops.append((_pack([w[n] for n in SMALL_SHARDED]), "gather"))
    x1, *rest = ffn_fwd(x0, ffn1_norm[0:1], *wf10, ride=mixer_ops)
    sv10, got = rest[:3], rest[3:]
    gw = dict(zip(("hy_w_in", "hy_w_out"), got[:2]))
    gw.update(zip(SMALL_SHARDED, _unpack(got[2], [w[n].shape for n in SMALL_SHARDED])))
    w0 = assemble_hybrid(gw, rep)
    x2, sv_h, wf20, got, wf11 = hybrid_fwd(x1, w0, seq, ride_in=ffn_gather("ffn2", 0),
                                           ride_ssd=[(w[n].astype(bf16), "gather") for n in ("s5_w_a", "s5_w_b")],
                                           ride_ml=ffn_gather("ffn1", 1))
    gw.update(zip(("s5_w_a", "s5_w_b"), got))
    w1 = assemble_s5(gw, rep)
    x3, *rest = ffn_fwd(x2, ffn2_norm[0:1], *wf20, ride=ffn_gather("ffn2", 1))
    sv20, wf21 = rest[:3], tuple(rest[3:])
    x4, *sv11 = ffn_fwd(x3, ffn1_norm[1:2], *wf11)
    x5, sv_s = s5_layer_fwd(x4, w1, seq)
    x6, *sv21 = ffn_fwd(x5, ffn2_norm[1:2], *wf21)
    loss, dx6, d_final = loss_head(x6, final_norm.reshape(1, d), tgt)

    dx5, dn21, dw21, _, _ = ffn_step_bwd(dx6, x5, ffn2_norm[1:2], wf21, sv21)
    dx4, g_s5 = s5_layer_bwd(dx5, x4, w1, sv_s, seq)
    dwab = g_s5.pop("wab")
    s5_ops = scatter([_shards(dwab[None, :, :D_MODEL], 1).astype(bf16), _shards(dwab[None, :, D_MODEL:], 1).astype(bf16)])
    dx3, dn11, dw11, p21, p_s5 = ffn_step_bwd(dx4, x3, ffn1_norm[1:2], wf11, sv11, ride_act=scatter(dw21), ride_w=s5_ops)
    small = s5_small_grads(g_s5, rep)
    dx2, dn20, dw20, p11, (parts_s5,) = ffn_step_bwd(dx3, x2, ffn2_norm[0:1], wf20, sv20, ride_act=scatter(dw11),
                                                      ride_w=[(_pack([small[n] for n in SMALL_S5]), "gather")])
    dx1, g_hy, p20 = hybrid_bwd(dx2, x1, w0, sv_h, seq, ride_ml=scatter(dw20))
    hy_ops = scatter([_shards(win_from_padded(g_hy.pop("win"))[None], 2).astype(bf16), _shards(g_hy.pop("wo")[None], 1).astype(bf16)])
    h10, g10, u10 = sv10
    dx0, dn10, dg, du, a, dyh, *p_hy = ffn_bwd_act(dx1, x0, ffn1_norm[0:1], g10, u10, *wf10, ride=hy_ops)
    g_norms = {"ffn1_norm": jnp.concatenate([dn10, dn11], axis=0), "ffn2_norm": jnp.concatenate([dn20, dn21], axis=0)}
    small.update(small_grads(g_norms, g_hy, g_s5, d_final, rep))
    *dw10, parts_rest = ffn_bwd_w(h10, dyh, dg, du, a, ride=[(_pack([small[n] for n in SMALL_REST]), "gather")])
    p10 = exchange("reduce_tail", scatter(dw10))
    small_parts = jnp.concatenate([parts_rest, parts_s5], axis=1)
    small_sum = sum_parts("sum_small", small_parts, tr=_tile_rows(small_parts.shape[1], ROW))

    out_g, out_d, out_m, out_v = {}, {}, {}, {}
    ffn_parts = {"ffn1": (p10, p11), "ffn2": (p20, p21)}
    for pre in ("ffn1", "ffn2"):
        for k, s in enumerate(ffn_w):
            n = pre + s
            r, c = w[n].shape[1:]
            res = None
            for l in (1, 0):
                res = adamw_layer("adamw_" + n, ffn_parts[pre][l][k].reshape(N_DEV, r, c), w[n], mom[n], var[n], l, res,
                                  tr=_tile_rows(r, c))
            out_g[n], out_d[n], out_m[n], out_v[n] = res
    for n, parts in zip(("hy_w_in", "hy_w_out", "s5_w_a", "s5_w_b"), tuple(p_hy) + tuple(p_s5)):
        shp = w[n].shape
        w2 = _flat2d(w[n])
        res = adamw("adamw_" + n, parts.reshape((N_DEV,) + w2.shape), w2, _flat2d(mom[n]), _flat2d(var[n]),
                    tr=_tile_rows(*w2.shape))
        out_g[n], out_d[n], out_m[n], out_v[n] = [a.reshape(shp) for a in res]
    g_small = {}
    for n, full in zip(SMALL, _unpack(small_sum, [small[n].shape for n in SMALL])):
        if n in SMALL_SHARDED:
            ax = SMALL_SHARDED[n]
            full = lax.dynamic_slice_in_dim(full, me * w[n].shape[ax], w[n].shape[ax], axis=ax)
        g_small[n] = full
    packs = [_pack([t[n] for n in SMALL]) for t in (g_small, w, mom, var)]
    res = adamw("adamw_small", packs[0][None], packs[1], packs[2], packs[3], tr=_tile_rows(*packs[0].shape))
    for dst, a in zip((out_g, out_d, out_m, out_v), res):
        dst.update(zip(SMALL, _unpack(a, [w[n].shape for n in SMALL])))

    total = lax.psum(loss[0, 0], ("x", "y", "c"))
    return (total, dx0.reshape(bl, seq, d), *[out_g[n] for n in WEIGHTS], *[out_d[n] for n in WEIGHTS],
            *[out_m[n] for n in WEIGHTS], *[out_v[n] for n in WEIGHTS])
```

```python
import functools
import math

import jax
import jax.numpy as jnp
from jax import lax
from jax.experimental import pallas as pl
from jax.experimental.pallas import tpu as pltpu

f32 = jnp.float32
bf16 = jnp.bfloat16

N_DEV = 8
D_MODEL = 1024
EPS = 1e-6
FFN_RES = 0.5
CONV_W = 4
SSD_HEADS = 16
SSD_HEAD_DIM = 64
SSD_GROUPS = 2
SSD_STATE = 128
SSD_HG = SSD_HEADS // SSD_GROUPS
SSD_GW = SSD_HG * SSD_HEAD_DIM
CHUNK = 128
ML_HEADS = 4
ML_HD = 256
S5_GROUP = 16
S5_GROUPS = 64
S5_STATE = 64
S5_CB = 8
S5_CH = (S5_GROUPS // S5_CB) * S5_STATE
S5_TL = 256
S5_SUB = 16
LANES = 128
IN_COLS = 4624
PROJ_W = 4864
OFF_Z, OFF_MX, OFF_MZ, OFF_XBC, OFF_DT = 0, 1024, 2048, 3072, 4608
ADAM_LR, ADAM_B1, ADAM_B2, ADAM_EPS, ADAM_WD, ADAM_STEP = 0.001, 0.9, 0.999, 1e-08, 0.01, 10
NEG = -1e30
VMEM_LIMIT = 56 * 1024 * 1024


def _cp(n):
    return pltpu.CompilerParams(dimension_semantics=("arbitrary",) * n, vmem_limit_bytes=VMEM_LIMIT)


def _dg(a, b, ca, cb):
    return lax.dot_general(a.astype(bf16), b.astype(bf16), (((ca,), (cb,)), ((), ())), preferred_element_type=f32)


@functools.partial(jax.custom_vjp, nondiff_argnums=(2, 3))
def bdot(a, b, ca, cb):
    return _dg(a, b, ca, cb)


def _bdot_fwd(a, b, ca, cb):
    return _dg(a, b, ca, cb), (a, b)


def _bdot_bwd(ca, cb, res, ct):
    a, b = res
    da = _dg(ct, b, 1, 1 - cb) if ca == 1 else _dg(b, ct, 1 - cb, 1)
    db = _dg(a, ct, 1 - ca, 0) if cb == 0 else _dg(ct, a, 0, 1 - ca)
    return da, db


bdot.defvjp(_bdot_fwd, _bdot_bwd)


def _split3(z):
    hi = z.astype(bf16)
    r1 = z - hi.astype(f32)
    mid = r1.astype(bf16)
    return hi, mid, (r1 - mid.astype(f32)).astype(bf16)


def _sel(z, m, z_left, transpose_m):
    mm = m.astype(bf16)
    dn = lambda zz: lax.dot_general(zz, mm, (((1,), (1 if transpose_m else 0,)), ((), ())), preferred_element_type=f32) \
        if z_left else lax.dot_general(mm, zz, (((0 if transpose_m else 1,), (0,)), ((), ())), preferred_element_type=f32)
    hi, mid, lo = _split3(z)
    return dn(hi) + dn(mid) + dn(lo)


@functools.partial(jax.custom_vjp, nondiff_argnums=(2,))
def _seldot(z, m, z_left):
    return _sel(z, m, z_left, False)


_seldot.defvjp(lambda z, m, z_left: (_sel(z, m, z_left, False), m),
               lambda z_left, m, ct: (_sel(ct, m, z_left, True), jnp.zeros_like(m)))


def hdot(a, b, exact="b"):
    return _seldot(a, b.astype(f32), True) if exact == "b" else _seldot(b, a.astype(f32), False)


def _iota(shape, dim):
    return lax.broadcasted_iota(jnp.int32, shape, dim)


def _tri(n):
    return (_iota((n, n), 0) >= _iota((n, n), 1))


@functools.partial(jax.custom_vjp, nondiff_argnums=(1,))
def tshift(x, k):
    return jnp.where(_iota(x.shape, 0) >= k, pltpu.roll(x, k, 0), 0.0)


def _tshift_fwd(x, k):
    return tshift(x, k), None


def _tshift_bwd(k, _, ct):
    n = ct.shape[0]
    return (jnp.where(_iota(ct.shape, 0) < n - k, pltpu.roll(ct, n - k, 0), 0.0),)


tshift.defvjp(_tshift_fwd, _tshift_bwd)


def _lane_pick(a, idx):
    return jnp.sum(jnp.where(_iota(a.shape, 1) == idx, a, 0.0), axis=1, keepdims=True)


def _row_pick(a, idx):
    return jnp.sum(jnp.where(_iota(a.shape, 0) == idx, a, 0.0), axis=0, keepdims=True)


def _silu(x):
    return x * jax.nn.sigmoid(x)


def map_fwd(name, f, grid, ins, in_specs, out_shapes, out_specs):
    n_in = len(ins)

    def body(*refs):
        pids = tuple(pl.program_id(i) for i in range(len(grid)))
        outs = f(pids, *[r[...] for r in refs[:n_in]])
        for r, o in zip(refs[n_in:], outs):
            r[...] = o.astype(r.dtype)

    return pl.pallas_call(body, name=name, grid=grid, in_specs=in_specs, out_specs=out_specs,
                          out_shape=out_shapes, compiler_params=_cp(len(grid)))(*ins)


def scan_fwd(name, f, grid, slot_axis, ins, in_specs, out_shapes, out_specs, state_shapes, state_init, save_shapes, save_specs,
             ride=None):
    n_in, n_out, n_st = len(ins), len(out_shapes), len(state_shapes)
    n_slots = grid[slot_axis]
    cax = len(grid) - 1 if slot_axis != len(grid) - 1 else len(grid) - 2
    rider = Rider(ride)
    nr = rider.n

    def body(*refs):
        pids = tuple(pl.program_id(i) for i in range(len(grid)))
        in_refs, r_ins = refs[:n_in], refs[n_in:n_in + nr]
        o0 = n_in + nr
        out_refs, save_refs = refs[o0:o0 + n_out], refs[o0 + n_out:o0 + n_out + n_st]
        r_outs = refs[o0 + n_out + n_st:o0 + n_out + n_st + nr]
        st_refs = refs[o0 + n_out + n_st + nr:o0 + n_out + 2 * n_st + nr]
        sems = refs[o0 + n_out + 2 * n_st + nr:]
        rider.start(grid, r_ins, r_outs, sems)
        slot = pids[slot_axis]

        @pl.when(pids[cax] == 0)
        def _():
            for s, init in zip(st_refs, state_init):
                s[slot] = jnp.full(s.shape[1:], init, f32)

        states = tuple(s[slot] for s in st_refs)
        for sv, st in zip(save_refs, states):
            sv[...] = st.reshape(sv.shape)
        outs, new = f(pids, states, *[r[...] for r in in_refs])
        for r, o in zip(out_refs, outs):
            r[...] = o.astype(r.dtype)
        for s, v in zip(st_refs, new):
            s[slot] = v
        rider.wait(grid, r_ins, r_outs, sems)

    scratch = [pltpu.VMEM((n_slots,) + tuple(s), f32) for s in state_shapes]
    return pl.pallas_call(body, name=name, grid=grid, in_specs=list(in_specs) + rider.specs(),
                          out_specs=list(out_specs) + list(save_specs) + rider.specs(),
                          out_shape=list(out_shapes) + list(save_shapes) + rider.out_shapes(),
                          scratch_shapes=scratch + rider.scratch(), compiler_params=_cp(len(grid)))(*ins, *rider.arrays())


def scan_bwd(name, f, grid, slot_axis, ins, in_specs, saves, save_specs, cts, ct_specs, state_shapes, wrt, acc_first):
    n_in, n_st, n_ct = len(ins), len(saves), len(cts)
    n_slots = grid[slot_axis]
    cax = len(grid) - 1 if slot_axis != len(grid) - 1 else len(grid) - 2

    def body(*refs):
        pids = tuple(pl.program_id(i) for i in range(len(grid)))
        in_refs = refs[:n_in]
        save_refs = refs[n_in:n_in + n_st]
        ct_refs = refs[n_in + n_st:n_in + n_st + n_ct]
        out_refs = refs[n_in + n_st + n_ct:n_in + n_st + n_ct + len(wrt)]
        dst_refs = refs[n_in + n_st + n_ct + len(wrt):]
        slot = pids[slot_axis]

        @pl.when(pids[cax] == 0)
        def _():
            for s in dst_refs:
                s[slot] = jnp.zeros(s.shape[1:], f32)

        vals = [r[...] for r in in_refs]
        states = tuple(sv[...].reshape(shp) for sv, shp in zip(save_refs, state_shapes))
        ctv = tuple(r[...].astype(f32) for r in ct_refs)
        dnew = tuple(s[slot] for s in dst_refs)

        def g(st, *dv):
            full = list(vals)
            for i, v in zip(wrt, dv):
                full[i] = v
            outs, new = f(pids, st, *full)
            return tuple(outs), tuple(new)

        _, vjp = jax.vjp(g, states, *[vals[i] for i in wrt])
        grads = vjp((ctv, dnew))
        for s, v in zip(dst_refs, grads[0]):
            s[slot] = v
        for i, o_ref, gr in zip(wrt, out_refs, grads[1:]):
            first = acc_first.get(i)
            if first is None:
                o_ref[...] = gr.astype(o_ref.dtype)
            else:
                @pl.when(first(pids))
                def _():
                    o_ref[...] = jnp.zeros_like(o_ref)
                o_ref[...] += gr

    out_shapes = [jax.ShapeDtypeStruct(ins[i].shape, f32) for i in wrt]
    out_specs = [in_specs[i] for i in wrt]
    scratch = [pltpu.VMEM((n_slots,) + tuple(s), f32) for s in state_shapes]
    return pl.pallas_call(body, name=name, grid=grid, in_specs=list(in_specs) + list(save_specs) + list(ct_specs),
                          out_specs=out_specs, out_shape=out_shapes, scratch_shapes=scratch,
                          compiler_params=_cp(len(grid)))(*ins, *saves, *cts)


def _fit(dim, cap):
    if dim <= cap:
        return dim
    return max(t for t in range(LANES, cap + 1, LANES) if dim % t == 0)


def _matmul_tiles(m, n, kdim, ca):
    if ca == 1:
        return _fit(m, 512), _fit(n, 2432), _fit(kdim, 2432)
    return _fit(m, 1024), _fit(n, 1280), _fit(kdim, 512)


def matmul(name, a, b, ca=1, cb=0, add=None, out_dtype=f32, a_off=0, a_width=None, ride=None, tiles=None):
    rider = Rider(ride)
    nr = rider.n
    a_width = a.shape[1] if a_width is None else a_width
    kdim = b.shape[cb]
    n = b.shape[1 - cb]
    m = a.shape[0] if ca == 1 else a_width
    tm, tn, tk = tiles or _matmul_tiles(m, n, kdim, ca)
    assert m % tm == 0 and n % tn == 0 and kdim % tk == 0
    nk = kdim // tk
    if ca == 1:
        assert a_off % tk == 0 and a_width == kdim
        koff = a_off // tk
        a_spec = pl.BlockSpec((tm, tk), lambda i, j, k: (i, k + koff))
    else:
        assert a_off % tm == 0 and a.shape[0] == kdim
        ioff = a_off // tm
        a_spec = pl.BlockSpec((tk, tm), lambda i, j, k: (k, i + ioff))
    b_spec = pl.BlockSpec((tk, tn), lambda i, j, k: (k, j)) if cb == 0 else pl.BlockSpec((tn, tk), lambda i, j, k: (j, k))
    o_spec = pl.BlockSpec((tm, tn), lambda i, j, k: (i, j))
    has_add = add is not None

    n_in = 3 if has_add else 2
    grid = (m // tm, n // tn, nk)

    def body(*refs):
        a_ref, b_ref = refs[0], refs[1]
        add_ref = refs[2] if has_add else None
        r_ins, o_ref = refs[n_in:n_in + nr], refs[n_in + nr]
        r_outs, acc, sems = refs[n_in + nr + 1:n_in + 2 * nr + 1], refs[n_in + 2 * nr + 1], refs[n_in + 2 * nr + 2:]
        rider.start(grid, r_ins, r_outs, sems)
        k = pl.program_id(2)

        @pl.when(k == 0)
        def _():
            acc[...] = add_ref[...].astype(f32) if has_add else jnp.zeros_like(acc)

        acc[...] += _dg(a_ref[...], b_ref[...], ca, cb)

        @pl.when(k == nk - 1)
        def _():
            o_ref[...] = acc[...].astype(o_ref.dtype)

        rider.wait(grid, r_ins, r_outs, sems)

    ins = [a, b] + ([add] if has_add else [])
    specs = [a_spec, b_spec] + ([o_spec] if has_add else [])
    res = pl.pallas_call(body, name=name, grid=grid, in_specs=specs + rider.specs(), out_specs=[o_spec] + rider.specs(),
                         out_shape=[jax.ShapeDtypeStruct((m, n), out_dtype)] + rider.out_shapes(),
                         scratch_shapes=[pltpu.VMEM((tm, tn), f32)] + rider.scratch(), compiler_params=_cp(3))(*ins, *rider.arrays())
    return res if nr else res[0]


def f_rms(pids, x, w):
    r = lax.rsqrt(jnp.mean(x * x, axis=-1, keepdims=True) + EPS)
    return (x * r * w,)


def _row_spec(tm, width, col=0):
    return pl.BlockSpec((tm, width), lambda i: (i, col))


def _par_spec(shape):
    return pl.BlockSpec(shape, lambda *p: (0,) * len(shape))


def rms_fwd(x, w, tm=512):
    t, d = x.shape
    return map_fwd("rms_fwd", f_rms, (t // tm,), [x, w], [_row_spec(tm, d), _par_spec((1, d))],
                   [jax.ShapeDtypeStruct((t, d), f32)], [_row_spec(tm, d)])[0]


def rms_bwd(dys, x, w, dres, tm=512):
    t, d = x.shape
    n = len(dys)

    def body(*refs):
        x_ref, w_ref, dres_ref, dx_ref, dw_ref = refs[n:]
        dy = refs[0][...]
        for r in refs[1:n]:
            dy = dy + r[...]
        _, vjp = jax.vjp(lambda xx, ww: f_rms(None, xx, ww)[0], x_ref[...], w_ref[...])
        dx, dw = vjp(dy)
        dx_ref[...] = dx + dres_ref[...]

        @pl.when(pl.program_id(0) == 0)
        def _():
            dw_ref[...] = jnp.zeros_like(dw_ref)
        dw_ref[...] += dw

    return pl.pallas_call(body, name="rms_bwd", grid=(t // tm,),
                          in_specs=[_row_spec(tm, d)] * (n + 1) + [_par_spec((1, d)), _row_spec(tm, d)],
                          out_specs=[_row_spec(tm, d), _par_spec((1, d))],
                          out_shape=[jax.ShapeDtypeStruct((t, d), f32), jax.ShapeDtypeStruct((1, d), f32)],
                          compiler_params=_cp(1))(*dys, x, w, dres)


def loss_head(x, w, tgt, tm=512):
    t, d = x.shape

    def fl(xx, ww, tt):
        y = f_rms(None, xx, ww)[0]
        return 0.5 * jnp.sum(jnp.mean(jnp.square(y - tt), axis=-1, keepdims=True), axis=0, keepdims=True)

    def body(x_ref, w_ref, t_ref, loss_ref, dx_ref, dw_ref):
        val, vjp = jax.vjp(lambda xx, ww: fl(xx, ww, t_ref[...]), x_ref[...], w_ref[...])
        dx, dw = vjp(jnp.ones((1, 1), f32))
        dx_ref[...] = dx

        @pl.when(pl.program_id(0) == 0)
        def _():
            dw_ref[...] = jnp.zeros_like(dw_ref)
            loss_ref[...] = jnp.zeros_like(loss_ref)
        dw_ref[...] += dw
        loss_ref[...] += val

    return pl.pallas_call(body, name="loss_head", grid=(t // tm,),
                          in_specs=[_row_spec(tm, d), _par_spec((1, d)), _row_spec(tm, d)],
                          out_specs=[_par_spec((1, 1)), _row_spec(tm, d), _par_spec((1, d))],
                          out_shape=[jax.ShapeDtypeStruct((1, 1), f32), jax.ShapeDtypeStruct((t, d), f32),
                                     jax.ShapeDtypeStruct((1, d), f32)],
                          compiler_params=_cp(1))(x, w, tgt)


def ffn_fwd(x, nw, wg, wu, wd, tm=1024, ride=None):
    t, d = x.shape
    ns, _, _, fs = wg.shape
    rider = Rider(ride)
    nr = rider.n
    grid = (t // tm, ns)

    def body(*refs):
        x_ref, nw_ref, wg_ref, wu_ref, wd_ref = refs[:5]
        r_ins = refs[5:5 + nr]
        xo_ref, h_ref, g_ref, u_ref = refs[5 + nr:9 + nr]
        r_outs, acc, sems = refs[9 + nr:9 + 2 * nr], refs[9 + 2 * nr], refs[10 + 2 * nr:]
        rider.start(grid, r_ins, r_outs, sems)
        j = pl.program_id(1)

        @pl.when(j == 0)
        def _():
            h_ref[...] = f_rms(None, x_ref[...], nw_ref[...])[0].astype(bf16)
            acc[...] = jnp.zeros_like(acc)

        h = h_ref[...]
        g = jnp.dot(h, wg_ref[0, 0], preferred_element_type=f32)
        u = jnp.dot(h, wu_ref[0, 0], preferred_element_type=f32)
        g_ref[0] = g
        u_ref[0] = u
        acc[...] += jnp.dot((_silu(g) * u).astype(bf16), wd_ref[0, 0], preferred_element_type=f32)

        @pl.when(j == ns - 1)
        def _():
            xo_ref[...] = x_ref[...] + FFN_RES * acc[...]

        rider.wait(grid, r_ins, r_outs, sems)

    row = pl.BlockSpec((tm, d), lambda i, j: (i, 0))
    wcol = pl.BlockSpec((1, 1, d, fs), lambda i, j: (j, 0, 0, 0))
    wrow = pl.BlockSpec((1, 1, fs, d), lambda i, j: (j, 0, 0, 0))
    act = pl.BlockSpec((1, tm, fs), lambda i, j: (j, i, 0))
    return pl.pallas_call(body, name="ffn_fwd", grid=grid,
                          in_specs=[row, pl.BlockSpec((1, d), lambda i, j: (0, 0)), wcol, wcol, wrow] + rider.specs(),
                          out_specs=[row, row, act, act] + rider.specs(),
                          out_shape=[jax.ShapeDtypeStruct((t, d), f32), jax.ShapeDtypeStruct((t, d), bf16),
                                     jax.ShapeDtypeStruct((ns, t, fs), f32), jax.ShapeDtypeStruct((ns, t, fs), f32)]
                          + rider.out_shapes(),
                          scratch_shapes=[pltpu.VMEM((tm, d), f32)] + rider.scratch(),
                          compiler_params=_cp(2))(x, nw, wg, wu, wd, *rider.arrays())


def ffn_bwd_act(dy, x, nw, g, u, wg, wu, wd, tm=512, ride=None):
    t, d = x.shape
    ns, _, _, fs = wg.shape
    rider = Rider(ride)
    nr = rider.n
    grid = (t // tm, ns)

    def body(*refs):
        dy_ref, x_ref, nw_ref, g_ref, u_ref, wg_ref, wu_ref, wd_ref = refs[:8]
        r_ins = refs[8:8 + nr]
        dx_ref, dnw_ref, dg_ref, du_ref, a_ref, dyh_ref = refs[8 + nr:14 + nr]
        r_outs, acc, sems = refs[14 + nr:14 + 2 * nr], refs[14 + 2 * nr], refs[15 + 2 * nr:]
        rider.start(grid, r_ins, r_outs, sems)
        i, j = pl.program_id(0), pl.program_id(1)

        @pl.when(j == 0)
        def _():
            acc[...] = jnp.zeros_like(acc)
            dyh_ref[...] = (FFN_RES * dy_ref[...]).astype(bf16)

        dyh = dyh_ref[...]
        da = _dg(dyh, wd_ref[0, 0], 1, 1)
        gg, uu = g_ref[0], u_ref[0]
        sg = jax.nn.sigmoid(gg)
        si = gg * sg
        dgv = (da * uu * (sg * (1.0 + gg * (1.0 - sg)))).astype(bf16)
        duv = (da * si).astype(bf16)
        dg_ref[0] = dgv
        du_ref[0] = duv
        a_ref[0] = (si * uu).astype(bf16)
        acc[...] += _dg(dgv, wg_ref[0, 0], 1, 1) + _dg(duv, wu_ref[0, 0], 1, 1)

        @pl.when(j == ns - 1)
        def _():
            _, vjp = jax.vjp(lambda xx, ww: f_rms(None, xx, ww)[0], x_ref[...], nw_ref[...])
            dx, dw = vjp(acc[...])
            dx_ref[...] = dx + dy_ref[...]

            @pl.when(i == 0)
            def _():
                dnw_ref[...] = jnp.zeros_like(dnw_ref)
            dnw_ref[...] += dw

        rider.wait(grid, r_ins, r_outs, sems)

    row = pl.BlockSpec((tm, d), lambda i, j: (i, 0))
    wcol = pl.BlockSpec((1, 1, d, fs), lambda i, j: (j, 0, 0, 0))
    wrow = pl.BlockSpec((1, 1, fs, d), lambda i, j: (j, 0, 0, 0))
    act = pl.BlockSpec((1, tm, fs), lambda i, j: (j, i, 0))
    par = pl.BlockSpec((1, d), lambda i, j: (0, 0))
    return pl.pallas_call(body, name="ffn_bwd_act", grid=grid,
                          in_specs=[row, row, par, act, act, wcol, wcol, wrow] + rider.specs(),
                          out_specs=[row, par, act, act, act, row] + rider.specs(),
                          out_shape=[jax.ShapeDtypeStruct((t, d), f32), jax.ShapeDtypeStruct((1, d), f32)]
                          + [jax.ShapeDtypeStruct((ns, t, fs), bf16)] * 3 + [jax.ShapeDtypeStruct((t, d), bf16)]
                          + rider.out_shapes(),
                          scratch_shapes=[pltpu.VMEM((tm, d), f32)] + rider.scratch(),
                          compiler_params=_cp(2))(dy, x, nw, g, u, wg, wu, wd, *rider.arrays())


def ffn_bwd_w(h, dyh, dg, du, a, tk=1024, ride=None):
    t, d = h.shape
    ns, _, fs = dg.shape
    nk = t // tk
    rider = Rider(ride)
    nr = rider.n
    grid = (ns, nk)

    def body(*refs):
        h_ref, dy_ref, dg_ref, du_ref, a_ref = refs[:5]
        r_ins = refs[5:5 + nr]
        og, ou, od = refs[5 + nr:8 + nr]
        r_outs = refs[8 + nr:8 + 2 * nr]
        ag, au, ad = refs[8 + 2 * nr:11 + 2 * nr]
        sems = refs[11 + 2 * nr:]
        rider.start(grid, r_ins, r_outs, sems)
        k = pl.program_id(1)

        @pl.when(k == 0)
        def _():
            ag[...] = jnp.zeros_like(ag)
            au[...] = jnp.zeros_like(au)
            ad[...] = jnp.zeros_like(ad)

        hh = h_ref[...]
        ag[...] += _dg(hh, dg_ref[0], 0, 0)
        au[...] += _dg(hh, du_ref[0], 0, 0)
        ad[...] += _dg(a_ref[0], dy_ref[...], 0, 0)

        @pl.when(k == nk - 1)
        def _():
            og[0, 0] = ag[...].astype(og.dtype)
            ou[0, 0] = au[...].astype(ou.dtype)
            od[0, 0] = ad[...].astype(od.dtype)

        rider.wait(grid, r_ins, r_outs, sems)

    row = pl.BlockSpec((tk, d), lambda j, k: (k, 0))
    act = pl.BlockSpec((1, tk, fs), lambda j, k: (j, k, 0))
    wcol = pl.BlockSpec((1, 1, d, fs), lambda j, k: (j, 0, 0, 0))
    wrow = pl.BlockSpec((1, 1, fs, d), lambda j, k: (j, 0, 0, 0))
    return pl.pallas_call(body, name="ffn_bwd_w", grid=grid, in_specs=[row, row, act, act, act] + rider.specs(),
                          out_specs=[wcol, wcol, wrow] + rider.specs(),
                          out_shape=[jax.ShapeDtypeStruct((ns, 1, d, fs), bf16)] * 2
                          + [jax.ShapeDtypeStruct((ns, 1, fs, d), bf16)] + rider.out_shapes(),
                          scratch_shapes=[pltpu.VMEM((d, fs), f32), pltpu.VMEM((d, fs), f32), pltpu.VMEM((fs, d), f32)]
                          + rider.scratch(),
                          compiler_params=_cp(2))(h, dyh, dg, du, a, *rider.arrays())


def f_conv(pids, x, w, b):
    y = b + x * w[CONV_W - 1:CONV_W, :]
    for j in range(CONV_W - 1):
        y = y + tshift(x, CONV_W - 1 - j) * w[j:j + 1, :]
    return (_silu(y),)


def _conv_specs(seq, col0, cb):
    xs = pl.BlockSpec((seq, cb), lambda c, b: (b, col0 + c))
    ws = pl.BlockSpec((CONV_W, cb), lambda c, b: (0, c))
    bs = pl.BlockSpec((1, cb), lambda c, b: (0, c))
    ys = pl.BlockSpec((seq, cb), lambda c, b: (b, c))
    return xs, ws, bs, ys


def conv_fwd(name, src, col_off, w, b, seq, cb=256):
    t = src.shape[0]
    c = w.shape[1]
    xs, ws, bs, ys = _conv_specs(seq, col_off // cb, cb)
    return map_fwd(name, f_conv, (c // cb, t // seq), [src, w, b], [xs, ws, bs],
                   [jax.ShapeDtypeStruct((t, c), f32)], [ys])[0]


def conv_bwd(name, dy, src, col_off, w, b, seq, cb=256):
    t = src.shape[0]
    c = w.shape[1]
    xs, ws, bs, ys = _conv_specs(seq, col_off // cb, cb)

    def body(x_ref, w_ref, b_ref, dy_ref, dx_ref, dw_ref, db_ref):
        _, vjp = jax.vjp(lambda xx, ww, bb: f_conv(None, xx, ww, bb)[0], x_ref[...], w_ref[...], b_ref[...])
        dx, dw, db = vjp(dy_ref[...])
        dx_ref[...] = dx

        @pl.when(pl.program_id(1) == 0)
        def _():
            dw_ref[...] = jnp.zeros_like(dw_ref)
            db_ref[...] = jnp.zeros_like(db_ref)
        dw_ref[...] += dw
        db_ref[...] += db

    return pl.pallas_call(body, name=name, grid=(c // cb, t // seq), in_specs=[xs, ws, bs, ys], out_specs=[ys, ws, bs],
                          out_shape=[jax.ShapeDtypeStruct((t, c), f32), jax.ShapeDtypeStruct(w.shape, f32),
                                     jax.ShapeDtypeStruct(b.shape, f32)], compiler_params=_cp(2))(src, w, b, dy)


def f_ssd(pids, states, xs, dtraw, bm, cm, a_log, dt_bias, d_skip):
    g = pids[2]
    (hn,) = states
    l = xs.shape[0]
    head_of_lane = _iota((LANES, SSD_GW), 1) // SSD_HEAD_DIM + SSD_HG * g
    expand = (_iota((LANES, SSD_GW), 0) == head_of_lane).astype(f32)
    tri = _tri(l)
    dt = jax.nn.softplus(dtraw + dt_bias)
    adt = dt * (-jnp.exp(a_log))
    cs = hdot(tri, adt, exact="a")
    cst = cs.T
    cs_last = cs[l - 1:l, :]
    dt_e, cs_e, csl_e = hdot(dt, expand), hdot(cs, expand), hdot(cs_last, expand)
    xd = xs * dt_e
    gmat = bdot(cm, bm, 1, 1)
    half = _iota((l, LANES), 1) < SSD_HEAD_DIM
    blocks = []
    for pair in range(SSD_HG // 2):
        xb = xd[:, pair * LANES:(pair + 1) * LANES]
        res = []
        for sub in range(2):
            hid = SSD_HG * g + 2 * pair + sub
            col, row = _lane_pick(cs, hid), _row_pick(cst, hid)
            lm = jnp.exp(jnp.where(tri, col - row, NEG))
            res.append(bdot(gmat * lm, xb, 1, 0))
        blocks.append(jnp.where(half, res[0], res[1]))
    y = jnp.concatenate(blocks, axis=1)
    y = y + jnp.exp(cs_e) * bdot(cm, hn, 1, 0)
    y = y + hdot(d_skip, expand) * xs
    hn_new = jnp.exp(csl_e) * hn + bdot(bm, jnp.exp(csl_e - cs_e) * xd, 0, 0)
    return (y,), (hn_new,)


def _ssd_specs(seq, nch, rev):
    cc = (lambda c: nch - 1 - c) if rev else (lambda c: c)
    xs = pl.BlockSpec((CHUNK, SSD_GW), lambda b, c, g: (b * nch + cc(c), g))
    dt = pl.BlockSpec((CHUNK, LANES), lambda b, c, g: (b * nch + cc(c), OFF_DT // LANES))
    bm = pl.BlockSpec((CHUNK, SSD_STATE), lambda b, c, g: (b * nch + cc(c), 1024 // SSD_STATE + g))
    cm = pl.BlockSpec((CHUNK, SSD_STATE), lambda b, c, g: (b * nch + cc(c), 1024 // SSD_STATE + SSD_GROUPS + g))
    par = pl.BlockSpec((1, LANES), lambda b, c, g: (0, 0))
    sv = pl.BlockSpec((1, 1, SSD_STATE, SSD_GW), lambda b, c, g: (b * nch + cc(c), g, 0, 0))
    ddt = pl.BlockSpec((CHUNK, LANES), lambda b, c, g: (b * nch + cc(c), 0))
    dbc = pl.BlockSpec((CHUNK, SSD_STATE), lambda b, c, g: (b * nch + cc(c), g))
    return xs, dt, bm, cm, par, sv, ddt, dbc


def ssd_fwd(xbc, proj, a_log, dt_bias, d_skip, seq, ride=None):
    t = xbc.shape[0]
    nch = seq // CHUNK
    xs, dt, bm, cm, par, sv, _, _ = _ssd_specs(seq, nch, False)
    grid = (t // seq, nch, SSD_GROUPS)
    y, hsave, *got = scan_fwd("ssd_fwd", f_ssd, grid, 2, [xbc, proj, xbc, xbc, a_log, dt_bias, d_skip],
                              [xs, dt, bm, cm, par, par, par], [jax.ShapeDtypeStruct((t, SSD_GROUPS * SSD_GW), f32)], [xs],
                              [(SSD_STATE, SSD_GW)], [0.0],
                              [jax.ShapeDtypeStruct((t // CHUNK, SSD_GROUPS, SSD_STATE, SSD_GW), f32)], [sv], ride=ride)
    return y, hsave, got


def ssd_bwd(dy, xbc, proj, a_log, dt_bias, d_skip, hsave, seq):
    t = xbc.shape[0]
    nch = seq // CHUNK
    xs, dt, bm, cm, par, sv, ddt, dbc = _ssd_specs(seq, nch, True)
    grid = (t // seq, nch, SSD_GROUPS)

    def body(x_ref, dt_ref, b_ref, c_ref, al_ref, db_ref, ds_ref, h_ref, dy_ref,
             dxbc_x, dxbc_b, dxbc_c, ddt_ref, dal_ref, ddb_ref, dds_ref, dst):
        pids = tuple(pl.program_id(i) for i in range(3))
        slot = pids[2]

        @pl.when(pids[1] == 0)
        def _():
            dst[slot] = jnp.zeros(dst.shape[1:], f32)

        vals = [x_ref[...], dt_ref[...], b_ref[...], c_ref[...], al_ref[...], db_ref[...], ds_ref[...]]

        def gfun(st, *v):
            outs, new = f_ssd(pids, (st,), *v)
            return outs[0], new[0]

        _, vjp = jax.vjp(gfun, h_ref[0, 0], *vals)
        grads = vjp((dy_ref[...], dst[slot]))
        dst[slot] = grads[0]
        dxbc_x[...] = grads[1]
        dxbc_b[...] = grads[3]
        dxbc_c[...] = grads[4]

        @pl.when(slot == 0)
        def _():
            ddt_ref[...] = jnp.zeros_like(ddt_ref)
        ddt_ref[...] += grads[2]
        first = jnp.logical_and(jnp.logical_and(pids[0] == 0, pids[1] == 0), slot == 0)

        @pl.when(first)
        def _():
            dal_ref[...] = jnp.zeros_like(dal_ref)
            ddb_ref[...] = jnp.zeros_like(ddb_ref)
            dds_ref[...] = jnp.zeros_like(dds_ref)
        dal_ref[...] += grads[5]
        ddb_ref[...] += grads[6]
        dds_ref[...] += grads[7]

    bc_shape = jax.ShapeDtypeStruct((t, SSD_GROUPS * SSD_STATE), f32)
    par_shape = jax.ShapeDtypeStruct((1, LANES), f32)
    outs = pl.pallas_call(body, name="ssd_bwd", grid=grid, in_specs=[xs, dt, bm, cm, par, par, par, sv, xs],
                          out_specs=[xs, dbc, dbc, ddt, par, par, par],
                          out_shape=[jax.ShapeDtypeStruct((t, SSD_GROUPS * SSD_GW), f32),
                                     bc_shape, bc_shape, jax.ShapeDtypeStruct((t, LANES), f32),
                                     par_shape, par_shape, par_shape],
                          scratch_shapes=[pltpu.VMEM((SSD_GROUPS, SSD_STATE, SSD_GW), f32)],
                          compiler_params=_cp(3))(xbc, proj, xbc, xbc, a_log, dt_bias, d_skip, hsave, dy)
    return outs


def f_ssd_epi(pids, y, z, nw):
    yg = y * _silu(z)
    hw = yg.shape[1] // SSD_GROUPS
    parts = []
    for g in range(SSD_GROUPS):
        p = yg[:, g * hw:(g + 1) * hw]
        parts.append(p * lax.rsqrt(jnp.mean(p * p, axis=-1, keepdims=True) + EPS))
    return (jnp.concatenate(parts, axis=1) * nw,)


def f_ml_epi(pids, hm, xc, mz, nw, skip):
    parts = []
    for h in range(ML_HEADS):
        p = hm[:, h * ML_HD:(h + 1) * ML_HD]
        mu = jnp.mean(p, axis=-1, keepdims=True)
        var = jnp.mean(jnp.square(p - mu), axis=-1, keepdims=True)
        parts.append((p - mu) * lax.rsqrt(var + EPS))
    hn = jnp.concatenate(parts, axis=1) * nw
    return ((hn + skip * xc) * _silu(mz),)


def f_s5_post(pids, ys, u, d_skip):
    return (jax.nn.gelu(ys + d_skip * u),)


def f_glu(pids, pab, ba, bb):
    d = ba.shape[1]
    return ((pab[:, :d] + ba) * jax.nn.sigmoid(pab[:, d:] + bb),)


def f_glu_res(pids, pab, xres, ba, bb):
    return (xres + f_glu(pids, pab, ba, bb)[0],)


def rowwise_fwd(name, f, rows, row_cols, pars, out_width, tm=512):
    t = rows[0].shape[0]
    specs = [_row_spec(tm, w, c) for (w, c) in row_cols] + [_par_spec(p.shape) for p in pars]
    return map_fwd(name, f, (t // tm,), list(rows) + list(pars), specs, [jax.ShapeDtypeStruct((t, out_width), f32)],
                   [_row_spec(tm, out_width)])[0]


def rowwise_bwd(name, f, rows, row_cols, pars, dy, tm=256):
    t = rows[0].shape[0]
    n_r, n_p = len(rows), len(pars)
    specs = [_row_spec(tm, w, c) for (w, c) in row_cols] + [_par_spec(p.shape) for p in pars]
    out_w = dy.shape[1]

    def body(*refs):
        vals = [r[...] for r in refs[:n_r + n_p]]
        dy_ref = refs[n_r + n_p]
        outs = refs[n_r + n_p + 1:]
        _, vjp = jax.vjp(lambda *v: f(None, *v)[0], *vals)
        grads = vjp(dy_ref[...])
        for k in range(n_r):
            outs[k][...] = grads[k]

        @pl.when(pl.program_id(0) == 0)
        def _():
            for k in range(n_p):
                outs[n_r + k][...] = jnp.zeros_like(outs[n_r + k])
        for k in range(n_p):
            outs[n_r + k][...] += grads[n_r + k]

    out_shapes = [jax.ShapeDtypeStruct((t, w), f32) for (w, c) in row_cols] + [jax.ShapeDtypeStruct(p.shape, f32) for p in pars]
    out_specs = [_row_spec(tm, w) for (w, c) in row_cols] + [_par_spec(p.shape) for p in pars]
    return pl.pallas_call(body, name=name, grid=(t // tm,), in_specs=specs + [_row_spec(tm, out_w)], out_specs=out_specs,
                          out_shape=out_shapes, compiler_params=_cp(1))(*rows, *pars, dy)


def f_ml(pids, states, q, k, v, g1, g2, g3, b_if):
    h = pids[2]
    cst, nst, mst = states
    l = q.shape[0]
    gt = g1 + g2 + g3 + b_if
    k = k * (1.0 / math.sqrt(ML_HD))
    tri = _tri(l)
    bc_all = hdot(tri, jax.nn.log_sigmoid(gt), exact="a")
    bcum, ig = _lane_pick(bc_all, ML_HEADS + h), _lane_pick(gt, h)
    bcum_t, ig_t = _row_pick(bc_all.T, ML_HEADS + h), _row_pick(gt.T, h)
    b_last = bcum[l - 1:l, :]
    dlog = jnp.where(tri, bcum - bcum_t + ig_t, NEG)
    ws = b_last - bcum + ig
    m_prev = mst[:, 0:1]
    m_new = lax.stop_gradient(jnp.maximum(b_last + m_prev, jnp.max(ws, axis=0, keepdims=True)))
    decay = jnp.exp(b_last + m_prev - m_new)
    wts = jnp.exp(ws - m_new)
    c_new = decay * cst + bdot(wts * v, k, 0, 0)
    n_new = decay * nst + jnp.sum(wts * k, axis=0, keepdims=True)
    m_inter = bcum + m_prev
    m_t = lax.stop_gradient(jnp.maximum(jnp.max(dlog, axis=1, keepdims=True), m_inter))
    scores = bdot(q, k, 1, 1) * jnp.exp(dlog - m_t)
    inter_w = jnp.exp(m_inter - m_t)
    num = bdot(scores, v, 1, 0) + inter_w * bdot(q, cst, 1, 1)
    den = jnp.sum(scores, axis=1, keepdims=True) + inter_w * jnp.sum(q * nst, axis=1, keepdims=True)
    hout = num / jnp.maximum(jnp.abs(den), jnp.exp(-m_t))
    return (hout,), (c_new, n_new, jnp.broadcast_to(m_new, mst.shape))


def _ml_specs(nch, rev):
    cc = (lambda c: nch - 1 - c) if rev else (lambda c: c)
    hd = pl.BlockSpec((CHUNK, ML_HD), lambda b, c, h: (b * nch + cc(c), h))
    gt = pl.BlockSpec((CHUNK, LANES), lambda b, c, h: (b * nch + cc(c), 0))
    par = pl.BlockSpec((1, LANES), lambda b, c, h: (0, 0))
    sc = pl.BlockSpec((1, 1, ML_HD, ML_HD), lambda b, c, h: (b * nch + cc(c), h, 0, 0))
    sn = pl.BlockSpec((1, 1, 1, ML_HD), lambda b, c, h: (b * nch + cc(c), h, 0, 0))
    sm = pl.BlockSpec((1, 1, 1, LANES), lambda b, c, h: (b * nch + cc(c), h, 0, 0))
    return hd, gt, par, sc, sn, sm


ML_STATE_SHAPES = [(ML_HD, ML_HD), (1, ML_HD), (1, LANES)]


def ml_fwd(q, k, v, g1, g2, g3, b_if, seq, ride=None):
    t = q.shape[0]
    nch = seq // CHUNK
    hd, gt, par, sc, sn, sm = _ml_specs(nch, False)
    nc = t // CHUNK
    outs = scan_fwd("ml_fwd", f_ml, (t // seq, nch, ML_HEADS), 2, [q, k, v, g1, g2, g3, b_if],
                    [hd, hd, hd, gt, gt, gt, par], [jax.ShapeDtypeStruct((t, ML_HEADS * ML_HD), f32)], [hd],
                    ML_STATE_SHAPES, [0.0, 0.0, NEG],
                    [jax.ShapeDtypeStruct((nc, ML_HEADS, ML_HD, ML_HD), f32), jax.ShapeDtypeStruct((nc, ML_HEADS, 1, ML_HD), f32),
                     jax.ShapeDtypeStruct((nc, ML_HEADS, 1, LANES), f32)], [sc, sn, sm], ride=ride)
    return outs[0], outs[1:4], outs[4:]


def ml_bwd(dh, q, k, v, g1, g2, g3, b_if, saves, seq, ride=None):
    t = q.shape[0]
    nch = seq // CHUNK
    hd, gt, par, sc, sn, sm = _ml_specs(nch, True)
    rider = Rider(ride)
    nr = rider.n
    grid = (t // seq, nch, ML_HEADS)

    def f(pids, states, q, k, v, gsum, b_if):
        return f_ml(pids, states, q, k, v, gsum, jnp.zeros_like(gsum), jnp.zeros_like(gsum), b_if)

    def body(*refs):
        q_ref, k_ref, v_ref, g1_ref, g2_ref, g3_ref, b_ref, c_ref, n_ref, m_ref, dh_ref = refs[:11]
        r_ins = refs[11:11 + nr]
        dq_ref, dk_ref, dv_ref, dg_ref, db_ref = refs[11 + nr:16 + nr]
        r_outs = refs[16 + nr:16 + 2 * nr]
        dc_s, dn_s = refs[16 + 2 * nr:18 + 2 * nr]
        sems = refs[18 + 2 * nr:]
        rider.start(grid, r_ins, r_outs, sems)
        pids = tuple(pl.program_id(i) for i in range(3))
        slot = pids[2]

        @pl.when(pids[1] == 0)
        def _():
            dc_s[slot] = jnp.zeros(dc_s.shape[1:], f32)
            dn_s[slot] = jnp.zeros(dn_s.shape[1:], f32)

        gsum = g1_ref[...] + g2_ref[...] + g3_ref[...]
        mst = m_ref[0, 0]

        def gfun(cst, nst, qq, kk, vv, gs, bb):
            outs, new = f(pids, (cst, nst, mst), qq, kk, vv, gs, bb)
            return outs[0], new[0], new[1]

        _, vjp = jax.vjp(gfun, c_ref[0, 0], n_ref[0, 0], q_ref[...], k_ref[...], v_ref[...], gsum, b_ref[...])
        grads = vjp((dh_ref[...], dc_s[slot], dn_s[slot]))
        dc_s[slot] = grads[0]
        dn_s[slot] = grads[1]
        dq_ref[...] = grads[2]
        dk_ref[...] = grads[3]
        dv_ref[...] = grads[4]

        @pl.when(slot == 0)
        def _():
            dg_ref[...] = jnp.zeros_like(dg_ref)
        dg_ref[...] += grads[5]
        first = jnp.logical_and(jnp.logical_and(pids[0] == 0, pids[1] == 0), slot == 0)

        @pl.when(first)
        def _():
            db_ref[...] = jnp.zeros_like(db_ref)
        db_ref[...] += grads[6]
        rider.wait(grid, r_ins, r_outs, sems)

    big = jax.ShapeDtypeStruct((t, ML_HEADS * ML_HD), f32)
    return pl.pallas_call(body, name="ml_bwd", grid=grid,
                          in_specs=[hd, hd, hd, gt, gt, gt, par, sc, sn, sm, hd] + rider.specs(),
                          out_specs=[hd, hd, hd, gt, par] + rider.specs(),
                          out_shape=[big, big, big, jax.ShapeDtypeStruct((t, LANES), f32), jax.ShapeDtypeStruct((1, LANES), f32)]
                          + rider.out_shapes(),
                          scratch_shapes=[pltpu.VMEM((ML_HEADS, ML_HD, ML_HD), f32), pltpu.VMEM((ML_HEADS, 1, ML_HD), f32)]
                          + rider.scratch(),
                          compiler_params=_cp(3))(q, k, v, g1, g2, g3, b_if, *saves, dh, *rider.arrays())


def _block_prefix(z, transpose):
    n = z.shape[0]
    r, c = _iota((n, n), 0), _iota((n, n), 1)
    keep = jnp.logical_and(r // S5_SUB == c // S5_SUB, (c >= r) if transpose else (c <= r))
    m = jnp.where(keep, 1.0, 0.0).astype(bf16)
    hi = z.astype(bf16)
    lo = (z - hi.astype(f32)).astype(bf16)
    return jnp.dot(m, hi, preferred_element_type=f32) + jnp.dot(m, lo, preferred_element_type=f32)


@jax.custom_vjp
def block_prefix(z):
    return _block_prefix(z, False)


block_prefix.defvjp(lambda z: (_block_prefix(z, False), None), lambda _, ct: (_block_prefix(ct, True),))


def _cmul(a, b):
    h = b.shape[1] // 2
    ar, ai, br, bi = a[:, :h], a[:, h:], b[:, :h], b[:, h:]
    return jnp.concatenate([ar * br - ai * bi, ar * bi + ai * br], axis=1)


def f_s5(pids, states, u, bb, cc, tab):
    (carry,) = states
    tl = u.shape[0]
    nsub = tl // S5_SUB
    rep = lambda t: jnp.concatenate([t] * nsub, axis=0)
    p1, p0, q0 = tab[0:S5_SUB], tab[S5_SUB:2 * S5_SUB], tab[2 * S5_SUB:3 * S5_SUB]
    lam_sub = tab[S5_SUB - 1:S5_SUB]
    bu = bdot(u, bb, 1, 0)
    xl = _cmul(rep(p0), block_prefix(_cmul(rep(q0), bu)))
    e, entering = carry, []
    for k in range(nsub):
        entering.append(jnp.broadcast_to(e, (S5_SUB, e.shape[1])))
        e = xl[(k + 1) * S5_SUB - 1:(k + 1) * S5_SUB] + _cmul(lam_sub, e)
    x = xl + _cmul(rep(p1), jnp.concatenate(entering, axis=0))
    y = bdot(x, cc, 1, 0)
    return (y,), (e,)


def _s5_specs(ntl, rev):
    tt = (lambda t: ntl - 1 - t) if rev else (lambda t: t)
    us = pl.BlockSpec((S5_TL, LANES), lambda c, b, t: (b * ntl + tt(t), c))
    bbs = pl.BlockSpec((1, LANES, 2 * S5_CH), lambda c, b, t: (c, 0, 0))
    ccs = pl.BlockSpec((1, 2 * S5_CH, LANES), lambda c, b, t: (c, 0, 0))
    pws = pl.BlockSpec((1, 3 * S5_SUB, 2 * S5_CH), lambda c, b, t: (c, 0, 0))
    sv = pl.BlockSpec((1, 1, 1, 2 * S5_CH), lambda c, b, t: (b * ntl + tt(t), c, 0, 0))
    return us, bbs, ccs, pws, sv


def s5_fwd(u, bb, cc, pw, seq):
    t = u.shape[0]
    ntl = seq // S5_TL
    us, bbs, ccs, pws, sv = _s5_specs(ntl, False)

    def f(pids, states, uu, b3, c3, p3):
        return f_s5(pids, states, uu, b3[0], c3[0], p3[0])

    y, carries = scan_fwd("s5_fwd", f, (S5_CB, t // seq, ntl), 0, [u, bb, cc, pw], [us, bbs, ccs, pws],
                          [jax.ShapeDtypeStruct((t, S5_CB * LANES), f32)], [us], [(1, 2 * S5_CH)], [0.0],
                          [jax.ShapeDtypeStruct((t // S5_TL, S5_CB, 1, 2 * S5_CH), f32)], [sv])
    return y, carries


def s5_bwd(dy, u, bb, cc, pw, carries, seq):
    t = u.shape[0]
    ntl = seq // S5_TL
    us, bbs, ccs, pws, sv = _s5_specs(ntl, True)

    def f(pids, states, uu, b3, c3, p3):
        return f_s5(pids, states, uu, b3[0], c3[0], p3[0])

    first = lambda pids: jnp.logical_and(pids[1] == 0, pids[2] == 0)
    return scan_bwd("s5_bwd", f, (S5_CB, t // seq, ntl), 0, [u, bb, cc, pw], [us, bbs, ccs, pws], [carries], [sv],
                    [dy], [us], [(1, 2 * S5_CH)], [0, 1, 2, 3], {1: first, 2: first, 3: first})


def _adam_math(g, w, m, v):
    m2 = ADAM_B1 * m + (1.0 - ADAM_B1) * g
    v2 = ADAM_B2 * v + (1.0 - ADAM_B2) * jnp.square(g)
    m_hat = m2 / (1.0 - ADAM_B1 ** ADAM_STEP)
    v_hat = v2 / (1.0 - ADAM_B2 ** ADAM_STEP)
    delta = -ADAM_LR * (m_hat / (jnp.sqrt(v_hat) + ADAM_EPS) + ADAM_WD * w)
    return delta, m2, v2


def adamw(name, parts, w, m, v, tr=256):
    n, r, c = parts.shape
    tr = min(tr, r)
    assert r % tr == 0

    def body(p_ref, w_ref, m_ref, v_ref, g_ref, d_ref, m2_ref, v2_ref):
        g = p_ref[0].astype(f32)
        for s in range(1, n):
            g = g + p_ref[s].astype(f32)
        d, m2, v2 = _adam_math(g, w_ref[...], m_ref[...], v_ref[...])
        g_ref[...] = g
        d_ref[...] = d
        m2_ref[...] = m2
        v2_ref[...] = v2

    ps = pl.BlockSpec((n, tr, c), lambda i: (0, i, 0))
    rs = pl.BlockSpec((tr, c), lambda i: (i, 0))
    return pl.pallas_call(body, name=name, grid=(r // tr,), in_specs=[ps, rs, rs, rs], out_specs=[rs] * 4,
                          out_shape=[jax.ShapeDtypeStruct((r, c), f32)] * 4, compiler_params=_cp(1))(parts, w, m, v)


def adamw_layer(name, parts, w, m, v, layer, prev=None, tr=256):
    n, r, c = parts.shape
    nl = w.shape[0]
    tr = min(tr, r)
    assert r % tr == 0 and w.shape[1:] == (r, c)
    n_prev = 0 if prev is None else 4

    def body(*refs):
        p_ref, w_ref, m_ref, v_ref = refs[:4]
        g_ref, d_ref, m2_ref, v2_ref = refs[4 + n_prev:]
        g = p_ref[0].astype(f32)
        for s in range(1, n):
            g = g + p_ref[s].astype(f32)
        d, m2, v2 = _adam_math(g, w_ref[0], m_ref[0], v_ref[0])
        g_ref[0] = g
        d_ref[0] = d
        m2_ref[0] = m2
        v2_ref[0] = v2

    ps = pl.BlockSpec((n, tr, c), lambda i: (0, i, 0))
    rs = pl.BlockSpec((1, tr, c), lambda i: (layer, i, 0))
    anyspec = pl.BlockSpec(memory_space=pl.ANY)
    return pl.pallas_call(body, name=name, grid=(r // tr,), in_specs=[ps, rs, rs, rs] + [anyspec] * n_prev, out_specs=[rs] * 4,
                          out_shape=[jax.ShapeDtypeStruct((nl, r, c), f32)] * 4,
                          input_output_aliases={4 + i: i for i in range(n_prev)},
                          compiler_params=_cp(1))(parts, w, m, v, *(prev or ()))


def sum_parts(name, parts, tr=256):
    n, r, c = parts.shape
    tr = min(tr, r)
    assert r % tr == 0

    def body(p_ref, o_ref):
        g = p_ref[0].astype(f32)
        for s in range(1, n):
            g = g + p_ref[s].astype(f32)
        o_ref[...] = g

    return pl.pallas_call(body, name=name, grid=(r // tr,), in_specs=[pl.BlockSpec((n, tr, c), lambda i: (0, i, 0))],
                          out_specs=pl.BlockSpec((tr, c), lambda i: (i, 0)),
                          out_shape=jax.ShapeDtypeStruct((r, c), f32), compiler_params=_cp(1))(parts)


class Rider:
    def __init__(self, ops):
        self.ops = list(ops or [])
        self.n = len(self.ops)

    def arrays(self):
        return [a for a, _ in self.ops]

    def specs(self):
        return [pl.BlockSpec(memory_space=pl.ANY)] * self.n

    def out_shapes(self):
        return [jax.ShapeDtypeStruct((N_DEV,) + tuple(a.shape) if mode == "gather" else tuple(a.shape), a.dtype)
                for a, mode in self.ops]

    def scratch(self):
        if not self.n:
            return []
        return [pltpu.SemaphoreType.DMA((self.n, N_DEV - 1)), pltpu.SemaphoreType.DMA((self.n, N_DEV - 1)),
                pltpu.SemaphoreType.DMA((self.n,))]

    def _copies(self, ins, outs, sems, with_relays=True):
        send_sems, recv_sems, loc_sems = sems
        x, y, c = lax.axis_index("x"), lax.axis_index("y"), lax.axis_index("c")
        me = 4 * x + 2 * y + c
        first, crossing, relays = [], [], []

        def remote(src, dst, k, idx, peer):
            return pltpu.make_async_remote_copy(src_ref=src, dst_ref=dst, send_sem=send_sems.at[k, idx], recv_sem=recv_sems.at[k, idx],
                                                device_id=peer, device_id_type=pl.DeviceIdType.MESH)

        for k, (_, mode) in enumerate(self.ops):
            src_me = ins[k] if mode == "gather" else ins[k].at[me]
            first.append(pltpu.make_async_copy(src_me, outs[k].at[me], loc_sems.at[k]))
            if mode == "scatter":
                for d in range(1, N_DEV):
                    px = 1 - x if (d >> 2) & 1 else x
                    py = 1 - y if (d >> 1) & 1 else y
                    pc = 1 - c if d & 1 else c
                    first.append(remote(ins[k].at[4 * px + 2 * py + pc], outs[k].at[me], k, d - 1, (px, py, pc)))
            else:
                first.append(remote(ins[k], outs[k].at[me], k, 0, (x, y, 1 - c)))
                for q in range(1, 4):
                    px = 1 - x if (q >> 1) & 1 else x
                    py = 1 - y if q & 1 else y
                    crossing.append(remote(ins[k], outs[k].at[me], k, q, (px, py, c)))
                    if with_relays:
                        block = outs[k].at[4 * px + 2 * py + c]
                        relays.append(remote(block, block, k, 3 + q, (x, y, 1 - c)))
        return first, crossing, relays

    def _start(self, ins, outs, sems):
        first, crossing, _ = self._copies(ins, outs, sems, with_relays=False)
        for cp in first + crossing:
            cp.start()

    def _finish(self, ins, outs, sems):
        first, crossing, relays = self._copies(ins, outs, sems)
        for cp, relay in zip(crossing, relays):
            cp.wait_recv()
            relay.start()
        for cp in first + relays:
            cp.wait()
        for cp in crossing:
            cp.wait_send()

    def start(self, grid, ins, outs, sems):
        if self.n:
            @pl.when(functools.reduce(jnp.logical_and, [pl.program_id(i) == 0 for i in range(len(grid))]))
            def _():
                self._start(ins, outs, sems)

    def wait(self, grid, ins, outs, sems):
        if self.n:
            @pl.when(functools.reduce(jnp.logical_and, [pl.program_id(i) == g - 1 for i, g in enumerate(grid)]))
            def _():
                self._finish(ins, outs, sems)


def exchange(name, ops):
    rider = Rider(ops)
    n = rider.n

    def body(*refs):
        rider._start(refs[:n], refs[n:2 * n], refs[2 * n:])
        rider._finish(refs[:n], refs[n:2 * n], refs[2 * n:])

    return pl.pallas_call(body, name=name, in_specs=rider.specs(), out_specs=rider.specs(), out_shape=rider.out_shapes(),
                          scratch_shapes=rider.scratch())(*rider.arrays())


def _lanes(v, width=LANES):
    v = v.reshape(1, -1)
    return jnp.pad(v, ((0, 0), (0, width - v.shape[1])))


def win_to_padded(w):
    return jnp.concatenate([w[:, :1024], w[:, 2576:3600], w[:, 3600:4624], w[:, 1024:2560], w[:, 2560:2576],
                            jnp.zeros((w.shape[0], PROJ_W - IN_COLS), w.dtype)], axis=1)


def win_from_padded(wp):
    return jnp.concatenate([wp[:, 0:1024], wp[:, 3072:4608], wp[:, 4608:4624], wp[:, 1024:2048], wp[:, 2048:3072]], axis=1)


def headwise_dense(w):
    nb, o, i = w.shape
    rows = jnp.tile(w.transpose(0, 2, 1).reshape(nb * i, o), (1, nb))
    same = (jnp.arange(nb * i)[:, None] // i) == (jnp.arange(nb * o)[None, :] // o)
    return jnp.where(same, rows, 0.0)


def diag_blocks(name, dd, blk, tm=256):
    n = dd.shape[0]

    def body(d_ref, o_ref):
        rows = _iota((tm, n), 0) + pl.program_id(0) * tm
        masked = jnp.where(rows // blk == _iota((tm, n), 1) // blk, d_ref[...], 0.0)
        sel = (_iota((n, LANES), 0) % blk == _iota((n, LANES), 1)).astype(f32)
        o_ref[...] = hdot(masked, sel)

    return pl.pallas_call(body, name=name, grid=(n // tm,), in_specs=[pl.BlockSpec((tm, n), lambda i: (i, 0))],
                          out_specs=pl.BlockSpec((tm, LANES), lambda i: (i, 0)),
                          out_shape=jax.ShapeDtypeStruct((n, LANES), f32), compiler_params=_cp(1))(dd)


def headwise_from_dense(name, dd, o=4, i=4):
    nb = dd.shape[0] // i
    return diag_blocks(name, dd, i)[:, :o].reshape(nb, i, o).transpose(0, 2, 1)


def s5_tables(a_re, a_im, log_step, b_re, b_im, c_re, c_im):
    step = jnp.exp(log_step)[:, None]
    j = jnp.arange(S5_SUB, dtype=f32)[:, None, None]
    expo = jnp.concatenate([j + 1.0, j, -j], axis=0)
    mag = jnp.exp(expo * (a_re * step))
    pw_re, pw_im = mag * jnp.cos(expo * (a_im * step)), mag * jnp.sin(expo * (a_im * step))
    lam_re, lam_im = pw_re[0], pw_im[0]
    den = a_re * a_re + a_im * a_im
    coef_re = ((lam_re - 1.0) * a_re + lam_im * a_im) / den
    coef_im = (lam_im * a_re - (lam_re - 1.0) * a_im) / den
    bb_re = coef_re[..., None] * b_re - coef_im[..., None] * b_im
    bb_im = coef_re[..., None] * b_im + coef_im[..., None] * b_re
    gl = S5_GROUPS // S5_CB
    eye = jnp.eye(gl, dtype=f32)

    def blk_b(t):
        t4 = t.transpose(0, 2, 1).reshape(S5_CB, gl, S5_GROUP, S5_STATE)
        return jnp.einsum("kgcn,gh->kgchn", t4, eye).reshape(S5_CB, gl * S5_GROUP, gl * S5_STATE)

    def blk_c(t):
        t4 = t.reshape(S5_CB, gl, S5_GROUP, S5_STATE)
        return jnp.einsum("kgcn,gh->kgnhc", t4, eye).reshape(S5_CB, gl * S5_STATE, gl * S5_GROUP)

    def blk_p(t):
        return t.reshape(t.shape[0], S5_CB, gl * S5_STATE).transpose(1, 0, 2)

    bb = jnp.concatenate([blk_b(bb_re), blk_b(bb_im)], axis=2)
    cc = jnp.concatenate([blk_c(c_re), -blk_c(c_im)], axis=1)
    pw = jnp.concatenate([blk_p(pw_re), blk_p(pw_im)], axis=2)
    return bb, cc, pw


def ffn_step_bwd(dy, x, nw, wts, saved, ride_act=None, ride_w=None):
    h, g, u = saved
    dx, dnw, dg, du, a, dyh, *got_act = ffn_bwd_act(dy, x, nw, g, u, *wts, ride=ride_act)
    dwg, dwu, dwd, *got_w = ffn_bwd_w(h, dyh, dg, du, a, ride=ride_w)
    return dx, dnw, (dwg, dwu, dwd), got_act, got_w


def hybrid_fwd(x1, p, seq, ride_in, ride_ssd, ride_ml):
    u = rms_fwd(x1, p["mix_norm"])
    proj, *got_in = matmul("hy_in", u, p["win"], ride=ride_in)
    xbc = conv_fwd("ssd_conv", proj, OFF_XBC, p["ssd_conv_w"], p["ssd_conv_b"], seq)
    yraw, hsave, got_ssd = ssd_fwd(xbc, proj, p["a_log"], p["dt_bias"], p["ssd_d"], seq, ride=ride_ssd)
    yssd = rowwise_fwd("ssd_epi", f_ssd_epi, [yraw, proj], [(D_MODEL, 0), (D_MODEL, OFF_Z // D_MODEL)], [p["ssd_norm_w"]], D_MODEL)
    xc = conv_fwd("ml_conv", proj, OFF_MX, p["ml_conv_w"], p["ml_conv_b"], seq)
    q = matmul("hw_q", xc, p["wq"])
    k = matmul("hw_k", xc, p["wk"])
    v = matmul("hw_v", proj, p["wv"], a_off=OFF_MX, a_width=D_MODEL)
    g1 = matmul("gate_q", q, p["wif_q"])
    g2 = matmul("gate_k", k, p["wif_k"])
    g3 = matmul("gate_v", v, p["wif_v"])
    hm, mlsave, got_ml = ml_fwd(q, k, v, g1, g2, g3, p["b_if"], seq, ride=ride_ml)
    yml = rowwise_fwd("ml_epi", f_ml_epi, [hm, xc, proj], [(D_MODEL, 0), (D_MODEL, 0), (D_MODEL, OFF_MZ // D_MODEL)],
                      [p["ml_norm_w"], p["ml_skip"]], D_MODEL)
    t = matmul("hy_out1", yssd, p["wo1"], add=x1)
    x2 = matmul("hy_out2", yml, p["wo2"], add=t)
    return x2, (u, proj, xbc, yraw, hsave, yssd, xc, q, k, v, g1, g2, g3, hm, mlsave, yml), got_in, got_ssd, got_ml


def hybrid_bwd(dx2, x1, p, saved, seq, ride_ml):
    u, proj, xbc, yraw, hsave, yssd, xc, q, k, v, g1, g2, g3, hm, mlsave, yml = saved
    gr = {}
    dyssd = matmul("d_yssd", dx2, p["wo1"], cb=1)
    dyml = matmul("d_yml", dx2, p["wo2"], cb=1)
    gr["wo"] = jnp.concatenate([matmul("dw_o1", yssd, dx2, ca=0), matmul("dw_o2", yml, dx2, ca=0)], axis=0)
    d_hm, d_xc, d_mz, gr["ml_norm_w"], gr["ml_skip"] = rowwise_bwd(
        "ml_epi_bwd", f_ml_epi, [hm, xc, proj], [(D_MODEL, 0), (D_MODEL, 0), (D_MODEL, OFF_MZ // D_MODEL)],
        [p["ml_norm_w"], p["ml_skip"]], dyml)
    dq, dk, dv, dgt, gr["b_if"], *got_ml = ml_bwd(d_hm, q, k, v, g1, g2, g3, p["b_if"], mlsave, seq, ride=ride_ml)
    dq = matmul("dq_gate", dgt, p["wif_q"], cb=1, add=dq)
    dk = matmul("dk_gate", dgt, p["wif_k"], cb=1, add=dk)
    dv = matmul("dv_gate", dgt, p["wif_v"], cb=1, add=dv)
    gr["wif"] = jnp.concatenate([matmul("dw_if_q", q, dgt, ca=0), matmul("dw_if_k", k, dgt, ca=0),
                                 matmul("dw_if_v", v, dgt, ca=0)], axis=0)
    d_xc = matmul("dxc_q", dq, p["wq"], cb=1, add=d_xc)
    d_xc = matmul("dxc_k", dk, p["wk"], cb=1, add=d_xc)
    gr["wq"] = matmul("dw_q", xc, dq, ca=0)
    gr["wk"] = matmul("dw_k", xc, dk, ca=0)
    gr["wv"] = matmul("dw_v", proj, dv, ca=0, a_off=OFF_MX, a_width=D_MODEL)
    d_mx, gr["ml_conv_w"], gr["ml_conv_b"] = conv_bwd("ml_conv_bwd", d_xc, proj, OFF_MX, p["ml_conv_w"], p["ml_conv_b"], seq)
    d_mx = matmul("dmx_v", dv, p["wv"], cb=1, add=d_mx)
    d_yraw, d_z, gr["ssd_norm_w"] = rowwise_bwd("ssd_epi_bwd", f_ssd_epi, [yraw, proj],
                                                [(D_MODEL, 0), (D_MODEL, OFF_Z // D_MODEL)], [p["ssd_norm_w"]], dyssd)
    d_xs, d_b, d_c, d_dt, gr["a_log"], gr["dt_bias"], gr["ssd_d"] = ssd_bwd(
        d_yraw, xbc, proj, p["a_log"], p["dt_bias"], p["ssd_d"], hsave, seq)
    d_xbc, gr["ssd_conv_w"], gr["ssd_conv_b"] = conv_bwd("ssd_conv_bwd", jnp.concatenate([d_xs, d_b, d_c], axis=1), proj, OFF_XBC,
                                                         p["ssd_conv_w"], p["ssd_conv_b"], seq)
    dproj = jnp.concatenate([d_z, d_mx, d_mz, d_xbc, d_dt, jnp.zeros((d_dt.shape[0], PROJ_W - OFF_DT - LANES), f32)], axis=1)
    gr["win"] = matmul("dw_in", u.astype(bf16).T, dproj, tiles=(D_MODEL, PROJ_W // 2, min(512, u.shape[0])))
    du = matmul("d_u", dproj, p["win"], cb=1)
    dx1, gr["mix_norm"] = rms_bwd([du], x1, p["mix_norm"], dx2)
    return dx1, gr, got_ml


def s5_layer_fwd(x4, p, seq):
    u = rms_fwd(x4, p["mix_norm"])
    ys, carries = s5_fwd(u, p["bb"], p["cc"], p["pw"], seq)
    gg = rowwise_fwd("s5_post", f_s5_post, [ys, u], [(D_MODEL, 0), (D_MODEL, 0)], [p["s5_d"]], D_MODEL)
    pab = matmul("s5_ab", gg, p["wab"])
    x5 = rowwise_fwd("s5_glu", f_glu_res, [pab, x4], [(2 * D_MODEL, 0), (D_MODEL, 0)], [p["b_a"], p["b_b"]], D_MODEL)
    return x5, (u, ys, carries, gg, pab)


def s5_layer_bwd(dx5, x4, p, saved, seq):
    u, ys, carries, gg, pab = saved
    gr = {}
    dpab, gr["b_a"], gr["b_b"] = rowwise_bwd("s5_glu_bwd", f_glu, [pab], [(2 * D_MODEL, 0)], [p["b_a"], p["b_b"]], dx5)
    dgg = matmul("d_gg", dpab, p["wab"], cb=1)
    gr["wab"] = matmul("dw_ab", gg, dpab, ca=0)
    dys, du_a, gr["s5_d"] = rowwise_bwd("s5_post_bwd", f_s5_post, [ys, u], [(D_MODEL, 0), (D_MODEL, 0)], [p["s5_d"]], dgg)
    du_b, gr["bb"], gr["cc"], gr["pw"] = s5_bwd(dys, u, p["bb"], p["cc"], p["pw"], carries, seq)
    dx4, gr["mix_norm"] = rms_bwd([du_a, du_b], x4, p["mix_norm"], dx5)
    return dx4, gr


BIG = ["ffn1_w_gate", "ffn1_w_up", "ffn1_w_down", "ffn2_w_gate", "ffn2_w_up", "ffn2_w_down", "hy_w_in", "hy_w_out", "s5_w_a", "s5_w_b"]
SMALL_SHARDED = {"ssd_conv_w": 2, "ml_conv_w": 2, "ml_w_q": 1, "ml_w_k": 1, "ml_w_v": 1, "ml_w_if": 1, "s5_d": 1, "s5_b_a": 1, "s5_b_b": 1}
WEIGHTS = ["ffn1_norm", "ffn1_w_gate", "ffn1_w_up", "ffn1_w_down", "mix_norm", "ffn2_norm", "ffn2_w_gate", "ffn2_w_up", "ffn2_w_down",
           "hy_w_in", "ssd_conv_w", "ssd_conv_b", "ssd_dt_bias", "ssd_a_log", "ssd_d", "ssd_norm_w", "ml_conv_w", "ml_conv_b",
           "ml_w_q", "ml_w_k", "ml_w_v", "ml_w_if", "ml_b_if", "ml_norm_w", "ml_skip", "hy_w_out", "s5_a_re", "s5_a_im",
           "s5_log_step", "s5_b_re", "s5_b_im", "s5_c_re", "s5_c_im", "s5_d", "s5_w_a", "s5_b_a", "s5_w_b", "s5_b_b", "final_norm"]
S5_PARAMS = ["s5_a_re", "s5_a_im", "s5_log_step", "s5_b_re", "s5_b_im", "s5_c_re", "s5_c_im"]
SMALL_S5 = S5_PARAMS + ["s5_d", "s5_b_a", "s5_b_b"]
SMALL_REST = [n for n in WEIGHTS if n not in BIG and n not in SMALL_S5]
SMALL = SMALL_REST + SMALL_S5


def _unshard(g, axis):
    return jnp.concatenate([g[i] for i in range(N_DEV)], axis=axis)


def assemble_hybrid(gw, rep):
    padn = lambda w: jnp.pad(w, ((0, 0), (0, LANES - w.shape[1]))).astype(bf16)
    wif = _unshard(gw["ml_w_if"], 1)[0]
    wo = _unshard(gw["hy_w_out"], 1)[0].astype(bf16)
    dense = lambda n: headwise_dense(_unshard(gw[n], 1)[0].astype(f32)).astype(bf16)
    w0 = dict(mix_norm=rep["mix_norm"][0:1],
              win=win_to_padded(_unshard(gw["hy_w_in"], 2)[0]).astype(bf16),
              ssd_conv_w=_unshard(gw["ssd_conv_w"], 2)[0], ssd_conv_b=rep["ssd_conv_b"],
              a_log=_lanes(rep["ssd_a_log"]), dt_bias=_lanes(rep["ssd_dt_bias"]), ssd_d=_lanes(rep["ssd_d"]),
              ssd_norm_w=rep["ssd_norm_w"], ml_conv_w=_unshard(gw["ml_conv_w"], 2)[0], ml_conv_b=rep["ml_conv_b"],
              wq=dense("ml_w_q"), wk=dense("ml_w_k"), wv=dense("ml_w_v"),
              wif_q=padn(wif[0:1024]), wif_k=padn(wif[1024:2048]), wif_v=padn(wif[2048:3072]),
              b_if=_lanes(rep["ml_b_if"]), ml_norm_w=rep["ml_norm_w"], ml_skip=rep["ml_skip"],
              wo1=wo[:D_MODEL], wo2=wo[D_MODEL:])
    return w0


def assemble_s5(gw, rep):
    bb, cc, pw = s5_tables(*[rep[n][0] for n in S5_PARAMS])
    wab = jnp.concatenate([_unshard(gw["s5_w_a"], 1)[0], _unshard(gw["s5_w_b"], 1)[0]], axis=1).astype(bf16)
    return dict(mix_norm=rep["mix_norm"][1:2], bb=bb, cc=cc, pw=pw,
                s5_d=_unshard(gw["s5_d"], 1), wab=wab, b_a=_unshard(gw["s5_b_a"], 1), b_b=_unshard(gw["s5_b_b"], 1))


def _shards(full, axis):
    return jnp.stack(jnp.split(full, N_DEV, axis=axis), axis=0)


def small_grads(g_norms, g_hy, g_s5, d_final, rep):
    small = dict(g_norms)
    small["mix_norm"] = jnp.concatenate([g_hy["mix_norm"], g_s5["mix_norm"]], axis=0)
    small["ssd_conv_w"] = g_hy["ssd_conv_w"][None]
    small["ssd_conv_b"] = g_hy["ssd_conv_b"]
    small["ssd_dt_bias"] = g_hy["dt_bias"][:, :SSD_HEADS]
    small["ssd_a_log"] = g_hy["a_log"][:, :SSD_HEADS]
    small["ssd_d"] = g_hy["ssd_d"][:, :SSD_HEADS]
    small["ssd_norm_w"] = g_hy["ssd_norm_w"]
    small["ml_conv_w"] = g_hy["ml_conv_w"][None]
    small["ml_conv_b"] = g_hy["ml_conv_b"]
    for nm, key in (("ml_w_q", "wq"), ("ml_w_k", "wk"), ("ml_w_v", "wv")):
        small[nm] = headwise_from_dense("diag_" + key, g_hy[key])[None]
    small["ml_w_if"] = g_hy["wif"][None, :, :2 * ML_HEADS]
    small["ml_b_if"] = g_hy["b_if"][:, :2 * ML_HEADS]
    small["ml_norm_w"] = g_hy["ml_norm_w"]
    small["ml_skip"] = g_hy["ml_skip"]
    small["final_norm"] = d_final.reshape(-1)
    return small


def s5_small_grads(g_s5, rep):
    small = {}
    _, tvjp = jax.vjp(s5_tables, *[rep[n][0] for n in S5_PARAMS])
    for n, g in zip(S5_PARAMS, tvjp((g_s5["bb"], g_s5["cc"], g_s5["pw"]))):
        small[n] = g[None]
    small["s5_d"] = g_s5["s5_d"]
    small["s5_b_a"] = g_s5["b_a"]
    small["s5_b_b"] = g_s5["b_b"]
    return small


ROW = 1024
F32_ROWS = 8


def _piece_rows(size):
    return -(-size // (ROW * F32_ROWS)) * F32_ROWS


def _pack(arrays):
    pieces = []
    for a in arrays:
        flat = a.astype(f32).reshape(-1)
        pieces.append(jnp.pad(flat, (0, _piece_rows(a.size) * ROW - a.size)).reshape(-1, ROW))
    return jnp.concatenate(pieces, axis=0)


def _unpack(buf, shapes):
    out, r0 = [], 0
    lead = buf.shape[:-2]
    for shp in shapes:
        size = math.prod(shp)
        r = _piece_rows(size)
        out.append(buf[..., r0:r0 + r, :].reshape(lead + (-1,))[..., :size].reshape(lead + tuple(shp)))
        r0 += r
    return out


ADAM_BLOCK_ELEMS = 500_000


def _tile_rows(r, c):
    cap = ADAM_BLOCK_ELEMS // (-(-c // LANES) * LANES)
    if r <= cap:
        return r
    return max(t for t in range(F32_ROWS, cap + 1, F32_ROWS) if r % t == 0)


def _flat2d(a):
    return a.reshape(-1, a.shape[-1])


def kernel(x, ffn1_norm, ffn1_w_gate, ffn1_w_up, ffn1_w_down, mix_norm, ffn2_norm, ffn2_w_gate, ffn2_w_up, ffn2_w_down, hy_w_in, ssd_conv_w, ssd_conv_b, ssd_dt_bias, ssd_a_log, ssd_d, ssd_norm_w, ml_conv_w, ml_conv_b, ml_w_q, ml_w_k, ml_w_v, ml_w_if, ml_b_if, ml_norm_w, ml_skip, hy_w_out, s5_a_re, s5_a_im, s5_log_step, s5_b_re, s5_b_im, s5_c_re, s5_c_im, s5_d, s5_w_a, s5_b_a, s5_w_b, s5_b_b, final_norm, loss_target, m_ffn1_norm, m_ffn1_w_gate, m_ffn1_w_up, m_ffn1_w_down, m_mix_norm, m_ffn2_norm, m_ffn2_w_gate, m_ffn2_w_up, m_ffn2_w_down, m_hy_w_in, m_ssd_conv_w, m_ssd_conv_b, m_ssd_dt_bias, m_ssd_a_log, m_ssd_d, m_ssd_norm_w, m_ml_conv_w, m_ml_conv_b, m_ml_w_q, m_ml_w_k, m_ml_w_v, m_ml_w_if, m_ml_b_if, m_ml_norm_w, m_ml_skip, m_hy_w_out, m_s5_a_re, m_s5_a_im, m_s5_log_step, m_s5_b_re, m_s5_b_im, m_s5_c_re, m_s5_c_im, m_s5_d, m_s5_w_a, m_s5_b_a, m_s5_w_b, m_s5_b_b, m_final_norm, v_ffn1_norm, v_ffn1_w_gate, v_ffn1_w_up, v_ffn1_w_down, v_mix_norm, v_ffn2_norm, v_ffn2_w_gate, v_ffn2_w_up, v_ffn2_w_down, v_hy_w_in, v_ssd_conv_w, v_ssd_conv_b, v_ssd_dt_bias, v_ssd_a_log, v_ssd_d, v_ssd_norm_w, v_ml_conv_w, v_ml_conv_b, v_ml_w_q, v_ml_w_k, v_ml_w_v, v_ml_w_if, v_ml_b_if, v_ml_norm_w, v_ml_skip, v_hy_w_out, v_s5_a_re, v_s5_a_im, v_s5_log_step, v_s5_b_re, v_s5_b_im, v_s5_c_re, v_s5_c_im, v_s5_d, v_s5_w_a, v_s5_b_a, v_s5_w_b, v_s5_b_b, v_final_norm):
    given = dict(locals())
    w = {n: given[n] for n in WEIGHTS}
    mom = {n: given["m_" + n] for n in WEIGHTS}
    var = {n: given["v_" + n] for n in WEIGHTS}
    bl, seq, d = x.shape
    me = 4 * lax.axis_index("x") + 2 * lax.axis_index("y") + lax.axis_index("c")

    x0, tgt = x.reshape(bl * seq, d), loss_target.reshape(bl * seq, d)
    rep = {n: w[n] for n in WEIGHTS if n not in BIG and n not in SMALL_SHARDED}
    ffn_w = ("_w_gate", "_w_up", "_w_down")

    def ffn_gather(pre, l):
        return [(w[pre + s][l:l + 1].astype(bf16), "gather") for s in ffn_w]

    def scatter(parts):
        return [(p, "scatter") for p in parts]

    wf10 = tuple(exchange("gather_ffn1_l0", ffn_gather("ffn1", 0)))
    mixer_ops = [(w[n].astype(bf16), "gather") for n in ("hy_w_in", "hy_w_out")]
    mixer_ops.append((_pack([w[n] for n in SMALL_SHARDED]), "gather"))
    x1, *rest = ffn_fwd(x0, ffn1_norm[0:1], *wf10, ride=mixer_ops)
    sv10, got = rest[:3], rest[3:]
    gw = dict(zip(("hy_w_in", "hy_w_out"), got[:2]))
    gw.update(zip(SMALL_SHARDED, _unpack(got[2], [w[n].shape for n in SMALL_SHARDED])))
    w0 = assemble_hybrid(gw, rep)
    x2, sv_h, wf20, got, wf11 = hybrid_fwd(x1, w0, seq, ride_in=ffn_gather("ffn2", 0),
                                           ride_ssd=[(w[n].astype(bf16), "gather") for n in ("s5_w_a", "s5_w_b")],
                                           ride_ml=ffn_gather("ffn1", 1))
    gw.update(zip(("s5_w_a", "s5_w_b"), got))
    w1 = assemble_s5(gw, rep)
    x3, *rest = ffn_fwd(x2, ffn2_norm[0:1], *wf20, ride=ffn_gather("ffn2", 1))
    sv20, wf21 = rest[:3], tuple(rest[3:])
    x4, *sv11 = ffn_fwd(x3, ffn1_norm[1:2], *wf11)
    x5, sv_s = s5_layer_fwd(x4, w1, seq)
    x6, *sv21 = ffn_fwd(x5, ffn2_norm[1:2], *wf21)
    loss, dx6, d_final = loss_head(x6, final_norm.reshape(1, d), tgt)

    dx5, dn21, dw21, _, _ = ffn_step_bwd(dx6, x5, ffn2_norm[1:2], wf21, sv21)
    dx4, g_s5 = s5_layer_bwd(dx5, x4, w1, sv_s, seq)
    dwab = g_s5.pop("wab")
    s5_ops = scatter([_shards(dwab[None, :, :D_MODEL], 1).astype(bf16), _shards(dwab[None, :, D_MODEL:], 1).astype(bf16)])
    dx3, dn11, dw11, p21, p_s5 = ffn_step_bwd(dx4, x3, ffn1_norm[1:2], wf11, sv11, ride_act=scatter(dw21), ride_w=s5_ops)
    small = s5_small_grads(g_s5, rep)
    dx2, dn20, dw20, p11, (parts_s5,) = ffn_step_bwd(dx3, x2, ffn2_norm[0:1], wf20, sv20, ride_act=scatter(dw11),
                                                      ride_w=[(_pack([small[n] for n in SMALL_S5]), "gather")])
    dx1, g_hy, p20 = hybrid_bwd(dx2, x1, w0, sv_h, seq, ride_ml=scatter(dw20))
    hy_ops = scatter([_shards(win_from_padded(g_hy.pop("win"))[None], 2).astype(bf16), _shards(g_hy.pop("wo")[None], 1).astype(bf16)])
    h10, g10, u10 = sv10
    dx0, dn10, dg, du, a, dyh, *p_hy = ffn_bwd_act(dx1, x0, ffn1_norm[0:1], g10, u10, *wf10, ride=hy_ops)
    g_norms = {"ffn1_norm": jnp.concatenate([dn10, dn11], axis=0), "ffn2_norm": jnp.concatenate([dn20, dn21], axis=0)}
    small.update(small_grads(g_norms, g_hy, g_s5, d_final, rep))
    *dw10, parts_rest = ffn_bwd_w(h10, dyh, dg, du, a, ride=[(_pack([small[n] for n in SMALL_REST]), "gather")])
    p10 = exchange("reduce_tail", scatter(dw10))
    small_parts = jnp.concatenate([parts_rest, parts_s5], axis=1)
    small_sum = sum_parts("sum_small", small_parts, tr=_tile_rows(small_parts.shape[1], ROW))

    out_g, out_d, out_m, out_v = {}, {}, {}, {}
    ffn_parts = {"ffn1": (p10, p11), "ffn2": (p20, p21)}
    for pre in ("ffn1", "ffn2"):
        for k, s in enumerate(ffn_w):
            n = pre + s
            r, c = w[n].shape[1:]
            res = None
            for l in (1, 0):
                res = adamw_layer("adamw_" + n, ffn_parts[pre][l][k].reshape(N_DEV, r, c), w[n], mom[n], var[n], l, res,
                                  tr=_tile_rows(r, c))
            out_g[n], out_d[n], out_m[n], out_v[n] = res
    for n, parts in zip(("hy_w_in", "hy_w_out", "s5_w_a", "s5_w_b"), tuple(p_hy) + tuple(p_s5)):
        shp = w[n].shape
        w2 = _flat2d(w[n])
        res = adamw("adamw_" + n, parts.reshape((N_DEV,) + w2.shape), w2, _flat2d(mom[n]), _flat2d(var[n]),
                    tr=_tile_rows(*w2.shape))
        out_g[n], out_d[n], out_m[n], out_v[n] = [a.reshape(shp) for a in res]
    g_small = {}
    for n, full in zip(SMALL, _unpack(small_sum, [small[n].shape for n in SMALL])):
        if n in SMALL_SHARDED:
            ax = SMALL_SHARDED[n]
            full = lax.dynamic_slice_in_dim(full, me * w[n].shape[ax], w[n].shape[ax], axis=ax)
        g_small[n] = full
    packs = [_pack([t[n] for n in SMALL]) for t in (g_small, w, mom, var)]
    res = adamw("adamw_small", packs[0][None], packs[1], packs[2], packs[3], tr=_tile_rows(*packs[0].shape))
    for dst, a in zip((out_g, out_d, out_m, out_v), res):
        dst.update(zip(SMALL, _unpack(a, [w[n].shape for n in SMALL])))

    total = lax.psum(loss[0, 0], ("x", "y", "c"))
    return (total, dx0.reshape(bl, seq, d), *[out_g[n] for n in WEIGHTS], *[out_d[n] for n in WEIGHTS],
            *[out_m[n] for n in WEIGHTS], *[out_v[n] for n in WEIGHTS])
```

```python
import functools
import math

import jax
import jax.numpy as jnp
from jax import lax
from jax.experimental import pallas as pl
from jax.experimental.pallas import tpu as pltpu

f32 = jnp.float32
bf16 = jnp.bfloat16

N_DEV = 8
D_MODEL = 1024
EPS = 1e-6
FFN_RES = 0.5
CONV_W = 4
SSD_HEADS = 16
SSD_HEAD_DIM = 64
SSD_GROUPS = 2
SSD_STATE = 128
SSD_HG = SSD_HEADS // SSD_GROUPS
SSD_GW = SSD_HG * SSD_HEAD_DIM
CHUNK = 128
ML_HEADS = 4
ML_HD = 256
S5_GROUP = 16
S5_GROUPS = 64
S5_STATE = 64
S5_CB = 8
S5_CH = (S5_GROUPS // S5_CB) * S5_STATE
S5_TL = 256
S5_SUB = 32
LANES = 128
IN_COLS = 4624
PROJ_W = 4864
OFF_Z, OFF_MX, OFF_MZ, OFF_XBC, OFF_DT = 0, 1024, 2048, 3072, 4608
ADAM_LR, ADAM_B1, ADAM_B2, ADAM_EPS, ADAM_WD, ADAM_STEP = 0.001, 0.9, 0.999, 1e-08, 0.01, 10
NEG = -1e30
VMEM_LIMIT = 56 * 1024 * 1024


def _cp(n):
    return pltpu.CompilerParams(dimension_semantics=("arbitrary",) * n, vmem_limit_bytes=VMEM_LIMIT)


def _dg(a, b, ca, cb):
    return lax.dot_general(a.astype(bf16), b.astype(bf16), (((ca,), (cb,)), ((), ())), preferred_element_type=f32)


@functools.partial(jax.custom_vjp, nondiff_argnums=(2, 3))
def bdot(a, b, ca, cb):
    return _dg(a, b, ca, cb)


def _bdot_fwd(a, b, ca, cb):
    return _dg(a, b, ca, cb), (a, b)


def _bdot_bwd(ca, cb, res, ct):
    a, b = res
    da = _dg(ct, b, 1, 1 - cb) if ca == 1 else _dg(b, ct, 1 - cb, 1)
    db = _dg(a, ct, 1 - ca, 0) if cb == 0 else _dg(ct, a, 0, 1 - ca)
    return da, db


bdot.defvjp(_bdot_fwd, _bdot_bwd)


def _split3(z):
    hi = z.astype(bf16)
    r1 = z - hi.astype(f32)
    mid = r1.astype(bf16)
    return hi, mid, (r1 - mid.astype(f32)).astype(bf16)


def _sel(z, m, z_left, transpose_m):
    mm = m.astype(bf16)
    dn = lambda zz: lax.dot_general(zz, mm, (((1,), (1 if transpose_m else 0,)), ((), ())), preferred_element_type=f32) \
        if z_left else lax.dot_general(mm, zz, (((0 if transpose_m else 1,), (0,)), ((), ())), preferred_element_type=f32)
    hi, mid, lo = _split3(z)
    return dn(hi) + dn(mid) + dn(lo)


@functools.partial(jax.custom_vjp, nondiff_argnums=(2,))
def _seldot(z, m, z_left):
    return _sel(z, m, z_left, False)


_seldot.defvjp(lambda z, m, z_left: (_sel(z, m, z_left, False), m),
               lambda z_left, m, ct: (_sel(ct, m, z_left, True), jnp.zeros_like(m)))


def hdot(a, b, exact="b"):
    return _seldot(a, b.astype(f32), True) if exact == "b" else _seldot(b, a.astype(f32), False)


def _iota(shape, dim):
    return lax.broadcasted_iota(jnp.int32, shape, dim)


def _tri(n):
    return (_iota((n, n), 0) >= _iota((n, n), 1))


@functools.partial(jax.custom_vjp, nondiff_argnums=(1,))
def tshift(x, k):
    return jnp.where(_iota(x.shape, 0) >= k, pltpu.roll(x, k, 0), 0.0)


def _tshift_fwd(x, k):
    return tshift(x, k), None


def _tshift_bwd(k, _, ct):
    n = ct.shape[0]
    return (jnp.where(_iota(ct.shape, 0) < n - k, pltpu.roll(ct, n - k, 0), 0.0),)


tshift.defvjp(_tshift_fwd, _tshift_bwd)


def _lane_pick(a, idx):
    return jnp.sum(jnp.where(_iota(a.shape, 1) == idx, a, 0.0), axis=1, keepdims=True)


def _row_pick(a, idx):
    return jnp.sum(jnp.where(_iota(a.shape, 0) == idx, a, 0.0), axis=0, keepdims=True)


def _silu(x):
    return x * jax.nn.sigmoid(x)


def map_fwd(name, f, grid, ins, in_specs, out_shapes, out_specs):
    n_in = len(ins)

    def body(*refs):
        pids = tuple(pl.program_id(i) for i in range(len(grid)))
        outs = f(pids, *[r[...] for r in refs[:n_in]])
        for r, o in zip(refs[n_in:], outs):
            r[...] = o.astype(r.dtype)

    return pl.pallas_call(body, name=name, grid=grid, in_specs=in_specs, out_specs=out_specs,
                          out_shape=out_shapes, compiler_params=_cp(len(grid)))(*ins)


def scan_fwd(name, f, grid, slot_axis, ins, in_specs, out_shapes, out_specs, state_shapes, state_init, save_shapes, save_specs,
             ride=None):
    n_in, n_out, n_st = len(ins), len(out_shapes), len(state_shapes)
    n_slots = grid[slot_axis]
    cax = len(grid) - 1 if slot_axis != len(grid) - 1 else len(grid) - 2
    rider = Rider(ride)
    nr = rider.n

    def body(*refs):
        pids = tuple(pl.program_id(i) for i in range(len(grid)))
        in_refs, r_ins = refs[:n_in], refs[n_in:n_in + nr]
        o0 = n_in + nr
        out_refs, save_refs = refs[o0:o0 + n_out], refs[o0 + n_out:o0 + n_out + n_st]
        r_outs = refs[o0 + n_out + n_st:o0 + n_out + n_st + nr]
        st_refs = refs[o0 + n_out + n_st + nr:o0 + n_out + 2 * n_st + nr]
        sems = refs[o0 + n_out + 2 * n_st + nr:]
        rider.start(grid, r_ins, r_outs, sems)
        slot = pids[slot_axis]

        @pl.when(pids[cax] == 0)
        def _():
            for s, init in zip(st_refs, state_init):
                s[slot] = jnp.full(s.shape[1:], init, f32)

        states = tuple(s[slot] for s in st_refs)
        for sv, st in zip(save_refs, states):
            sv[...] = st.reshape(sv.shape)
        outs, new = f(pids, states, *[r[...] for r in in_refs])
        for r, o in zip(out_refs, outs):
            r[...] = o.astype(r.dtype)
        for s, v in zip(st_refs, new):
            s[slot] = v
        rider.wait(grid, r_ins, r_outs, sems)

    scratch = [pltpu.VMEM((n_slots,) + tuple(s), f32) for s in state_shapes]
    return pl.pallas_call(body, name=name, grid=grid, in_specs=list(in_specs) + rider.specs(),
                          out_specs=list(out_specs) + list(save_specs) + rider.specs(),
                          out_shape=list(out_shapes) + list(save_shapes) + rider.out_shapes(),
                          scratch_shapes=scratch + rider.scratch(), compiler_params=_cp(len(grid)))(*ins, *rider.arrays())


def scan_bwd(name, f, grid, slot_axis, ins, in_specs, saves, save_specs, cts, ct_specs, state_shapes, wrt, acc_first):
    n_in, n_st, n_ct = len(ins), len(saves), len(cts)
    n_slots = grid[slot_axis]
    cax = len(grid) - 1 if slot_axis != len(grid) - 1 else len(grid) - 2

    def body(*refs):
        pids = tuple(pl.program_id(i) for i in range(len(grid)))
        in_refs = refs[:n_in]
        save_refs = refs[n_in:n_in + n_st]
        ct_refs = refs[n_in + n_st:n_in + n_st + n_ct]
        out_refs = refs[n_in + n_st + n_ct:n_in + n_st + n_ct + len(wrt)]
        dst_refs = refs[n_in + n_st + n_ct + len(wrt):]
        slot = pids[slot_axis]

        @pl.when(pids[cax] == 0)
        def _():
            for s in dst_refs:
                s[slot] = jnp.zeros(s.shape[1:], f32)

        vals = [r[...] for r in in_refs]
        states = tuple(sv[...].reshape(shp) for sv, shp in zip(save_refs, state_shapes))
        ctv = tuple(r[...].astype(f32) for r in ct_refs)
        dnew = tuple(s[slot] for s in dst_refs)

        def g(st, *dv):
            full = list(vals)
            for i, v in zip(wrt, dv):
                full[i] = v
            outs, new = f(pids, st, *full)
            return tuple(outs), tuple(new)

        _, vjp = jax.vjp(g, states, *[vals[i] for i in wrt])
        grads = vjp((ctv, dnew))
        for s, v in zip(dst_refs, grads[0]):
            s[slot] = v
        for i, o_ref, gr in zip(wrt, out_refs, grads[1:]):
            first = acc_first.get(i)
            if first is None:
                o_ref[...] = gr.astype(o_ref.dtype)
            else:
                @pl.when(first(pids))
                def _():
                    o_ref[...] = jnp.zeros_like(o_ref)
                o_ref[...] += gr

    out_shapes = [jax.ShapeDtypeStruct(ins[i].shape, f32) for i in wrt]
    out_specs = [in_specs[i] for i in wrt]
    scratch = [pltpu.VMEM((n_slots,) + tuple(s), f32) for s in state_shapes]
    return pl.pallas_call(body, name=name, grid=grid, in_specs=list(in_specs) + list(save_specs) + list(ct_specs),
                          out_specs=out_specs, out_shape=out_shapes, scratch_shapes=scratch,
                          compiler_params=_cp(len(grid)))(*ins, *saves, *cts)


def _fit(dim, cap):
    if dim <= cap:
        return dim
    return max(t for t in range(LANES, cap + 1, LANES) if dim % t == 0)


def _matmul_tiles(m, n, kdim, ca):
    if ca == 1:
        return _fit(m, 512), _fit(n, 2432), _fit(kdim, 2432)
    return _fit(m, 1024), _fit(n, 1280), _fit(kdim, 512)


def matmul(name, a, b, ca=1, cb=0, add=None, out_dtype=f32, a_off=0, a_width=None, ride=None, tiles=None):
    rider = Rider(ride)
    nr = rider.n
    a_width = a.shape[1] if a_width is None else a_width
    kdim = b.shape[cb]
    n = b.shape[1 - cb]
    m = a.shape[0] if ca == 1 else a_width
    tm, tn, tk = tiles or _matmul_tiles(m, n, kdim, ca)
    assert m % tm == 0 and n % tn == 0 and kdim % tk == 0
    nk = kdim // tk
    if ca == 1:
        assert a_off % tk == 0 and a_width == kdim
        koff = a_off // tk
        a_spec = pl.BlockSpec((tm, tk), lambda i, j, k: (i, k + koff))
    else:
        assert a_off % tm == 0 and a.shape[0] == kdim
        ioff = a_off // tm
        a_spec = pl.BlockSpec((tk, tm), lambda i, j, k: (k, i + ioff))
    b_spec = pl.BlockSpec((tk, tn), lambda i, j, k: (k, j)) if cb == 0 else pl.BlockSpec((tn, tk), lambda i, j, k: (j, k))
    o_spec = pl.BlockSpec((tm, tn), lambda i, j, k: (i, j))
    has_add = add is not None

    n_in = 3 if has_add else 2
    grid = (m // tm, n // tn, nk)

    def body(*refs):
        a_ref, b_ref = refs[0], refs[1]
        add_ref = refs[2] if has_add else None
        r_ins, o_ref = refs[n_in:n_in + nr], refs[n_in + nr]
        r_outs, acc, sems = refs[n_in + nr + 1:n_in + 2 * nr + 1], refs[n_in + 2 * nr + 1], refs[n_in + 2 * nr + 2:]
        rider.start(grid, r_ins, r_outs, sems)
        k = pl.program_id(2)

        @pl.when(k == 0)
        def _():
            acc[...] = add_ref[...].astype(f32) if has_add else jnp.zeros_like(acc)

        acc[...] += _dg(a_ref[...], b_ref[...], ca, cb)

        @pl.when(k == nk - 1)
        def _():
            o_ref[...] = acc[...].astype(o_ref.dtype)

        rider.wait(grid, r_ins, r_outs, sems)

    ins = [a, b] + ([add] if has_add else [])
    specs = [a_spec, b_spec] + ([o_spec] if has_add else [])
    res = pl.pallas_call(body, name=name, grid=grid, in_specs=specs + rider.specs(), out_specs=[o_spec] + rider.specs(),
                         out_shape=[jax.ShapeDtypeStruct((m, n), out_dtype)] + rider.out_shapes(),
                         scratch_shapes=[pltpu.VMEM((tm, tn), f32)] + rider.scratch(), compiler_params=_cp(3))(*ins, *rider.arrays())
    return res if nr else res[0]


def f_rms(pids, x, w):
    r = lax.rsqrt(jnp.mean(x * x, axis=-1, keepdims=True) + EPS)
    return (x * r * w,)


def _row_spec(tm, width, col=0):
    return pl.BlockSpec((tm, width), lambda i: (i, col))


def _par_spec(shape):
    return pl.BlockSpec(shape, lambda *p: (0,) * len(shape))


def rms_fwd(x, w, tm=512):
    t, d = x.shape
    return map_fwd("rms_fwd", f_rms, (t // tm,), [x, w], [_row_spec(tm, d), _par_spec((1, d))],
                   [jax.ShapeDtypeStruct((t, d), f32)], [_row_spec(tm, d)])[0]


def rms_bwd(dys, x, w, dres, tm=512):
    t, d = x.shape
    n = len(dys)

    def body(*refs):
        x_ref, w_ref, dres_ref, dx_ref, dw_ref = refs[n:]
        dy = refs[0][...]
        for r in refs[1:n]:
            dy = dy + r[...]
        _, vjp = jax.vjp(lambda xx, ww: f_rms(None, xx, ww)[0], x_ref[...], w_ref[...])
        dx, dw = vjp(dy)
        dx_ref[...] = dx + dres_ref[...]

        @pl.when(pl.program_id(0) == 0)
        def _():
            dw_ref[...] = jnp.zeros_like(dw_ref)
        dw_ref[...] += dw

    return pl.pallas_call(body, name="rms_bwd", grid=(t // tm,),
                          in_specs=[_row_spec(tm, d)] * (n + 1) + [_par_spec((1, d)), _row_spec(tm, d)],
                          out_specs=[_row_spec(tm, d), _par_spec((1, d))],
                          out_shape=[jax.ShapeDtypeStruct((t, d), f32), jax.ShapeDtypeStruct((1, d), f32)],
                          compiler_params=_cp(1))(*dys, x, w, dres)


def loss_head(x, w, tgt, tm=512):
    t, d = x.shape

    def fl(xx, ww, tt):
        y = f_rms(None, xx, ww)[0]
        return 0.5 * jnp.sum(jnp.mean(jnp.square(y - tt), axis=-1, keepdims=True), axis=0, keepdims=True)

    def body(x_ref, w_ref, t_ref, loss_ref, dx_ref, dw_ref):
        val, vjp = jax.vjp(lambda xx, ww: fl(xx, ww, t_ref[...]), x_ref[...], w_ref[...])
        dx, dw = vjp(jnp.ones((1, 1), f32))
        dx_ref[...] = dx

        @pl.when(pl.program_id(0) == 0)
        def _():
            dw_ref[...] = jnp.zeros_like(dw_ref)
            loss_ref[...] = jnp.zeros_like(loss_ref)
        dw_ref[...] += dw
        loss_ref[...] += val

    return pl.pallas_call(body, name="loss_head", grid=(t // tm,),
                          in_specs=[_row_spec(tm, d), _par_spec((1, d)), _row_spec(tm, d)],
                          out_specs=[_par_spec((1, 1)), _row_spec(tm, d), _par_spec((1, d))],
                          out_shape=[jax.ShapeDtypeStruct((1, 1), f32), jax.ShapeDtypeStruct((t, d), f32),
                                     jax.ShapeDtypeStruct((1, d), f32)],
                          compiler_params=_cp(1))(x, w, tgt)


def ffn_fwd(x, nw, wg, wu, wd, tm=1024, ride=None):
    t, d = x.shape
    ns, _, _, fs = wg.shape
    rider = Rider(ride)
    nr = rider.n
    grid = (t // tm, ns)

    def body(*refs):
        x_ref, nw_ref, wg_ref, wu_ref, wd_ref = refs[:5]
        r_ins = refs[5:5 + nr]
        xo_ref, h_ref, g_ref, u_ref = refs[5 + nr:9 + nr]
        r_outs, acc, sems = refs[9 + nr:9 + 2 * nr], refs[9 + 2 * nr], refs[10 + 2 * nr:]
        rider.start(grid, r_ins, r_outs, sems)
        j = pl.program_id(1)

        @pl.when(j == 0)
        def _():
            h_ref[...] = f_rms(None, x_ref[...], nw_ref[...])[0].astype(bf16)
            acc[...] = jnp.zeros_like(acc)

        h = h_ref[...]
        g = jnp.dot(h, wg_ref[0, 0], preferred_element_type=f32)
        u = jnp.dot(h, wu_ref[0, 0], preferred_element_type=f32)
        g_ref[0] = g
        u_ref[0] = u
        acc[...] += jnp.dot((_silu(g) * u).astype(bf16), wd_ref[0, 0], preferred_element_type=f32)

        @pl.when(j == ns - 1)
        def _():
            xo_ref[...] = x_ref[...] + FFN_RES * acc[...]

        rider.wait(grid, r_ins, r_outs, sems)

    row = pl.BlockSpec((tm, d), lambda i, j: (i, 0))
    wcol = pl.BlockSpec((1, 1, d, fs), lambda i, j: (j, 0, 0, 0))
    wrow = pl.BlockSpec((1, 1, fs, d), lambda i, j: (j, 0, 0, 0))
    act = pl.BlockSpec((1, tm, fs), lambda i, j: (j, i, 0))
    return pl.pallas_call(body, name="ffn_fwd", grid=grid,
                          in_specs=[row, pl.BlockSpec((1, d), lambda i, j: (0, 0)), wcol, wcol, wrow] + rider.specs(),
                          out_specs=[row, row, act, act] + rider.specs(),
                          out_shape=[jax.ShapeDtypeStruct((t, d), f32), jax.ShapeDtypeStruct((t, d), bf16),
                                     jax.ShapeDtypeStruct((ns, t, fs), f32), jax.ShapeDtypeStruct((ns, t, fs), f32)]
                          + rider.out_shapes(),
                          scratch_shapes=[pltpu.VMEM((tm, d), f32)] + rider.scratch(),
                          compiler_params=_cp(2))(x, nw, wg, wu, wd, *rider.arrays())


def ffn_bwd_act(dy, x, nw, g, u, wg, wu, wd, tm=512, ride=None):
    t, d = x.shape
    ns, _, _, fs = wg.shape
    rider = Rider(ride)
    nr = rider.n
    grid = (t // tm, ns)

    def body(*refs):
        dy_ref, x_ref, nw_ref, g_ref, u_ref, wg_ref, wu_ref, wd_ref = refs[:8]
        r_ins = refs[8:8 + nr]
        dx_ref, dnw_ref, dg_ref, du_ref, a_ref, dyh_ref = refs[8 + nr:14 + nr]
        r_outs, acc, sems = refs[14 + nr:14 + 2 * nr], refs[14 + 2 * nr], refs[15 + 2 * nr:]
        rider.start(grid, r_ins, r_outs, sems)
        i, j = pl.program_id(0), pl.program_id(1)

        @pl.when(j == 0)
        def _():
            acc[...] = jnp.zeros_like(acc)
            dyh_ref[...] = (FFN_RES * dy_ref[...]).astype(bf16)

        dyh = dyh_ref[...]
        da = _dg(dyh, wd_ref[0, 0], 1, 1)
        gg, uu = g_ref[0], u_ref[0]
        sg = jax.nn.sigmoid(gg)
        si = gg * sg
        dgv = (da * uu * (sg * (1.0 + gg * (1.0 - sg)))).astype(bf16)
        duv = (da * si).astype(bf16)
        dg_ref[0] = dgv
        du_ref[0] = duv
        a_ref[0] = (si * uu).astype(bf16)
        acc[...] += _dg(dgv, wg_ref[0, 0], 1, 1) + _dg(duv, wu_ref[0, 0], 1, 1)

        @pl.when(j == ns - 1)
        def _():
            _, vjp = jax.vjp(lambda xx, ww: f_rms(None, xx, ww)[0], x_ref[...], nw_ref[...])
            dx, dw = vjp(acc[...])
            dx_ref[...] = dx + dy_ref[...]

            @pl.when(i == 0)
            def _():
                dnw_ref[...] = jnp.zeros_like(dnw_ref)
            dnw_ref[...] += dw

        rider.wait(grid, r_ins, r_outs, sems)

    row = pl.BlockSpec((tm, d), lambda i, j: (i, 0))
    wcol = pl.BlockSpec((1, 1, d, fs), lambda i, j: (j, 0, 0, 0))
    wrow = pl.BlockSpec((1, 1, fs, d), lambda i, j: (j, 0, 0, 0))
    act = pl.BlockSpec((1, tm, fs), lambda i, j: (j, i, 0))
    par = pl.BlockSpec((1, d), lambda i, j: (0, 0))
    return pl.pallas_call(body, name="ffn_bwd_act", grid=grid,
                          in_specs=[row, row, par, act, act, wcol, wcol, wrow] + rider.specs(),
                          out_specs=[row, par, act, act, act, row] + rider.specs(),
                          out_shape=[jax.ShapeDtypeStruct((t, d), f32), jax.ShapeDtypeStruct((1, d), f32)]
                          + [jax.ShapeDtypeStruct((ns, t, fs), bf16)] * 3 + [jax.ShapeDtypeStruct((t, d), bf16)]
                          + rider.out_shapes(),
                          scratch_shapes=[pltpu.VMEM((tm, d), f32)] + rider.scratch(),
                          compiler_params=_cp(2))(dy, x, nw, g, u, wg, wu, wd, *rider.arrays())


def ffn_bwd_w(h, dyh, dg, du, a, tk=1024, ride=None):
    t, d = h.shape
    ns, _, fs = dg.shape
    nk = t // tk
    rider = Rider(ride)
    nr = rider.n
    grid = (ns, nk)

    def body(*refs):
        h_ref, dy_ref, dg_ref, du_ref, a_ref = refs[:5]
        r_ins = refs[5:5 + nr]
        og, ou, od = refs[5 + nr:8 + nr]
        r_outs = refs[8 + nr:8 + 2 * nr]
        ag, au, ad = refs[8 + 2 * nr:11 + 2 * nr]
        sems = refs[11 + 2 * nr:]
        rider.start(grid, r_ins, r_outs, sems)
        k = pl.program_id(1)

        @pl.when(k == 0)
        def _():
            ag[...] = jnp.zeros_like(ag)
            au[...] = jnp.zeros_like(au)
            ad[...] = jnp.zeros_like(ad)

        hh = h_ref[...]
        ag[...] += _dg(hh, dg_ref[0], 0, 0)
        au[...] += _dg(hh, du_ref[0], 0, 0)
        ad[...] += _dg(a_ref[0], dy_ref[...], 0, 0)

        @pl.when(k == nk - 1)
        def _():
            og[0, 0] = ag[...].astype(og.dtype)
            ou[0, 0] = au[...].astype(ou.dtype)
            od[0, 0] = ad[...].astype(od.dtype)

        rider.wait(grid, r_ins, r_outs, sems)

    row = pl.BlockSpec((tk, d), lambda j, k: (k, 0))
    act = pl.BlockSpec((1, tk, fs), lambda j, k: (j, k, 0))
    wcol = pl.BlockSpec((1, 1, d, fs), lambda j, k: (j, 0, 0, 0))
    wrow = pl.BlockSpec((1, 1, fs, d), lambda j, k: (j, 0, 0, 0))
    return pl.pallas_call(body, name="ffn_bwd_w", grid=grid, in_specs=[row, row, act, act, act] + rider.specs(),
                          out_specs=[wcol, wcol, wrow] + rider.specs(),
                          out_shape=[jax.ShapeDtypeStruct((ns, 1, d, fs), bf16)] * 2
                          + [jax.ShapeDtypeStruct((ns, 1, fs, d), bf16)] + rider.out_shapes(),
                          scratch_shapes=[pltpu.VMEM((d, fs), f32), pltpu.VMEM((d, fs), f32), pltpu.VMEM((fs, d), f32)]
                          + rider.scratch(),
                          compiler_params=_cp(2))(h, dyh, dg, du, a, *rider.arrays())


def f_conv(pids, x, w, b):
    y = b + x * w[CONV_W - 1:CONV_W, :]
    for j in range(CONV_W - 1):
        y = y + tshift(x, CONV_W - 1 - j) * w[j:j + 1, :]
    return (_silu(y),)


def _conv_specs(seq, col0, cb):
    xs = pl.BlockSpec((seq, cb), lambda c, b: (b, col0 + c))
    ws = pl.BlockSpec((CONV_W, cb), lambda c, b: (0, c))
    bs = pl.BlockSpec((1, cb), lambda c, b: (0, c))
    ys = pl.BlockSpec((seq, cb), lambda c, b: (b, c))
    return xs, ws, bs, ys


def conv_fwd(name, src, col_off, w, b, seq, cb=256):
    t = src.shape[0]
    c = w.shape[1]
    xs, ws, bs, ys = _conv_specs(seq, col_off // cb, cb)
    return map_fwd(name, f_conv, (c // cb, t // seq), [src, w, b], [xs, ws, bs],
                   [jax.ShapeDtypeStruct((t, c), f32)], [ys])[0]


def conv_bwd(name, dy, src, col_off, w, b, seq, cb=256):
    t = src.shape[0]
    c = w.shape[1]
    xs, ws, bs, ys = _conv_specs(seq, col_off // cb, cb)

    def body(x_ref, w_ref, b_ref, dy_ref, dx_ref, dw_ref, db_ref):
        _, vjp = jax.vjp(lambda xx, ww, bb: f_conv(None, xx, ww, bb)[0], x_ref[...], w_ref[...], b_ref[...])
        dx, dw, db = vjp(dy_ref[...])
        dx_ref[...] = dx

        @pl.when(pl.program_id(1) == 0)
        def _():
            dw_ref[...] = jnp.zeros_like(dw_ref)
            db_ref[...] = jnp.zeros_like(db_ref)
        dw_ref[...] += dw
        db_ref[...] += db

    return pl.pallas_call(body, name=name, grid=(c // cb, t // seq), in_specs=[xs, ws, bs, ys], out_specs=[ys, ws, bs],
                          out_shape=[jax.ShapeDtypeStruct((t, c), f32), jax.ShapeDtypeStruct(w.shape, f32),
                                     jax.ShapeDtypeStruct(b.shape, f32)], compiler_params=_cp(2))(src, w, b, dy)


def f_ssd(pids, states, xs, dtraw, bm, cm, a_log, dt_bias, d_skip):
    g = pids[2]
    (hn,) = states
    l = xs.shape[0]
    head_of_lane = _iota((LANES, SSD_GW), 1) // SSD_HEAD_DIM + SSD_HG * g
    expand = (_iota((LANES, SSD_GW), 0) == head_of_lane).astype(f32)
    tri = _tri(l)
    dt = jax.nn.softplus(dtraw + dt_bias)
    adt = dt * (-jnp.exp(a_log))
    cs = hdot(tri, adt, exact="a")
    cst = cs.T
    cs_last = cs[l - 1:l, :]
    dt_e, cs_e, csl_e = hdot(dt, expand), hdot(cs, expand), hdot(cs_last, expand)
    xd = xs * dt_e
    gmat = bdot(cm, bm, 1, 1)
    half = _iota((l, LANES), 1) < SSD_HEAD_DIM
    blocks = []
    for pair in range(SSD_HG // 2):
        xb = xd[:, pair * LANES:(pair + 1) * LANES]
        res = []
        for sub in range(2):
            hid = SSD_HG * g + 2 * pair + sub
            col, row = _lane_pick(cs, hid), _row_pick(cst, hid)
            lm = jnp.exp(jnp.where(tri, col - row, NEG))
            res.append(bdot(gmat * lm, xb, 1, 0))
        blocks.append(jnp.where(half, res[0], res[1]))
    y = jnp.concatenate(blocks, axis=1)
    y = y + jnp.exp(cs_e) * bdot(cm, hn, 1, 0)
    y = y + hdot(d_skip, expand) * xs
    hn_new = jnp.exp(csl_e) * hn + bdot(bm, jnp.exp(csl_e - cs_e) * xd, 0, 0)
    return (y,), (hn_new,)


def _ssd_specs(seq, nch, rev):
    cc = (lambda c: nch - 1 - c) if rev else (lambda c: c)
    xs = pl.BlockSpec((CHUNK, SSD_GW), lambda b, c, g: (b * nch + cc(c), g))
    dt = pl.BlockSpec((CHUNK, LANES), lambda b, c, g: (b * nch + cc(c), OFF_DT // LANES))
    bm = pl.BlockSpec((CHUNK, SSD_STATE), lambda b, c, g: (b * nch + cc(c), 1024 // SSD_STATE + g))
    cm = pl.BlockSpec((CHUNK, SSD_STATE), lambda b, c, g: (b * nch + cc(c), 1024 // SSD_STATE + SSD_GROUPS + g))
    par = pl.BlockSpec((1, LANES), lambda b, c, g: (0, 0))
    sv = pl.BlockSpec((1, 1, SSD_STATE, SSD_GW), lambda b, c, g: (b * nch + cc(c), g, 0, 0))
    ddt = pl.BlockSpec((CHUNK, LANES), lambda b, c, g: (b * nch + cc(c), 0))
    dbc = pl.BlockSpec((CHUNK, SSD_STATE), lambda b, c, g: (b * nch + cc(c), g))
    return xs, dt, bm, cm, par, sv, ddt, dbc


def ssd_fwd(xbc, proj, a_log, dt_bias, d_skip, seq, ride=None):
    t = xbc.shape[0]
    nch = seq // CHUNK
    xs, dt, bm, cm, par, sv, _, _ = _ssd_specs(seq, nch, False)
    grid = (t // seq, nch, SSD_GROUPS)
    y, hsave, *got = scan_fwd("ssd_fwd", f_ssd, grid, 2, [xbc, proj, xbc, xbc, a_log, dt_bias, d_skip],
                              [xs, dt, bm, cm, par, par, par], [jax.ShapeDtypeStruct((t, SSD_GROUPS * SSD_GW), f32)], [xs],
                              [(SSD_STATE, SSD_GW)], [0.0],
                              [jax.ShapeDtypeStruct((t // CHUNK, SSD_GROUPS, SSD_STATE, SSD_GW), f32)], [sv], ride=ride)
    return y, hsave, got


def ssd_bwd(dy, xbc, proj, a_log, dt_bias, d_skip, hsave, seq):
    t = xbc.shape[0]
    nch = seq // CHUNK
    xs, dt, bm, cm, par, sv, ddt, dbc = _ssd_specs(seq, nch, True)
    grid = (t // seq, nch, SSD_GROUPS)

    def body(x_ref, dt_ref, b_ref, c_ref, al_ref, db_ref, ds_ref, h_ref, dy_ref,
             dxbc_x, dxbc_b, dxbc_c, ddt_ref, dal_ref, ddb_ref, dds_ref, dst):
        pids = tuple(pl.program_id(i) for i in range(3))
        slot = pids[2]

        @pl.when(pids[1] == 0)
        def _():
            dst[slot] = jnp.zeros(dst.shape[1:], f32)

        vals = [x_ref[...], dt_ref[...], b_ref[...], c_ref[...], al_ref[...], db_ref[...], ds_ref[...]]

        def gfun(st, *v):
            outs, new = f_ssd(pids, (st,), *v)
            return outs[0], new[0]

        _, vjp = jax.vjp(gfun, h_ref[0, 0], *vals)
        grads = vjp((dy_ref[...], dst[slot]))
        dst[slot] = grads[0]
        dxbc_x[...] = grads[1]
        dxbc_b[...] = grads[3]
        dxbc_c[...] = grads[4]

        @pl.when(slot == 0)
        def _():
            ddt_ref[...] = jnp.zeros_like(ddt_ref)
        ddt_ref[...] += grads[2]
        first = jnp.logical_and(jnp.logical_and(pids[0] == 0, pids[1] == 0), slot == 0)

        @pl.when(first)
        def _():
            dal_ref[...] = jnp.zeros_like(dal_ref)
            ddb_ref[...] = jnp.zeros_like(ddb_ref)
            dds_ref[...] = jnp.zeros_like(dds_ref)
        dal_ref[...] += grads[5]
        ddb_ref[...] += grads[6]
        dds_ref[...] += grads[7]

    bc_shape = jax.ShapeDtypeStruct((t, SSD_GROUPS * SSD_STATE), f32)
    par_shape = jax.ShapeDtypeStruct((1, LANES), f32)
    outs = pl.pallas_call(body, name="ssd_bwd", grid=grid, in_specs=[xs, dt, bm, cm, par, par, par, sv, xs],
                          out_specs=[xs, dbc, dbc, ddt, par, par, par],
                          out_shape=[jax.ShapeDtypeStruct((t, SSD_GROUPS * SSD_GW), f32),
                                     bc_shape, bc_shape, jax.ShapeDtypeStruct((t, LANES), f32),
                                     par_shape, par_shape, par_shape],
                          scratch_shapes=[pltpu.VMEM((SSD_GROUPS, SSD_STATE, SSD_GW), f32)],
                          compiler_params=_cp(3))(xbc, proj, xbc, xbc, a_log, dt_bias, d_skip, hsave, dy)
    return outs


def f_ssd_epi(pids, y, z, nw):
    yg = y * _silu(z)
    hw = yg.shape[1] // SSD_GROUPS
    parts = []
    for g in range(SSD_GROUPS):
        p = yg[:, g * hw:(g + 1) * hw]
        parts.append(p * lax.rsqrt(jnp.mean(p * p, axis=-1, keepdims=True) + EPS))
    return (jnp.concatenate(parts, axis=1) * nw,)


def f_ml_epi(pids, hm, xc, mz, nw, skip):
    parts = []
    for h in range(ML_HEADS):
        p = hm[:, h * ML_HD:(h + 1) * ML_HD]
        mu = jnp.mean(p, axis=-1, keepdims=True)
        var = jnp.mean(jnp.square(p - mu), axis=-1, keepdims=True)
        parts.append((p - mu) * lax.rsqrt(var + EPS))
    hn = jnp.concatenate(parts, axis=1) * nw
    return ((hn + skip * xc) * _silu(mz),)


def f_s5_post(pids, ys, u, d_skip):
    return (jax.nn.gelu(ys + d_skip * u),)


def f_glu(pids, pab, ba, bb):
    d = ba.shape[1]
    return ((pab[:, :d] + ba) * jax.nn.sigmoid(pab[:, d:] + bb),)


def f_glu_res(pids, pab, xres, ba, bb):
    return (xres + f_glu(pids, pab, ba, bb)[0],)


def rowwise_fwd(name, f, rows, row_cols, pars, out_width, tm=512):
    t = rows[0].shape[0]
    specs = [_row_spec(tm, w, c) for (w, c) in row_cols] + [_par_spec(p.shape) for p in pars]
    return map_fwd(name, f, (t // tm,), list(rows) + list(pars), specs, [jax.ShapeDtypeStruct((t, out_width), f32)],
                   [_row_spec(tm, out_width)])[0]


def rowwise_bwd(name, f, rows, row_cols, pars, dy, tm=256):
    t = rows[0].shape[0]
    n_r, n_p = len(rows), len(pars)
    specs = [_row_spec(tm, w, c) for (w, c) in row_cols] + [_par_spec(p.shape) for p in pars]
    out_w = dy.shape[1]

    def body(*refs):
        vals = [r[...] for r in refs[:n_r + n_p]]
        dy_ref = refs[n_r + n_p]
        outs = refs[n_r + n_p + 1:]
        _, vjp = jax.vjp(lambda *v: f(None, *v)[0], *vals)
        grads = vjp(dy_ref[...])
        for k in range(n_r):
            outs[k][...] = grads[k]

        @pl.when(pl.program_id(0) == 0)
        def _():
            for k in range(n_p):
                outs[n_r + k][...] = jnp.zeros_like(outs[n_r + k])
        for k in range(n_p):
            outs[n_r + k][...] += grads[n_r + k]

    out_shapes = [jax.ShapeDtypeStruct((t, w), f32) for (w, c) in row_cols] + [jax.ShapeDtypeStruct(p.shape, f32) for p in pars]
    out_specs = [_row_spec(tm, w) for (w, c) in row_cols] + [_par_spec(p.shape) for p in pars]
    return pl.pallas_call(body, name=name, grid=(t // tm,), in_specs=specs + [_row_spec(tm, out_w)], out_specs=out_specs,
                          out_shape=out_shapes, compiler_params=_cp(1))(*rows, *pars, dy)


def f_ml(pids, states, q, k, v, g1, g2, g3, b_if):
    h = pids[2]
    cst, nst, mst = states
    l = q.shape[0]
    gt = g1 + g2 + g3 + b_if
    k = k * (1.0 / math.sqrt(ML_HD))
    tri = _tri(l)
    bc_all = hdot(tri, jax.nn.log_sigmoid(gt), exact="a")
    bcum, ig = _lane_pick(bc_all, ML_HEADS + h), _lane_pick(gt, h)
    bcum_t, ig_t = _row_pick(bc_all.T, ML_HEADS + h), _row_pick(gt.T, h)
    b_last = bcum[l - 1:l, :]
    dlog = jnp.where(tri, bcum - bcum_t + ig_t, NEG)
    ws = b_last - bcum + ig
    m_prev = mst[:, 0:1]
    m_new = lax.stop_gradient(jnp.maximum(b_last + m_prev, jnp.max(ws, axis=0, keepdims=True)))
    decay = jnp.exp(b_last + m_prev - m_new)
    wts = jnp.exp(ws - m_new)
    c_new = decay * cst + bdot(wts * v, k, 0, 0)
    n_new = decay * nst + jnp.sum(wts * k, axis=0, keepdims=True)
    m_inter = bcum + m_prev
    m_t = lax.stop_gradient(jnp.maximum(jnp.max(dlog, axis=1, keepdims=True), m_inter))
    scores = bdot(q, k, 1, 1) * jnp.exp(dlog - m_t)
    inter_w = jnp.exp(m_inter - m_t)
    num = bdot(scores, v, 1, 0) + inter_w * bdot(q, cst, 1, 1)
    den = jnp.sum(scores, axis=1, keepdims=True) + inter_w * jnp.sum(q * nst, axis=1, keepdims=True)
    hout = num / jnp.maximum(jnp.abs(den), jnp.exp(-m_t))
    return (hout,), (c_new, n_new, jnp.broadcast_to(m_new, mst.shape))


def _ml_specs(nch, rev):
    cc = (lambda c: nch - 1 - c) if rev else (lambda c: c)
    hd = pl.BlockSpec((CHUNK, ML_HD), lambda b, c, h: (b * nch + cc(c), h))
    gt = pl.BlockSpec((CHUNK, LANES), lambda b, c, h: (b * nch + cc(c), 0))
    par = pl.BlockSpec((1, LANES), lambda b, c, h: (0, 0))
    sc = pl.BlockSpec((1, 1, ML_HD, ML_HD), lambda b, c, h: (b * nch + cc(c), h, 0, 0))
    sn = pl.BlockSpec((1, 1, 1, ML_HD), lambda b, c, h: (b * nch + cc(c), h, 0, 0))
    sm = pl.BlockSpec((1, 1, 1, LANES), lambda b, c, h: (b * nch + cc(c), h, 0, 0))
    return hd, gt, par, sc, sn, sm


ML_STATE_SHAPES = [(ML_HD, ML_HD), (1, ML_HD), (1, LANES)]


def ml_fwd(q, k, v, g1, g2, g3, b_if, seq, ride=None):
    t = q.shape[0]
    nch = seq // CHUNK
    hd, gt, par, sc, sn, sm = _ml_specs(nch, False)
    nc = t // CHUNK
    outs = scan_fwd("ml_fwd", f_ml, (t // seq, nch, ML_HEADS), 2, [q, k, v, g1, g2, g3, b_if],
                    [hd, hd, hd, gt, gt, gt, par], [jax.ShapeDtypeStruct((t, ML_HEADS * ML_HD), f32)], [hd],
                    ML_STATE_SHAPES, [0.0, 0.0, NEG],
                    [jax.ShapeDtypeStruct((nc, ML_HEADS, ML_HD, ML_HD), f32), jax.ShapeDtypeStruct((nc, ML_HEADS, 1, ML_HD), f32),
                     jax.ShapeDtypeStruct((nc, ML_HEADS, 1, LANES), f32)], [sc, sn, sm], ride=ride)
    return outs[0], outs[1:4], outs[4:]


def ml_bwd(dh, q, k, v, g1, g2, g3, b_if, saves, seq, ride=None):
    t = q.shape[0]
    nch = seq // CHUNK
    hd, gt, par, sc, sn, sm = _ml_specs(nch, True)
    rider = Rider(ride)
    nr = rider.n
    grid = (t // seq, nch, ML_HEADS)

    def f(pids, states, q, k, v, gsum, b_if):
        return f_ml(pids, states, q, k, v, gsum, jnp.zeros_like(gsum), jnp.zeros_like(gsum), b_if)

    def body(*refs):
        q_ref, k_ref, v_ref, g1_ref, g2_ref, g3_ref, b_ref, c_ref, n_ref, m_ref, dh_ref = refs[:11]
        r_ins = refs[11:11 + nr]
        dq_ref, dk_ref, dv_ref, dg_ref, db_ref = refs[11 + nr:16 + nr]
        r_outs = refs[16 + nr:16 + 2 * nr]
        dc_s, dn_s = refs[16 + 2 * nr:18 + 2 * nr]
        sems = refs[18 + 2 * nr:]
        rider.start(grid, r_ins, r_outs, sems)
        pids = tuple(pl.program_id(i) for i in range(3))
        slot = pids[2]

        @pl.when(pids[1] == 0)
        def _():
            dc_s[slot] = jnp.zeros(dc_s.shape[1:], f32)
            dn_s[slot] = jnp.zeros(dn_s.shape[1:], f32)

        gsum = g1_ref[...] + g2_ref[...] + g3_ref[...]
        mst = m_ref[0, 0]

        def gfun(cst, nst, qq, kk, vv, gs, bb):
            outs, new = f(pids, (cst, nst, mst), qq, kk, vv, gs, bb)
            return outs[0], new[0], new[1]

        _, vjp = jax.vjp(gfun, c_ref[0, 0], n_ref[0, 0], q_ref[...], k_ref[...], v_ref[...], gsum, b_ref[...])
        grads = vjp((dh_ref[...], dc_s[slot], dn_s[slot]))
        dc_s[slot] = grads[0]
        dn_s[slot] = grads[1]
        dq_ref[...] = grads[2]
        dk_ref[...] = grads[3]
        dv_ref[...] = grads[4]

        @pl.when(slot == 0)
        def _():
            dg_ref[...] = jnp.zeros_like(dg_ref)
        dg_ref[...] += grads[5]
        first = jnp.logical_and(jnp.logical_and(pids[0] == 0, pids[1] == 0), slot == 0)

        @pl.when(first)
        def _():
            db_ref[...] = jnp.zeros_like(db_ref)
        db_ref[...] += grads[6]
        rider.wait(grid, r_ins, r_outs, sems)

    big = jax.ShapeDtypeStruct((t, ML_HEADS * ML_HD), f32)
    return pl.pallas_call(body, name="ml_bwd", grid=grid,
                          in_specs=[hd, hd, hd, gt, gt, gt, par, sc, sn, sm, hd] + rider.specs(),
                          out_specs=[hd, hd, hd, gt, par] + rider.specs(),
                          out_shape=[big, big, big, jax.ShapeDtypeStruct((t, LANES), f32), jax.ShapeDtypeStruct((1, LANES), f32)]
                          + rider.out_shapes(),
                          scratch_shapes=[pltpu.VMEM((ML_HEADS, ML_HD, ML_HD), f32), pltpu.VMEM((ML_HEADS, 1, ML_HD), f32)]
                          + rider.scratch(),
                          compiler_params=_cp(3))(q, k, v, g1, g2, g3, b_if, *saves, dh, *rider.arrays())


def _block_prefix(z, transpose):
    n = z.shape[0]
    r, c = _iota((n, n), 0), _iota((n, n), 1)
    keep = jnp.logical_and(r // S5_SUB == c // S5_SUB, (c >= r) if transpose else (c <= r))
    m = jnp.where(keep, 1.0, 0.0).astype(bf16)
    hi = z.astype(bf16)
    lo = (z - hi.astype(f32)).astype(bf16)
    return jnp.dot(m, hi, preferred_element_type=f32) + jnp.dot(m, lo, preferred_element_type=f32)


@jax.custom_vjp
def block_prefix(z):
    return _block_prefix(z, False)


block_prefix.defvjp(lambda z: (_block_prefix(z, False), None), lambda _, ct: (_block_prefix(ct, True),))


def _cmul(a, b):
    h = b.shape[1] // 2
    ar, ai, br, bi = a[:, :h], a[:, h:], b[:, :h], b[:, h:]
    return jnp.concatenate([ar * br - ai * bi, ar * bi + ai * br], axis=1)


def f_s5(pids, states, u, bb, cc, tab):
    (carry,) = states
    tl = u.shape[0]
    nsub = tl // S5_SUB
    rep = lambda t: jnp.concatenate([t] * nsub, axis=0)
    p0, q0 = tab[S5_SUB:2 * S5_SUB], tab[2 * S5_SUB:3 * S5_SUB]
    lam, p0_last = tab[0:1], tab[2 * S5_SUB - 1:2 * S5_SUB]
    bu = bdot(u, bb, 1, 0)
    pre = block_prefix(_cmul(rep(q0), bu))
    e, entering = carry, []
    for k in range(nsub):
        le = _cmul(lam, e)
        entering.append(jnp.broadcast_to(le, (S5_SUB, le.shape[1])))
        e = _cmul(p0_last, pre[(k + 1) * S5_SUB - 1:(k + 1) * S5_SUB] + le)
    x = _cmul(rep(p0), pre + jnp.concatenate(entering, axis=0))
    y = bdot(x, cc, 1, 0)
    return (y,), (e,)


def _s5_specs(ntl, rev):
    tt = (lambda t: ntl - 1 - t) if rev else (lambda t: t)
    us = pl.BlockSpec((S5_TL, LANES), lambda c, b, t: (b * ntl + tt(t), c))
    bbs = pl.BlockSpec((1, LANES, 2 * S5_CH), lambda c, b, t: (c, 0, 0))
    ccs = pl.BlockSpec((1, 2 * S5_CH, LANES), lambda c, b, t: (c, 0, 0))
    pws = pl.BlockSpec((1, 3 * S5_SUB, 2 * S5_CH), lambda c, b, t: (c, 0, 0))
    sv = pl.BlockSpec((1, 1, 1, 2 * S5_CH), lambda c, b, t: (b * ntl + tt(t), c, 0, 0))
    return us, bbs, ccs, pws, sv


def s5_fwd(u, bb, cc, pw, seq):
    t = u.shape[0]
    ntl = seq // S5_TL
    us, bbs, ccs, pws, sv = _s5_specs(ntl, False)

    def f(pids, states, uu, b3, c3, p3):
        return f_s5(pids, states, uu, b3[0], c3[0], p3[0])

    y, carries = scan_fwd("s5_fwd", f, (S5_CB, t // seq, ntl), 0, [u, bb, cc, pw], [us, bbs, ccs, pws],
                          [jax.ShapeDtypeStruct((t, S5_CB * LANES), f32)], [us], [(1, 2 * S5_CH)], [0.0],
                          [jax.ShapeDtypeStruct((t // S5_TL, S5_CB, 1, 2 * S5_CH), f32)], [sv])
    return y, carries


def s5_bwd(dy, u, bb, cc, pw, carries, seq):
    t = u.shape[0]
    ntl = seq // S5_TL
    us, bbs, ccs, pws, sv = _s5_specs(ntl, True)

    def f(pids, states, uu, b3, c3, p3):
        return f_s5(pids, states, uu, b3[0], c3[0], p3[0])

    first = lambda pids: jnp.logical_and(pids[1] == 0, pids[2] == 0)
    return scan_bwd("s5_bwd", f, (S5_CB, t // seq, ntl), 0, [u, bb, cc, pw], [us, bbs, ccs, pws], [carries], [sv],
                    [dy], [us], [(1, 2 * S5_CH)], [0, 1, 2, 3], {1: first, 2: first, 3: first})


def _adam_math(g, w, m, v):
    m2 = ADAM_B1 * m + (1.0 - ADAM_B1) * g
    v2 = ADAM_B2 * v + (1.0 - ADAM_B2) * jnp.square(g)
    m_hat = m2 / (1.0 - ADAM_B1 ** ADAM_STEP)
    v_hat = v2 / (1.0 - ADAM_B2 ** ADAM_STEP)
    delta = -ADAM_LR * (m_hat / (jnp.sqrt(v_hat) + ADAM_EPS) + ADAM_WD * w)
    return delta, m2, v2


def adamw(name, parts, w, m, v, tr=256):
    n, r, c = parts.shape
    tr = min(tr, r)
    assert r % tr == 0

    def body(p_ref, w_ref, m_ref, v_ref, g_ref, d_ref, m2_ref, v2_ref):
        g = p_ref[0].astype(f32)
        for s in range(1, n):
            g = g + p_ref[s].astype(f32)
        d, m2, v2 = _adam_math(g, w_ref[...], m_ref[...], v_ref[...])
        g_ref[...] = g
        d_ref[...] = d
        m2_ref[...] = m2
        v2_ref[...] = v2

    ps = pl.BlockSpec((n, tr, c), lambda i: (0, i, 0))
    rs = pl.BlockSpec((tr, c), lambda i: (i, 0))
    return pl.pallas_call(body, name=name, grid=(r // tr,), in_specs=[ps, rs, rs, rs], out_specs=[rs] * 4,
                          out_shape=[jax.ShapeDtypeStruct((r, c), f32)] * 4, compiler_params=_cp(1))(parts, w, m, v)


def adamw_layer(name, parts, w, m, v, layer, prev=None, tr=256):
    n, r, c = parts.shape
    nl = w.shape[0]
    tr = min(tr, r)
    assert r % tr == 0 and w.shape[1:] == (r, c)
    n_prev = 0 if prev is None else 4

    def body(*refs):
        p_ref, w_ref, m_ref, v_ref = refs[:4]
        g_ref, d_ref, m2_ref, v2_ref = refs[4 + n_prev:]
        g = p_ref[0].astype(f32)
        for s in range(1, n):
            g = g + p_ref[s].astype(f32)
        d, m2, v2 = _adam_math(g, w_ref[0], m_ref[0], v_ref[0])
        g_ref[0] = g
        d_ref[0] = d
        m2_ref[0] = m2
        v2_ref[0] = v2

    ps = pl.BlockSpec((n, tr, c), lambda i: (0, i, 0))
    rs = pl.BlockSpec((1, tr, c), lambda i: (layer, i, 0))
    anyspec = pl.BlockSpec(memory_space=pl.ANY)
    return pl.pallas_call(body, name=name, grid=(r // tr,), in_specs=[ps, rs, rs, rs] + [anyspec] * n_prev, out_specs=[rs] * 4,
                          out_shape=[jax.ShapeDtypeStruct((nl, r, c), f32)] * 4,
                          input_output_aliases={4 + i: i for i in range(n_prev)},
                          compiler_params=_cp(1))(parts, w, m, v, *(prev or ()))


def sum_parts(name, parts, tr=256):
    n, r, c = parts.shape
    tr = min(tr, r)
    assert r % tr == 0

    def body(p_ref, o_ref):
        g = p_ref[0].astype(f32)
        for s in range(1, n):
            g = g + p_ref[s].astype(f32)
        o_ref[...] = g

    return pl.pallas_call(body, name=name, grid=(r // tr,), in_specs=[pl.BlockSpec((n, tr, c), lambda i: (0, i, 0))],
                          out_specs=pl.BlockSpec((tr, c), lambda i: (i, 0)),
                          out_shape=jax.ShapeDtypeStruct((r, c), f32), compiler_params=_cp(1))(parts)


class Rider:
    def __init__(self, ops):
        self.ops = list(ops or [])
        self.n = len(self.ops)

    def arrays(self):
        return [a for a, _ in self.ops]

    def specs(self):
        return [pl.BlockSpec(memory_space=pl.ANY)] * self.n

    def out_shapes(self):
        return [jax.ShapeDtypeStruct((N_DEV,) + tuple(a.shape) if mode == "gather" else tuple(a.shape), a.dtype)
                for a, mode in self.ops]

    def scratch(self):
        if not self.n:
            return []
        return [pltpu.SemaphoreType.DMA((self.n, N_DEV - 1)), pltpu.SemaphoreType.DMA((self.n, N_DEV - 1)),
                pltpu.SemaphoreType.DMA((self.n,))]

    def _copies(self, ins, outs, sems, with_relays=True):
        send_sems, recv_sems, loc_sems = sems
        x, y, c = lax.axis_index("x"), lax.axis_index("y"), lax.axis_index("c")
        me = 4 * x + 2 * y + c
        first, crossing, relays = [], [], []

        def remote(src, dst, k, idx, peer):
            return pltpu.make_async_remote_copy(src_ref=src, dst_ref=dst, send_sem=send_sems.at[k, idx], recv_sem=recv_sems.at[k, idx],
                                                device_id=peer, device_id_type=pl.DeviceIdType.MESH)

        for k, (_, mode) in enumerate(self.ops):
            src_me = ins[k] if mode == "gather" else ins[k].at[me]
            first.append(pltpu.make_async_copy(src_me, outs[k].at[me], loc_sems.at[k]))
            if mode == "scatter":
                for d in range(1, N_DEV):
                    px = 1 - x if (d >> 2) & 1 else x
                    py = 1 - y if (d >> 1) & 1 else y
                    pc = 1 - c if d & 1 else c
                    first.append(remote(ins[k].at[4 * px + 2 * py + pc], outs[k].at[me], k, d - 1, (px, py, pc)))
            else:
                first.append(remote(ins[k], outs[k].at[me], k, 0, (x, y, 1 - c)))
                for q in range(1, 4):
                    px = 1 - x if (q >> 1) & 1 else x
                    py = 1 - y if q & 1 else y
                    crossing.append(remote(ins[k], outs[k].at[me], k, q, (px, py, c)))
                    if with_relays:
                        block = outs[k].at[4 * px + 2 * py + c]
                        relays.append(remote(block, block, k, 3 + q, (x, y, 1 - c)))
        return first, crossing, relays

    def _start(self, ins, outs, sems):
        first, crossing, _ = self._copies(ins, outs, sems, with_relays=False)
        for cp in first + crossing:
            cp.start()

    def _finish(self, ins, outs, sems):
        first, crossing, relays = self._copies(ins, outs, sems)
        for cp, relay in zip(crossing, relays):
            cp.wait_recv()
            relay.start()
        for cp in first + relays:
            cp.wait()
        for cp in crossing:
            cp.wait_send()

    def start(self, grid, ins, outs, sems):
        if self.n:
            @pl.when(functools.reduce(jnp.logical_and, [pl.program_id(i) == 0 for i in range(len(grid))]))
            def _():
                self._start(ins, outs, sems)

    def wait(self, grid, ins, outs, sems):
        if self.n:
            @pl.when(functools.reduce(jnp.logical_and, [pl.program_id(i) == g - 1 for i, g in enumerate(grid)]))
            def _():
                self._finish(ins, outs, sems)


def exchange(name, ops):
    rider = Rider(ops)
    n = rider.n

    def body(*refs):
        rider._start(refs[:n], refs[n:2 * n], refs[2 * n:])
        rider._finish(refs[:n], refs[n:2 * n], refs[2 * n:])

    return pl.pallas_call(body, name=name, in_specs=rider.specs(), out_specs=rider.specs(), out_shape=rider.out_shapes(),
                          scratch_shapes=rider.scratch())(*rider.arrays())


def _lanes(v, width=LANES):
    v = v.reshape(1, -1)
    return jnp.pad(v, ((0, 0), (0, width - v.shape[1])))


def win_to_padded(w):
    return jnp.concatenate([w[:, :1024], w[:, 2576:3600], w[:, 3600:4624], w[:, 1024:2560], w[:, 2560:2576],
                            jnp.zeros((w.shape[0], PROJ_W - IN_COLS), w.dtype)], axis=1)


def win_from_padded(wp):
    return jnp.concatenate([wp[:, 0:1024], wp[:, 3072:4608], wp[:, 4608:4624], wp[:, 1024:2048], wp[:, 2048:3072]], axis=1)


def headwise_dense(w):
    nb, o, i = w.shape
    rows = jnp.tile(w.transpose(0, 2, 1).reshape(nb * i, o), (1, nb))
    same = (jnp.arange(nb * i)[:, None] // i) == (jnp.arange(nb * o)[None, :] // o)
    return jnp.where(same, rows, 0.0)


def diag_blocks(name, dd, blk, tm=256):
    n = dd.shape[0]

    def body(d_ref, o_ref):
        rows = _iota((tm, n), 0) + pl.program_id(0) * tm
        masked = jnp.where(rows // blk == _iota((tm, n), 1) // blk, d_ref[...], 0.0)
        sel = (_iota((n, LANES), 0) % blk == _iota((n, LANES), 1)).astype(f32)
        o_ref[...] = hdot(masked, sel)

    return pl.pallas_call(body, name=name, grid=(n // tm,), in_specs=[pl.BlockSpec((tm, n), lambda i: (i, 0))],
                          out_specs=pl.BlockSpec((tm, LANES), lambda i: (i, 0)),
                          out_shape=jax.ShapeDtypeStruct((n, LANES), f32), compiler_params=_cp(1))(dd)


def headwise_from_dense(name, dd, o=4, i=4):
    nb = dd.shape[0] // i
    return diag_blocks(name, dd, i)[:, :o].reshape(nb, i, o).transpose(0, 2, 1)


def s5_tables(a_re, a_im, log_step, b_re, b_im, c_re, c_im):
    step = jnp.exp(log_step)[:, None]
    j = jnp.arange(S5_SUB, dtype=f32)[:, None, None]
    expo = jnp.concatenate([j + 1.0, j, -j], axis=0)
    mag = jnp.exp(expo * (a_re * step))
    pw_re, pw_im = mag * jnp.cos(expo * (a_im * step)), mag * jnp.sin(expo * (a_im * step))
    lam_re, lam_im = pw_re[0], pw_im[0]
    den = a_re * a_re + a_im * a_im
    coef_re = ((lam_re - 1.0) * a_re + lam_im * a_im) / den
    coef_im = (lam_im * a_re - (lam_re - 1.0) * a_im) / den
    bb_re = coef_re[..., None] * b_re - coef_im[..., None] * b_im
    bb_im = coef_re[..., None] * b_im + coef_im[..., None] * b_re
    gl = S5_GROUPS // S5_CB
    eye = jnp.eye(gl, dtype=f32)

    def blk_b(t):
        t4 = t.transpose(0, 2, 1).reshape(S5_CB, gl, S5_GROUP, S5_STATE)
        return jnp.einsum("kgcn,gh->kgchn", t4, eye).reshape(S5_CB, gl * S5_GROUP, gl * S5_STATE)

    def blk_c(t):
        t4 = t.reshape(S5_CB, gl, S5_GROUP, S5_STATE)
        return jnp.einsum("kgcn,gh->kgnhc", t4, eye).reshape(S5_CB, gl * S5_STATE, gl * S5_GROUP)

    def blk_p(t):
        return t.reshape(t.shape[0], S5_CB, gl * S5_STATE).transpose(1, 0, 2)

    bb = jnp.concatenate([blk_b(bb_re), blk_b(bb_im)], axis=2)
    cc = jnp.concatenate([blk_c(c_re), -blk_c(c_im)], axis=1)
    pw = jnp.concatenate([blk_p(pw_re), blk_p(pw_im)], axis=2)
    return bb, cc, pw


def ffn_step_bwd(dy, x, nw, wts, saved, ride_act=None, ride_w=None):
    h, g, u = saved
    dx, dnw, dg, du, a, dyh, *got_act = ffn_bwd_act(dy, x, nw, g, u, *wts, ride=ride_act)
    dwg, dwu, dwd, *got_w = ffn_bwd_w(h, dyh, dg, du, a, ride=ride_w)
    return dx, dnw, (dwg, dwu, dwd), got_act, got_w


def hybrid_fwd(x1, p, seq, ride_in, ride_ssd, ride_ml):
    u = rms_fwd(x1, p["mix_norm"])
    proj, *got_in = matmul("hy_in", u, p["win"], ride=ride_in)
    xbc = conv_fwd("ssd_conv", proj, OFF_XBC, p["ssd_conv_w"], p["ssd_conv_b"], seq)
    yraw, hsave, got_ssd = ssd_fwd(xbc, proj, p["a_log"], p["dt_bias"], p["ssd_d"], seq, ride=ride_ssd)
    yssd = rowwise_fwd("ssd_epi", f_ssd_epi, [yraw, proj], [(D_MODEL, 0), (D_MODEL, OFF_Z // D_MODEL)], [p["ssd_norm_w"]], D_MODEL)
    xc = conv_fwd("ml_conv", proj, OFF_MX, p["ml_conv_w"], p["ml_conv_b"], seq)
    q = matmul("hw_q", xc, p["wq"])
    k = matmul("hw_k", xc, p["wk"])
    v = matmul("hw_v", proj, p["wv"], a_off=OFF_MX, a_width=D_MODEL)
    g1 = matmul("gate_q", q, p["wif_q"])
    g2 = matmul("gate_k", k, p["wif_k"])
    g3 = matmul("gate_v", v, p["wif_v"])
    hm, mlsave, got_ml = ml_fwd(q, k, v, g1, g2, g3, p["b_if"], seq, ride=ride_ml)
    yml = rowwise_fwd("ml_epi", f_ml_epi, [hm, xc, proj], [(D_MODEL, 0), (D_MODEL, 0), (D_MODEL, OFF_MZ // D_MODEL)],
                      [p["ml_norm_w"], p["ml_skip"]], D_MODEL)
    t = matmul("hy_out1", yssd, p["wo1"], add=x1)
    x2 = matmul("hy_out2", yml, p["wo2"], add=t)
    return x2, (u, proj, xbc, yraw, hsave, yssd, xc, q, k, v, g1, g2, g3, hm, mlsave, yml), got_in, got_ssd, got_ml


def hybrid_bwd(dx2, x1, p, saved, seq, ride_ml):
    u, proj, xbc, yraw, hsave, yssd, xc, q, k, v, g1, g2, g3, hm, mlsave, yml = saved
    gr = {}
    dyssd = matmul("d_yssd", dx2, p["wo1"], cb=1)
    dyml = matmul("d_yml", dx2, p["wo2"], cb=1)
    gr["wo"] = jnp.concatenate([matmul("dw_o1", yssd, dx2, ca=0), matmul("dw_o2", yml, dx2, ca=0)], axis=0)
    d_hm, d_xc, d_mz, gr["ml_norm_w"], gr["ml_skip"] = rowwise_bwd(
        "ml_epi_bwd", f_ml_epi, [hm, xc, proj], [(D_MODEL, 0), (D_MODEL, 0), (D_MODEL, OFF_MZ // D_MODEL)],
        [p["ml_norm_w"], p["ml_skip"]], dyml)
    dq, dk, dv, dgt, gr["b_if"], *got_ml = ml_bwd(d_hm, q, k, v, g1, g2, g3, p["b_if"], mlsave, seq, ride=ride_ml)
    dq = matmul("dq_gate", dgt, p["wif_q"], cb=1, add=dq)
    dk = matmul("dk_gate", dgt, p["wif_k"], cb=1, add=dk)
    dv = matmul("dv_gate", dgt, p["wif_v"], cb=1, add=dv)
    gr["wif"] = jnp.concatenate([matmul("dw_if_q", q, dgt, ca=0), matmul("dw_if_k", k, dgt, ca=0),
                                 matmul("dw_if_v", v, dgt, ca=0)], axis=0)
    d_xc = matmul("dxc_q", dq, p["wq"], cb=1, add=d_xc)
    d_xc = matmul("dxc_k", dk, p["wk"], cb=1, add=d_xc)
    gr["wq"] = matmul("dw_q", xc, dq, ca=0)
    gr["wk"] = matmul("dw_k", xc, dk, ca=0)
    gr["wv"] = matmul("dw_v", proj, dv, ca=0, a_off=OFF_MX, a_width=D_MODEL)
    d_mx, gr["ml_conv_w"], gr["ml_conv_b"] = conv_bwd("ml_conv_bwd", d_xc, proj, OFF_MX, p["ml_conv_w"], p["ml_conv_b"], seq)
    d_mx = matmul("dmx_v", dv, p["wv"], cb=1, add=d_mx)
    d_yraw, d_z, gr["ssd_norm_w"] = rowwise_bwd("ssd_epi_bwd", f_ssd_epi, [yraw, proj],
                                                [(D_MODEL, 0), (D_MODEL, OFF_Z // D_MODEL)], [p["ssd_norm_w"]], dyssd)
    d_xs, d_b, d_c, d_dt, gr["a_log"], gr["dt_bias"], gr["ssd_d"] = ssd_bwd(
        d_yraw, xbc, proj, p["a_log"], p["dt_bias"], p["ssd_d"], hsave, seq)
    d_xbc, gr["ssd_conv_w"], gr["ssd_conv_b"] = conv_bwd("ssd_conv_bwd", jnp.concatenate([d_xs, d_b, d_c], axis=1), proj, OFF_XBC,
                                                         p["ssd_conv_w"], p["ssd_conv_b"], seq)
    dproj = jnp.concatenate([d_z, d_mx, d_mz, d_xbc, d_dt, jnp.zeros((d_dt.shape[0], PROJ_W - OFF_DT - LANES), f32)], axis=1)
    gr["win"] = matmul("dw_in", u.astype(bf16).T, dproj, tiles=(D_MODEL, PROJ_W // 2, min(512, u.shape[0])))
    du = matmul("d_u", dproj, p["win"], cb=1)
    dx1, gr["mix_norm"] = rms_bwd([du], x1, p["mix_norm"], dx2)
    return dx1, gr, got_ml


def s5_layer_fwd(x4, p, seq):
    u = rms_fwd(x4, p["mix_norm"])
    ys, carries = s5_fwd(u, p["bb"], p["cc"], p["pw"], seq)
    gg = rowwise_fwd("s5_post", f_s5_post, [ys, u], [(D_MODEL, 0), (D_MODEL, 0)], [p["s5_d"]], D_MODEL)
    pab = matmul("s5_ab", gg, p["wab"])
    x5 = rowwise_fwd("s5_glu", f_glu_res, [pab, x4], [(2 * D_MODEL, 0), (D_MODEL, 0)], [p["b_a"], p["b_b"]], D_MODEL)
    return x5, (u, ys, carries, gg, pab)


def s5_layer_bwd(dx5, x4, p, saved, seq):
    u, ys, carries, gg, pab = saved
    gr = {}
    dpab, gr["b_a"], gr["b_b"] = rowwise_bwd("s5_glu_bwd", f_glu, [pab], [(2 * D_MODEL, 0)], [p["b_a"], p["b_b"]], dx5)
    dgg = matmul("d_gg", dpab, p["wab"], cb=1)
    gr["wab"] = matmul("dw_ab", gg, dpab, ca=0)
    dys, du_a, gr["s5_d"] = rowwise_bwd("s5_post_bwd", f_s5_post, [ys, u], [(D_MODEL, 0), (D_MODEL, 0)], [p["s5_d"]], dgg)
    du_b, gr["bb"], gr["cc"], gr["pw"] = s5_bwd(dys, u, p["bb"], p["cc"], p["pw"], carries, seq)
    dx4, gr["mix_norm"] = rms_bwd([du_a, du_b], x4, p["mix_norm"], dx5)
    return dx4, gr


BIG = ["ffn1_w_gate", "ffn1_w_up", "ffn1_w_down", "ffn2_w_gate", "ffn2_w_up", "ffn2_w_down", "hy_w_in", "hy_w_out", "s5_w_a", "s5_w_b"]
SMALL_SHARDED = {"ssd_conv_w": 2, "ml_conv_w": 2, "ml_w_q": 1, "ml_w_k": 1, "ml_w_v": 1, "ml_w_if": 1, "s5_d": 1, "s5_b_a": 1, "s5_b_b": 1}
WEIGHTS = ["ffn1_norm", "ffn1_w_gate", "ffn1_w_up", "ffn1_w_down", "mix_norm", "ffn2_norm", "ffn2_w_gate", "ffn2_w_up", "ffn2_w_down",
           "hy_w_in", "ssd_conv_w", "ssd_conv_b", "ssd_dt_bias", "ssd_a_log", "ssd_d", "ssd_norm_w", "ml_conv_w", "ml_conv_b",
           "ml_w_q", "ml_w_k", "ml_w_v", "ml_w_if", "ml_b_if", "ml_norm_w", "ml_skip", "hy_w_out", "s5_a_re", "s5_a_im",
           "s5_log_step", "s5_b_re", "s5_b_im", "s5_c_re", "s5_c_im", "s5_d", "s5_w_a", "s5_b_a", "s5_w_b", "s5_b_b", "final_norm"]
S5_PARAMS = ["s5_a_re", "s5_a_im", "s5_log_step", "s5_b_re", "s5_b_im", "s5_c_re", "s5_c_im"]
SMALL_S5 = S5_PARAMS + ["s5_d", "s5_b_a", "s5_b_b"]
SMALL_REST = [n for n in WEIGHTS if n not in BIG and n not in SMALL_S5]
SMALL = SMALL_REST + SMALL_S5


def _unshard(g, axis):
    return jnp.concatenate([g[i] for i in range(N_DEV)], axis=axis)


def assemble_hybrid(gw, rep):
    padn = lambda w: jnp.pad(w, ((0, 0), (0, LANES - w.shape[1]))).astype(bf16)
    wif = _unshard(gw["ml_w_if"], 1)[0]
    wo = _unshard(gw["hy_w_out"], 1)[0].astype(bf16)
    dense = lambda n: headwise_dense(_unshard(gw[n], 1)[0].astype(f32)).astype(bf16)
    w0 = dict(mix_norm=rep["mix_norm"][0:1],
              win=win_to_padded(_unshard(gw["hy_w_in"], 2)[0]).astype(bf16),
              ssd_conv_w=_unshard(gw["ssd_conv_w"], 2)[0], ssd_conv_b=rep["ssd_conv_b"],
              a_log=_lanes(rep["ssd_a_log"]), dt_bias=_lanes(rep["ssd_dt_bias"]), ssd_d=_lanes(rep["ssd_d"]),
              ssd_norm_w=rep["ssd_norm_w"], ml_conv_w=_unshard(gw["ml_conv_w"], 2)[0], ml_conv_b=rep["ml_conv_b"],
              wq=dense("ml_w_q"), wk=dense("ml_w_k"), wv=dense("ml_w_v"),
              wif_q=padn(wif[0:1024]), wif_k=padn(wif[1024:2048]), wif_v=padn(wif[2048:3072]),
              b_if=_lanes(rep["ml_b_if"]), ml_norm_w=rep["ml_norm_w"], ml_skip=rep["ml_skip"],
              wo1=wo[:D_MODEL], wo2=wo[D_MODEL:])
    return w0


def assemble_s5(gw, rep):
    bb, cc, pw = s5_tables(*[rep[n][0] for n in S5_PARAMS])
    wab = jnp.concatenate([_unshard(gw["s5_w_a"], 1)[0], _unshard(gw["s5_w_b"], 1)[0]], axis=1).astype(bf16)
    return dict(mix_norm=rep["mix_norm"][1:2], bb=bb, cc=cc, pw=pw,
                s5_d=_unshard(gw["s5_d"], 1), wab=wab, b_a=_unshard(gw["s5_b_a"], 1), b_b=_unshard(gw["s5_b_b"], 1))


def _shards(full, axis):
    return jnp.stack(jnp.split(full, N_DEV, axis=axis), axis=0)


def small_grads(g_norms, g_hy, g_s5, d_final, rep):
    small = dict(g_norms)
    small["mix_norm"] = jnp.concatenate([g_hy["mix_norm"], g_s5["mix_norm"]], axis=0)
    small["ssd_conv_w"] = g_hy["ssd_conv_w"][None]
    small["ssd_conv_b"] = g_hy["ssd_conv_b"]
    small["ssd_dt_bias"] = g_hy["dt_bias"][:, :SSD_HEADS]
    small["ssd_a_log"] = g_hy["a_log"][:, :SSD_HEADS]
    small["ssd_d"] = g_hy["ssd_d"][:, :SSD_HEADS]
    small["ssd_norm_w"] = g_hy["ssd_norm_w"]
    small["ml_conv_w"] = g_hy["ml_conv_w"][None]
    small["ml_conv_b"] = g_hy["ml_conv_b"]
    for nm, key in (("ml_w_q", "wq"), ("ml_w_k", "wk"), ("ml_w_v", "wv")):
        small[nm] = headwise_from_dense("diag_" + key, g_hy[key])[None]
    small["ml_w_if"] = g_hy["wif"][None, :, :2 * ML_HEADS]
    small["ml_b_if"] = g_hy["b_if"][:, :2 * ML_HEADS]
    small["ml_norm_w"] = g_hy["ml_norm_w"]
    small["ml_skip"] = g_hy["ml_skip"]
    small["final_norm"] = d_final.reshape(-1)
    return small


def s5_small_grads(g_s5, rep):
    small = {}
    _, tvjp = jax.vjp(s5_tables, *[rep[n][0] for n in S5_PARAMS])
    for n, g in zip(S5_PARAMS, tvjp((g_s5["bb"], g_s5["cc"], g_s5["pw"]))):
        small[n] = g[None]
    small["s5_d"] = g_s5["s5_d"]
    small["s5_b_a"] = g_s5["b_a"]
    small["s5_b_b"] = g_s5["b_b"]
    return small


ROW = 1024
F32_ROWS = 8


def _piece_rows(size):
    return -(-size // (ROW * F32_ROWS)) * F32_ROWS


def _pack(arrays):
    pieces = []
    for a in arrays:
        flat = a.astype(f32).reshape(-1)
        pieces.append(jnp.pad(flat, (0, _piece_rows(a.size) * ROW - a.size)).reshape(-1, ROW))
    return jnp.concatenate(pieces, axis=0)


def _unpack(buf, shapes):
    out, r0 = [], 0
    lead = buf.shape[:-2]
    for shp in shapes:
        size = math.prod(shp)
        r = _piece_rows(size)
        out.append(buf[..., r0:r0 + r, :].reshape(lead + (-1,))[..., :size].reshape(lead + tuple(shp)))
        r0 += r
    return out


ADAM_BLOCK_ELEMS = 500_000


def _tile_rows(r, c):
    cap = ADAM_BLOCK_ELEMS // (-(-c // LANES) * LANES)
    if r <= cap:
        return r
    return max(t for t in range(F32_ROWS, cap + 1, F32_ROWS) if r % t == 0)


def _flat2d(a):
    return a.reshape(-1, a.shape[-1])


def kernel(x, ffn1_norm, ffn1_w_gate, ffn1_w_up, ffn1_w_down, mix_norm, ffn2_norm, ffn2_w_gate, ffn2_w_up, ffn2_w_down, hy_w_in, ssd_conv_w, ssd_conv_b, ssd_dt_bias, ssd_a_log, ssd_d, ssd_norm_w, ml_conv_w, ml_conv_b, ml_w_q, ml_w_k, ml_w_v, ml_w_if, ml_b_if, ml_norm_w, ml_skip, hy_w_out, s5_a_re, s5_a_im, s5_log_step, s5_b_re, s5_b_im, s5_c_re, s5_c_im, s5_d, s5_w_a, s5_b_a, s5_w_b, s5_b_b, final_norm, loss_target, m_ffn1_norm, m_ffn1_w_gate, m_ffn1_w_up, m_ffn1_w_down, m_mix_norm, m_ffn2_norm, m_ffn2_w_gate, m_ffn2_w_up, m_ffn2_w_down, m_hy_w_in, m_ssd_conv_w, m_ssd_conv_b, m_ssd_dt_bias, m_ssd_a_log, m_ssd_d, m_ssd_norm_w, m_ml_conv_w, m_ml_conv_b, m_ml_w_q, m_ml_w_k, m_ml_w_v, m_ml_w_if, m_ml_b_if, m_ml_norm_w, m_ml_skip, m_hy_w_out, m_s5_a_re, m_s5_a_im, m_s5_log_step, m_s5_b_re, m_s5_b_im, m_s5_c_re, m_s5_c_im, m_s5_d, m_s5_w_a, m_s5_b_a, m_s5_w_b, m_s5_b_b, m_final_norm, v_ffn1_norm, v_ffn1_w_gate, v_ffn1_w_up, v_ffn1_w_down, v_mix_norm, v_ffn2_norm, v_ffn2_w_gate, v_ffn2_w_up, v_ffn2_w_down, v_hy_w_in, v_ssd_conv_w, v_ssd_conv_b, v_ssd_dt_bias, v_ssd_a_log, v_ssd_d, v_ssd_norm_w, v_ml_conv_w, v_ml_conv_b, v_ml_w_q, v_ml_w_k, v_ml_w_v, v_ml_w_if, v_ml_b_if, v_ml_norm_w, v_ml_skip, v_hy_w_out, v_s5_a_re, v_s5_a_im, v_s5_log_step, v_s5_b_re, v_s5_b_im, v_s5_c_re, v_s5_c_im, v_s5_d, v_s5_w_a, v_s5_b_a, v_s5_w_b, v_s5_b_b, v_final_norm):
    given = dict(locals())
    w = {n: given[n] for n in WEIGHTS}
    mom = {n: given["m_" + n] for n in WEIGHTS}
    var = {n: given["v_" + n] for n in WEIGHTS}
    bl, seq, d = x.shape
    me = 4 * lax.axis_index("x") + 2 * lax.axis_index("y") + lax.axis_index("c")

    x0, tgt = x.reshape(bl * seq, d), loss_target.reshape(bl * seq, d)
    rep = {n: w[n] for n in WEIGHTS if n not in BIG and n not in SMALL_SHARDED}
    ffn_w = ("_w_gate", "_w_up", "_w_down")

    def ffn_gather(pre, l):
        return [(w[pre + s][l:l + 1].astype(bf16), "gather") for s in ffn_w]

    def scatter(parts):
        return [(p, "scatter") for p in parts]

    wf10 = tuple(exchange("gather_ffn1_l0", ffn_gather("ffn1", 0)))
    mixer_ops = [(w[n].astype(bf16), "gather") for n in ("hy_w_in", "hy_w_out")]
    mixer_ops.append((_pack([w[n] for n in SMALL_SHARDED]), "gather"))
    x1, *rest = ffn_fwd(x0, ffn1_norm[0:1], *wf10, ride=mixer_ops)
    sv10, got = rest[:3], rest[3:]
    gw = dict(zip(("hy_w_in", "hy_w_out"), got[:2]))
    gw.update(zip(SMALL_SHARDED, _unpack(got[2], [w[n].shape for n in SMALL_SHARDED])))
    w0 = assemble_hybrid(gw, rep)
    x2, sv_h, wf20, got, wf11 = hybrid_fwd(x1, w0, seq, ride_in=ffn_gather("ffn2", 0),
                                           ride_ssd=[(w[n].astype(bf16), "gather") for n in ("s5_w_a", "s5_w_b")],
                                           ride_ml=ffn_gather("ffn1", 1))
    gw.update(zip(("s5_w_a", "s5_w_b"), got))
    w1 = assemble_s5(gw, rep)
    x3, *rest = ffn_fwd(x2, ffn2_norm[0:1], *wf20, ride=ffn_gather("ffn2", 1))
    sv20, wf21 = rest[:3], tuple(rest[3:])
    x4, *sv11 = ffn_fwd(x3, ffn1_norm[1:2], *wf11)
    x5, sv_s = s5_layer_fwd(x4, w1, seq)
    x6, *sv21 = ffn_fwd(x5, ffn2_norm[1:2], *wf21)
    loss, dx6, d_final = loss_head(x6, final_norm.reshape(1, d), tgt)

    dx5, dn21, dw21, _, _ = ffn_step_bwd(dx6, x5, ffn2_norm[1:2], wf21, sv21)
    dx4, g_s5 = s5_layer_bwd(dx5, x4, w1, sv_s, seq)
    dwab = g_s5.pop("wab")
    s5_ops = scatter([_shards(dwab[None, :, :D_MODEL], 1).astype(bf16), _shards(dwab[None, :, D_MODEL:], 1).astype(bf16)])
    dx3, dn11, dw11, p21, p_s5 = ffn_step_bwd(dx4, x3, ffn1_norm[1:2], wf11, sv11, ride_act=scatter(dw21), ride_w=s5_ops)
    small = s5_small_grads(g_s5, rep)
    dx2, dn20, dw20, p11, (parts_s5,) = ffn_step_bwd(dx3, x2, ffn2_norm[0:1], wf20, sv20, ride_act=scatter(dw11),
                                                      ride_w=[(_pack([small[n] for n in SMALL_S5]), "gather")])
    dx1, g_hy, p20 = hybrid_bwd(dx2, x1, w0, sv_h, seq, ride_ml=scatter(dw20))
    hy_ops = scatter([_shards(win_from_padded(g_hy.pop("win"))[None], 2).astype(bf16), _shards(g_hy.pop("wo")[None], 1).astype(bf16)])
    h10, g10, u10 = sv10
    dx0, dn10, dg, du, a, dyh, *p_hy = ffn_bwd_act(dx1, x0, ffn1_norm[0:1], g10, u10, *wf10, ride=hy_ops)
    g_norms = {"ffn1_norm": jnp.concatenate([dn10, dn11], axis=0), "ffn2_norm": jnp.concatenate([dn20, dn21], axis=0)}
    small.update(small_grads(g_norms, g_hy, g_s5, d_final, rep))
    *dw10, parts_rest = ffn_bwd_w(h10, dyh, dg, du, a, ride=[(_pack([small[n] for n in SMALL_REST]), "gather")])
    p10 = exchange("reduce_tail", scatter(dw10))
    small_parts = jnp.concatenate([parts_rest, parts_s5], axis=1)
    small_sum = sum_parts("sum_small", small_parts, tr=_tile_rows(small_parts.shape[1], ROW))

    out_g, out_d, out_m, out_v = {}, {}, {}, {}
    ffn_parts = {"ffn1": (p10, p11), "ffn2": (p20, p21)}
    for pre in ("ffn1", "ffn2"):
        for k, s in enumerate(ffn_w):
            n = pre + s
            r, c = w[n].shape[1:]
            res = None
            for l in (1, 0):
                res = adamw_layer("adamw_" + n, ffn_parts[pre][l][k].reshape(N_DEV, r, c), w[n], mom[n], var[n], l, res,
                                  tr=_tile_rows(r, c))
            out_g[n], out_d[n], out_m[n], out_v[n] = res
    for n, parts in zip(("hy_w_in", "hy_w_out", "s5_w_a", "s5_w_b"), tuple(p_hy) + tuple(p_s5)):
        shp = w[n].shape
        w2 = _flat2d(w[n])
        res = adamw("adamw_" + n, parts.reshape((N_DEV,) + w2.shape), w2, _flat2d(mom[n]), _flat2d(var[n]),
                    tr=_tile_rows(*w2.shape))
        out_g[n], out_d[n], out_m[n], out_v[n] = [a.reshape(shp) for a in res]
    g_small = {}
    for n, full in zip(SMALL, _unpack(small_sum, [small[n].shape for n in SMALL])):
        if n in SMALL_SHARDED:
            ax = SMALL_SHARDED[n]
            full = lax.dynamic_slice_in_dim(full, me * w[n].shape[ax], w[n].shape[ax], axis=ax)
        g_small[n] = full
    packs = [_pack([t[n] for n in SMALL]) for t in (g_small, w, mom, var)]
    res = adamw("adamw_small", packs[0][None], packs[1], packs[2], packs[3], tr=_tile_rows(*packs[0].shape))
    for dst, a in zip((out_g, out_d, out_m, out_v), res):
        dst.update(zip(SMALL, _unpack(a, [w[n].shape for n in SMALL])))

    total = lax.psum(loss[0, 0], ("x", "y", "c"))
    return (total, dx0.reshape(bl, seq, d), *[out_g[n] for n in WEIGHTS], *[out_d[n] for n in WEIGHTS],
            *[out_m[n] for n in WEIGHTS], *[out_v[n] for n in WEIGHTS])
```

```python
import functools
import math

import jax
import jax.numpy as jnp
from jax import lax
from jax.experimental import pallas as pl
from jax.experimental.pallas import tpu as pltpu

f32 = jnp.float32
bf16 = jnp.bfloat16

N_DEV = 8
D_MODEL = 1024
EPS = 1e-6
FFN_RES = 0.5
CONV_W = 4
SSD_HEADS = 16
SSD_HEAD_DIM = 64
SSD_GROUPS = 2
SSD_STATE = 128
SSD_HG = SSD_HEADS // SSD_GROUPS
SSD_GW = SSD_HG * SSD_HEAD_DIM
CHUNK = 128
ML_HEADS = 4
ML_HD = 256
S5_GROUP = 16
S5_GROUPS = 64
S5_STATE = 64
S5_CB = 8
S5_CH = (S5_GROUPS // S5_CB) * S5_STATE
S5_TL = 256
S5_SUB = 32
LANES = 128
IN_COLS = 4624
PROJ_W = 4864
OFF_Z, OFF_MX, OFF_MZ, OFF_XBC, OFF_DT = 0, 1024, 2048, 3072, 4608
ADAM_LR, ADAM_B1, ADAM_B2, ADAM_EPS, ADAM_WD, ADAM_STEP = 0.001, 0.9, 0.999, 1e-08, 0.01, 10
NEG = -1e30
VMEM_LIMIT = 56 * 1024 * 1024


def _cp(n):
    return pltpu.CompilerParams(dimension_semantics=("arbitrary",) * n, vmem_limit_bytes=VMEM_LIMIT)


def _dg(a, b, ca, cb):
    return lax.dot_general(a.astype(bf16), b.astype(bf16), (((ca,), (cb,)), ((), ())), preferred_element_type=f32)


@functools.partial(jax.custom_vjp, nondiff_argnums=(2, 3))
def bdot(a, b, ca, cb):
    return _dg(a, b, ca, cb)


def _bdot_fwd(a, b, ca, cb):
    return _dg(a, b, ca, cb), (a, b)


def _bdot_bwd(ca, cb, res, ct):
    a, b = res
    da = _dg(ct, b, 1, 1 - cb) if ca == 1 else _dg(b, ct, 1 - cb, 1)
    db = _dg(a, ct, 1 - ca, 0) if cb == 0 else _dg(ct, a, 0, 1 - ca)
    return da, db


bdot.defvjp(_bdot_fwd, _bdot_bwd)


def _split3(z):
    hi = z.astype(bf16)
    r1 = z - hi.astype(f32)
    mid = r1.astype(bf16)
    return hi, mid, (r1 - mid.astype(f32)).astype(bf16)


def _sel(z, m, z_left, transpose_m):
    mm = m.astype(bf16)
    dn = lambda zz: lax.dot_general(zz, mm, (((1,), (1 if transpose_m else 0,)), ((), ())), preferred_element_type=f32) \
        if z_left else lax.dot_general(mm, zz, (((0 if transpose_m else 1,), (0,)), ((), ())), preferred_element_type=f32)
    hi, mid, lo = _split3(z)
    return dn(hi) + dn(mid) + dn(lo)


@functools.partial(jax.custom_vjp, nondiff_argnums=(2,))
def _seldot(z, m, z_left):
    return _sel(z, m, z_left, False)


_seldot.defvjp(lambda z, m, z_left: (_sel(z, m, z_left, False), m),
               lambda z_left, m, ct: (_sel(ct, m, z_left, True), jnp.zeros_like(m)))


def hdot(a, b, exact="b"):
    return _seldot(a, b.astype(f32), True) if exact == "b" else _seldot(b, a.astype(f32), False)


def _iota(shape, dim):
    return lax.broadcasted_iota(jnp.int32, shape, dim)


def _tri(n):
    return (_iota((n, n), 0) >= _iota((n, n), 1))


@functools.partial(jax.custom_vjp, nondiff_argnums=(1,))
def tshift(x, k):
    return jnp.where(_iota(x.shape, 0) >= k, pltpu.roll(x, k, 0), 0.0)


def _tshift_fwd(x, k):
    return tshift(x, k), None


def _tshift_bwd(k, _, ct):
    n = ct.shape[0]
    return (jnp.where(_iota(ct.shape, 0) < n - k, pltpu.roll(ct, n - k, 0), 0.0),)


tshift.defvjp(_tshift_fwd, _tshift_bwd)


def _lane_pick(a, idx):
    return jnp.sum(jnp.where(_iota(a.shape, 1) == idx, a, 0.0), axis=1, keepdims=True)


def _row_pick(a, idx):
    return jnp.sum(jnp.where(_iota(a.shape, 0) == idx, a, 0.0), axis=0, keepdims=True)


def _silu(x):
    return x * jax.nn.sigmoid(x)


def map_fwd(name, f, grid, ins, in_specs, out_shapes, out_specs):
    n_in = len(ins)

    def body(*refs):
        pids = tuple(pl.program_id(i) for i in range(len(grid)))
        outs = f(pids, *[r[...] for r in refs[:n_in]])
        for r, o in zip(refs[n_in:], outs):
            r[...] = o.astype(r.dtype)

    return pl.pallas_call(body, name=name, grid=grid, in_specs=in_specs, out_specs=out_specs,
                          out_shape=out_shapes, compiler_params=_cp(len(grid)))(*ins)


def scan_fwd(name, f, grid, slot_axis, ins, in_specs, out_shapes, out_specs, state_shapes, state_init, save_shapes, save_specs,
             ride=None):
    n_in, n_out, n_st = len(ins), len(out_shapes), len(state_shapes)
    n_slots = grid[slot_axis]
    cax = len(grid) - 1 if slot_axis != len(grid) - 1 else len(grid) - 2
    rider = Rider(ride)
    nr = rider.n

    def body(*refs):
        pids = tuple(pl.program_id(i) for i in range(len(grid)))
        in_refs, r_ins = refs[:n_in], refs[n_in:n_in + nr]
        o0 = n_in + nr
        out_refs, save_refs = refs[o0:o0 + n_out], refs[o0 + n_out:o0 + n_out + n_st]
        r_outs = refs[o0 + n_out + n_st:o0 + n_out + n_st + nr]
        st_refs = refs[o0 + n_out + n_st + nr:o0 + n_out + 2 * n_st + nr]
        sems = refs[o0 + n_out + 2 * n_st + nr:]
        rider.start(grid, r_ins, r_outs, sems)
        slot = pids[slot_axis]

        @pl.when(pids[cax] == 0)
        def _():
            for s, init in zip(st_refs, state_init):
                s[slot] = jnp.full(s.shape[1:], init, f32)

        states = tuple(s[slot] for s in st_refs)
        for sv, st in zip(save_refs, states):
            sv[...] = st.reshape(sv.shape)
        outs, new = f(pids, states, *[r[...] for r in in_refs])
        for r, o in zip(out_refs, outs):
            r[...] = o.astype(r.dtype)
        for s, v in zip(st_refs, new):
            s[slot] = v
        rider.wait(grid, r_ins, r_outs, sems)

    scratch = [pltpu.VMEM((n_slots,) + tuple(s), f32) for s in state_shapes]
    return pl.pallas_call(body, name=name, grid=grid, in_specs=list(in_specs) + rider.specs(),
                          out_specs=list(out_specs) + list(save_specs) + rider.specs(),
                          out_shape=list(out_shapes) + list(save_shapes) + rider.out_shapes(),
                          scratch_shapes=scratch + rider.scratch(), compiler_params=_cp(len(grid)))(*ins, *rider.arrays())


def scan_bwd(name, f, grid, slot_axis, ins, in_specs, saves, save_specs, cts, ct_specs, state_shapes, wrt, acc_first):
    n_in, n_st, n_ct = len(ins), len(saves), len(cts)
    n_slots = grid[slot_axis]
    cax = len(grid) - 1 if slot_axis != len(grid) - 1 else len(grid) - 2

    def body(*refs):
        pids = tuple(pl.program_id(i) for i in range(len(grid)))
        in_refs = refs[:n_in]
        save_refs = refs[n_in:n_in + n_st]
        ct_refs = refs[n_in + n_st:n_in + n_st + n_ct]
        out_refs = refs[n_in + n_st + n_ct:n_in + n_st + n_ct + len(wrt)]
        dst_refs = refs[n_in + n_st + n_ct + len(wrt):]
        slot = pids[slot_axis]

        @pl.when(pids[cax] == 0)
        def _():
            for s in dst_refs:
                s[slot] = jnp.zeros(s.shape[1:], f32)

        vals = [r[...] for r in in_refs]
        states = tuple(sv[...].reshape(shp) for sv, shp in zip(save_refs, state_shapes))
        ctv = tuple(r[...].astype(f32) for r in ct_refs)
        dnew = tuple(s[slot] for s in dst_refs)

        def g(st, *dv):
            full = list(vals)
            for i, v in zip(wrt, dv):
                full[i] = v
            outs, new = f(pids, st, *full)
            return tuple(outs), tuple(new)

        _, vjp = jax.vjp(g, states, *[vals[i] for i in wrt])
        grads = vjp((ctv, dnew))
        for s, v in zip(dst_refs, grads[0]):
            s[slot] = v
        for i, o_ref, gr in zip(wrt, out_refs, grads[1:]):
            first = acc_first.get(i)
            if first is None:
                o_ref[...] = gr.astype(o_ref.dtype)
            else:
                @pl.when(first(pids))
                def _():
                    o_ref[...] = jnp.zeros_like(o_ref)
                o_ref[...] += gr

    out_shapes = [jax.ShapeDtypeStruct(ins[i].shape, f32) for i in wrt]
    out_specs = [in_specs[i] for i in wrt]
    scratch = [pltpu.VMEM((n_slots,) + tuple(s), f32) for s in state_shapes]
    return pl.pallas_call(body, name=name, grid=grid, in_specs=list(in_specs) + list(save_specs) + list(ct_specs),
                          out_specs=out_specs, out_shape=out_shapes, scratch_shapes=scratch,
                          compiler_params=_cp(len(grid)))(*ins, *saves, *cts)


def _fit(dim, cap):
    if dim <= cap:
        return dim
    return max(t for t in range(LANES, cap + 1, LANES) if dim % t == 0)


def _matmul_tiles(m, n, kdim, ca):
    if ca == 1:
        return _fit(m, 512), _fit(n, 2432), _fit(kdim, 2432)
    return _fit(m, 1024), _fit(n, 1280), _fit(kdim, 512)


def matmul(name, a, b, ca=1, cb=0, add=None, out_dtype=f32, a_off=0, a_width=None, ride=None, tiles=None):
    rider = Rider(ride)
    nr = rider.n
    a_width = a.shape[1] if a_width is None else a_width
    kdim = b.shape[cb]
    n = b.shape[1 - cb]
    m = a.shape[0] if ca == 1 else a_width
    tm, tn, tk = tiles or _matmul_tiles(m, n, kdim, ca)
    assert m % tm == 0 and n % tn == 0 and kdim % tk == 0
    nk = kdim // tk
    if ca == 1:
        assert a_off % tk == 0 and a_width == kdim
        koff = a_off // tk
        a_spec = pl.BlockSpec((tm, tk), lambda i, j, k: (i, k + koff))
    else:
        assert a_off % tm == 0 and a.shape[0] == kdim
        ioff = a_off // tm
        a_spec = pl.BlockSpec((tk, tm), lambda i, j, k: (k, i + ioff))
    b_spec = pl.BlockSpec((tk, tn), lambda i, j, k: (k, j)) if cb == 0 else pl.BlockSpec((tn, tk), lambda i, j, k: (j, k))
    o_spec = pl.BlockSpec((tm, tn), lambda i, j, k: (i, j))
    has_add = add is not None

    n_in = 3 if has_add else 2
    grid = (m // tm, n // tn, nk)

    def body(*refs):
        a_ref, b_ref = refs[0], refs[1]
        add_ref = refs[2] if has_add else None
        r_ins, o_ref = refs[n_in:n_in + nr], refs[n_in + nr]
        r_outs, acc, sems = refs[n_in + nr + 1:n_in + 2 * nr + 1], refs[n_in + 2 * nr + 1], refs[n_in + 2 * nr + 2:]
        rider.start(grid, r_ins, r_outs, sems)
        k = pl.program_id(2)

        @pl.when(k == 0)
        def _():
            acc[...] = add_ref[...].astype(f32) if has_add else jnp.zeros_like(acc)

        acc[...] += _dg(a_ref[...], b_ref[...], ca, cb)

        @pl.when(k == nk - 1)
        def _():
            o_ref[...] = acc[...].astype(o_ref.dtype)

        rider.wait(grid, r_ins, r_outs, sems)

    ins = [a, b] + ([add] if has_add else [])
    specs = [a_spec, b_spec] + ([o_spec] if has_add else [])
    res = pl.pallas_call(body, name=name, grid=grid, in_specs=specs + rider.specs(), out_specs=[o_spec] + rider.specs(),
                         out_shape=[jax.ShapeDtypeStruct((m, n), out_dtype)] + rider.out_shapes(),
                         scratch_shapes=[pltpu.VMEM((tm, tn), f32)] + rider.scratch(), compiler_params=_cp(3))(*ins, *rider.arrays())
    return res if nr else res[0]


def f_rms(pids, x, w):
    r = lax.rsqrt(jnp.mean(x * x, axis=-1, keepdims=True) + EPS)
    return (x * r * w,)


def _row_spec(tm, width, col=0):
    return pl.BlockSpec((tm, width), lambda i: (i, col))


def _par_spec(shape):
    return pl.BlockSpec(shape, lambda *p: (0,) * len(shape))


def rms_fwd(x, w, tm=512):
    t, d = x.shape
    return map_fwd("rms_fwd", f_rms, (t // tm,), [x, w], [_row_spec(tm, d), _par_spec((1, d))],
                   [jax.ShapeDtypeStruct((t, d), f32)], [_row_spec(tm, d)])[0]


def rms_bwd(dys, x, w, dres, tm=512):
    t, d = x.shape
    n = len(dys)

    def body(*refs):
        x_ref, w_ref, dres_ref, dx_ref, dw_ref = refs[n:]
        dy = refs[0][...]
        for r in refs[1:n]:
            dy = dy + r[...]
        _, vjp = jax.vjp(lambda xx, ww: f_rms(None, xx, ww)[0], x_ref[...], w_ref[...])
        dx, dw = vjp(dy)
        dx_ref[...] = dx + dres_ref[...]

        @pl.when(pl.program_id(0) == 0)
        def _():
            dw_ref[...] = jnp.zeros_like(dw_ref)
        dw_ref[...] += dw

    return pl.pallas_call(body, name="rms_bwd", grid=(t // tm,),
                          in_specs=[_row_spec(tm, d)] * (n + 1) + [_par_spec((1, d)), _row_spec(tm, d)],
                          out_specs=[_row_spec(tm, d), _par_spec((1, d))],
                          out_shape=[jax.ShapeDtypeStruct((t, d), f32), jax.ShapeDtypeStruct((1, d), f32)],
                          compiler_params=_cp(1))(*dys, x, w, dres)


def loss_head(x, w, tgt, tm=512):
    t, d = x.shape

    def fl(xx, ww, tt):
        y = f_rms(None, xx, ww)[0]
        return 0.5 * jnp.sum(jnp.mean(jnp.square(y - tt), axis=-1, keepdims=True), axis=0, keepdims=True)

    def body(x_ref, w_ref, t_ref, loss_ref, dx_ref, dw_ref):
        val, vjp = jax.vjp(lambda xx, ww: fl(xx, ww, t_ref[...]), x_ref[...], w_ref[...])
        dx, dw = vjp(jnp.ones((1, 1), f32))
        dx_ref[...] = dx

        @pl.when(pl.program_id(0) == 0)
        def _():
            dw_ref[...] = jnp.zeros_like(dw_ref)
            loss_ref[...] = jnp.zeros_like(loss_ref)
        dw_ref[...] += dw
        loss_ref[...] += val

    return pl.pallas_call(body, name="loss_head", grid=(t // tm,),
                          in_specs=[_row_spec(tm, d), _par_spec((1, d)), _row_spec(tm, d)],
                          out_specs=[_par_spec((1, 1)), _row_spec(tm, d), _par_spec((1, d))],
                          out_shape=[jax.ShapeDtypeStruct((1, 1), f32), jax.ShapeDtypeStruct((t, d), f32),
                                     jax.ShapeDtypeStruct((1, d), f32)],
                          compiler_params=_cp(1))(x, w, tgt)


def ffn_fwd(x, nw, wg, wu, wd, tm=1024, ride=None):
    t, d = x.shape
    ns, _, _, fs = wg.shape
    rider = Rider(ride)
    nr = rider.n
    grid = (t // tm, ns)

    def body(*refs):
        x_ref, nw_ref, wg_ref, wu_ref, wd_ref = refs[:5]
        r_ins = refs[5:5 + nr]
        xo_ref, h_ref, g_ref, u_ref = refs[5 + nr:9 + nr]
        r_outs, acc, sems = refs[9 + nr:9 + 2 * nr], refs[9 + 2 * nr], refs[10 + 2 * nr:]
        rider.start(grid, r_ins, r_outs, sems)
        j = pl.program_id(1)

        @pl.when(j == 0)
        def _():
            h_ref[...] = f_rms(None, x_ref[...], nw_ref[...])[0].astype(bf16)
            acc[...] = jnp.zeros_like(acc)

        h = h_ref[...]
        g = jnp.dot(h, wg_ref[0, 0], preferred_element_type=f32)
        u = jnp.dot(h, wu_ref[0, 0], preferred_element_type=f32)
        g_ref[0] = g
        u_ref[0] = u
        acc[...] += jnp.dot((_silu(g) * u).astype(bf16), wd_ref[0, 0], preferred_element_type=f32)

        @pl.when(j == ns - 1)
        def _():
            xo_ref[...] = x_ref[...] + FFN_RES * acc[...]

        rider.wait(grid, r_ins, r_outs, sems)

    row = pl.BlockSpec((tm, d), lambda i, j: (i, 0))
    wcol = pl.BlockSpec((1, 1, d, fs), lambda i, j: (j, 0, 0, 0))
    wrow = pl.BlockSpec((1, 1, fs, d), lambda i, j: (j, 0, 0, 0))
    act = pl.BlockSpec((1, tm, fs), lambda i, j: (j, i, 0))
    return pl.pallas_call(body, name="ffn_fwd", grid=grid,
                          in_specs=[row, pl.BlockSpec((1, d), lambda i, j: (0, 0)), wcol, wcol, wrow] + rider.specs(),
                          out_specs=[row, row, act, act] + rider.specs(),
                          out_shape=[jax.ShapeDtypeStruct((t, d), f32), jax.ShapeDtypeStruct((t, d), bf16),
                                     jax.ShapeDtypeStruct((ns, t, fs), f32), jax.ShapeDtypeStruct((ns, t, fs), f32)]
                          + rider.out_shapes(),
                          scratch_shapes=[pltpu.VMEM((tm, d), f32)] + rider.scratch(),
                          compiler_params=_cp(2))(x, nw, wg, wu, wd, *rider.arrays())


def ffn_bwd_act(dy, x, nw, g, u, wg, wu, wd, tm=512, ride=None):
    t, d = x.shape
    ns, _, _, fs = wg.shape
    rider = Rider(ride)
    nr = rider.n
    grid = (t // tm, ns)

    def body(*refs):
        dy_ref, x_ref, nw_ref, g_ref, u_ref, wg_ref, wu_ref, wd_ref = refs[:8]
        r_ins = refs[8:8 + nr]
        dx_ref, dnw_ref, dg_ref, du_ref, a_ref, dyh_ref = refs[8 + nr:14 + nr]
        r_outs, acc, sems = refs[14 + nr:14 + 2 * nr], refs[14 + 2 * nr], refs[15 + 2 * nr:]
        rider.start(grid, r_ins, r_outs, sems)
        i, j = pl.program_id(0), pl.program_id(1)

        @pl.when(j == 0)
        def _():
            acc[...] = jnp.zeros_like(acc)
            dyh_ref[...] = (FFN_RES * dy_ref[...]).astype(bf16)

        dyh = dyh_ref[...]
        da = _dg(dyh, wd_ref[0, 0], 1, 1)
        gg, uu = g_ref[0], u_ref[0]
        sg = jax.nn.sigmoid(gg)
        si = gg * sg
        dgv = (da * uu * (sg * (1.0 + gg * (1.0 - sg)))).astype(bf16)
        duv = (da * si).astype(bf16)
        dg_ref[0] = dgv
        du_ref[0] = duv
        a_ref[0] = (si * uu).astype(bf16)
        acc[...] += _dg(dgv, wg_ref[0, 0], 1, 1) + _dg(duv, wu_ref[0, 0], 1, 1)

        @pl.when(j == ns - 1)
        def _():
            _, vjp = jax.vjp(lambda xx, ww: f_rms(None, xx, ww)[0], x_ref[...], nw_ref[...])
            dx, dw = vjp(acc[...])
            dx_ref[...] = dx + dy_ref[...]

            @pl.when(i == 0)
            def _():
                dnw_ref[...] = jnp.zeros_like(dnw_ref)
            dnw_ref[...] += dw

        rider.wait(grid, r_ins, r_outs, sems)

    row = pl.BlockSpec((tm, d), lambda i, j: (i, 0))
    wcol = pl.BlockSpec((1, 1, d, fs), lambda i, j: (j, 0, 0, 0))
    wrow = pl.BlockSpec((1, 1, fs, d), lambda i, j: (j, 0, 0, 0))
    act = pl.BlockSpec((1, tm, fs), lambda i, j: (j, i, 0))
    par = pl.BlockSpec((1, d), lambda i, j: (0, 0))
    return pl.pallas_call(body, name="ffn_bwd_act", grid=grid,
                          in_specs=[row, row, par, act, act, wcol, wcol, wrow] + rider.specs(),
                          out_specs=[row, par, act, act, act, row] + rider.specs(),
                          out_shape=[jax.ShapeDtypeStruct((t, d), f32), jax.ShapeDtypeStruct((1, d), f32)]
                          + [jax.ShapeDtypeStruct((ns, t, fs), bf16)] * 3 + [jax.ShapeDtypeStruct((t, d), bf16)]
                          + rider.out_shapes(),
                          scratch_shapes=[pltpu.VMEM((tm, d), f32)] + rider.scratch(),
                          compiler_params=_cp(2))(dy, x, nw, g, u, wg, wu, wd, *rider.arrays())


def ffn_bwd_w(h, dyh, dg, du, a, tk=1024, ride=None):
    t, d = h.shape
    ns, _, fs = dg.shape
    nk = t // tk
    rider = Rider(ride)
    nr = rider.n
    grid = (ns, nk)

    def body(*refs):
        h_ref, dy_ref, dg_ref, du_ref, a_ref = refs[:5]
        r_ins = refs[5:5 + nr]
        og, ou, od = refs[5 + nr:8 + nr]
        r_outs = refs[8 + nr:8 + 2 * nr]
        ag, au, ad = refs[8 + 2 * nr:11 + 2 * nr]
        sems = refs[11 + 2 * nr:]
        rider.start(grid, r_ins, r_outs, sems)
        k = pl.program_id(1)

        @pl.when(k == 0)
        def _():
            ag[...] = jnp.zeros_like(ag)
            au[...] = jnp.zeros_like(au)
            ad[...] = jnp.zeros_like(ad)

        hh = h_ref[...]
        ag[...] += _dg(hh, dg_ref[0], 0, 0)
        au[...] += _dg(hh, du_ref[0], 0, 0)
        ad[...] += _dg(a_ref[0], dy_ref[...], 0, 0)

        @pl.when(k == nk - 1)
        def _():
            og[0, 0] = ag[...].astype(og.dtype)
            ou[0, 0] = au[...].astype(ou.dtype)
            od[0, 0] = ad[...].astype(od.dtype)

        rider.wait(grid, r_ins, r_outs, sems)

    row = pl.BlockSpec((tk, d), lambda j, k: (k, 0))
    act = pl.BlockSpec((1, tk, fs), lambda j, k: (j, k, 0))
    wcol = pl.BlockSpec((1, 1, d, fs), lambda j, k: (j, 0, 0, 0))
    wrow = pl.BlockSpec((1, 1, fs, d), lambda j, k: (j, 0, 0, 0))
    return pl.pallas_call(body, name="ffn_bwd_w", grid=grid, in_specs=[row, row, act, act, act] + rider.specs(),
                          out_specs=[wcol, wcol, wrow] + rider.specs(),
                          out_shape=[jax.ShapeDtypeStruct((ns, 1, d, fs), bf16)] * 2
                          + [jax.ShapeDtypeStruct((ns, 1, fs, d), bf16)] + rider.out_shapes(),
                          scratch_shapes=[pltpu.VMEM((d, fs), f32), pltpu.VMEM((d, fs), f32), pltpu.VMEM((fs, d), f32)]
                          + rider.scratch(),
                          compiler_params=_cp(2))(h, dyh, dg, du, a, *rider.arrays())


def f_conv(pids, x, w, b):
    y = b + x * w[CONV_W - 1:CONV_W, :]
    for j in range(CONV_W - 1):
        y = y + tshift(x, CONV_W - 1 - j) * w[j:j + 1, :]
    return (_silu(y),)


def _conv_specs(seq, col0, cb):
    xs = pl.BlockSpec((seq, cb), lambda c, b: (b, col0 + c))
    ws = pl.BlockSpec((CONV_W, cb), lambda c, b: (0, c))
    bs = pl.BlockSpec((1, cb), lambda c, b: (0, c))
    ys = pl.BlockSpec((seq, cb), lambda c, b: (b, c))
    return xs, ws, bs, ys


def conv_fwd(name, src, col_off, w, b, seq, cb=256):
    t = src.shape[0]
    c = w.shape[1]
    xs, ws, bs, ys = _conv_specs(seq, col_off // cb, cb)
    return map_fwd(name, f_conv, (c // cb, t // seq), [src, w, b], [xs, ws, bs],
                   [jax.ShapeDtypeStruct((t, c), f32)], [ys])[0]


def conv_bwd(name, dy, src, col_off, w, b, seq, cb=256):
    t = src.shape[0]
    c = w.shape[1]
    xs, ws, bs, ys = _conv_specs(seq, col_off // cb, cb)

    def body(x_ref, w_ref, b_ref, dy_ref, dx_ref, dw_ref, db_ref):
        _, vjp = jax.vjp(lambda xx, ww, bb: f_conv(None, xx, ww, bb)[0], x_ref[...], w_ref[...], b_ref[...])
        dx, dw, db = vjp(dy_ref[...])
        dx_ref[...] = dx

        @pl.when(pl.program_id(1) == 0)
        def _():
            dw_ref[...] = jnp.zeros_like(dw_ref)
            db_ref[...] = jnp.zeros_like(db_ref)
        dw_ref[...] += dw
        db_ref[...] += db

    return pl.pallas_call(body, name=name, grid=(c // cb, t // seq), in_specs=[xs, ws, bs, ys], out_specs=[ys, ws, bs],
                          out_shape=[jax.ShapeDtypeStruct((t, c), f32), jax.ShapeDtypeStruct(w.shape, f32),
                                     jax.ShapeDtypeStruct(b.shape, f32)], compiler_params=_cp(2))(src, w, b, dy)


def f_ssd(pids, states, xs, dtraw, bm, cm, a_log, dt_bias, d_skip):
    g = pids[2]
    (hn,) = states
    l = xs.shape[0]
    head_of_lane = _iota((LANES, SSD_GW), 1) // SSD_HEAD_DIM + SSD_HG * g
    expand = (_iota((LANES, SSD_GW), 0) == head_of_lane).astype(f32)
    tri = _tri(l)
    dt = jax.nn.softplus(dtraw + dt_bias)
    adt = dt * (-jnp.exp(a_log))
    cs = hdot(tri, adt, exact="a")
    cst = cs.T
    cs_last = cs[l - 1:l, :]
    dt_e, cs_e, csl_e = hdot(dt, expand), hdot(cs, expand), hdot(cs_last, expand)
    xd = xs * dt_e
    gmat = bdot(cm, bm, 1, 1)
    half = _iota((l, LANES), 1) < SSD_HEAD_DIM
    blocks = []
    for pair in range(SSD_HG // 2):
        xb = xd[:, pair * LANES:(pair + 1) * LANES]
        res = []
        for sub in range(2):
            hid = SSD_HG * g + 2 * pair + sub
            col, row = _lane_pick(cs, hid), _row_pick(cst, hid)
            lm = jnp.exp(jnp.where(tri, col - row, NEG))
            res.append(bdot(gmat * lm, xb, 1, 0))
        blocks.append(jnp.where(half, res[0], res[1]))
    y = jnp.concatenate(blocks, axis=1)
    y = y + jnp.exp(cs_e) * bdot(cm, hn, 1, 0)
    y = y + hdot(d_skip, expand) * xs
    hn_new = jnp.exp(csl_e) * hn + bdot(bm, jnp.exp(csl_e - cs_e) * xd, 0, 0)
    return (y,), (hn_new,)


def _ssd_specs(seq, nch, rev):
    cc = (lambda c: nch - 1 - c) if rev else (lambda c: c)
    xs = pl.BlockSpec((CHUNK, SSD_GW), lambda b, c, g: (b * nch + cc(c), g))
    dt = pl.BlockSpec((CHUNK, LANES), lambda b, c, g: (b * nch + cc(c), OFF_DT // LANES))
    bm = pl.BlockSpec((CHUNK, SSD_STATE), lambda b, c, g: (b * nch + cc(c), 1024 // SSD_STATE + g))
    cm = pl.BlockSpec((CHUNK, SSD_STATE), lambda b, c, g: (b * nch + cc(c), 1024 // SSD_STATE + SSD_GROUPS + g))
    par = pl.BlockSpec((1, LANES), lambda b, c, g: (0, 0))
    sv = pl.BlockSpec((1, 1, SSD_STATE, SSD_GW), lambda b, c, g: (b * nch + cc(c), g, 0, 0))
    ddt = pl.BlockSpec((CHUNK, LANES), lambda b, c, g: (b * nch + cc(c), 0))
    dbc = pl.BlockSpec((CHUNK, SSD_STATE), lambda b, c, g: (b * nch + cc(c), g))
    return xs, dt, bm, cm, par, sv, ddt, dbc


def ssd_fwd(xbc, proj, a_log, dt_bias, d_skip, seq, ride=None):
    t = xbc.shape[0]
    nch = seq // CHUNK
    xs, dt, bm, cm, par, sv, _, _ = _ssd_specs(seq, nch, False)
    grid = (t // seq, nch, SSD_GROUPS)
    y, hsave, *got = scan_fwd("ssd_fwd", f_ssd, grid, 2, [xbc, proj, xbc, xbc, a_log, dt_bias, d_skip],
                              [xs, dt, bm, cm, par, par, par], [jax.ShapeDtypeStruct((t, SSD_GROUPS * SSD_GW), f32)], [xs],
                              [(SSD_STATE, SSD_GW)], [0.0],
                              [jax.ShapeDtypeStruct((t // CHUNK, SSD_GROUPS, SSD_STATE, SSD_GW), f32)], [sv], ride=ride)
    return y, hsave, got


def ssd_bwd(dy, xbc, proj, a_log, dt_bias, d_skip, hsave, seq):
    t = xbc.shape[0]
    nch = seq // CHUNK
    xs, dt, bm, cm, par, sv, ddt, dbc = _ssd_specs(seq, nch, True)
    grid = (t // seq, nch, SSD_GROUPS)

    def body(x_ref, dt_ref, b_ref, c_ref, al_ref, db_ref, ds_ref, h_ref, dy_ref,
             dxbc_x, dxbc_b, dxbc_c, ddt_ref, dal_ref, ddb_ref, dds_ref, dst):
        pids = tuple(pl.program_id(i) for i in range(3))
        slot = pids[2]

        @pl.when(pids[1] == 0)
        def _():
            dst[slot] = jnp.zeros(dst.shape[1:], f32)

        vals = [x_ref[...], dt_ref[...], b_ref[...], c_ref[...], al_ref[...], db_ref[...], ds_ref[...]]

        def gfun(st, *v):
            outs, new = f_ssd(pids, (st,), *v)
            return outs[0], new[0]

        _, vjp = jax.vjp(gfun, h_ref[0, 0], *vals)
        grads = vjp((dy_ref[...], dst[slot]))
        dst[slot] = grads[0]
        dxbc_x[...] = grads[1]
        dxbc_b[...] = grads[3]
        dxbc_c[...] = grads[4]

        @pl.when(slot == 0)
        def _():
            ddt_ref[...] = jnp.zeros_like(ddt_ref)
        ddt_ref[...] += grads[2]
        first = jnp.logical_and(jnp.logical_and(pids[0] == 0, pids[1] == 0), slot == 0)

        @pl.when(first)
        def _():
            dal_ref[...] = jnp.zeros_like(dal_ref)
            ddb_ref[...] = jnp.zeros_like(ddb_ref)
            dds_ref[...] = jnp.zeros_like(dds_ref)
        dal_ref[...] += grads[5]
        ddb_ref[...] += grads[6]
        dds_ref[...] += grads[7]

    bc_shape = jax.ShapeDtypeStruct((t, SSD_GROUPS * SSD_STATE), f32)
    par_shape = jax.ShapeDtypeStruct((1, LANES), f32)
    outs = pl.pallas_call(body, name="ssd_bwd", grid=grid, in_specs=[xs, dt, bm, cm, par, par, par, sv, xs],
                          out_specs=[xs, dbc, dbc, ddt, par, par, par],
                          out_shape=[jax.ShapeDtypeStruct((t, SSD_GROUPS * SSD_GW), f32),
                                     bc_shape, bc_shape, jax.ShapeDtypeStruct((t, LANES), f32),
                                     par_shape, par_shape, par_shape],
                          scratch_shapes=[pltpu.VMEM((SSD_GROUPS, SSD_STATE, SSD_GW), f32)],
                          compiler_params=_cp(3))(xbc, proj, xbc, xbc, a_log, dt_bias, d_skip, hsave, dy)
    return outs


def f_ssd_epi(pids, y, z, nw):
    yg = y * _silu(z)
    hw = yg.shape[1] // SSD_GROUPS
    parts = []
    for g in range(SSD_GROUPS):
        p = yg[:, g * hw:(g + 1) * hw]
        parts.append(p * lax.rsqrt(jnp.mean(p * p, axis=-1, keepdims=True) + EPS))
    return (jnp.concatenate(parts, axis=1) * nw,)


def f_ml_epi(pids, hm, xc, mz, nw, skip):
    parts = []
    for h in range(ML_HEADS):
        p = hm[:, h * ML_HD:(h + 1) * ML_HD]
        mu = jnp.mean(p, axis=-1, keepdims=True)
        var = jnp.mean(jnp.square(p - mu), axis=-1, keepdims=True)
        parts.append((p - mu) * lax.rsqrt(var + EPS))
    hn = jnp.concatenate(parts, axis=1) * nw
    return ((hn + skip * xc) * _silu(mz),)


def f_s5_post(pids, ys, u, d_skip):
    return (jax.nn.gelu(ys + d_skip * u),)


def f_glu(pids, pab, ba, bb):
    d = ba.shape[1]
    return ((pab[:, :d] + ba) * jax.nn.sigmoid(pab[:, d:] + bb),)


def f_glu_res(pids, pab, xres, ba, bb):
    return (xres + f_glu(pids, pab, ba, bb)[0],)


def rowwise_fwd(name, f, rows, row_cols, pars, out_width, tm=512):
    t = rows[0].shape[0]
    specs = [_row_spec(tm, w, c) for (w, c) in row_cols] + [_par_spec(p.shape) for p in pars]
    return map_fwd(name, f, (t // tm,), list(rows) + list(pars), specs, [jax.ShapeDtypeStruct((t, out_width), f32)],
                   [_row_spec(tm, out_width)])[0]


def rowwise_bwd(name, f, rows, row_cols, pars, dy, tm=256):
    t = rows[0].shape[0]
    n_r, n_p = len(rows), len(pars)
    specs = [_row_spec(tm, w, c) for (w, c) in row_cols] + [_par_spec(p.shape) for p in pars]
    out_w = dy.shape[1]

    def body(*refs):
        vals = [r[...] for r in refs[:n_r + n_p]]
        dy_ref = refs[n_r + n_p]
        outs = refs[n_r + n_p + 1:]
        _, vjp = jax.vjp(lambda *v: f(None, *v)[0], *vals)
        grads = vjp(dy_ref[...])
        for k in range(n_r):
            outs[k][...] = grads[k]

        @pl.when(pl.program_id(0) == 0)
        def _():
            for k in range(n_p):
                outs[n_r + k][...] = jnp.zeros_like(outs[n_r + k])
        for k in range(n_p):
            outs[n_r + k][...] += grads[n_r + k]

    out_shapes = [jax.ShapeDtypeStruct((t, w), f32) for (w, c) in row_cols] + [jax.ShapeDtypeStruct(p.shape, f32) for p in pars]
    out_specs = [_row_spec(tm, w) for (w, c) in row_cols] + [_par_spec(p.shape) for p in pars]
    return pl.pallas_call(body, name=name, grid=(t // tm,), in_specs=specs + [_row_spec(tm, out_w)], out_specs=out_specs,
                          out_shape=out_shapes, compiler_params=_cp(1))(*rows, *pars, dy)


def f_ml(pids, states, q, k, v, g1, g2, g3, b_if):
    h = pids[2]
    cst, nst, mst = states
    l = q.shape[0]
    gt = g1 + g2 + g3 + b_if
    k = k * (1.0 / math.sqrt(ML_HD))
    tri = _tri(l)
    bc_all = hdot(tri, jax.nn.log_sigmoid(gt), exact="a")
    bcum, ig = _lane_pick(bc_all, ML_HEADS + h), _lane_pick(gt, h)
    bcum_t, ig_t = _row_pick(bc_all.T, ML_HEADS + h), _row_pick(gt.T, h)
    b_last = bcum[l - 1:l, :]
    dlog = jnp.where(tri, bcum - bcum_t + ig_t, NEG)
    ws = b_last - bcum + ig
    m_prev = mst[:, 0:1]
    m_new = lax.stop_gradient(jnp.maximum(b_last + m_prev, jnp.max(ws, axis=0, keepdims=True)))
    decay = jnp.exp(b_last + m_prev - m_new)
    wts = jnp.exp(ws - m_new)
    c_new = decay * cst + bdot(wts * v, k, 0, 0)
    n_new = decay * nst + jnp.sum(wts * k, axis=0, keepdims=True)
    m_inter = bcum + m_prev
    m_t = lax.stop_gradient(jnp.maximum(jnp.max(dlog, axis=1, keepdims=True), m_inter))
    scores = bdot(q, k, 1, 1) * jnp.exp(dlog - m_t)
    inter_w = jnp.exp(m_inter - m_t)
    num = bdot(scores, v, 1, 0) + inter_w * bdot(q, cst, 1, 1)
    den = jnp.sum(scores, axis=1, keepdims=True) + inter_w * jnp.sum(q * nst, axis=1, keepdims=True)
    hout = num / jnp.maximum(jnp.abs(den), jnp.exp(-m_t))
    return (hout,), (c_new, n_new, jnp.broadcast_to(m_new, mst.shape))


def _ml_specs(nch, rev):
    cc = (lambda c: nch - 1 - c) if rev else (lambda c: c)
    hd = pl.BlockSpec((CHUNK, ML_HD), lambda b, c, h: (b * nch + cc(c), h))
    gt = pl.BlockSpec((CHUNK, LANES), lambda b, c, h: (b * nch + cc(c), 0))
    par = pl.BlockSpec((1, LANES), lambda b, c, h: (0, 0))
    sc = pl.BlockSpec((1, 1, ML_HD, ML_HD), lambda b, c, h: (b * nch + cc(c), h, 0, 0))
    sn = pl.BlockSpec((1, 1, 1, ML_HD), lambda b, c, h: (b * nch + cc(c), h, 0, 0))
    sm = pl.BlockSpec((1, 1, 1, LANES), lambda b, c, h: (b * nch + cc(c), h, 0, 0))
    return hd, gt, par, sc, sn, sm


ML_STATE_SHAPES = [(ML_HD, ML_HD), (1, ML_HD), (1, LANES)]


def ml_fwd(q, k, v, g1, g2, g3, b_if, seq, ride=None):
    t = q.shape[0]
    nch = seq // CHUNK
    hd, gt, par, sc, sn, sm = _ml_specs(nch, False)
    nc = t // CHUNK
    outs = scan_fwd("ml_fwd", f_ml, (t // seq, nch, ML_HEADS), 2, [q, k, v, g1, g2, g3, b_if],
                    [hd, hd, hd, gt, gt, gt, par], [jax.ShapeDtypeStruct((t, ML_HEADS * ML_HD), f32)], [hd],
                    ML_STATE_SHAPES, [0.0, 0.0, NEG],
                    [jax.ShapeDtypeStruct((nc, ML_HEADS, ML_HD, ML_HD), f32), jax.ShapeDtypeStruct((nc, ML_HEADS, 1, ML_HD), f32),
                     jax.ShapeDtypeStruct((nc, ML_HEADS, 1, LANES), f32)], [sc, sn, sm], ride=ride)
    return outs[0], outs[1:4], outs[4:]


def ml_bwd(dh, q, k, v, g1, g2, g3, b_if, saves, seq, ride=None):
    t = q.shape[0]
    nch = seq // CHUNK
    hd, gt, par, sc, sn, sm = _ml_specs(nch, True)
    rider = Rider(ride)
    nr = rider.n
    grid = (t // seq, nch, ML_HEADS)

    def f(pids, states, q, k, v, gsum, b_if):
        return f_ml(pids, states, q, k, v, gsum, jnp.zeros_like(gsum), jnp.zeros_like(gsum), b_if)

    def body(*refs):
        q_ref, k_ref, v_ref, g1_ref, g2_ref, g3_ref, b_ref, c_ref, n_ref, m_ref, dh_ref = refs[:11]
        r_ins = refs[11:11 + nr]
        dq_ref, dk_ref, dv_ref, dg_ref, db_ref = refs[11 + nr:16 + nr]
        r_outs = refs[16 + nr:16 + 2 * nr]
        dc_s, dn_s = refs[16 + 2 * nr:18 + 2 * nr]
        sems = refs[18 + 2 * nr:]
        rider.start(grid, r_ins, r_outs, sems)
        pids = tuple(pl.program_id(i) for i in range(3))
        slot = pids[2]

        @pl.when(pids[1] == 0)
        def _():
            dc_s[slot] = jnp.zeros(dc_s.shape[1:], f32)
            dn_s[slot] = jnp.zeros(dn_s.shape[1:], f32)

        gsum = g1_ref[...] + g2_ref[...] + g3_ref[...]
        mst = m_ref[0, 0]

        def gfun(cst, nst, qq, kk, vv, gs, bb):
            outs, new = f(pids, (cst, nst, mst), qq, kk, vv, gs, bb)
            return outs[0], new[0], new[1]

        _, vjp = jax.vjp(gfun, c_ref[0, 0], n_ref[0, 0], q_ref[...], k_ref[...], v_ref[...], gsum, b_ref[...])
        grads = vjp((dh_ref[...], dc_s[slot], dn_s[slot]))
        dc_s[slot] = grads[0]
        dn_s[slot] = grads[1]
        dq_ref[...] = grads[2]
        dk_ref[...] = grads[3]
        dv_ref[...] = grads[4]

        @pl.when(slot == 0)
        def _():
            dg_ref[...] = jnp.zeros_like(dg_ref)
        dg_ref[...] += grads[5]
        first = jnp.logical_and(jnp.logical_and(pids[0] == 0, pids[1] == 0), slot == 0)

        @pl.when(first)
        def _():
            db_ref[...] = jnp.zeros_like(db_ref)
        db_ref[...] += grads[6]
        rider.wait(grid, r_ins, r_outs, sems)

    big = jax.ShapeDtypeStruct((t, ML_HEADS * ML_HD), f32)
    return pl.pallas_call(body, name="ml_bwd", grid=grid,
                          in_specs=[hd, hd, hd, gt, gt, gt, par, sc, sn, sm, hd] + rider.specs(),
                          out_specs=[hd, hd, hd, gt, par] + rider.specs(),
                          out_shape=[big, big, big, jax.ShapeDtypeStruct((t, LANES), f32), jax.ShapeDtypeStruct((1, LANES), f32)]
                          + rider.out_shapes(),
                          scratch_shapes=[pltpu.VMEM((ML_HEADS, ML_HD, ML_HD), f32), pltpu.VMEM((ML_HEADS, 1, ML_HD), f32)]
                          + rider.scratch(),
                          compiler_params=_cp(3))(q, k, v, g1, g2, g3, b_if, *saves, dh, *rider.arrays())


def _block_prefix(z, transpose):
    n = z.shape[0]
    r, c = _iota((n, n), 0), _iota((n, n), 1)
    keep = jnp.logical_and(r // S5_SUB == c // S5_SUB, (c >= r) if transpose else (c <= r))
    m = jnp.where(keep, 1.0, 0.0).astype(bf16)
    hi = z.astype(bf16)
    lo = (z - hi.astype(f32)).astype(bf16)
    return jnp.dot(m, hi, preferred_element_type=f32) + jnp.dot(m, lo, preferred_element_type=f32)


@jax.custom_vjp
def block_prefix(z):
    return _block_prefix(z, False)


block_prefix.defvjp(lambda z: (_block_prefix(z, False), None), lambda _, ct: (_block_prefix(ct, True),))


def _cmul(a, b):
    h = b.shape[1] // 2
    ar, ai, br, bi = a[:, :h], a[:, h:], b[:, :h], b[:, h:]
    return jnp.concatenate([ar * br - ai * bi, ar * bi + ai * br], axis=1)


def f_s5(pids, states, u, bb, cc, tab):
    (carry,) = states
    tl = u.shape[0]
    nsub = tl // S5_SUB
    rep = lambda t: jnp.concatenate([t] * nsub, axis=0)
    p0, q0 = tab[S5_SUB:2 * S5_SUB], tab[2 * S5_SUB:3 * S5_SUB]
    lam, p0_last = tab[0:1], tab[2 * S5_SUB - 1:2 * S5_SUB]
    bu = bdot(u, bb, 1, 0)
    pre = block_prefix(_cmul(rep(q0), bu))
    e, entering = carry, []
    for k in range(nsub):
        le = _cmul(lam, e)
        entering.append(jnp.broadcast_to(le, (S5_SUB, le.shape[1])))
        e = _cmul(p0_last, pre[(k + 1) * S5_SUB - 1:(k + 1) * S5_SUB] + le)
    x = _cmul(rep(p0), pre + jnp.concatenate(entering, axis=0))
    y = bdot(x, cc, 1, 0)
    return (y,), (e,)


def _s5_specs(ntl, rev):
    tt = (lambda t: ntl - 1 - t) if rev else (lambda t: t)
    us = pl.BlockSpec((S5_TL, LANES), lambda c, b, t: (b * ntl + tt(t), c))
    bbs = pl.BlockSpec((1, LANES, 2 * S5_CH), lambda c, b, t: (c, 0, 0))
    ccs = pl.BlockSpec((1, 2 * S5_CH, LANES), lambda c, b, t: (c, 0, 0))
    pws = pl.BlockSpec((1, 3 * S5_SUB, 2 * S5_CH), lambda c, b, t: (c, 0, 0))
    sv = pl.BlockSpec((1, 1, 1, 2 * S5_CH), lambda c, b, t: (b * ntl + tt(t), c, 0, 0))
    return us, bbs, ccs, pws, sv


def s5_fwd(u, bb, cc, pw, seq):
    t = u.shape[0]
    ntl = seq // S5_TL
    us, bbs, ccs, pws, sv = _s5_specs(ntl, False)

    def f(pids, states, uu, b3, c3, p3):
        return f_s5(pids, states, uu, b3[0], c3[0], p3[0])

    y, carries = scan_fwd("s5_fwd", f, (S5_CB, t // seq, ntl), 0, [u, bb, cc, pw], [us, bbs, ccs, pws],
                          [jax.ShapeDtypeStruct((t, S5_CB * LANES), f32)], [us], [(1, 2 * S5_CH)], [0.0],
                          [jax.ShapeDtypeStruct((t // S5_TL, S5_CB, 1, 2 * S5_CH), f32)], [sv])
    return y, carries


def s5_bwd(dy, u, bb, cc, pw, carries, seq):
    t = u.shape[0]
    ntl = seq // S5_TL
    us, bbs, ccs, pws, sv = _s5_specs(ntl, True)

    def f(pids, states, uu, b3, c3, p3):
        return f_s5(pids, states, uu, b3[0], c3[0], p3[0])

    first = lambda pids: jnp.logical_and(pids[1] == 0, pids[2] == 0)
    return scan_bwd("s5_bwd", f, (S5_CB, t // seq, ntl), 0, [u, bb, cc, pw], [us, bbs, ccs, pws], [carries], [sv],
                    [dy], [us], [(1, 2 * S5_CH)], [0, 1, 2, 3], {1: first, 2: first, 3: first})


def _adam_math(g, w, m, v):
    m2 = ADAM_B1 * m + (1.0 - ADAM_B1) * g
    v2 = ADAM_B2 * v + (1.0 - ADAM_B2) * jnp.square(g)
    m_hat = m2 / (1.0 - ADAM_B1 ** ADAM_STEP)
    v_hat = v2 / (1.0 - ADAM_B2 ** ADAM_STEP)
    delta = -ADAM_LR * (m_hat / (jnp.sqrt(v_hat) + ADAM_EPS) + ADAM_WD * w)
    return delta, m2, v2


def adamw(name, parts, w, m, v, tr=256):
    n, r, c = parts.shape
    tr = min(tr, r)
    assert r % tr == 0

    def body(p_ref, w_ref, m_ref, v_ref, g_ref, d_ref, m2_ref, v2_ref):
        g = p_ref[0].astype(f32)
        for s in range(1, n):
            g = g + p_ref[s].astype(f32)
        d, m2, v2 = _adam_math(g, w_ref[...], m_ref[...], v_ref[...])
        g_ref[...] = g
        d_ref[...] = d
        m2_ref[...] = m2
        v2_ref[...] = v2

    ps = pl.BlockSpec((n, tr, c), lambda i: (0, i, 0))
    rs = pl.BlockSpec((tr, c), lambda i: (i, 0))
    return pl.pallas_call(body, name=name, grid=(r // tr,), in_specs=[ps, rs, rs, rs], out_specs=[rs] * 4,
                          out_shape=[jax.ShapeDtypeStruct((r, c), f32)] * 4, compiler_params=_cp(1))(parts, w, m, v)


def adamw_layer(name, parts, w, m, v, layer, prev=None, tr=256):
    n, r, c = parts.shape
    nl = w.shape[0]
    tr = min(tr, r)
    assert r % tr == 0 and w.shape[1:] == (r, c)
    n_prev = 0 if prev is None else 4

    def body(*refs):
        p_ref, w_ref, m_ref, v_ref = refs[:4]
        g_ref, d_ref, m2_ref, v2_ref = refs[4 + n_prev:]
        g = p_ref[0].astype(f32)
        for s in range(1, n):
            g = g + p_ref[s].astype(f32)
        d, m2, v2 = _adam_math(g, w_ref[0], m_ref[0], v_ref[0])
        g_ref[0] = g
        d_ref[0] = d
        m2_ref[0] = m2
        v2_ref[0] = v2

    ps = pl.BlockSpec((n, tr, c), lambda i: (0, i, 0))
    rs = pl.BlockSpec((1, tr, c), lambda i: (layer, i, 0))
    anyspec = pl.BlockSpec(memory_space=pl.ANY)
    return pl.pallas_call(body, name=name, grid=(r // tr,), in_specs=[ps, rs, rs, rs] + [anyspec] * n_prev, out_specs=[rs] * 4,
                          out_shape=[jax.ShapeDtypeStruct((nl, r, c), f32)] * 4,
                          input_output_aliases={4 + i: i for i in range(n_prev)},
                          compiler_params=_cp(1))(parts, w, m, v, *(prev or ()))


def sum_parts(name, parts, tr=256):
    n, r, c = parts.shape
    tr = min(tr, r)
    assert r % tr == 0

    def body(p_ref, o_ref):
        g = p_ref[0].astype(f32)
        for s in range(1, n):
            g = g + p_ref[s].astype(f32)
        o_ref[...] = g

    return pl.pallas_call(body, name=name, grid=(r // tr,), in_specs=[pl.BlockSpec((n, tr, c), lambda i: (0, i, 0))],
                          out_specs=pl.BlockSpec((tr, c), lambda i: (i, 0)),
                          out_shape=jax.ShapeDtypeStruct((r, c), f32), compiler_params=_cp(1))(parts)


class Rider:
    def __init__(self, ops):
        self.ops = list(ops or [])
        self.n = len(self.ops)

    def arrays(self):
        return [a for a, _ in self.ops]

    def specs(self):
        return [pl.BlockSpec(memory_space=pl.ANY)] * self.n

    def out_shapes(self):
        return [jax.ShapeDtypeStruct((N_DEV,) + tuple(a.shape) if mode == "gather" else tuple(a.shape), a.dtype)
                for a, mode in self.ops]

    def scratch(self):
        if not self.n:
            return []
        return [pltpu.SemaphoreType.DMA((self.n, N_DEV - 1)), pltpu.SemaphoreType.DMA((self.n, N_DEV - 1)),
                pltpu.SemaphoreType.DMA((self.n,))]

    def _copies(self, ins, outs, sems, with_relays=True):
        send_sems, recv_sems, loc_sems = sems
        x, y, c = lax.axis_index("x"), lax.axis_index("y"), lax.axis_index("c")
        me = 4 * x + 2 * y + c
        first, crossing, relays = [], [], []

        def remote(src, dst, k, idx, peer):
            return pltpu.make_async_remote_copy(src_ref=src, dst_ref=dst, send_sem=send_sems.at[k, idx], recv_sem=recv_sems.at[k, idx],
                                                device_id=peer, device_id_type=pl.DeviceIdType.MESH)

        for k, (_, mode) in enumerate(self.ops):
            src_me = ins[k] if mode == "gather" else ins[k].at[me]
            first.append(pltpu.make_async_copy(src_me, outs[k].at[me], loc_sems.at[k]))
            if mode == "scatter":
                for d in range(1, N_DEV):
                    px = 1 - x if (d >> 2) & 1 else x
                    py = 1 - y if (d >> 1) & 1 else y
                    pc = 1 - c if d & 1 else c
                    first.append(remote(ins[k].at[4 * px + 2 * py + pc], outs[k].at[me], k, d - 1, (px, py, pc)))
            else:
                first.append(remote(ins[k], outs[k].at[me], k, 0, (x, y, 1 - c)))
                for q in range(1, 4):
                    px = 1 - x if (q >> 1) & 1 else x
                    py = 1 - y if q & 1 else y
                    crossing.append(remote(ins[k], outs[k].at[me], k, q, (px, py, c)))
                    if with_relays:
                        block = outs[k].at[4 * px + 2 * py + c]
                        relays.append(remote(block, block, k, 3 + q, (x, y, 1 - c)))
        return first, crossing, relays

    def _start(self, ins, outs, sems):
        first, crossing, _ = self._copies(ins, outs, sems, with_relays=False)
        for cp in first + crossing:
            cp.start()

    def _finish(self, ins, outs, sems):
        first, crossing, relays = self._copies(ins, outs, sems)
        for cp, relay in zip(crossing, relays):
            cp.wait_recv()
            relay.start()
        for cp in first + relays:
            cp.wait()
        for cp in crossing:
            cp.wait_send()

    def start(self, grid, ins, outs, sems):
        if self.n:
            @pl.when(functools.reduce(jnp.logical_and, [pl.program_id(i) == 0 for i in range(len(grid))]))
            def _():
                self._start(ins, outs, sems)

    def wait(self, grid, ins, outs, sems):
        if self.n:
            @pl.when(functools.reduce(jnp.logical_and, [pl.program_id(i) == g - 1 for i, g in enumerate(grid)]))
            def _():
                self._finish(ins, outs, sems)


def exchange(name, ops):
    rider = Rider(ops)
    n = rider.n

    def body(*refs):
        rider._start(refs[:n], refs[n:2 * n], refs[2 * n:])
        rider._finish(refs[:n], refs[n:2 * n], refs[2 * n:])

    return pl.pallas_call(body, name=name, in_specs=rider.specs(), out_specs=rider.specs(), out_shape=rider.out_shapes(),
                          scratch_shapes=rider.scratch())(*rider.arrays())


def _lanes(v, width=LANES):
    v = v.reshape(1, -1)
    return jnp.pad(v, ((0, 0), (0, width - v.shape[1])))


def win_to_padded(w):
    return jnp.concatenate([w[:, :1024], w[:, 2576:3600], w[:, 3600:4624], w[:, 1024:2560], w[:, 2560:2576],
                            jnp.zeros((w.shape[0], PROJ_W - IN_COLS), w.dtype)], axis=1)


def win_from_padded(wp):
    return jnp.concatenate([wp[:, 0:1024], wp[:, 3072:4608], wp[:, 4608:4624], wp[:, 1024:2048], wp[:, 2048:3072]], axis=1)


def headwise_dense(w):
    nb, o, i = w.shape
    rows = jnp.tile(w.transpose(0, 2, 1).reshape(nb * i, o), (1, nb))
    same = (jnp.arange(nb * i)[:, None] // i) == (jnp.arange(nb * o)[None, :] // o)
    return jnp.where(same, rows, 0.0)


def diag_blocks(name, dd, blk, tm=256):
    n = dd.shape[0]

    def body(d_ref, o_ref):
        rows = _iota((tm, n), 0) + pl.program_id(0) * tm
        masked = jnp.where(rows // blk == _iota((tm, n), 1) // blk, d_ref[...], 0.0)
        sel = (_iota((n, LANES), 0) % blk == _iota((n, LANES), 1)).astype(f32)
        o_ref[...] = hdot(masked, sel)

    return pl.pallas_call(body, name=name, grid=(n // tm,), in_specs=[pl.BlockSpec((tm, n), lambda i: (i, 0))],
                          out_specs=pl.BlockSpec((tm, LANES), lambda i: (i, 0)),
                          out_shape=jax.ShapeDtypeStruct((n, LANES), f32), compiler_params=_cp(1))(dd)


def headwise_from_dense(name, dd, o=4, i=4):
    nb = dd.shape[0] // i
    return diag_blocks(name, dd, i)[:, :o].reshape(nb, i, o).transpose(0, 2, 1)


def s5_tables(a_re, a_im, log_step, b_re, b_im, c_re, c_im):
    step = jnp.exp(log_step)[:, None]
    j = jnp.arange(S5_SUB, dtype=f32)[:, None, None]
    expo = jnp.concatenate([j + 1.0, j, -j], axis=0)
    mag = jnp.exp(expo * (a_re * step))
    pw_re, pw_im = mag * jnp.cos(expo * (a_im * step)), mag * jnp.sin(expo * (a_im * step))
    lam_re, lam_im = pw_re[0], pw_im[0]
    den = a_re * a_re + a_im * a_im
    coef_re = ((lam_re - 1.0) * a_re + lam_im * a_im) / den
    coef_im = (lam_im * a_re - (lam_re - 1.0) * a_im) / den
    bb_re = coef_re[..., None] * b_re - coef_im[..., None] * b_im
    bb_im = coef_re[..., None] * b_im + coef_im[..., None] * b_re
    gl = S5_GROUPS // S5_CB
    eye = jnp.eye(gl, dtype=f32)

    def blk_b(t):
        t4 = t.transpose(0, 2, 1).reshape(S5_CB, gl, S5_GROUP, S5_STATE)
        return jnp.einsum("kgcn,gh->kgchn", t4, eye).reshape(S5_CB, gl * S5_GROUP, gl * S5_STATE)

    def blk_c(t):
        t4 = t.reshape(S5_CB, gl, S5_GROUP, S5_STATE)
        return jnp.einsum("kgcn,gh->kgnhc", t4, eye).reshape(S5_CB, gl * S5_STATE, gl * S5_GROUP)

    def blk_p(t):
        return t.reshape(t.shape[0], S5_CB, gl * S5_STATE).transpose(1, 0, 2)

    bb = jnp.concatenate([blk_b(bb_re), blk_b(bb_im)], axis=2)
    cc = jnp.concatenate([blk_c(c_re), -blk_c(c_im)], axis=1)
    pw = jnp.concatenate([blk_p(pw_re), blk_p(pw_im)], axis=2)
    return bb, cc, pw


def ffn_step_bwd(dy, x, nw, wts, saved, ride_act=None, ride_w=None):
    h, g, u = saved
    dx, dnw, dg, du, a, dyh, *got_act = ffn_bwd_act(dy, x, nw, g, u, *wts, ride=ride_act)
    dwg, dwu, dwd, *got_w = ffn_bwd_w(h, dyh, dg, du, a, ride=ride_w)
    return dx, dnw, (dwg, dwu, dwd), got_act, got_w


def hybrid_fwd(x1, p, seq, ride_in, ride_ssd, ride_ml):
    u = rms_fwd(x1, p["mix_norm"])
    proj, *got_in = matmul("hy_in", u, p["win"], ride=ride_in)
    xbc = conv_fwd("ssd_conv", proj, OFF_XBC, p["ssd_conv_w"], p["ssd_conv_b"], seq)
    yraw, hsave, got_ssd = ssd_fwd(xbc, proj, p["a_log"], p["dt_bias"], p["ssd_d"], seq, ride=ride_ssd)
    yssd = rowwise_fwd("ssd_epi", f_ssd_epi, [yraw, proj], [(D_MODEL, 0), (D_MODEL, OFF_Z // D_MODEL)], [p["ssd_norm_w"]], D_MODEL)
    xc = conv_fwd("ml_conv", proj, OFF_MX, p["ml_conv_w"], p["ml_conv_b"], seq)
    q = matmul("hw_q", xc, p["wq"])
    k = matmul("hw_k", xc, p["wk"])
    v = matmul("hw_v", proj, p["wv"], a_off=OFF_MX, a_width=D_MODEL)
    g1 = matmul("gate_q", q, p["wif_q"])
    g2 = matmul("gate_k", k, p["wif_k"])
    g3 = matmul("gate_v", v, p["wif_v"])
    hm, mlsave, got_ml = ml_fwd(q, k, v, g1, g2, g3, p["b_if"], seq, ride=ride_ml)
    yml = rowwise_fwd("ml_epi", f_ml_epi, [hm, xc, proj], [(D_MODEL, 0), (D_MODEL, 0), (D_MODEL, OFF_MZ // D_MODEL)],
                      [p["ml_norm_w"], p["ml_skip"]], D_MODEL)
    t = matmul("hy_out1", yssd, p["wo1"], add=x1)
    x2 = matmul("hy_out2", yml, p["wo2"], add=t)
    return x2, (u, proj, xbc, yraw, hsave, yssd, xc, q, k, v, g1, g2, g3, hm, mlsave, yml), got_in, got_ssd, got_ml


def hybrid_bwd(dx2, x1, p, saved, seq, ride_ml, ride_du):
    u, proj, xbc, yraw, hsave, yssd, xc, q, k, v, g1, g2, g3, hm, mlsave, yml = saved
    gr = {}
    dyssd = matmul("d_yssd", dx2, p["wo1"], cb=1)
    dyml = matmul("d_yml", dx2, p["wo2"], cb=1)
    gr["wo"] = jnp.concatenate([matmul("dw_o1", yssd, dx2, ca=0), matmul("dw_o2", yml, dx2, ca=0)], axis=0)
    d_hm, d_xc, d_mz, gr["ml_norm_w"], gr["ml_skip"] = rowwise_bwd(
        "ml_epi_bwd", f_ml_epi, [hm, xc, proj], [(D_MODEL, 0), (D_MODEL, 0), (D_MODEL, OFF_MZ // D_MODEL)],
        [p["ml_norm_w"], p["ml_skip"]], dyml)
    dq, dk, dv, dgt, gr["b_if"], *got_ml = ml_bwd(d_hm, q, k, v, g1, g2, g3, p["b_if"], mlsave, seq, ride=ride_ml)
    dq = matmul("dq_gate", dgt, p["wif_q"], cb=1, add=dq)
    dk = matmul("dk_gate", dgt, p["wif_k"], cb=1, add=dk)
    dv = matmul("dv_gate", dgt, p["wif_v"], cb=1, add=dv)
    gr["wif"] = jnp.concatenate([matmul("dw_if_q", q, dgt, ca=0), matmul("dw_if_k", k, dgt, ca=0),
                                 matmul("dw_if_v", v, dgt, ca=0)], axis=0)
    d_xc = matmul("dxc_q", dq, p["wq"], cb=1, add=d_xc)
    d_xc = matmul("dxc_k", dk, p["wk"], cb=1, add=d_xc)
    gr["wq"] = matmul("dw_q", xc, dq, ca=0)
    gr["wk"] = matmul("dw_k", xc, dk, ca=0)
    gr["wv"] = matmul("dw_v", proj, dv, ca=0, a_off=OFF_MX, a_width=D_MODEL)
    d_mx, gr["ml_conv_w"], gr["ml_conv_b"] = conv_bwd("ml_conv_bwd", d_xc, proj, OFF_MX, p["ml_conv_w"], p["ml_conv_b"], seq)
    d_mx = matmul("dmx_v", dv, p["wv"], cb=1, add=d_mx)
    d_yraw, d_z, gr["ssd_norm_w"] = rowwise_bwd("ssd_epi_bwd", f_ssd_epi, [yraw, proj],
                                                [(D_MODEL, 0), (D_MODEL, OFF_Z // D_MODEL)], [p["ssd_norm_w"]], dyssd)
    d_xs, d_b, d_c, d_dt, gr["a_log"], gr["dt_bias"], gr["ssd_d"] = ssd_bwd(
        d_yraw, xbc, proj, p["a_log"], p["dt_bias"], p["ssd_d"], hsave, seq)
    d_xbc, gr["ssd_conv_w"], gr["ssd_conv_b"] = conv_bwd("ssd_conv_bwd", jnp.concatenate([d_xs, d_b, d_c], axis=1), proj, OFF_XBC,
                                                         p["ssd_conv_w"], p["ssd_conv_b"], seq)
    dproj = jnp.concatenate([d_z, d_mx, d_mz, d_xbc, d_dt, jnp.zeros((d_dt.shape[0], PROJ_W - OFF_DT - LANES), f32)], axis=1)
    gr["win"] = matmul("dw_in", u.astype(bf16).T, dproj, tiles=(D_MODEL, PROJ_W // 2, min(512, u.shape[0])))
    du, *got_du = matmul("d_u", dproj, p["win"], cb=1, ride=ride_du)
    dx1, gr["mix_norm"] = rms_bwd([du], x1, p["mix_norm"], dx2)
    return dx1, gr, got_ml, got_du


def s5_layer_fwd(x4, p, seq):
    u = rms_fwd(x4, p["mix_norm"])
    ys, carries = s5_fwd(u, p["bb"], p["cc"], p["pw"], seq)
    gg = rowwise_fwd("s5_post", f_s5_post, [ys, u], [(D_MODEL, 0), (D_MODEL, 0)], [p["s5_d"]], D_MODEL)
    pab = matmul("s5_ab", gg, p["wab"])
    x5 = rowwise_fwd("s5_glu", f_glu_res, [pab, x4], [(2 * D_MODEL, 0), (D_MODEL, 0)], [p["b_a"], p["b_b"]], D_MODEL)
    return x5, (u, ys, carries, gg, pab)


def s5_layer_bwd(dx5, x4, p, saved, seq):
    u, ys, carries, gg, pab = saved
    gr = {}
    dpab, gr["b_a"], gr["b_b"] = rowwise_bwd("s5_glu_bwd", f_glu, [pab], [(2 * D_MODEL, 0)], [p["b_a"], p["b_b"]], dx5)
    dgg = matmul("d_gg", dpab, p["wab"], cb=1)
    gr["wab"] = matmul("dw_ab", gg, dpab, ca=0)
    dys, du_a, gr["s5_d"] = rowwise_bwd("s5_post_bwd", f_s5_post, [ys, u], [(D_MODEL, 0), (D_MODEL, 0)], [p["s5_d"]], dgg)
    du_b, gr["bb"], gr["cc"], gr["pw"] = s5_bwd(dys, u, p["bb"], p["cc"], p["pw"], carries, seq)
    dx4, gr["mix_norm"] = rms_bwd([du_a, du_b], x4, p["mix_norm"], dx5)
    return dx4, gr


BIG = ["ffn1_w_gate", "ffn1_w_up", "ffn1_w_down", "ffn2_w_gate", "ffn2_w_up", "ffn2_w_down", "hy_w_in", "hy_w_out", "s5_w_a", "s5_w_b"]
SMALL_SHARDED = {"ssd_conv_w": 2, "ml_conv_w": 2, "ml_w_q": 1, "ml_w_k": 1, "ml_w_v": 1, "ml_w_if": 1, "s5_d": 1, "s5_b_a": 1, "s5_b_b": 1}
WEIGHTS = ["ffn1_norm", "ffn1_w_gate", "ffn1_w_up", "ffn1_w_down", "mix_norm", "ffn2_norm", "ffn2_w_gate", "ffn2_w_up", "ffn2_w_down",
           "hy_w_in", "ssd_conv_w", "ssd_conv_b", "ssd_dt_bias", "ssd_a_log", "ssd_d", "ssd_norm_w", "ml_conv_w", "ml_conv_b",
           "ml_w_q", "ml_w_k", "ml_w_v", "ml_w_if", "ml_b_if", "ml_norm_w", "ml_skip", "hy_w_out", "s5_a_re", "s5_a_im",
           "s5_log_step", "s5_b_re", "s5_b_im", "s5_c_re", "s5_c_im", "s5_d", "s5_w_a", "s5_b_a", "s5_w_b", "s5_b_b", "final_norm"]
S5_PARAMS = ["s5_a_re", "s5_a_im", "s5_log_step", "s5_b_re", "s5_b_im", "s5_c_re", "s5_c_im"]
SMALL_S5 = S5_PARAMS + ["s5_d", "s5_b_a", "s5_b_b"]
SMALL_REST = [n for n in WEIGHTS if n not in BIG and n not in SMALL_S5]
SMALL = SMALL_REST + SMALL_S5


def _unshard(g, axis):
    return jnp.concatenate([g[i] for i in range(N_DEV)], axis=axis)


def assemble_hybrid(gw, rep):
    padn = lambda w: jnp.pad(w, ((0, 0), (0, LANES - w.shape[1]))).astype(bf16)
    wif = _unshard(gw["ml_w_if"], 1)[0]
    wo = _unshard(gw["hy_w_out"], 1)[0].astype(bf16)
    dense = lambda n: headwise_dense(_unshard(gw[n], 1)[0].astype(f32)).astype(bf16)
    w0 = dict(mix_norm=rep["mix_norm"][0:1],
              win=win_to_padded(_unshard(gw["hy_w_in"], 2)[0]).astype(bf16),
              ssd_conv_w=_unshard(gw["ssd_conv_w"], 2)[0], ssd_conv_b=rep["ssd_conv_b"],
              a_log=_lanes(rep["ssd_a_log"]), dt_bias=_lanes(rep["ssd_dt_bias"]), ssd_d=_lanes(rep["ssd_d"]),
              ssd_norm_w=rep["ssd_norm_w"], ml_conv_w=_unshard(gw["ml_conv_w"], 2)[0], ml_conv_b=rep["ml_conv_b"],
              wq=dense("ml_w_q"), wk=dense("ml_w_k"), wv=dense("ml_w_v"),
              wif_q=padn(wif[0:1024]), wif_k=padn(wif[1024:2048]), wif_v=padn(wif[2048:3072]),
              b_if=_lanes(rep["ml_b_if"]), ml_norm_w=rep["ml_norm_w"], ml_skip=rep["ml_skip"],
              wo1=wo[:D_MODEL], wo2=wo[D_MODEL:])
    return w0


def assemble_s5(gw, rep):
    bb, cc, pw = s5_tables(*[rep[n][0] for n in S5_PARAMS])
    wab = jnp.concatenate([_unshard(gw["s5_w_a"], 1)[0], _unshard(gw["s5_w_b"], 1)[0]], axis=1).astype(bf16)
    return dict(mix_norm=rep["mix_norm"][1:2], bb=bb, cc=cc, pw=pw,
                s5_d=_unshard(gw["s5_d"], 1), wab=wab, b_a=_unshard(gw["s5_b_a"], 1), b_b=_unshard(gw["s5_b_b"], 1))


def _shards(full, axis):
    return jnp.stack(jnp.split(full, N_DEV, axis=axis), axis=0)


def small_grads(g_norms, g_hy, g_s5, d_final, rep):
    small = dict(g_norms)
    small["mix_norm"] = jnp.concatenate([g_hy["mix_norm"], g_s5["mix_norm"]], axis=0)
    small["ssd_conv_w"] = g_hy["ssd_conv_w"][None]
    small["ssd_conv_b"] = g_hy["ssd_conv_b"]
    small["ssd_dt_bias"] = g_hy["dt_bias"][:, :SSD_HEADS]
    small["ssd_a_log"] = g_hy["a_log"][:, :SSD_HEADS]
    small["ssd_d"] = g_hy["ssd_d"][:, :SSD_HEADS]
    small["ssd_norm_w"] = g_hy["ssd_norm_w"]
    small["ml_conv_w"] = g_hy["ml_conv_w"][None]
    small["ml_conv_b"] = g_hy["ml_conv_b"]
    for nm, key in (("ml_w_q", "wq"), ("ml_w_k", "wk"), ("ml_w_v", "wv")):
        small[nm] = headwise_from_dense("diag_" + key, g_hy[key])[None]
    small["ml_w_if"] = g_hy["wif"][None, :, :2 * ML_HEADS]
    small["ml_b_if"] = g_hy["b_if"][:, :2 * ML_HEADS]
    small["ml_norm_w"] = g_hy["ml_norm_w"]
    small["ml_skip"] = g_hy["ml_skip"]
    small["final_norm"] = d_final.reshape(-1)
    return small


def s5_small_grads(g_s5, rep):
    small = {}
    _, tvjp = jax.vjp(s5_tables, *[rep[n][0] for n in S5_PARAMS])
    for n, g in zip(S5_PARAMS, tvjp((g_s5["bb"], g_s5["cc"], g_s5["pw"]))):
        small[n] = g[None]
    small["s5_d"] = g_s5["s5_d"]
    small["s5_b_a"] = g_s5["b_a"]
    small["s5_b_b"] = g_s5["b_b"]
    return small


ROW = 1024
F32_ROWS = 8


def _piece_rows(size):
    return -(-size // (ROW * F32_ROWS)) * F32_ROWS


def _pack(arrays):
    pieces = []
    for a in arrays:
        flat = a.astype(f32).reshape(-1)
        pieces.append(jnp.pad(flat, (0, _piece_rows(a.size) * ROW - a.size)).reshape(-1, ROW))
    return jnp.concatenate(pieces, axis=0)


def _unpack(buf, shapes):
    out, r0 = [], 0
    lead = buf.shape[:-2]
    for shp in shapes:
        size = math.prod(shp)
        r = _piece_rows(size)
        out.append(buf[..., r0:r0 + r, :].reshape(lead + (-1,))[..., :size].reshape(lead + tuple(shp)))
        r0 += r
    return out


ADAM_BLOCK_ELEMS = 500_000


def _tile_rows(r, c):
    cap = ADAM_BLOCK_ELEMS // (-(-c // LANES) * LANES)
    if r <= cap:
        return r
    return max(t for t in range(F32_ROWS, cap + 1, F32_ROWS) if r % t == 0)


def _flat2d(a):
    return a.reshape(-1, a.shape[-1])


def kernel(x, ffn1_norm, ffn1_w_gate, ffn1_w_up, ffn1_w_down, mix_norm, ffn2_norm, ffn2_w_gate, ffn2_w_up, ffn2_w_down, hy_w_in, ssd_conv_w, ssd_conv_b, ssd_dt_bias, ssd_a_log, ssd_d, ssd_norm_w, ml_conv_w, ml_conv_b, ml_w_q, ml_w_k, ml_w_v, ml_w_if, ml_b_if, ml_norm_w, ml_skip, hy_w_out, s5_a_re, s5_a_im, s5_log_step, s5_b_re, s5_b_im, s5_c_re, s5_c_im, s5_d, s5_w_a, s5_b_a, s5_w_b, s5_b_b, final_norm, loss_target, m_ffn1_norm, m_ffn1_w_gate, m_ffn1_w_up, m_ffn1_w_down, m_mix_norm, m_ffn2_norm, m_ffn2_w_gate, m_ffn2_w_up, m_ffn2_w_down, m_hy_w_in, m_ssd_conv_w, m_ssd_conv_b, m_ssd_dt_bias, m_ssd_a_log, m_ssd_d, m_ssd_norm_w, m_ml_conv_w, m_ml_conv_b, m_ml_w_q, m_ml_w_k, m_ml_w_v, m_ml_w_if, m_ml_b_if, m_ml_norm_w, m_ml_skip, m_hy_w_out, m_s5_a_re, m_s5_a_im, m_s5_log_step, m_s5_b_re, m_s5_b_im, m_s5_c_re, m_s5_c_im, m_s5_d, m_s5_w_a, m_s5_b_a, m_s5_w_b, m_s5_b_b, m_final_norm, v_ffn1_norm, v_ffn1_w_gate, v_ffn1_w_up, v_ffn1_w_down, v_mix_norm, v_ffn2_norm, v_ffn2_w_gate, v_ffn2_w_up, v_ffn2_w_down, v_hy_w_in, v_ssd_conv_w, v_ssd_conv_b, v_ssd_dt_bias, v_ssd_a_log, v_ssd_d, v_ssd_norm_w, v_ml_conv_w, v_ml_conv_b, v_ml_w_q, v_ml_w_k, v_ml_w_v, v_ml_w_if, v_ml_b_if, v_ml_norm_w, v_ml_skip, v_hy_w_out, v_s5_a_re, v_s5_a_im, v_s5_log_step, v_s5_b_re, v_s5_b_im, v_s5_c_re, v_s5_c_im, v_s5_d, v_s5_w_a, v_s5_b_a, v_s5_w_b, v_s5_b_b, v_final_norm):
    given = dict(locals())
    w = {n: given[n] for n in WEIGHTS}
    mom = {n: given["m_" + n] for n in WEIGHTS}
    var = {n: given["v_" + n] for n in WEIGHTS}
    bl, seq, d = x.shape
    me = 4 * lax.axis_index("x") + 2 * lax.axis_index("y") + lax.axis_index("c")

    x0, tgt = x.reshape(bl * seq, d), loss_target.reshape(bl * seq, d)
    rep = {n: w[n] for n in WEIGHTS if n not in BIG and n not in SMALL_SHARDED}
    ffn_w = ("_w_gate", "_w_up", "_w_down")

    def ffn_gather(pre, l):
        return [(w[pre + s][l:l + 1].astype(bf16), "gather") for s in ffn_w]

    def scatter(parts):
        return [(p, "scatter") for p in parts]

    wf10 = tuple(exchange("gather_ffn1_l0", ffn_gather("ffn1", 0)))
    mixer_ops = [(w[n].astype(bf16), "gather") for n in ("hy_w_in", "hy_w_out")]
    mixer_ops.append((_pack([w[n] for n in SMALL_SHARDED]), "gather"))
    x1, *rest = ffn_fwd(x0, ffn1_norm[0:1], *wf10, ride=mixer_ops)
    sv10, got = rest[:3], rest[3:]
    gw = dict(zip(("hy_w_in", "hy_w_out"), got[:2]))
    gw.update(zip(SMALL_SHARDED, _unpack(got[2], [w[n].shape for n in SMALL_SHARDED])))
    w0 = assemble_hybrid(gw, rep)
    x2, sv_h, wf20, got, wf11 = hybrid_fwd(x1, w0, seq, ride_in=ffn_gather("ffn2", 0),
                                           ride_ssd=[(w[n].astype(bf16), "gather") for n in ("s5_w_a", "s5_w_b")],
                                           ride_ml=ffn_gather("ffn1", 1))
    gw.update(zip(("s5_w_a", "s5_w_b"), got))
    w1 = assemble_s5(gw, rep)
    x3, *rest = ffn_fwd(x2, ffn2_norm[0:1], *wf20, ride=ffn_gather("ffn2", 1))
    sv20, wf21 = rest[:3], tuple(rest[3:])
    x4, *sv11 = ffn_fwd(x3, ffn1_norm[1:2], *wf11)
    x5, sv_s = s5_layer_fwd(x4, w1, seq)
    x6, *sv21 = ffn_fwd(x5, ffn2_norm[1:2], *wf21)
    loss, dx6, d_final = loss_head(x6, final_norm.reshape(1, d), tgt)

    dx5, dn21, dw21, _, _ = ffn_step_bwd(dx6, x5, ffn2_norm[1:2], wf21, sv21)
    dx4, g_s5 = s5_layer_bwd(dx5, x4, w1, sv_s, seq)
    dwab = g_s5.pop("wab")
    s5_ops = scatter([_shards(dwab[None, :, :D_MODEL], 1).astype(bf16), _shards(dwab[None, :, D_MODEL:], 1).astype(bf16)])
    dx3, dn11, dw11, got_a, got_w = ffn_step_bwd(dx4, x3, ffn1_norm[1:2], wf11, sv11, ride_act=scatter(dw21[:2]),
                                                 ride_w=scatter(dw21[2:]) + s5_ops)
    p21, p_s5 = got_a + got_w[:1], got_w[1:]
    small = s5_small_grads(g_s5, rep)
    dx2, dn20, dw20, got_a, got_w = ffn_step_bwd(dx3, x2, ffn2_norm[0:1], wf20, sv20, ride_act=scatter(dw11[:2]),
                                                 ride_w=scatter(dw11[2:]))
    p11 = got_a + got_w
    dx1, g_hy, p20, (parts_s5,) = hybrid_bwd(dx2, x1, w0, sv_h, seq, ride_ml=scatter(dw20),
                                             ride_du=[(_pack([small[n] for n in SMALL_S5]), "gather")])
    hy_ops = scatter([_shards(win_from_padded(g_hy.pop("win"))[None], 2).astype(bf16), _shards(g_hy.pop("wo")[None], 1).astype(bf16)])
    h10, g10, u10 = sv10
    dx0, dn10, dg, du, a, dyh, *p_hy = ffn_bwd_act(dx1, x0, ffn1_norm[0:1], g10, u10, *wf10, ride=hy_ops)
    g_norms = {"ffn1_norm": jnp.concatenate([dn10, dn11], axis=0), "ffn2_norm": jnp.concatenate([dn20, dn21], axis=0)}
    small.update(small_grads(g_norms, g_hy, g_s5, d_final, rep))
    *dw10, parts_rest = ffn_bwd_w(h10, dyh, dg, du, a, ride=[(_pack([small[n] for n in SMALL_REST]), "gather")])
    p10 = exchange("reduce_tail", scatter(dw10))
    small_parts = jnp.concatenate([parts_rest, parts_s5], axis=1)
    small_sum = sum_parts("sum_small", small_parts, tr=_tile_rows(small_parts.shape[1], ROW))

    out_g, out_d, out_m, out_v = {}, {}, {}, {}
    ffn_parts = {"ffn1": (p10, p11), "ffn2": (p20, p21)}
    for pre in ("ffn1", "ffn2"):
        for k, s in enumerate(ffn_w):
            n = pre + s
            r, c = w[n].shape[1:]
            res = None
            for l in (1, 0):
                res = adamw_layer("adamw_" + n, ffn_parts[pre][l][k].reshape(N_DEV, r, c), w[n], mom[n], var[n], l, res,
                                  tr=_tile_rows(r, c))
            out_g[n], out_d[n], out_m[n], out_v[n] = res
    for n, parts in zip(("hy_w_in", "hy_w_out", "s5_w_a", "s5_w_b"), tuple(p_hy) + tuple(p_s5)):
        shp = w[n].shape
        w2 = _flat2d(w[n])
        res = adamw("adamw_" + n, parts.reshape((N_DEV,) + w2.shape), w2, _flat2d(mom[n]), _flat2d(var[n]),
                    tr=_tile_rows(*w2.shape))
        out_g[n], out_d[n], out_m[n], out_v[n] = [a.reshape(shp) for a in res]
    g_small = {}
    for n, full in zip(SMALL, _unpack(small_sum, [small[n].shape for n in SMALL])):
        if n in SMALL_SHARDED:
            ax = SMALL_SHARDED[n]
            full = lax.dynamic_slice_in_dim(full, me * w[n].shape[ax], w[n].shape[ax], axis=ax)
        g_small[n] = full
    packs = [_pack([t[n] for n in SMALL]) for t in (g_small, w, mom, var)]
    res = adamw("adamw_small", packs[0][None], packs[1], packs[2], packs[3], tr=_tile_rows(*packs[0].shape))
    for dst, a in zip((out_g, out_d, out_m, out_v), res):
        dst.update(zip(SMALL, _unpack(a, [w[n].shape for n in SMALL])))

    total = lax.psum(loss[0, 0], ("x", "y", "c"))
    return (total, dx0.reshape(bl, seq, d), *[out_g[n] for n in WEIGHTS], *[out_d[n] for n in WEIGHTS],
            *[out_m[n] for n in WEIGHTS], *[out_v[n] for n in WEIGHTS])
```

```python
import functools
import math

import jax
import jax.numpy as jnp
from jax import lax
from jax.experimental import pallas as pl
from jax.experimental.pallas import tpu as pltpu

f32 = jnp.float32
bf16 = jnp.bfloat16

N_DEV = 8
D_MODEL = 1024
EPS = 1e-6
FFN_RES = 0.5
CONV_W = 4
SSD_HEADS = 16
SSD_HEAD_DIM = 64
SSD_GROUPS = 2
SSD_STATE = 128
SSD_HG = SSD_HEADS // SSD_GROUPS
SSD_GW = SSD_HG * SSD_HEAD_DIM
CHUNK = 128
ML_HEADS = 4
ML_HD = 256
S5_GROUP = 16
S5_GROUPS = 64
S5_STATE = 64
S5_CB = 8
S5_CH = (S5_GROUPS // S5_CB) * S5_STATE
S5_TL = 512
S5_SUB = 32
LANES = 128
IN_COLS = 4624
PROJ_W = 4864
OFF_Z, OFF_MX, OFF_MZ, OFF_XBC, OFF_DT = 0, 1024, 2048, 3072, 4608
ADAM_LR, ADAM_B1, ADAM_B2, ADAM_EPS, ADAM_WD, ADAM_STEP = 0.001, 0.9, 0.999, 1e-08, 0.01, 10
NEG = -1e30
VMEM_LIMIT = 56 * 1024 * 1024


def _cp(n):
    return pltpu.CompilerParams(dimension_semantics=("arbitrary",) * n, vmem_limit_bytes=VMEM_LIMIT)


def _dg(a, b, ca, cb):
    return lax.dot_general(a.astype(bf16), b.astype(bf16), (((ca,), (cb,)), ((), ())), preferred_element_type=f32)


@functools.partial(jax.custom_vjp, nondiff_argnums=(2, 3))
def bdot(a, b, ca, cb):
    return _dg(a, b, ca, cb)


def _bdot_fwd(a, b, ca, cb):
    return _dg(a, b, ca, cb), (a, b)


def _bdot_bwd(ca, cb, res, ct):
    a, b = res
    da = _dg(ct, b, 1, 1 - cb) if ca == 1 else _dg(b, ct, 1 - cb, 1)
    db = _dg(a, ct, 1 - ca, 0) if cb == 0 else _dg(ct, a, 0, 1 - ca)
    return da, db


bdot.defvjp(_bdot_fwd, _bdot_bwd)


def _split3(z):
    hi = z.astype(bf16)
    r1 = z - hi.astype(f32)
    mid = r1.astype(bf16)
    return hi, mid, (r1 - mid.astype(f32)).astype(bf16)


def _sel(z, m, z_left, transpose_m):
    mm = m.astype(bf16)
    dn = lambda zz: lax.dot_general(zz, mm, (((1,), (1 if transpose_m else 0,)), ((), ())), preferred_element_type=f32) \
        if z_left else lax.dot_general(mm, zz, (((0 if transpose_m else 1,), (0,)), ((), ())), preferred_element_type=f32)
    hi, mid, lo = _split3(z)
    return dn(hi) + dn(mid) + dn(lo)


@functools.partial(jax.custom_vjp, nondiff_argnums=(2,))
def _seldot(z, m, z_left):
    return _sel(z, m, z_left, False)


_seldot.defvjp(lambda z, m, z_left: (_sel(z, m, z_left, False), m),
               lambda z_left, m, ct: (_sel(ct, m, z_left, True), jnp.zeros_like(m)))


def hdot(a, b, exact="b"):
    return _seldot(a, b.astype(f32), True) if exact == "b" else _seldot(b, a.astype(f32), False)


def _iota(shape, dim):
    return lax.broadcasted_iota(jnp.int32, shape, dim)


def _tri(n):
    return (_iota((n, n), 0) >= _iota((n, n), 1))


@functools.partial(jax.custom_vjp, nondiff_argnums=(1,))
def tshift(x, k):
    return jnp.where(_iota(x.shape, 0) >= k, pltpu.roll(x, k, 0), 0.0)


def _tshift_fwd(x, k):
    return tshift(x, k), None


def _tshift_bwd(k, _, ct):
    n = ct.shape[0]
    return (jnp.where(_iota(ct.shape, 0) < n - k, pltpu.roll(ct, n - k, 0), 0.0),)


tshift.defvjp(_tshift_fwd, _tshift_bwd)


def _lane_pick(a, idx):
    return jnp.sum(jnp.where(_iota(a.shape, 1) == idx, a, 0.0), axis=1, keepdims=True)


def _row_pick(a, idx):
    return jnp.sum(jnp.where(_iota(a.shape, 0) == idx, a, 0.0), axis=0, keepdims=True)


def _silu(x):
    return x * jax.nn.sigmoid(x)


def map_fwd(name, f, grid, ins, in_specs, out_shapes, out_specs):
    n_in = len(ins)

    def body(*refs):
        pids = tuple(pl.program_id(i) for i in range(len(grid)))
        outs = f(pids, *[r[...] for r in refs[:n_in]])
        for r, o in zip(refs[n_in:], outs):
            r[...] = o.astype(r.dtype)

    return pl.pallas_call(body, name=name, grid=grid, in_specs=in_specs, out_specs=out_specs,
                          out_shape=out_shapes, compiler_params=_cp(len(grid)))(*ins)


def scan_fwd(name, f, grid, slot_axis, ins, in_specs, out_shapes, out_specs, state_shapes, state_init, save_shapes, save_specs,
             ride=None):
    n_in, n_out, n_st = len(ins), len(out_shapes), len(state_shapes)
    n_slots = grid[slot_axis]
    cax = len(grid) - 1 if slot_axis != len(grid) - 1 else len(grid) - 2
    rider = Rider(ride)
    nr = rider.n

    def body(*refs):
        pids = tuple(pl.program_id(i) for i in range(len(grid)))
        in_refs, r_ins = refs[:n_in], refs[n_in:n_in + nr]
        o0 = n_in + nr
        out_refs, save_refs = refs[o0:o0 + n_out], refs[o0 + n_out:o0 + n_out + n_st]
        r_outs = refs[o0 + n_out + n_st:o0 + n_out + n_st + nr]
        st_refs = refs[o0 + n_out + n_st + nr:o0 + n_out + 2 * n_st + nr]
        sems = refs[o0 + n_out + 2 * n_st + nr:]
        rider.start(grid, r_ins, r_outs, sems)
        slot = pids[slot_axis]

        @pl.when(pids[cax] == 0)
        def _():
            for s, init in zip(st_refs, state_init):
                s[slot] = jnp.full(s.shape[1:], init, f32)

        states = tuple(s[slot] for s in st_refs)
        for sv, st in zip(save_refs, states):
            sv[...] = st.reshape(sv.shape)
        outs, new = f(pids, states, *[r[...] for r in in_refs])
        for r, o in zip(out_refs, outs):
            r[...] = o.astype(r.dtype)
        for s, v in zip(st_refs, new):
            s[slot] = v
        rider.wait(grid, r_ins, r_outs, sems)

    scratch = [pltpu.VMEM((n_slots,) + tuple(s), f32) for s in state_shapes]
    return pl.pallas_call(body, name=name, grid=grid, in_specs=list(in_specs) + rider.specs(),
                          out_specs=list(out_specs) + list(save_specs) + rider.specs(),
                          out_shape=list(out_shapes) + list(save_shapes) + rider.out_shapes(),
                          scratch_shapes=scratch + rider.scratch(), compiler_params=_cp(len(grid)))(*ins, *rider.arrays())


def scan_bwd(name, f, grid, slot_axis, ins, in_specs, saves, save_specs, cts, ct_specs, state_shapes, wrt, acc_first):
    n_in, n_st, n_ct = len(ins), len(saves), len(cts)
    n_slots = grid[slot_axis]
    cax = len(grid) - 1 if slot_axis != len(grid) - 1 else len(grid) - 2

    def body(*refs):
        pids = tuple(pl.program_id(i) for i in range(len(grid)))
        in_refs = refs[:n_in]
        save_refs = refs[n_in:n_in + n_st]
        ct_refs = refs[n_in + n_st:n_in + n_st + n_ct]
        out_refs = refs[n_in + n_st + n_ct:n_in + n_st + n_ct + len(wrt)]
        dst_refs = refs[n_in + n_st + n_ct + len(wrt):]
        slot = pids[slot_axis]

        @pl.when(pids[cax] == 0)
        def _():
            for s in dst_refs:
                s[slot] = jnp.zeros(s.shape[1:], f32)

        vals = [r[...] for r in in_refs]
        states = tuple(sv[...].reshape(shp) for sv, shp in zip(save_refs, state_shapes))
        ctv = tuple(r[...].astype(f32) for r in ct_refs)
        dnew = tuple(s[slot] for s in dst_refs)

        def g(st, *dv):
            full = list(vals)
            for i, v in zip(wrt, dv):
                full[i] = v
            outs, new = f(pids, st, *full)
            return tuple(outs), tuple(new)

        _, vjp = jax.vjp(g, states, *[vals[i] for i in wrt])
        grads = vjp((ctv, dnew))
        for s, v in zip(dst_refs, grads[0]):
            s[slot] = v
        for i, o_ref, gr in zip(wrt, out_refs, grads[1:]):
            first = acc_first.get(i)
            if first is None:
                o_ref[...] = gr.astype(o_ref.dtype)
            else:
                @pl.when(first(pids))
                def _():
                    o_ref[...] = jnp.zeros_like(o_ref)
                o_ref[...] += gr

    out_shapes = [jax.ShapeDtypeStruct(ins[i].shape, f32) for i in wrt]
    out_specs = [in_specs[i] for i in wrt]
    scratch = [pltpu.VMEM((n_slots,) + tuple(s), f32) for s in state_shapes]
    return pl.pallas_call(body, name=name, grid=grid, in_specs=list(in_specs) + list(save_specs) + list(ct_specs),
                          out_specs=out_specs, out_shape=out_shapes, scratch_shapes=scratch,
                          compiler_params=_cp(len(grid)))(*ins, *saves, *cts)


def _fit(dim, cap):
    if dim <= cap:
        return dim
    return max(t for t in range(LANES, cap + 1, LANES) if dim % t == 0)


def _matmul_tiles(m, n, kdim, ca):
    if ca == 1:
        return _fit(m, 512), _fit(n, 2432), _fit(kdim, 2432)
    return _fit(m, 1024), _fit(n, 1280), _fit(kdim, 512)


def matmul(name, a, b, ca=1, cb=0, add=None, out_dtype=f32, a_off=0, a_width=None, ride=None, tiles=None):
    rider = Rider(ride)
    nr = rider.n
    a_width = a.shape[1] if a_width is None else a_width
    kdim = b.shape[cb]
    n = b.shape[1 - cb]
    m = a.shape[0] if ca == 1 else a_width
    tm, tn, tk = tiles or _matmul_tiles(m, n, kdim, ca)
    assert m % tm == 0 and n % tn == 0 and kdim % tk == 0
    nk = kdim // tk
    if ca == 1:
        assert a_off % tk == 0 and a_width == kdim
        koff = a_off // tk
        a_spec = pl.BlockSpec((tm, tk), lambda i, j, k: (i, k + koff))
    else:
        assert a_off % tm == 0 and a.shape[0] == kdim
        ioff = a_off // tm
        a_spec = pl.BlockSpec((tk, tm), lambda i, j, k: (k, i + ioff))
    b_spec = pl.BlockSpec((tk, tn), lambda i, j, k: (k, j)) if cb == 0 else pl.BlockSpec((tn, tk), lambda i, j, k: (j, k))
    o_spec = pl.BlockSpec((tm, tn), lambda i, j, k: (i, j))
    has_add = add is not None

    n_in = 3 if has_add else 2
    grid = (m // tm, n // tn, nk)

    def body(*refs):
        a_ref, b_ref = refs[0], refs[1]
        add_ref = refs[2] if has_add else None
        r_ins, o_ref = refs[n_in:n_in + nr], refs[n_in + nr]
        r_outs, acc, sems = refs[n_in + nr + 1:n_in + 2 * nr + 1], refs[n_in + 2 * nr + 1], refs[n_in + 2 * nr + 2:]
        rider.start(grid, r_ins, r_outs, sems)
        k = pl.program_id(2)

        @pl.when(k == 0)
        def _():
            acc[...] = add_ref[...].astype(f32) if has_add else jnp.zeros_like(acc)

        acc[...] += _dg(a_ref[...], b_ref[...], ca, cb)

        @pl.when(k == nk - 1)
        def _():
            o_ref[...] = acc[...].astype(o_ref.dtype)

        rider.wait(grid, r_ins, r_outs, sems)

    ins = [a, b] + ([add] if has_add else [])
    specs = [a_spec, b_spec] + ([o_spec] if has_add else [])
    res = pl.pallas_call(body, name=name, grid=grid, in_specs=specs + rider.specs(), out_specs=[o_spec] + rider.specs(),
                         out_shape=[jax.ShapeDtypeStruct((m, n), out_dtype)] + rider.out_shapes(),
                         scratch_shapes=[pltpu.VMEM((tm, tn), f32)] + rider.scratch(), compiler_params=_cp(3))(*ins, *rider.arrays())
    return res if nr else res[0]


def f_rms(pids, x, w):
    r = lax.rsqrt(jnp.mean(x * x, axis=-1, keepdims=True) + EPS)
    return (x * r * w,)


def _row_spec(tm, width, col=0):
    return pl.BlockSpec((tm, width), lambda i: (i, col))


def _par_spec(shape):
    return pl.BlockSpec(shape, lambda *p: (0,) * len(shape))


def rms_fwd(x, w, tm=512):
    t, d = x.shape
    return map_fwd("rms_fwd", f_rms, (t // tm,), [x, w], [_row_spec(tm, d), _par_spec((1, d))],
                   [jax.ShapeDtypeStruct((t, d), f32)], [_row_spec(tm, d)])[0]


def rms_bwd(dys, x, w, dres, tm=512):
    t, d = x.shape
    n = len(dys)

    def body(*refs):
        x_ref, w_ref, dres_ref, dx_ref, dw_ref = refs[n:]
        dy = refs[0][...]
        for r in refs[1:n]:
            dy = dy + r[...]
        _, vjp = jax.vjp(lambda xx, ww: f_rms(None, xx, ww)[0], x_ref[...], w_ref[...])
        dx, dw = vjp(dy)
        dx_ref[...] = dx + dres_ref[...]

        @pl.when(pl.program_id(0) == 0)
        def _():
            dw_ref[...] = jnp.zeros_like(dw_ref)
        dw_ref[...] += dw

    return pl.pallas_call(body, name="rms_bwd", grid=(t // tm,),
                          in_specs=[_row_spec(tm, d)] * (n + 1) + [_par_spec((1, d)), _row_spec(tm, d)],
                          out_specs=[_row_spec(tm, d), _par_spec((1, d))],
                          out_shape=[jax.ShapeDtypeStruct((t, d), f32), jax.ShapeDtypeStruct((1, d), f32)],
                          compiler_params=_cp(1))(*dys, x, w, dres)


def loss_head(x, w, tgt, tm=512):
    t, d = x.shape

    def fl(xx, ww, tt):
        y = f_rms(None, xx, ww)[0]
        return 0.5 * jnp.sum(jnp.mean(jnp.square(y - tt), axis=-1, keepdims=True), axis=0, keepdims=True)

    def body(x_ref, w_ref, t_ref, loss_ref, dx_ref, dw_ref):
        val, vjp = jax.vjp(lambda xx, ww: fl(xx, ww, t_ref[...]), x_ref[...], w_ref[...])
        dx, dw = vjp(jnp.ones((1, 1), f32))
        dx_ref[...] = dx

        @pl.when(pl.program_id(0) == 0)
        def _():
            dw_ref[...] = jnp.zeros_like(dw_ref)
            loss_ref[...] = jnp.zeros_like(loss_ref)
        dw_ref[...] += dw
        loss_ref[...] += val

    return pl.pallas_call(body, name="loss_head", grid=(t // tm,),
                          in_specs=[_row_spec(tm, d), _par_spec((1, d)), _row_spec(tm, d)],
                          out_specs=[_par_spec((1, 1)), _row_spec(tm, d), _par_spec((1, d))],
                          out_shape=[jax.ShapeDtypeStruct((1, 1), f32), jax.ShapeDtypeStruct((t, d), f32),
                                     jax.ShapeDtypeStruct((1, d), f32)],
                          compiler_params=_cp(1))(x, w, tgt)


def ffn_fwd(x, nw, wg, wu, wd, tm=1024, ride=None):
    t, d = x.shape
    ns, _, _, fs = wg.shape
    rider = Rider(ride)
    nr = rider.n
    grid = (t // tm, ns)

    def body(*refs):
        x_ref, nw_ref, wg_ref, wu_ref, wd_ref = refs[:5]
        r_ins = refs[5:5 + nr]
        xo_ref, h_ref, g_ref, u_ref = refs[5 + nr:9 + nr]
        r_outs, acc, sems = refs[9 + nr:9 + 2 * nr], refs[9 + 2 * nr], refs[10 + 2 * nr:]
        rider.start(grid, r_ins, r_outs, sems)
        j = pl.program_id(1)

        @pl.when(j == 0)
        def _():
            h_ref[...] = f_rms(None, x_ref[...], nw_ref[...])[0].astype(bf16)
            acc[...] = jnp.zeros_like(acc)

        h = h_ref[...]
        g = jnp.dot(h, wg_ref[0, 0], preferred_element_type=f32)
        u = jnp.dot(h, wu_ref[0, 0], preferred_element_type=f32)
        g_ref[0] = g
        u_ref[0] = u
        acc[...] += jnp.dot((_silu(g) * u).astype(bf16), wd_ref[0, 0], preferred_element_type=f32)

        @pl.when(j == ns - 1)
        def _():
            xo_ref[...] = x_ref[...] + FFN_RES * acc[...]

        rider.wait(grid, r_ins, r_outs, sems)

    row = pl.BlockSpec((tm, d), lambda i, j: (i, 0))
    wcol = pl.BlockSpec((1, 1, d, fs), lambda i, j: (j, 0, 0, 0))
    wrow = pl.BlockSpec((1, 1, fs, d), lambda i, j: (j, 0, 0, 0))
    act = pl.BlockSpec((1, tm, fs), lambda i, j: (j, i, 0))
    return pl.pallas_call(body, name="ffn_fwd", grid=grid,
                          in_specs=[row, pl.BlockSpec((1, d), lambda i, j: (0, 0)), wcol, wcol, wrow] + rider.specs(),
                          out_specs=[row, row, act, act] + rider.specs(),
                          out_shape=[jax.ShapeDtypeStruct((t, d), f32), jax.ShapeDtypeStruct((t, d), bf16),
                                     jax.ShapeDtypeStruct((ns, t, fs), f32), jax.ShapeDtypeStruct((ns, t, fs), f32)]
                          + rider.out_shapes(),
                          scratch_shapes=[pltpu.VMEM((tm, d), f32)] + rider.scratch(),
                          compiler_params=_cp(2))(x, nw, wg, wu, wd, *rider.arrays())


def ffn_bwd_act(dy, x, nw, g, u, wg, wu, wd, tm=512, ride=None):
    t, d = x.shape
    ns, _, _, fs = wg.shape
    rider = Rider(ride)
    nr = rider.n
    grid = (t // tm, ns)

    def body(*refs):
        dy_ref, x_ref, nw_ref, g_ref, u_ref, wg_ref, wu_ref, wd_ref = refs[:8]
        r_ins = refs[8:8 + nr]
        dx_ref, dnw_ref, dg_ref, du_ref, a_ref, dyh_ref = refs[8 + nr:14 + nr]
        r_outs, acc, sems = refs[14 + nr:14 + 2 * nr], refs[14 + 2 * nr], refs[15 + 2 * nr:]
        rider.start(grid, r_ins, r_outs, sems)
        i, j = pl.program_id(0), pl.program_id(1)

        @pl.when(j == 0)
        def _():
            acc[...] = jnp.zeros_like(acc)
            dyh_ref[...] = (FFN_RES * dy_ref[...]).astype(bf16)

        dyh = dyh_ref[...]
        da = _dg(dyh, wd_ref[0, 0], 1, 1)
        gg, uu = g_ref[0], u_ref[0]
        sg = jax.nn.sigmoid(gg)
        si = gg * sg
        dgv = (da * uu * (sg * (1.0 + gg * (1.0 - sg)))).astype(bf16)
        duv = (da * si).astype(bf16)
        dg_ref[0] = dgv
        du_ref[0] = duv
        a_ref[0] = (si * uu).astype(bf16)
        acc[...] += _dg(dgv, wg_ref[0, 0], 1, 1) + _dg(duv, wu_ref[0, 0], 1, 1)

        @pl.when(j == ns - 1)
        def _():
            _, vjp = jax.vjp(lambda xx, ww: f_rms(None, xx, ww)[0], x_ref[...], nw_ref[...])
            dx, dw = vjp(acc[...])
            dx_ref[...] = dx + dy_ref[...]

            @pl.when(i == 0)
            def _():
                dnw_ref[...] = jnp.zeros_like(dnw_ref)
            dnw_ref[...] += dw

        rider.wait(grid, r_ins, r_outs, sems)

    row = pl.BlockSpec((tm, d), lambda i, j: (i, 0))
    wcol = pl.BlockSpec((1, 1, d, fs), lambda i, j: (j, 0, 0, 0))
    wrow = pl.BlockSpec((1, 1, fs, d), lambda i, j: (j, 0, 0, 0))
    act = pl.BlockSpec((1, tm, fs), lambda i, j: (j, i, 0))
    par = pl.BlockSpec((1, d), lambda i, j: (0, 0))
    return pl.pallas_call(body, name="ffn_bwd_act", grid=grid,
                          in_specs=[row, row, par, act, act, wcol, wcol, wrow] + rider.specs(),
                          out_specs=[row, par, act, act, act, row] + rider.specs(),
                          out_shape=[jax.ShapeDtypeStruct((t, d), f32), jax.ShapeDtypeStruct((1, d), f32)]
                          + [jax.ShapeDtypeStruct((ns, t, fs), bf16)] * 3 + [jax.ShapeDtypeStruct((t, d), bf16)]
                          + rider.out_shapes(),
                          scratch_shapes=[pltpu.VMEM((tm, d), f32)] + rider.scratch(),
                          compiler_params=_cp(2))(dy, x, nw, g, u, wg, wu, wd, *rider.arrays())


def ffn_bwd_w(h, dyh, dg, du, a, tk=1024, ride=None):
    t, d = h.shape
    ns, _, fs = dg.shape
    nk = t // tk
    rider = Rider(ride)
    nr = rider.n
    grid = (ns, nk)

    def body(*refs):
        h_ref, dy_ref, dg_ref, du_ref, a_ref = refs[:5]
        r_ins = refs[5:5 + nr]
        og, ou, od = refs[5 + nr:8 + nr]
        r_outs = refs[8 + nr:8 + 2 * nr]
        ag, au, ad = refs[8 + 2 * nr:11 + 2 * nr]
        sems = refs[11 + 2 * nr:]
        rider.start(grid, r_ins, r_outs, sems)
        k = pl.program_id(1)

        @pl.when(k == 0)
        def _():
            ag[...] = jnp.zeros_like(ag)
            au[...] = jnp.zeros_like(au)
            ad[...] = jnp.zeros_like(ad)

        hh = h_ref[...]
        ag[...] += _dg(hh, dg_ref[0], 0, 0)
        au[...] += _dg(hh, du_ref[0], 0, 0)
        ad[...] += _dg(a_ref[0], dy_ref[...], 0, 0)

        @pl.when(k == nk - 1)
        def _():
            og[0, 0] = ag[...].astype(og.dtype)
            ou[0, 0] = au[...].astype(ou.dtype)
            od[0, 0] = ad[...].astype(od.dtype)

        rider.wait(grid, r_ins, r_outs, sems)

    row = pl.BlockSpec((tk, d), lambda j, k: (k, 0))
    act = pl.BlockSpec((1, tk, fs), lambda j, k: (j, k, 0))
    wcol = pl.BlockSpec((1, 1, d, fs), lambda j, k: (j, 0, 0, 0))
    wrow = pl.BlockSpec((1, 1, fs, d), lambda j, k: (j, 0, 0, 0))
    return pl.pallas_call(body, name="ffn_bwd_w", grid=grid, in_specs=[row, row, act, act, act] + rider.specs(),
                          out_specs=[wcol, wcol, wrow] + rider.specs(),
                          out_shape=[jax.ShapeDtypeStruct((ns, 1, d, fs), bf16)] * 2
                          + [jax.ShapeDtypeStruct((ns, 1, fs, d), bf16)] + rider.out_shapes(),
                          scratch_shapes=[pltpu.VMEM((d, fs), f32), pltpu.VMEM((d, fs), f32), pltpu.VMEM((fs, d), f32)]
                          + rider.scratch(),
                          compiler_params=_cp(2))(h, dyh, dg, du, a, *rider.arrays())


def f_conv(pids, x, w, b):
    y = b + x * w[CONV_W - 1:CONV_W, :]
    for j in range(CONV_W - 1):
        y = y + tshift(x, CONV_W - 1 - j) * w[j:j + 1, :]
    return (_silu(y),)


def _conv_specs(seq, col0, cb):
    xs = pl.BlockSpec((seq, cb), lambda c, b: (b, col0 + c))
    ws = pl.BlockSpec((CONV_W, cb), lambda c, b: (0, c))
    bs = pl.BlockSpec((1, cb), lambda c, b: (0, c))
    ys = pl.BlockSpec((seq, cb), lambda c, b: (b, c))
    return xs, ws, bs, ys


def conv_fwd(name, src, col_off, w, b, seq, cb=256):
    t = src.shape[0]
    c = w.shape[1]
    xs, ws, bs, ys = _conv_specs(seq, col_off // cb, cb)
    return map_fwd(name, f_conv, (c // cb, t // seq), [src, w, b], [xs, ws, bs],
                   [jax.ShapeDtypeStruct((t, c), f32)], [ys])[0]


def conv_bwd(name, dy, src, col_off, w, b, seq, cb=256):
    t = src.shape[0]
    c = w.shape[1]
    xs, ws, bs, ys = _conv_specs(seq, col_off // cb, cb)

    def body(x_ref, w_ref, b_ref, dy_ref, dx_ref, dw_ref, db_ref):
        _, vjp = jax.vjp(lambda xx, ww, bb: f_conv(None, xx, ww, bb)[0], x_ref[...], w_ref[...], b_ref[...])
        dx, dw, db = vjp(dy_ref[...])
        dx_ref[...] = dx

        @pl.when(pl.program_id(1) == 0)
        def _():
            dw_ref[...] = jnp.zeros_like(dw_ref)
            db_ref[...] = jnp.zeros_like(db_ref)
        dw_ref[...] += dw
        db_ref[...] += db

    return pl.pallas_call(body, name=name, grid=(c // cb, t // seq), in_specs=[xs, ws, bs, ys], out_specs=[ys, ws, bs],
                          out_shape=[jax.ShapeDtypeStruct((t, c), f32), jax.ShapeDtypeStruct(w.shape, f32),
                                     jax.ShapeDtypeStruct(b.shape, f32)], compiler_params=_cp(2))(src, w, b, dy)


def f_ssd(pids, states, xs, dtraw, bm, cm, a_log, dt_bias, d_skip):
    g = pids[2]
    (hn,) = states
    l = xs.shape[0]
    head_of_lane = _iota((LANES, SSD_GW), 1) // SSD_HEAD_DIM + SSD_HG * g
    expand = (_iota((LANES, SSD_GW), 0) == head_of_lane).astype(f32)
    tri = _tri(l)
    dt = jax.nn.softplus(dtraw + dt_bias)
    adt = dt * (-jnp.exp(a_log))
    cs = hdot(tri, adt, exact="a")
    cst = cs.T
    cs_last = cs[l - 1:l, :]
    dt_e, cs_e, csl_e = hdot(dt, expand), hdot(cs, expand), hdot(cs_last, expand)
    xd = xs * dt_e
    gmat = bdot(cm, bm, 1, 1)
    half = _iota((l, LANES), 1) < SSD_HEAD_DIM
    blocks = []
    for pair in range(SSD_HG // 2):
        xb = xd[:, pair * LANES:(pair + 1) * LANES]
        res = []
        for sub in range(2):
            hid = SSD_HG * g + 2 * pair + sub
            col, row = _lane_pick(cs, hid), _row_pick(cst, hid)
            lm = jnp.exp(jnp.where(tri, col - row, NEG))
            res.append(bdot(gmat * lm, xb, 1, 0))
        blocks.append(jnp.where(half, res[0], res[1]))
    y = jnp.concatenate(blocks, axis=1)
    y = y + jnp.exp(cs_e) * bdot(cm, hn, 1, 0)
    y = y + hdot(d_skip, expand) * xs
    hn_new = jnp.exp(csl_e) * hn + bdot(bm, jnp.exp(csl_e - cs_e) * xd, 0, 0)
    return (y,), (hn_new,)


def _ssd_specs(seq, nch, rev):
    cc = (lambda c: nch - 1 - c) if rev else (lambda c: c)
    xs = pl.BlockSpec((CHUNK, SSD_GW), lambda b, c, g: (b * nch + cc(c), g))
    dt = pl.BlockSpec((CHUNK, LANES), lambda b, c, g: (b * nch + cc(c), OFF_DT // LANES))
    bm = pl.BlockSpec((CHUNK, SSD_STATE), lambda b, c, g: (b * nch + cc(c), 1024 // SSD_STATE + g))
    cm = pl.BlockSpec((CHUNK, SSD_STATE), lambda b, c, g: (b * nch + cc(c), 1024 // SSD_STATE + SSD_GROUPS + g))
    par = pl.BlockSpec((1, LANES), lambda b, c, g: (0, 0))
    sv = pl.BlockSpec((1, 1, SSD_STATE, SSD_GW), lambda b, c, g: (b * nch + cc(c), g, 0, 0))
    ddt = pl.BlockSpec((CHUNK, LANES), lambda b, c, g: (b * nch + cc(c), 0))
    dbc = pl.BlockSpec((CHUNK, SSD_STATE), lambda b, c, g: (b * nch + cc(c), g))
    return xs, dt, bm, cm, par, sv, ddt, dbc


def ssd_fwd(xbc, proj, a_log, dt_bias, d_skip, seq, ride=None):
    t = xbc.shape[0]
    nch = seq // CHUNK
    xs, dt, bm, cm, par, sv, _, _ = _ssd_specs(seq, nch, False)
    grid = (t // seq, nch, SSD_GROUPS)
    y, hsave, *got = scan_fwd("ssd_fwd", f_ssd, grid, 2, [xbc, proj, xbc, xbc, a_log, dt_bias, d_skip],
                              [xs, dt, bm, cm, par, par, par], [jax.ShapeDtypeStruct((t, SSD_GROUPS * SSD_GW), f32)], [xs],
                              [(SSD_STATE, SSD_GW)], [0.0],
                              [jax.ShapeDtypeStruct((t // CHUNK, SSD_GROUPS, SSD_STATE, SSD_GW), f32)], [sv], ride=ride)
    return y, hsave, got


def ssd_bwd(dy, xbc, proj, a_log, dt_bias, d_skip, hsave, seq):
    t = xbc.shape[0]
    nch = seq // CHUNK
    xs, dt, bm, cm, par, sv, ddt, dbc = _ssd_specs(seq, nch, True)
    grid = (t // seq, nch, SSD_GROUPS)

    def body(x_ref, dt_ref, b_ref, c_ref, al_ref, db_ref, ds_ref, h_ref, dy_ref,
             dxbc_x, dxbc_b, dxbc_c, ddt_ref, dal_ref, ddb_ref, dds_ref, dst):
        pids = tuple(pl.program_id(i) for i in range(3))
        slot = pids[2]

        @pl.when(pids[1] == 0)
        def _():
            dst[slot] = jnp.zeros(dst.shape[1:], f32)

        vals = [x_ref[...], dt_ref[...], b_ref[...], c_ref[...], al_ref[...], db_ref[...], ds_ref[...]]

        def gfun(st, *v):
            outs, new = f_ssd(pids, (st,), *v)
            return outs[0], new[0]

        _, vjp = jax.vjp(gfun, h_ref[0, 0], *vals)
        grads = vjp((dy_ref[...], dst[slot]))
        dst[slot] = grads[0]
        dxbc_x[...] = grads[1]
        dxbc_b[...] = grads[3]
        dxbc_c[...] = grads[4]

        @pl.when(slot == 0)
        def _():
            ddt_ref[...] = jnp.zeros_like(ddt_ref)
        ddt_ref[...] += grads[2]
        first = jnp.logical_and(jnp.logical_and(pids[0] == 0, pids[1] == 0), slot == 0)

        @pl.when(first)
        def _():
            dal_ref[...] = jnp.zeros_like(dal_ref)
            ddb_ref[...] = jnp.zeros_like(ddb_ref)
            dds_ref[...] = jnp.zeros_like(dds_ref)
        dal_ref[...] += grads[5]
        ddb_ref[...] += grads[6]
        dds_ref[...] += grads[7]

    bc_shape = jax.ShapeDtypeStruct((t, SSD_GROUPS * SSD_STATE), f32)
    par_shape = jax.ShapeDtypeStruct((1, LANES), f32)
    outs = pl.pallas_call(body, name="ssd_bwd", grid=grid, in_specs=[xs, dt, bm, cm, par, par, par, sv, xs],
                          out_specs=[xs, dbc, dbc, ddt, par, par, par],
                          out_shape=[jax.ShapeDtypeStruct((t, SSD_GROUPS * SSD_GW), f32),
                                     bc_shape, bc_shape, jax.ShapeDtypeStruct((t, LANES), f32),
                                     par_shape, par_shape, par_shape],
                          scratch_shapes=[pltpu.VMEM((SSD_GROUPS, SSD_STATE, SSD_GW), f32)],
                          compiler_params=_cp(3))(xbc, proj, xbc, xbc, a_log, dt_bias, d_skip, hsave, dy)
    return outs


def f_ssd_epi(pids, y, z, nw):
    yg = y * _silu(z)
    hw = yg.shape[1] // SSD_GROUPS
    parts = []
    for g in range(SSD_GROUPS):
        p = yg[:, g * hw:(g + 1) * hw]
        parts.append(p * lax.rsqrt(jnp.mean(p * p, axis=-1, keepdims=True) + EPS))
    return (jnp.concatenate(parts, axis=1) * nw,)


def f_ml_epi(pids, hm, xc, mz, nw, skip):
    parts = []
    for h in range(ML_HEADS):
        p = hm[:, h * ML_HD:(h + 1) * ML_HD]
        mu = jnp.mean(p, axis=-1, keepdims=True)
        var = jnp.mean(jnp.square(p - mu), axis=-1, keepdims=True)
        parts.append((p - mu) * lax.rsqrt(var + EPS))
    hn = jnp.concatenate(parts, axis=1) * nw
    return ((hn + skip * xc) * _silu(mz),)


def f_s5_post(pids, ys, u, d_skip):
    return (jax.nn.gelu(ys + d_skip * u),)


def f_glu(pids, pab, ba, bb):
    d = ba.shape[1]
    return ((pab[:, :d] + ba) * jax.nn.sigmoid(pab[:, d:] + bb),)


def f_glu_res(pids, pab, xres, ba, bb):
    return (xres + f_glu(pids, pab, ba, bb)[0],)


def rowwise_fwd(name, f, rows, row_cols, pars, out_width, tm=512):
    t = rows[0].shape[0]
    specs = [_row_spec(tm, w, c) for (w, c) in row_cols] + [_par_spec(p.shape) for p in pars]
    return map_fwd(name, f, (t // tm,), list(rows) + list(pars), specs, [jax.ShapeDtypeStruct((t, out_width), f32)],
                   [_row_spec(tm, out_width)])[0]


def rowwise_bwd(name, f, rows, row_cols, pars, dy, tm=256):
    t = rows[0].shape[0]
    n_r, n_p = len(rows), len(pars)
    specs = [_row_spec(tm, w, c) for (w, c) in row_cols] + [_par_spec(p.shape) for p in pars]
    out_w = dy.shape[1]

    def body(*refs):
        vals = [r[...] for r in refs[:n_r + n_p]]
        dy_ref = refs[n_r + n_p]
        outs = refs[n_r + n_p + 1:]
        _, vjp = jax.vjp(lambda *v: f(None, *v)[0], *vals)
        grads = vjp(dy_ref[...])
        for k in range(n_r):
            outs[k][...] = grads[k]

        @pl.when(pl.program_id(0) == 0)
        def _():
            for k in range(n_p):
                outs[n_r + k][...] = jnp.zeros_like(outs[n_r + k])
        for k in range(n_p):
            outs[n_r + k][...] += grads[n_r + k]

    out_shapes = [jax.ShapeDtypeStruct((t, w), f32) for (w, c) in row_cols] + [jax.ShapeDtypeStruct(p.shape, f32) for p in pars]
    out_specs = [_row_spec(tm, w) for (w, c) in row_cols] + [_par_spec(p.shape) for p in pars]
    return pl.pallas_call(body, name=name, grid=(t // tm,), in_specs=specs + [_row_spec(tm, out_w)], out_specs=out_specs,
                          out_shape=out_shapes, compiler_params=_cp(1))(*rows, *pars, dy)


def f_ml(pids, states, q, k, v, g1, g2, g3, b_if):
    h = pids[2]
    cst, nst, mst = states
    l = q.shape[0]
    gt = g1 + g2 + g3 + b_if
    k = k * (1.0 / math.sqrt(ML_HD))
    tri = _tri(l)
    bc_all = hdot(tri, jax.nn.log_sigmoid(gt), exact="a")
    bcum, ig = _lane_pick(bc_all, ML_HEADS + h), _lane_pick(gt, h)
    bcum_t, ig_t = _row_pick(bc_all.T, ML_HEADS + h), _row_pick(gt.T, h)
    b_last = bcum[l - 1:l, :]
    dlog = jnp.where(tri, bcum - bcum_t + ig_t, NEG)
    ws = b_last - bcum + ig
    m_prev = mst[:, 0:1]
    m_new = lax.stop_gradient(jnp.maximum(b_last + m_prev, jnp.max(ws, axis=0, keepdims=True)))
    decay = jnp.exp(b_last + m_prev - m_new)
    wts = jnp.exp(ws - m_new)
    c_new = decay * cst + bdot(wts * v, k, 0, 0)
    n_new = decay * nst + jnp.sum(wts * k, axis=0, keepdims=True)
    m_inter = bcum + m_prev
    m_t = lax.stop_gradient(jnp.maximum(jnp.max(dlog, axis=1, keepdims=True), m_inter))
    scores = bdot(q, k, 1, 1) * jnp.exp(dlog - m_t)
    inter_w = jnp.exp(m_inter - m_t)
    num = bdot(scores, v, 1, 0) + inter_w * bdot(q, cst, 1, 1)
    den = jnp.sum(scores, axis=1, keepdims=True) + inter_w * jnp.sum(q * nst, axis=1, keepdims=True)
    hout = num / jnp.maximum(jnp.abs(den), jnp.exp(-m_t))
    return (hout,), (c_new, n_new, jnp.broadcast_to(m_new, mst.shape))


def _ml_specs(nch, rev):
    cc = (lambda c: nch - 1 - c) if rev else (lambda c: c)
    hd = pl.BlockSpec((CHUNK, ML_HD), lambda b, c, h: (b * nch + cc(c), h))
    gt = pl.BlockSpec((CHUNK, LANES), lambda b, c, h: (b * nch + cc(c), 0))
    par = pl.BlockSpec((1, LANES), lambda b, c, h: (0, 0))
    sc = pl.BlockSpec((1, 1, ML_HD, ML_HD), lambda b, c, h: (b * nch + cc(c), h, 0, 0))
    sn = pl.BlockSpec((1, 1, 1, ML_HD), lambda b, c, h: (b * nch + cc(c), h, 0, 0))
    sm = pl.BlockSpec((1, 1, 1, LANES), lambda b, c, h: (b * nch + cc(c), h, 0, 0))
    return hd, gt, par, sc, sn, sm


ML_STATE_SHAPES = [(ML_HD, ML_HD), (1, ML_HD), (1, LANES)]


def ml_fwd(q, k, v, g1, g2, g3, b_if, seq, ride=None):
    t = q.shape[0]
    nch = seq // CHUNK
    hd, gt, par, sc, sn, sm = _ml_specs(nch, False)
    nc = t // CHUNK
    outs = scan_fwd("ml_fwd", f_ml, (t // seq, nch, ML_HEADS), 2, [q, k, v, g1, g2, g3, b_if],
                    [hd, hd, hd, gt, gt, gt, par], [jax.ShapeDtypeStruct((t, ML_HEADS * ML_HD), f32)], [hd],
                    ML_STATE_SHAPES, [0.0, 0.0, NEG],
                    [jax.ShapeDtypeStruct((nc, ML_HEADS, ML_HD, ML_HD), f32), jax.ShapeDtypeStruct((nc, ML_HEADS, 1, ML_HD), f32),
                     jax.ShapeDtypeStruct((nc, ML_HEADS, 1, LANES), f32)], [sc, sn, sm], ride=ride)
    return outs[0], outs[1:4], outs[4:]


def ml_bwd(dh, q, k, v, g1, g2, g3, b_if, saves, seq, ride=None):
    t = q.shape[0]
    nch = seq // CHUNK
    hd, gt, par, sc, sn, sm = _ml_specs(nch, True)
    rider = Rider(ride)
    nr = rider.n
    grid = (t // seq, nch, ML_HEADS)

    def f(pids, states, q, k, v, gsum, b_if):
        return f_ml(pids, states, q, k, v, gsum, jnp.zeros_like(gsum), jnp.zeros_like(gsum), b_if)

    def body(*refs):
        q_ref, k_ref, v_ref, g1_ref, g2_ref, g3_ref, b_ref, c_ref, n_ref, m_ref, dh_ref = refs[:11]
        r_ins = refs[11:11 + nr]
        dq_ref, dk_ref, dv_ref, dg_ref, db_ref = refs[11 + nr:16 + nr]
        r_outs = refs[16 + nr:16 + 2 * nr]
        dc_s, dn_s = refs[16 + 2 * nr:18 + 2 * nr]
        sems = refs[18 + 2 * nr:]
        rider.start(grid, r_ins, r_outs, sems)
        pids = tuple(pl.program_id(i) for i in range(3))
        slot = pids[2]

        @pl.when(pids[1] == 0)
        def _():
            dc_s[slot] = jnp.zeros(dc_s.shape[1:], f32)
            dn_s[slot] = jnp.zeros(dn_s.shape[1:], f32)

        gsum = g1_ref[...] + g2_ref[...] + g3_ref[...]
        mst = m_ref[0, 0]

        def gfun(cst, nst, qq, kk, vv, gs, bb):
            outs, new = f(pids, (cst, nst, mst), qq, kk, vv, gs, bb)
            return outs[0], new[0], new[1]

        _, vjp = jax.vjp(gfun, c_ref[0, 0], n_ref[0, 0], q_ref[...], k_ref[...], v_ref[...], gsum, b_ref[...])
        grads = vjp((dh_ref[...], dc_s[slot], dn_s[slot]))
        dc_s[slot] = grads[0]
        dn_s[slot] = grads[1]
        dq_ref[...] = grads[2]
        dk_ref[...] = grads[3]
        dv_ref[...] = grads[4]

        @pl.when(slot == 0)
        def _():
            dg_ref[...] = jnp.zeros_like(dg_ref)
        dg_ref[...] += grads[5]
        first = jnp.logical_and(jnp.logical_and(pids[0] == 0, pids[1] == 0), slot == 0)

        @pl.when(first)
        def _():
            db_ref[...] = jnp.zeros_like(db_ref)
        db_ref[...] += grads[6]
        rider.wait(grid, r_ins, r_outs, sems)

    big = jax.ShapeDtypeStruct((t, ML_HEADS * ML_HD), f32)
    return pl.pallas_call(body, name="ml_bwd", grid=grid,
                          in_specs=[hd, hd, hd, gt, gt, gt, par, sc, sn, sm, hd] + rider.specs(),
                          out_specs=[hd, hd, hd, gt, par] + rider.specs(),
                          out_shape=[big, big, big, jax.ShapeDtypeStruct((t, LANES), f32), jax.ShapeDtypeStruct((1, LANES), f32)]
                          + rider.out_shapes(),
                          scratch_shapes=[pltpu.VMEM((ML_HEADS, ML_HD, ML_HD), f32), pltpu.VMEM((ML_HEADS, 1, ML_HD), f32)]
                          + rider.scratch(),
                          compiler_params=_cp(3))(q, k, v, g1, g2, g3, b_if, *saves, dh, *rider.arrays())


def _block_prefix(z, transpose):
    n = z.shape[0]
    r, c = _iota((n, n), 0), _iota((n, n), 1)
    keep = jnp.logical_and(r // S5_SUB == c // S5_SUB, (c >= r) if transpose else (c <= r))
    m = jnp.where(keep, 1.0, 0.0).astype(bf16)
    hi = z.astype(bf16)
    lo = (z - hi.astype(f32)).astype(bf16)
    return jnp.dot(m, hi, preferred_element_type=f32) + jnp.dot(m, lo, preferred_element_type=f32)


@jax.custom_vjp
def block_prefix(z):
    return _block_prefix(z, False)


block_prefix.defvjp(lambda z: (_block_prefix(z, False), None), lambda _, ct: (_block_prefix(ct, True),))


def _cmul(a, b):
    h = b.shape[1] // 2
    ar, ai, br, bi = a[:, :h], a[:, h:], b[:, :h], b[:, h:]
    return jnp.concatenate([ar * br - ai * bi, ar * bi + ai * br], axis=1)


def f_s5(pids, states, u, bb, cc, tab):
    (carry,) = states
    tl = u.shape[0]
    nsub = tl // S5_SUB
    rep = lambda t: jnp.concatenate([t] * nsub, axis=0)
    p0, q0 = tab[S5_SUB:2 * S5_SUB], tab[2 * S5_SUB:3 * S5_SUB]
    lam, p0_last = tab[0:1], tab[2 * S5_SUB - 1:2 * S5_SUB]
    bu = bdot(u, bb, 1, 0)
    pre = block_prefix(_cmul(rep(q0), bu))
    e, entering = carry, []
    for k in range(nsub):
        le = _cmul(lam, e)
        entering.append(jnp.broadcast_to(le, (S5_SUB, le.shape[1])))
        e = _cmul(p0_last, pre[(k + 1) * S5_SUB - 1:(k + 1) * S5_SUB] + le)
    x = _cmul(rep(p0), pre + jnp.concatenate(entering, axis=0))
    y = bdot(x, cc, 1, 0)
    return (y,), (e,)


def _s5_specs(ntl, rev):
    tt = (lambda t: ntl - 1 - t) if rev else (lambda t: t)
    us = pl.BlockSpec((S5_TL, LANES), lambda c, b, t: (b * ntl + tt(t), c))
    bbs = pl.BlockSpec((1, LANES, 2 * S5_CH), lambda c, b, t: (c, 0, 0))
    ccs = pl.BlockSpec((1, 2 * S5_CH, LANES), lambda c, b, t: (c, 0, 0))
    pws = pl.BlockSpec((1, 3 * S5_SUB, 2 * S5_CH), lambda c, b, t: (c, 0, 0))
    sv = pl.BlockSpec((1, 1, 1, 2 * S5_CH), lambda c, b, t: (b * ntl + tt(t), c, 0, 0))
    return us, bbs, ccs, pws, sv


def s5_fwd(u, bb, cc, pw, seq):
    t = u.shape[0]
    ntl = seq // S5_TL
    us, bbs, ccs, pws, sv = _s5_specs(ntl, False)

    def f(pids, states, uu, b3, c3, p3):
        return f_s5(pids, states, uu, b3[0], c3[0], p3[0])

    y, carries = scan_fwd("s5_fwd", f, (S5_CB, t // seq, ntl), 0, [u, bb, cc, pw], [us, bbs, ccs, pws],
                          [jax.ShapeDtypeStruct((t, S5_CB * LANES), f32)], [us], [(1, 2 * S5_CH)], [0.0],
                          [jax.ShapeDtypeStruct((t // S5_TL, S5_CB, 1, 2 * S5_CH), f32)], [sv])
    return y, carries


def s5_bwd(dy, u, bb, cc, pw, carries, seq):
    t = u.shape[0]
    ntl = seq // S5_TL
    us, bbs, ccs, pws, sv = _s5_specs(ntl, True)

    def f(pids, states, uu, b3, c3, p3):
        return f_s5(pids, states, uu, b3[0], c3[0], p3[0])

    first = lambda pids: jnp.logical_and(pids[1] == 0, pids[2] == 0)
    return scan_bwd("s5_bwd", f, (S5_CB, t // seq, ntl), 0, [u, bb, cc, pw], [us, bbs, ccs, pws], [carries], [sv],
                    [dy], [us], [(1, 2 * S5_CH)], [0, 1, 2, 3], {1: first, 2: first, 3: first})


def _adam_math(g, w, m, v):
    m2 = ADAM_B1 * m + (1.0 - ADAM_B1) * g
    v2 = ADAM_B2 * v + (1.0 - ADAM_B2) * jnp.square(g)
    m_hat = m2 / (1.0 - ADAM_B1 ** ADAM_STEP)
    v_hat = v2 / (1.0 - ADAM_B2 ** ADAM_STEP)
    delta = -ADAM_LR * (m_hat / (jnp.sqrt(v_hat) + ADAM_EPS) + ADAM_WD * w)
    return delta, m2, v2


def adamw(name, parts, w, m, v, tr=256):
    n, r, c = parts.shape
    tr = min(tr, r)
    assert r % tr == 0

    def body(p_ref, w_ref, m_ref, v_ref, g_ref, d_ref, m2_ref, v2_ref):
        g = p_ref[0].astype(f32)
        for s in range(1, n):
            g = g + p_ref[s].astype(f32)
        d, m2, v2 = _adam_math(g, w_ref[...], m_ref[...], v_ref[...])
        g_ref[...] = g
        d_ref[...] = d
        m2_ref[...] = m2
        v2_ref[...] = v2

    ps = pl.BlockSpec((n, tr, c), lambda i: (0, i, 0))
    rs = pl.BlockSpec((tr, c), lambda i: (i, 0))
    return pl.pallas_call(body, name=name, grid=(r // tr,), in_specs=[ps, rs, rs, rs], out_specs=[rs] * 4,
                          out_shape=[jax.ShapeDtypeStruct((r, c), f32)] * 4, compiler_params=_cp(1))(parts, w, m, v)


def adamw_layer(name, parts, w, m, v, layer, prev=None, tr=256):
    n, r, c = parts.shape
    nl = w.shape[0]
    tr = min(tr, r)
    assert r % tr == 0 and w.shape[1:] == (r, c)
    n_prev = 0 if prev is None else 4

    def body(*refs):
        p_ref, w_ref, m_ref, v_ref = refs[:4]
        g_ref, d_ref, m2_ref, v2_ref = refs[4 + n_prev:]
        g = p_ref[0].astype(f32)
        for s in range(1, n):
            g = g + p_ref[s].astype(f32)
        d, m2, v2 = _adam_math(g, w_ref[0], m_ref[0], v_ref[0])
        g_ref[0] = g
        d_ref[0] = d
        m2_ref[0] = m2
        v2_ref[0] = v2

    ps = pl.BlockSpec((n, tr, c), lambda i: (0, i, 0))
    rs = pl.BlockSpec((1, tr, c), lambda i: (layer, i, 0))
    anyspec = pl.BlockSpec(memory_space=pl.ANY)
    return pl.pallas_call(body, name=name, grid=(r // tr,), in_specs=[ps, rs, rs, rs] + [anyspec] * n_prev, out_specs=[rs] * 4,
                          out_shape=[jax.ShapeDtypeStruct((nl, r, c), f32)] * 4,
                          input_output_aliases={4 + i: i for i in range(n_prev)},
                          compiler_params=_cp(1))(parts, w, m, v, *(prev or ()))


def sum_parts(name, parts, tr=256):
    n, r, c = parts.shape
    tr = min(tr, r)
    assert r % tr == 0

    def body(p_ref, o_ref):
        g = p_ref[0].astype(f32)
        for s in range(1, n):
            g = g + p_ref[s].astype(f32)
        o_ref[...] = g

    return pl.pallas_call(body, name=name, grid=(r // tr,), in_specs=[pl.BlockSpec((n, tr, c), lambda i: (0, i, 0))],
                          out_specs=pl.BlockSpec((tr, c), lambda i: (i, 0)),
                          out_shape=jax.ShapeDtypeStruct((r, c), f32), compiler_params=_cp(1))(parts)


class Rider:
    def __init__(self, ops):
        self.ops = list(ops or [])
        self.n = len(self.ops)

    def arrays(self):
        return [a for a, _ in self.ops]

    def specs(self):
        return [pl.BlockSpec(memory_space=pl.ANY)] * self.n

    def out_shapes(self):
        return [jax.ShapeDtypeStruct((N_DEV,) + tuple(a.shape) if mode == "gather" else tuple(a.shape), a.dtype)
                for a, mode in self.ops]

    def scratch(self):
        if not self.n:
            return []
        return [pltpu.SemaphoreType.DMA((self.n, N_DEV - 1)), pltpu.SemaphoreType.DMA((self.n, N_DEV - 1)),
                pltpu.SemaphoreType.DMA((self.n,))]

    def _copies(self, ins, outs, sems, with_relays=True):
        send_sems, recv_sems, loc_sems = sems
        x, y, c = lax.axis_index("x"), lax.axis_index("y"), lax.axis_index("c")
        me = 4 * x + 2 * y + c
        first, crossing, relays = [], [], []

        def remote(src, dst, k, idx, peer):
            return pltpu.make_async_remote_copy(src_ref=src, dst_ref=dst, send_sem=send_sems.at[k, idx], recv_sem=recv_sems.at[k, idx],
                                                device_id=peer, device_id_type=pl.DeviceIdType.MESH)

        for k, (_, mode) in enumerate(self.ops):
            src_me = ins[k] if mode == "gather" else ins[k].at[me]
            first.append(pltpu.make_async_copy(src_me, outs[k].at[me], loc_sems.at[k]))
            if mode == "scatter":
                for d in range(1, N_DEV):
                    px = 1 - x if (d >> 2) & 1 else x
                    py = 1 - y if (d >> 1) & 1 else y
                    pc = 1 - c if d & 1 else c
                    first.append(remote(ins[k].at[4 * px + 2 * py + pc], outs[k].at[me], k, d - 1, (px, py, pc)))
            else:
                first.append(remote(ins[k], outs[k].at[me], k, 0, (x, y, 1 - c)))
                for q in range(1, 4):
                    px = 1 - x if (q >> 1) & 1 else x
                    py = 1 - y if q & 1 else y
                    crossing.append(remote(ins[k], outs[k].at[me], k, q, (px, py, c)))
                    if with_relays:
                        block = outs[k].at[4 * px + 2 * py + c]
                        relays.append(remote(block, block, k, 3 + q, (x, y, 1 - c)))
        return first, crossing, relays

    def _start(self, ins, outs, sems):
        first, crossing, _ = self._copies(ins, outs, sems, with_relays=False)
        for cp in first + crossing:
            cp.start()

    def _finish(self, ins, outs, sems):
        first, crossing, relays = self._copies(ins, outs, sems)
        for cp, relay in zip(crossing, relays):
            cp.wait_recv()
            relay.start()
        for cp in first + relays:
            cp.wait()
        for cp in crossing:
            cp.wait_send()

    def start(self, grid, ins, outs, sems):
        if self.n:
            @pl.when(functools.reduce(jnp.logical_and, [pl.program_id(i) == 0 for i in range(len(grid))]))
            def _():
                self._start(ins, outs, sems)

    def wait(self, grid, ins, outs, sems):
        if self.n:
            @pl.when(functools.reduce(jnp.logical_and, [pl.program_id(i) == g - 1 for i, g in enumerate(grid)]))
            def _():
                self._finish(ins, outs, sems)


def exchange(name, ops):
    rider = Rider(ops)
    n = rider.n

    def body(*refs):
        rider._start(refs[:n], refs[n:2 * n], refs[2 * n:])
        rider._finish(refs[:n], refs[n:2 * n], refs[2 * n:])

    return pl.pallas_call(body, name=name, in_specs=rider.specs(), out_specs=rider.specs(), out_shape=rider.out_shapes(),
                          scratch_shapes=rider.scratch())(*rider.arrays())


def _lanes(v, width=LANES):
    v = v.reshape(1, -1)
    return jnp.pad(v, ((0, 0), (0, width - v.shape[1])))


def win_to_padded(w):
    return jnp.concatenate([w[:, :1024], w[:, 2576:3600], w[:, 3600:4624], w[:, 1024:2560], w[:, 2560:2576],
                            jnp.zeros((w.shape[0], PROJ_W - IN_COLS), w.dtype)], axis=1)


def win_from_padded(wp):
    return jnp.concatenate([wp[:, 0:1024], wp[:, 3072:4608], wp[:, 4608:4624], wp[:, 1024:2048], wp[:, 2048:3072]], axis=1)


def headwise_dense(w):
    nb, o, i = w.shape
    rows = jnp.tile(w.transpose(0, 2, 1).reshape(nb * i, o), (1, nb))
    same = (jnp.arange(nb * i)[:, None] // i) == (jnp.arange(nb * o)[None, :] // o)
    return jnp.where(same, rows, 0.0)


def diag_blocks(name, dd, blk, tm=256):
    n = dd.shape[0]

    def body(d_ref, o_ref):
        rows = _iota((tm, n), 0) + pl.program_id(0) * tm
        masked = jnp.where(rows // blk == _iota((tm, n), 1) // blk, d_ref[...], 0.0)
        sel = (_iota((n, LANES), 0) % blk == _iota((n, LANES), 1)).astype(f32)
        o_ref[...] = hdot(masked, sel)

    return pl.pallas_call(body, name=name, grid=(n // tm,), in_specs=[pl.BlockSpec((tm, n), lambda i: (i, 0))],
                          out_specs=pl.BlockSpec((tm, LANES), lambda i: (i, 0)),
                          out_shape=jax.ShapeDtypeStruct((n, LANES), f32), compiler_params=_cp(1))(dd)


def headwise_from_dense(name, dd, o=4, i=4):
    nb = dd.shape[0] // i
    return diag_blocks(name, dd, i)[:, :o].reshape(nb, i, o).transpose(0, 2, 1)


def s5_tables(a_re, a_im, log_step, b_re, b_im, c_re, c_im):
    step = jnp.exp(log_step)[:, None]
    j = jnp.arange(S5_SUB, dtype=f32)[:, None, None]
    expo = jnp.concatenate([j + 1.0, j, -j], axis=0)
    mag = jnp.exp(expo * (a_re * step))
    pw_re, pw_im = mag * jnp.cos(expo * (a_im * step)), mag * jnp.sin(expo * (a_im * step))
    lam_re, lam_im = pw_re[0], pw_im[0]
    den = a_re * a_re + a_im * a_im
    coef_re = ((lam_re - 1.0) * a_re + lam_im * a_im) / den
    coef_im = (lam_im * a_re - (lam_re - 1.0) * a_im) / den
    bb_re = coef_re[..., None] * b_re - coef_im[..., None] * b_im
    bb_im = coef_re[..., None] * b_im + coef_im[..., None] * b_re
    gl = S5_GROUPS // S5_CB
    eye = jnp.eye(gl, dtype=f32)

    def blk_b(t):
        t4 = t.transpose(0, 2, 1).reshape(S5_CB, gl, S5_GROUP, S5_STATE)
        return jnp.einsum("kgcn,gh->kgchn", t4, eye).reshape(S5_CB, gl * S5_GROUP, gl * S5_STATE)

    def blk_c(t):
        t4 = t.reshape(S5_CB, gl, S5_GROUP, S5_STATE)
        return jnp.einsum("kgcn,gh->kgnhc", t4, eye).reshape(S5_CB, gl * S5_STATE, gl * S5_GROUP)

    def blk_p(t):
        return t.reshape(t.shape[0], S5_CB, gl * S5_STATE).transpose(1, 0, 2)

    bb = jnp.concatenate([blk_b(bb_re), blk_b(bb_im)], axis=2)
    cc = jnp.concatenate([blk_c(c_re), -blk_c(c_im)], axis=1)
    pw = jnp.concatenate([blk_p(pw_re), blk_p(pw_im)], axis=2)
    return bb, cc, pw


def ffn_step_bwd(dy, x, nw, wts, saved, ride_act=None, ride_w=None):
    h, g, u = saved
    dx, dnw, dg, du, a, dyh, *got_act = ffn_bwd_act(dy, x, nw, g, u, *wts, ride=ride_act)
    dwg, dwu, dwd, *got_w = ffn_bwd_w(h, dyh, dg, du, a, ride=ride_w)
    return dx, dnw, (dwg, dwu, dwd), got_act, got_w


def hybrid_fwd(x1, p, seq, ride_in, ride_ssd, ride_ml):
    u = rms_fwd(x1, p["mix_norm"])
    proj, *got_in = matmul("hy_in", u, p["win"], ride=ride_in)
    xbc = conv_fwd("ssd_conv", proj, OFF_XBC, p["ssd_conv_w"], p["ssd_conv_b"], seq)
    yraw, hsave, got_ssd = ssd_fwd(xbc, proj, p["a_log"], p["dt_bias"], p["ssd_d"], seq, ride=ride_ssd)
    yssd = rowwise_fwd("ssd_epi", f_ssd_epi, [yraw, proj], [(D_MODEL, 0), (D_MODEL, OFF_Z // D_MODEL)], [p["ssd_norm_w"]], D_MODEL)
    xc = conv_fwd("ml_conv", proj, OFF_MX, p["ml_conv_w"], p["ml_conv_b"], seq)
    q = matmul("hw_q", xc, p["wq"])
    k = matmul("hw_k", xc, p["wk"])
    v = matmul("hw_v", proj, p["wv"], a_off=OFF_MX, a_width=D_MODEL)
    g1 = matmul("gate_q", q, p["wif_q"])
    g2 = matmul("gate_k", k, p["wif_k"])
    g3 = matmul("gate_v", v, p["wif_v"])
    hm, mlsave, got_ml = ml_fwd(q, k, v, g1, g2, g3, p["b_if"], seq, ride=ride_ml)
    yml = rowwise_fwd("ml_epi", f_ml_epi, [hm, xc, proj], [(D_MODEL, 0), (D_MODEL, 0), (D_MODEL, OFF_MZ // D_MODEL)],
                      [p["ml_norm_w"], p["ml_skip"]], D_MODEL)
    t = matmul("hy_out1", yssd, p["wo1"], add=x1)
    x2 = matmul("hy_out2", yml, p["wo2"], add=t)
    return x2, (u, proj, xbc, yraw, hsave, yssd, xc, q, k, v, g1, g2, g3, hm, mlsave, yml), got_in, got_ssd, got_ml


def hybrid_bwd(dx2, x1, p, saved, seq, ride_ml, ride_du):
    u, proj, xbc, yraw, hsave, yssd, xc, q, k, v, g1, g2, g3, hm, mlsave, yml = saved
    gr = {}
    dyssd = matmul("d_yssd", dx2, p["wo1"], cb=1)
    dyml = matmul("d_yml", dx2, p["wo2"], cb=1)
    gr["wo"] = jnp.concatenate([matmul("dw_o1", yssd, dx2, ca=0), matmul("dw_o2", yml, dx2, ca=0)], axis=0)
    d_hm, d_xc, d_mz, gr["ml_norm_w"], gr["ml_skip"] = rowwise_bwd(
        "ml_epi_bwd", f_ml_epi, [hm, xc, proj], [(D_MODEL, 0), (D_MODEL, 0), (D_MODEL, OFF_MZ // D_MODEL)],
        [p["ml_norm_w"], p["ml_skip"]], dyml)
    dq, dk, dv, dgt, gr["b_if"], *got_ml = ml_bwd(d_hm, q, k, v, g1, g2, g3, p["b_if"], mlsave, seq, ride=ride_ml)
    dq = matmul("dq_gate", dgt, p["wif_q"], cb=1, add=dq)
    dk = matmul("dk_gate", dgt, p["wif_k"], cb=1, add=dk)
    dv = matmul("dv_gate", dgt, p["wif_v"], cb=1, add=dv)
    gr["wif"] = jnp.concatenate([matmul("dw_if_q", q, dgt, ca=0), matmul("dw_if_k", k, dgt, ca=0),
                                 matmul("dw_if_v", v, dgt, ca=0)], axis=0)
    d_xc = matmul("dxc_q", dq, p["wq"], cb=1, add=d_xc)
    d_xc = matmul("dxc_k", dk, p["wk"], cb=1, add=d_xc)
    gr["wq"] = matmul("dw_q", xc, dq, ca=0)
    gr["wk"] = matmul("dw_k", xc, dk, ca=0)
    gr["wv"] = matmul("dw_v", proj, dv, ca=0, a_off=OFF_MX, a_width=D_MODEL)
    d_mx, gr["ml_conv_w"], gr["ml_conv_b"] = conv_bwd("ml_conv_bwd", d_xc, proj, OFF_MX, p["ml_conv_w"], p["ml_conv_b"], seq)
    d_mx = matmul("dmx_v", dv, p["wv"], cb=1, add=d_mx)
    d_yraw, d_z, gr["ssd_norm_w"] = rowwise_bwd("ssd_epi_bwd", f_ssd_epi, [yraw, proj],
                                                [(D_MODEL, 0), (D_MODEL, OFF_Z // D_MODEL)], [p["ssd_norm_w"]], dyssd)
    d_xs, d_b, d_c, d_dt, gr["a_log"], gr["dt_bias"], gr["ssd_d"] = ssd_bwd(
        d_yraw, xbc, proj, p["a_log"], p["dt_bias"], p["ssd_d"], hsave, seq)
    d_xbc, gr["ssd_conv_w"], gr["ssd_conv_b"] = conv_bwd("ssd_conv_bwd", jnp.concatenate([d_xs, d_b, d_c], axis=1), proj, OFF_XBC,
                                                         p["ssd_conv_w"], p["ssd_conv_b"], seq)
    dproj = jnp.concatenate([d_z, d_mx, d_mz, d_xbc, d_dt, jnp.zeros((d_dt.shape[0], PROJ_W - OFF_DT - LANES), f32)], axis=1)
    gr["win"] = matmul("dw_in", u.astype(bf16).T, dproj, tiles=(D_MODEL, PROJ_W // 2, min(512, u.shape[0])))
    du, *got_du = matmul("d_u", dproj, p["win"], cb=1, ride=ride_du)
    dx1, gr["mix_norm"] = rms_bwd([du], x1, p["mix_norm"], dx2)
    return dx1, gr, got_ml, got_du


def s5_layer_fwd(x4, p, seq):
    u = rms_fwd(x4, p["mix_norm"])
    ys, carries = s5_fwd(u, p["bb"], p["cc"], p["pw"], seq)
    gg = rowwise_fwd("s5_post", f_s5_post, [ys, u], [(D_MODEL, 0), (D_MODEL, 0)], [p["s5_d"]], D_MODEL)
    pab = matmul("s5_ab", gg, p["wab"])
    x5 = rowwise_fwd("s5_glu", f_glu_res, [pab, x4], [(2 * D_MODEL, 0), (D_MODEL, 0)], [p["b_a"], p["b_b"]], D_MODEL)
    return x5, (u, ys, carries, gg, pab)


def s5_layer_bwd(dx5, x4, p, saved, seq):
    u, ys, carries, gg, pab = saved
    gr = {}
    dpab, gr["b_a"], gr["b_b"] = rowwise_bwd("s5_glu_bwd", f_glu, [pab], [(2 * D_MODEL, 0)], [p["b_a"], p["b_b"]], dx5)
    dgg = matmul("d_gg", dpab, p["wab"], cb=1)
    gr["wab"] = matmul("dw_ab", gg, dpab, ca=0)
    dys, du_a, gr["s5_d"] = rowwise_bwd("s5_post_bwd", f_s5_post, [ys, u], [(D_MODEL, 0), (D_MODEL, 0)], [p["s5_d"]], dgg)
    du_b, gr["bb"], gr["cc"], gr["pw"] = s5_bwd(dys, u, p["bb"], p["cc"], p["pw"], carries, seq)
    dx4, gr["mix_norm"] = rms_bwd([du_a, du_b], x4, p["mix_norm"], dx5)
    return dx4, gr


BIG = ["ffn1_w_gate", "ffn1_w_up", "ffn1_w_down", "ffn2_w_gate", "ffn2_w_up", "ffn2_w_down", "hy_w_in", "hy_w_out", "s5_w_a", "s5_w_b"]
SMALL_SHARDED = {"ssd_conv_w": 2, "ml_conv_w": 2, "ml_w_q": 1, "ml_w_k": 1, "ml_w_v": 1, "ml_w_if": 1, "s5_d": 1, "s5_b_a": 1, "s5_b_b": 1}
WEIGHTS = ["ffn1_norm", "ffn1_w_gate", "ffn1_w_up", "ffn1_w_down", "mix_norm", "ffn2_norm", "ffn2_w_gate", "ffn2_w_up", "ffn2_w_down",
           "hy_w_in", "ssd_conv_w", "ssd_conv_b", "ssd_dt_bias", "ssd_a_log", "ssd_d", "ssd_norm_w", "ml_conv_w", "ml_conv_b",
           "ml_w_q", "ml_w_k", "ml_w_v", "ml_w_if", "ml_b_if", "ml_norm_w", "ml_skip", "hy_w_out", "s5_a_re", "s5_a_im",
           "s5_log_step", "s5_b_re", "s5_b_im", "s5_c_re", "s5_c_im", "s5_d", "s5_w_a", "s5_b_a", "s5_w_b", "s5_b_b", "final_norm"]
S5_PARAMS = ["s5_a_re", "s5_a_im", "s5_log_step", "s5_b_re", "s5_b_im", "s5_c_re", "s5_c_im"]
SMALL_S5 = S5_PARAMS + ["s5_d", "s5_b_a", "s5_b_b"]
SMALL_REST = [n for n in WEIGHTS if n not in BIG and n not in SMALL_S5]
SMALL = SMALL_REST + SMALL_S5


def _unshard(g, axis):
    return jnp.concatenate([g[i] for i in range(N_DEV)], axis=axis)


def assemble_hybrid(gw, rep):
    padn = lambda w: jnp.pad(w, ((0, 0), (0, LANES - w.shape[1]))).astype(bf16)
    wif = _unshard(gw["ml_w_if"], 1)[0]
    wo = _unshard(gw["hy_w_out"], 1)[0].astype(bf16)
    dense = lambda n: headwise_dense(_unshard(gw[n], 1)[0].astype(f32)).astype(bf16)
    w0 = dict(mix_norm=rep["mix_norm"][0:1],
              win=win_to_padded(_unshard(gw["hy_w_in"], 2)[0]).astype(bf16),
              ssd_conv_w=_unshard(gw["ssd_conv_w"], 2)[0], ssd_conv_b=rep["ssd_conv_b"],
              a_log=_lanes(rep["ssd_a_log"]), dt_bias=_lanes(rep["ssd_dt_bias"]), ssd_d=_lanes(rep["ssd_d"]),
              ssd_norm_w=rep["ssd_norm_w"], ml_conv_w=_unshard(gw["ml_conv_w"], 2)[0], ml_conv_b=rep["ml_conv_b"],
              wq=dense("ml_w_q"), wk=dense("ml_w_k"), wv=dense("ml_w_v"),
              wif_q=padn(wif[0:1024]), wif_k=padn(wif[1024:2048]), wif_v=padn(wif[2048:3072]),
              b_if=_lanes(rep["ml_b_if"]), ml_norm_w=rep["ml_norm_w"], ml_skip=rep["ml_skip"],
              wo1=wo[:D_MODEL], wo2=wo[D_MODEL:])
    return w0


def assemble_s5(gw, rep):
    bb, cc, pw = s5_tables(*[rep[n][0] for n in S5_PARAMS])
    wab = jnp.concatenate([_unshard(gw["s5_w_a"], 1)[0], _unshard(gw["s5_w_b"], 1)[0]], axis=1).astype(bf16)
    return dict(mix_norm=rep["mix_norm"][1:2], bb=bb, cc=cc, pw=pw,
                s5_d=_unshard(gw["s5_d"], 1), wab=wab, b_a=_unshard(gw["s5_b_a"], 1), b_b=_unshard(gw["s5_b_b"], 1))


def _shards(full, axis):
    return jnp.stack(jnp.split(full, N_DEV, axis=axis), axis=0)


def small_grads(g_norms, g_hy, g_s5, d_final, rep):
    small = dict(g_norms)
    small["mix_norm"] = jnp.concatenate([g_hy["mix_norm"], g_s5["mix_norm"]], axis=0)
    small["ssd_conv_w"] = g_hy["ssd_conv_w"][None]
    small["ssd_conv_b"] = g_hy["ssd_conv_b"]
    small["ssd_dt_bias"] = g_hy["dt_bias"][:, :SSD_HEADS]
    small["ssd_a_log"] = g_hy["a_log"][:, :SSD_HEADS]
    small["ssd_d"] = g_hy["ssd_d"][:, :SSD_HEADS]
    small["ssd_norm_w"] = g_hy["ssd_norm_w"]
    small["ml_conv_w"] = g_hy["ml_conv_w"][None]
    small["ml_conv_b"] = g_hy["ml_conv_b"]
    for nm, key in (("ml_w_q", "wq"), ("ml_w_k", "wk"), ("ml_w_v", "wv")):
        small[nm] = headwise_from_dense("diag_" + key, g_hy[key])[None]
    small["ml_w_if"] = g_hy["wif"][None, :, :2 * ML_HEADS]
    small["ml_b_if"] = g_hy["b_if"][:, :2 * ML_HEADS]
    small["ml_norm_w"] = g_hy["ml_norm_w"]
    small["ml_skip"] = g_hy["ml_skip"]
    small["final_norm"] = d_final.reshape(-1)
    return small


def s5_small_grads(g_s5, rep):
    small = {}
    _, tvjp = jax.vjp(s5_tables, *[rep[n][0] for n in S5_PARAMS])
    for n, g in zip(S5_PARAMS, tvjp((g_s5["bb"], g_s5["cc"], g_s5["pw"]))):
        small[n] = g[None]
    small["s5_d"] = g_s5["s5_d"]
    small["s5_b_a"] = g_s5["b_a"]
    small["s5_b_b"] = g_s5["b_b"]
    return small


ROW = 1024
F32_ROWS = 8


def _piece_rows(size):
    return -(-size // (ROW * F32_ROWS)) * F32_ROWS


def _pack(arrays):
    pieces = []
    for a in arrays:
        flat = a.astype(f32).reshape(-1)
        pieces.append(jnp.pad(flat, (0, _piece_rows(a.size) * ROW - a.size)).reshape(-1, ROW))
    return jnp.concatenate(pieces, axis=0)


def _unpack(buf, shapes):
    out, r0 = [], 0
    lead = buf.shape[:-2]
    for shp in shapes:
        size = math.prod(shp)
        r = _piece_rows(size)
        out.append(buf[..., r0:r0 + r, :].reshape(lead + (-1,))[..., :size].reshape(lead + tuple(shp)))
        r0 += r
    return out


ADAM_BLOCK_ELEMS = 500_000


def _tile_rows(r, c):
    cap = ADAM_BLOCK_ELEMS // (-(-c // LANES) * LANES)
    if r <= cap:
        return r
    return max(t for t in range(F32_ROWS, cap + 1, F32_ROWS) if r % t == 0)


def _flat2d(a):
    return a.reshape(-1, a.shape[-1])


def kernel(x, ffn1_norm, ffn1_w_gate, ffn1_w_up, ffn1_w_down, mix_norm, ffn2_norm, ffn2_w_gate, ffn2_w_up, ffn2_w_down, hy_w_in, ssd_conv_w, ssd_conv_b, ssd_dt_bias, ssd_a_log, ssd_d, ssd_norm_w, ml_conv_w, ml_conv_b, ml_w_q, ml_w_k, ml_w_v, ml_w_if, ml_b_if, ml_norm_w, ml_skip, hy_w_out, s5_a_re, s5_a_im, s5_log_step, s5_b_re, s5_b_im, s5_c_re, s5_c_im, s5_d, s5_w_a, s5_b_a, s5_w_b, s5_b_b, final_norm, loss_target, m_ffn1_norm, m_ffn1_w_gate, m_ffn1_w_up, m_ffn1_w_down, m_mix_norm, m_ffn2_norm, m_ffn2_w_gate, m_ffn2_w_up, m_ffn2_w_down, m_hy_w_in, m_ssd_conv_w, m_ssd_conv_b, m_ssd_dt_bias, m_ssd_a_log, m_ssd_d, m_ssd_norm_w, m_ml_conv_w, m_ml_conv_b, m_ml_w_q, m_ml_w_k, m_ml_w_v, m_ml_w_if, m_ml_b_if, m_ml_norm_w, m_ml_skip, m_hy_w_out, m_s5_a_re, m_s5_a_im, m_s5_log_step, m_s5_b_re, m_s5_b_im, m_s5_c_re, m_s5_c_im, m_s5_d, m_s5_w_a, m_s5_b_a, m_s5_w_b, m_s5_b_b, m_final_norm, v_ffn1_norm, v_ffn1_w_gate, v_ffn1_w_up, v_ffn1_w_down, v_mix_norm, v_ffn2_norm, v_ffn2_w_gate, v_ffn2_w_up, v_ffn2_w_down, v_hy_w_in, v_ssd_conv_w, v_ssd_conv_b, v_ssd_dt_bias, v_ssd_a_log, v_ssd_d, v_ssd_norm_w, v_ml_conv_w, v_ml_conv_b, v_ml_w_q, v_ml_w_k, v_ml_w_v, v_ml_w_if, v_ml_b_if, v_ml_norm_w, v_ml_skip, v_hy_w_out, v_s5_a_re, v_s5_a_im, v_s5_log_step, v_s5_b_re, v_s5_b_im, v_s5_c_re, v_s5_c_im, v_s5_d, v_s5_w_a, v_s5_b_a, v_s5_w_b, v_s5_b_b, v_final_norm):
    given = dict(locals())
    w = {n: given[n] for n in WEIGHTS}
    mom = {n: given["m_" + n] for n in WEIGHTS}
    var = {n: given["v_" + n] for n in WEIGHTS}
    bl, seq, d = x.shape
    me = 4 * lax.axis_index("x") + 2 * lax.axis_index("y") + lax.axis_index("c")

    x0, tgt = x.reshape(bl * seq, d), loss_target.reshape(bl * seq, d)
    rep = {n: w[n] for n in WEIGHTS if n not in BIG and n not in SMALL_SHARDED}
    ffn_w = ("_w_gate", "_w_up", "_w_down")

    def ffn_gather(pre, l):
        return [(w[pre + s][l:l + 1].astype(bf16), "gather") for s in ffn_w]

    def scatter(parts):
        return [(p, "scatter") for p in parts]

    wf10 = tuple(exchange("gather_ffn1_l0", ffn_gather("ffn1", 0)))
    mixer_ops = [(w[n].astype(bf16), "gather") for n in ("hy_w_in", "hy_w_out")]
    mixer_ops.append((_pack([w[n] for n in SMALL_SHARDED]), "gather"))
    x1, *rest = ffn_fwd(x0, ffn1_norm[0:1], *wf10, ride=mixer_ops)
    sv10, got = rest[:3], rest[3:]
    gw = dict(zip(("hy_w_in", "hy_w_out"), got[:2]))
    gw.update(zip(SMALL_SHARDED, _unpack(got[2], [w[n].shape for n in SMALL_SHARDED])))
    w0 = assemble_hybrid(gw, rep)
    x2, sv_h, wf20, got, wf11 = hybrid_fwd(x1, w0, seq, ride_in=ffn_gather("ffn2", 0),
                                           ride_ssd=[(w[n].astype(bf16), "gather") for n in ("s5_w_a", "s5_w_b")],
                                           ride_ml=ffn_gather("ffn1", 1))
    gw.update(zip(("s5_w_a", "s5_w_b"), got))
    w1 = assemble_s5(gw, rep)
    x3, *rest = ffn_fwd(x2, ffn2_norm[0:1], *wf20, ride=ffn_gather("ffn2", 1))
    sv20, wf21 = rest[:3], tuple(rest[3:])
    x4, *sv11 = ffn_fwd(x3, ffn1_norm[1:2], *wf11)
    x5, sv_s = s5_layer_fwd(x4, w1, seq)
    x6, *sv21 = ffn_fwd(x5, ffn2_norm[1:2], *wf21)
    loss, dx6, d_final = loss_head(x6, final_norm.reshape(1, d), tgt)

    dx5, dn21, dw21, _, _ = ffn_step_bwd(dx6, x5, ffn2_norm[1:2], wf21, sv21)
    dx4, g_s5 = s5_layer_bwd(dx5, x4, w1, sv_s, seq)
    dwab = g_s5.pop("wab")
    s5_ops = scatter([_shards(dwab[None, :, :D_MODEL], 1).astype(bf16), _shards(dwab[None, :, D_MODEL:], 1).astype(bf16)])
    dx3, dn11, dw11, got_a, got_w = ffn_step_bwd(dx4, x3, ffn1_norm[1:2], wf11, sv11, ride_act=scatter(dw21[:2]),
                                                 ride_w=scatter(dw21[2:]) + s5_ops)
    p21, p_s5 = got_a + got_w[:1], got_w[1:]
    small = s5_small_grads(g_s5, rep)
    dx2, dn20, dw20, got_a, got_w = ffn_step_bwd(dx3, x2, ffn2_norm[0:1], wf20, sv20, ride_act=scatter(dw11[:2]),
                                                 ride_w=scatter(dw11[2:]))
    p11 = got_a + got_w
    dx1, g_hy, p20, (parts_s5,) = hybrid_bwd(dx2, x1, w0, sv_h, seq, ride_ml=scatter(dw20),
                                             ride_du=[(_pack([small[n] for n in SMALL_S5]), "gather")])
    hy_ops = scatter([_shards(win_from_padded(g_hy.pop("win"))[None], 2).astype(bf16), _shards(g_hy.pop("wo")[None], 1).astype(bf16)])
    h10, g10, u10 = sv10
    dx0, dn10, dg, du, a, dyh, *p_hy = ffn_bwd_act(dx1, x0, ffn1_norm[0:1], g10, u10, *wf10, ride=hy_ops)
    g_norms = {"ffn1_norm": jnp.concatenate([dn10, dn11], axis=0), "ffn2_norm": jnp.concatenate([dn20, dn21], axis=0)}
    small.update(small_grads(g_norms, g_hy, g_s5, d_final, rep))
    *dw10, parts_rest = ffn_bwd_w(h10, dyh, dg, du, a, ride=[(_pack([small[n] for n in SMALL_REST]), "gather")])
    p10 = exchange("reduce_tail", scatter(dw10))
    small_parts = jnp.concatenate([parts_rest, parts_s5], axis=1)
    small_sum = sum_parts("sum_small", small_parts, tr=_tile_rows(small_parts.shape[1], ROW))

    out_g, out_d, out_m, out_v = {}, {}, {}, {}
    ffn_parts = {"ffn1": (p10, p11), "ffn2": (p20, p21)}
    for pre in ("ffn1", "ffn2"):
        for k, s in enumerate(ffn_w):
            n = pre + s
            r, c = w[n].shape[1:]
            res = None
            for l in (1, 0):
                res = adamw_layer("adamw_" + n, ffn_parts[pre][l][k].reshape(N_DEV, r, c), w[n], mom[n], var[n], l, res,
                                  tr=_tile_rows(r, c))
            out_g[n], out_d[n], out_m[n], out_v[n] = res
    for n, parts in zip(("hy_w_in", "hy_w_out", "s5_w_a", "s5_w_b"), tuple(p_hy) + tuple(p_s5)):
        shp = w[n].shape
        w2 = _flat2d(w[n])
        res = adamw("adamw_" + n, parts.reshape((N_DEV,) + w2.shape), w2, _flat2d(mom[n]), _flat2d(var[n]),
                    tr=_tile_rows(*w2.shape))
        out_g[n], out_d[n], out_m[n], out_v[n] = [a.reshape(shp) for a in res]
    g_small = {}
    for n, full in zip(SMALL, _unpack(small_sum, [small[n].shape for n in SMALL])):
        if n in SMALL_SHARDED:
            ax = SMALL_SHARDED[n]
            full = lax.dynamic_slice_in_dim(full, me * w[n].shape[ax], w[n].shape[ax], axis=ax)
        g_small[n] = full
    packs = [_pack([t[n] for n in SMALL]) for t in (g_small, w, mom, var)]
    res = adamw("adamw_small", packs[0][None], packs[1], packs[2], packs[3], tr=_tile_rows(*packs[0].shape))
    for dst, a in zip((out_g, out_d, out_m, out_v), res):
        dst.update(zip(SMALL, _unpack(a, [w[n].shape for n in SMALL])))

    total = lax.psum(loss[0, 0], ("x", "y", "c"))
    return (total, dx0.reshape(bl, seq, d), *[out_g[n] for n in WEIGHTS], *[out_d[n] for n in WEIGHTS],
            *[out_m[n] for n in WEIGHTS], *[out_v[n] for n in WEIGHTS])
```

```python
import functools
import math

import jax
import jax.numpy as jnp
from jax import lax
from jax.experimental import pallas as pl
from jax.experimental.pallas import tpu as pltpu

f32 = jnp.float32
bf16 = jnp.bfloat16

N_DEV = 8
D_MODEL = 1024
EPS = 1e-6
FFN_RES = 0.5
CONV_W = 4
SSD_HEADS = 16
SSD_HEAD_DIM = 64
SSD_GROUPS = 2
SSD_STATE = 128
SSD_HG = SSD_HEADS // SSD_GROUPS
SSD_GW = SSD_HG * SSD_HEAD_DIM
CHUNK = 128
ML_HEADS = 4
ML_HD = 256
S5_GROUP = 16
S5_GROUPS = 64
S5_STATE = 64
S5_CB = 8
S5_CH = (S5_GROUPS // S5_CB) * S5_STATE
S5_TL = 512
S5_SUB = 32
LANES = 128
IN_COLS = 4624
PROJ_W = 4864
OFF_Z, OFF_MX, OFF_MZ, OFF_XBC, OFF_DT = 0, 1024, 2048, 3072, 4608
ADAM_LR, ADAM_B1, ADAM_B2, ADAM_EPS, ADAM_WD, ADAM_STEP = 0.001, 0.9, 0.999, 1e-08, 0.01, 10
NEG = -1e30
VMEM_LIMIT = 56 * 1024 * 1024


def _cp(n):
    return pltpu.CompilerParams(dimension_semantics=("arbitrary",) * n, vmem_limit_bytes=VMEM_LIMIT)


def _dg(a, b, ca, cb):
    return lax.dot_general(a.astype(bf16), b.astype(bf16), (((ca,), (cb,)), ((), ())), preferred_element_type=f32)


@functools.partial(jax.custom_vjp, nondiff_argnums=(2, 3))
def bdot(a, b, ca, cb):
    return _dg(a, b, ca, cb)


def _bdot_fwd(a, b, ca, cb):
    return _dg(a, b, ca, cb), (a, b)


def _bdot_bwd(ca, cb, res, ct):
    a, b = res
    da = _dg(ct, b, 1, 1 - cb) if ca == 1 else _dg(b, ct, 1 - cb, 1)
    db = _dg(a, ct, 1 - ca, 0) if cb == 0 else _dg(ct, a, 0, 1 - ca)
    return da, db


bdot.defvjp(_bdot_fwd, _bdot_bwd)


def _split3(z):
    hi = z.astype(bf16)
    r1 = z - hi.astype(f32)
    mid = r1.astype(bf16)
    return hi, mid, (r1 - mid.astype(f32)).astype(bf16)


def _sel(z, m, z_left, transpose_m):
    mm = m.astype(bf16)
    dn = lambda zz: lax.dot_general(zz, mm, (((1,), (1 if transpose_m else 0,)), ((), ())), preferred_element_type=f32) \
        if z_left else lax.dot_general(mm, zz, (((0 if transpose_m else 1,), (0,)), ((), ())), preferred_element_type=f32)
    hi, mid, lo = _split3(z)
    return dn(hi) + dn(mid) + dn(lo)


@functools.partial(jax.custom_vjp, nondiff_argnums=(2,))
def _seldot(z, m, z_left):
    return _sel(z, m, z_left, False)


_seldot.defvjp(lambda z, m, z_left: (_sel(z, m, z_left, False), m),
               lambda z_left, m, ct: (_sel(ct, m, z_left, True), jnp.zeros_like(m)))


def hdot(a, b, exact="b"):
    return _seldot(a, b.astype(f32), True) if exact == "b" else _seldot(b, a.astype(f32), False)


def _iota(shape, dim):
    return lax.broadcasted_iota(jnp.int32, shape, dim)


def _tri(n):
    return (_iota((n, n), 0) >= _iota((n, n), 1))


@functools.partial(jax.custom_vjp, nondiff_argnums=(1,))
def tshift(x, k):
    return jnp.where(_iota(x.shape, 0) >= k, pltpu.roll(x, k, 0), 0.0)


def _tshift_fwd(x, k):
    return tshift(x, k), None


def _tshift_bwd(k, _, ct):
    n = ct.shape[0]
    return (jnp.where(_iota(ct.shape, 0) < n - k, pltpu.roll(ct, n - k, 0), 0.0),)


tshift.defvjp(_tshift_fwd, _tshift_bwd)


def _lane_pick(a, idx):
    return jnp.sum(jnp.where(_iota(a.shape, 1) == idx, a, 0.0), axis=1, keepdims=True)


def _row_pick(a, idx):
    return jnp.sum(jnp.where(_iota(a.shape, 0) == idx, a, 0.0), axis=0, keepdims=True)


def _silu(x):
    return x * jax.nn.sigmoid(x)


def map_fwd(name, f, grid, ins, in_specs, out_shapes, out_specs):
    n_in = len(ins)

    def body(*refs):
        pids = tuple(pl.program_id(i) for i in range(len(grid)))
        outs = f(pids, *[r[...] for r in refs[:n_in]])
        for r, o in zip(refs[n_in:], outs):
            r[...] = o.astype(r.dtype)

    return pl.pallas_call(body, name=name, grid=grid, in_specs=in_specs, out_specs=out_specs,
                          out_shape=out_shapes, compiler_params=_cp(len(grid)))(*ins)


def scan_fwd(name, f, grid, slot_axis, ins, in_specs, out_shapes, out_specs, state_shapes, state_init, save_shapes, save_specs,
             ride=None):
    n_in, n_out, n_st = len(ins), len(out_shapes), len(state_shapes)
    n_slots = grid[slot_axis]
    cax = len(grid) - 1 if slot_axis != len(grid) - 1 else len(grid) - 2
    rider = Rider(ride)
    nr = rider.n

    def body(*refs):
        pids = tuple(pl.program_id(i) for i in range(len(grid)))
        in_refs, r_ins = refs[:n_in], refs[n_in:n_in + nr]
        o0 = n_in + nr
        out_refs, save_refs = refs[o0:o0 + n_out], refs[o0 + n_out:o0 + n_out + n_st]
        r_outs = refs[o0 + n_out + n_st:o0 + n_out + n_st + nr]
        st_refs = refs[o0 + n_out + n_st + nr:o0 + n_out + 2 * n_st + nr]
        sems = refs[o0 + n_out + 2 * n_st + nr:]
        rider.start(grid, r_ins, r_outs, sems)
        slot = pids[slot_axis]

        @pl.when(pids[cax] == 0)
        def _():
            for s, init in zip(st_refs, state_init):
                s[slot] = jnp.full(s.shape[1:], init, f32)

        states = tuple(s[slot] for s in st_refs)
        for sv, st in zip(save_refs, states):
            sv[...] = st.reshape(sv.shape)
        outs, new = f(pids, states, *[r[...] for r in in_refs])
        for r, o in zip(out_refs, outs):
            r[...] = o.astype(r.dtype)
        for s, v in zip(st_refs, new):
            s[slot] = v
        rider.wait(grid, r_ins, r_outs, sems)

    scratch = [pltpu.VMEM((n_slots,) + tuple(s), f32) for s in state_shapes]
    return pl.pallas_call(body, name=name, grid=grid, in_specs=list(in_specs) + rider.specs(),
                          out_specs=list(out_specs) + list(save_specs) + rider.specs(),
                          out_shape=list(out_shapes) + list(save_shapes) + rider.out_shapes(),
                          scratch_shapes=scratch + rider.scratch(), compiler_params=_cp(len(grid)))(*ins, *rider.arrays())


def scan_bwd(name, f, grid, slot_axis, ins, in_specs, saves, save_specs, cts, ct_specs, state_shapes, wrt, acc_first):
    n_in, n_st, n_ct = len(ins), len(saves), len(cts)
    n_slots = grid[slot_axis]
    cax = len(grid) - 1 if slot_axis != len(grid) - 1 else len(grid) - 2

    def body(*refs):
        pids = tuple(pl.program_id(i) for i in range(len(grid)))
        in_refs = refs[:n_in]
        save_refs = refs[n_in:n_in + n_st]
        ct_refs = refs[n_in + n_st:n_in + n_st + n_ct]
        out_refs = refs[n_in + n_st + n_ct:n_in + n_st + n_ct + len(wrt)]
        dst_refs = refs[n_in + n_st + n_ct + len(wrt):]
        slot = pids[slot_axis]

        @pl.when(pids[cax] == 0)
        def _():
            for s in dst_refs:
                s[slot] = jnp.zeros(s.shape[1:], f32)

        vals = [r[...] for r in in_refs]
        states = tuple(sv[...].reshape(shp) for sv, shp in zip(save_refs, state_shapes))
        ctv = tuple(r[...].astype(f32) for r in ct_refs)
        dnew = tuple(s[slot] for s in dst_refs)

        def g(st, *dv):
            full = list(vals)
            for i, v in zip(wrt, dv):
                full[i] = v
            outs, new = f(pids, st, *full)
            return tuple(outs), tuple(new)

        _, vjp = jax.vjp(g, states, *[vals[i] for i in wrt])
        grads = vjp((ctv, dnew))
        for s, v in zip(dst_refs, grads[0]):
            s[slot] = v
        for i, o_ref, gr in zip(wrt, out_refs, grads[1:]):
            first = acc_first.get(i)
            if first is None:
                o_ref[...] = gr.astype(o_ref.dtype)
            else:
                @pl.when(first(pids))
                def _():
                    o_ref[...] = jnp.zeros_like(o_ref)
                o_ref[...] += gr

    out_shapes = [jax.ShapeDtypeStruct(ins[i].shape, f32) for i in wrt]
    out_specs = [in_specs[i] for i in wrt]
    scratch = [pltpu.VMEM((n_slots,) + tuple(s), f32) for s in state_shapes]
    return pl.pallas_call(body, name=name, grid=grid, in_specs=list(in_specs) + list(save_specs) + list(ct_specs),
                          out_specs=out_specs, out_shape=out_shapes, scratch_shapes=scratch,
                          compiler_params=_cp(len(grid)))(*ins, *saves, *cts)


def _fit(dim, cap):
    if dim <= cap:
        return dim
    return max(t for t in range(LANES, cap + 1, LANES) if dim % t == 0)


def _matmul_tiles(m, n, kdim, ca):
    if ca == 1:
        return _fit(m, 512), _fit(n, 2432), _fit(kdim, 2432)
    return _fit(m, 1024), _fit(n, 1280), _fit(kdim, 512)


def matmul(name, a, b, ca=1, cb=0, add=None, out_dtype=f32, a_off=0, a_width=None, ride=None, tiles=None):
    rider = Rider(ride)
    nr = rider.n
    a_width = a.shape[1] if a_width is None else a_width
    kdim = b.shape[cb]
    n = b.shape[1 - cb]
    m = a.shape[0] if ca == 1 else a_width
    tm, tn, tk = tiles or _matmul_tiles(m, n, kdim, ca)
    assert m % tm == 0 and n % tn == 0 and kdim % tk == 0
    nk = kdim // tk
    if ca == 1:
        assert a_off % tk == 0 and a_width == kdim
        koff = a_off // tk
        a_spec = pl.BlockSpec((tm, tk), lambda i, j, k: (i, k + koff))
    else:
        assert a_off % tm == 0 and a.shape[0] == kdim
        ioff = a_off // tm
        a_spec = pl.BlockSpec((tk, tm), lambda i, j, k: (k, i + ioff))
    b_spec = pl.BlockSpec((tk, tn), lambda i, j, k: (k, j)) if cb == 0 else pl.BlockSpec((tn, tk), lambda i, j, k: (j, k))
    o_spec = pl.BlockSpec((tm, tn), lambda i, j, k: (i, j))
    has_add = add is not None

    n_in = 3 if has_add else 2
    grid = (m // tm, n // tn, nk)

    def body(*refs):
        a_ref, b_ref = refs[0], refs[1]
        add_ref = refs[2] if has_add else None
        r_ins, o_ref = refs[n_in:n_in + nr], refs[n_in + nr]
        r_outs, acc, sems = refs[n_in + nr + 1:n_in + 2 * nr + 1], refs[n_in + 2 * nr + 1], refs[n_in + 2 * nr + 2:]
        rider.start(grid, r_ins, r_outs, sems)
        k = pl.program_id(2)

        @pl.when(k == 0)
        def _():
            acc[...] = add_ref[...].astype(f32) if has_add else jnp.zeros_like(acc)

        acc[...] += _dg(a_ref[...], b_ref[...], ca, cb)

        @pl.when(k == nk - 1)
        def _():
            o_ref[...] = acc[...].astype(o_ref.dtype)

        rider.wait(grid, r_ins, r_outs, sems)

    ins = [a, b] + ([add] if has_add else [])
    specs = [a_spec, b_spec] + ([o_spec] if has_add else [])
    res = pl.pallas_call(body, name=name, grid=grid, in_specs=specs + rider.specs(), out_specs=[o_spec] + rider.specs(),
                         out_shape=[jax.ShapeDtypeStruct((m, n), out_dtype)] + rider.out_shapes(),
                         scratch_shapes=[pltpu.VMEM((tm, tn), f32)] + rider.scratch(), compiler_params=_cp(3))(*ins, *rider.arrays())
    return res if nr else res[0]


def f_rms(pids, x, w):
    r = lax.rsqrt(jnp.mean(x * x, axis=-1, keepdims=True) + EPS)
    return (x * r * w,)


def _row_spec(tm, width, col=0):
    return pl.BlockSpec((tm, width), lambda i: (i, col))


def _par_spec(shape):
    return pl.BlockSpec(shape, lambda *p: (0,) * len(shape))


def rms_fwd(x, w, tm=512):
    t, d = x.shape
    return map_fwd("rms_fwd", f_rms, (t // tm,), [x, w], [_row_spec(tm, d), _par_spec((1, d))],
                   [jax.ShapeDtypeStruct((t, d), f32)], [_row_spec(tm, d)])[0]


def rms_bwd(dys, x, w, dres, tm=512):
    t, d = x.shape
    n = len(dys)

    def body(*refs):
        x_ref, w_ref, dres_ref, dx_ref, dw_ref = refs[n:]
        dy = refs[0][...]
        for r in refs[1:n]:
            dy = dy + r[...]
        _, vjp = jax.vjp(lambda xx, ww: f_rms(None, xx, ww)[0], x_ref[...], w_ref[...])
        dx, dw = vjp(dy)
        dx_ref[...] = dx + dres_ref[...]

        @pl.when(pl.program_id(0) == 0)
        def _():
            dw_ref[...] = jnp.zeros_like(dw_ref)
        dw_ref[...] += dw

    return pl.pallas_call(body, name="rms_bwd", grid=(t // tm,),
                          in_specs=[_row_spec(tm, d)] * (n + 1) + [_par_spec((1, d)), _row_spec(tm, d)],
                          out_specs=[_row_spec(tm, d), _par_spec((1, d))],
                          out_shape=[jax.ShapeDtypeStruct((t, d), f32), jax.ShapeDtypeStruct((1, d), f32)],
                          compiler_params=_cp(1))(*dys, x, w, dres)


def loss_head(x, w, tgt, tm=512):
    t, d = x.shape

    def fl(xx, ww, tt):
        y = f_rms(None, xx, ww)[0]
        return 0.5 * jnp.sum(jnp.mean(jnp.square(y - tt), axis=-1, keepdims=True), axis=0, keepdims=True)

    def body(x_ref, w_ref, t_ref, loss_ref, dx_ref, dw_ref):
        val, vjp = jax.vjp(lambda xx, ww: fl(xx, ww, t_ref[...]), x_ref[...], w_ref[...])
        dx, dw = vjp(jnp.ones((1, 1), f32))
        dx_ref[...] = dx

        @pl.when(pl.program_id(0) == 0)
        def _():
            dw_ref[...] = jnp.zeros_like(dw_ref)
            loss_ref[...] = jnp.zeros_like(loss_ref)
        dw_ref[...] += dw
        loss_ref[...] += val

    return pl.pallas_call(body, name="loss_head", grid=(t // tm,),
                          in_specs=[_row_spec(tm, d), _par_spec((1, d)), _row_spec(tm, d)],
                          out_specs=[_par_spec((1, 1)), _row_spec(tm, d), _par_spec((1, d))],
                          out_shape=[jax.ShapeDtypeStruct((1, 1), f32), jax.ShapeDtypeStruct((t, d), f32),
                                     jax.ShapeDtypeStruct((1, d), f32)],
                          compiler_params=_cp(1))(x, w, tgt)


def ffn_fwd(x, nw, wg, wu, wd, tm=1024, ride=None):
    t, d = x.shape
    ns, _, _, fs = wg.shape
    rider = Rider(ride)
    nr = rider.n
    grid = (t // tm, ns)

    def body(*refs):
        x_ref, nw_ref, wg_ref, wu_ref, wd_ref = refs[:5]
        r_ins = refs[5:5 + nr]
        xo_ref, h_ref, g_ref, u_ref = refs[5 + nr:9 + nr]
        r_outs, acc, sems = refs[9 + nr:9 + 2 * nr], refs[9 + 2 * nr], refs[10 + 2 * nr:]
        rider.start(grid, r_ins, r_outs, sems)
        j = pl.program_id(1)

        @pl.when(j == 0)
        def _():
            h_ref[...] = f_rms(None, x_ref[...], nw_ref[...])[0].astype(bf16)
            acc[...] = jnp.zeros_like(acc)

        h = h_ref[...]
        g = jnp.dot(h, wg_ref[0, 0], preferred_element_type=f32)
        u = jnp.dot(h, wu_ref[0, 0], preferred_element_type=f32)
        g_ref[0] = g.astype(g_ref.dtype)
        u_ref[0] = u.astype(u_ref.dtype)
        acc[...] += jnp.dot((_silu(g) * u).astype(bf16), wd_ref[0, 0], preferred_element_type=f32)

        @pl.when(j == ns - 1)
        def _():
            xo_ref[...] = x_ref[...] + FFN_RES * acc[...]

        rider.wait(grid, r_ins, r_outs, sems)

    row = pl.BlockSpec((tm, d), lambda i, j: (i, 0))
    wcol = pl.BlockSpec((1, 1, d, fs), lambda i, j: (j, 0, 0, 0))
    wrow = pl.BlockSpec((1, 1, fs, d), lambda i, j: (j, 0, 0, 0))
    act = pl.BlockSpec((1, tm, fs), lambda i, j: (j, i, 0))
    return pl.pallas_call(body, name="ffn_fwd", grid=grid,
                          in_specs=[row, pl.BlockSpec((1, d), lambda i, j: (0, 0)), wcol, wcol, wrow] + rider.specs(),
                          out_specs=[row, row, act, act] + rider.specs(),
                          out_shape=[jax.ShapeDtypeStruct((t, d), f32), jax.ShapeDtypeStruct((t, d), bf16),
                                     jax.ShapeDtypeStruct((ns, t, fs), bf16), jax.ShapeDtypeStruct((ns, t, fs), bf16)]
                          + rider.out_shapes(),
                          scratch_shapes=[pltpu.VMEM((tm, d), f32)] + rider.scratch(),
                          compiler_params=_cp(2))(x, nw, wg, wu, wd, *rider.arrays())


def ffn_bwd_act(dy, x, nw, g, u, wg, wu, wd, tm=512, ride=None):
    t, d = x.shape
    ns, _, _, fs = wg.shape
    rider = Rider(ride)
    nr = rider.n
    grid = (t // tm, ns)

    def body(*refs):
        dy_ref, x_ref, nw_ref, g_ref, u_ref, wg_ref, wu_ref, wd_ref = refs[:8]
        r_ins = refs[8:8 + nr]
        dx_ref, dnw_ref, dg_ref, du_ref, a_ref, dyh_ref = refs[8 + nr:14 + nr]
        r_outs, acc, sems = refs[14 + nr:14 + 2 * nr], refs[14 + 2 * nr], refs[15 + 2 * nr:]
        rider.start(grid, r_ins, r_outs, sems)
        i, j = pl.program_id(0), pl.program_id(1)

        @pl.when(j == 0)
        def _():
            acc[...] = jnp.zeros_like(acc)
            dyh_ref[...] = (FFN_RES * dy_ref[...]).astype(bf16)

        dyh = dyh_ref[...]
        da = _dg(dyh, wd_ref[0, 0], 1, 1)
        gg, uu = g_ref[0].astype(f32), u_ref[0].astype(f32)
        sg = jax.nn.sigmoid(gg)
        si = gg * sg
        dgv = (da * uu * (sg * (1.0 + gg * (1.0 - sg)))).astype(bf16)
        duv = (da * si).astype(bf16)
        dg_ref[0] = dgv
        du_ref[0] = duv
        a_ref[0] = (si * uu).astype(bf16)
        acc[...] += _dg(dgv, wg_ref[0, 0], 1, 1) + _dg(duv, wu_ref[0, 0], 1, 1)

        @pl.when(j == ns - 1)
        def _():
            _, vjp = jax.vjp(lambda xx, ww: f_rms(None, xx, ww)[0], x_ref[...], nw_ref[...])
            dx, dw = vjp(acc[...])
            dx_ref[...] = dx + dy_ref[...]

            @pl.when(i == 0)
            def _():
                dnw_ref[...] = jnp.zeros_like(dnw_ref)
            dnw_ref[...] += dw

        rider.wait(grid, r_ins, r_outs, sems)

    row = pl.BlockSpec((tm, d), lambda i, j: (i, 0))
    wcol = pl.BlockSpec((1, 1, d, fs), lambda i, j: (j, 0, 0, 0))
    wrow = pl.BlockSpec((1, 1, fs, d), lambda i, j: (j, 0, 0, 0))
    act = pl.BlockSpec((1, tm, fs), lambda i, j: (j, i, 0))
    par = pl.BlockSpec((1, d), lambda i, j: (0, 0))
    return pl.pallas_call(body, name="ffn_bwd_act", grid=grid,
                          in_specs=[row, row, par, act, act, wcol, wcol, wrow] + rider.specs(),
                          out_specs=[row, par, act, act, act, row] + rider.specs(),
                          out_shape=[jax.ShapeDtypeStruct((t, d), f32), jax.ShapeDtypeStruct((1, d), f32)]
                          + [jax.ShapeDtypeStruct((ns, t, fs), bf16)] * 3 + [jax.ShapeDtypeStruct((t, d), bf16)]
                          + rider.out_shapes(),
                          scratch_shapes=[pltpu.VMEM((tm, d), f32)] + rider.scratch(),
                          compiler_params=_cp(2))(dy, x, nw, g, u, wg, wu, wd, *rider.arrays())


def ffn_bwd_w(h, dyh, dg, du, a, tk=1024, ride=None):
    t, d = h.shape
    ns, _, fs = dg.shape
    nk = t // tk
    rider = Rider(ride)
    nr = rider.n
    grid = (ns, nk)

    def body(*refs):
        h_ref, dy_ref, dg_ref, du_ref, a_ref = refs[:5]
        r_ins = refs[5:5 + nr]
        og, ou, od = refs[5 + nr:8 + nr]
        r_outs = refs[8 + nr:8 + 2 * nr]
        ag, au, ad = refs[8 + 2 * nr:11 + 2 * nr]
        sems = refs[11 + 2 * nr:]
        rider.start(grid, r_ins, r_outs, sems)
        k = pl.program_id(1)

        @pl.when(k == 0)
        def _():
            ag[...] = jnp.zeros_like(ag)
            au[...] = jnp.zeros_like(au)
            ad[...] = jnp.zeros_like(ad)

        hh = h_ref[...]
        ag[...] += _dg(hh, dg_ref[0], 0, 0)
        au[...] += _dg(hh, du_ref[0], 0, 0)
        ad[...] += _dg(a_ref[0], dy_ref[...], 0, 0)

        @pl.when(k == nk - 1)
        def _():
            og[0, 0] = ag[...].astype(og.dtype)
            ou[0, 0] = au[...].astype(ou.dtype)
            od[0, 0] = ad[...].astype(od.dtype)

        rider.wait(grid, r_ins, r_outs, sems)

    row = pl.BlockSpec((tk, d), lambda j, k: (k, 0))
    act = pl.BlockSpec((1, tk, fs), lambda j, k: (j, k, 0))
    wcol = pl.BlockSpec((1, 1, d, fs), lambda j, k: (j, 0, 0, 0))
    wrow = pl.BlockSpec((1, 1, fs, d), lambda j, k: (j, 0, 0, 0))
    return pl.pallas_call(body, name="ffn_bwd_w", grid=grid, in_specs=[row, row, act, act, act] + rider.specs(),
                          out_specs=[wcol, wcol, wrow] + rider.specs(),
                          out_shape=[jax.ShapeDtypeStruct((ns, 1, d, fs), bf16)] * 2
                          + [jax.ShapeDtypeStruct((ns, 1, fs, d), bf16)] + rider.out_shapes(),
                          scratch_shapes=[pltpu.VMEM((d, fs), f32), pltpu.VMEM((d, fs), f32), pltpu.VMEM((fs, d), f32)]
                          + rider.scratch(),
                          compiler_params=_cp(2))(h, dyh, dg, du, a, *rider.arrays())


def f_conv(pids, x, w, b):
    y = b + x * w[CONV_W - 1:CONV_W, :]
    for j in range(CONV_W - 1):
        y = y + tshift(x, CONV_W - 1 - j) * w[j:j + 1, :]
    return (_silu(y),)


def _conv_specs(seq, col0, cb):
    xs = pl.BlockSpec((seq, cb), lambda c, b: (b, col0 + c))
    ws = pl.BlockSpec((CONV_W, cb), lambda c, b: (0, c))
    bs = pl.BlockSpec((1, cb), lambda c, b: (0, c))
    ys = pl.BlockSpec((seq, cb), lambda c, b: (b, c))
    return xs, ws, bs, ys


def conv_fwd(name, src, col_off, w, b, seq, cb=256):
    t = src.shape[0]
    c = w.shape[1]
    xs, ws, bs, ys = _conv_specs(seq, col_off // cb, cb)
    return map_fwd(name, f_conv, (c // cb, t // seq), [src, w, b], [xs, ws, bs],
                   [jax.ShapeDtypeStruct((t, c), f32)], [ys])[0]


def conv_bwd(name, dy, src, col_off, w, b, seq, cb=256):
    t = src.shape[0]
    c = w.shape[1]
    xs, ws, bs, ys = _conv_specs(seq, col_off // cb, cb)

    def body(x_ref, w_ref, b_ref, dy_ref, dx_ref, dw_ref, db_ref):
        _, vjp = jax.vjp(lambda xx, ww, bb: f_conv(None, xx, ww, bb)[0], x_ref[...], w_ref[...], b_ref[...])
        dx, dw, db = vjp(dy_ref[...])
        dx_ref[...] = dx

        @pl.when(pl.program_id(1) == 0)
        def _():
            dw_ref[...] = jnp.zeros_like(dw_ref)
            db_ref[...] = jnp.zeros_like(db_ref)
        dw_ref[...] += dw
        db_ref[...] += db

    return pl.pallas_call(body, name=name, grid=(c // cb, t // seq), in_specs=[xs, ws, bs, ys], out_specs=[ys, ws, bs],
                          out_shape=[jax.ShapeDtypeStruct((t, c), f32), jax.ShapeDtypeStruct(w.shape, f32),
                                     jax.ShapeDtypeStruct(b.shape, f32)], compiler_params=_cp(2))(src, w, b, dy)


def f_ssd(pids, states, xs, dtraw, bm, cm, a_log, dt_bias, d_skip):
    g = pids[2]
    (hn,) = states
    l = xs.shape[0]
    head_of_lane = _iota((LANES, SSD_GW), 1) // SSD_HEAD_DIM + SSD_HG * g
    expand = (_iota((LANES, SSD_GW), 0) == head_of_lane).astype(f32)
    tri = _tri(l)
    dt = jax.nn.softplus(dtraw + dt_bias)
    adt = dt * (-jnp.exp(a_log))
    cs = hdot(tri, adt, exact="a")
    cst = cs.T
    cs_last = cs[l - 1:l, :]
    dt_e, cs_e, csl_e = hdot(dt, expand), hdot(cs, expand), hdot(cs_last, expand)
    xd = xs * dt_e
    gmat = bdot(cm, bm, 1, 1)
    half = _iota((l, LANES), 1) < SSD_HEAD_DIM
    blocks = []
    for pair in range(SSD_HG // 2):
        xb = xd[:, pair * LANES:(pair + 1) * LANES]
        res = []
        for sub in range(2):
            hid = SSD_HG * g + 2 * pair + sub
            col, row = _lane_pick(cs, hid), _row_pick(cst, hid)
            lm = jnp.exp(jnp.where(tri, col - row, NEG))
            res.append(bdot(gmat * lm, xb, 1, 0))
        blocks.append(jnp.where(half, res[0], res[1]))
    y = jnp.concatenate(blocks, axis=1)
    y = y + jnp.exp(cs_e) * bdot(cm, hn, 1, 0)
    y = y + hdot(d_skip, expand) * xs
    hn_new = jnp.exp(csl_e) * hn + bdot(bm, jnp.exp(csl_e - cs_e) * xd, 0, 0)
    return (y,), (hn_new,)


def _ssd_specs(seq, nch, rev):
    cc = (lambda c: nch - 1 - c) if rev else (lambda c: c)
    xs = pl.BlockSpec((CHUNK, SSD_GW), lambda b, c, g: (b * nch + cc(c), g))
    dt = pl.BlockSpec((CHUNK, LANES), lambda b, c, g: (b * nch + cc(c), OFF_DT // LANES))
    bm = pl.BlockSpec((CHUNK, SSD_STATE), lambda b, c, g: (b * nch + cc(c), 1024 // SSD_STATE + g))
    cm = pl.BlockSpec((CHUNK, SSD_STATE), lambda b, c, g: (b * nch + cc(c), 1024 // SSD_STATE + SSD_GROUPS + g))
    par = pl.BlockSpec((1, LANES), lambda b, c, g: (0, 0))
    sv = pl.BlockSpec((1, 1, SSD_STATE, SSD_GW), lambda b, c, g: (b * nch + cc(c), g, 0, 0))
    ddt = pl.BlockSpec((CHUNK, LANES), lambda b, c, g: (b * nch + cc(c), 0))
    dbc = pl.BlockSpec((CHUNK, SSD_STATE), lambda b, c, g: (b * nch + cc(c), g))
    return xs, dt, bm, cm, par, sv, ddt, dbc


def ssd_fwd(xbc, proj, a_log, dt_bias, d_skip, seq, ride=None):
    t = xbc.shape[0]
    nch = seq // CHUNK
    xs, dt, bm, cm, par, sv, _, _ = _ssd_specs(seq, nch, False)
    grid = (t // seq, nch, SSD_GROUPS)
    y, hsave, *got = scan_fwd("ssd_fwd", f_ssd, grid, 2, [xbc, proj, xbc, xbc, a_log, dt_bias, d_skip],
                              [xs, dt, bm, cm, par, par, par], [jax.ShapeDtypeStruct((t, SSD_GROUPS * SSD_GW), f32)], [xs],
                              [(SSD_STATE, SSD_GW)], [0.0],
                              [jax.ShapeDtypeStruct((t // CHUNK, SSD_GROUPS, SSD_STATE, SSD_GW), f32)], [sv], ride=ride)
    return y, hsave, got


def ssd_bwd(dy, xbc, proj, a_log, dt_bias, d_skip, hsave, seq):
    t = xbc.shape[0]
    nch = seq // CHUNK
    xs, dt, bm, cm, par, sv, ddt, dbc = _ssd_specs(seq, nch, True)
    grid = (t // seq, nch, SSD_GROUPS)

    def body(x_ref, dt_ref, b_ref, c_ref, al_ref, db_ref, ds_ref, h_ref, dy_ref,
             dxbc_x, dxbc_b, dxbc_c, ddt_ref, dal_ref, ddb_ref, dds_ref, dst):
        pids = tuple(pl.program_id(i) for i in range(3))
        slot = pids[2]

        @pl.when(pids[1] == 0)
        def _():
            dst[slot] = jnp.zeros(dst.shape[1:], f32)

        vals = [x_ref[...], dt_ref[...], b_ref[...], c_ref[...], al_ref[...], db_ref[...], ds_ref[...]]

        def gfun(st, *v):
            outs, new = f_ssd(pids, (st,), *v)
            return outs[0], new[0]

        _, vjp = jax.vjp(gfun, h_ref[0, 0], *vals)
        grads = vjp((dy_ref[...], dst[slot]))
        dst[slot] = grads[0]
        dxbc_x[...] = grads[1]
        dxbc_b[...] = grads[3]
        dxbc_c[...] = grads[4]

        @pl.when(slot == 0)
        def _():
            ddt_ref[...] = jnp.zeros_like(ddt_ref)
        ddt_ref[...] += grads[2]
        first = jnp.logical_and(jnp.logical_and(pids[0] == 0, pids[1] == 0), slot == 0)

        @pl.when(first)
        def _():
            dal_ref[...] = jnp.zeros_like(dal_ref)
            ddb_ref[...] = jnp.zeros_like(ddb_ref)
            dds_ref[...] = jnp.zeros_like(dds_ref)
        dal_ref[...] += grads[5]
        ddb_ref[...] += grads[6]
        dds_ref[...] += grads[7]

    bc_shape = jax.ShapeDtypeStruct((t, SSD_GROUPS * SSD_STATE), f32)
    par_shape = jax.ShapeDtypeStruct((1, LANES), f32)
    outs = pl.pallas_call(body, name="ssd_bwd", grid=grid, in_specs=[xs, dt, bm, cm, par, par, par, sv, xs],
                          out_specs=[xs, dbc, dbc, ddt, par, par, par],
                          out_shape=[jax.ShapeDtypeStruct((t, SSD_GROUPS * SSD_GW), f32),
                                     bc_shape, bc_shape, jax.ShapeDtypeStruct((t, LANES), f32),
                                     par_shape, par_shape, par_shape],
                          scratch_shapes=[pltpu.VMEM((SSD_GROUPS, SSD_STATE, SSD_GW), f32)],
                          compiler_params=_cp(3))(xbc, proj, xbc, xbc, a_log, dt_bias, d_skip, hsave, dy)
    return outs


def f_ssd_epi(pids, y, z, nw):
    yg = y * _silu(z)
    hw = yg.shape[1] // SSD_GROUPS
    parts = []
    for g in range(SSD_GROUPS):
        p = yg[:, g * hw:(g + 1) * hw]
        parts.append(p * lax.rsqrt(jnp.mean(p * p, axis=-1, keepdims=True) + EPS))
    return (jnp.concatenate(parts, axis=1) * nw,)


def f_ml_epi(pids, hm, xc, mz, nw, skip):
    parts = []
    for h in range(ML_HEADS):
        p = hm[:, h * ML_HD:(h + 1) * ML_HD]
        mu = jnp.mean(p, axis=-1, keepdims=True)
        var = jnp.mean(jnp.square(p - mu), axis=-1, keepdims=True)
        parts.append((p - mu) * lax.rsqrt(var + EPS))
    hn = jnp.concatenate(parts, axis=1) * nw
    return ((hn + skip * xc) * _silu(mz),)


def f_s5_post(pids, ys, u, d_skip):
    return (jax.nn.gelu(ys + d_skip * u),)


def f_glu(pids, pab, ba, bb):
    d = ba.shape[1]
    return ((pab[:, :d] + ba) * jax.nn.sigmoid(pab[:, d:] + bb),)


def f_glu_res(pids, pab, xres, ba, bb):
    return (xres + f_glu(pids, pab, ba, bb)[0],)


def rowwise_fwd(name, f, rows, row_cols, pars, out_width, tm=512):
    t = rows[0].shape[0]
    specs = [_row_spec(tm, w, c) for (w, c) in row_cols] + [_par_spec(p.shape) for p in pars]
    return map_fwd(name, f, (t // tm,), list(rows) + list(pars), specs, [jax.ShapeDtypeStruct((t, out_width), f32)],
                   [_row_spec(tm, out_width)])[0]


def rowwise_bwd(name, f, rows, row_cols, pars, dy, tm=256):
    t = rows[0].shape[0]
    n_r, n_p = len(rows), len(pars)
    specs = [_row_spec(tm, w, c) for (w, c) in row_cols] + [_par_spec(p.shape) for p in pars]
    out_w = dy.shape[1]

    def body(*refs):
        vals = [r[...] for r in refs[:n_r + n_p]]
        dy_ref = refs[n_r + n_p]
        outs = refs[n_r + n_p + 1:]
        _, vjp = jax.vjp(lambda *v: f(None, *v)[0], *vals)
        grads = vjp(dy_ref[...])
        for k in range(n_r):
            outs[k][...] = grads[k]

        @pl.when(pl.program_id(0) == 0)
        def _():
            for k in range(n_p):
                outs[n_r + k][...] = jnp.zeros_like(outs[n_r + k])
        for k in range(n_p):
            outs[n_r + k][...] += grads[n_r + k]

    out_shapes = [jax.ShapeDtypeStruct((t, w), f32) for (w, c) in row_cols] + [jax.ShapeDtypeStruct(p.shape, f32) for p in pars]
    out_specs = [_row_spec(tm, w) for (w, c) in row_cols] + [_par_spec(p.shape) for p in pars]
    return pl.pallas_call(body, name=name, grid=(t // tm,), in_specs=specs + [_row_spec(tm, out_w)], out_specs=out_specs,
                          out_shape=out_shapes, compiler_params=_cp(1))(*rows, *pars, dy)


def f_ml(pids, states, q, k, v, g1, g2, g3, b_if):
    h = pids[2]
    cst, nst, mst = states
    l = q.shape[0]
    gt = g1 + g2 + g3 + b_if
    k = k * (1.0 / math.sqrt(ML_HD))
    tri = _tri(l)
    bc_all = hdot(tri, jax.nn.log_sigmoid(gt), exact="a")
    bcum, ig = _lane_pick(bc_all, ML_HEADS + h), _lane_pick(gt, h)
    bcum_t, ig_t = _row_pick(bc_all.T, ML_HEADS + h), _row_pick(gt.T, h)
    b_last = bcum[l - 1:l, :]
    dlog = jnp.where(tri, bcum - bcum_t + ig_t, NEG)
    ws = b_last - bcum + ig
    m_prev = mst[:, 0:1]
    m_new = lax.stop_gradient(jnp.maximum(b_last + m_prev, jnp.max(ws, axis=0, keepdims=True)))
    decay = jnp.exp(b_last + m_prev - m_new)
    wts = jnp.exp(ws - m_new)
    c_new = decay * cst + bdot(wts * v, k, 0, 0)
    n_new = decay * nst + jnp.sum(wts * k, axis=0, keepdims=True)
    m_inter = bcum + m_prev
    m_t = lax.stop_gradient(jnp.maximum(jnp.max(dlog, axis=1, keepdims=True), m_inter))
    scores = bdot(q, k, 1, 1) * jnp.exp(dlog - m_t)
    inter_w = jnp.exp(m_inter - m_t)
    num = bdot(scores, v, 1, 0) + inter_w * bdot(q, cst, 1, 1)
    den = jnp.sum(scores, axis=1, keepdims=True) + inter_w * jnp.sum(q * nst, axis=1, keepdims=True)
    hout = num / jnp.maximum(jnp.abs(den), jnp.exp(-m_t))
    return (hout,), (c_new, n_new, jnp.broadcast_to(m_new, mst.shape))


def _ml_specs(nch, rev):
    cc = (lambda c: nch - 1 - c) if rev else (lambda c: c)
    hd = pl.BlockSpec((CHUNK, ML_HD), lambda b, c, h: (b * nch + cc(c), h))
    gt = pl.BlockSpec((CHUNK, LANES), lambda b, c, h: (b * nch + cc(c), 0))
    par = pl.BlockSpec((1, LANES), lambda b, c, h: (0, 0))
    sc = pl.BlockSpec((1, 1, ML_HD, ML_HD), lambda b, c, h: (b * nch + cc(c), h, 0, 0))
    sn = pl.BlockSpec((1, 1, 1, ML_HD), lambda b, c, h: (b * nch + cc(c), h, 0, 0))
    sm = pl.BlockSpec((1, 1, 1, LANES), lambda b, c, h: (b * nch + cc(c), h, 0, 0))
    return hd, gt, par, sc, sn, sm


ML_STATE_SHAPES = [(ML_HD, ML_HD), (1, ML_HD), (1, LANES)]


def ml_fwd(q, k, v, g1, g2, g3, b_if, seq, ride=None):
    t = q.shape[0]
    nch = seq // CHUNK
    hd, gt, par, sc, sn, sm = _ml_specs(nch, False)
    nc = t // CHUNK
    outs = scan_fwd("ml_fwd", f_ml, (t // seq, nch, ML_HEADS), 2, [q, k, v, g1, g2, g3, b_if],
                    [hd, hd, hd, gt, gt, gt, par], [jax.ShapeDtypeStruct((t, ML_HEADS * ML_HD), f32)], [hd],
                    ML_STATE_SHAPES, [0.0, 0.0, NEG],
                    [jax.ShapeDtypeStruct((nc, ML_HEADS, ML_HD, ML_HD), f32), jax.ShapeDtypeStruct((nc, ML_HEADS, 1, ML_HD), f32),
                     jax.ShapeDtypeStruct((nc, ML_HEADS, 1, LANES), f32)], [sc, sn, sm], ride=ride)
    return outs[0], outs[1:4], outs[4:]


def ml_bwd(dh, q, k, v, g1, g2, g3, b_if, saves, seq, ride=None):
    t = q.shape[0]
    nch = seq // CHUNK
    hd, gt, par, sc, sn, sm = _ml_specs(nch, True)
    rider = Rider(ride)
    nr = rider.n
    grid = (t // seq, nch, ML_HEADS)

    def f(pids, states, q, k, v, gsum, b_if):
        return f_ml(pids, states, q, k, v, gsum, jnp.zeros_like(gsum), jnp.zeros_like(gsum), b_if)

    def body(*refs):
        q_ref, k_ref, v_ref, g1_ref, g2_ref, g3_ref, b_ref, c_ref, n_ref, m_ref, dh_ref = refs[:11]
        r_ins = refs[11:11 + nr]
        dq_ref, dk_ref, dv_ref, dg_ref, db_ref = refs[11 + nr:16 + nr]
        r_outs = refs[16 + nr:16 + 2 * nr]
        dc_s, dn_s = refs[16 + 2 * nr:18 + 2 * nr]
        sems = refs[18 + 2 * nr:]
        rider.start(grid, r_ins, r_outs, sems)
        pids = tuple(pl.program_id(i) for i in range(3))
        slot = pids[2]

        @pl.when(pids[1] == 0)
        def _():
            dc_s[slot] = jnp.zeros(dc_s.shape[1:], f32)
            dn_s[slot] = jnp.zeros(dn_s.shape[1:], f32)

        gsum = g1_ref[...] + g2_ref[...] + g3_ref[...]
        mst = m_ref[0, 0]

        def gfun(cst, nst, qq, kk, vv, gs, bb):
            outs, new = f(pids, (cst, nst, mst), qq, kk, vv, gs, bb)
            return outs[0], new[0], new[1]

        _, vjp = jax.vjp(gfun, c_ref[0, 0], n_ref[0, 0], q_ref[...], k_ref[...], v_ref[...], gsum, b_ref[...])
        grads = vjp((dh_ref[...], dc_s[slot], dn_s[slot]))
        dc_s[slot] = grads[0]
        dn_s[slot] = grads[1]
        dq_ref[...] = grads[2]
        dk_ref[...] = grads[3]
        dv_ref[...] = grads[4]

        @pl.when(slot == 0)
        def _():
            dg_ref[...] = jnp.zeros_like(dg_ref)
        dg_ref[...] += grads[5]
        first = jnp.logical_and(jnp.logical_and(pids[0] == 0, pids[1] == 0), slot == 0)

        @pl.when(first)
        def _():
            db_ref[...] = jnp.zeros_like(db_ref)
        db_ref[...] += grads[6]
        rider.wait(grid, r_ins, r_outs, sems)

    big = jax.ShapeDtypeStruct((t, ML_HEADS * ML_HD), f32)
    return pl.pallas_call(body, name="ml_bwd", grid=grid,
                          in_specs=[hd, hd, hd, gt, gt, gt, par, sc, sn, sm, hd] + rider.specs(),
                          out_specs=[hd, hd, hd, gt, par] + rider.specs(),
                          out_shape=[big, big, big, jax.ShapeDtypeStruct((t, LANES), f32), jax.ShapeDtypeStruct((1, LANES), f32)]
                          + rider.out_shapes(),
                          scratch_shapes=[pltpu.VMEM((ML_HEADS, ML_HD, ML_HD), f32), pltpu.VMEM((ML_HEADS, 1, ML_HD), f32)]
                          + rider.scratch(),
                          compiler_params=_cp(3))(q, k, v, g1, g2, g3, b_if, *saves, dh, *rider.arrays())


def _block_prefix(z, transpose):
    n = z.shape[0]
    r, c = _iota((n, n), 0), _iota((n, n), 1)
    keep = jnp.logical_and(r // S5_SUB == c // S5_SUB, (c >= r) if transpose else (c <= r))
    m = jnp.where(keep, 1.0, 0.0).astype(bf16)
    hi = z.astype(bf16)
    lo = (z - hi.astype(f32)).astype(bf16)
    return jnp.dot(m, hi, preferred_element_type=f32) + jnp.dot(m, lo, preferred_element_type=f32)


@jax.custom_vjp
def block_prefix(z):
    return _block_prefix(z, False)


block_prefix.defvjp(lambda z: (_block_prefix(z, False), None), lambda _, ct: (_block_prefix(ct, True),))


def _cmul(a, b):
    h = b.shape[1] // 2
    ar, ai, br, bi = a[:, :h], a[:, h:], b[:, :h], b[:, h:]
    return jnp.concatenate([ar * br - ai * bi, ar * bi + ai * br], axis=1)


def f_s5(pids, states, u, bb, cc, tab):
    (carry,) = states
    tl = u.shape[0]
    nsub = tl // S5_SUB
    rep = lambda t: jnp.concatenate([t] * nsub, axis=0)
    p0, q0 = tab[S5_SUB:2 * S5_SUB], tab[2 * S5_SUB:3 * S5_SUB]
    lam, p0_last = tab[0:1], tab[2 * S5_SUB - 1:2 * S5_SUB]
    bu = bdot(u, bb, 1, 0)
    pre = block_prefix(_cmul(rep(q0), bu))
    e, entering = carry, []
    for k in range(nsub):
        le = _cmul(lam, e)
        entering.append(jnp.broadcast_to(le, (S5_SUB, le.shape[1])))
        e = _cmul(p0_last, pre[(k + 1) * S5_SUB - 1:(k + 1) * S5_SUB] + le)
    x = _cmul(rep(p0), pre + jnp.concatenate(entering, axis=0))
    y = bdot(x, cc, 1, 0)
    return (y,), (e,)


def _s5_specs(ntl, rev):
    tt = (lambda t: ntl - 1 - t) if rev else (lambda t: t)
    us = pl.BlockSpec((S5_TL, LANES), lambda c, b, t: (b * ntl + tt(t), c))
    bbs = pl.BlockSpec((1, LANES, 2 * S5_CH), lambda c, b, t: (c, 0, 0))
    ccs = pl.BlockSpec((1, 2 * S5_CH, LANES), lambda c, b, t: (c, 0, 0))
    pws = pl.BlockSpec((1, 3 * S5_SUB, 2 * S5_CH), lambda c, b, t: (c, 0, 0))
    sv = pl.BlockSpec((1, 1, 1, 2 * S5_CH), lambda c, b, t: (b * ntl + tt(t), c, 0, 0))
    return us, bbs, ccs, pws, sv


def s5_fwd(u, bb, cc, pw, seq):
    t = u.shape[0]
    ntl = seq // S5_TL
    us, bbs, ccs, pws, sv = _s5_specs(ntl, False)

    def f(pids, states, uu, b3, c3, p3):
        return f_s5(pids, states, uu, b3[0], c3[0], p3[0])

    y, carries = scan_fwd("s5_fwd", f, (S5_CB, t // seq, ntl), 0, [u, bb, cc, pw], [us, bbs, ccs, pws],
                          [jax.ShapeDtypeStruct((t, S5_CB * LANES), f32)], [us], [(1, 2 * S5_CH)], [0.0],
                          [jax.ShapeDtypeStruct((t // S5_TL, S5_CB, 1, 2 * S5_CH), f32)], [sv])
    return y, carries


def s5_bwd(dy, u, bb, cc, pw, carries, seq):
    t = u.shape[0]
    ntl = seq // S5_TL
    us, bbs, ccs, pws, sv = _s5_specs(ntl, True)

    def f(pids, states, uu, b3, c3, p3):
        return f_s5(pids, states, uu, b3[0], c3[0], p3[0])

    first = lambda pids: jnp.logical_and(pids[1] == 0, pids[2] == 0)
    return scan_bwd("s5_bwd", f, (S5_CB, t // seq, ntl), 0, [u, bb, cc, pw], [us, bbs, ccs, pws], [carries], [sv],
                    [dy], [us], [(1, 2 * S5_CH)], [0, 1, 2, 3], {1: first, 2: first, 3: first})


def _adam_math(g, w, m, v):
    m2 = ADAM_B1 * m + (1.0 - ADAM_B1) * g
    v2 = ADAM_B2 * v + (1.0 - ADAM_B2) * jnp.square(g)
    m_hat = m2 / (1.0 - ADAM_B1 ** ADAM_STEP)
    v_hat = v2 / (1.0 - ADAM_B2 ** ADAM_STEP)
    delta = -ADAM_LR * (m_hat / (jnp.sqrt(v_hat) + ADAM_EPS) + ADAM_WD * w)
    return delta, m2, v2


def adamw(name, parts, w, m, v, tr=256):
    n, r, c = parts.shape
    tr = min(tr, r)
    assert r % tr == 0

    def body(p_ref, w_ref, m_ref, v_ref, g_ref, d_ref, m2_ref, v2_ref):
        g = p_ref[0].astype(f32)
        for s in range(1, n):
            g = g + p_ref[s].astype(f32)
        d, m2, v2 = _adam_math(g, w_ref[...], m_ref[...], v_ref[...])
        g_ref[...] = g
        d_ref[...] = d
        m2_ref[...] = m2
        v2_ref[...] = v2

    ps = pl.BlockSpec((n, tr, c), lambda i: (0, i, 0))
    rs = pl.BlockSpec((tr, c), lambda i: (i, 0))
    return pl.pallas_call(body, name=name, grid=(r // tr,), in_specs=[ps, rs, rs, rs], out_specs=[rs] * 4,
                          out_shape=[jax.ShapeDtypeStruct((r, c), f32)] * 4, compiler_params=_cp(1))(parts, w, m, v)


def adamw_layer(name, parts, w, m, v, layer, prev=None, tr=256):
    n, r, c = parts.shape
    nl = w.shape[0]
    tr = min(tr, r)
    assert r % tr == 0 and w.shape[1:] == (r, c)
    n_prev = 0 if prev is None else 4

    def body(*refs):
        p_ref, w_ref, m_ref, v_ref = refs[:4]
        g_ref, d_ref, m2_ref, v2_ref = refs[4 + n_prev:]
        g = p_ref[0].astype(f32)
        for s in range(1, n):
            g = g + p_ref[s].astype(f32)
        d, m2, v2 = _adam_math(g, w_ref[0], m_ref[0], v_ref[0])
        g_ref[0] = g
        d_ref[0] = d
        m2_ref[0] = m2
        v2_ref[0] = v2

    ps = pl.BlockSpec((n, tr, c), lambda i: (0, i, 0))
    rs = pl.BlockSpec((1, tr, c), lambda i: (layer, i, 0))
    anyspec = pl.BlockSpec(memory_space=pl.ANY)
    return pl.pallas_call(body, name=name, grid=(r // tr,), in_specs=[ps, rs, rs, rs] + [anyspec] * n_prev, out_specs=[rs] * 4,
                          out_shape=[jax.ShapeDtypeStruct((nl, r, c), f32)] * 4,
                          input_output_aliases={4 + i: i for i in range(n_prev)},
                          compiler_params=_cp(1))(parts, w, m, v, *(prev or ()))


def sum_parts(name, parts, tr=256):
    n, r, c = parts.shape
    tr = min(tr, r)
    assert r % tr == 0

    def body(p_ref, o_ref):
        g = p_ref[0].astype(f32)
        for s in range(1, n):
            g = g + p_ref[s].astype(f32)
        o_ref[...] = g

    return pl.pallas_call(body, name=name, grid=(r // tr,), in_specs=[pl.BlockSpec((n, tr, c), lambda i: (0, i, 0))],
                          out_specs=pl.BlockSpec((tr, c), lambda i: (i, 0)),
                          out_shape=jax.ShapeDtypeStruct((r, c), f32), compiler_params=_cp(1))(parts)


class Rider:
    def __init__(self, ops):
        self.ops = list(ops or [])
        self.n = len(self.ops)

    def arrays(self):
        return [a for a, _ in self.ops]

    def specs(self):
        return [pl.BlockSpec(memory_space=pl.ANY)] * self.n

    def out_shapes(self):
        return [jax.ShapeDtypeStruct((N_DEV,) + tuple(a.shape) if mode == "gather" else tuple(a.shape), a.dtype)
                for a, mode in self.ops]

    def scratch(self):
        if not self.n:
            return []
        return [pltpu.SemaphoreType.DMA((self.n, N_DEV - 1)), pltpu.SemaphoreType.DMA((self.n, N_DEV - 1)),
                pltpu.SemaphoreType.DMA((self.n,))]

    def _copies(self, ins, outs, sems, with_relays=True):
        send_sems, recv_sems, loc_sems = sems
        x, y, c = lax.axis_index("x"), lax.axis_index("y"), lax.axis_index("c")
        me = 4 * x + 2 * y + c
        first, crossing, relays = [], [], []

        def remote(src, dst, k, idx, peer):
            return pltpu.make_async_remote_copy(src_ref=src, dst_ref=dst, send_sem=send_sems.at[k, idx], recv_sem=recv_sems.at[k, idx],
                                                device_id=peer, device_id_type=pl.DeviceIdType.MESH)

        for k, (_, mode) in enumerate(self.ops):
            src_me = ins[k] if mode == "gather" else ins[k].at[me]
            first.append(pltpu.make_async_copy(src_me, outs[k].at[me], loc_sems.at[k]))
            if mode == "scatter":
                for d in range(1, N_DEV):
                    px = 1 - x if (d >> 2) & 1 else x
                    py = 1 - y if (d >> 1) & 1 else y
                    pc = 1 - c if d & 1 else c
                    first.append(remote(ins[k].at[4 * px + 2 * py + pc], outs[k].at[me], k, d - 1, (px, py, pc)))
            else:
                first.append(remote(ins[k], outs[k].at[me], k, 0, (x, y, 1 - c)))
                for q in range(1, 4):
                    px = 1 - x if (q >> 1) & 1 else x
                    py = 1 - y if q & 1 else y
                    crossing.append(remote(ins[k], outs[k].at[me], k, q, (px, py, c)))
                    if with_relays:
                        block = outs[k].at[4 * px + 2 * py + c]
                        relays.append(remote(block, block, k, 3 + q, (x, y, 1 - c)))
        return first, crossing, relays

    def _start(self, ins, outs, sems):
        first, crossing, _ = self._copies(ins, outs, sems, with_relays=False)
        for cp in first + crossing:
            cp.start()

    def _finish(self, ins, outs, sems):
        first, crossing, relays = self._copies(ins, outs, sems)
        for cp, relay in zip(crossing, relays):
            cp.wait_recv()
            relay.start()
        for cp in first + relays:
            cp.wait()
        for cp in crossing:
            cp.wait_send()

    def start(self, grid, ins, outs, sems):
        if self.n:
            @pl.when(functools.reduce(jnp.logical_and, [pl.program_id(i) == 0 for i in range(len(grid))]))
            def _():
                self._start(ins, outs, sems)

    def wait(self, grid, ins, outs, sems):
        if self.n:
            @pl.when(functools.reduce(jnp.logical_and, [pl.program_id(i) == g - 1 for i, g in enumerate(grid)]))
            def _():
                self._finish(ins, outs, sems)


def exchange(name, ops):
    rider = Rider(ops)
    n = rider.n

    def body(*refs):
        rider._start(refs[:n], refs[n:2 * n], refs[2 * n:])
        rider._finish(refs[:n], refs[n:2 * n], refs[2 * n:])

    return pl.pallas_call(body, name=name, in_specs=rider.specs(), out_specs=rider.specs(), out_shape=rider.out_shapes(),
                          scratch_shapes=rider.scratch())(*rider.arrays())


def _lanes(v, width=LANES):
    v = v.reshape(1, -1)
    return jnp.pad(v, ((0, 0), (0, width - v.shape[1])))


def win_to_padded(w):
    return jnp.concatenate([w[:, :1024], w[:, 2576:3600], w[:, 3600:4624], w[:, 1024:2560], w[:, 2560:2576],
                            jnp.zeros((w.shape[0], PROJ_W - IN_COLS), w.dtype)], axis=1)


def win_from_padded(wp):
    return jnp.concatenate([wp[:, 0:1024], wp[:, 3072:4608], wp[:, 4608:4624], wp[:, 1024:2048], wp[:, 2048:3072]], axis=1)


def headwise_dense(w):
    nb, o, i = w.shape
    rows = jnp.tile(w.transpose(0, 2, 1).reshape(nb * i, o), (1, nb))
    same = (jnp.arange(nb * i)[:, None] // i) == (jnp.arange(nb * o)[None, :] // o)
    return jnp.where(same, rows, 0.0)


def diag_blocks(name, dd, blk, tm=256):
    n = dd.shape[0]

    def body(d_ref, o_ref):
        rows = _iota((tm, n), 0) + pl.program_id(0) * tm
        masked = jnp.where(rows // blk == _iota((tm, n), 1) // blk, d_ref[...], 0.0)
        sel = (_iota((n, LANES), 0) % blk == _iota((n, LANES), 1)).astype(f32)
        o_ref[...] = hdot(masked, sel)

    return pl.pallas_call(body, name=name, grid=(n // tm,), in_specs=[pl.BlockSpec((tm, n), lambda i: (i, 0))],
                          out_specs=pl.BlockSpec((tm, LANES), lambda i: (i, 0)),
                          out_shape=jax.ShapeDtypeStruct((n, LANES), f32), compiler_params=_cp(1))(dd)


def headwise_from_dense(name, dd, o=4, i=4):
    nb = dd.shape[0] // i
    return diag_blocks(name, dd, i)[:, :o].reshape(nb, i, o).transpose(0, 2, 1)


def s5_tables(a_re, a_im, log_step, b_re, b_im, c_re, c_im):
    step = jnp.exp(log_step)[:, None]
    j = jnp.arange(S5_SUB, dtype=f32)[:, None, None]
    expo = jnp.concatenate([j + 1.0, j, -j], axis=0)
    mag = jnp.exp(expo * (a_re * step))
    pw_re, pw_im = mag * jnp.cos(expo * (a_im * step)), mag * jnp.sin(expo * (a_im * step))
    lam_re, lam_im = pw_re[0], pw_im[0]
    den = a_re * a_re + a_im * a_im
    coef_re = ((lam_re - 1.0) * a_re + lam_im * a_im) / den
    coef_im = (lam_im * a_re - (lam_re - 1.0) * a_im) / den
    bb_re = coef_re[..., None] * b_re - coef_im[..., None] * b_im
    bb_im = coef_re[..., None] * b_im + coef_im[..., None] * b_re
    gl = S5_GROUPS // S5_CB
    eye = jnp.eye(gl, dtype=f32)

    def blk_b(t):
        t4 = t.transpose(0, 2, 1).reshape(S5_CB, gl, S5_GROUP, S5_STATE)
        return jnp.einsum("kgcn,gh->kgchn", t4, eye).reshape(S5_CB, gl * S5_GROUP, gl * S5_STATE)

    def blk_c(t):
        t4 = t.reshape(S5_CB, gl, S5_GROUP, S5_STATE)
        return jnp.einsum("kgcn,gh->kgnhc", t4, eye).reshape(S5_CB, gl * S5_STATE, gl * S5_GROUP)

    def blk_p(t):
        return t.reshape(t.shape[0], S5_CB, gl * S5_STATE).transpose(1, 0, 2)

    bb = jnp.concatenate([blk_b(bb_re), blk_b(bb_im)], axis=2)
    cc = jnp.concatenate([blk_c(c_re), -blk_c(c_im)], axis=1)
    pw = jnp.concatenate([blk_p(pw_re), blk_p(pw_im)], axis=2)
    return bb, cc, pw


def ffn_step_bwd(dy, x, nw, wts, saved, ride_act=None, ride_w=None):
    h, g, u = saved
    dx, dnw, dg, du, a, dyh, *got_act = ffn_bwd_act(dy, x, nw, g, u, *wts, ride=ride_act)
    dwg, dwu, dwd, *got_w = ffn_bwd_w(h, dyh, dg, du, a, ride=ride_w)
    return dx, dnw, (dwg, dwu, dwd), got_act, got_w


def hybrid_fwd(x1, p, seq, ride_in, ride_ssd, ride_ml):
    u = rms_fwd(x1, p["mix_norm"])
    proj, *got_in = matmul("hy_in", u, p["win"], ride=ride_in)
    xbc = conv_fwd("ssd_conv", proj, OFF_XBC, p["ssd_conv_w"], p["ssd_conv_b"], seq)
    yraw, hsave, got_ssd = ssd_fwd(xbc, proj, p["a_log"], p["dt_bias"], p["ssd_d"], seq, ride=ride_ssd)
    yssd = rowwise_fwd("ssd_epi", f_ssd_epi, [yraw, proj], [(D_MODEL, 0), (D_MODEL, OFF_Z // D_MODEL)], [p["ssd_norm_w"]], D_MODEL)
    xc = conv_fwd("ml_conv", proj, OFF_MX, p["ml_conv_w"], p["ml_conv_b"], seq)
    q = matmul("hw_q", xc, p["wq"])
    k = matmul("hw_k", xc, p["wk"])
    v = matmul("hw_v", proj, p["wv"], a_off=OFF_MX, a_width=D_MODEL)
    g1 = matmul("gate_q", q, p["wif_q"])
    g2 = matmul("gate_k", k, p["wif_k"])
    g3 = matmul("gate_v", v, p["wif_v"])
    hm, mlsave, got_ml = ml_fwd(q, k, v, g1, g2, g3, p["b_if"], seq, ride=ride_ml)
    yml = rowwise_fwd("ml_epi", f_ml_epi, [hm, xc, proj], [(D_MODEL, 0), (D_MODEL, 0), (D_MODEL, OFF_MZ // D_MODEL)],
                      [p["ml_norm_w"], p["ml_skip"]], D_MODEL)
    t = matmul("hy_out1", yssd, p["wo1"], add=x1)
    x2 = matmul("hy_out2", yml, p["wo2"], add=t)
    return x2, (u, proj, xbc, yraw, hsave, yssd, xc, q, k, v, g1, g2, g3, hm, mlsave, yml), got_in, got_ssd, got_ml


def hybrid_bwd(dx2, x1, p, saved, seq, ride_ml, ride_du):
    u, proj, xbc, yraw, hsave, yssd, xc, q, k, v, g1, g2, g3, hm, mlsave, yml = saved
    gr = {}
    dyssd = matmul("d_yssd", dx2, p["wo1"], cb=1)
    dyml = matmul("d_yml", dx2, p["wo2"], cb=1)
    gr["wo"] = jnp.concatenate([matmul("dw_o1", yssd, dx2, ca=0), matmul("dw_o2", yml, dx2, ca=0)], axis=0)
    d_hm, d_xc, d_mz, gr["ml_norm_w"], gr["ml_skip"] = rowwise_bwd(
        "ml_epi_bwd", f_ml_epi, [hm, xc, proj], [(D_MODEL, 0), (D_MODEL, 0), (D_MODEL, OFF_MZ // D_MODEL)],
        [p["ml_norm_w"], p["ml_skip"]], dyml)
    dq, dk, dv, dgt, gr["b_if"], *got_ml = ml_bwd(d_hm, q, k, v, g1, g2, g3, p["b_if"], mlsave, seq, ride=ride_ml)
    dq = matmul("dq_gate", dgt, p["wif_q"], cb=1, add=dq)
    dk = matmul("dk_gate", dgt, p["wif_k"], cb=1, add=dk)
    dv = matmul("dv_gate", dgt, p["wif_v"], cb=1, add=dv)
    gr["wif"] = jnp.concatenate([matmul("dw_if_q", q, dgt, ca=0), matmul("dw_if_k", k, dgt, ca=0),
                                 matmul("dw_if_v", v, dgt, ca=0)], axis=0)
    d_xc = matmul("dxc_q", dq, p["wq"], cb=1, add=d_xc)
    d_xc = matmul("dxc_k", dk, p["wk"], cb=1, add=d_xc)
    gr["wq"] = matmul("dw_q", xc, dq, ca=0)
    gr["wk"] = matmul("dw_k", xc, dk, ca=0)
    gr["wv"] = matmul("dw_v", proj, dv, ca=0, a_off=OFF_MX, a_width=D_MODEL)
    d_mx, gr["ml_conv_w"], gr["ml_conv_b"] = conv_bwd("ml_conv_bwd", d_xc, proj, OFF_MX, p["ml_conv_w"], p["ml_conv_b"], seq)
    d_mx = matmul("dmx_v", dv, p["wv"], cb=1, add=d_mx)
    d_yraw, d_z, gr["ssd_norm_w"] = rowwise_bwd("ssd_epi_bwd", f_ssd_epi, [yraw, proj],
                                                [(D_MODEL, 0), (D_MODEL, OFF_Z // D_MODEL)], [p["ssd_norm_w"]], dyssd)
    d_xs, d_b, d_c, d_dt, gr["a_log"], gr["dt_bias"], gr["ssd_d"] = ssd_bwd(
        d_yraw, xbc, proj, p["a_log"], p["dt_bias"], p["ssd_d"], hsave, seq)
    d_xbc, gr["ssd_conv_w"], gr["ssd_conv_b"] = conv_bwd("ssd_conv_bwd", jnp.concatenate([d_xs, d_b, d_c], axis=1), proj, OFF_XBC,
                                                         p["ssd_conv_w"], p["ssd_conv_b"], seq)
    dproj = jnp.concatenate([d_z, d_mx, d_mz, d_xbc, d_dt, jnp.zeros((d_dt.shape[0], PROJ_W - OFF_DT - LANES), f32)], axis=1)
    gr["win"] = matmul("dw_in", u.astype(bf16).T, dproj, tiles=(D_MODEL, PROJ_W // 2, min(512, u.shape[0])))
    du, *got_du = matmul("d_u", dproj, p["win"], cb=1, ride=ride_du)
    dx1, gr["mix_norm"] = rms_bwd([du], x1, p["mix_norm"], dx2)
    return dx1, gr, got_ml, got_du


def s5_layer_fwd(x4, p, seq):
    u = rms_fwd(x4, p["mix_norm"])
    ys, carries = s5_fwd(u, p["bb"], p["cc"], p["pw"], seq)
    gg = rowwise_fwd("s5_post", f_s5_post, [ys, u], [(D_MODEL, 0), (D_MODEL, 0)], [p["s5_d"]], D_MODEL)
    pab = matmul("s5_ab", gg, p["wab"])
    x5 = rowwise_fwd("s5_glu", f_glu_res, [pab, x4], [(2 * D_MODEL, 0), (D_MODEL, 0)], [p["b_a"], p["b_b"]], D_MODEL)
    return x5, (u, ys, carries, gg, pab)


def s5_layer_bwd(dx5, x4, p, saved, seq):
    u, ys, carries, gg, pab = saved
    gr = {}
    dpab, gr["b_a"], gr["b_b"] = rowwise_bwd("s5_glu_bwd", f_glu, [pab], [(2 * D_MODEL, 0)], [p["b_a"], p["b_b"]], dx5)
    dgg = matmul("d_gg", dpab, p["wab"], cb=1)
    gr["wab"] = matmul("dw_ab", gg, dpab, ca=0)
    dys, du_a, gr["s5_d"] = rowwise_bwd("s5_post_bwd", f_s5_post, [ys, u], [(D_MODEL, 0), (D_MODEL, 0)], [p["s5_d"]], dgg)
    du_b, gr["bb"], gr["cc"], gr["pw"] = s5_bwd(dys, u, p["bb"], p["cc"], p["pw"], carries, seq)
    dx4, gr["mix_norm"] = rms_bwd([du_a, du_b], x4, p["mix_norm"], dx5)
    return dx4, gr


BIG = ["ffn1_w_gate", "ffn1_w_up", "ffn1_w_down", "ffn2_w_gate", "ffn2_w_up", "ffn2_w_down", "hy_w_in", "hy_w_out", "s5_w_a", "s5_w_b"]
SMALL_SHARDED = {"ssd_conv_w": 2, "ml_conv_w": 2, "ml_w_q": 1, "ml_w_k": 1, "ml_w_v": 1, "ml_w_if": 1, "s5_d": 1, "s5_b_a": 1, "s5_b_b": 1}
WEIGHTS = ["ffn1_norm", "ffn1_w_gate", "ffn1_w_up", "ffn1_w_down", "mix_norm", "ffn2_norm", "ffn2_w_gate", "ffn2_w_up", "ffn2_w_down",
           "hy_w_in", "ssd_conv_w", "ssd_conv_b", "ssd_dt_bias", "ssd_a_log", "ssd_d", "ssd_norm_w", "ml_conv_w", "ml_conv_b",
           "ml_w_q", "ml_w_k", "ml_w_v", "ml_w_if", "ml_b_if", "ml_norm_w", "ml_skip", "hy_w_out", "s5_a_re", "s5_a_im",
           "s5_log_step", "s5_b_re", "s5_b_im", "s5_c_re", "s5_c_im", "s5_d", "s5_w_a", "s5_b_a", "s5_w_b", "s5_b_b", "final_norm"]
S5_PARAMS = ["s5_a_re", "s5_a_im", "s5_log_step", "s5_b_re", "s5_b_im", "s5_c_re", "s5_c_im"]
SMALL_S5 = S5_PARAMS + ["s5_d", "s5_b_a", "s5_b_b"]
SMALL_REST = [n for n in WEIGHTS if n not in BIG and n not in SMALL_S5]
SMALL = SMALL_REST + SMALL_S5


def _unshard(g, axis):
    return jnp.concatenate([g[i] for i in range(N_DEV)], axis=axis)


def assemble_hybrid(gw, rep):
    padn = lambda w: jnp.pad(w, ((0, 0), (0, LANES - w.shape[1]))).astype(bf16)
    wif = _unshard(gw["ml_w_if"], 1)[0]
    wo = _unshard(gw["hy_w_out"], 1)[0].astype(bf16)
    dense = lambda n: headwise_dense(_unshard(gw[n], 1)[0].astype(f32)).astype(bf16)
    w0 = dict(mix_norm=rep["mix_norm"][0:1],
              win=win_to_padded(_unshard(gw["hy_w_in"], 2)[0]).astype(bf16),
              ssd_conv_w=_unshard(gw["ssd_conv_w"], 2)[0], ssd_conv_b=rep["ssd_conv_b"],
              a_log=_lanes(rep["ssd_a_log"]), dt_bias=_lanes(rep["ssd_dt_bias"]), ssd_d=_lanes(rep["ssd_d"]),
              ssd_norm_w=rep["ssd_norm_w"], ml_conv_w=_unshard(gw["ml_conv_w"], 2)[0], ml_conv_b=rep["ml_conv_b"],
              wq=dense("ml_w_q"), wk=dense("ml_w_k"), wv=dense("ml_w_v"),
              wif_q=padn(wif[0:1024]), wif_k=padn(wif[1024:2048]), wif_v=padn(wif[2048:3072]),
              b_if=_lanes(rep["ml_b_if"]), ml_norm_w=rep["ml_norm_w"], ml_skip=rep["ml_skip"],
              wo1=wo[:D_MODEL], wo2=wo[D_MODEL:])
    return w0


def assemble_s5(gw, rep):
    bb, cc, pw = s5_tables(*[rep[n][0] for n in S5_PARAMS])
    wab = jnp.concatenate([_unshard(gw["s5_w_a"], 1)[0], _unshard(gw["s5_w_b"], 1)[0]], axis=1).astype(bf16)
    return dict(mix_norm=rep["mix_norm"][1:2], bb=bb, cc=cc, pw=pw,
                s5_d=_unshard(gw["s5_d"], 1), wab=wab, b_a=_unshard(gw["s5_b_a"], 1), b_b=_unshard(gw["s5_b_b"], 1))


def _shards(full, axis):
    return jnp.stack(jnp.split(full, N_DEV, axis=axis), axis=0)


def small_grads(g_norms, g_hy, g_s5, d_final, rep):
    small = dict(g_norms)
    small["mix_norm"] = jnp.concatenate([g_hy["mix_norm"], g_s5["mix_norm"]], axis=0)
    small["ssd_conv_w"] = g_hy["ssd_conv_w"][None]
    small["ssd_conv_b"] = g_hy["ssd_conv_b"]
    small["ssd_dt_bias"] = g_hy["dt_bias"][:, :SSD_HEADS]
    small["ssd_a_log"] = g_hy["a_log"][:, :SSD_HEADS]
    small["ssd_d"] = g_hy["ssd_d"][:, :SSD_HEADS]
    small["ssd_norm_w"] = g_hy["ssd_norm_w"]
    small["ml_conv_w"] = g_hy["ml_conv_w"][None]
    small["ml_conv_b"] = g_hy["ml_conv_b"]
    for nm, key in (("ml_w_q", "wq"), ("ml_w_k", "wk"), ("ml_w_v", "wv")):
        small[nm] = headwise_from_dense("diag_" + key, g_hy[key])[None]
    small["ml_w_if"] = g_hy["wif"][None, :, :2 * ML_HEADS]
    small["ml_b_if"] = g_hy["b_if"][:, :2 * ML_HEADS]
    small["ml_norm_w"] = g_hy["ml_norm_w"]
    small["ml_skip"] = g_hy["ml_skip"]
    small["final_norm"] = d_final.reshape(-1)
    return small


def s5_small_grads(g_s5, rep):
    small = {}
    _, tvjp = jax.vjp(s5_tables, *[rep[n][0] for n in S5_PARAMS])
    for n, g in zip(S5_PARAMS, tvjp((g_s5["bb"], g_s5["cc"], g_s5["pw"]))):
        small[n] = g[None]
    small["s5_d"] = g_s5["s5_d"]
    small["s5_b_a"] = g_s5["b_a"]
    small["s5_b_b"] = g_s5["b_b"]
    return small


ROW = 1024
F32_ROWS = 8


def _piece_rows(size):
    return -(-size // (ROW * F32_ROWS)) * F32_ROWS


def _pack(arrays):
    pieces = []
    for a in arrays:
        flat = a.astype(f32).reshape(-1)
        pieces.append(jnp.pad(flat, (0, _piece_rows(a.size) * ROW - a.size)).reshape(-1, ROW))
    return jnp.concatenate(pieces, axis=0)


def _unpack(buf, shapes):
    out, r0 = [], 0
    lead = buf.shape[:-2]
    for shp in shapes:
        size = math.prod(shp)
        r = _piece_rows(size)
        out.append(buf[..., r0:r0 + r, :].reshape(lead + (-1,))[..., :size].reshape(lead + tuple(shp)))
        r0 += r
    return out


ADAM_BLOCK_ELEMS = 500_000


def _tile_rows(r, c):
    cap = ADAM_BLOCK_ELEMS // (-(-c // LANES) * LANES)
    if r <= cap:
        return r
    return max(t for t in range(F32_ROWS, cap + 1, F32_ROWS) if r % t == 0)


def _flat2d(a):
    return a.reshape(-1, a.shape[-1])


def kernel(x, ffn1_norm, ffn1_w_gate, ffn1_w_up, ffn1_w_down, mix_norm, ffn2_norm, ffn2_w_gate, ffn2_w_up, ffn2_w_down, hy_w_in, ssd_conv_w, ssd_conv_b, ssd_dt_bias, ssd_a_log, ssd_d, ssd_norm_w, ml_conv_w, ml_conv_b, ml_w_q, ml_w_k, ml_w_v, ml_w_if, ml_b_if, ml_norm_w, ml_skip, hy_w_out, s5_a_re, s5_a_im, s5_log_step, s5_b_re, s5_b_im, s5_c_re, s5_c_im, s5_d, s5_w_a, s5_b_a, s5_w_b, s5_b_b, final_norm, loss_target, m_ffn1_norm, m_ffn1_w_gate, m_ffn1_w_up, m_ffn1_w_down, m_mix_norm, m_ffn2_norm, m_ffn2_w_gate, m_ffn2_w_up, m_ffn2_w_down, m_hy_w_in, m_ssd_conv_w, m_ssd_conv_b, m_ssd_dt_bias, m_ssd_a_log, m_ssd_d, m_ssd_norm_w, m_ml_conv_w, m_ml_conv_b, m_ml_w_q, m_ml_w_k, m_ml_w_v, m_ml_w_if, m_ml_b_if, m_ml_norm_w, m_ml_skip, m_hy_w_out, m_s5_a_re, m_s5_a_im, m_s5_log_step, m_s5_b_re, m_s5_b_im, m_s5_c_re, m_s5_c_im, m_s5_d, m_s5_w_a, m_s5_b_a, m_s5_w_b, m_s5_b_b, m_final_norm, v_ffn1_norm, v_ffn1_w_gate, v_ffn1_w_up, v_ffn1_w_down, v_mix_norm, v_ffn2_norm, v_ffn2_w_gate, v_ffn2_w_up, v_ffn2_w_down, v_hy_w_in, v_ssd_conv_w, v_ssd_conv_b, v_ssd_dt_bias, v_ssd_a_log, v_ssd_d, v_ssd_norm_w, v_ml_conv_w, v_ml_conv_b, v_ml_w_q, v_ml_w_k, v_ml_w_v, v_ml_w_if, v_ml_b_if, v_ml_norm_w, v_ml_skip, v_hy_w_out, v_s5_a_re, v_s5_a_im, v_s5_log_step, v_s5_b_re, v_s5_b_im, v_s5_c_re, v_s5_c_im, v_s5_d, v_s5_w_a, v_s5_b_a, v_s5_w_b, v_s5_b_b, v_final_norm):
    given = dict(locals())
    w = {n: given[n] for n in WEIGHTS}
    mom = {n: given["m_" + n] for n in WEIGHTS}
    var = {n: given["v_" + n] for n in WEIGHTS}
    bl, seq, d = x.shape
    me = 4 * lax.axis_index("x") + 2 * lax.axis_index("y") + lax.axis_index("c")

    x0, tgt = x.reshape(bl * seq, d), loss_target.reshape(bl * seq, d)
    rep = {n: w[n] for n in WEIGHTS if n not in BIG and n not in SMALL_SHARDED}
    ffn_w = ("_w_gate", "_w_up", "_w_down")

    def ffn_gather(pre, l):
        return [(w[pre + s][l:l + 1].astype(bf16), "gather") for s in ffn_w]

    def scatter(parts):
        return [(p, "scatter") for p in parts]

    wf10 = tuple(exchange("gather_ffn1_l0", ffn_gather("ffn1", 0)))
    mixer_ops = [(w[n].astype(bf16), "gather") for n in ("hy_w_in", "hy_w_out")]
    mixer_ops.append((_pack([w[n] for n in SMALL_SHARDED]), "gather"))
    x1, *rest = ffn_fwd(x0, ffn1_norm[0:1], *wf10, ride=mixer_ops)
    sv10, got = rest[:3], rest[3:]
    gw = dict(zip(("hy_w_in", "hy_w_out"), got[:2]))
    gw.update(zip(SMALL_SHARDED, _unpack(got[2], [w[n].shape for n in SMALL_SHARDED])))
    w0 = assemble_hybrid(gw, rep)
    x2, sv_h, wf20, got, wf11 = hybrid_fwd(x1, w0, seq, ride_in=ffn_gather("ffn2", 0),
                                           ride_ssd=[(w[n].astype(bf16), "gather") for n in ("s5_w_a", "s5_w_b")],
                                           ride_ml=ffn_gather("ffn1", 1))
    gw.update(zip(("s5_w_a", "s5_w_b"), got))
    w1 = assemble_s5(gw, rep)
    x3, *rest = ffn_fwd(x2, ffn2_norm[0:1], *wf20, ride=ffn_gather("ffn2", 1))
    sv20, wf21 = rest[:3], tuple(rest[3:])
    x4, *sv11 = ffn_fwd(x3, ffn1_norm[1:2], *wf11)
    x5, sv_s = s5_layer_fwd(x4, w1, seq)
    x6, *sv21 = ffn_fwd(x5, ffn2_norm[1:2], *wf21)
    loss, dx6, d_final = loss_head(x6, final_norm.reshape(1, d), tgt)

    dx5, dn21, dw21, _, _ = ffn_step_bwd(dx6, x5, ffn2_norm[1:2], wf21, sv21)
    dx4, g_s5 = s5_layer_bwd(dx5, x4, w1, sv_s, seq)
    dwab = g_s5.pop("wab")
    s5_ops = scatter([_shards(dwab[None, :, :D_MODEL], 1).astype(bf16), _shards(dwab[None, :, D_MODEL:], 1).astype(bf16)])
    dx3, dn11, dw11, got_a, got_w = ffn_step_bwd(dx4, x3, ffn1_norm[1:2], wf11, sv11, ride_act=scatter(dw21[:2]),
                                                 ride_w=scatter(dw21[2:]) + s5_ops)
    p21, p_s5 = got_a + got_w[:1], got_w[1:]
    small = s5_small_grads(g_s5, rep)
    dx2, dn20, dw20, got_a, got_w = ffn_step_bwd(dx3, x2, ffn2_norm[0:1], wf20, sv20, ride_act=scatter(dw11[:2]),
                                                 ride_w=scatter(dw11[2:]))
    p11 = got_a + got_w
    dx1, g_hy, p20, (parts_s5,) = hybrid_bwd(dx2, x1, w0, sv_h, seq, ride_ml=scatter(dw20),
                                             ride_du=[(_pack([small[n] for n in SMALL_S5]), "gather")])
    hy_ops = scatter([_shards(win_from_padded(g_hy.pop("win"))[None], 2).astype(bf16), _shards(g_hy.pop("wo")[None], 1).astype(bf16)])
    h10, g10, u10 = sv10
    dx0, dn10, dg, du, a, dyh, *p_hy = ffn_bwd_act(dx1, x0, ffn1_norm[0:1], g10, u10, *wf10, ride=hy_ops)
    g_norms = {"ffn1_norm": jnp.concatenate([dn10, dn11], axis=0), "ffn2_norm": jnp.concatenate([dn20, dn21], axis=0)}
    small.update(small_grads(g_norms, g_hy, g_s5, d_final, rep))
    *dw10, parts_rest = ffn_bwd_w(h10, dyh, dg, du, a, ride=[(_pack([small[n] for n in SMALL_REST]), "gather")])
    p10 = exchange("reduce_tail", scatter(dw10))
    small_parts = jnp.concatenate([parts_rest, parts_s5], axis=1)
    small_sum = sum_parts("sum_small", small_parts, tr=_tile_rows(small_parts.shape[1], ROW))

    out_g, out_d, out_m, out_v = {}, {}, {}, {}
    ffn_parts = {"ffn1": (p10, p11), "ffn2": (p20, p21)}
    for pre in ("ffn1", "ffn2"):
        for k, s in enumerate(ffn_w):
            n = pre + s
            r, c = w[n].shape[1:]
            res = None
            for l in (1, 0):
                res = adamw_layer("adamw_" + n, ffn_parts[pre][l][k].reshape(N_DEV, r, c), w[n], mom[n], var[n], l, res,
                                  tr=_tile_rows(r, c))
            out_g[n], out_d[n], out_m[n], out_v[n] = res
    for n, parts in zip(("hy_w_in", "hy_w_out", "s5_w_a", "s5_w_b"), tuple(p_hy) + tuple(p_s5)):
        shp = w[n].shape
        w2 = _flat2d(w[n])
        res = adamw("adamw_" + n, parts.reshape((N_DEV,) + w2.shape), w2, _flat2d(mom[n]), _flat2d(var[n]),
                    tr=_tile_rows(*w2.shape))
        out_g[n], out_d[n], out_m[n], out_v[n] = [a.reshape(shp) for a in res]
    g_small = {}
    for n, full in zip(SMALL, _unpack(small_sum, [small[n].shape for n in SMALL])):
        if n in SMALL_SHARDED:
            ax = SMALL_SHARDED[n]
            full = lax.dynamic_slice_in_dim(full, me * w[n].shape[ax], w[n].shape[ax], axis=ax)
        g_small[n] = full
    packs = [_pack([t[n] for n in SMALL]) for t in (g_small, w, mom, var)]
    res = adamw("adamw_small", packs[0][None], packs[1], packs[2], packs[3], tr=_tile_rows(*packs[0].shape))
    for dst, a in zip((out_g, out_d, out_m, out_v), res):
        dst.update(zip(SMALL, _unpack(a, [w[n].shape for n in SMALL])))

    total = lax.psum(loss[0, 0], ("x", "y", "c"))
    return (total, dx0.reshape(bl, seq, d), *[out_g[n] for n in WEIGHTS], *[out_d[n] for n in WEIGHTS],
            *[out_m[n] for n in WEIGHTS], *[out_v[n] for n in WEIGHTS])
```
